```python
import math
import jax, jax.numpy as jnp
from jax import lax
import numpy as np

D_MODEL = 2048
BATCH = 8
SEQ = 8192
DEPTH = 1

D_RNN = D_MODEL
RG_HEADS = 16
RG_HEAD_DIM = D_RNN // RG_HEADS
CONV_WIDTH = 4
RG_C = 8.0
D_SSM = D_MODEL // 2
SSM_GROUP = 16
SSM_GROUPS = D_SSM // SSM_GROUP
SSM_STATE = 64
D_FF = 4 * D_MODEL
D_IN = 2 * D_RNN + D_SSM + 2 * D_MODEL
LN_EPS = 1e-5

kernel_name = "hybrid_rglru_s5_gated_deepnorm_block"


def _layernorm(x, g, b):
    xf = x.astype(jnp.float32)
    mu = jnp.mean(xf, axis=-1, keepdims=True)
    var = jnp.mean(jnp.square(xf - mu), axis=-1, keepdims=True)
    y = (xf - mu) * lax.rsqrt(var + LN_EPS)
    return (y * g.astype(jnp.float32) + b.astype(jnp.float32)).astype(x.dtype)


def _real_linear_scan(a, b):
    def combine(c1, c2):
        a1, b1 = c1
        a2, b2 = c2
        return a1 * a2, a2 * b1 + b2
    _, h = lax.associative_scan(combine, (a, b), axis=1)
    return h


def _complex_linear_scan(a_re, a_im, b_re, b_im):
    def combine(c1, c2):
        a1r, a1i, b1r, b1i = c1
        a2r, a2i, b2r, b2i = c2
        ar = a2r * a1r - a2i * a1i
        ai = a2r * a1i + a2i * a1r
        br = a2r * b1r - a2i * b1i + b2r
        bi = a2r * b1i + a2i * b1r + b2i
        return ar, ai, br, bi
    _, _, h_re, h_im = lax.associative_scan(combine, (a_re, a_im, b_re, b_im), axis=1)
    return h_re, h_im


def _causal_depthwise_conv(x, w, bias):
    c = x.shape[-1]
    y = lax.conv_general_dilated(
        x, w[:, None, :].astype(x.dtype), window_strides=(1,),
        padding=[(CONV_WIDTH - 1, 0)], dimension_numbers=("NWC", "WIO", "NWC"),
        feature_group_count=c)
    return y + bias


def _rglru_branch(xr, gate, conv_w, conv_b, wa, ba, wx, bx, lam, w_a_out):
    bsz, s, _ = xr.shape
    xc = _causal_depthwise_conv(xr, conv_w, conv_b)
    xh = xc.reshape(bsz, s, RG_HEADS, RG_HEAD_DIM)
    r = jax.nn.sigmoid(jnp.einsum("bshi,hij->bshj", xh, wa) + ba).reshape(bsz, s, D_RNN)
    i = jax.nn.sigmoid(jnp.einsum("bshi,hij->bshj", xh, wx) + bx).reshape(bsz, s, D_RNN)
    log_a = (-RG_C * r.astype(jnp.float32)) * jax.nn.softplus(-lam.astype(jnp.float32))
    a = jnp.exp(log_a)
    mult = jnp.sqrt(-jnp.expm1(2.0 * log_a))
    b = mult * (i.astype(jnp.float32) * xc.astype(jnp.float32))
    h = _real_linear_scan(a, b).astype(xr.dtype)
    return (h * jax.nn.gelu(gate)) @ w_a_out


def _s5_branch(u, a_re, a_im, log_dt, b_re, b_im, c_re, c_im, d, glu_w, glu_v):
    bsz, s, _ = u.shape
    uf = u.astype(jnp.float32).reshape(bsz, s, SSM_GROUPS, SSM_GROUP)
    dt = jnp.exp(log_dt.astype(jnp.float32))[:, None]
    lr = jnp.minimum(a_re.astype(jnp.float32), -1e-4)
    li = a_im.astype(jnp.float32)
    mag = jnp.exp(lr * dt)
    lbr = mag * jnp.cos(li * dt)
    lbi = mag * jnp.sin(li * dt)
    zr, zi = lbr - 1.0, lbi
    den = lr * lr + li * li
    fr = (zr * lr + zi * li) / den
    fi = (zi * lr - zr * li) / den
    br32, bi32 = b_re.astype(jnp.float32), b_im.astype(jnp.float32)
    bbr = fr[..., None] * br32 - fi[..., None] * bi32
    bbi = fr[..., None] * bi32 + fi[..., None] * br32
    bu_re = jnp.einsum("bsgh,gph->bsgp", uf, bbr)
    bu_im = jnp.einsum("bsgh,gph->bsgp", uf, bbi)
    shp = (1, s, SSM_GROUPS, SSM_STATE)
    h_re, h_im = _complex_linear_scan(jnp.broadcast_to(lbr, shp), jnp.broadcast_to(lbi, shp),
                                      bu_re, bu_im)
    y = (jnp.einsum("bsgp,ghp->bsgh", h_re, c_re.astype(jnp.float32))
         - jnp.einsum("bsgp,ghp->bsgh", h_im, c_im.astype(jnp.float32))
         + d.astype(jnp.float32) * uf)
    y = jax.nn.gelu(y.reshape(bsz, s, D_SSM)).astype(u.dtype)
    return (y @ glu_w) * jax.nn.sigmoid(y @ glu_v)


def _fwd_setup_inputs(seed: int = 0) -> dict:
    key = jax.random.key(seed)
    ks = jax.random.split(key, 32)
    L = DEPTH
    beta = (8.0 * DEPTH) ** -0.25

    def nrm(k, shape, scale):
        return jax.random.normal(k, shape, jnp.float32) * scale

    x = nrm(ks[0], (BATCH, SEQ, D_MODEL), 1.0)
    w_in = nrm(ks[1], (L, D_MODEL, D_IN), D_MODEL ** -0.5)
    conv_w = nrm(ks[2], (L, CONV_WIDTH, D_RNN), CONV_WIDTH ** -0.5)
    conv_b = nrm(ks[3], (L, D_RNN), 0.01)
    rg_wa = nrm(ks[4], (L, RG_HEADS, RG_HEAD_DIM, RG_HEAD_DIM), RG_HEAD_DIM ** -0.5)
    rg_ba = nrm(ks[5], (L, RG_HEADS, RG_HEAD_DIM), 0.01)
    rg_wx = nrm(ks[6], (L, RG_HEADS, RG_HEAD_DIM, RG_HEAD_DIM), RG_HEAD_DIM ** -0.5)
    rg_bx = nrm(ks[7], (L, RG_HEADS, RG_HEAD_DIM), 0.01)
    a_c = jax.random.uniform(ks[8], (L, D_RNN), jnp.float32, 0.9, 0.999)
    a0 = a_c ** (1.0 / RG_C)
    rg_lambda = jnp.log(a0) - jnp.log1p(-a0)
    w_a_out = nrm(ks[9], (L, D_RNN, D_MODEL), D_RNN ** -0.5)
    n = jnp.arange(SSM_STATE, dtype=jnp.float32)
    ssm_a_re = -0.5 + nrm(ks[10], (L, SSM_GROUPS, SSM_STATE), 0.01)
    ssm_a_im = math.pi * n + nrm(ks[11], (L, SSM_GROUPS, SSM_STATE), 0.01)
    ssm_log_dt = jax.random.uniform(ks[12], (L, SSM_GROUPS), jnp.float32,
                                    math.log(1e-3), math.log(1e-1))
    ssm_b_re = nrm(ks[13], (L, SSM_GROUPS, SSM_STATE, SSM_GROUP), (2.0 * SSM_GROUP) ** -0.5)
    ssm_b_im = nrm(ks[14], (L, SSM_GROUPS, SSM_STATE, SSM_GROUP), (2.0 * SSM_GROUP) ** -0.5)
    ssm_c_re = nrm(ks[15], (L, SSM_GROUPS, SSM_GROUP, SSM_STATE), (0.5 * SSM_STATE) ** -0.5)
    ssm_c_im = nrm(ks[16], (L, SSM_GROUPS, SSM_GROUP, SSM_STATE), (0.5 * SSM_STATE) ** -0.5)
    ssm_d = nrm(ks[17], (L, SSM_GROUPS, SSM_GROUP), 1.0)
    glu_w = nrm(ks[18], (L, D_SSM, D_MODEL), D_SSM ** -0.5)
    glu_v = nrm(ks[19], (L, D_SSM, D_MODEL), D_SSM ** -0.5)
    w_out = nrm(ks[20], (L, D_MODEL, D_MODEL), beta * D_MODEL ** -0.5)
    ln1_g = 1.0 + nrm(ks[21], (L, D_MODEL), 0.02)
    ln1_b = nrm(ks[22], (L, D_MODEL), 0.02)
    mlp_w_up = nrm(ks[23], (L, D_MODEL, D_FF), beta * D_MODEL ** -0.5)
    mlp_b_up = nrm(ks[24], (L, D_FF), 0.01)
    mlp_w_down = nrm(ks[25], (L, D_FF, D_MODEL), beta * D_FF ** -0.5)
    mlp_b_down = nrm(ks[26], (L, D_MODEL), 0.01)
    ln2_g = 1.0 + nrm(ks[27], (L, D_MODEL), 0.02)
    ln2_b = nrm(ks[28], (L, D_MODEL), 0.02)
    return {"x": x, "w_in": w_in, "conv_w": conv_w, "conv_b": conv_b,
            "rg_wa": rg_wa, "rg_ba": rg_ba, "rg_wx": rg_wx, "rg_bx": rg_bx,
            "rg_lambda": rg_lambda, "w_a_out": w_a_out,
            "ssm_a_re": ssm_a_re, "ssm_a_im": ssm_a_im, "ssm_log_dt": ssm_log_dt,
            "ssm_b_re": ssm_b_re, "ssm_b_im": ssm_b_im, "ssm_c_re": ssm_c_re,
            "ssm_c_im": ssm_c_im, "ssm_d": ssm_d, "glu_w": glu_w, "glu_v": glu_v,
            "w_out": w_out, "ln1_g": ln1_g, "ln1_b": ln1_b,
            "mlp_w_up": mlp_w_up, "mlp_b_up": mlp_b_up, "mlp_w_down": mlp_w_down,
            "mlp_b_down": mlp_b_down, "ln2_g": ln2_g, "ln2_b": ln2_b}


def _fwd_reference(x, w_in, conv_w, conv_b, rg_wa, rg_ba, rg_wx, rg_bx, rg_lambda, w_a_out,
              ssm_a_re, ssm_a_im, ssm_log_dt, ssm_b_re, ssm_b_im, ssm_c_re, ssm_c_im,
              ssm_d, glu_w, glu_v, w_out, ln1_g, ln1_b, mlp_w_up, mlp_b_up,
              mlp_w_down, mlp_b_down, ln2_g, ln2_b):
    alpha = (2.0 * DEPTH) ** 0.25
    splits = [D_RNN, 2 * D_RNN, 2 * D_RNN + D_SSM, 2 * D_RNN + D_SSM + D_MODEL]
    for l in range(DEPTH):
        z = x @ w_in[l]
        xr, gate_r, u_s, g_a, g_b = jnp.split(z, splits, axis=-1)
        y_a = _rglru_branch(xr, gate_r, conv_w[l], conv_b[l], rg_wa[l], rg_ba[l],
                            rg_wx[l], rg_bx[l], rg_lambda[l], w_a_out[l])
        y_b = _s5_branch(u_s, ssm_a_re[l], ssm_a_im[l], ssm_log_dt[l], ssm_b_re[l],
                         ssm_b_im[l], ssm_c_re[l], ssm_c_im[l], ssm_d[l], glu_w[l], glu_v[l])
        mix = jax.nn.sigmoid(g_a) * y_a + jax.nn.sigmoid(g_b) * y_b
        x = _layernorm(alpha * x + mix @ w_out[l], ln1_g[l], ln1_b[l])
        h = jnp.square(jax.nn.relu(x @ mlp_w_up[l] + mlp_b_up[l])) @ mlp_w_down[l] + mlp_b_down[l]
        x = _layernorm(alpha * x + h, ln2_g[l], ln2_b[l])
    return x


import jax as _jax
import jax.numpy as _jnp

TWIN_FORMAT = 'train_step'
FWD_PARAMS = ['x', 'w_in', 'conv_w', 'conv_b', 'rg_wa', 'rg_ba', 'rg_wx', 'rg_bx', 'rg_lambda', 'w_a_out', 'ssm_a_re', 'ssm_a_im', 'ssm_log_dt', 'ssm_b_re', 'ssm_b_im', 'ssm_c_re', 'ssm_c_im', 'ssm_d', 'glu_w', 'glu_v', 'w_out', 'ln1_g', 'ln1_b', 'mlp_w_up', 'mlp_b_up', 'mlp_w_down', 'mlp_b_down', 'ln2_g', 'ln2_b']
TWIN_WEIGHTS = ['w_in', 'conv_w', 'conv_b', 'rg_wa', 'rg_ba', 'rg_wx', 'rg_bx', 'rg_lambda', 'w_a_out', 'ssm_a_re', 'ssm_a_im', 'ssm_log_dt', 'ssm_b_re', 'ssm_b_im', 'ssm_c_re', 'ssm_c_im', 'ssm_d', 'glu_w', 'glu_v', 'w_out', 'ln1_g', 'ln1_b', 'mlp_w_up', 'mlp_b_up', 'mlp_w_down', 'mlp_b_down', 'ln2_g', 'ln2_b']
TWIN_DIFF_INPUT = 'x'
TWIN_INPUTS = ['x', 'w_in', 'conv_w', 'conv_b', 'rg_wa', 'rg_ba', 'rg_wx', 'rg_bx', 'rg_lambda', 'w_a_out', 'ssm_a_re', 'ssm_a_im', 'ssm_log_dt', 'ssm_b_re', 'ssm_b_im', 'ssm_c_re', 'ssm_c_im', 'ssm_d', 'glu_w', 'glu_v', 'w_out', 'ln1_g', 'ln1_b', 'mlp_w_up', 'mlp_b_up', 'mlp_w_down', 'mlp_b_down', 'ln2_g', 'ln2_b', 'loss_target', 'm_w_in', 'm_conv_w', 'm_conv_b', 'm_rg_wa', 'm_rg_ba', 'm_rg_wx', 'm_rg_bx', 'm_rg_lambda', 'm_w_a_out', 'm_ssm_a_re', 'm_ssm_a_im', 'm_ssm_log_dt', 'm_ssm_b_re', 'm_ssm_b_im', 'm_ssm_c_re', 'm_ssm_c_im', 'm_ssm_d', 'm_glu_w', 'm_glu_v', 'm_w_out', 'm_ln1_g', 'm_ln1_b', 'm_mlp_w_up', 'm_mlp_b_up', 'm_mlp_w_down', 'm_mlp_b_down', 'm_ln2_g', 'm_ln2_b', 'v_w_in', 'v_conv_w', 'v_conv_b', 'v_rg_wa', 'v_rg_ba', 'v_rg_wx', 'v_rg_bx', 'v_rg_lambda', 'v_w_a_out', 'v_ssm_a_re', 'v_ssm_a_im', 'v_ssm_log_dt', 'v_ssm_b_re', 'v_ssm_b_im', 'v_ssm_c_re', 'v_ssm_c_im', 'v_ssm_d', 'v_glu_w', 'v_glu_v', 'v_w_out', 'v_ln1_g', 'v_ln1_b', 'v_mlp_w_up', 'v_mlp_b_up', 'v_mlp_w_down', 'v_mlp_b_down', 'v_ln2_g', 'v_ln2_b']
TWIN_OUTPUTS = ['loss', 'grad_x', 'grad_w_in', 'grad_conv_w', 'grad_conv_b', 'grad_rg_wa', 'grad_rg_ba', 'grad_rg_wx', 'grad_rg_bx', 'grad_rg_lambda', 'grad_w_a_out', 'grad_ssm_a_re', 'grad_ssm_a_im', 'grad_ssm_log_dt', 'grad_ssm_b_re', 'grad_ssm_b_im', 'grad_ssm_c_re', 'grad_ssm_c_im', 'grad_ssm_d', 'grad_glu_w', 'grad_glu_v', 'grad_w_out', 'grad_ln1_g', 'grad_ln1_b', 'grad_mlp_w_up', 'grad_mlp_b_up', 'grad_mlp_w_down', 'grad_mlp_b_down', 'grad_ln2_g', 'grad_ln2_b', 'delta_w_in', 'delta_conv_w', 'delta_conv_b', 'delta_rg_wa', 'delta_rg_ba', 'delta_rg_wx', 'delta_rg_bx', 'delta_rg_lambda', 'delta_w_a_out', 'delta_ssm_a_re', 'delta_ssm_a_im', 'delta_ssm_log_dt', 'delta_ssm_b_re', 'delta_ssm_b_im', 'delta_ssm_c_re', 'delta_ssm_c_im', 'delta_ssm_d', 'delta_glu_w', 'delta_glu_v', 'delta_w_out', 'delta_ln1_g', 'delta_ln1_b', 'delta_mlp_w_up', 'delta_mlp_b_up', 'delta_mlp_w_down', 'delta_mlp_b_down', 'delta_ln2_g', 'delta_ln2_b', 'new_m_w_in', 'new_m_conv_w', 'new_m_conv_b', 'new_m_rg_wa', 'new_m_rg_ba', 'new_m_rg_wx', 'new_m_rg_bx', 'new_m_rg_lambda', 'new_m_w_a_out', 'new_m_ssm_a_re', 'new_m_ssm_a_im', 'new_m_ssm_log_dt', 'new_m_ssm_b_re', 'new_m_ssm_b_im', 'new_m_ssm_c_re', 'new_m_ssm_c_im', 'new_m_ssm_d', 'new_m_glu_w', 'new_m_glu_v', 'new_m_w_out', 'new_m_ln1_g', 'new_m_ln1_b', 'new_m_mlp_w_up', 'new_m_mlp_b_up', 'new_m_mlp_w_down', 'new_m_mlp_b_down', 'new_m_ln2_g', 'new_m_ln2_b', 'new_v_w_in', 'new_v_conv_w', 'new_v_conv_b', 'new_v_rg_wa', 'new_v_rg_ba', 'new_v_rg_wx', 'new_v_rg_bx', 'new_v_rg_lambda', 'new_v_w_a_out', 'new_v_ssm_a_re', 'new_v_ssm_a_im', 'new_v_ssm_log_dt', 'new_v_ssm_b_re', 'new_v_ssm_b_im', 'new_v_ssm_c_re', 'new_v_ssm_c_im', 'new_v_ssm_d', 'new_v_glu_w', 'new_v_glu_v', 'new_v_w_out', 'new_v_ln1_g', 'new_v_ln1_b', 'new_v_mlp_w_up', 'new_v_mlp_b_up', 'new_v_mlp_w_down', 'new_v_mlp_b_down', 'new_v_ln2_g', 'new_v_ln2_b']
TWIN_LEAF_KINDS = {'loss': 'loss', 'grad_x': 'grad_x', 'grad_w_in': 'grad_w', 'grad_conv_w': 'grad_w', 'grad_conv_b': 'grad_w', 'grad_rg_wa': 'grad_w', 'grad_rg_ba': 'grad_w', 'grad_rg_wx': 'grad_w', 'grad_rg_bx': 'grad_w', 'grad_rg_lambda': 'grad_w', 'grad_w_a_out': 'grad_w', 'grad_ssm_a_re': 'grad_w', 'grad_ssm_a_im': 'grad_w', 'grad_ssm_log_dt': 'grad_w', 'grad_ssm_b_re': 'grad_w', 'grad_ssm_b_im': 'grad_w', 'grad_ssm_c_re': 'grad_w', 'grad_ssm_c_im': 'grad_w', 'grad_ssm_d': 'grad_w', 'grad_glu_w': 'grad_w', 'grad_glu_v': 'grad_w', 'grad_w_out': 'grad_w', 'grad_ln1_g': 'grad_w', 'grad_ln1_b': 'grad_w', 'grad_mlp_w_up': 'grad_w', 'grad_mlp_b_up': 'grad_w', 'grad_mlp_w_down': 'grad_w', 'grad_mlp_b_down': 'grad_w', 'grad_ln2_g': 'grad_w', 'grad_ln2_b': 'grad_w', 'delta_w_in': 'delta_w', 'delta_conv_w': 'delta_w', 'delta_conv_b': 'delta_w', 'delta_rg_wa': 'delta_w', 'delta_rg_ba': 'delta_w', 'delta_rg_wx': 'delta_w', 'delta_rg_bx': 'delta_w', 'delta_rg_lambda': 'delta_w', 'delta_w_a_out': 'delta_w', 'delta_ssm_a_re': 'delta_w', 'delta_ssm_a_im': 'delta_w', 'delta_ssm_log_dt': 'delta_w', 'delta_ssm_b_re': 'delta_w', 'delta_ssm_b_im': 'delta_w', 'delta_ssm_c_re': 'delta_w', 'delta_ssm_c_im': 'delta_w', 'delta_ssm_d': 'delta_w', 'delta_glu_w': 'delta_w', 'delta_glu_v': 'delta_w', 'delta_w_out': 'delta_w', 'delta_ln1_g': 'delta_w', 'delta_ln1_b': 'delta_w', 'delta_mlp_w_up': 'delta_w', 'delta_mlp_b_up': 'delta_w', 'delta_mlp_w_down': 'delta_w', 'delta_mlp_b_down': 'delta_w', 'delta_ln2_g': 'delta_w', 'delta_ln2_b': 'delta_w', 'new_m_w_in': 'new_m', 'new_m_conv_w': 'new_m', 'new_m_conv_b': 'new_m', 'new_m_rg_wa': 'new_m', 'new_m_rg_ba': 'new_m', 'new_m_rg_wx': 'new_m', 'new_m_rg_bx': 'new_m', 'new_m_rg_lambda': 'new_m', 'new_m_w_a_out': 'new_m', 'new_m_ssm_a_re': 'new_m', 'new_m_ssm_a_im': 'new_m', 'new_m_ssm_log_dt': 'new_m', 'new_m_ssm_b_re': 'new_m', 'new_m_ssm_b_im': 'new_m', 'new_m_ssm_c_re': 'new_m', 'new_m_ssm_c_im': 'new_m', 'new_m_ssm_d': 'new_m', 'new_m_glu_w': 'new_m', 'new_m_glu_v': 'new_m', 'new_m_w_out': 'new_m', 'new_m_ln1_g': 'new_m', 'new_m_ln1_b': 'new_m', 'new_m_mlp_w_up': 'new_m', 'new_m_mlp_b_up': 'new_m', 'new_m_mlp_w_down': 'new_m', 'new_m_mlp_b_down': 'new_m', 'new_m_ln2_g': 'new_m', 'new_m_ln2_b': 'new_m', 'new_v_w_in': 'new_v', 'new_v_conv_w': 'new_v', 'new_v_conv_b': 'new_v', 'new_v_rg_wa': 'new_v', 'new_v_rg_ba': 'new_v', 'new_v_rg_wx': 'new_v', 'new_v_rg_bx': 'new_v', 'new_v_rg_lambda': 'new_v', 'new_v_w_a_out': 'new_v', 'new_v_ssm_a_re': 'new_v', 'new_v_ssm_a_im': 'new_v', 'new_v_ssm_log_dt': 'new_v', 'new_v_ssm_b_re': 'new_v', 'new_v_ssm_b_im': 'new_v', 'new_v_ssm_c_re': 'new_v', 'new_v_ssm_c_im': 'new_v', 'new_v_ssm_d': 'new_v', 'new_v_glu_w': 'new_v', 'new_v_glu_v': 'new_v', 'new_v_w_out': 'new_v', 'new_v_ln1_g': 'new_v', 'new_v_ln1_b': 'new_v', 'new_v_mlp_w_up': 'new_v', 'new_v_mlp_b_up': 'new_v', 'new_v_mlp_w_down': 'new_v', 'new_v_mlp_b_down': 'new_v', 'new_v_ln2_g': 'new_v', 'new_v_ln2_b': 'new_v'}


def _forward(args):
    return _fwd_reference(*[args[k] for k in FWD_PARAMS])


def _output_shape():
    def fwd():
        inp = _fwd_setup_inputs(0)
        return _fwd_reference(*[inp[k] for k in FWD_PARAMS])
    out = _jax.eval_shape(fwd)
    return out.shape, out.dtype

N_MICROBATCH = 1
ADAM_LR = 0.001
ADAM_B1 = 0.9
ADAM_B2 = 0.999
ADAM_EPS = 1e-08
ADAM_WD = 0.01
ADAM_STEP = 10
PER_EXAMPLE_BATCH_AXIS = {'x': 0, 'loss_target': 0}
SHARED_INPUTS = []
_WEIGHT_DTYPES = {'w_in': _jnp.float32, 'conv_w': _jnp.float32, 'conv_b': _jnp.float32, 'rg_wa': _jnp.float32, 'rg_ba': _jnp.float32, 'rg_wx': _jnp.float32, 'rg_bx': _jnp.float32, 'rg_lambda': _jnp.float32, 'w_a_out': _jnp.float32, 'ssm_a_re': _jnp.float32, 'ssm_a_im': _jnp.float32, 'ssm_log_dt': _jnp.float32, 'ssm_b_re': _jnp.float32, 'ssm_b_im': _jnp.float32, 'ssm_c_re': _jnp.float32, 'ssm_c_im': _jnp.float32, 'ssm_d': _jnp.float32, 'glu_w': _jnp.float32, 'glu_v': _jnp.float32, 'w_out': _jnp.float32, 'ln1_g': _jnp.float32, 'ln1_b': _jnp.float32, 'mlp_w_up': _jnp.float32, 'mlp_b_up': _jnp.float32, 'mlp_w_down': _jnp.float32, 'mlp_b_down': _jnp.float32, 'ln2_g': _jnp.float32, 'ln2_b': _jnp.float32}
MOMENT_SCALE = {'w_in': 1.198276e-02, 'conv_w': 1.707215e-02, 'conv_b': 3.029996e-01, 'rg_wa': 7.310626e-03, 'rg_ba': 6.128296e-03, 'rg_wx': 1.346164e-02, 'rg_bx': 5.668074e-03, 'rg_lambda': 1.045035e-02, 'w_a_out': 1.731355e-02, 'ssm_a_re': 1.887949e-03, 'ssm_a_im': 1.999643e-03, 'ssm_log_dt': 1.995689e+00, 'ssm_b_re': 1.219934e-03, 'ssm_b_im': 1.219600e-03, 'ssm_c_re': 1.192217e-03, 'ssm_c_im': 1.213627e-03, 'ssm_d': 3.106724e-02, 'glu_w': 1.982440e-02, 'glu_v': 3.933606e-03, 'w_out': 4.310565e-02, 'ln1_g': 1.076054e+00, 'ln1_b': 5.329871e-01, 'mlp_w_up': 2.584003e-02, 'mlp_b_up': 5.646051e-02, 'mlp_w_down': 7.320386e-02, 'mlp_b_down': 4.486489e-01, 'ln2_g': 3.198594e+01, 'ln2_b': 3.125604e+00}


def _to_microbatches(a, axis):
    t = _jnp.moveaxis(a, axis, 0)
    t = t.reshape((N_MICROBATCH, t.shape[0] // N_MICROBATCH) + t.shape[1:])
    return _jnp.moveaxis(t, 1, axis + 1)


def setup_inputs(seed: int = 0) -> dict:
    inp = _fwd_setup_inputs(seed)
    key = _jax.random.fold_in(_jax.random.key(seed), 7919)
    shape, _ = _output_shape()
    out = dict(inp)
    out["loss_target"] = _jax.random.normal(_jax.random.fold_in(key, 0), shape, _jnp.float32)
    for i, name in enumerate(TWIN_WEIGHTS):
        w = inp[name].astype(_jnp.float32)
        if MOMENT_SCALE is None:
            s = _jnp.sqrt(_jnp.mean(_jnp.square(w)) + 1e-30)
        else:
            s = MOMENT_SCALE[name]
        km, kv = _jax.random.split(_jax.random.fold_in(key, i + 1))
        out[name] = w
        out["m_" + name] = s * _jax.random.normal(km, w.shape, _jnp.float32)
        out["v_" + name] = (s * s) * _jax.random.uniform(kv, w.shape, _jnp.float32, 0.5, 1.5)
    if N_MICROBATCH > 1:
        for name, axis in PER_EXAMPLE_BATCH_AXIS.items():
            out[name] = _to_microbatches(out[name], axis)
    return {'x': out['x'], 'w_in': out['w_in'], 'conv_w': out['conv_w'], 'conv_b': out['conv_b'], 'rg_wa': out['rg_wa'], 'rg_ba': out['rg_ba'], 'rg_wx': out['rg_wx'], 'rg_bx': out['rg_bx'], 'rg_lambda': out['rg_lambda'], 'w_a_out': out['w_a_out'], 'ssm_a_re': out['ssm_a_re'], 'ssm_a_im': out['ssm_a_im'], 'ssm_log_dt': out['ssm_log_dt'], 'ssm_b_re': out['ssm_b_re'], 'ssm_b_im': out['ssm_b_im'], 'ssm_c_re': out['ssm_c_re'], 'ssm_c_im': out['ssm_c_im'], 'ssm_d': out['ssm_d'], 'glu_w': out['glu_w'], 'glu_v': out['glu_v'], 'w_out': out['w_out'], 'ln1_g': out['ln1_g'], 'ln1_b': out['ln1_b'], 'mlp_w_up': out['mlp_w_up'], 'mlp_b_up': out['mlp_b_up'], 'mlp_w_down': out['mlp_w_down'], 'mlp_b_down': out['mlp_b_down'], 'ln2_g': out['ln2_g'], 'ln2_b': out['ln2_b'], 'loss_target': out['loss_target'], 'm_w_in': out['m_w_in'], 'm_conv_w': out['m_conv_w'], 'm_conv_b': out['m_conv_b'], 'm_rg_wa': out['m_rg_wa'], 'm_rg_ba': out['m_rg_ba'], 'm_rg_wx': out['m_rg_wx'], 'm_rg_bx': out['m_rg_bx'], 'm_rg_lambda': out['m_rg_lambda'], 'm_w_a_out': out['m_w_a_out'], 'm_ssm_a_re': out['m_ssm_a_re'], 'm_ssm_a_im': out['m_ssm_a_im'], 'm_ssm_log_dt': out['m_ssm_log_dt'], 'm_ssm_b_re': out['m_ssm_b_re'], 'm_ssm_b_im': out['m_ssm_b_im'], 'm_ssm_c_re': out['m_ssm_c_re'], 'm_ssm_c_im': out['m_ssm_c_im'], 'm_ssm_d': out['m_ssm_d'], 'm_glu_w': out['m_glu_w'], 'm_glu_v': out['m_glu_v'], 'm_w_out': out['m_w_out'], 'm_ln1_g': out['m_ln1_g'], 'm_ln1_b': out['m_ln1_b'], 'm_mlp_w_up': out['m_mlp_w_up'], 'm_mlp_b_up': out['m_mlp_b_up'], 'm_mlp_w_down': out['m_mlp_w_down'], 'm_mlp_b_down': out['m_mlp_b_down'], 'm_ln2_g': out['m_ln2_g'], 'm_ln2_b': out['m_ln2_b'], 'v_w_in': out['v_w_in'], 'v_conv_w': out['v_conv_w'], 'v_conv_b': out['v_conv_b'], 'v_rg_wa': out['v_rg_wa'], 'v_rg_ba': out['v_rg_ba'], 'v_rg_wx': out['v_rg_wx'], 'v_rg_bx': out['v_rg_bx'], 'v_rg_lambda': out['v_rg_lambda'], 'v_w_a_out': out['v_w_a_out'], 'v_ssm_a_re': out['v_ssm_a_re'], 'v_ssm_a_im': out['v_ssm_a_im'], 'v_ssm_log_dt': out['v_ssm_log_dt'], 'v_ssm_b_re': out['v_ssm_b_re'], 'v_ssm_b_im': out['v_ssm_b_im'], 'v_ssm_c_re': out['v_ssm_c_re'], 'v_ssm_c_im': out['v_ssm_c_im'], 'v_ssm_d': out['v_ssm_d'], 'v_glu_w': out['v_glu_w'], 'v_glu_v': out['v_glu_v'], 'v_w_out': out['v_w_out'], 'v_ln1_g': out['v_ln1_g'], 'v_ln1_b': out['v_ln1_b'], 'v_mlp_w_up': out['v_mlp_w_up'], 'v_mlp_b_up': out['v_mlp_b_up'], 'v_mlp_w_down': out['v_mlp_w_down'], 'v_mlp_b_down': out['v_mlp_b_down'], 'v_ln2_g': out['v_ln2_g'], 'v_ln2_b': out['v_ln2_b']}


def _loss(weights, diff, rest, loss_target):
    with _jax.named_scope("forward"):
        args = {**rest, TWIN_DIFF_INPUT: diff, **{k: w.astype(_WEIGHT_DTYPES[k]) for k, w in weights.items()}}
        y = _forward(args)
    with _jax.named_scope("loss_head"):
        err = _jnp.square(y.astype(_jnp.float32) - loss_target)
        return 0.5 * _jnp.sum(_jnp.mean(err, axis=-1)) if err.ndim else 0.5 * err


def _adamw(w, g, m, v):
    m = ADAM_B1 * m + (1.0 - ADAM_B1) * g
    v = ADAM_B2 * v + (1.0 - ADAM_B2) * _jnp.square(g)
    m_hat = m / (1.0 - ADAM_B1 ** ADAM_STEP)
    v_hat = v / (1.0 - ADAM_B2 ** ADAM_STEP)
    delta = -ADAM_LR * (m_hat / (_jnp.sqrt(v_hat) + ADAM_EPS) + ADAM_WD * w)
    return delta, m, v


def reference(x, w_in, conv_w, conv_b, rg_wa, rg_ba, rg_wx, rg_bx, rg_lambda, w_a_out, ssm_a_re, ssm_a_im, ssm_log_dt, ssm_b_re, ssm_b_im, ssm_c_re, ssm_c_im, ssm_d, glu_w, glu_v, w_out, ln1_g, ln1_b, mlp_w_up, mlp_b_up, mlp_w_down, mlp_b_down, ln2_g, ln2_b, loss_target, m_w_in, m_conv_w, m_conv_b, m_rg_wa, m_rg_ba, m_rg_wx, m_rg_bx, m_rg_lambda, m_w_a_out, m_ssm_a_re, m_ssm_a_im, m_ssm_log_dt, m_ssm_b_re, m_ssm_b_im, m_ssm_c_re, m_ssm_c_im, m_ssm_d, m_glu_w, m_glu_v, m_w_out, m_ln1_g, m_ln1_b, m_mlp_w_up, m_mlp_b_up, m_mlp_w_down, m_mlp_b_down, m_ln2_g, m_ln2_b, v_w_in, v_conv_w, v_conv_b, v_rg_wa, v_rg_ba, v_rg_wx, v_rg_bx, v_rg_lambda, v_w_a_out, v_ssm_a_re, v_ssm_a_im, v_ssm_log_dt, v_ssm_b_re, v_ssm_b_im, v_ssm_c_re, v_ssm_c_im, v_ssm_d, v_glu_w, v_glu_v, v_w_out, v_ln1_g, v_ln1_b, v_mlp_w_up, v_mlp_b_up, v_mlp_w_down, v_mlp_b_down, v_ln2_g, v_ln2_b):
    given = dict(x=x, w_in=w_in, conv_w=conv_w, conv_b=conv_b, rg_wa=rg_wa, rg_ba=rg_ba, rg_wx=rg_wx, rg_bx=rg_bx, rg_lambda=rg_lambda, w_a_out=w_a_out, ssm_a_re=ssm_a_re, ssm_a_im=ssm_a_im, ssm_log_dt=ssm_log_dt, ssm_b_re=ssm_b_re, ssm_b_im=ssm_b_im, ssm_c_re=ssm_c_re, ssm_c_im=ssm_c_im, ssm_d=ssm_d, glu_w=glu_w, glu_v=glu_v, w_out=w_out, ln1_g=ln1_g, ln1_b=ln1_b, mlp_w_up=mlp_w_up, mlp_b_up=mlp_b_up, mlp_w_down=mlp_w_down, mlp_b_down=mlp_b_down, ln2_g=ln2_g, ln2_b=ln2_b, loss_target=loss_target, m_w_in=m_w_in, m_conv_w=m_conv_w, m_conv_b=m_conv_b, m_rg_wa=m_rg_wa, m_rg_ba=m_rg_ba, m_rg_wx=m_rg_wx, m_rg_bx=m_rg_bx, m_rg_lambda=m_rg_lambda, m_w_a_out=m_w_a_out, m_ssm_a_re=m_ssm_a_re, m_ssm_a_im=m_ssm_a_im, m_ssm_log_dt=m_ssm_log_dt, m_ssm_b_re=m_ssm_b_re, m_ssm_b_im=m_ssm_b_im, m_ssm_c_re=m_ssm_c_re, m_ssm_c_im=m_ssm_c_im, m_ssm_d=m_ssm_d, m_glu_w=m_glu_w, m_glu_v=m_glu_v, m_w_out=m_w_out, m_ln1_g=m_ln1_g, m_ln1_b=m_ln1_b, m_mlp_w_up=m_mlp_w_up, m_mlp_b_up=m_mlp_b_up, m_mlp_w_down=m_mlp_w_down, m_mlp_b_down=m_mlp_b_down, m_ln2_g=m_ln2_g, m_ln2_b=m_ln2_b, v_w_in=v_w_in, v_conv_w=v_conv_w, v_conv_b=v_conv_b, v_rg_wa=v_rg_wa, v_rg_ba=v_rg_ba, v_rg_wx=v_rg_wx, v_rg_bx=v_rg_bx, v_rg_lambda=v_rg_lambda, v_w_a_out=v_w_a_out, v_ssm_a_re=v_ssm_a_re, v_ssm_a_im=v_ssm_a_im, v_ssm_log_dt=v_ssm_log_dt, v_ssm_b_re=v_ssm_b_re, v_ssm_b_im=v_ssm_b_im, v_ssm_c_re=v_ssm_c_re, v_ssm_c_im=v_ssm_c_im, v_ssm_d=v_ssm_d, v_glu_w=v_glu_w, v_glu_v=v_glu_v, v_w_out=v_w_out, v_ln1_g=v_ln1_g, v_ln1_b=v_ln1_b, v_mlp_w_up=v_mlp_w_up, v_mlp_b_up=v_mlp_b_up, v_mlp_w_down=v_mlp_w_down, v_mlp_b_down=v_mlp_b_down, v_ln2_g=v_ln2_g, v_ln2_b=v_ln2_b)
    weights = {n: given[n] for n in TWIN_WEIGHTS}
    shared = {n: given[n] for n in SHARED_INPUTS}
    per_example = {n: given[n] for n in ['x']}
    grad_fn = _jax.value_and_grad(_loss, argnums=(0, 1))

    def one_microbatch(ex, loss_target):
        ex = dict(ex)
        diff = ex.pop(TWIN_DIFF_INPUT)
        return grad_fn(weights, diff, {**shared, **ex}, loss_target)

    if N_MICROBATCH == 1:
        loss, (grad_w, grad_x) = one_microbatch(per_example, given["loss_target"])
    else:
        def body(carry, xs):
            loss_sum, grad_sum = carry
            l_k, (gw_k, gx_k) = one_microbatch(xs[0], xs[1])
            with _jax.named_scope("update"):
                return (loss_sum + l_k, _jax.tree.map(_jnp.add, grad_sum, gw_k)), gx_k

        init = (_jnp.zeros((), _jnp.float32), _jax.tree.map(_jnp.zeros_like, weights))
        (loss, grad_w), grad_x = _jax.lax.scan(body, init, (per_example, given["loss_target"]))
    with _jax.named_scope("update"):
        delta_w, new_m, new_v = {}, {}, {}
        for n in TWIN_WEIGHTS:
            delta_w[n], new_m[n], new_v[n] = _adamw(weights[n], grad_w[n], given["m_" + n], given["v_" + n])
    return (loss, grad_x, *[grad_w[n] for n in TWIN_WEIGHTS], *[delta_w[n] for n in TWIN_WEIGHTS],
            *[new_m[n] for n in TWIN_WEIGHTS], *[new_v[n] for n in TWIN_WEIGHTS])
```

```python
import functools
import math

import jax
import jax.numpy as jnp
from jax import lax
from jax.experimental import pallas as pl
from jax.experimental.pallas import tpu as pltpu

F32 = jnp.float32
BF16 = jnp.bfloat16
MESH = pl.DeviceIdType.MESH
N_DEV = 8
SUBLANES = 8
LANES = 128
VMEM_BYTES_V7X = 64 * 2 ** 20
VMEM_CAP = VMEM_BYTES_V7X - 8 * 2 ** 20

ALPHA = 2.0 ** 0.25
LN_EPS = 1e-5
RG_C = 8.0
ADAM_LR, ADAM_B1, ADAM_B2, ADAM_EPS, ADAM_WD, ADAM_STEP = 0.001, 0.9, 0.999, 1e-08, 0.01, 10
GELU_C = math.sqrt(2.0 / math.pi)
GELU_K = 0.044715

ANY = pl.BlockSpec(memory_space=pl.ANY)


def _params(sem, vmem_bytes):
    limit = int(min(max(2 * vmem_bytes, 16 * 2 ** 20), VMEM_CAP))
    return pltpu.CompilerParams(dimension_semantics=sem, vmem_limit_bytes=limit)


def _sig(x):
    return 1.0 / (1.0 + jnp.exp(-x))


def _gelu(x):
    return 0.5 * x * (1.0 + jnp.tanh(GELU_C * (x + GELU_K * x * x * x)))


def _dgelu(x):
    th = jnp.tanh(GELU_C * (x + GELU_K * x * x * x))
    return 0.5 * (1.0 + th) + 0.5 * x * (1.0 - th * th) * (GELU_C * (1.0 + 3.0 * GELU_K * x * x))


def _expm1(x):
    p = x * (1.0 + x * (1 / 2 + x * (1 / 6 + x * (1 / 24 + x * (1 / 120 + x * (1 / 720 + x * (1 / 5040)))))))
    return jnp.where(jnp.abs(x) < 0.25, p, jnp.exp(x) - 1.0)


def _accumulate(ref, val, first):
    @pl.when(first)
    def _():
        ref[...] = val

    @pl.when(jnp.logical_not(first))
    def _():
        ref[...] += val


def _rows8(cw):
    return lax.broadcasted_iota(jnp.int32, (SUBLANES, cw), 0)


def _shift_down(cur, prev, s, rows):
    return jnp.where(rows < s, pltpu.roll(prev, s, 0), pltpu.roll(cur, s, 0))


def _shift_up(cur, nxt, s, rows):
    return jnp.where(rows < SUBLANES - s, pltpu.roll(cur, SUBLANES - s, 0), pltpu.roll(nxt, SUBLANES - s, 0))


def _mm(a, b, *, M, N, K, ta=False, tb=False, b_split=1, n_split=1, a_fn=None, extras=(), epilogue=None,
        n_out=1, n_cs=0, tm=512, tn=512, tk=512, name):
    tm, tn, tk = min(tm, M), min(tn, N), min(tk, K)
    assert M % tm == 0 and N % tn == 0 and K % tk == 0, (name, M, N, K, tm, tn, tk)
    nk = K // tk
    grid = (N // tn, M // tm, nk)
    a_spec = pl.BlockSpec((tk, tm), lambda j, i, k: (k, i)) if ta else pl.BlockSpec((tm, tk), lambda j, i, k: (i, k))
    if b_split == 1:
        b_spec = pl.BlockSpec((tn, tk), lambda j, i, k: (j, k)) if tb else pl.BlockSpec((tk, tn), lambda j, i, k: (k, j))
    elif tb:
        kb = (K // b_split) // tk
        assert kb * tk * b_split == K, name
        b_spec = pl.BlockSpec((None, tn, tk), lambda j, i, k: (k // kb, j, k % kb))
    else:
        nb = (N // b_split) // tn
        assert nb * tn * b_split == N, name
        b_spec = pl.BlockSpec((None, tk, tn), lambda j, i, k: (j // nb, k, j % nb))
    in_specs = [a_spec, b_spec]
    for arr, kind in extras:
        in_specs.append(pl.BlockSpec((tm, tn), lambda j, i, k: (i, j)) if kind == "mn"
                        else pl.BlockSpec((1, tn), lambda j, i, k: (0, j)))
    if n_split == 1:
        out_shape = [jax.ShapeDtypeStruct((M, N), F32) for _ in range(n_out)]
        out_specs = [pl.BlockSpec((tm, tn), lambda j, i, k: (i, j)) for _ in range(n_out)]
    else:
        assert n_out == 1
        nbo = (N // n_split) // tn
        assert nbo * tn * n_split == N, name
        out_shape = [jax.ShapeDtypeStruct((n_split, M, N // n_split), F32)]
        out_specs = [pl.BlockSpec((None, tm, tn), lambda j, i, k: (j // nbo, i, j % nbo))]
    out_shape += [jax.ShapeDtypeStruct((1, N), F32) for _ in range(n_cs)]
    out_specs += [pl.BlockSpec((1, tn), lambda j, i, k: (0, j)) for _ in range(n_cs)]
    ne = len(extras)
    dims = (((0 if ta else 1,), (1 if tb else 0,)), ((), ()))

    def body(*refs):
        a_ref, b_ref = refs[0], refs[1]
        ex_refs = refs[2:2 + ne]
        out_refs = refs[2 + ne:2 + ne + n_out]
        cs_refs = refs[2 + ne + n_out:2 + ne + n_out + n_cs]
        acc_ref = refs[-1]
        i, k = pl.program_id(1), pl.program_id(2)

        @pl.when(k == 0)
        def _():
            acc_ref[...] = jnp.zeros_like(acc_ref)

        av = a_ref[...]
        if a_fn is not None:
            av = a_fn(av.astype(F32))
        acc_ref[...] += lax.dot_general(av.astype(BF16), b_ref[...].astype(BF16), dims, preferred_element_type=F32)

        @pl.when(k == nk - 1)
        def _():
            acc = acc_ref[...]
            res = (acc,) if epilogue is None else epilogue(acc, *[r[...] for r in ex_refs])
            for r, o in zip(out_refs, res[:n_out]):
                r[...] = o.astype(r.dtype)
            for r, cval in zip(cs_refs, res[n_out:]):
                _accumulate(r, jnp.sum(cval, axis=0, keepdims=True), i == 0)

    vmem = 2 * tm * tk * a.dtype.itemsize + 2 * tk * tn * b.dtype.itemsize + (1 + 2 * n_out + 2 * ne + 2) * tm * tn * 4
    outs = pl.pallas_call(
        body, name=name, grid=grid, in_specs=in_specs, out_specs=out_specs, out_shape=out_shape,
        scratch_shapes=[pltpu.VMEM((tm, tn), F32)],
        compiler_params=_params(("parallel", "arbitrary", "arbitrary"), vmem),
    )(a, b, *[arr for arr, _ in extras])
    return outs[0] if len(outs) == 1 else outs


def _bd(pairs, *, T, J, kb, nb, tw=False, extras=(), epilogue=None, n_out=1, n_cs=0, tm=512, name):
    assert T % tm == 0
    grid = (J, T // tm)
    npair, ne = len(pairs), len(extras)
    in_specs, args = [], []
    for arr, off, w in pairs:
        in_specs.append(pl.BlockSpec((tm, kb), lambda j, i, off=off: (i, off + j)))
        in_specs.append(pl.BlockSpec((None,) + tuple(w.shape[1:]), lambda j, i: (j, 0, 0)))
        args += [arr, w]
    for arr, kind, off in extras:
        in_specs.append(pl.BlockSpec((tm, nb), lambda j, i, off=off: (i, off + j)) if kind == "tile"
                        else pl.BlockSpec((1, nb), lambda j, i, off=off: (0, off + j)))
        args.append(arr)
    out_shape = [jax.ShapeDtypeStruct((T, J * nb), F32) for _ in range(n_out)]
    out_specs = [pl.BlockSpec((tm, nb), lambda j, i: (i, j)) for _ in range(n_out)]
    out_shape += [jax.ShapeDtypeStruct((1, J * nb), F32) for _ in range(n_cs)]
    out_specs += [pl.BlockSpec((1, nb), lambda j, i: (0, j)) for _ in range(n_cs)]
    dims = (((1,), (1 if tw else 0,)), ((), ()))

    def body(*refs):
        ex_refs = refs[2 * npair:2 * npair + ne]
        out_refs = refs[2 * npair + ne:2 * npair + ne + n_out]
        cs_refs = refs[2 * npair + ne + n_out:]
        i = pl.program_id(1)
        acc = None
        for p in range(npair):
            d = lax.dot_general(refs[2 * p][...].astype(BF16), refs[2 * p + 1][...].astype(BF16), dims,
                                preferred_element_type=F32)
            acc = d if acc is None else acc + d
        res = (acc,) if epilogue is None else epilogue(acc, *[r[...] for r in ex_refs])
        for r, o in zip(out_refs, res[:n_out]):
            r[...] = o
        for r, cval in zip(cs_refs, res[n_out:]):
            _accumulate(r, jnp.sum(cval, axis=0, keepdims=True), i == 0)

    vmem = (2 * npair * tm * kb + 2 * npair * kb * nb + (2 * n_out + 2 * ne + 3) * tm * nb) * 4
    outs = pl.pallas_call(
        body, name=name, grid=grid, in_specs=in_specs, out_specs=out_specs, out_shape=out_shape,
        compiler_params=_params(("parallel", "arbitrary"), vmem),
    )(*args)
    return outs[0] if len(outs) == 1 else outs


def _bdw(a, a_off, b, b_off, *, T, J, kb, nb, tm=512, name):
    assert T % tm == 0

    def body(a_ref, b_ref, o_ref):
        i = pl.program_id(1)
        d = lax.dot_general(a_ref[...].astype(BF16), b_ref[...].astype(BF16), (((0,), (0,)), ((), ())),
                            preferred_element_type=F32)
        _accumulate(o_ref, d, i == 0)

    return pl.pallas_call(
        body, name=name, grid=(J, T // tm),
        in_specs=[pl.BlockSpec((tm, kb), lambda j, i: (i, a_off + j)), pl.BlockSpec((tm, nb), lambda j, i: (i, b_off + j))],
        out_specs=pl.BlockSpec((None, kb, nb), lambda j, i: (j, 0, 0)),
        out_shape=jax.ShapeDtypeStruct((J, kb, nb), F32),
        compiler_params=_params(("parallel", "arbitrary"), (2 * tm * (kb + nb) + 3 * kb * nb) * 4),
    )(a, b)


def _bd_pack(w, q):
    g, a, b = w.shape
    eye = jnp.eye(q, dtype=w.dtype)
    return jnp.einsum("jqab,qr->jqarb", w.reshape(g // q, q, a, b), eye).reshape(g // q, q * a, q * b)


def _bd_unpack(wp, q):
    j, qa, qb = wp.shape
    a, b = qa // q, qb // q
    w5 = wp.reshape(j, q, a, q, b)
    return jnp.stack([w5[:, r, :, r, :] for r in range(q)], axis=1).reshape(j * q, a, b)


def _ew(fn, ins, *, T, C, n_out, n_cs=0, tm=256, cw=None, name):
    cw = C if cw is None else cw
    assert T % tm == 0 and C % cw == 0
    grid = (C // cw, T // tm)
    in_specs = []
    for arr, kind, off in ins:
        in_specs.append(pl.BlockSpec((tm, cw), lambda j, i, off=off: (i, off + j)) if kind == "tile"
                        else pl.BlockSpec((arr.shape[0], cw), lambda j, i, off=off: (0, off + j)))
    out_shape = [jax.ShapeDtypeStruct((T, C), F32) for _ in range(n_out)]
    out_specs = [pl.BlockSpec((tm, cw), lambda j, i: (i, j)) for _ in range(n_out)]
    out_shape += [jax.ShapeDtypeStruct((1, C), F32) for _ in range(n_cs)]
    out_specs += [pl.BlockSpec((1, cw), lambda j, i: (0, j)) for _ in range(n_cs)]
    nin = len(ins)

    def body(*refs):
        i = pl.program_id(1)
        res = fn(*[r[...] for r in refs[:nin]])
        for r, o in zip(refs[nin:nin + n_out], res[:n_out]):
            r[...] = o
        for r, cval in zip(refs[nin + n_out:], res[n_out:]):
            _accumulate(r, jnp.sum(cval, axis=0, keepdims=True), i == 0)

    vmem = (2 * nin + 2 * n_out + 6) * tm * cw * 4
    outs = pl.pallas_call(
        body, name=name, grid=grid, in_specs=in_specs, out_specs=out_specs, out_shape=out_shape,
        compiler_params=_params(("parallel", "arbitrary"), vmem),
    )(*[arr for arr, _, _ in ins])
    return outs[0] if len(outs) == 1 else outs


def _ln_stats(s):
    mu = jnp.mean(s, axis=-1, keepdims=True)
    d = s - mu
    var = jnp.mean(d * d, axis=-1, keepdims=True)
    rstd = lax.rsqrt(var + LN_EPS)
    return d * rstd, rstd


def _ln_bwd(dy, g, xhat, rstd):
    dxh = dy * g
    m1 = jnp.mean(dxh, axis=-1, keepdims=True)
    m2 = jnp.mean(dxh * xhat, axis=-1, keepdims=True)
    return rstd * (dxh - m1 - xhat * m2)


def _conv_fwd(z, conv_w, conv_b, *, T, C, tm=512, cw=1024, name):
    ng, hb = tm // SUBLANES, tm // SUBLANES

    def body(x_ref, halo_ref, w_ref, b_ref, o_ref):
        it = pl.program_id(1)
        rows = _rows8(cw)
        halo = jnp.where(it == 0, 0.0, halo_ref[...])
        w = w_ref[...]
        bias = b_ref[...]

        def group(g, carry):
            off = pl.multiple_of(g * SUBLANES, SUBLANES)
            cur = x_ref[pl.ds(off, SUBLANES), :]
            prev = x_ref[pl.ds(pl.multiple_of(jnp.maximum(off - SUBLANES, 0), SUBLANES), SUBLANES), :]
            prev = jnp.where(g == 0, halo, prev)
            acc = cur * w[3:4] + bias
            for s in (1, 2, 3):
                acc = acc + _shift_down(cur, prev, s, rows) * w[3 - s:4 - s]
            o_ref[pl.ds(off, SUBLANES), :] = acc
            return carry

        lax.fori_loop(0, ng, group, 0)

    return pl.pallas_call(
        body, name=name, grid=(C // cw, T // tm),
        in_specs=[pl.BlockSpec((tm, cw), lambda j, i: (i, j)),
                  pl.BlockSpec((SUBLANES, cw), lambda j, i: (jnp.maximum(i * hb - 1, 0), j)),
                  pl.BlockSpec((4, cw), lambda j, i: (0, j)), pl.BlockSpec((1, cw), lambda j, i: (0, j))],
        out_specs=pl.BlockSpec((tm, cw), lambda j, i: (i, j)),
        out_shape=jax.ShapeDtypeStruct((T, C), F32),
        compiler_params=_params(("parallel", "arbitrary"), 5 * tm * cw * 4),
    )(z, z, conv_w, conv_b)


def _conv_bwd(dxc, z, conv_w, *, T, C, tm=512, cw=512, name):
    ng, hb, last = tm // SUBLANES, tm // SUBLANES, T // SUBLANES - 1
    nt = T // tm

    def body(d_ref, dn_ref, x_ref, xp_ref, w_ref, o_ref, sums_ref):
        it = pl.program_id(1)
        rows = _rows8(cw)
        dnext = jnp.where(it == nt - 1, 0.0, dn_ref[...])
        xprev = jnp.where(it == 0, 0.0, xp_ref[...])
        w = w_ref[...]

        def group(g, accs):
            off = pl.multiple_of(g * SUBLANES, SUBLANES)
            dcur = d_ref[pl.ds(off, SUBLANES), :]
            dnx = d_ref[pl.ds(pl.multiple_of(jnp.minimum(off + SUBLANES, tm - SUBLANES), SUBLANES), SUBLANES), :]
            dnx = jnp.where(g == ng - 1, dnext, dnx)
            xcur = x_ref[pl.ds(off, SUBLANES), :]
            xpv = x_ref[pl.ds(pl.multiple_of(jnp.maximum(off - SUBLANES, 0), SUBLANES), SUBLANES), :]
            xpv = jnp.where(g == 0, xprev, xpv)
            acc = dcur * w[3:4]
            for s in (1, 2, 3):
                acc = acc + _shift_up(dcur, dnx, s, rows) * w[3 - s:4 - s]
            o_ref[pl.ds(off, SUBLANES), :] = acc
            a0, a1, a2, a3, ab = accs
            a0 = a0 + dcur * _shift_down(xcur, xpv, 3, rows)
            a1 = a1 + dcur * _shift_down(xcur, xpv, 2, rows)
            a2 = a2 + dcur * _shift_down(xcur, xpv, 1, rows)
            a3 = a3 + dcur * xcur
            return a0, a1, a2, a3, ab + dcur

        zero = jnp.zeros((SUBLANES, cw), F32)
        accs = lax.fori_loop(0, ng, group, (zero,) * 5)
        sums = jnp.zeros((SUBLANES, cw), F32)
        for k, a in enumerate(accs):
            sums = jnp.where(rows == k, jnp.sum(a, axis=0, keepdims=True), sums)
        _accumulate(sums_ref, sums, it == 0)

    tile = pl.BlockSpec((tm, cw), lambda j, i: (i, j))
    return pl.pallas_call(
        body, name=name, grid=(C // cw, nt),
        in_specs=[tile, pl.BlockSpec((SUBLANES, cw), lambda j, i: (jnp.minimum((i + 1) * hb, last), j)),
                  tile, pl.BlockSpec((SUBLANES, cw), lambda j, i: (jnp.maximum(i * hb - 1, 0), j)),
                  pl.BlockSpec((4, cw), lambda j, i: (0, j))],
        out_specs=[tile, pl.BlockSpec((SUBLANES, cw), lambda j, i: (0, j))],
        out_shape=[jax.ShapeDtypeStruct((T, C), F32), jax.ShapeDtypeStruct((SUBLANES, C), F32)],
        compiler_params=_params(("parallel", "arbitrary"), 7 * tm * cw * 4),
    )(dxc, dxc, z, z, conv_w)


def _rg_coeffs(r, ig, xc, sp):
    la = (-RG_C) * r * sp
    a = jnp.exp(la)
    m = jnp.sqrt(-_expm1(2.0 * la))
    return a, m, m * (ig * xc)


def _rg_scan_fwd(z, r, ig, xc, sp, *, T, C, gate_off, tm=512, cw=256, name):
    ng = tm // SUBLANES

    def body(gate_ref, r_ref, i_ref, xc_ref, sp_ref, h_ref, p_ref, carry_ref):
        it = pl.program_id(1)

        @pl.when(it == 0)
        def _():
            carry_ref[...] = jnp.zeros_like(carry_ref)

        rows = _rows8(cw)
        sp_row = sp_ref[...]

        def group(g, carry):
            sl = pl.ds(pl.multiple_of(g * SUBLANES, SUBLANES), SUBLANES)
            a, _, b = _rg_coeffs(r_ref[sl, :], i_ref[sl, :], xc_ref[sl, :], sp_row)
            for s in (1, 2, 4):
                keep = rows >= s
                sa = jnp.where(keep, pltpu.roll(a, s, 0), 1.0)
                sb = jnp.where(keep, pltpu.roll(b, s, 0), 0.0)
                b = b + a * sb
                a = a * sa
            h = b + a * carry
            h_ref[sl, :] = h
            p_ref[sl, :] = h * _gelu(gate_ref[sl, :])
            return h[SUBLANES - 1:SUBLANES, :]

        last = lax.fori_loop(0, ng, group, carry_ref[0:1, :])
        carry_ref[...] = jnp.broadcast_to(last, carry_ref.shape)

    tile = pl.BlockSpec((tm, cw), lambda j, i: (i, j))
    gate_blk = gate_off // cw
    return pl.pallas_call(
        body, name=name, grid=(C // cw, T // tm),
        in_specs=[pl.BlockSpec((tm, cw), lambda j, i: (i, gate_blk + j)), tile, tile, tile,
                  pl.BlockSpec((1, cw), lambda j, i: (0, j))],
        out_specs=[tile, tile],
        out_shape=[jax.ShapeDtypeStruct((T, C), F32)] * 2,
        scratch_shapes=[pltpu.VMEM((SUBLANES, cw), F32)],
        compiler_params=_params(("parallel", "arbitrary"), 12 * tm * cw * 4),
    )(z, r, ig, xc, sp)


def _rg_scan_bwd(dp, h, z, r, ig, xc, sp, *, T, C, gate_off, tm=512, cw=256, name):
    ng, hb, nt = tm // SUBLANES, tm // SUBLANES, T // tm

    def body(dp_ref, h_ref, hp_ref, gate_ref, r_ref, i_ref, xc_ref, sp_ref,
             dgate_ref, dra_ref, dia_ref, dxc_ref, cra_ref, cia_ref, csp_ref, cg_ref, ca_ref):
        step = pl.program_id(1)

        @pl.when(step == 0)
        def _():
            cg_ref[...] = jnp.zeros_like(cg_ref)
            ca_ref[...] = jnp.zeros_like(ca_ref)

        rows = _rows8(cw)
        sp_row = sp_ref[...]
        hhalo = jnp.where(step == nt - 1, 0.0, hp_ref[...])

        def group(gi, carry):
            g_next, a_next, s_ra, s_ia, s_sp = carry
            g = ng - 1 - gi
            off = pl.multiple_of(g * SUBLANES, SUBLANES)
            sl = pl.ds(off, SUBLANES)
            rr, ii, xx = r_ref[sl, :], i_ref[sl, :], xc_ref[sl, :]
            a, m, _ = _rg_coeffs(rr, ii, xx, sp_row)
            hh = h_ref[sl, :]
            hpv = h_ref[pl.ds(pl.multiple_of(jnp.maximum(off - SUBLANES, 0), SUBLANES), SUBLANES), :]
            hpv = jnp.where(g == 0, hhalo, hpv)
            hprev = _shift_down(hh, hpv, 1, rows)
            gate = gate_ref[sl, :]
            dpv = dp_ref[sl, :]
            d = dpv * _gelu(gate)
            dgate_ref[sl, :] = dpv * hh * _dgelu(gate)
            c = jnp.where(rows < SUBLANES - 1, pltpu.roll(a, SUBLANES - 1, 0), a_next)
            for s in (1, 2, 4):
                keep = rows < SUBLANES - s
                sc = jnp.where(keep, pltpu.roll(c, SUBLANES - s, 0), 1.0)
                sd = jnp.where(keep, pltpu.roll(d, SUBLANES - s, 0), 0.0)
                d = d + c * sd
                c = c * sc
            gg = d + c * g_next
            da = gg * hprev
            dm = gg * (ii * xx)
            di = gg * (m * xx)
            dxc_ref[sl, :] = gg * (m * ii)
            dla = da * a - dm * (a * a / m)
            dra = dla * ((-RG_C) * sp_row) * (rr * (1.0 - rr))
            dia = di * (ii * (1.0 - ii))
            dra_ref[sl, :] = dra
            dia_ref[sl, :] = dia
            return (gg[0:1, :], a[0:1, :], s_ra + dra, s_ia + dia, s_sp + dla * ((-RG_C) * rr))

        zero = jnp.zeros((SUBLANES, cw), F32)
        g_first, a_first, s_ra, s_ia, s_sp = lax.fori_loop(
            0, ng, group, (cg_ref[0:1, :], ca_ref[0:1, :], zero, zero, zero))
        cg_ref[...] = jnp.broadcast_to(g_first, cg_ref.shape)
        ca_ref[...] = jnp.broadcast_to(a_first, ca_ref.shape)
        for ref, acc in ((cra_ref, s_ra), (cia_ref, s_ia), (csp_ref, s_sp)):
            _accumulate(ref, jnp.sum(acc, axis=0, keepdims=True), step == 0)

    tile = pl.BlockSpec((tm, cw), lambda j, i: (nt - 1 - i, j))
    vec = pl.BlockSpec((1, cw), lambda j, i: (0, j))
    gate_blk = gate_off // cw
    return pl.pallas_call(
        body, name=name, grid=(C // cw, nt),
        in_specs=[tile, tile, pl.BlockSpec((SUBLANES, cw), lambda j, i: (jnp.maximum((nt - 1 - i) * hb - 1, 0), j)),
                  pl.BlockSpec((tm, cw), lambda j, i: (nt - 1 - i, gate_blk + j)), tile, tile, tile, vec],
        out_specs=[tile, tile, tile, tile, vec, vec, vec],
        out_shape=[jax.ShapeDtypeStruct((T, C), F32)] * 4 + [jax.ShapeDtypeStruct((1, C), F32)] * 3,
        scratch_shapes=[pltpu.VMEM((SUBLANES, cw), F32), pltpu.VMEM((SUBLANES, cw), F32)],
        compiler_params=_params(("parallel", "arbitrary"), 24 * tm * cw * 4),
    )(dp, h, h, z, r, ig, xc, sp)


def _cscan_tables(lr, li, reverse):
    lam = (lr.reshape(-1), -li.reshape(-1) if reverse else li.reshape(-1))

    def mul(p, q):
        return p[0] * q[0] - p[1] * q[1], p[0] * q[1] + p[1] * q[0]

    pows = [lam]
    for _ in range(SUBLANES - 1):
        pows.append(mul(pows[-1], lam))
    zero = jnp.zeros_like(lam[0])
    tab = jnp.stack([pows[0][0], pows[0][1], pows[1][0], pows[1][1], pows[3][0], pows[3][1], zero, zero])
    if reverse:
        pows = pows[::-1]
    return tab, jnp.stack([p[0] for p in pows]), jnp.stack([p[1] for p in pows])


def _cscan(b_re, b_im, lr, li, *, T, C, reverse=False, h_re=None, h_im=None, tm=512, cw=512, name):
    ng, hb, nt = tm // SUBLANES, tm // SUBLANES, T // tm
    with_grad = h_re is not None
    tab, pw_re, pw_im = _cscan_tables(lr, li, reverse)

    def body(*refs):
        tab_ref, pwr_ref, pwi_ref, br_ref, bi_ref = refs[:5]
        if with_grad:
            hr_ref, hrp_ref, hi_ref, hip_ref, or_ref, oi_ref, dlr_ref, dli_ref, cr_ref, ci_ref = refs[5:]
        else:
            or_ref, oi_ref, cr_ref, ci_ref = refs[5:]
        step = pl.program_id(1)

        @pl.when(step == 0)
        def _():
            cr_ref[...] = jnp.zeros_like(cr_ref)
            ci_ref[...] = jnp.zeros_like(ci_ref)

        rows = _rows8(cw)
        lam = [(tab_ref[2 * k:2 * k + 1, :], tab_ref[2 * k + 1:2 * k + 2, :]) for k in range(3)]
        pwr, pwi = pwr_ref[...], pwi_ref[...]
        if with_grad:
            hr_halo = jnp.where(step == nt - 1, 0.0, hrp_ref[...])
            hi_halo = jnp.where(step == nt - 1, 0.0, hip_ref[...])

        def group(gi, carry):
            g = ng - 1 - gi if reverse else gi
            off = pl.multiple_of(g * SUBLANES, SUBLANES)
            sl = pl.ds(off, SUBLANES)
            xr, xi = br_ref[sl, :], bi_ref[sl, :]
            for k, s in enumerate((1, 2, 4)):
                shift = SUBLANES - s if reverse else s
                keep = rows < SUBLANES - s if reverse else rows >= s
                sr = jnp.where(keep, pltpu.roll(xr, shift, 0), 0.0)
                si = jnp.where(keep, pltpu.roll(xi, shift, 0), 0.0)
                l_re, l_im = lam[k]
                xr, xi = xr + (l_re * sr - l_im * si), xi + (l_re * si + l_im * sr)
            cr, ci = carry[0], carry[1]
            xr, xi = xr + (pwr * cr - pwi * ci), xi + (pwr * ci + pwi * cr)
            or_ref[sl, :] = xr
            oi_ref[sl, :] = xi
            edge = slice(0, 1) if reverse else slice(SUBLANES - 1, SUBLANES)
            out = (xr[edge, :], xi[edge, :])
            if with_grad:
                poff = pl.multiple_of(jnp.maximum(off - SUBLANES, 0), SUBLANES)
                hrp = jnp.where(g == 0, hr_halo, hr_ref[pl.ds(poff, SUBLANES), :])
                hip = jnp.where(g == 0, hi_halo, hi_ref[pl.ds(poff, SUBLANES), :])
                hr1 = _shift_down(hr_ref[sl, :], hrp, 1, rows)
                hi1 = _shift_down(hi_ref[sl, :], hip, 1, rows)
                out += (carry[2] + (xr * hr1 + xi * hi1), carry[3] + (xi * hr1 - xr * hi1))
            return out

        init = (cr_ref[0:1, :], ci_ref[0:1, :])
        if with_grad:
            init += (jnp.zeros((SUBLANES, cw), F32),) * 2
        res = lax.fori_loop(0, ng, group, init)
        cr_ref[...] = jnp.broadcast_to(res[0], cr_ref.shape)
        ci_ref[...] = jnp.broadcast_to(res[1], ci_ref.shape)
        if with_grad:
            for ref, acc in ((dlr_ref, res[2]), (dli_ref, res[3])):
                _accumulate(ref, jnp.sum(acc, axis=0, keepdims=True), step == 0)

    def tix(i):
        return nt - 1 - i if reverse else i

    tile = pl.BlockSpec((tm, cw), lambda j, i: (tix(i), j))
    small = pl.BlockSpec((SUBLANES, cw), lambda j, i: (0, j))
    halo = pl.BlockSpec((SUBLANES, cw), lambda j, i: (jnp.maximum(tix(i) * hb - 1, 0), j))
    vec = pl.BlockSpec((1, cw), lambda j, i: (0, j))
    in_specs = [small, small, small, tile, tile]
    args = [tab, pw_re, pw_im, b_re, b_im]
    out_specs = [tile, tile]
    out_shape = [jax.ShapeDtypeStruct((T, C), F32)] * 2
    if with_grad:
        assert reverse
        in_specs += [tile, halo, tile, halo]
        args += [h_re, h_re, h_im, h_im]
        out_specs += [vec, vec]
        out_shape += [jax.ShapeDtypeStruct((1, C), F32)] * 2
    return pl.pallas_call(
        body, name=name, grid=(C // cw, nt), in_specs=in_specs, out_specs=out_specs, out_shape=out_shape,
        scratch_shapes=[pltpu.VMEM((SUBLANES, cw), F32), pltpu.VMEM((SUBLANES, cw), F32)],
        compiler_params=_params(("parallel", "arbitrary"), (8 + (4 if with_grad else 0)) * tm * cw * 4),
    )(*args)


def _mesh_pos():
    return lax.axis_index("x"), lax.axis_index("y"), lax.axis_index("c")


def _dev_index(px, py, pc):
    return 4 * px + 2 * py + pc


def _all_gather(shards, name):
    n = len(shards)

    def body(*refs):
        ins, outs = refs[:n], refs[n:2 * n]
        send_sems, recv_sems, local_sems = refs[2 * n:]
        x, y, c = _mesh_pos()
        me, sibling = (x, y, c), (x, y, 1 - c)
        chips = [(1 - x, y), (x, 1 - y), (1 - x, 1 - y)]

        def copy(a, k, block, to, src=None):
            dst = outs[a].at[_dev_index(*block)]
            return pltpu.make_async_remote_copy(
                src_ref=dst if src is None else src, dst_ref=dst, send_sem=send_sems.at[a * 7 + k],
                recv_sem=recv_sems.at[a * 7 + k], device_id=to, device_id_type=MESH)

        mine = [pltpu.make_async_copy(ins[a], outs[a].at[_dev_index(*me)], local_sems.at[a]) for a in range(n)]
        for cp in mine:
            cp.start()
        first = []
        for a in range(n):
            first.append(copy(a, 0, me, sibling, src=ins[a]))
            first += [copy(a, 1 + j, me, (*chip, c), src=ins[a]) for j, chip in enumerate(chips)]
        for cp in first:
            cp.start()
        passed = []
        for j, chip in enumerate(chips):
            for a in range(n):
                copy(a, 1 + j, (*chip, c), me).wait_recv()
                fwd = copy(a, 4 + j, (*chip, c), sibling)
                fwd.start()
                passed.append(fwd)
        for a in range(n):
            copy(a, 0, sibling, me).wait_recv()
            for j, chip in enumerate(chips):
                copy(a, 4 + j, (*chip, 1 - c), me).wait_recv()
        for cp in first + passed:
            cp.wait_send()
        for cp in mine:
            cp.wait()

    return pl.pallas_call(
        body, name=name, in_specs=[ANY] * n, out_specs=[ANY] * n,
        out_shape=[jax.ShapeDtypeStruct((N_DEV,) + s.shape, s.dtype) for s in shards],
        scratch_shapes=[pltpu.SemaphoreType.DMA((7 * n,)), pltpu.SemaphoreType.DMA((7 * n,)),
                        pltpu.SemaphoreType.DMA((n,))],
    )(*shards)


def _exchange_blocks(parts, name):
    n = len(parts)
    relations = [(dx, dy, dc) for dx in (0, 1) for dy in (0, 1) for dc in (0, 1) if (dx, dy, dc) != (0, 0, 0)]

    def body(*refs):
        ins, outs = refs[:n], refs[n:2 * n]
        send_sems, recv_sems, local_sems = refs[2 * n:]
        x, y, c = _mesh_pos()
        me = _dev_index(x, y, c)
        mine = [pltpu.make_async_copy(ins[a].at[me], outs[a].at[me], local_sems.at[a]) for a in range(n)]
        for cp in mine:
            cp.start()
        copies = []
        for k, (dx, dy, dc) in enumerate(relations):
            peer = (x + dx - 2 * x * dx, y + dy - 2 * y * dy, c + dc - 2 * c * dc)
            for a in range(n):
                copies.append((pltpu.make_async_remote_copy(
                    src_ref=ins[a].at[_dev_index(*peer)], dst_ref=outs[a].at[me], send_sem=send_sems.at[a * 7 + k],
                    recv_sem=recv_sems.at[a * 7 + k], device_id=peer, device_id_type=MESH),
                    pltpu.make_async_remote_copy(
                    src_ref=ins[a].at[_dev_index(*peer)], dst_ref=outs[a].at[_dev_index(*peer)],
                    send_sem=send_sems.at[a * 7 + k], recv_sem=recv_sems.at[a * 7 + k], device_id=peer,
                    device_id_type=MESH)))
        for send, _ in copies:
            send.start()
        for _, recv in copies:
            recv.wait_recv()
        for send, _ in copies:
            send.wait_send()
        for cp in mine:
            cp.wait()

    return pl.pallas_call(
        body, name=name, in_specs=[ANY] * n, out_specs=[ANY] * n,
        out_shape=[jax.ShapeDtypeStruct(p.shape, p.dtype) for p in parts],
        scratch_shapes=[pltpu.SemaphoreType.DMA((7 * n,)), pltpu.SemaphoreType.DMA((7 * n,)),
                        pltpu.SemaphoreType.DMA((n,))],
    )(*parts)


def _row_tile(rows, want):
    t = min(want, rows) // SUBLANES * SUBLANES
    while rows % t:
        t -= SUBLANES
    return t


def _sum_slots(recv, *, tr, name):
    s_, r_, c_ = recv.shape
    tr = _row_tile(r_, tr)

    def body(g_ref, o_ref):
        acc = g_ref[0]
        for s in range(1, s_):
            acc = acc + g_ref[s]
        o_ref[...] = acc

    return pl.pallas_call(
        body, name=name, grid=(r_ // tr,),
        in_specs=[pl.BlockSpec((s_, tr, c_), lambda i: (0, i, 0))],
        out_specs=pl.BlockSpec((tr, c_), lambda i: (i, 0)),
        out_shape=jax.ShapeDtypeStruct((r_, c_), F32),
        compiler_params=_params(("parallel",), (2 * s_ + 3) * tr * c_ * 4),
    )(recv)


def _adamw(recv, w, m, v, *, tr, name):
    s_, r_, c_ = recv.shape
    tr = _row_tile(r_, tr)
    assert w.shape == (r_, c_), (name, w.shape, recv.shape)
    c1 = 1.0 - ADAM_B1 ** ADAM_STEP
    c2 = 1.0 - ADAM_B2 ** ADAM_STEP

    def body(g_ref, w_ref, m_ref, v_ref, go_ref, d_ref, mo_ref, vo_ref):
        g = g_ref[0]
        for s in range(1, s_):
            g = g + g_ref[s]
        mn = ADAM_B1 * m_ref[...] + (1.0 - ADAM_B1) * g
        vn = ADAM_B2 * v_ref[...] + (1.0 - ADAM_B2) * (g * g)
        go_ref[...] = g
        mo_ref[...] = mn
        vo_ref[...] = vn
        d_ref[...] = -ADAM_LR * ((mn / c1) / (jnp.sqrt(vn / c2) + ADAM_EPS) + ADAM_WD * w_ref[...])

    tile = pl.BlockSpec((tr, c_), lambda i: (i, 0))
    return pl.pallas_call(
        body, name=name, grid=(r_ // tr,),
        in_specs=[pl.BlockSpec((s_, tr, c_), lambda i: (0, i, 0)), tile, tile, tile],
        out_specs=[tile] * 4, out_shape=[jax.ShapeDtypeStruct((r_, c_), F32)] * 4,
        compiler_params=_params(("parallel",), (2 * s_ + 16) * tr * c_ * 4),
    )(recv, w, m, v)


def _s5_discretise(a_re, a_im, log_dt, b_re, b_im):
    dt = jnp.exp(log_dt)[:, None]
    lr = jnp.minimum(a_re, -1e-4)
    li = a_im
    mag = jnp.exp(lr * dt)
    lbr = mag * jnp.cos(li * dt)
    lbi = mag * jnp.sin(li * dt)
    zr, zi = lbr - 1.0, lbi
    den = lr * lr + li * li
    fr = (zr * lr + zi * li) / den
    fi = (zi * lr - zr * li) / den
    bbr = fr[..., None] * b_re - fi[..., None] * b_im
    bbi = fr[..., None] * b_im + fi[..., None] * b_re
    return lbr, lbi, bbr, bbi


def _softplus_neg(lam):
    return jnp.maximum(-lam, 0.0) + jnp.log(1.0 + jnp.exp(-jnp.abs(lam)))


S5_Q = 8
RG_Q = 2


def _local_step(x, tgt, W):
    T, D = x.shape
    C = D
    S = W["glu_w"].shape[1]
    G, P, H = W["ssm_b_re"].shape
    F = W["mlp_b_up"].shape[1]
    n_in = 2 * C + S + 2 * D
    heads, hd = W["rg_wa"].shape[0], W["rg_wa"].shape[1]
    u_off, ga_off, gb_off = 2 * C, 2 * C + S, 2 * C + S + D

    sp, sp_vjp = jax.vjp(_softplus_neg, W["rg_lambda"])
    (lbr, lbi, bbr, bbi), s5_vjp = jax.vjp(_s5_discretise, W["ssm_a_re"], W["ssm_a_im"], W["ssm_log_dt"],
                                           W["ssm_b_re"], W["ssm_b_im"])
    lam_re, lam_im = lbr.reshape(-1), lbi.reshape(-1)
    wa_bd = _bd_pack(W["rg_wa"], RG_Q).astype(BF16)
    wx_bd = _bd_pack(W["rg_wx"], RG_Q).astype(BF16)
    wb_re = _bd_pack(jnp.swapaxes(bbr, 1, 2), S5_Q).astype(BF16)
    wb_im = _bd_pack(jnp.swapaxes(bbi, 1, 2), S5_Q).astype(BF16)
    wc_re = _bd_pack(jnp.swapaxes(W["ssm_c_re"], 1, 2), S5_Q).astype(BF16)
    wc_im_neg = _bd_pack(jnp.swapaxes(-W["ssm_c_im"], 1, 2), S5_Q).astype(BF16)
    d_row = W["ssm_d"].reshape(1, S)
    jr, kr = heads // RG_Q, RG_Q * hd
    js, ku, kp = G // S5_Q, S5_Q * H, S5_Q * P

    z = _mm(x, W["w_in"], M=T, N=n_in, K=D, b_split=N_DEV, tm=512, tn=n_in // N_DEV, tk=D, name="fwd_in_proj")
    xc = _conv_fwd(z, W["conv_w"], W["conv_b"], T=T, C=C, name="fwd_conv")
    r = _bd([(xc, 0, wa_bd)], T=T, J=jr, kb=kr, nb=kr, extras=[(W["rg_ba"], "vec", 0)],
            epilogue=lambda acc, b: (_sig(acc + b),), name="fwd_rgate")
    ig = _bd([(xc, 0, wx_bd)], T=T, J=jr, kb=kr, nb=kr, extras=[(W["rg_bx"], "vec", 0)],
             epilogue=lambda acc, b: (_sig(acc + b),), name="fwd_igate")
    h, p = _rg_scan_fwd(z, r, ig, xc, sp, T=T, C=C, gate_off=C, name="fwd_rg_scan")
    y_a = _mm(p, W["w_a_out"], M=T, N=D, K=C, tm=512, tn=1024, tk=C, name="fwd_rg_out")

    bu_re = _bd([(z, u_off // ku, wb_re)], T=T, J=js, kb=ku, nb=kp, name="fwd_s5_bu_re")
    bu_im = _bd([(z, u_off // ku, wb_im)], T=T, J=js, kb=ku, nb=kp, name="fwd_s5_bu_im")
    h_re, h_im = _cscan(bu_re, bu_im, lam_re, lam_im, T=T, C=G * P, name="fwd_s5_scan")

    def s5_out(acc, u, d):
        y = acc + d * u
        return y, _gelu(y)

    y_s, yg = _bd([(h_re, 0, wc_re), (h_im, 0, wc_im_neg)], T=T, J=js, kb=kp, nb=ku,
                  extras=[(z, "tile", u_off // ku), (d_row, "vec", 0)], epilogue=s5_out, n_out=2, name="fwd_s5_out")
    glu_a = _mm(yg, W["glu_w"], M=T, N=D, K=S, b_split=N_DEV, tm=512, tn=D // N_DEV, tk=S, name="fwd_glu_w")
    glu_b = _mm(yg, W["glu_v"], M=T, N=D, K=S, b_split=N_DEV, tm=512, tn=D // N_DEV, tk=S, name="fwd_glu_v")

    cwm = 1024

    def mix_fn(ga, gb, ya, a, b):
        return (_sig(ga) * ya + _sig(gb) * (a * _sig(b)),)

    mix = _ew(mix_fn, [(z, "tile", ga_off // cwm), (z, "tile", gb_off // cwm), (y_a, "tile", 0), (glu_a, "tile", 0),
                       (glu_b, "tile", 0)], T=T, C=D, n_out=1, cw=cwm, name="fwd_mix")
    s1 = _mm(mix, W["w_out"], M=T, N=D, K=D, tm=512, tn=1024, tk=D, extras=[(x, "mn")],
             epilogue=lambda acc, xv: (ALPHA * xv + acc,), name="fwd_out_proj")

    def ln1_fn(s, g, b):
        xhat, _ = _ln_stats(s)
        return (xhat * g + b,)

    x1 = _ew(ln1_fn, [(s1, "tile", 0), (W["ln1_g"], "vec", 0), (W["ln1_b"], "vec", 0)], T=T, C=D, n_out=1,
             tm=128, name="fwd_ln1")
    hpre = _mm(x1, W["mlp_w_up"], M=T, N=F, K=D, b_split=N_DEV, tm=512, tn=F // N_DEV, tk=D,
               extras=[(W["mlp_b_up"], "n")], epilogue=lambda acc, b: (acc + b,), name="fwd_mlp_up")

    def relu2(v):
        rl = jnp.maximum(v, 0.0)
        return rl * rl

    s2 = _mm(hpre, W["mlp_w_down"], M=T, N=D, K=F, a_fn=relu2, tm=512, tn=1024, tk=1024,
             extras=[(x1, "mn"), (W["mlp_b_down"], "n")], epilogue=lambda acc, xv, b: (ALPHA * xv + acc + b,),
             name="fwd_mlp_down")

    def ln2_fn(s, t, g, b):
        xhat, rstd = _ln_stats(s)
        err = xhat * g + b - t
        dy = err * (1.0 / D)
        ds = _ln_bwd(dy, g, xhat, rstd)
        return ds, 0.5 * dy * err, dy * xhat, dy, ds

    ds2, loss_cols, d_ln2_g, d_ln2_b, d_b_down = _ew(
        ln2_fn, [(s2, "tile", 0), (tgt, "tile", 0), (W["ln2_g"], "vec", 0), (W["ln2_b"], "vec", 0)],
        T=T, C=D, n_out=1, n_cs=4, tm=128, name="bwd_loss_ln2")
    d_w_down = _mm(hpre, ds2, M=F, N=D, K=T, ta=True, a_fn=relu2, tm=1024, tn=1024, tk=512, name="bwd_w_down")

    def dhpre_fn(acc, hp):
        dv = acc * (2.0 * jnp.maximum(hp, 0.0))
        return dv, dv

    dhpre, d_b_up = _mm(ds2, W["mlp_w_down"], M=T, N=F, K=D, tb=True, tm=512, tn=1024, tk=D, extras=[(hpre, "mn")],
                        epilogue=dhpre_fn, n_cs=1, name="bwd_mlp_down")
    d_w_up = _mm(x1, dhpre, M=D, N=F, K=T, ta=True, n_split=N_DEV, tm=1024, tn=F // N_DEV, tk=512, name="bwd_w_up")
    dx1 = _mm(dhpre, W["mlp_w_up"], M=T, N=D, K=F, tb=True, b_split=N_DEV, tm=512, tn=1024, tk=F // N_DEV,
              extras=[(ds2, "mn")], epilogue=lambda acc, dv: (ALPHA * dv + acc,), name="bwd_mlp_up")

    def ln1_bwd_fn(s, dy, g):
        xhat, rstd = _ln_stats(s)
        return _ln_bwd(dy, g, xhat, rstd), dy * xhat, dy

    ds1, d_ln1_g, d_ln1_b = _ew(ln1_bwd_fn, [(s1, "tile", 0), (dx1, "tile", 0), (W["ln1_g"], "vec", 0)],
                                T=T, C=D, n_out=1, n_cs=2, tm=128, name="bwd_ln1")
    d_w_out = _mm(mix, ds1, M=D, N=D, K=T, ta=True, tm=1024, tn=1024, tk=512, name="bwd_w_out")
    dmix = _mm(ds1, W["w_out"], M=T, N=D, K=D, tb=True, tm=512, tn=1024, tk=D, name="bwd_out_proj")

    def mix_bwd_fn(dm, ga, gb, ya, a, b):
        sa, sb, sv = _sig(ga), _sig(gb), _sig(b)
        yb = a * sv
        dyb = dm * sb
        return (dm * ya * (sa * (1.0 - sa)), dm * yb * (sb * (1.0 - sb)), dm * sa, dyb * sv,
                dyb * a * (sv * (1.0 - sv)))

    dg_a, dg_b, dy_a, dglu_a, dglu_b = _ew(
        mix_bwd_fn, [(dmix, "tile", 0), (z, "tile", ga_off // cwm), (z, "tile", gb_off // cwm), (y_a, "tile", 0),
                     (glu_a, "tile", 0), (glu_b, "tile", 0)], T=T, C=D, n_out=5, cw=cwm, name="bwd_mix")

    d_w_a_out = _mm(p, dy_a, M=C, N=D, K=T, ta=True, tm=1024, tn=1024, tk=512, name="bwd_w_a_out")
    dp = _mm(dy_a, W["w_a_out"], M=T, N=C, K=D, tb=True, tm=512, tn=1024, tk=D, name="bwd_rg_out")
    dgate, dra, dia, dxc0, d_ba, d_bx, d_sp = _rg_scan_bwd(dp, h, z, r, ig, xc, sp, T=T, C=C, gate_off=C,
                                                          name="bwd_rg_scan")
    dxc = _bd([(dra, 0, wa_bd), (dia, 0, wx_bd)], T=T, J=jr, kb=kr, nb=kr, tw=True, extras=[(dxc0, "tile", 0)],
              epilogue=lambda acc, d0: (acc + d0,), name="bwd_gates")
    d_wa = _bd_unpack(_bdw(xc, 0, dra, 0, T=T, J=jr, kb=kr, nb=kr, name="bwd_w_rgate"), RG_Q)
    d_wx = _bd_unpack(_bdw(xc, 0, dia, 0, T=T, J=jr, kb=kr, nb=kr, name="bwd_w_igate"), RG_Q)
    dxr, conv_sums = _conv_bwd(dxc, z, W["conv_w"], T=T, C=C, name="bwd_conv")
    d_conv_w, d_conv_b = conv_sums[0:4], conv_sums[4:5]
    (d_lambda,) = sp_vjp(d_sp)

    d_glu_w = _mm(yg, dglu_a, M=S, N=D, K=T, ta=True, n_split=N_DEV, tm=1024, tn=D // N_DEV, tk=512, name="bwd_w_glu_w")
    d_glu_v = _mm(yg, dglu_b, M=S, N=D, K=T, ta=True, n_split=N_DEV, tm=1024, tn=D // N_DEV, tk=512, name="bwd_w_glu_v")
    dyg0 = _mm(dglu_a, W["glu_w"], M=T, N=S, K=D, tb=True, b_split=N_DEV, tm=512, tn=S, tk=D // N_DEV, name="bwd_glu_w")
    dy_s = _mm(dglu_b, W["glu_v"], M=T, N=S, K=D, tb=True, b_split=N_DEV, tm=512, tn=S, tk=D // N_DEV,
               extras=[(dyg0, "mn"), (y_s, "mn")], epilogue=lambda acc, d0, yv: ((acc + d0) * _dgelu(yv),),
               name="bwd_glu_v")
    dh_re = _bd([(dy_s, 0, wc_re)], T=T, J=js, kb=ku, nb=kp, tw=True, name="bwd_s5_dh_re")
    dh_im = _bd([(dy_s, 0, wc_im_neg)], T=T, J=js, kb=ku, nb=kp, tw=True, name="bwd_s5_dh_im")
    d_wc_re = _bdw(h_re, 0, dy_s, 0, T=T, J=js, kb=kp, nb=ku, name="bwd_w_c_re")
    d_wc_im_neg = _bdw(h_im, 0, dy_s, 0, T=T, J=js, kb=kp, nb=ku, name="bwd_w_c_im")
    g_re, g_im, d_lbr, d_lbi = _cscan(dh_re, dh_im, lam_re, lam_im, T=T, C=G * P, reverse=True, h_re=h_re, h_im=h_im,
                                      name="bwd_s5_scan")

    def du_fn(acc, dyv, uv, d):
        return acc + dyv * d, dyv * uv

    du, d_ssm_d = _bd([(g_re, 0, wb_re), (g_im, 0, wb_im)], T=T, J=js, kb=kp, nb=ku, tw=True,
                      extras=[(dy_s, "tile", 0), (z, "tile", u_off // ku), (d_row, "vec", 0)], epilogue=du_fn, n_cs=1,
                      name="bwd_s5_du")
    d_wb_re = _bdw(z, u_off // ku, g_re, 0, T=T, J=js, kb=ku, nb=kp, name="bwd_w_b_re")
    d_wb_im = _bdw(z, u_off // ku, g_im, 0, T=T, J=js, kb=ku, nb=kp, name="bwd_w_b_im")
    d_bbr = jnp.swapaxes(_bd_unpack(d_wb_re, S5_Q), 1, 2)
    d_bbi = jnp.swapaxes(_bd_unpack(d_wb_im, S5_Q), 1, 2)
    d_a_re, d_a_im, d_log_dt, d_b_re, d_b_im = s5_vjp((d_lbr.reshape(G, P), d_lbi.reshape(G, P), d_bbr, d_bbi))
    d_c_re = jnp.swapaxes(_bd_unpack(d_wc_re, S5_Q), 1, 2)
    d_c_im = -jnp.swapaxes(_bd_unpack(d_wc_im_neg, S5_Q), 1, 2)

    dz = jnp.concatenate([dxr, dgate, du, dg_a, dg_b], axis=1)
    d_w_in = _mm(x, dz, M=D, N=n_in, K=T, ta=True, n_split=N_DEV, tm=1024, tn=n_in // N_DEV, tk=512, name="bwd_w_in")
    grad_x = _mm(dz, W["w_in"], M=T, N=D, K=n_in, tb=True, b_split=N_DEV, tm=512, tn=1024, tk=n_in // N_DEV,
                 extras=[(ds1, "mn")], epilogue=lambda acc, dv: (ALPHA * dv + acc,), name="bwd_in_proj")

    grads = dict(
        w_in=d_w_in, conv_w=d_conv_w, conv_b=d_conv_b, rg_wa=d_wa, rg_ba=d_ba, rg_wx=d_wx, rg_bx=d_bx,
        rg_lambda=d_lambda, w_a_out=d_w_a_out, ssm_a_re=d_a_re, ssm_a_im=d_a_im, ssm_log_dt=d_log_dt,
        ssm_b_re=d_b_re, ssm_b_im=d_b_im, ssm_c_re=d_c_re, ssm_c_im=d_c_im, ssm_d=d_ssm_d.reshape(G, H),
        glu_w=d_glu_w, glu_v=d_glu_v, w_out=d_w_out, ln1_g=d_ln1_g, ln1_b=d_ln1_b, mlp_w_up=d_w_up, mlp_b_up=d_b_up,
        mlp_w_down=d_w_down, mlp_b_down=d_b_down, ln2_g=d_ln2_g, ln2_b=d_ln2_b)
    return jnp.sum(loss_cols), grad_x, grads


BIG = ("w_in", "w_a_out", "glu_w", "glu_v", "w_out", "mlp_w_up", "mlp_w_down")
COL_SHARDED = ("w_in", "glu_w", "glu_v", "mlp_w_up")
SMALL = ("conv_w", "conv_b", "rg_wa", "rg_ba", "rg_wx", "rg_bx", "rg_lambda", "ssm_a_re", "ssm_a_im", "ssm_log_dt",
         "ssm_b_re", "ssm_b_im", "ssm_c_re", "ssm_c_im", "ssm_d", "ln1_g", "ln1_b", "mlp_b_up", "mlp_b_down", "ln2_g",
         "ln2_b")
ORDER = ("w_in", "conv_w", "conv_b", "rg_wa", "rg_ba", "rg_wx", "rg_bx", "rg_lambda", "w_a_out", "ssm_a_re",
         "ssm_a_im", "ssm_log_dt", "ssm_b_re", "ssm_b_im", "ssm_c_re", "ssm_c_im", "ssm_d", "glu_w", "glu_v", "w_out",
         "ln1_g", "ln1_b", "mlp_w_up", "mlp_b_up", "mlp_w_down", "mlp_b_down", "ln2_g", "ln2_b")
TILE_ELEMS = SUBLANES * LANES


def _pack(arrs):
    pieces = []
    for a in arrs:
        flat = a.reshape(-1)
        flat = jnp.pad(flat, (0, (-flat.shape[0]) % TILE_ELEMS))
        pieces.append(flat.reshape(-1, LANES))
    rows = sum(p.shape[0] for p in pieces)
    pad_rows = (-rows) % (N_DEV * SUBLANES)
    if pad_rows:
        pieces.append(jnp.zeros((pad_rows, LANES), pieces[0].dtype))
    return jnp.concatenate(pieces, axis=0)


def _unpack(packed, shapes):
    out, row = [], 0
    for shp in shapes:
        n = math.prod(shp)
        rows = -(-n // TILE_ELEMS) * SUBLANES
        out.append(packed[row:row + rows].reshape(-1)[:n].reshape(shp))
        row += rows
    return out


def _step(x, tgt, w, m, v):
    dev = _dev_index(*_mesh_pos())

    gathered = _all_gather([w[k].astype(BF16) for k in BIG] + [w["conv_w"]], name="gather_weights")
    full = dict(w)
    for k, gk in zip(BIG, gathered[:len(BIG)]):
        full[k] = gk if k in COL_SHARDED else gk.reshape(-1, gk.shape[-1])
    full["conv_w"] = jnp.swapaxes(gathered[-1], 0, 1).reshape(w["conv_w"].shape[0], -1)
    for k in ("conv_b", "rg_ba", "rg_bx", "rg_lambda", "ln1_g", "ln1_b", "mlp_b_up", "mlp_b_down", "ln2_g", "ln2_b"):
        full[k] = w[k].reshape(1, -1)

    loss_part, grad_x, grads = _local_step(x, tgt, full)

    parts = []
    for k in BIG:
        gk = grads[k]
        parts.append(gk if k in COL_SHARDED else gk.reshape(N_DEV, gk.shape[0] // N_DEV, gk.shape[1]))
    small_shapes = [grads[k].shape for k in SMALL]
    parts.append(_pack([grads[k] for k in SMALL]).reshape(N_DEV, -1, LANES))
    recv = _exchange_blocks(parts, name="exchange_grads")

    out_g, out_d, out_m, out_v = {}, {}, {}, {}
    for k, rk in zip(BIG, recv[:len(BIG)]):
        out_g[k], out_d[k], out_m[k], out_v[k] = _adamw(rk, w[k], m[k], v[k], tr=128, name="adamw_" + k)

    small_block = _sum_slots(recv[-1], tr=512, name="sum_small_grads")
    (small_all,) = _all_gather([small_block], name="gather_small_grads")
    g_small = dict(zip(SMALL, _unpack(small_all.reshape(-1, LANES), small_shapes)))
    cw_cols = w["conv_w"].shape[1]
    g_small["conv_w"] = lax.dynamic_slice_in_dim(g_small["conv_w"], dev * cw_cols, cw_cols, axis=1)
    shapes = [w[k].shape for k in SMALL]
    g_pack, w_pack, m_pack, v_pack = [_pack([src[k] for k in SMALL]) for src in (g_small, w, m, v)]
    res = _adamw(g_pack[None], w_pack, m_pack, v_pack, tr=1024, name="adamw_small")
    for dst, packed in zip((out_g, out_d, out_m, out_v), res):
        dst.update(zip(SMALL, _unpack(packed, shapes)))

    loss = lax.psum(loss_part, ("x", "y", "c"))
    return loss, grad_x, out_g, out_d, out_m, out_v


def kernel(x, w_in, conv_w, conv_b, rg_wa, rg_ba, rg_wx, rg_bx, rg_lambda, w_a_out, ssm_a_re, ssm_a_im, ssm_log_dt, ssm_b_re, ssm_b_im, ssm_c_re, ssm_c_im, ssm_d, glu_w, glu_v, w_out, ln1_g, ln1_b, mlp_w_up, mlp_b_up, mlp_w_down, mlp_b_down, ln2_g, ln2_b, loss_target, m_w_in, m_conv_w, m_conv_b, m_rg_wa, m_rg_ba, m_rg_wx, m_rg_bx, m_rg_lambda, m_w_a_out, m_ssm_a_re, m_ssm_a_im, m_ssm_log_dt, m_ssm_b_re, m_ssm_b_im, m_ssm_c_re, m_ssm_c_im, m_ssm_d, m_glu_w, m_glu_v, m_w_out, m_ln1_g, m_ln1_b, m_mlp_w_up, m_mlp_b_up, m_mlp_w_down, m_mlp_b_down, m_ln2_g, m_ln2_b, v_w_in, v_conv_w, v_conv_b, v_rg_wa, v_rg_ba, v_rg_wx, v_rg_bx, v_rg_lambda, v_w_a_out, v_ssm_a_re, v_ssm_a_im, v_ssm_log_dt, v_ssm_b_re, v_ssm_b_im, v_ssm_c_re, v_ssm_c_im, v_ssm_d, v_glu_w, v_glu_v, v_w_out, v_ln1_g, v_ln1_b, v_mlp_w_up, v_mlp_b_up, v_mlp_w_down, v_mlp_b_down, v_ln2_g, v_ln2_b):
    args = locals()
    w = {k: args[k][0] for k in ORDER}
    m = {k: args["m_" + k][0] for k in ORDER}
    v = {k: args["v_" + k][0] for k in ORDER}
    loss, grad_x, out_g, out_d, out_m, out_v = _step(x[0], loss_target[0], w, m, v)
    outs = [loss, grad_x[None]]
    for group in (out_g, out_d, out_m, out_v):
        outs += [group[k].reshape(args[k].shape) for k in ORDER]
    return tuple(outs)
```

```python
import functools
import math

import jax
import jax.numpy as jnp
from jax import lax
from jax.experimental import pallas as pl
from jax.experimental.pallas import tpu as pltpu

F32 = jnp.float32
BF16 = jnp.bfloat16
MESH = pl.DeviceIdType.MESH
N_DEV = 8
SUBLANES = 8
LANES = 128
VMEM_BYTES_V7X = 64 * 2 ** 20
VMEM_CAP = VMEM_BYTES_V7X - 8 * 2 ** 20

ALPHA = 2.0 ** 0.25
LN_EPS = 1e-5
RG_C = 8.0
ADAM_LR, ADAM_B1, ADAM_B2, ADAM_EPS, ADAM_WD, ADAM_STEP = 0.001, 0.9, 0.999, 1e-08, 0.01, 10
GELU_C = math.sqrt(2.0 / math.pi)
GELU_K = 0.044715

ANY = pl.BlockSpec(memory_space=pl.ANY)


def _params(sem, vmem_bytes):
    limit = int(min(max(2 * vmem_bytes, 16 * 2 ** 20), VMEM_CAP))
    return pltpu.CompilerParams(dimension_semantics=sem, vmem_limit_bytes=limit)


def _sig(x):
    return 1.0 / (1.0 + jnp.exp(-x))


def _gelu(x):
    return 0.5 * x * (1.0 + jnp.tanh(GELU_C * (x + GELU_K * x * x * x)))


def _dgelu(x):
    th = jnp.tanh(GELU_C * (x + GELU_K * x * x * x))
    return 0.5 * (1.0 + th) + 0.5 * x * (1.0 - th * th) * (GELU_C * (1.0 + 3.0 * GELU_K * x * x))


def _expm1(x):
    p = x * (1.0 + x * (1 / 2 + x * (1 / 6 + x * (1 / 24 + x * (1 / 120 + x * (1 / 720 + x * (1 / 5040)))))))
    return jnp.where(jnp.abs(x) < 0.25, p, jnp.exp(x) - 1.0)


def _accumulate(ref, val, first):
    @pl.when(first)
    def _():
        ref[...] = val

    @pl.when(jnp.logical_not(first))
    def _():
        ref[...] += val


def _rows8(cw):
    return lax.broadcasted_iota(jnp.int32, (SUBLANES, cw), 0)


def _shift_down(cur, prev, s, rows):
    return jnp.where(rows < s, pltpu.roll(prev, s, 0), pltpu.roll(cur, s, 0))


def _shift_up(cur, nxt, s, rows):
    return jnp.where(rows < SUBLANES - s, pltpu.roll(cur, SUBLANES - s, 0), pltpu.roll(nxt, SUBLANES - s, 0))


def _mm(a, b, *, M, N, K, ta=False, tb=False, b_split=1, n_split=1, a_fn=None, extras=(), epilogue=None,
        n_out=1, n_cs=0, tm=512, tn=512, tk=512, after=None, name):
    tm, tn, tk = min(tm, M), min(tn, N), min(tk, K)
    assert M % tm == 0 and N % tn == 0 and K % tk == 0, (name, M, N, K, tm, tn, tk)
    nk = K // tk
    grid = (N // tn, M // tm, nk)
    a_spec = pl.BlockSpec((tk, tm), lambda j, i, k: (k, i)) if ta else pl.BlockSpec((tm, tk), lambda j, i, k: (i, k))
    if b_split == 1:
        b_spec = pl.BlockSpec((tn, tk), lambda j, i, k: (j, k)) if tb else pl.BlockSpec((tk, tn), lambda j, i, k: (k, j))
    elif tb:
        kb = (K // b_split) // tk
        assert kb * tk * b_split == K, name
        b_spec = pl.BlockSpec((None, tn, tk), lambda j, i, k: (k // kb, j, k % kb))
    else:
        nb = (N // b_split) // tn
        assert nb * tn * b_split == N, name
        b_spec = pl.BlockSpec((None, tk, tn), lambda j, i, k: (j // nb, k, j % nb))
    in_specs = [a_spec, b_spec]
    for arr, kind in extras:
        in_specs.append(pl.BlockSpec((tm, tn), lambda j, i, k: (i, j)) if kind == "mn"
                        else pl.BlockSpec((1, tn), lambda j, i, k: (0, j)))
    if n_split == 1:
        out_shape = [jax.ShapeDtypeStruct((M, N), F32) for _ in range(n_out)]
        out_specs = [pl.BlockSpec((tm, tn), lambda j, i, k: (i, j)) for _ in range(n_out)]
    else:
        assert n_out == 1
        nbo = (N // n_split) // tn
        assert nbo * tn * n_split == N, name
        out_shape = [jax.ShapeDtypeStruct((n_split, M, N // n_split), F32)]
        out_specs = [pl.BlockSpec((None, tm, tn), lambda j, i, k: (j // nbo, i, j % nbo))]
    out_shape += [jax.ShapeDtypeStruct((1, N), F32) for _ in range(n_cs)]
    out_specs += [pl.BlockSpec((1, tn), lambda j, i, k: (0, j)) for _ in range(n_cs)]
    ne = len(extras)
    dims = (((0 if ta else 1,), (1 if tb else 0,)), ((), ()))

    n_after = 0 if after is None else 1
    in_specs += [ANY] * n_after

    def body(*refs):
        a_ref, b_ref = refs[0], refs[1]
        ex_refs = refs[2:2 + ne]
        first_out = 2 + ne + n_after
        out_refs = refs[first_out:first_out + n_out]
        cs_refs = refs[first_out + n_out:first_out + n_out + n_cs]
        i, k = pl.program_id(1), pl.program_id(2)

        def product():
            av = a_ref[...]
            if a_fn is not None:
                av = a_fn(av.astype(F32))
            return lax.dot_general(av.astype(BF16), b_ref[...].astype(BF16), dims, preferred_element_type=F32)

        def finish(acc):
            res = (acc,) if epilogue is None else epilogue(acc, *[r[...] for r in ex_refs])
            for r, o in zip(out_refs, res[:n_out]):
                r[...] = o.astype(r.dtype)
            for r, cval in zip(cs_refs, res[n_out:]):
                _accumulate(r, jnp.sum(cval, axis=0, keepdims=True), i == 0)

        if nk == 1:
            finish(product())
            return
        acc_ref = refs[-1]

        @pl.when(k == 0)
        def _():
            acc_ref[...] = jnp.zeros_like(acc_ref)

        acc_ref[...] += product()

        @pl.when(k == nk - 1)
        def _():
            finish(acc_ref[...])

    vmem = 2 * tm * tk * a.dtype.itemsize + 2 * tk * tn * b.dtype.itemsize + (1 + 2 * n_out + 2 * ne + 2) * tm * tn * 4
    outs = pl.pallas_call(
        body, name=name, grid=grid, in_specs=in_specs, out_specs=out_specs, out_shape=out_shape,
        scratch_shapes=[pltpu.VMEM((tm, tn), F32)] if nk > 1 else [],
        compiler_params=_params(("parallel", "arbitrary", "arbitrary"), vmem),
    )(a, b, *[arr for arr, _ in extras], *([after] if n_after else []))
    return outs[0] if len(outs) == 1 else outs


def _bd(pairs, *, T, J, kb, nb, tw=False, extras=(), epilogue=None, n_out=1, n_cs=0, tm=512, name):
    assert T % tm == 0
    grid = (J, T // tm)
    npair, ne = len(pairs), len(extras)
    in_specs, args = [], []
    for arr, off, w in pairs:
        in_specs.append(pl.BlockSpec((tm, kb), lambda j, i, off=off: (i, off + j)))
        in_specs.append(pl.BlockSpec((None,) + tuple(w.shape[1:]), lambda j, i: (j, 0, 0)))
        args += [arr, w]
    for arr, kind, off in extras:
        in_specs.append(pl.BlockSpec((tm, nb), lambda j, i, off=off: (i, off + j)) if kind == "tile"
                        else pl.BlockSpec((1, nb), lambda j, i, off=off: (0, off + j)))
        args.append(arr)
    out_shape = [jax.ShapeDtypeStruct((T, J * nb), F32) for _ in range(n_out)]
    out_specs = [pl.BlockSpec((tm, nb), lambda j, i: (i, j)) for _ in range(n_out)]
    out_shape += [jax.ShapeDtypeStruct((1, J * nb), F32) for _ in range(n_cs)]
    out_specs += [pl.BlockSpec((1, nb), lambda j, i: (0, j)) for _ in range(n_cs)]
    dims = (((1,), (1 if tw else 0,)), ((), ()))

    def body(*refs):
        ex_refs = refs[2 * npair:2 * npair + ne]
        out_refs = refs[2 * npair + ne:2 * npair + ne + n_out]
        cs_refs = refs[2 * npair + ne + n_out:]
        i = pl.program_id(1)
        acc = None
        for p in range(npair):
            d = lax.dot_general(refs[2 * p][...].astype(BF16), refs[2 * p + 1][...].astype(BF16), dims,
                                preferred_element_type=F32)
            acc = d if acc is None else acc + d
        res = (acc,) if epilogue is None else epilogue(acc, *[r[...] for r in ex_refs])
        for r, o in zip(out_refs, res[:n_out]):
            r[...] = o
        for r, cval in zip(cs_refs, res[n_out:]):
            _accumulate(r, jnp.sum(cval, axis=0, keepdims=True), i == 0)

    vmem = (2 * npair * tm * kb + 2 * npair * kb * nb + (2 * n_out + 2 * ne + 3) * tm * nb) * 4
    outs = pl.pallas_call(
        body, name=name, grid=grid, in_specs=in_specs, out_specs=out_specs, out_shape=out_shape,
        compiler_params=_params(("parallel", "arbitrary"), vmem),
    )(*args)
    return outs[0] if len(outs) == 1 else outs


def _bdw(a, a_off, b, b_off, *, T, J, kb, nb, tm=512, name):
    assert T % tm == 0

    def body(a_ref, b_ref, o_ref):
        i = pl.program_id(1)
        d = lax.dot_general(a_ref[...].astype(BF16), b_ref[...].astype(BF16), (((0,), (0,)), ((), ())),
                            preferred_element_type=F32)
        _accumulate(o_ref, d, i == 0)

    return pl.pallas_call(
        body, name=name, grid=(J, T // tm),
        in_specs=[pl.BlockSpec((tm, kb), lambda j, i: (i, a_off + j)), pl.BlockSpec((tm, nb), lambda j, i: (i, b_off + j))],
        out_specs=pl.BlockSpec((None, kb, nb), lambda j, i: (j, 0, 0)),
        out_shape=jax.ShapeDtypeStruct((J, kb, nb), F32),
        compiler_params=_params(("parallel", "arbitrary"), (2 * tm * (kb + nb) + 3 * kb * nb) * 4),
    )(a, b)


def _bd_pack(w, q):
    g, a, b = w.shape
    eye = jnp.eye(q, dtype=w.dtype)
    return jnp.einsum("jqab,qr->jqarb", w.reshape(g // q, q, a, b), eye).reshape(g // q, q * a, q * b)


def _bd_unpack(wp, q):
    j, qa, qb = wp.shape
    a, b = qa // q, qb // q
    w5 = wp.reshape(j, q, a, q, b)
    return jnp.stack([w5[:, r, :, r, :] for r in range(q)], axis=1).reshape(j * q, a, b)


def _ew(fn, ins, *, T, C, n_out, n_cs=0, tm=256, cw=None, name):
    cw = C if cw is None else cw
    assert T % tm == 0 and C % cw == 0
    grid = (C // cw, T // tm)
    in_specs = []
    for arr, kind, off in ins:
        in_specs.append(pl.BlockSpec((tm, cw), lambda j, i, off=off: (i, off + j)) if kind == "tile"
                        else pl.BlockSpec((arr.shape[0], cw), lambda j, i, off=off: (0, off + j)))
    out_shape = [jax.ShapeDtypeStruct((T, C), F32) for _ in range(n_out)]
    out_specs = [pl.BlockSpec((tm, cw), lambda j, i: (i, j)) for _ in range(n_out)]
    out_shape += [jax.ShapeDtypeStruct((1, C), F32) for _ in range(n_cs)]
    out_specs += [pl.BlockSpec((1, cw), lambda j, i: (0, j)) for _ in range(n_cs)]
    nin = len(ins)

    def body(*refs):
        i = pl.program_id(1)
        res = fn(*[r[...] for r in refs[:nin]])
        for r, o in zip(refs[nin:nin + n_out], res[:n_out]):
            r[...] = o
        for r, cval in zip(refs[nin + n_out:], res[n_out:]):
            _accumulate(r, jnp.sum(cval, axis=0, keepdims=True), i == 0)

    vmem = (2 * nin + 2 * n_out + 6) * tm * cw * 4
    outs = pl.pallas_call(
        body, name=name, grid=grid, in_specs=in_specs, out_specs=out_specs, out_shape=out_shape,
        compiler_params=_params(("parallel", "arbitrary"), vmem),
    )(*[arr for arr, _, _ in ins])
    return outs[0] if len(outs) == 1 else outs


def _ln_stats(s):
    mu = jnp.mean(s, axis=-1, keepdims=True)
    d = s - mu
    var = jnp.mean(d * d, axis=-1, keepdims=True)
    rstd = lax.rsqrt(var + LN_EPS)
    return d * rstd, rstd


def _ln_bwd(dy, g, xhat, rstd):
    dxh = dy * g
    m1 = jnp.mean(dxh, axis=-1, keepdims=True)
    m2 = jnp.mean(dxh * xhat, axis=-1, keepdims=True)
    return rstd * (dxh - m1 - xhat * m2)


def _conv_fwd(z, conv_w, conv_b, *, T, C, tm=512, cw=1024, name):
    ng, hb = tm // SUBLANES, tm // SUBLANES

    def body(x_ref, halo_ref, w_ref, b_ref, o_ref):
        it = pl.program_id(1)
        rows = _rows8(cw)
        halo = jnp.where(it == 0, 0.0, halo_ref[...])
        w = w_ref[...]
        bias = b_ref[...]

        def group(g, carry):
            off = pl.multiple_of(g * SUBLANES, SUBLANES)
            cur = x_ref[pl.ds(off, SUBLANES), :]
            prev = x_ref[pl.ds(pl.multiple_of(jnp.maximum(off - SUBLANES, 0), SUBLANES), SUBLANES), :]
            prev = jnp.where(g == 0, halo, prev)
            acc = cur * w[3:4] + bias
            for s in (1, 2, 3):
                acc = acc + _shift_down(cur, prev, s, rows) * w[3 - s:4 - s]
            o_ref[pl.ds(off, SUBLANES), :] = acc
            return carry

        lax.fori_loop(0, ng, group, 0)

    return pl.pallas_call(
        body, name=name, grid=(C // cw, T // tm),
        in_specs=[pl.BlockSpec((tm, cw), lambda j, i: (i, j)),
                  pl.BlockSpec((SUBLANES, cw), lambda j, i: (jnp.maximum(i * hb - 1, 0), j)),
                  pl.BlockSpec((4, cw), lambda j, i: (0, j)), pl.BlockSpec((1, cw), lambda j, i: (0, j))],
        out_specs=pl.BlockSpec((tm, cw), lambda j, i: (i, j)),
        out_shape=jax.ShapeDtypeStruct((T, C), F32),
        compiler_params=_params(("parallel", "arbitrary"), 5 * tm * cw * 4),
    )(z, z, conv_w, conv_b)


def _conv_bwd(dxc, z, conv_w, *, T, C, tm=512, cw=512, name):
    ng, hb, last = tm // SUBLANES, tm // SUBLANES, T // SUBLANES - 1
    nt = T // tm

    def body(d_ref, dn_ref, x_ref, xp_ref, w_ref, o_ref, sums_ref):
        it = pl.program_id(1)
        rows = _rows8(cw)
        dnext = jnp.where(it == nt - 1, 0.0, dn_ref[...])
        xprev = jnp.where(it == 0, 0.0, xp_ref[...])
        w = w_ref[...]

        def group(g, accs):
            off = pl.multiple_of(g * SUBLANES, SUBLANES)
            dcur = d_ref[pl.ds(off, SUBLANES), :]
            dnx = d_ref[pl.ds(pl.multiple_of(jnp.minimum(off + SUBLANES, tm - SUBLANES), SUBLANES), SUBLANES), :]
            dnx = jnp.where(g == ng - 1, dnext, dnx)
            xcur = x_ref[pl.ds(off, SUBLANES), :]
            xpv = x_ref[pl.ds(pl.multiple_of(jnp.maximum(off - SUBLANES, 0), SUBLANES), SUBLANES), :]
            xpv = jnp.where(g == 0, xprev, xpv)
            acc = dcur * w[3:4]
            for s in (1, 2, 3):
                acc = acc + _shift_up(dcur, dnx, s, rows) * w[3 - s:4 - s]
            o_ref[pl.ds(off, SUBLANES), :] = acc
            a0, a1, a2, a3, ab = accs
            a0 = a0 + dcur * _shift_down(xcur, xpv, 3, rows)
            a1 = a1 + dcur * _shift_down(xcur, xpv, 2, rows)
            a2 = a2 + dcur * _shift_down(xcur, xpv, 1, rows)
            a3 = a3 + dcur * xcur
            return a0, a1, a2, a3, ab + dcur

        zero = jnp.zeros((SUBLANES, cw), F32)
        accs = lax.fori_loop(0, ng, group, (zero,) * 5)
        sums = jnp.zeros((SUBLANES, cw), F32)
        for k, a in enumerate(accs):
            sums = jnp.where(rows == k, jnp.sum(a, axis=0, keepdims=True), sums)
        _accumulate(sums_ref, sums, it == 0)

    tile = pl.BlockSpec((tm, cw), lambda j, i: (i, j))
    return pl.pallas_call(
        body, name=name, grid=(C // cw, nt),
        in_specs=[tile, pl.BlockSpec((SUBLANES, cw), lambda j, i: (jnp.minimum((i + 1) * hb, last), j)),
                  tile, pl.BlockSpec((SUBLANES, cw), lambda j, i: (jnp.maximum(i * hb - 1, 0), j)),
                  pl.BlockSpec((4, cw), lambda j, i: (0, j))],
        out_specs=[tile, pl.BlockSpec((SUBLANES, cw), lambda j, i: (0, j))],
        out_shape=[jax.ShapeDtypeStruct((T, C), F32), jax.ShapeDtypeStruct((SUBLANES, C), F32)],
        compiler_params=_params(("parallel", "arbitrary"), 7 * tm * cw * 4),
    )(dxc, dxc, z, z, conv_w)


def _rg_coeffs(r, ig, xc, sp):
    la = (-RG_C) * r * sp
    a = jnp.exp(la)
    m = jnp.sqrt(-_expm1(2.0 * la))
    return a, m, m * (ig * xc)


def _rg_scan_fwd(z, r, ig, xc, sp, *, T, C, gate_off, tm=512, cw=256, name):
    ng = tm // SUBLANES

    def body(gate_ref, r_ref, i_ref, xc_ref, sp_ref, h_ref, p_ref, carry_ref):
        it = pl.program_id(1)

        @pl.when(it == 0)
        def _():
            carry_ref[...] = jnp.zeros_like(carry_ref)

        rows = _rows8(cw)
        sp_row = sp_ref[...]

        def group(g, carry):
            sl = pl.ds(pl.multiple_of(g * SUBLANES, SUBLANES), SUBLANES)
            a, _, b = _rg_coeffs(r_ref[sl, :], i_ref[sl, :], xc_ref[sl, :], sp_row)
            for s in (1, 2, 4):
                keep = rows >= s
                sa = jnp.where(keep, pltpu.roll(a, s, 0), 1.0)
                sb = jnp.where(keep, pltpu.roll(b, s, 0), 0.0)
                b = b + a * sb
                a = a * sa
            h = b + a * carry
            h_ref[sl, :] = h
            p_ref[sl, :] = h * _gelu(gate_ref[sl, :])
            return h[SUBLANES - 1:SUBLANES, :]

        last = lax.fori_loop(0, ng, group, carry_ref[0:1, :])
        carry_ref[...] = jnp.broadcast_to(last, carry_ref.shape)

    tile = pl.BlockSpec((tm, cw), lambda j, i: (i, j))
    gate_blk = gate_off // cw
    return pl.pallas_call(
        body, name=name, grid=(C // cw, T // tm),
        in_specs=[pl.BlockSpec((tm, cw), lambda j, i: (i, gate_blk + j)), tile, tile, tile,
                  pl.BlockSpec((1, cw), lambda j, i: (0, j))],
        out_specs=[tile, tile],
        out_shape=[jax.ShapeDtypeStruct((T, C), F32)] * 2,
        scratch_shapes=[pltpu.VMEM((SUBLANES, cw), F32)],
        compiler_params=_params(("parallel", "arbitrary"), 12 * tm * cw * 4),
    )(z, r, ig, xc, sp)


def _rg_scan_bwd(dp, h, z, r, ig, xc, sp, *, T, C, gate_off, tm=512, cw=256, name):
    ng, hb, nt = tm // SUBLANES, tm // SUBLANES, T // tm

    def body(dp_ref, h_ref, hp_ref, gate_ref, r_ref, i_ref, xc_ref, sp_ref,
             dgate_ref, dra_ref, dia_ref, dxc_ref, cra_ref, cia_ref, csp_ref, cg_ref, ca_ref):
        step = pl.program_id(1)

        @pl.when(step == 0)
        def _():
            cg_ref[...] = jnp.zeros_like(cg_ref)
            ca_ref[...] = jnp.zeros_like(ca_ref)

        rows = _rows8(cw)
        sp_row = sp_ref[...]
        hhalo = jnp.where(step == nt - 1, 0.0, hp_ref[...])

        def group(gi, carry):
            g_next, a_next, s_ra, s_ia, s_sp = carry
            g = ng - 1 - gi
            off = pl.multiple_of(g * SUBLANES, SUBLANES)
            sl = pl.ds(off, SUBLANES)
            rr, ii, xx = r_ref[sl, :], i_ref[sl, :], xc_ref[sl, :]
            a, m, _ = _rg_coeffs(rr, ii, xx, sp_row)
            hh = h_ref[sl, :]
            hpv = h_ref[pl.ds(pl.multiple_of(jnp.maximum(off - SUBLANES, 0), SUBLANES), SUBLANES), :]
            hpv = jnp.where(g == 0, hhalo, hpv)
            hprev = _shift_down(hh, hpv, 1, rows)
            gate = gate_ref[sl, :]
            dpv = dp_ref[sl, :]
            d = dpv * _gelu(gate)
            dgate_ref[sl, :] = dpv * hh * _dgelu(gate)
            c = jnp.where(rows < SUBLANES - 1, pltpu.roll(a, SUBLANES - 1, 0), a_next)
            for s in (1, 2, 4):
                keep = rows < SUBLANES - s
                sc = jnp.where(keep, pltpu.roll(c, SUBLANES - s, 0), 1.0)
                sd = jnp.where(keep, pltpu.roll(d, SUBLANES - s, 0), 0.0)
                d = d + c * sd
                c = c * sc
            gg = d + c * g_next
            da = gg * hprev
            dm = gg * (ii * xx)
            di = gg * (m * xx)
            dxc_ref[sl, :] = gg * (m * ii)
            dla = da * a - dm * (a * a / m)
            dra = dla * ((-RG_C) * sp_row) * (rr * (1.0 - rr))
            dia = di * (ii * (1.0 - ii))
            dra_ref[sl, :] = dra
            dia_ref[sl, :] = dia
            return (gg[0:1, :], a[0:1, :], s_ra + dra, s_ia + dia, s_sp + dla * ((-RG_C) * rr))

        zero = jnp.zeros((SUBLANES, cw), F32)
        g_first, a_first, s_ra, s_ia, s_sp = lax.fori_loop(
            0, ng, group, (cg_ref[0:1, :], ca_ref[0:1, :], zero, zero, zero))
        cg_ref[...] = jnp.broadcast_to(g_first, cg_ref.shape)
        ca_ref[...] = jnp.broadcast_to(a_first, ca_ref.shape)
        for ref, acc in ((cra_ref, s_ra), (cia_ref, s_ia), (csp_ref, s_sp)):
            _accumulate(ref, jnp.sum(acc, axis=0, keepdims=True), step == 0)

    tile = pl.BlockSpec((tm, cw), lambda j, i: (nt - 1 - i, j))
    vec = pl.BlockSpec((1, cw), lambda j, i: (0, j))
    gate_blk = gate_off // cw
    return pl.pallas_call(
        body, name=name, grid=(C // cw, nt),
        in_specs=[tile, tile, pl.BlockSpec((SUBLANES, cw), lambda j, i: (jnp.maximum((nt - 1 - i) * hb - 1, 0), j)),
                  pl.BlockSpec((tm, cw), lambda j, i: (nt - 1 - i, gate_blk + j)), tile, tile, tile, vec],
        out_specs=[tile, tile, tile, tile, vec, vec, vec],
        out_shape=[jax.ShapeDtypeStruct((T, C), F32)] * 4 + [jax.ShapeDtypeStruct((1, C), F32)] * 3,
        scratch_shapes=[pltpu.VMEM((SUBLANES, cw), F32), pltpu.VMEM((SUBLANES, cw), F32)],
        compiler_params=_params(("parallel", "arbitrary"), 24 * tm * cw * 4),
    )(dp, h, h, z, r, ig, xc, sp)


def _cscan_tables(lr, li, reverse):
    lam = (lr.reshape(-1), -li.reshape(-1) if reverse else li.reshape(-1))

    def mul(p, q):
        return p[0] * q[0] - p[1] * q[1], p[0] * q[1] + p[1] * q[0]

    pows = [lam]
    for _ in range(SUBLANES - 1):
        pows.append(mul(pows[-1], lam))
    zero = jnp.zeros_like(lam[0])
    tab = jnp.stack([pows[0][0], pows[0][1], pows[1][0], pows[1][1], pows[3][0], pows[3][1], zero, zero])
    if reverse:
        pows = pows[::-1]
    return tab, jnp.stack([p[0] for p in pows]), jnp.stack([p[1] for p in pows])


def _cscan(b_re, b_im, lr, li, *, T, C, reverse=False, h_re=None, h_im=None, tm=512, cw=512, name):
    ng, hb, nt = tm // SUBLANES, tm // SUBLANES, T // tm
    with_grad = h_re is not None
    tab, pw_re, pw_im = _cscan_tables(lr, li, reverse)

    def body(*refs):
        tab_ref, pwr_ref, pwi_ref, br_ref, bi_ref = refs[:5]
        if with_grad:
            hr_ref, hrp_ref, hi_ref, hip_ref, or_ref, oi_ref, dlr_ref, dli_ref, cr_ref, ci_ref = refs[5:]
        else:
            or_ref, oi_ref, cr_ref, ci_ref = refs[5:]
        step = pl.program_id(1)

        @pl.when(step == 0)
        def _():
            cr_ref[...] = jnp.zeros_like(cr_ref)
            ci_ref[...] = jnp.zeros_like(ci_ref)

        rows = _rows8(cw)
        lam = [(tab_ref[2 * k:2 * k + 1, :], tab_ref[2 * k + 1:2 * k + 2, :]) for k in range(3)]
        pwr, pwi = pwr_ref[...], pwi_ref[...]
        if with_grad:
            hr_halo = jnp.where(step == nt - 1, 0.0, hrp_ref[...])
            hi_halo = jnp.where(step == nt - 1, 0.0, hip_ref[...])

        def group(gi, carry):
            g = ng - 1 - gi if reverse else gi
            off = pl.multiple_of(g * SUBLANES, SUBLANES)
            sl = pl.ds(off, SUBLANES)
            xr, xi = br_ref[sl, :], bi_ref[sl, :]
            for k, s in enumerate((1, 2, 4)):
                shift = SUBLANES - s if reverse else s
                keep = rows < SUBLANES - s if reverse else rows >= s
                sr = jnp.where(keep, pltpu.roll(xr, shift, 0), 0.0)
                si = jnp.where(keep, pltpu.roll(xi, shift, 0), 0.0)
                l_re, l_im = lam[k]
                xr, xi = xr + (l_re * sr - l_im * si), xi + (l_re * si + l_im * sr)
            cr, ci = carry[0], carry[1]
            xr, xi = xr + (pwr * cr - pwi * ci), xi + (pwr * ci + pwi * cr)
            or_ref[sl, :] = xr
            oi_ref[sl, :] = xi
            edge = slice(0, 1) if reverse else slice(SUBLANES - 1, SUBLANES)
            out = (xr[edge, :], xi[edge, :])
            if with_grad:
                poff = pl.multiple_of(jnp.maximum(off - SUBLANES, 0), SUBLANES)
                hrp = jnp.where(g == 0, hr_halo, hr_ref[pl.ds(poff, SUBLANES), :])
                hip = jnp.where(g == 0, hi_halo, hi_ref[pl.ds(poff, SUBLANES), :])
                hr1 = _shift_down(hr_ref[sl, :], hrp, 1, rows)
                hi1 = _shift_down(hi_ref[sl, :], hip, 1, rows)
                out += (carry[2] + (xr * hr1 + xi * hi1), carry[3] + (xi * hr1 - xr * hi1))
            return out

        init = (cr_ref[0:1, :], ci_ref[0:1, :])
        if with_grad:
            init += (jnp.zeros((SUBLANES, cw), F32),) * 2
        res = lax.fori_loop(0, ng, group, init)
        cr_ref[...] = jnp.broadcast_to(res[0], cr_ref.shape)
        ci_ref[...] = jnp.broadcast_to(res[1], ci_ref.shape)
        if with_grad:
            for ref, acc in ((dlr_ref, res[2]), (dli_ref, res[3])):
                _accumulate(ref, jnp.sum(acc, axis=0, keepdims=True), step == 0)

    def tix(i):
        return nt - 1 - i if reverse else i

    tile = pl.BlockSpec((tm, cw), lambda j, i: (tix(i), j))
    small = pl.BlockSpec((SUBLANES, cw), lambda j, i: (0, j))
    halo = pl.BlockSpec((SUBLANES, cw), lambda j, i: (jnp.maximum(tix(i) * hb - 1, 0), j))
    vec = pl.BlockSpec((1, cw), lambda j, i: (0, j))
    in_specs = [small, small, small, tile, tile]
    args = [tab, pw_re, pw_im, b_re, b_im]
    out_specs = [tile, tile]
    out_shape = [jax.ShapeDtypeStruct((T, C), F32)] * 2
    if with_grad:
        assert reverse
        in_specs += [tile, halo, tile, halo]
        args += [h_re, h_re, h_im, h_im]
        out_specs += [vec, vec]
        out_shape += [jax.ShapeDtypeStruct((1, C), F32)] * 2
    return pl.pallas_call(
        body, name=name, grid=(C // cw, nt), in_specs=in_specs, out_specs=out_specs, out_shape=out_shape,
        scratch_shapes=[pltpu.VMEM((SUBLANES, cw), F32), pltpu.VMEM((SUBLANES, cw), F32)],
        compiler_params=_params(("parallel", "arbitrary"), (8 + (4 if with_grad else 0)) * tm * cw * 4),
    )(*args)


def _mesh_pos():
    return lax.axis_index("x"), lax.axis_index("y"), lax.axis_index("c")


def _dev_index(px, py, pc):
    return 4 * px + 2 * py + pc


def _all_gather(shards, name):
    n = len(shards)

    def body(*refs):
        ins, outs = refs[:n], refs[n:2 * n]
        send_sems, recv_sems, local_sems = refs[2 * n:]
        x, y, c = _mesh_pos()
        me, sibling = (x, y, c), (x, y, 1 - c)
        chips = [(1 - x, y), (x, 1 - y), (1 - x, 1 - y)]

        def copy(a, k, block, to, src=None):
            dst = outs[a].at[_dev_index(*block)]
            return pltpu.make_async_remote_copy(
                src_ref=dst if src is None else src, dst_ref=dst, send_sem=send_sems.at[a * 7 + k],
                recv_sem=recv_sems.at[a * 7 + k], device_id=to, device_id_type=MESH)

        mine = [pltpu.make_async_copy(ins[a], outs[a].at[_dev_index(*me)], local_sems.at[a]) for a in range(n)]
        for cp in mine:
            cp.start()
        first = []
        for a in range(n):
            first.append(copy(a, 0, me, sibling, src=ins[a]))
            first += [copy(a, 1 + j, me, (*chip, c), src=ins[a]) for j, chip in enumerate(chips)]
        for cp in first:
            cp.start()
        passed = []
        for j, chip in enumerate(chips):
            for a in range(n):
                copy(a, 1 + j, (*chip, c), me).wait_recv()
                fwd = copy(a, 4 + j, (*chip, c), sibling)
                fwd.start()
                passed.append(fwd)
        for a in range(n):
            copy(a, 0, sibling, me).wait_recv()
            for j, chip in enumerate(chips):
                copy(a, 4 + j, (*chip, 1 - c), me).wait_recv()
        for cp in first + passed:
            cp.wait_send()
        for cp in mine:
            cp.wait()

    return pl.pallas_call(
        body, name=name, in_specs=[ANY] * n, out_specs=[ANY] * n,
        out_shape=[jax.ShapeDtypeStruct((N_DEV,) + s.shape, s.dtype) for s in shards],
        scratch_shapes=[pltpu.SemaphoreType.DMA((7 * n,)), pltpu.SemaphoreType.DMA((7 * n,)),
                        pltpu.SemaphoreType.DMA((n,))],
    )(*shards)


def _exchange_blocks(parts, name):
    n = len(parts)
    relations = [(dx, dy, dc) for dx in (0, 1) for dy in (0, 1) for dc in (0, 1) if (dx, dy, dc) != (0, 0, 0)]

    def body(*refs):
        ins, outs = refs[:n], refs[n:2 * n]
        send_sems, recv_sems, local_sems = refs[2 * n:]
        x, y, c = _mesh_pos()
        me = _dev_index(x, y, c)
        mine = [pltpu.make_async_copy(ins[a].at[me], outs[a].at[me], local_sems.at[a]) for a in range(n)]
        for cp in mine:
            cp.start()
        copies = []
        for k, (dx, dy, dc) in enumerate(relations):
            peer = (x + dx - 2 * x * dx, y + dy - 2 * y * dy, c + dc - 2 * c * dc)
            for a in range(n):
                copies.append((pltpu.make_async_remote_copy(
                    src_ref=ins[a].at[_dev_index(*peer)], dst_ref=outs[a].at[me], send_sem=send_sems.at[a * 7 + k],
                    recv_sem=recv_sems.at[a * 7 + k], device_id=peer, device_id_type=MESH),
                    pltpu.make_async_remote_copy(
                    src_ref=ins[a].at[_dev_index(*peer)], dst_ref=outs[a].at[_dev_index(*peer)],
                    send_sem=send_sems.at[a * 7 + k], recv_sem=recv_sems.at[a * 7 + k], device_id=peer,
                    device_id_type=MESH)))
        for send, _ in copies:
            send.start()
        for _, recv in copies:
            recv.wait_recv()
        for send, _ in copies:
            send.wait_send()
        for cp in mine:
            cp.wait()

    return pl.pallas_call(
        body, name=name, in_specs=[ANY] * n, out_specs=[ANY] * n,
        out_shape=[jax.ShapeDtypeStruct(p.shape, p.dtype) for p in parts],
        scratch_shapes=[pltpu.SemaphoreType.DMA((7 * n,)), pltpu.SemaphoreType.DMA((7 * n,)),
                        pltpu.SemaphoreType.DMA((n,))],
    )(*parts)


HBM = pl.BlockSpec(memory_space=pltpu.HBM)
SEM = pl.BlockSpec(memory_space=pltpu.SEMAPHORE)
EFFECT = pltpu.SideEffectType.DATAFLOW_SIDE_EFFECTING
RELATIONS = [(dx, dy, dc) for dx in (0, 1) for dy in (0, 1) for dc in (0, 1) if (dx, dy, dc) != (0, 0, 0)]


def _peer(rel):
    x, y, c = _mesh_pos()
    dx, dy, dc = rel
    return (x + dx - 2 * x * dx, y + dy - 2 * y * dy, c + dc - 2 * c * dc)


def _split_copy(src_ref, land_ref, send_sems, recv_sems, k, scatter, incoming):
    peer = _peer(RELATIONS[k])
    me = _dev_index(*_mesh_pos())
    src = src_ref.at[_dev_index(*peer)] if scatter else src_ref
    dst = land_ref.at[_dev_index(*peer) if incoming else me]
    return pltpu.make_async_remote_copy(src_ref=src, dst_ref=dst, send_sem=send_sems.at[k], recv_sem=recv_sems.at[k],
                                        device_id=peer, device_id_type=MESH)


def _exchange_start(srcs, lands, *, scatter, name):
    n = len(srcs)

    def body(*refs):
        src_refs, land_refs = refs[:n], refs[n:2 * n]
        send, recv = refs[2 * n:3 * n], refs[3 * n:4 * n]
        token = refs[-1]
        for k in range(len(RELATIONS)):
            for a in range(n):
                _split_copy(src_refs[a], land_refs[a], send[a], recv[a], k, scatter, incoming=False).start()
        token[...] = jnp.zeros_like(token)

    n_rel = len(RELATIONS)
    outs = pl.pallas_call(
        body, name=name, in_specs=[HBM] * (2 * n),
        out_shape=[pltpu.SemaphoreType.DMA((n_rel,))] * (2 * n)
        + [pltpu.HBM(s.shape, s.dtype) for s in srcs] + [pltpu.HBM(s.shape, s.dtype) for s in lands]
        + [jax.ShapeDtypeStruct((SUBLANES, LANES), F32)],
        out_specs=[SEM] * (2 * n) + [HBM] * (2 * n) + [pl.BlockSpec(memory_space=pltpu.VMEM)],
        input_output_aliases={**{a: 2 * n + a for a in range(n)}, **{n + a: 3 * n + a for a in range(n)}},
        compiler_params=pltpu.CompilerParams(has_side_effects=EFFECT),
    )(*[pltpu.with_memory_space_constraint(s, pltpu.HBM) for s in srcs],
      *[pltpu.with_memory_space_constraint(s, pltpu.HBM) for s in lands])
    per_array = [(outs[a], outs[n + a], outs[2 * n + a], outs[3 * n + a]) for a in range(n)]
    return per_array, outs[-1]


def _exchange_wait(handle, after, *, scatter, name):
    send_sems, recv_sems, src_thru, land_thru = handle

    def body(src_ref, land_ref, send, recv, after_ref, src_dead, got_ref):
        for k in range(len(RELATIONS)):
            cp = _split_copy(src_ref, land_ref, send, recv, k, scatter, incoming=True)
            cp.wait_send()
            cp.wait_recv()

    return pl.pallas_call(
        body, name=name, in_specs=[HBM, HBM, SEM, SEM, ANY],
        out_shape=[pltpu.HBM(src_thru.shape, src_thru.dtype), pltpu.HBM(land_thru.shape, land_thru.dtype)],
        out_specs=[HBM, HBM], input_output_aliases={0: 0, 1: 1},
        compiler_params=pltpu.CompilerParams(has_side_effects=EFFECT),
    )(src_thru, land_thru, send_sems, recv_sems, after)[1]


def _landing_zone(own_block):
    me = _dev_index(*_mesh_pos())
    zone = lax.empty((N_DEV,) + own_block.shape, own_block.dtype)
    return lax.dynamic_update_index_in_dim(zone, own_block, me, 0)


def _row_tile(rows, want):
    t = min(want, rows) // SUBLANES * SUBLANES
    while rows % t:
        t -= SUBLANES
    return t


def _sum_slots(recv, *, tr, name):
    s_, r_, c_ = recv.shape
    tr = _row_tile(r_, tr)

    def body(g_ref, o_ref):
        acc = g_ref[0]
        for s in range(1, s_):
            acc = acc + g_ref[s]
        o_ref[...] = acc

    return pl.pallas_call(
        body, name=name, grid=(r_ // tr,),
        in_specs=[pl.BlockSpec((s_, tr, c_), lambda i: (0, i, 0))],
        out_specs=pl.BlockSpec((tr, c_), lambda i: (i, 0)),
        out_shape=jax.ShapeDtypeStruct((r_, c_), F32),
        compiler_params=_params(("parallel",), (2 * s_ + 3) * tr * c_ * 4),
    )(recv)


def _adamw(recv, w, m, v, *, tr, name):
    s_, r_, c_ = recv.shape
    tr = _row_tile(r_, tr)
    assert w.shape == (r_, c_), (name, w.shape, recv.shape)
    c1 = 1.0 - ADAM_B1 ** ADAM_STEP
    c2 = 1.0 - ADAM_B2 ** ADAM_STEP

    def body(g_ref, w_ref, m_ref, v_ref, go_ref, d_ref, mo_ref, vo_ref):
        g = g_ref[0]
        for s in range(1, s_):
            g = g + g_ref[s]
        mn = ADAM_B1 * m_ref[...] + (1.0 - ADAM_B1) * g
        vn = ADAM_B2 * v_ref[...] + (1.0 - ADAM_B2) * (g * g)
        go_ref[...] = g
        mo_ref[...] = mn
        vo_ref[...] = vn
        d_ref[...] = -ADAM_LR * ((mn / c1) / (jnp.sqrt(vn / c2) + ADAM_EPS) + ADAM_WD * w_ref[...])

    tile = pl.BlockSpec((tr, c_), lambda i: (i, 0))
    return pl.pallas_call(
        body, name=name, grid=(r_ // tr,),
        in_specs=[pl.BlockSpec((s_, tr, c_), lambda i: (0, i, 0)), tile, tile, tile],
        out_specs=[tile] * 4, out_shape=[jax.ShapeDtypeStruct((r_, c_), F32)] * 4,
        compiler_params=_params(("parallel",), (2 * s_ + 16) * tr * c_ * 4),
    )(recv, w, m, v)


def _s5_discretise(a_re, a_im, log_dt, b_re, b_im):
    dt = jnp.exp(log_dt)[:, None]
    lr = jnp.minimum(a_re, -1e-4)
    li = a_im
    mag = jnp.exp(lr * dt)
    lbr = mag * jnp.cos(li * dt)
    lbi = mag * jnp.sin(li * dt)
    zr, zi = lbr - 1.0, lbi
    den = lr * lr + li * li
    fr = (zr * lr + zi * li) / den
    fi = (zi * lr - zr * li) / den
    bbr = fr[..., None] * b_re - fi[..., None] * b_im
    bbi = fr[..., None] * b_im + fi[..., None] * b_re
    return lbr, lbi, bbr, bbi


def _softplus_neg(lam):
    return jnp.maximum(-lam, 0.0) + jnp.log(1.0 + jnp.exp(-jnp.abs(lam)))


S5_Q = 8
RG_Q = 2


def _local_step(x, tgt, W, comm):
    T, D = x.shape
    C = D
    G, P, H = W["ssm_b_re"].shape
    S = G * H
    F = W["mlp_b_up"].shape[1]
    n_in = 2 * C + S + 2 * D
    heads, hd = W["rg_wa"].shape[0], W["rg_wa"].shape[1]
    u_off, ga_off, gb_off = 2 * C, 2 * C + S, 2 * C + S + D

    sp, sp_vjp = jax.vjp(_softplus_neg, W["rg_lambda"])
    (lbr, lbi, bbr, bbi), s5_vjp = jax.vjp(_s5_discretise, W["ssm_a_re"], W["ssm_a_im"], W["ssm_log_dt"],
                                           W["ssm_b_re"], W["ssm_b_im"])
    lam_re, lam_im = lbr.reshape(-1), lbi.reshape(-1)
    wa_bd = _bd_pack(W["rg_wa"], RG_Q).astype(BF16)
    wx_bd = _bd_pack(W["rg_wx"], RG_Q).astype(BF16)
    wb_re = _bd_pack(jnp.swapaxes(bbr, 1, 2), S5_Q).astype(BF16)
    wb_im = _bd_pack(jnp.swapaxes(bbi, 1, 2), S5_Q).astype(BF16)
    wc_re = _bd_pack(jnp.swapaxes(W["ssm_c_re"], 1, 2), S5_Q).astype(BF16)
    wc_im_neg = _bd_pack(jnp.swapaxes(-W["ssm_c_im"], 1, 2), S5_Q).astype(BF16)
    d_row = W["ssm_d"].reshape(1, S)
    jr, kr = heads // RG_Q, RG_Q * hd
    js, ku, kp = G // S5_Q, S5_Q * H, S5_Q * P

    w_in = comm.weight("w_in", None)
    z = _mm(x, w_in, M=T, N=n_in, K=D, b_split=N_DEV, tm=512, tn=n_in // N_DEV, tk=D, after=comm.gather_token,
            name="fwd_in_proj")
    xc = _conv_fwd(z, W["conv_w"], W["conv_b"], T=T, C=C, name="fwd_conv")
    r = _bd([(xc, 0, wa_bd)], T=T, J=jr, kb=kr, nb=kr, extras=[(W["rg_ba"], "vec", 0)],
            epilogue=lambda acc, b: (_sig(acc + b),), name="fwd_rgate")
    ig = _bd([(xc, 0, wx_bd)], T=T, J=jr, kb=kr, nb=kr, extras=[(W["rg_bx"], "vec", 0)],
             epilogue=lambda acc, b: (_sig(acc + b),), name="fwd_igate")
    h, p = _rg_scan_fwd(z, r, ig, xc, sp, T=T, C=C, gate_off=C, name="fwd_rg_scan")
    w_a_out = comm.weight("w_a_out", p)
    y_a = _mm(p, w_a_out, M=T, N=D, K=C, tm=512, tn=1024, tk=C, name="fwd_rg_out")

    bu_re = _bd([(z, u_off // ku, wb_re)], T=T, J=js, kb=ku, nb=kp, name="fwd_s5_bu_re")
    bu_im = _bd([(z, u_off // ku, wb_im)], T=T, J=js, kb=ku, nb=kp, name="fwd_s5_bu_im")
    h_re, h_im = _cscan(bu_re, bu_im, lam_re, lam_im, T=T, C=G * P, name="fwd_s5_scan")

    def s5_out(acc, u, d):
        y = acc + d * u
        return y, _gelu(y)

    y_s, yg = _bd([(h_re, 0, wc_re), (h_im, 0, wc_im_neg)], T=T, J=js, kb=kp, nb=ku,
                  extras=[(z, "tile", u_off // ku), (d_row, "vec", 0)], epilogue=s5_out, n_out=2, name="fwd_s5_out")
    w_glu_w, w_glu_v = comm.weight("glu_w", yg), comm.weight("glu_v", yg)
    glu_a = _mm(yg, w_glu_w, M=T, N=D, K=S, b_split=N_DEV, tm=512, tn=D // N_DEV, tk=S, name="fwd_glu_w")
    glu_b = _mm(yg, w_glu_v, M=T, N=D, K=S, b_split=N_DEV, tm=512, tn=D // N_DEV, tk=S, name="fwd_glu_v")

    cwm = 1024

    def mix_fn(ga, gb, ya, a, b):
        return (_sig(ga) * ya + _sig(gb) * (a * _sig(b)),)

    mix = _ew(mix_fn, [(z, "tile", ga_off // cwm), (z, "tile", gb_off // cwm), (y_a, "tile", 0), (glu_a, "tile", 0),
                       (glu_b, "tile", 0)], T=T, C=D, n_out=1, cw=cwm, name="fwd_mix")
    w_out = comm.weight("w_out", mix)
    s1 = _mm(mix, w_out, M=T, N=D, K=D, tm=512, tn=1024, tk=D, extras=[(x, "mn")],
             epilogue=lambda acc, xv: (ALPHA * xv + acc,), name="fwd_out_proj")

    def ln1_fn(s, g, b):
        xhat, _ = _ln_stats(s)
        return (xhat * g + b,)

    x1 = _ew(ln1_fn, [(s1, "tile", 0), (W["ln1_g"], "vec", 0), (W["ln1_b"], "vec", 0)], T=T, C=D, n_out=1,
             tm=128, name="fwd_ln1")
    w_up = comm.weight("mlp_w_up", x1)
    hpre = _mm(x1, w_up, M=T, N=F, K=D, b_split=N_DEV, tm=512, tn=F // N_DEV, tk=D,
               extras=[(W["mlp_b_up"], "n")], epilogue=lambda acc, b: (acc + b,), name="fwd_mlp_up")

    def relu2(v):
        rl = jnp.maximum(v, 0.0)
        return rl * rl

    w_down = comm.weight("mlp_w_down", hpre)
    s2 = _mm(hpre, w_down, M=T, N=D, K=F, a_fn=relu2, tm=512, tn=1024, tk=1024,
             extras=[(x1, "mn"), (W["mlp_b_down"], "n")], epilogue=lambda acc, xv, b: (ALPHA * xv + acc + b,),
             name="fwd_mlp_down")

    def ln2_fn(s, t, g, b):
        xhat, rstd = _ln_stats(s)
        err = xhat * g + b - t
        dy = err * (1.0 / D)
        ds = _ln_bwd(dy, g, xhat, rstd)
        return ds, 0.5 * dy * err, dy * xhat, dy, ds

    ds2, loss_cols, d_ln2_g, d_ln2_b, d_b_down = _ew(
        ln2_fn, [(s2, "tile", 0), (tgt, "tile", 0), (W["ln2_g"], "vec", 0), (W["ln2_b"], "vec", 0)],
        T=T, C=D, n_out=1, n_cs=4, tm=128, name="bwd_loss_ln2")
    d_w_down = _mm(hpre, ds2, M=F, N=D, K=T, ta=True, a_fn=relu2, tm=1024, tn=1024, tk=512, name="bwd_w_down")
    sent = comm.send_grad("mlp_w_down", d_w_down)

    def dhpre_fn(acc, hp):
        dv = acc * (2.0 * jnp.maximum(hp, 0.0))
        return dv, dv

    dhpre, d_b_up = _mm(ds2, w_down, M=T, N=F, K=D, tb=True, tm=512, tn=1024, tk=D, extras=[(hpre, "mn")],
                        epilogue=dhpre_fn, n_cs=1, after=sent, name="bwd_mlp_down")
    d_w_up = _mm(x1, dhpre, M=D, N=F, K=T, ta=True, n_split=N_DEV, tm=1024, tn=F // N_DEV, tk=512, name="bwd_w_up")
    sent = comm.send_grad("mlp_w_up", d_w_up)
    dx1 = _mm(dhpre, w_up, M=T, N=D, K=F, tb=True, b_split=N_DEV, tm=512, tn=1024, tk=F // N_DEV,
              extras=[(ds2, "mn")], epilogue=lambda acc, dv: (ALPHA * dv + acc,), after=sent, name="bwd_mlp_up")

    def ln1_bwd_fn(s, dy, g):
        xhat, rstd = _ln_stats(s)
        return _ln_bwd(dy, g, xhat, rstd), dy * xhat, dy

    ds1, d_ln1_g, d_ln1_b = _ew(ln1_bwd_fn, [(s1, "tile", 0), (dx1, "tile", 0), (W["ln1_g"], "vec", 0)],
                                T=T, C=D, n_out=1, n_cs=2, tm=128, name="bwd_ln1")
    d_w_out = _mm(mix, ds1, M=D, N=D, K=T, ta=True, tm=1024, tn=1024, tk=512, name="bwd_w_out")
    sent = comm.send_grad("w_out", d_w_out)
    dmix = _mm(ds1, w_out, M=T, N=D, K=D, tb=True, tm=512, tn=1024, tk=D, after=sent, name="bwd_out_proj")

    def mix_bwd_fn(dm, ga, gb, ya, a, b):
        sa, sb, sv = _sig(ga), _sig(gb), _sig(b)
        yb = a * sv
        dyb = dm * sb
        return (dm * ya * (sa * (1.0 - sa)), dm * yb * (sb * (1.0 - sb)), dm * sa, dyb * sv,
                dyb * a * (sv * (1.0 - sv)))

    dg_a, dg_b, dy_a, dglu_a, dglu_b = _ew(
        mix_bwd_fn, [(dmix, "tile", 0), (z, "tile", ga_off // cwm), (z, "tile", gb_off // cwm), (y_a, "tile", 0),
                     (glu_a, "tile", 0), (glu_b, "tile", 0)], T=T, C=D, n_out=5, cw=cwm, name="bwd_mix")

    d_w_a_out = _mm(p, dy_a, M=C, N=D, K=T, ta=True, tm=1024, tn=1024, tk=512, name="bwd_w_a_out")
    sent = comm.send_grad("w_a_out", d_w_a_out)
    dp = _mm(dy_a, w_a_out, M=T, N=C, K=D, tb=True, tm=512, tn=1024, tk=D, after=sent, name="bwd_rg_out")
    dgate, dra, dia, dxc0, d_ba, d_bx, d_sp = _rg_scan_bwd(dp, h, z, r, ig, xc, sp, T=T, C=C, gate_off=C,
                                                          name="bwd_rg_scan")
    dxc = _bd([(dra, 0, wa_bd), (dia, 0, wx_bd)], T=T, J=jr, kb=kr, nb=kr, tw=True, extras=[(dxc0, "tile", 0)],
              epilogue=lambda acc, d0: (acc + d0,), name="bwd_gates")
    d_wa = _bd_unpack(_bdw(xc, 0, dra, 0, T=T, J=jr, kb=kr, nb=kr, name="bwd_w_rgate"), RG_Q)
    d_wx = _bd_unpack(_bdw(xc, 0, dia, 0, T=T, J=jr, kb=kr, nb=kr, name="bwd_w_igate"), RG_Q)
    dxr, conv_sums = _conv_bwd(dxc, z, W["conv_w"], T=T, C=C, name="bwd_conv")
    d_conv_w, d_conv_b = conv_sums[0:4], conv_sums[4:5]
    (d_lambda,) = sp_vjp(d_sp)

    d_glu_w = _mm(yg, dglu_a, M=S, N=D, K=T, ta=True, n_split=N_DEV, tm=1024, tn=D // N_DEV, tk=512, name="bwd_w_glu_w")
    d_glu_v = _mm(yg, dglu_b, M=S, N=D, K=T, ta=True, n_split=N_DEV, tm=1024, tn=D // N_DEV, tk=512, name="bwd_w_glu_v")
    sent = comm.send_grad("glu_w", d_glu_w, "glu_v", d_glu_v)
    dyg0 = _mm(dglu_a, w_glu_w, M=T, N=S, K=D, tb=True, b_split=N_DEV, tm=512, tn=S, tk=D // N_DEV, after=sent,
               name="bwd_glu_w")
    dy_s = _mm(dglu_b, w_glu_v, M=T, N=S, K=D, tb=True, b_split=N_DEV, tm=512, tn=S, tk=D // N_DEV,
               extras=[(dyg0, "mn"), (y_s, "mn")], epilogue=lambda acc, d0, yv: ((acc + d0) * _dgelu(yv),),
               name="bwd_glu_v")
    dh_re = _bd([(dy_s, 0, wc_re)], T=T, J=js, kb=ku, nb=kp, tw=True, name="bwd_s5_dh_re")
    dh_im = _bd([(dy_s, 0, wc_im_neg)], T=T, J=js, kb=ku, nb=kp, tw=True, name="bwd_s5_dh_im")
    d_wc_re = _bdw(h_re, 0, dy_s, 0, T=T, J=js, kb=kp, nb=ku, name="bwd_w_c_re")
    d_wc_im_neg = _bdw(h_im, 0, dy_s, 0, T=T, J=js, kb=kp, nb=ku, name="bwd_w_c_im")
    g_re, g_im, d_lbr, d_lbi = _cscan(dh_re, dh_im, lam_re, lam_im, T=T, C=G * P, reverse=True, h_re=h_re, h_im=h_im,
                                      name="bwd_s5_scan")

    def du_fn(acc, dyv, uv, d):
        return acc + dyv * d, dyv * uv

    du, d_ssm_d = _bd([(g_re, 0, wb_re), (g_im, 0, wb_im)], T=T, J=js, kb=kp, nb=ku, tw=True,
                      extras=[(dy_s, "tile", 0), (z, "tile", u_off // ku), (d_row, "vec", 0)], epilogue=du_fn, n_cs=1,
                      name="bwd_s5_du")
    d_wb_re = _bdw(z, u_off // ku, g_re, 0, T=T, J=js, kb=ku, nb=kp, name="bwd_w_b_re")
    d_wb_im = _bdw(z, u_off // ku, g_im, 0, T=T, J=js, kb=ku, nb=kp, name="bwd_w_b_im")
    d_bbr = jnp.swapaxes(_bd_unpack(d_wb_re, S5_Q), 1, 2)
    d_bbi = jnp.swapaxes(_bd_unpack(d_wb_im, S5_Q), 1, 2)
    d_a_re, d_a_im, d_log_dt, d_b_re, d_b_im = s5_vjp((d_lbr.reshape(G, P), d_lbi.reshape(G, P), d_bbr, d_bbi))
    d_c_re = jnp.swapaxes(_bd_unpack(d_wc_re, S5_Q), 1, 2)
    d_c_im = -jnp.swapaxes(_bd_unpack(d_wc_im_neg, S5_Q), 1, 2)

    dz = jnp.concatenate([dxr, dgate, du, dg_a, dg_b], axis=1)
    d_w_in = _mm(x, dz, M=D, N=n_in, K=T, ta=True, n_split=N_DEV, tm=1024, tn=n_in // N_DEV, tk=512, name="bwd_w_in")
    sent = comm.send_grad("w_in", d_w_in)
    grad_x = _mm(dz, w_in, M=T, N=D, K=n_in, tb=True, b_split=N_DEV, tm=512, tn=1024, tk=n_in // N_DEV,
                 extras=[(ds1, "mn")], epilogue=lambda acc, dv: (ALPHA * dv + acc,), after=sent, name="bwd_in_proj")

    grads = dict(
        conv_w=d_conv_w, conv_b=d_conv_b, rg_wa=d_wa, rg_ba=d_ba, rg_wx=d_wx, rg_bx=d_bx,
        rg_lambda=d_lambda, ssm_a_re=d_a_re, ssm_a_im=d_a_im, ssm_log_dt=d_log_dt,
        ssm_b_re=d_b_re, ssm_b_im=d_b_im, ssm_c_re=d_c_re, ssm_c_im=d_c_im, ssm_d=d_ssm_d.reshape(G, H),
        ln1_g=d_ln1_g, ln1_b=d_ln1_b, mlp_b_up=d_b_up, mlp_b_down=d_b_down, ln2_g=d_ln2_g, ln2_b=d_ln2_b)
    return jnp.sum(loss_cols), grad_x, grads


BIG = ("w_in", "w_a_out", "glu_w", "glu_v", "w_out", "mlp_w_up", "mlp_w_down")
COL_SHARDED = ("w_in", "glu_w", "glu_v", "mlp_w_up")
SMALL = ("conv_w", "conv_b", "rg_wa", "rg_ba", "rg_wx", "rg_bx", "rg_lambda", "ssm_a_re", "ssm_a_im", "ssm_log_dt",
         "ssm_b_re", "ssm_b_im", "ssm_c_re", "ssm_c_im", "ssm_d", "ln1_g", "ln1_b", "mlp_b_up", "mlp_b_down", "ln2_g",
         "ln2_b")
ORDER = ("w_in", "conv_w", "conv_b", "rg_wa", "rg_ba", "rg_wx", "rg_bx", "rg_lambda", "w_a_out", "ssm_a_re",
         "ssm_a_im", "ssm_log_dt", "ssm_b_re", "ssm_b_im", "ssm_c_re", "ssm_c_im", "ssm_d", "glu_w", "glu_v", "w_out",
         "ln1_g", "ln1_b", "mlp_w_up", "mlp_b_up", "mlp_w_down", "mlp_b_down", "ln2_g", "ln2_b")
TILE_ELEMS = SUBLANES * LANES


def _pack(arrs):
    pieces = []
    for a in arrs:
        flat = a.reshape(-1)
        flat = jnp.pad(flat, (0, (-flat.shape[0]) % TILE_ELEMS))
        pieces.append(flat.reshape(-1, LANES))
    rows = sum(p.shape[0] for p in pieces)
    pad_rows = (-rows) % (N_DEV * SUBLANES)
    if pad_rows:
        pieces.append(jnp.zeros((pad_rows, LANES), pieces[0].dtype))
    return jnp.concatenate(pieces, axis=0)


def _unpack(packed, shapes):
    out, row = [], 0
    for shp in shapes:
        n = math.prod(shp)
        rows = -(-n // TILE_ELEMS) * SUBLANES
        out.append(packed[row:row + rows].reshape(-1)[:n].reshape(shp))
        row += rows
    return out


class _Comm:
    def __init__(self, w):
        first = _all_gather([w["w_in"].astype(BF16), w["conv_w"]], name="gather_w_in")
        self._weights = {"w_in": first[0]}
        self.conv_w = jnp.swapaxes(first[1], 0, 1).reshape(w["conv_w"].shape[0], -1)
        later = [k for k in BIG if k != "w_in"]
        shards = [w[k].astype(BF16) for k in later]
        handles, self.gather_token = _exchange_start(shards, [_landing_zone(s) for s in shards], scatter=False,
                                                     name="gather_weights_start")
        self._gathers = dict(zip(later, handles))
        self._grads = {}

    def weight(self, k, after):
        if k not in self._weights:
            self._weights[k] = _exchange_wait(self._gathers.pop(k), after, scatter=False, name="gather_wait_" + k)
        gk = self._weights[k]
        return gk if k in COL_SHARDED else gk.reshape(-1, gk.shape[-1])

    def send_grad(self, *names_and_parts):
        names, parts = names_and_parts[0::2], names_and_parts[1::2]
        parts = [p if k in COL_SHARDED else p.reshape(N_DEV, p.shape[0] // N_DEV, p.shape[1])
                 for k, p in zip(names, parts)]
        me = _dev_index(*_mesh_pos())
        lands = [_landing_zone(lax.dynamic_index_in_dim(p, me, 0, keepdims=False)) for p in parts]
        handles, token = _exchange_start(parts, lands, scatter=True, name="grad_start_" + names[0])
        self._grads.update(zip(names, handles))
        return token

    def received_grad(self, k, after):
        return _exchange_wait(self._grads.pop(k), after, scatter=True, name="grad_wait_" + k)


def _step(x, tgt, w, m, v):
    dev = _dev_index(*_mesh_pos())

    comm = _Comm(w)
    small = dict(w)
    small["conv_w"] = comm.conv_w
    for k in ("conv_b", "rg_ba", "rg_bx", "rg_lambda", "ln1_g", "ln1_b", "mlp_b_up", "mlp_b_down", "ln2_g", "ln2_b"):
        small[k] = w[k].reshape(1, -1)

    loss_part, grad_x, grads = _local_step(x, tgt, small, comm)

    out_g, out_d, out_m, out_v = {}, {}, {}, {}
    for k in BIG:
        rk = comm.received_grad(k, grad_x)
        out_g[k], out_d[k], out_m[k], out_v[k] = _adamw(rk, w[k], m[k], v[k], tr=128, name="adamw_" + k)

    small_shapes = [grads[k].shape for k in SMALL]
    (small_recv,) = _exchange_blocks([_pack([grads[k] for k in SMALL]).reshape(N_DEV, -1, LANES)],
                                     name="exchange_small_grads")
    small_block = _sum_slots(small_recv, tr=512, name="sum_small_grads")
    (small_all,) = _all_gather([small_block], name="gather_small_grads")
    g_small = dict(zip(SMALL, _unpack(small_all.reshape(-1, LANES), small_shapes)))
    cw_cols = w["conv_w"].shape[1]
    g_small["conv_w"] = lax.dynamic_slice_in_dim(g_small["conv_w"], dev * cw_cols, cw_cols, axis=1)
    shapes = [w[k].shape for k in SMALL]
    g_pack, w_pack, m_pack, v_pack = [_pack([src[k] for k in SMALL]) for src in (g_small, w, m, v)]
    res = _adamw(g_pack[None], w_pack, m_pack, v_pack, tr=1024, name="adamw_small")
    for dst, packed in zip((out_g, out_d, out_m, out_v), res):
        dst.update(zip(SMALL, _unpack(packed, shapes)))

    loss = lax.psum(loss_part, ("x", "y", "c"))
    return loss, grad_x, out_g, out_d, out_m, out_v


def kernel(x, w_in, conv_w, conv_b, rg_wa, rg_ba, rg_wx, rg_bx, rg_lambda, w_a_out, ssm_a_re, ssm_a_im, ssm_log_dt, ssm_b_re, ssm_b_im, ssm_c_re, ssm_c_im, ssm_d, glu_w, glu_v, w_out, ln1_g, ln1_b, mlp_w_up, mlp_b_up, mlp_w_down, mlp_b_down, ln2_g, ln2_b, loss_target, m_w_in, m_conv_w, m_conv_b, m_rg_wa, m_rg_ba, m_rg_wx, m_rg_bx, m_rg_lambda, m_w_a_out, m_ssm_a_re, m_ssm_a_im, m_ssm_log_dt, m_ssm_b_re, m_ssm_b_im, m_ssm_c_re, m_ssm_c_im, m_ssm_d, m_glu_w, m_glu_v, m_w_out, m_ln1_g, m_ln1_b, m_mlp_w_up, m_mlp_b_up, m_mlp_w_down, m_mlp_b_down, m_ln2_g, m_ln2_b, v_w_in, v_conv_w, v_conv_b, v_rg_wa, v_rg_ba, v_rg_wx, v_rg_bx, v_rg_lambda, v_w_a_out, v_ssm_a_re, v_ssm_a_im, v_ssm_log_dt, v_ssm_b_re, v_ssm_b_im, v_ssm_c_re, v_ssm_c_im, v_ssm_d, v_glu_w, v_glu_v, v_w_out, v_ln1_g, v_ln1_b, v_mlp_w_up, v_mlp_b_up, v_mlp_w_down, v_mlp_b_down, v_ln2_g, v_ln2_b):
    args = locals()
    w = {k: args[k][0] for k in ORDER}
    m = {k: args["m_" + k][0] for k in ORDER}
    v = {k: args["v_" + k][0] for k in ORDER}
    loss, grad_x, out_g, out_d, out_m, out_v = _step(x[0], loss_target[0], w, m, v)
    outs = [loss, grad_x[None]]
    for group in (out_g, out_d, out_m, out_v):
        outs += [group[k].reshape(args[k].shape) for k in ORDER]
    return tuple(outs)
```

```python
import functools
import math

import jax
import jax.numpy as jnp
from jax import lax
from jax.experimental import pallas as pl
from jax.experimental.pallas import tpu as pltpu

F32 = jnp.float32
BF16 = jnp.bfloat16
MESH = pl.DeviceIdType.MESH
N_DEV = 8
SUBLANES = 8
LANES = 128
VMEM_BYTES_V7X = 64 * 2 ** 20
VMEM_CAP = VMEM_BYTES_V7X - 8 * 2 ** 20

ALPHA = 2.0 ** 0.25
LN_EPS = 1e-5
RG_C = 8.0
ADAM_LR, ADAM_B1, ADAM_B2, ADAM_EPS, ADAM_WD, ADAM_STEP = 0.001, 0.9, 0.999, 1e-08, 0.01, 10
GELU_C = math.sqrt(2.0 / math.pi)
GELU_K = 0.044715

ANY = pl.BlockSpec(memory_space=pl.ANY)


def _params(sem, vmem_bytes):
    limit = int(min(max(2 * vmem_bytes, 16 * 2 ** 20), VMEM_CAP))
    return pltpu.CompilerParams(dimension_semantics=sem, vmem_limit_bytes=limit)


def _sig(x):
    return 1.0 / (1.0 + jnp.exp(-x))


def _gelu(x):
    return 0.5 * x * (1.0 + jnp.tanh(GELU_C * (x + GELU_K * x * x * x)))


def _dgelu(x):
    th = jnp.tanh(GELU_C * (x + GELU_K * x * x * x))
    return 0.5 * (1.0 + th) + 0.5 * x * (1.0 - th * th) * (GELU_C * (1.0 + 3.0 * GELU_K * x * x))


def _expm1(x):
    p = x * (1.0 + x * (1 / 2 + x * (1 / 6 + x * (1 / 24 + x * (1 / 120 + x * (1 / 720 + x * (1 / 5040)))))))
    return jnp.where(jnp.abs(x) < 0.25, p, jnp.exp(x) - 1.0)


def _accumulate(ref, val, first):
    @pl.when(first)
    def _():
        ref[...] = val

    @pl.when(jnp.logical_not(first))
    def _():
        ref[...] += val


def _rows8(cw):
    return lax.broadcasted_iota(jnp.int32, (SUBLANES, cw), 0)


def _shift_down(cur, prev, s, rows):
    return jnp.where(rows < s, pltpu.roll(prev, s, 0), pltpu.roll(cur, s, 0))


def _shift_up(cur, nxt, s, rows):
    return jnp.where(rows < SUBLANES - s, pltpu.roll(cur, SUBLANES - s, 0), pltpu.roll(nxt, SUBLANES - s, 0))


def _mm(a, b, *, M, N, K, ta=False, tb=False, b_split=1, n_split=1, a_fn=None, extras=(), epilogue=None,
        n_out=1, n_cs=0, out_dtypes=None, tm=512, tn=512, tk=512, after=None, name):
    tm, tn, tk = min(tm, M), min(tn, N), min(tk, K)
    assert M % tm == 0 and N % tn == 0 and K % tk == 0, (name, M, N, K, tm, tn, tk)
    nk = K // tk
    grid = (N // tn, M // tm, nk)
    a_spec = pl.BlockSpec((tk, tm), lambda j, i, k: (k, i)) if ta else pl.BlockSpec((tm, tk), lambda j, i, k: (i, k))
    if b_split == 1:
        b_spec = pl.BlockSpec((tn, tk), lambda j, i, k: (j, k)) if tb else pl.BlockSpec((tk, tn), lambda j, i, k: (k, j))
    elif tb:
        kb = (K // b_split) // tk
        assert kb * tk * b_split == K, name
        b_spec = pl.BlockSpec((None, tn, tk), lambda j, i, k: (k // kb, j, k % kb))
    else:
        nb = (N // b_split) // tn
        assert nb * tn * b_split == N, name
        b_spec = pl.BlockSpec((None, tk, tn), lambda j, i, k: (j // nb, k, j % nb))
    in_specs = [a_spec, b_spec]
    for arr, kind in extras:
        in_specs.append(pl.BlockSpec((tm, tn), lambda j, i, k: (i, j)) if kind == "mn"
                        else pl.BlockSpec((1, tn), lambda j, i, k: (0, j)))
    out_dtypes = (F32,) * n_out if out_dtypes is None else out_dtypes
    if n_split == 1:
        out_shape = [jax.ShapeDtypeStruct((M, N), dt) for dt in out_dtypes]
        out_specs = [pl.BlockSpec((tm, tn), lambda j, i, k: (i, j)) for _ in range(n_out)]
    else:
        assert n_out == 1
        nbo = (N // n_split) // tn
        assert nbo * tn * n_split == N, name
        out_shape = [jax.ShapeDtypeStruct((n_split, M, N // n_split), out_dtypes[0])]
        out_specs = [pl.BlockSpec((None, tm, tn), lambda j, i, k: (j // nbo, i, j % nbo))]
    out_shape += [jax.ShapeDtypeStruct((1, N), F32) for _ in range(n_cs)]
    out_specs += [pl.BlockSpec((1, tn), lambda j, i, k: (0, j)) for _ in range(n_cs)]
    ne = len(extras)
    dims = (((0 if ta else 1,), (1 if tb else 0,)), ((), ()))

    n_after = 0 if after is None else 1
    in_specs += [ANY] * n_after

    def body(*refs):
        a_ref, b_ref = refs[0], refs[1]
        ex_refs = refs[2:2 + ne]
        first_out = 2 + ne + n_after
        out_refs = refs[first_out:first_out + n_out]
        cs_refs = refs[first_out + n_out:first_out + n_out + n_cs]
        i, k = pl.program_id(1), pl.program_id(2)

        def product():
            av = a_ref[...]
            if a_fn is not None:
                av = a_fn(av.astype(F32))
            return lax.dot_general(av.astype(BF16), b_ref[...].astype(BF16), dims, preferred_element_type=F32)

        def finish(acc):
            res = (acc,) if epilogue is None else epilogue(acc, *[r[...] for r in ex_refs])
            for r, o in zip(out_refs, res[:n_out]):
                r[...] = o.astype(r.dtype)
            for r, cval in zip(cs_refs, res[n_out:]):
                _accumulate(r, jnp.sum(cval, axis=0, keepdims=True), i == 0)

        if nk == 1:
            finish(product())
            return
        acc_ref = refs[-1]

        @pl.when(k == 0)
        def _():
            acc_ref[...] = jnp.zeros_like(acc_ref)

        acc_ref[...] += product()

        @pl.when(k == nk - 1)
        def _():
            finish(acc_ref[...])

    vmem = 2 * tm * tk * a.dtype.itemsize + 2 * tk * tn * b.dtype.itemsize + (1 + 2 * n_out + 2 * ne + 2) * tm * tn * 4
    outs = pl.pallas_call(
        body, name=name, grid=grid, in_specs=in_specs, out_specs=out_specs, out_shape=out_shape,
        scratch_shapes=[pltpu.VMEM((tm, tn), F32)] if nk > 1 else [],
        compiler_params=_params(("parallel", "arbitrary", "arbitrary"), vmem),
    )(a, b, *[arr for arr, _ in extras], *([after] if n_after else []))
    return outs[0] if len(outs) == 1 else outs


def _bd(pairs, *, T, J, kb, nb, tw=False, extras=(), epilogue=None, n_out=1, n_cs=0, out_dtypes=None, tm=512, name):
    assert T % tm == 0
    grid = (J, T // tm)
    npair, ne = len(pairs), len(extras)
    in_specs, args = [], []
    for arr, off, w in pairs:
        in_specs.append(pl.BlockSpec((tm, kb), lambda j, i, off=off: (i, off + j)))
        in_specs.append(pl.BlockSpec((None,) + tuple(w.shape[1:]), lambda j, i: (j, 0, 0)))
        args += [arr, w]
    for arr, kind, off in extras:
        in_specs.append(pl.BlockSpec((tm, nb), lambda j, i, off=off: (i, off + j)) if kind == "tile"
                        else pl.BlockSpec((1, nb), lambda j, i, off=off: (0, off + j)))
        args.append(arr)
    out_dtypes = (F32,) * n_out if out_dtypes is None else out_dtypes
    out_shape = [jax.ShapeDtypeStruct((T, J * nb), dt) for dt in out_dtypes]
    out_specs = [pl.BlockSpec((tm, nb), lambda j, i: (i, j)) for _ in range(n_out)]
    out_shape += [jax.ShapeDtypeStruct((1, J * nb), F32) for _ in range(n_cs)]
    out_specs += [pl.BlockSpec((1, nb), lambda j, i: (0, j)) for _ in range(n_cs)]
    dims = (((1,), (1 if tw else 0,)), ((), ()))

    def body(*refs):
        ex_refs = refs[2 * npair:2 * npair + ne]
        out_refs = refs[2 * npair + ne:2 * npair + ne + n_out]
        cs_refs = refs[2 * npair + ne + n_out:]
        i = pl.program_id(1)
        acc = None
        for p in range(npair):
            d = lax.dot_general(refs[2 * p][...].astype(BF16), refs[2 * p + 1][...].astype(BF16), dims,
                                preferred_element_type=F32)
            acc = d if acc is None else acc + d
        res = (acc,) if epilogue is None else epilogue(acc, *[r[...] for r in ex_refs])
        for r, o in zip(out_refs, res[:n_out]):
            r[...] = o.astype(r.dtype)
        for r, cval in zip(cs_refs, res[n_out:]):
            _accumulate(r, jnp.sum(cval, axis=0, keepdims=True), i == 0)

    vmem = (2 * npair * tm * kb + 2 * npair * kb * nb + (2 * n_out + 2 * ne + 3) * tm * nb) * 4
    outs = pl.pallas_call(
        body, name=name, grid=grid, in_specs=in_specs, out_specs=out_specs, out_shape=out_shape,
        compiler_params=_params(("parallel", "arbitrary"), vmem),
    )(*args)
    return outs[0] if len(outs) == 1 else outs


def _bdw(a, a_off, b, b_off, *, T, J, kb, nb, tm=512, name):
    assert T % tm == 0

    def body(a_ref, b_ref, o_ref):
        i = pl.program_id(1)
        d = lax.dot_general(a_ref[...].astype(BF16), b_ref[...].astype(BF16), (((0,), (0,)), ((), ())),
                            preferred_element_type=F32)
        _accumulate(o_ref, d, i == 0)

    return pl.pallas_call(
        body, name=name, grid=(J, T // tm),
        in_specs=[pl.BlockSpec((tm, kb), lambda j, i: (i, a_off + j)), pl.BlockSpec((tm, nb), lambda j, i: (i, b_off + j))],
        out_specs=pl.BlockSpec((None, kb, nb), lambda j, i: (j, 0, 0)),
        out_shape=jax.ShapeDtypeStruct((J, kb, nb), F32),
        compiler_params=_params(("parallel", "arbitrary"), (2 * tm * (kb + nb) + 3 * kb * nb) * 4),
    )(a, b)


def _bd_pack(w, q):
    g, a, b = w.shape
    eye = jnp.eye(q, dtype=w.dtype)
    return jnp.einsum("jqab,qr->jqarb", w.reshape(g // q, q, a, b), eye).reshape(g // q, q * a, q * b)


def _bd_unpack(wp, q):
    j, qa, qb = wp.shape
    a, b = qa // q, qb // q
    w5 = wp.reshape(j, q, a, q, b)
    return jnp.stack([w5[:, r, :, r, :] for r in range(q)], axis=1).reshape(j * q, a, b)


def _ew(fn, ins, *, T, C, n_out, n_cs=0, out_dtypes=None, tm=256, cw=None, name):
    cw = C if cw is None else cw
    assert T % tm == 0 and C % cw == 0
    grid = (C // cw, T // tm)
    in_specs = []
    for arr, kind, off in ins:
        in_specs.append(pl.BlockSpec((tm, cw), lambda j, i, off=off: (i, off + j)) if kind == "tile"
                        else pl.BlockSpec((arr.shape[0], cw), lambda j, i, off=off: (0, off + j)))
    out_dtypes = (F32,) * n_out if out_dtypes is None else out_dtypes
    out_shape = [jax.ShapeDtypeStruct((T, C), dt) for dt in out_dtypes]
    out_specs = [pl.BlockSpec((tm, cw), lambda j, i: (i, j)) for _ in range(n_out)]
    out_shape += [jax.ShapeDtypeStruct((1, C), F32) for _ in range(n_cs)]
    out_specs += [pl.BlockSpec((1, cw), lambda j, i: (0, j)) for _ in range(n_cs)]
    nin = len(ins)

    def body(*refs):
        i = pl.program_id(1)
        res = fn(*[r[...] for r in refs[:nin]])
        for r, o in zip(refs[nin:nin + n_out], res[:n_out]):
            r[...] = o.astype(r.dtype)
        for r, cval in zip(refs[nin + n_out:], res[n_out:]):
            _accumulate(r, jnp.sum(cval, axis=0, keepdims=True), i == 0)

    vmem = (2 * nin + 2 * n_out + 6) * tm * cw * 4
    outs = pl.pallas_call(
        body, name=name, grid=grid, in_specs=in_specs, out_specs=out_specs, out_shape=out_shape,
        compiler_params=_params(("parallel", "arbitrary"), vmem),
    )(*[arr for arr, _, _ in ins])
    return outs[0] if len(outs) == 1 else outs


def _ln_stats(s):
    mu = jnp.mean(s, axis=-1, keepdims=True)
    d = s - mu
    var = jnp.mean(d * d, axis=-1, keepdims=True)
    rstd = lax.rsqrt(var + LN_EPS)
    return d * rstd, rstd


def _ln_bwd(dy, g, xhat, rstd):
    dxh = dy * g
    m1 = jnp.mean(dxh, axis=-1, keepdims=True)
    m2 = jnp.mean(dxh * xhat, axis=-1, keepdims=True)
    return rstd * (dxh - m1 - xhat * m2)


def _conv_fwd(z, conv_w, conv_b, *, T, C, tm=512, cw=1024, name):
    ng, hb = tm // SUBLANES, tm // SUBLANES

    def body(x_ref, halo_ref, w_ref, b_ref, o_ref):
        it = pl.program_id(1)
        rows = _rows8(cw)
        halo = jnp.where(it == 0, 0.0, halo_ref[...])
        w = w_ref[...]
        bias = b_ref[...]

        def group(g, carry):
            off = pl.multiple_of(g * SUBLANES, SUBLANES)
            cur = x_ref[pl.ds(off, SUBLANES), :]
            prev = x_ref[pl.ds(pl.multiple_of(jnp.maximum(off - SUBLANES, 0), SUBLANES), SUBLANES), :]
            prev = jnp.where(g == 0, halo, prev)
            acc = cur * w[3:4] + bias
            for s in (1, 2, 3):
                acc = acc + _shift_down(cur, prev, s, rows) * w[3 - s:4 - s]
            o_ref[pl.ds(off, SUBLANES), :] = acc
            return carry

        lax.fori_loop(0, ng, group, 0)

    return pl.pallas_call(
        body, name=name, grid=(C // cw, T // tm),
        in_specs=[pl.BlockSpec((tm, cw), lambda j, i: (i, j)),
                  pl.BlockSpec((SUBLANES, cw), lambda j, i: (jnp.maximum(i * hb - 1, 0), j)),
                  pl.BlockSpec((4, cw), lambda j, i: (0, j)), pl.BlockSpec((1, cw), lambda j, i: (0, j))],
        out_specs=pl.BlockSpec((tm, cw), lambda j, i: (i, j)),
        out_shape=jax.ShapeDtypeStruct((T, C), F32),
        compiler_params=_params(("parallel", "arbitrary"), 5 * tm * cw * 4),
    )(z, z, conv_w, conv_b)


def _conv_bwd(dxc, z, conv_w, *, T, C, tm=512, cw=512, name):
    ng, hb, last = tm // SUBLANES, tm // SUBLANES, T // SUBLANES - 1
    nt = T // tm

    def body(d_ref, dn_ref, x_ref, xp_ref, w_ref, o_ref, sums_ref):
        it = pl.program_id(1)
        rows = _rows8(cw)
        dnext = jnp.where(it == nt - 1, 0.0, dn_ref[...])
        xprev = jnp.where(it == 0, 0.0, xp_ref[...])
        w = w_ref[...]

        def group(g, accs):
            off = pl.multiple_of(g * SUBLANES, SUBLANES)
            dcur = d_ref[pl.ds(off, SUBLANES), :]
            dnx = d_ref[pl.ds(pl.multiple_of(jnp.minimum(off + SUBLANES, tm - SUBLANES), SUBLANES), SUBLANES), :]
            dnx = jnp.where(g == ng - 1, dnext, dnx)
            xcur = x_ref[pl.ds(off, SUBLANES), :]
            xpv = x_ref[pl.ds(pl.multiple_of(jnp.maximum(off - SUBLANES, 0), SUBLANES), SUBLANES), :]
            xpv = jnp.where(g == 0, xprev, xpv)
            acc = dcur * w[3:4]
            for s in (1, 2, 3):
                acc = acc + _shift_up(dcur, dnx, s, rows) * w[3 - s:4 - s]
            o_ref[pl.ds(off, SUBLANES), :] = acc
            a0, a1, a2, a3, ab = accs
            a0 = a0 + dcur * _shift_down(xcur, xpv, 3, rows)
            a1 = a1 + dcur * _shift_down(xcur, xpv, 2, rows)
            a2 = a2 + dcur * _shift_down(xcur, xpv, 1, rows)
            a3 = a3 + dcur * xcur
            return a0, a1, a2, a3, ab + dcur

        zero = jnp.zeros((SUBLANES, cw), F32)
        accs = lax.fori_loop(0, ng, group, (zero,) * 5)
        sums = jnp.zeros((SUBLANES, cw), F32)
        for k, a in enumerate(accs):
            sums = jnp.where(rows == k, jnp.sum(a, axis=0, keepdims=True), sums)
        _accumulate(sums_ref, sums, it == 0)

    tile = pl.BlockSpec((tm, cw), lambda j, i: (i, j))
    return pl.pallas_call(
        body, name=name, grid=(C // cw, nt),
        in_specs=[tile, pl.BlockSpec((SUBLANES, cw), lambda j, i: (jnp.minimum((i + 1) * hb, last), j)),
                  tile, pl.BlockSpec((SUBLANES, cw), lambda j, i: (jnp.maximum(i * hb - 1, 0), j)),
                  pl.BlockSpec((4, cw), lambda j, i: (0, j))],
        out_specs=[tile, pl.BlockSpec((SUBLANES, cw), lambda j, i: (0, j))],
        out_shape=[jax.ShapeDtypeStruct((T, C), F32), jax.ShapeDtypeStruct((SUBLANES, C), F32)],
        compiler_params=_params(("parallel", "arbitrary"), 7 * tm * cw * 4),
    )(dxc, dxc, z, z, conv_w)


def _rg_coeffs(r, ig, xc, sp):
    la = (-RG_C) * r * sp
    a = jnp.exp(la)
    m = jnp.sqrt(-_expm1(2.0 * la))
    return a, m, m * (ig * xc)


def _rg_scan_fwd(z, r, ig, xc, sp, *, T, C, gate_off, tm=512, cw=256, name):
    rows16 = 2 * SUBLANES
    nq = tm // rows16

    def body(gate_ref, r_ref, i_ref, xc_ref, sp_ref, h_ref, p_ref, carry_ref):
        it = pl.program_id(1)

        @pl.when(it == 0)
        def _():
            carry_ref[...] = jnp.zeros_like(carry_ref)

        rows = _rows8(cw)
        sp_row = sp_ref[...]

        def pair(q, carry):
            base = pl.multiple_of(q * rows16, rows16)
            halves = []
            for half in range(2):
                sl = pl.ds(pl.multiple_of(base + half * SUBLANES, SUBLANES), SUBLANES)
                a, _, b = _rg_coeffs(r_ref[sl, :], i_ref[sl, :], xc_ref[sl, :], sp_row)
                for s in (1, 2, 4):
                    keep = rows >= s
                    sa = jnp.where(keep, pltpu.roll(a, s, 0), 1.0)
                    sb = jnp.where(keep, pltpu.roll(b, s, 0), 0.0)
                    b = b + a * sb
                    a = a * sa
                h = b + a * carry
                h_ref[sl, :] = h
                halves.append(h * _gelu(gate_ref[sl, :]))
                carry = h[SUBLANES - 1:SUBLANES, :]
            p_ref[pl.ds(base, rows16), :] = jnp.concatenate(halves, axis=0).astype(p_ref.dtype)
            return carry

        last = lax.fori_loop(0, nq, pair, carry_ref[0:1, :], unroll=2)
        carry_ref[...] = jnp.broadcast_to(last, carry_ref.shape)

    tile = pl.BlockSpec((tm, cw), lambda j, i: (i, j))
    gate_blk = gate_off // cw
    return pl.pallas_call(
        body, name=name, grid=(C // cw, T // tm),
        in_specs=[pl.BlockSpec((tm, cw), lambda j, i: (i, gate_blk + j)), tile, tile, tile,
                  pl.BlockSpec((1, cw), lambda j, i: (0, j))],
        out_specs=[tile, tile],
        out_shape=[jax.ShapeDtypeStruct((T, C), F32), jax.ShapeDtypeStruct((T, C), BF16)],
        scratch_shapes=[pltpu.VMEM((SUBLANES, cw), F32)],
        compiler_params=_params(("parallel", "arbitrary"), 12 * tm * cw * 4),
    )(z, r, ig, xc, sp)


def _rg_scan_bwd(dp, h, z, r, ig, xc, sp, *, T, C, gate_off, tm=512, cw=256, name):
    ng, hb, nt = tm // SUBLANES, tm // SUBLANES, T // tm

    def body(dp_ref, h_ref, hp_ref, gate_ref, r_ref, i_ref, xc_ref, sp_ref,
             dgate_ref, dra_ref, dia_ref, dxc_ref, cra_ref, cia_ref, csp_ref, cg_ref, ca_ref):
        step = pl.program_id(1)

        @pl.when(step == 0)
        def _():
            cg_ref[...] = jnp.zeros_like(cg_ref)
            ca_ref[...] = jnp.zeros_like(ca_ref)

        rows = _rows8(cw)
        sp_row = sp_ref[...]
        hhalo = jnp.where(step == nt - 1, 0.0, hp_ref[...])

        def group(gi, carry):
            g_next, a_next, s_ra, s_ia, s_sp = carry
            g = ng - 1 - gi
            off = pl.multiple_of(g * SUBLANES, SUBLANES)
            sl = pl.ds(off, SUBLANES)
            rr, ii, xx = r_ref[sl, :], i_ref[sl, :], xc_ref[sl, :]
            a, m, _ = _rg_coeffs(rr, ii, xx, sp_row)
            hh = h_ref[sl, :]
            hpv = h_ref[pl.ds(pl.multiple_of(jnp.maximum(off - SUBLANES, 0), SUBLANES), SUBLANES), :]
            hpv = jnp.where(g == 0, hhalo, hpv)
            hprev = _shift_down(hh, hpv, 1, rows)
            gate = gate_ref[sl, :]
            dpv = dp_ref[sl, :]
            d = dpv * _gelu(gate)
            dgate_ref[sl, :] = dpv * hh * _dgelu(gate)
            c = jnp.where(rows < SUBLANES - 1, pltpu.roll(a, SUBLANES - 1, 0), a_next)
            for s in (1, 2, 4):
                keep = rows < SUBLANES - s
                sc = jnp.where(keep, pltpu.roll(c, SUBLANES - s, 0), 1.0)
                sd = jnp.where(keep, pltpu.roll(d, SUBLANES - s, 0), 0.0)
                d = d + c * sd
                c = c * sc
            gg = d + c * g_next
            da = gg * hprev
            dm = gg * (ii * xx)
            di = gg * (m * xx)
            dxc_ref[sl, :] = gg * (m * ii)
            dla = da * a - dm * (a * a / m)
            dra = dla * ((-RG_C) * sp_row) * (rr * (1.0 - rr))
            dia = di * (ii * (1.0 - ii))
            dra_ref[sl, :] = dra
            dia_ref[sl, :] = dia
            return (gg[0:1, :], a[0:1, :], s_ra + dra, s_ia + dia, s_sp + dla * ((-RG_C) * rr))

        zero = jnp.zeros((SUBLANES, cw), F32)
        g_first, a_first, s_ra, s_ia, s_sp = lax.fori_loop(
            0, ng, group, (cg_ref[0:1, :], ca_ref[0:1, :], zero, zero, zero), unroll=2)
        cg_ref[...] = jnp.broadcast_to(g_first, cg_ref.shape)
        ca_ref[...] = jnp.broadcast_to(a_first, ca_ref.shape)
        for ref, acc in ((cra_ref, s_ra), (cia_ref, s_ia), (csp_ref, s_sp)):
            _accumulate(ref, jnp.sum(acc, axis=0, keepdims=True), step == 0)

    tile = pl.BlockSpec((tm, cw), lambda j, i: (nt - 1 - i, j))
    vec = pl.BlockSpec((1, cw), lambda j, i: (0, j))
    gate_blk = gate_off // cw
    return pl.pallas_call(
        body, name=name, grid=(C // cw, nt),
        in_specs=[tile, tile, pl.BlockSpec((SUBLANES, cw), lambda j, i: (jnp.maximum((nt - 1 - i) * hb - 1, 0), j)),
                  pl.BlockSpec((tm, cw), lambda j, i: (nt - 1 - i, gate_blk + j)), tile, tile, tile, vec],
        out_specs=[tile, tile, tile, tile, vec, vec, vec],
        out_shape=[jax.ShapeDtypeStruct((T, C), F32)] * 4 + [jax.ShapeDtypeStruct((1, C), F32)] * 3,
        scratch_shapes=[pltpu.VMEM((SUBLANES, cw), F32), pltpu.VMEM((SUBLANES, cw), F32)],
        compiler_params=_params(("parallel", "arbitrary"), 24 * tm * cw * 4),
    )(dp, h, h, z, r, ig, xc, sp)


def _cscan_tables(lr, li, reverse):
    lam = (lr.reshape(-1), -li.reshape(-1) if reverse else li.reshape(-1))

    def mul(p, q):
        return p[0] * q[0] - p[1] * q[1], p[0] * q[1] + p[1] * q[0]

    pows = [lam]
    for _ in range(SUBLANES - 1):
        pows.append(mul(pows[-1], lam))
    zero = jnp.zeros_like(lam[0])
    tab = jnp.stack([pows[0][0], pows[0][1], pows[1][0], pows[1][1], pows[3][0], pows[3][1], zero, zero])
    if reverse:
        pows = pows[::-1]
    return tab, jnp.stack([p[0] for p in pows]), jnp.stack([p[1] for p in pows])


def _cscan(b_re, b_im, lr, li, *, T, C, reverse=False, h_re=None, h_im=None, tm=512, cw=512, name):
    ng, hb, nt = tm // SUBLANES, tm // SUBLANES, T // tm
    with_grad = h_re is not None
    tab, pw_re, pw_im = _cscan_tables(lr, li, reverse)

    def body(*refs):
        tab_ref, pwr_ref, pwi_ref, br_ref, bi_ref = refs[:5]
        if with_grad:
            hr_ref, hrp_ref, hi_ref, hip_ref, or_ref, oi_ref, dlr_ref, dli_ref, cr_ref, ci_ref = refs[5:]
        else:
            or_ref, oi_ref, cr_ref, ci_ref = refs[5:]
        step = pl.program_id(1)

        @pl.when(step == 0)
        def _():
            cr_ref[...] = jnp.zeros_like(cr_ref)
            ci_ref[...] = jnp.zeros_like(ci_ref)

        rows = _rows8(cw)
        lam = [(tab_ref[2 * k:2 * k + 1, :], tab_ref[2 * k + 1:2 * k + 2, :]) for k in range(3)]
        pwr, pwi = pwr_ref[...], pwi_ref[...]
        if with_grad:
            hr_halo = jnp.where(step == nt - 1, 0.0, hrp_ref[...])
            hi_halo = jnp.where(step == nt - 1, 0.0, hip_ref[...])

        def group(gi, carry):
            g = ng - 1 - gi if reverse else gi
            off = pl.multiple_of(g * SUBLANES, SUBLANES)
            sl = pl.ds(off, SUBLANES)
            xr, xi = br_ref[sl, :], bi_ref[sl, :]
            for k, s in enumerate((1, 2, 4)):
                shift = SUBLANES - s if reverse else s
                keep = rows < SUBLANES - s if reverse else rows >= s
                sr = jnp.where(keep, pltpu.roll(xr, shift, 0), 0.0)
                si = jnp.where(keep, pltpu.roll(xi, shift, 0), 0.0)
                l_re, l_im = lam[k]
                xr, xi = xr + (l_re * sr - l_im * si), xi + (l_re * si + l_im * sr)
            cr, ci = carry[0], carry[1]
            xr, xi = xr + (pwr * cr - pwi * ci), xi + (pwr * ci + pwi * cr)
            or_ref[sl, :] = xr
            oi_ref[sl, :] = xi
            edge = slice(0, 1) if reverse else slice(SUBLANES - 1, SUBLANES)
            out = (xr[edge, :], xi[edge, :])
            if with_grad:
                poff = pl.multiple_of(jnp.maximum(off - SUBLANES, 0), SUBLANES)
                hrp = jnp.where(g == 0, hr_halo, hr_ref[pl.ds(poff, SUBLANES), :])
                hip = jnp.where(g == 0, hi_halo, hi_ref[pl.ds(poff, SUBLANES), :])
                hr1 = _shift_down(hr_ref[sl, :], hrp, 1, rows)
                hi1 = _shift_down(hi_ref[sl, :], hip, 1, rows)
                out += (carry[2] + (xr * hr1 + xi * hi1), carry[3] + (xi * hr1 - xr * hi1))
            return out

        init = (cr_ref[0:1, :], ci_ref[0:1, :])
        if with_grad:
            init += (jnp.zeros((SUBLANES, cw), F32),) * 2
        res = lax.fori_loop(0, ng, group, init)
        cr_ref[...] = jnp.broadcast_to(res[0], cr_ref.shape)
        ci_ref[...] = jnp.broadcast_to(res[1], ci_ref.shape)
        if with_grad:
            for ref, acc in ((dlr_ref, res[2]), (dli_ref, res[3])):
                _accumulate(ref, jnp.sum(acc, axis=0, keepdims=True), step == 0)

    def tix(i):
        return nt - 1 - i if reverse else i

    tile = pl.BlockSpec((tm, cw), lambda j, i: (tix(i), j))
    small = pl.BlockSpec((SUBLANES, cw), lambda j, i: (0, j))
    halo = pl.BlockSpec((SUBLANES, cw), lambda j, i: (jnp.maximum(tix(i) * hb - 1, 0), j))
    vec = pl.BlockSpec((1, cw), lambda j, i: (0, j))
    in_specs = [small, small, small, tile, tile]
    args = [tab, pw_re, pw_im, b_re, b_im]
    out_specs = [tile, tile]
    out_shape = [jax.ShapeDtypeStruct((T, C), F32)] * 2
    if with_grad:
        assert reverse
        in_specs += [tile, halo, tile, halo]
        args += [h_re, h_re, h_im, h_im]
        out_specs += [vec, vec]
        out_shape += [jax.ShapeDtypeStruct((1, C), F32)] * 2
    return pl.pallas_call(
        body, name=name, grid=(C // cw, nt), in_specs=in_specs, out_specs=out_specs, out_shape=out_shape,
        scratch_shapes=[pltpu.VMEM((SUBLANES, cw), F32), pltpu.VMEM((SUBLANES, cw), F32)],
        compiler_params=_params(("parallel", "arbitrary"), (8 + (4 if with_grad else 0)) * tm * cw * 4),
    )(*args)


def _mesh_pos():
    return lax.axis_index("x"), lax.axis_index("y"), lax.axis_index("c")


def _dev_index(px, py, pc):
    return 4 * px + 2 * py + pc


def _all_gather(shards, name):
    n = len(shards)

    def body(*refs):
        ins, outs = refs[:n], refs[n:2 * n]
        send_sems, recv_sems, local_sems = refs[2 * n:]
        x, y, c = _mesh_pos()
        me, sibling = (x, y, c), (x, y, 1 - c)
        chips = [(1 - x, y), (x, 1 - y), (1 - x, 1 - y)]

        def copy(a, k, block, to, src=None):
            dst = outs[a].at[_dev_index(*block)]
            return pltpu.make_async_remote_copy(
                src_ref=dst if src is None else src, dst_ref=dst, send_sem=send_sems.at[a * 7 + k],
                recv_sem=recv_sems.at[a * 7 + k], device_id=to, device_id_type=MESH)

        mine = [pltpu.make_async_copy(ins[a], outs[a].at[_dev_index(*me)], local_sems.at[a]) for a in range(n)]
        for cp in mine:
            cp.start()
        first = []
        for a in range(n):
            first.append(copy(a, 0, me, sibling, src=ins[a]))
            first += [copy(a, 1 + j, me, (*chip, c), src=ins[a]) for j, chip in enumerate(chips)]
        for cp in first:
            cp.start()
        passed = []
        for j, chip in enumerate(chips):
            for a in range(n):
                copy(a, 1 + j, (*chip, c), me).wait_recv()
                fwd = copy(a, 4 + j, (*chip, c), sibling)
                fwd.start()
                passed.append(fwd)
        for a in range(n):
            copy(a, 0, sibling, me).wait_recv()
            for j, chip in enumerate(chips):
                copy(a, 4 + j, (*chip, 1 - c), me).wait_recv()
        for cp in first + passed:
            cp.wait_send()
        for cp in mine:
            cp.wait()

    return pl.pallas_call(
        body, name=name, in_specs=[ANY] * n, out_specs=[ANY] * n,
        out_shape=[jax.ShapeDtypeStruct((N_DEV,) + s.shape, s.dtype) for s in shards],
        scratch_shapes=[pltpu.SemaphoreType.DMA((7 * n,)), pltpu.SemaphoreType.DMA((7 * n,)),
                        pltpu.SemaphoreType.DMA((n,))],
    )(*shards)


def _exchange_blocks(parts, name):
    n = len(parts)
    relations = [(dx, dy, dc) for dx in (0, 1) for dy in (0, 1) for dc in (0, 1) if (dx, dy, dc) != (0, 0, 0)]

    def body(*refs):
        ins, outs = refs[:n], refs[n:2 * n]
        send_sems, recv_sems, local_sems = refs[2 * n:]
        x, y, c = _mesh_pos()
        me = _dev_index(x, y, c)
        mine = [pltpu.make_async_copy(ins[a].at[me], outs[a].at[me], local_sems.at[a]) for a in range(n)]
        for cp in mine:
            cp.start()
        copies = []
        for k, (dx, dy, dc) in enumerate(relations):
            peer = (x + dx - 2 * x * dx, y + dy - 2 * y * dy, c + dc - 2 * c * dc)
            for a in range(n):
                copies.append((pltpu.make_async_remote_copy(
                    src_ref=ins[a].at[_dev_index(*peer)], dst_ref=outs[a].at[me], send_sem=send_sems.at[a * 7 + k],
                    recv_sem=recv_sems.at[a * 7 + k], device_id=peer, device_id_type=MESH),
                    pltpu.make_async_remote_copy(
                    src_ref=ins[a].at[_dev_index(*peer)], dst_ref=outs[a].at[_dev_index(*peer)],
                    send_sem=send_sems.at[a * 7 + k], recv_sem=recv_sems.at[a * 7 + k], device_id=peer,
                    device_id_type=MESH)))
        for send, _ in copies:
            send.start()
        for _, recv in copies:
            recv.wait_recv()
        for send, _ in copies:
            send.wait_send()
        for cp in mine:
            cp.wait()

    return pl.pallas_call(
        body, name=name, in_specs=[ANY] * n, out_specs=[ANY] * n,
        out_shape=[jax.ShapeDtypeStruct(p.shape, p.dtype) for p in parts],
        scratch_shapes=[pltpu.SemaphoreType.DMA((7 * n,)), pltpu.SemaphoreType.DMA((7 * n,)),
                        pltpu.SemaphoreType.DMA((n,))],
    )(*parts)


HBM = pl.BlockSpec(memory_space=pltpu.HBM)
SEM = pl.BlockSpec(memory_space=pltpu.SEMAPHORE)
EFFECT = pltpu.SideEffectType.DATAFLOW_SIDE_EFFECTING
RELATIONS = [(dx, dy, dc) for dx in (0, 1) for dy in (0, 1) for dc in (0, 1) if (dx, dy, dc) != (0, 0, 0)]


def _peer(rel):
    x, y, c = _mesh_pos()
    dx, dy, dc = rel
    return (x + dx - 2 * x * dx, y + dy - 2 * y * dy, c + dc - 2 * c * dc)


def _split_copy(src_ref, land_ref, send_sems, recv_sems, k, scatter, incoming):
    peer = _peer(RELATIONS[k])
    me = _dev_index(*_mesh_pos())
    src = src_ref.at[_dev_index(*peer)] if scatter else src_ref
    dst = land_ref.at[_dev_index(*peer) if incoming else me]
    return pltpu.make_async_remote_copy(src_ref=src, dst_ref=dst, send_sem=send_sems.at[k], recv_sem=recv_sems.at[k],
                                        device_id=peer, device_id_type=MESH)


def _exchange_start(srcs, lands, *, scatter, name):
    n = len(srcs)

    def body(*refs):
        src_refs, land_refs = refs[:n], refs[n:2 * n]
        send, recv = refs[2 * n:3 * n], refs[3 * n:4 * n]
        token = refs[-1]
        for k in range(len(RELATIONS)):
            for a in range(n):
                _split_copy(src_refs[a], land_refs[a], send[a], recv[a], k, scatter, incoming=False).start()
        token[...] = jnp.zeros_like(token)

    n_rel = len(RELATIONS)
    outs = pl.pallas_call(
        body, name=name, in_specs=[HBM] * (2 * n),
        out_shape=[pltpu.SemaphoreType.DMA((n_rel,))] * (2 * n)
        + [pltpu.HBM(s.shape, s.dtype) for s in srcs] + [pltpu.HBM(s.shape, s.dtype) for s in lands]
        + [jax.ShapeDtypeStruct((SUBLANES, LANES), F32)],
        out_specs=[SEM] * (2 * n) + [HBM] * (2 * n) + [pl.BlockSpec(memory_space=pltpu.VMEM)],
        input_output_aliases={**{a: 2 * n + a for a in range(n)}, **{n + a: 3 * n + a for a in range(n)}},
        compiler_params=pltpu.CompilerParams(has_side_effects=EFFECT),
    )(*[pltpu.with_memory_space_constraint(s, pltpu.HBM) for s in srcs],
      *[pltpu.with_memory_space_constraint(s, pltpu.HBM) for s in lands])
    per_array = [(outs[a], outs[n + a], outs[2 * n + a], outs[3 * n + a]) for a in range(n)]
    return per_array, outs[-1]


def _exchange_wait(handle, after, *, scatter, name):
    send_sems, recv_sems, src_thru, land_thru = handle

    def body(src_ref, land_ref, send, recv, after_ref, src_dead, got_ref):
        for k in range(len(RELATIONS)):
            cp = _split_copy(src_ref, land_ref, send, recv, k, scatter, incoming=True)
            cp.wait_send()
            cp.wait_recv()

    return pl.pallas_call(
        body, name=name, in_specs=[HBM, HBM, SEM, SEM, ANY],
        out_shape=[pltpu.HBM(src_thru.shape, src_thru.dtype), pltpu.HBM(land_thru.shape, land_thru.dtype)],
        out_specs=[HBM, HBM], input_output_aliases={0: 0, 1: 1},
        compiler_params=pltpu.CompilerParams(has_side_effects=EFFECT),
    )(src_thru, land_thru, send_sems, recv_sems, after)[1]


def _landing_zone(own_block):
    me = _dev_index(*_mesh_pos())
    zone = lax.empty((N_DEV,) + own_block.shape, own_block.dtype)
    return lax.dynamic_update_index_in_dim(zone, own_block, me, 0)


def _row_tile(rows, want):
    t = min(want, rows) // SUBLANES * SUBLANES
    while rows % t:
        t -= SUBLANES
    return t


def _sum_slots(recv, *, tr, name):
    s_, r_, c_ = recv.shape
    tr = _row_tile(r_, tr)

    def body(g_ref, o_ref):
        acc = g_ref[0]
        for s in range(1, s_):
            acc = acc + g_ref[s]
        o_ref[...] = acc

    return pl.pallas_call(
        body, name=name, grid=(r_ // tr,),
        in_specs=[pl.BlockSpec((s_, tr, c_), lambda i: (0, i, 0))],
        out_specs=pl.BlockSpec((tr, c_), lambda i: (i, 0)),
        out_shape=jax.ShapeDtypeStruct((r_, c_), F32),
        compiler_params=_params(("parallel",), (2 * s_ + 3) * tr * c_ * 4),
    )(recv)


def _adamw(recv, w, m, v, *, tr, name):
    s_, r_, c_ = recv.shape
    tr = _row_tile(r_, tr)
    assert w.shape == (r_, c_), (name, w.shape, recv.shape)
    c1 = 1.0 - ADAM_B1 ** ADAM_STEP
    c2 = 1.0 - ADAM_B2 ** ADAM_STEP

    def body(g_ref, w_ref, m_ref, v_ref, go_ref, d_ref, mo_ref, vo_ref):
        g = g_ref[0]
        for s in range(1, s_):
            g = g + g_ref[s]
        mn = ADAM_B1 * m_ref[...] + (1.0 - ADAM_B1) * g
        vn = ADAM_B2 * v_ref[...] + (1.0 - ADAM_B2) * (g * g)
        go_ref[...] = g
        mo_ref[...] = mn
        vo_ref[...] = vn
        d_ref[...] = -ADAM_LR * ((mn / c1) / (jnp.sqrt(vn / c2) + ADAM_EPS) + ADAM_WD * w_ref[...])

    tile = pl.BlockSpec((tr, c_), lambda i: (i, 0))
    return pl.pallas_call(
        body, name=name, grid=(r_ // tr,),
        in_specs=[pl.BlockSpec((s_, tr, c_), lambda i: (0, i, 0)), tile, tile, tile],
        out_specs=[tile] * 4, out_shape=[jax.ShapeDtypeStruct((r_, c_), F32)] * 4,
        compiler_params=_params(("parallel",), (2 * s_ + 16) * tr * c_ * 4),
    )(recv, w, m, v)


def _s5_discretise(a_re, a_im, log_dt, b_re, b_im):
    dt = jnp.exp(log_dt)[:, None]
    lr = jnp.minimum(a_re, -1e-4)
    li = a_im
    mag = jnp.exp(lr * dt)
    lbr = mag * jnp.cos(li * dt)
    lbi = mag * jnp.sin(li * dt)
    zr, zi = lbr - 1.0, lbi
    den = lr * lr + li * li
    fr = (zr * lr + zi * li) / den
    fi = (zi * lr - zr * li) / den
    bbr = fr[..., None] * b_re - fi[..., None] * b_im
    bbi = fr[..., None] * b_im + fi[..., None] * b_re
    return lbr, lbi, bbr, bbi


def _softplus_neg(lam):
    return jnp.maximum(-lam, 0.0) + jnp.log(1.0 + jnp.exp(-jnp.abs(lam)))


S5_Q = 8
RG_Q = 2


def _local_step(x, tgt, W, comm):
    T, D = x.shape
    C = D
    G, P, H = W["ssm_b_re"].shape
    S = G * H
    F = W["mlp_b_up"].shape[1]
    n_in = 2 * C + S + 2 * D
    heads, hd = W["rg_wa"].shape[0], W["rg_wa"].shape[1]
    u_off, ga_off, gb_off = 2 * C, 2 * C + S, 2 * C + S + D

    sp, sp_vjp = jax.vjp(_softplus_neg, W["rg_lambda"])
    (lbr, lbi, bbr, bbi), s5_vjp = jax.vjp(_s5_discretise, W["ssm_a_re"], W["ssm_a_im"], W["ssm_log_dt"],
                                           W["ssm_b_re"], W["ssm_b_im"])
    lam_re, lam_im = lbr.reshape(-1), lbi.reshape(-1)
    wa_bd = _bd_pack(W["rg_wa"], RG_Q).astype(BF16)
    wx_bd = _bd_pack(W["rg_wx"], RG_Q).astype(BF16)
    wb_re = _bd_pack(jnp.swapaxes(bbr, 1, 2), S5_Q).astype(BF16)
    wb_im = _bd_pack(jnp.swapaxes(bbi, 1, 2), S5_Q).astype(BF16)
    wc_re = _bd_pack(jnp.swapaxes(W["ssm_c_re"], 1, 2), S5_Q).astype(BF16)
    wc_im_neg = _bd_pack(jnp.swapaxes(-W["ssm_c_im"], 1, 2), S5_Q).astype(BF16)
    d_row = W["ssm_d"].reshape(1, S)
    jr, kr = heads // RG_Q, RG_Q * hd
    js, ku, kp = G // S5_Q, S5_Q * H, S5_Q * P

    x_bf = x.astype(BF16)
    w_in = comm.weight("w_in", None)
    z = _mm(x_bf, w_in, M=T, N=n_in, K=D, tm=512, tn=n_in // 4, tk=D, after=comm.gather_token, name="fwd_in_proj")
    xc = _conv_fwd(z, W["conv_w"], W["conv_b"], T=T, C=C, name="fwd_conv")
    r = _bd([(xc, 0, wa_bd)], T=T, J=jr, kb=kr, nb=kr, extras=[(W["rg_ba"], "vec", 0)],
            epilogue=lambda acc, b: (_sig(acc + b),), name="fwd_rgate")
    ig = _bd([(xc, 0, wx_bd)], T=T, J=jr, kb=kr, nb=kr, extras=[(W["rg_bx"], "vec", 0)],
             epilogue=lambda acc, b: (_sig(acc + b),), name="fwd_igate")
    h, p = _rg_scan_fwd(z, r, ig, xc, sp, T=T, C=C, gate_off=C, name="fwd_rg_scan")
    w_a_out = comm.weight("w_a_out", p)
    y_a = _mm(p, w_a_out, M=T, N=D, K=C, tm=512, tn=D, tk=C, name="fwd_rg_out")

    bu_re = _bd([(z, u_off // ku, wb_re)], T=T, J=js, kb=ku, nb=kp, name="fwd_s5_bu_re")
    bu_im = _bd([(z, u_off // ku, wb_im)], T=T, J=js, kb=ku, nb=kp, name="fwd_s5_bu_im")
    h_re, h_im = _cscan(bu_re, bu_im, lam_re, lam_im, T=T, C=G * P, name="fwd_s5_scan")

    def s5_out(acc, u, d):
        y = acc + d * u
        return y, _gelu(y)

    y_s, yg = _bd([(h_re, 0, wc_re), (h_im, 0, wc_im_neg)], T=T, J=js, kb=kp, nb=ku,
                  extras=[(z, "tile", u_off // ku), (d_row, "vec", 0)], epilogue=s5_out, n_out=2,
                  out_dtypes=(F32, BF16), name="fwd_s5_out")
    w_glu_w, w_glu_v = comm.weight("glu_w", yg), comm.weight("glu_v", yg)
    glu_a = _mm(yg, w_glu_w, M=T, N=D, K=S, tm=1024, tn=D, tk=S, name="fwd_glu_w")
    glu_b = _mm(yg, w_glu_v, M=T, N=D, K=S, tm=1024, tn=D, tk=S, name="fwd_glu_v")

    cwm = 1024

    def mix_fn(ga, gb, ya, a, b):
        return (_sig(ga) * ya + _sig(gb) * (a * _sig(b)),)

    mix = _ew(mix_fn, [(z, "tile", ga_off // cwm), (z, "tile", gb_off // cwm), (y_a, "tile", 0), (glu_a, "tile", 0),
                       (glu_b, "tile", 0)], T=T, C=D, n_out=1, out_dtypes=(BF16,), cw=cwm, name="fwd_mix")
    w_out = comm.weight("w_out", mix)
    s1 = _mm(mix, w_out, M=T, N=D, K=D, tm=512, tn=1024, tk=D, extras=[(x, "mn")],
             epilogue=lambda acc, xv: (ALPHA * xv + acc,), name="fwd_out_proj")

    def ln1_fn(s, g, b):
        xhat, _ = _ln_stats(s)
        y = xhat * g + b
        return y, y

    x1, x1_bf = _ew(ln1_fn, [(s1, "tile", 0), (W["ln1_g"], "vec", 0), (W["ln1_b"], "vec", 0)], T=T, C=D, n_out=2,
                    out_dtypes=(F32, BF16), tm=128, name="fwd_ln1")
    w_up = comm.weight("mlp_w_up", x1_bf)

    def mlp_up_fn(acc, b):
        hp = acc + b
        rl = jnp.maximum(hp, 0.0)
        return rl * rl, hp

    hact, hpre = _mm(x1_bf, w_up, M=T, N=F, K=D, tm=512, tn=1024, tk=D, extras=[(W["mlp_b_up"], "n")],
                     epilogue=mlp_up_fn, n_out=2, out_dtypes=(BF16, BF16), name="fwd_mlp_up")
    w_down = comm.weight("mlp_w_down", hact)
    s2 = _mm(hact, w_down, M=T, N=D, K=F, tm=512, tn=1024, tk=2048,
             extras=[(x1, "mn"), (W["mlp_b_down"], "n")], epilogue=lambda acc, xv, b: (ALPHA * xv + acc + b,),
             name="fwd_mlp_down")

    def ln2_fn(s, t, g, b):
        xhat, rstd = _ln_stats(s)
        err = xhat * g + b - t
        dy = err * (1.0 / D)
        ds = _ln_bwd(dy, g, xhat, rstd)
        return ds, ds, 0.5 * dy * err, dy * xhat, dy, ds

    ds2, ds2_bf, loss_cols, d_ln2_g, d_ln2_b, d_b_down = _ew(
        ln2_fn, [(s2, "tile", 0), (tgt, "tile", 0), (W["ln2_g"], "vec", 0), (W["ln2_b"], "vec", 0)],
        T=T, C=D, n_out=2, n_cs=4, out_dtypes=(F32, BF16), tm=128, name="bwd_loss_ln2")
    d_w_down = _mm(hact, ds2_bf, M=F, N=D, K=T, ta=True, tm=1024, tn=1024, tk=2048, name="bwd_w_down")
    sent = comm.send_grad("mlp_w_down", d_w_down)

    def dhpre_fn(acc, hp):
        dv = acc * (2.0 * jnp.maximum(hp.astype(F32), 0.0))
        return dv, dv

    dhpre, d_b_up = _mm(ds2_bf, w_down, M=T, N=F, K=D, tb=True, tm=512, tn=1024, tk=D, extras=[(hpre, "mn")],
                        epilogue=dhpre_fn, n_cs=1, out_dtypes=(BF16,), after=sent, name="bwd_mlp_down")
    d_w_up = _mm(x1_bf, dhpre, M=D, N=F, K=T, ta=True, n_split=N_DEV, tm=1024, tn=F // N_DEV, tk=2048, name="bwd_w_up")
    sent = comm.send_grad("mlp_w_up", d_w_up)
    dx1 = _mm(dhpre, w_up, M=T, N=D, K=F, tb=True, tm=512, tn=1024, tk=2048,
              extras=[(ds2, "mn")], epilogue=lambda acc, dv: (ALPHA * dv + acc,), after=sent, name="bwd_mlp_up")

    def ln1_bwd_fn(s, dy, g):
        xhat, rstd = _ln_stats(s)
        ds = _ln_bwd(dy, g, xhat, rstd)
        return ds, ds, dy * xhat, dy

    ds1, ds1_bf, d_ln1_g, d_ln1_b = _ew(ln1_bwd_fn, [(s1, "tile", 0), (dx1, "tile", 0), (W["ln1_g"], "vec", 0)],
                                        T=T, C=D, n_out=2, n_cs=2, out_dtypes=(F32, BF16), tm=128, name="bwd_ln1")
    d_w_out = _mm(mix, ds1_bf, M=D, N=D, K=T, ta=True, tm=1024, tn=1024, tk=2048, name="bwd_w_out")
    sent = comm.send_grad("w_out", d_w_out)
    dmix = _mm(ds1_bf, w_out, M=T, N=D, K=D, tb=True, tm=512, tn=D, tk=D, after=sent, name="bwd_out_proj")

    def mix_bwd_fn(dm, ga, gb, ya, a, b):
        sa, sb, sv = _sig(ga), _sig(gb), _sig(b)
        yb = a * sv
        dyb = dm * sb
        return (dm * ya * (sa * (1.0 - sa)), dm * yb * (sb * (1.0 - sb)), dm * sa, dyb * sv,
                dyb * a * (sv * (1.0 - sv)))

    dg_a, dg_b, dy_a, dglu_a, dglu_b = _ew(
        mix_bwd_fn, [(dmix, "tile", 0), (z, "tile", ga_off // cwm), (z, "tile", gb_off // cwm), (y_a, "tile", 0),
                     (glu_a, "tile", 0), (glu_b, "tile", 0)], T=T, C=D, n_out=5, out_dtypes=(BF16,) * 5, cw=cwm,
        name="bwd_mix")

    d_w_a_out = _mm(p, dy_a, M=C, N=D, K=T, ta=True, tm=1024, tn=1024, tk=2048, name="bwd_w_a_out")
    sent = comm.send_grad("w_a_out", d_w_a_out)
    dp = _mm(dy_a, w_a_out, M=T, N=C, K=D, tb=True, tm=512, tn=C, tk=D, after=sent, name="bwd_rg_out")
    dgate, dra, dia, dxc0, d_ba, d_bx, d_sp = _rg_scan_bwd(dp, h, z, r, ig, xc, sp, T=T, C=C, gate_off=C,
                                                          name="bwd_rg_scan")
    dxc = _bd([(dra, 0, wa_bd), (dia, 0, wx_bd)], T=T, J=jr, kb=kr, nb=kr, tw=True, extras=[(dxc0, "tile", 0)],
              epilogue=lambda acc, d0: (acc + d0,), name="bwd_gates")
    d_wa = _bd_unpack(_bdw(xc, 0, dra, 0, T=T, J=jr, kb=kr, nb=kr, name="bwd_w_rgate"), RG_Q)
    d_wx = _bd_unpack(_bdw(xc, 0, dia, 0, T=T, J=jr, kb=kr, nb=kr, name="bwd_w_igate"), RG_Q)
    dxr, conv_sums = _conv_bwd(dxc, z, W["conv_w"], T=T, C=C, name="bwd_conv")
    d_conv_w, d_conv_b = conv_sums[0:4], conv_sums[4:5]
    (d_lambda,) = sp_vjp(d_sp)

    d_glu_w = _mm(yg, dglu_a, M=S, N=D, K=T, ta=True, n_split=N_DEV, tm=1024, tn=D // N_DEV, tk=2048, name="bwd_w_glu_w")
    d_glu_v = _mm(yg, dglu_b, M=S, N=D, K=T, ta=True, n_split=N_DEV, tm=1024, tn=D // N_DEV, tk=2048, name="bwd_w_glu_v")
    sent = comm.send_grad("glu_w", d_glu_w, "glu_v", d_glu_v)
    dyg0 = _mm(dglu_a, w_glu_w, M=T, N=S, K=D, tb=True, tm=512, tn=S, tk=D, after=sent, name="bwd_glu_w")
    dy_s = _mm(dglu_b, w_glu_v, M=T, N=S, K=D, tb=True, tm=512, tn=S, tk=D,
               extras=[(dyg0, "mn"), (y_s, "mn")], epilogue=lambda acc, d0, yv: ((acc + d0) * _dgelu(yv),),
               name="bwd_glu_v")
    dh_re = _bd([(dy_s, 0, wc_re)], T=T, J=js, kb=ku, nb=kp, tw=True, name="bwd_s5_dh_re")
    dh_im = _bd([(dy_s, 0, wc_im_neg)], T=T, J=js, kb=ku, nb=kp, tw=True, name="bwd_s5_dh_im")
    d_wc_re = _bdw(h_re, 0, dy_s, 0, T=T, J=js, kb=kp, nb=ku, name="bwd_w_c_re")
    d_wc_im_neg = _bdw(h_im, 0, dy_s, 0, T=T, J=js, kb=kp, nb=ku, name="bwd_w_c_im")
    g_re, g_im, d_lbr, d_lbi = _cscan(dh_re, dh_im, lam_re, lam_im, T=T, C=G * P, reverse=True, h_re=h_re, h_im=h_im,
                                      name="bwd_s5_scan")

    def du_fn(acc, dyv, uv, d):
        return acc + dyv * d, dyv * uv

    du, d_ssm_d = _bd([(g_re, 0, wb_re), (g_im, 0, wb_im)], T=T, J=js, kb=kp, nb=ku, tw=True,
                      extras=[(dy_s, "tile", 0), (z, "tile", u_off // ku), (d_row, "vec", 0)], epilogue=du_fn, n_cs=1,
                      out_dtypes=(BF16,), name="bwd_s5_du")
    d_wb_re = _bdw(z, u_off // ku, g_re, 0, T=T, J=js, kb=ku, nb=kp, name="bwd_w_b_re")
    d_wb_im = _bdw(z, u_off // ku, g_im, 0, T=T, J=js, kb=ku, nb=kp, name="bwd_w_b_im")
    d_bbr = jnp.swapaxes(_bd_unpack(d_wb_re, S5_Q), 1, 2)
    d_bbi = jnp.swapaxes(_bd_unpack(d_wb_im, S5_Q), 1, 2)
    d_a_re, d_a_im, d_log_dt, d_b_re, d_b_im = s5_vjp((d_lbr.reshape(G, P), d_lbi.reshape(G, P), d_bbr, d_bbi))
    d_c_re = jnp.swapaxes(_bd_unpack(d_wc_re, S5_Q), 1, 2)
    d_c_im = -jnp.swapaxes(_bd_unpack(d_wc_im_neg, S5_Q), 1, 2)

    dz = jnp.concatenate([dxr.astype(BF16), dgate.astype(BF16), du, dg_a, dg_b], axis=1)
    d_w_in = _mm(x_bf, dz, M=D, N=n_in, K=T, ta=True, n_split=N_DEV, tm=1024, tn=n_in // N_DEV, tk=2048, name="bwd_w_in")
    sent = comm.send_grad("w_in", d_w_in)
    grad_x = _mm(dz, w_in, M=T, N=D, K=n_in, tb=True, tm=512, tn=1024, tk=n_in // 4,
                 extras=[(ds1, "mn")], epilogue=lambda acc, dv: (ALPHA * dv + acc,), after=sent, name="bwd_in_proj")

    grads = dict(
        conv_w=d_conv_w, conv_b=d_conv_b, rg_wa=d_wa, rg_ba=d_ba, rg_wx=d_wx, rg_bx=d_bx,
        rg_lambda=d_lambda, ssm_a_re=d_a_re, ssm_a_im=d_a_im, ssm_log_dt=d_log_dt,
        ssm_b_re=d_b_re, ssm_b_im=d_b_im, ssm_c_re=d_c_re, ssm_c_im=d_c_im, ssm_d=d_ssm_d.reshape(G, H),
        ln1_g=d_ln1_g, ln1_b=d_ln1_b, mlp_b_up=d_b_up, mlp_b_down=d_b_down, ln2_g=d_ln2_g, ln2_b=d_ln2_b)
    return jnp.sum(loss_cols), grad_x, grads


BIG = ("w_in", "w_a_out", "glu_w", "glu_v", "w_out", "mlp_w_up", "mlp_w_down")
COL_SHARDED = ("w_in", "glu_w", "glu_v", "mlp_w_up")
SMALL = ("conv_w", "conv_b", "rg_wa", "rg_ba", "rg_wx", "rg_bx", "rg_lambda", "ssm_a_re", "ssm_a_im", "ssm_log_dt",
         "ssm_b_re", "ssm_b_im", "ssm_c_re", "ssm_c_im", "ssm_d", "ln1_g", "ln1_b", "mlp_b_up", "mlp_b_down", "ln2_g",
         "ln2_b")
ORDER = ("w_in", "conv_w", "conv_b", "rg_wa", "rg_ba", "rg_wx", "rg_bx", "rg_lambda", "w_a_out", "ssm_a_re",
         "ssm_a_im", "ssm_log_dt", "ssm_b_re", "ssm_b_im", "ssm_c_re", "ssm_c_im", "ssm_d", "glu_w", "glu_v", "w_out",
         "ln1_g", "ln1_b", "mlp_w_up", "mlp_b_up", "mlp_w_down", "mlp_b_down", "ln2_g", "ln2_b")
TILE_ELEMS = SUBLANES * LANES


def _pack(arrs):
    pieces = []
    for a in arrs:
        flat = a.reshape(-1)
        flat = jnp.pad(flat, (0, (-flat.shape[0]) % TILE_ELEMS))
        pieces.append(flat.reshape(-1, LANES))
    rows = sum(p.shape[0] for p in pieces)
    pad_rows = (-rows) % (N_DEV * SUBLANES)
    if pad_rows:
        pieces.append(jnp.zeros((pad_rows, LANES), pieces[0].dtype))
    return jnp.concatenate(pieces, axis=0)


def _unpack(packed, shapes):
    out, row = [], 0
    for shp in shapes:
        n = math.prod(shp)
        rows = -(-n // TILE_ELEMS) * SUBLANES
        out.append(packed[row:row + rows].reshape(-1)[:n].reshape(shp))
        row += rows
    return out


class _Comm:
    def __init__(self, w):
        first = _all_gather([w["w_in"].astype(BF16), w["conv_w"]], name="gather_w_in")
        self._weights = {"w_in": first[0]}
        self.conv_w = jnp.swapaxes(first[1], 0, 1).reshape(w["conv_w"].shape[0], -1)
        later = [k for k in BIG if k != "w_in"]
        shards = [w[k].astype(BF16) for k in later]
        handles, self.gather_token = _exchange_start(shards, [_landing_zone(s) for s in shards], scatter=False,
                                                     name="gather_weights_start")
        self._gathers = dict(zip(later, handles))
        self._grads = {}

    def weight(self, k, after):
        if k not in self._weights:
            self._weights[k] = _exchange_wait(self._gathers.pop(k), after, scatter=False, name="gather_wait_" + k)
        gk = self._weights[k]
        if k in COL_SHARDED:
            return jnp.swapaxes(gk, 0, 1).reshape(gk.shape[1], -1)
        return gk.reshape(-1, gk.shape[-1])

    def send_grad(self, *names_and_parts):
        names, parts = names_and_parts[0::2], names_and_parts[1::2]
        parts = [p if k in COL_SHARDED else p.reshape(N_DEV, p.shape[0] // N_DEV, p.shape[1])
                 for k, p in zip(names, parts)]
        me = _dev_index(*_mesh_pos())
        lands = [_landing_zone(lax.dynamic_index_in_dim(p, me, 0, keepdims=False)) for p in parts]
        handles, token = _exchange_start(parts, lands, scatter=True, name="grad_start_" + names[0])
        self._grads.update(zip(names, handles))
        return token

    def received_grad(self, k, after):
        return _exchange_wait(self._grads.pop(k), after, scatter=True, name="grad_wait_" + k)


def _step(x, tgt, w, m, v):
    dev = _dev_index(*_mesh_pos())

    comm = _Comm(w)
    small = dict(w)
    small["conv_w"] = comm.conv_w
    for k in ("conv_b", "rg_ba", "rg_bx", "rg_lambda", "ln1_g", "ln1_b", "mlp_b_up", "mlp_b_down", "ln2_g", "ln2_b"):
        small[k] = w[k].reshape(1, -1)

    loss_part, grad_x, grads = _local_step(x, tgt, small, comm)

    out_g, out_d, out_m, out_v = {}, {}, {}, {}
    for k in BIG:
        rk = comm.received_grad(k, grad_x)
        out_g[k], out_d[k], out_m[k], out_v[k] = _adamw(rk, w[k], m[k], v[k], tr=128, name="adamw_" + k)

    small_shapes = [grads[k].shape for k in SMALL]
    (small_recv,) = _exchange_blocks([_pack([grads[k] for k in SMALL]).reshape(N_DEV, -1, LANES)],
                                     name="exchange_small_grads")
    small_block = _sum_slots(small_recv, tr=512, name="sum_small_grads")
    (small_all,) = _all_gather([small_block], name="gather_small_grads")
    g_small = dict(zip(SMALL, _unpack(small_all.reshape(-1, LANES), small_shapes)))
    cw_cols = w["conv_w"].shape[1]
    g_small["conv_w"] = lax.dynamic_slice_in_dim(g_small["conv_w"], dev * cw_cols, cw_cols, axis=1)
    shapes = [w[k].shape for k in SMALL]
    g_pack, w_pack, m_pack, v_pack = [_pack([src[k] for k in SMALL]) for src in (g_small, w, m, v)]
    res = _adamw(g_pack[None], w_pack, m_pack, v_pack, tr=1024, name="adamw_small")
    for dst, packed in zip((out_g, out_d, out_m, out_v), res):
        dst.update(zip(SMALL, _unpack(packed, shapes)))

    loss = lax.psum(loss_part, ("x", "y", "c"))
    return loss, grad_x, out_g, out_d, out_m, out_v


def kernel(x, w_in, conv_w, conv_b, rg_wa, rg_ba, rg_wx, rg_bx, rg_lambda, w_a_out, ssm_a_re, ssm_a_im, ssm_log_dt, ssm_b_re, ssm_b_im, ssm_c_re, ssm_c_im, ssm_d, glu_w, glu_v, w_out, ln1_g, ln1_b, mlp_w_up, mlp_b_up, mlp_w_down, mlp_b_down, ln2_g, ln2_b, loss_target, m_w_in, m_conv_w, m_conv_b, m_rg_wa, m_rg_ba, m_rg_wx, m_rg_bx, m_rg_lambda, m_w_a_out, m_ssm_a_re, m_ssm_a_im, m_ssm_log_dt, m_ssm_b_re, m_ssm_b_im, m_ssm_c_re, m_ssm_c_im, m_ssm_d, m_glu_w, m_glu_v, m_w_out, m_ln1_g, m_ln1_b, m_mlp_w_up, m_mlp_b_up, m_mlp_w_down, m_mlp_b_down, m_ln2_g, m_ln2_b, v_w_in, v_conv_w, v_conv_b, v_rg_wa, v_rg_ba, v_rg_wx, v_rg_bx, v_rg_lambda, v_w_a_out, v_ssm_a_re, v_ssm_a_im, v_ssm_log_dt, v_ssm_b_re, v_ssm_b_im, v_ssm_c_re, v_ssm_c_im, v_ssm_d, v_glu_w, v_glu_v, v_w_out, v_ln1_g, v_ln1_b, v_mlp_w_up, v_mlp_b_up, v_mlp_w_down, v_mlp_b_down, v_ln2_g, v_ln2_b):
    args = locals()
    w = {k: args[k][0] for k in ORDER}
    m = {k: args["m_" + k][0] for k in ORDER}
    v = {k: args["v_" + k][0] for k in ORDER}
    loss, grad_x, out_g, out_d, out_m, out_v = _step(x[0], loss_target[0], w, m, v)
    outs = [loss, grad_x[None]]
    for group in (out_g, out_d, out_m, out_v):
        outs += [group[k].reshape(args[k].shape) for k in ORDER]
    return tuple(outs)
```

```python
import functools
import math

import jax
import jax.numpy as jnp
from jax import lax
from jax.experimental import pallas as pl
from jax.experimental.pallas import tpu as pltpu

F32 = jnp.float32
BF16 = jnp.bfloat16
MESH = pl.DeviceIdType.MESH
N_DEV = 8
SUBLANES = 8
LANES = 128
VMEM_BYTES_V7X = 64 * 2 ** 20
VMEM_CAP = VMEM_BYTES_V7X - 8 * 2 ** 20

ALPHA = 2.0 ** 0.25
LN_EPS = 1e-5
RG_C = 8.0
ADAM_LR, ADAM_B1, ADAM_B2, ADAM_EPS, ADAM_WD, ADAM_STEP = 0.001, 0.9, 0.999, 1e-08, 0.01, 10
GELU_C = math.sqrt(2.0 / math.pi)
GELU_K = 0.044715

ANY = pl.BlockSpec(memory_space=pl.ANY)


def _params(sem, vmem_bytes):
    limit = int(min(max(2 * vmem_bytes, 16 * 2 ** 20), VMEM_CAP))
    return pltpu.CompilerParams(dimension_semantics=sem, vmem_limit_bytes=limit)


def _sig(x):
    return 1.0 / (1.0 + jnp.exp(-x))


def _gelu(x):
    return 0.5 * x * (1.0 + jnp.tanh(GELU_C * (x + GELU_K * x * x * x)))


def _dgelu(x):
    th = jnp.tanh(GELU_C * (x + GELU_K * x * x * x))
    return 0.5 * (1.0 + th) + 0.5 * x * (1.0 - th * th) * (GELU_C * (1.0 + 3.0 * GELU_K * x * x))


def _expm1(x):
    p = x * (1.0 + x * (1 / 2 + x * (1 / 6 + x * (1 / 24 + x * (1 / 120 + x * (1 / 720 + x * (1 / 5040)))))))
    return jnp.where(jnp.abs(x) < 0.25, p, jnp.exp(x) - 1.0)


def _accumulate(ref, val, first):
    @pl.when(first)
    def _():
        ref[...] = val

    @pl.when(jnp.logical_not(first))
    def _():
        ref[...] += val


def _rows8(cw):
    return lax.broadcasted_iota(jnp.int32, (SUBLANES, cw), 0)


def _shift_down(cur, prev, s, rows):
    return jnp.where(rows < s, pltpu.roll(prev, s, 0), pltpu.roll(cur, s, 0))


def _shift_up(cur, nxt, s, rows):
    return jnp.where(rows < SUBLANES - s, pltpu.roll(cur, SUBLANES - s, 0), pltpu.roll(nxt, SUBLANES - s, 0))


def _mm(a, b, *, M, N, K, ta=False, tb=False, b_split=1, n_split=1, a_fn=None, extras=(), epilogue=None,
        n_out=1, n_cs=0, out_dtypes=None, tm=512, tn=512, tk=512, after=None, name):
    tm, tn, tk = min(tm, M), min(tn, N), min(tk, K)
    assert M % tm == 0 and N % tn == 0 and K % tk == 0, (name, M, N, K, tm, tn, tk)
    nk = K // tk
    grid = (N // tn, M // tm, nk)
    a_spec = pl.BlockSpec((tk, tm), lambda j, i, k: (k, i)) if ta else pl.BlockSpec((tm, tk), lambda j, i, k: (i, k))
    if b_split == 1:
        b_spec = pl.BlockSpec((tn, tk), lambda j, i, k: (j, k)) if tb else pl.BlockSpec((tk, tn), lambda j, i, k: (k, j))
    elif tb:
        kb = (K // b_split) // tk
        assert kb * tk * b_split == K, name
        b_spec = pl.BlockSpec((None, tn, tk), lambda j, i, k: (k // kb, j, k % kb))
    else:
        nb = (N // b_split) // tn
        assert nb * tn * b_split == N, name
        b_spec = pl.BlockSpec((None, tk, tn), lambda j, i, k: (j // nb, k, j % nb))
    in_specs = [a_spec, b_spec]
    for arr, kind in extras:
        in_specs.append(pl.BlockSpec((tm, tn), lambda j, i, k: (i, j)) if kind == "mn"
                        else pl.BlockSpec((1, tn), lambda j, i, k: (0, j)))
    out_dtypes = (F32,) * n_out if out_dtypes is None else out_dtypes
    if n_split == 1:
        out_shape = [jax.ShapeDtypeStruct((M, N), dt) for dt in out_dtypes]
        out_specs = [pl.BlockSpec((tm, tn), lambda j, i, k: (i, j)) for _ in range(n_out)]
    else:
        assert n_out == 1
        nbo = (N // n_split) // tn
        assert nbo * tn * n_split == N, name
        out_shape = [jax.ShapeDtypeStruct((n_split, M, N // n_split), out_dtypes[0])]
        out_specs = [pl.BlockSpec((None, tm, tn), lambda j, i, k: (j // nbo, i, j % nbo))]
    out_shape += [jax.ShapeDtypeStruct((1, N), F32) for _ in range(n_cs)]
    out_specs += [pl.BlockSpec((1, tn), lambda j, i, k: (0, j)) for _ in range(n_cs)]
    ne = len(extras)
    dims = (((0 if ta else 1,), (1 if tb else 0,)), ((), ()))

    n_after = 0 if after is None else 1
    in_specs += [ANY] * n_after

    def body(*refs):
        a_ref, b_ref = refs[0], refs[1]
        ex_refs = refs[2:2 + ne]
        first_out = 2 + ne + n_after
        out_refs = refs[first_out:first_out + n_out]
        cs_refs = refs[first_out + n_out:first_out + n_out + n_cs]
        i, k = pl.program_id(1), pl.program_id(2)

        def product():
            av = a_ref[...]
            if a_fn is not None:
                av = a_fn(av.astype(F32))
            return lax.dot_general(av.astype(BF16), b_ref[...].astype(BF16), dims, preferred_element_type=F32)

        def finish(acc):
            res = (acc,) if epilogue is None else epilogue(acc, *[r[...] for r in ex_refs])
            for r, o in zip(out_refs, res[:n_out]):
                r[...] = o.astype(r.dtype)
            for r, cval in zip(cs_refs, res[n_out:]):
                _accumulate(r, jnp.sum(cval, axis=0, keepdims=True), i == 0)

        if nk == 1:
            finish(product())
            return
        acc_ref = refs[-1]

        @pl.when(k == 0)
        def _():
            acc_ref[...] = jnp.zeros_like(acc_ref)

        acc_ref[...] += product()

        @pl.when(k == nk - 1)
        def _():
            finish(acc_ref[...])

    vmem = 2 * tm * tk * a.dtype.itemsize + 2 * tk * tn * b.dtype.itemsize + (1 + 2 * n_out + 2 * ne + 2) * tm * tn * 4
    outs = pl.pallas_call(
        body, name=name, grid=grid, in_specs=in_specs, out_specs=out_specs, out_shape=out_shape,
        scratch_shapes=[pltpu.VMEM((tm, tn), F32)] if nk > 1 else [],
        compiler_params=_params(("parallel", "arbitrary", "arbitrary"), vmem),
    )(a, b, *[arr for arr, _ in extras], *([after] if n_after else []))
    return outs[0] if len(outs) == 1 else outs


BD_STEP = 4

def _bd(pairs, *, T, J, kb, nb, tw=False, extras=(), epilogue=None, n_out=1, n_cs=0, out_dtypes=None, tm=512, name):
    jb = BD_STEP
    assert T % tm == 0 and J % jb == 0
    grid = (J // jb, T // tm)
    npair, ne = len(pairs), len(extras)
    in_specs, args = [], []
    for arr, off, w in pairs:
        assert off % jb == 0, name
        in_specs.append(pl.BlockSpec((tm, jb * kb), lambda j, i, off=off // jb: (i, off + j)))
        in_specs.append(pl.BlockSpec((jb,) + tuple(w.shape[1:]), lambda j, i: (j, 0, 0)))
        args += [arr, w]
    for arr, kind, off in extras:
        assert off % jb == 0, name
        in_specs.append(pl.BlockSpec((tm, jb * nb), lambda j, i, off=off // jb: (i, off + j)) if kind == "tile"
                        else pl.BlockSpec((1, jb * nb), lambda j, i, off=off // jb: (0, off + j)))
        args.append(arr)
    out_dtypes = (F32,) * n_out if out_dtypes is None else out_dtypes
    out_shape = [jax.ShapeDtypeStruct((T, J * nb), dt) for dt in out_dtypes]
    out_specs = [pl.BlockSpec((tm, jb * nb), lambda j, i: (i, j)) for _ in range(n_out)]
    out_shape += [jax.ShapeDtypeStruct((1, J * nb), F32) for _ in range(n_cs)]
    out_specs += [pl.BlockSpec((1, jb * nb), lambda j, i: (0, j)) for _ in range(n_cs)]
    dims = (((1,), (1 if tw else 0,)), ((), ()))

    def body(*refs):
        ex_refs = refs[2 * npair:2 * npair + ne]
        out_refs = refs[2 * npair + ne:2 * npair + ne + n_out]
        cs_refs = refs[2 * npair + ne + n_out:]
        i = pl.program_id(1)
        for s in range(jb):
            cols_in, cols_out = pl.ds(s * kb, kb), pl.ds(s * nb, nb)
            acc = None
            for p in range(npair):
                d = lax.dot_general(refs[2 * p][:, cols_in].astype(BF16), refs[2 * p + 1][s].astype(BF16), dims,
                                    preferred_element_type=F32)
                acc = d if acc is None else acc + d
            res = (acc,) if epilogue is None else epilogue(acc, *[r[:, cols_out] for r in ex_refs])
            for r, o in zip(out_refs, res[:n_out]):
                r[:, cols_out] = o.astype(r.dtype)
            for r, cval in zip(cs_refs, res[n_out:]):
                _accumulate(r.at[:, cols_out], jnp.sum(cval, axis=0, keepdims=True), i == 0)

    vmem = jb * (2 * npair * tm * kb + 2 * npair * kb * nb + (2 * n_out + 2 * ne + 3) * tm * nb) * 4
    outs = pl.pallas_call(
        body, name=name, grid=grid, in_specs=in_specs, out_specs=out_specs, out_shape=out_shape,
        compiler_params=_params(("parallel", "arbitrary"), vmem),
    )(*args)
    return outs[0] if len(outs) == 1 else outs


def _bdw(a, a_off, b, b_off, *, T, J, kb, nb, tm=512, name):
    jb = BD_STEP
    assert T % tm == 0 and J % jb == 0 and a_off % jb == 0 and b_off % jb == 0
    a_blk, b_blk = a_off // jb, b_off // jb

    def body(a_ref, b_ref, o_ref):
        i = pl.program_id(1)
        for s in range(jb):
            d = lax.dot_general(a_ref[:, pl.ds(s * kb, kb)].astype(BF16), b_ref[:, pl.ds(s * nb, nb)].astype(BF16),
                                (((0,), (0,)), ((), ())), preferred_element_type=F32)
            _accumulate(o_ref.at[s], d, i == 0)

    return pl.pallas_call(
        body, name=name, grid=(J // jb, T // tm),
        in_specs=[pl.BlockSpec((tm, jb * kb), lambda j, i: (i, a_blk + j)),
                  pl.BlockSpec((tm, jb * nb), lambda j, i: (i, b_blk + j))],
        out_specs=pl.BlockSpec((jb, kb, nb), lambda j, i: (j, 0, 0)),
        out_shape=jax.ShapeDtypeStruct((J, kb, nb), F32),
        compiler_params=_params(("parallel", "arbitrary"), jb * (2 * tm * (kb + nb) + 3 * kb * nb) * 4),
    )(a, b)


def _bd_pack(w, q):
    g, a, b = w.shape
    eye = jnp.eye(q, dtype=w.dtype)
    return jnp.einsum("jqab,qr->jqarb", w.reshape(g // q, q, a, b), eye).reshape(g // q, q * a, q * b)


def _bd_unpack(wp, q):
    j, qa, qb = wp.shape
    a, b = qa // q, qb // q
    w5 = wp.reshape(j, q, a, q, b)
    return jnp.stack([w5[:, r, :, r, :] for r in range(q)], axis=1).reshape(j * q, a, b)


def _ew(fn, ins, *, T, C, n_out, n_cs=0, out_dtypes=None, tm=256, cw=None, name):
    cw = C if cw is None else cw
    assert T % tm == 0 and C % cw == 0
    grid = (C // cw, T // tm)
    in_specs = []
    for arr, kind, off in ins:
        in_specs.append(pl.BlockSpec((tm, cw), lambda j, i, off=off: (i, off + j)) if kind == "tile"
                        else pl.BlockSpec((arr.shape[0], cw), lambda j, i, off=off: (0, off + j)))
    out_dtypes = (F32,) * n_out if out_dtypes is None else out_dtypes
    out_shape = [jax.ShapeDtypeStruct((T, C), dt) for dt in out_dtypes]
    out_specs = [pl.BlockSpec((tm, cw), lambda j, i: (i, j)) for _ in range(n_out)]
    out_shape += [jax.ShapeDtypeStruct((1, C), F32) for _ in range(n_cs)]
    out_specs += [pl.BlockSpec((1, cw), lambda j, i: (0, j)) for _ in range(n_cs)]
    nin = len(ins)

    def body(*refs):
        i = pl.program_id(1)
        res = fn(*[r[...].astype(F32) for r in refs[:nin]])
        for r, o in zip(refs[nin:nin + n_out], res[:n_out]):
            r[...] = o.astype(r.dtype)
        for r, cval in zip(refs[nin + n_out:], res[n_out:]):
            _accumulate(r, jnp.sum(cval, axis=0, keepdims=True), i == 0)

    vmem = (2 * nin + 2 * n_out + 6) * tm * cw * 4
    outs = pl.pallas_call(
        body, name=name, grid=grid, in_specs=in_specs, out_specs=out_specs, out_shape=out_shape,
        compiler_params=_params(("parallel", "arbitrary"), vmem),
    )(*[arr for arr, _, _ in ins])
    return outs[0] if len(outs) == 1 else outs


def _ln_stats(s):
    mu = jnp.mean(s, axis=-1, keepdims=True)
    d = s - mu
    var = jnp.mean(d * d, axis=-1, keepdims=True)
    rstd = lax.rsqrt(var + LN_EPS)
    return d * rstd, rstd


def _ln_bwd(dy, g, xhat, rstd):
    dxh = dy * g
    m1 = jnp.mean(dxh, axis=-1, keepdims=True)
    m2 = jnp.mean(dxh * xhat, axis=-1, keepdims=True)
    return rstd * (dxh - m1 - xhat * m2)


def _conv_fwd(z, conv_w, conv_b, *, T, C, tm=512, cw=1024, name):
    ng, hb = tm // SUBLANES, tm // SUBLANES

    def body(x_ref, halo_ref, w_ref, b_ref, o_ref):
        it = pl.program_id(1)
        rows = _rows8(cw)
        halo = jnp.where(it == 0, 0.0, halo_ref[...])
        w = w_ref[...]
        bias = b_ref[...]

        def group(g, carry):
            off = pl.multiple_of(g * SUBLANES, SUBLANES)
            cur = x_ref[pl.ds(off, SUBLANES), :]
            prev = x_ref[pl.ds(pl.multiple_of(jnp.maximum(off - SUBLANES, 0), SUBLANES), SUBLANES), :]
            prev = jnp.where(g == 0, halo, prev)
            acc = cur * w[3:4] + bias
            for s in (1, 2, 3):
                acc = acc + _shift_down(cur, prev, s, rows) * w[3 - s:4 - s]
            o_ref[pl.ds(off, SUBLANES), :] = acc
            return carry

        lax.fori_loop(0, ng, group, 0)

    return pl.pallas_call(
        body, name=name, grid=(C // cw, T // tm),
        in_specs=[pl.BlockSpec((tm, cw), lambda j, i: (i, j)),
                  pl.BlockSpec((SUBLANES, cw), lambda j, i: (jnp.maximum(i * hb - 1, 0), j)),
                  pl.BlockSpec((4, cw), lambda j, i: (0, j)), pl.BlockSpec((1, cw), lambda j, i: (0, j))],
        out_specs=pl.BlockSpec((tm, cw), lambda j, i: (i, j)),
        out_shape=jax.ShapeDtypeStruct((T, C), F32),
        compiler_params=_params(("parallel", "arbitrary"), 5 * tm * cw * 4),
    )(z, z, conv_w, conv_b)


def _conv_bwd(dxc, z, conv_w, *, T, C, tm=512, cw=512, name):
    ng, hb, last = tm // SUBLANES, tm // SUBLANES, T // SUBLANES - 1
    nt = T // tm

    def body(d_ref, dn_ref, x_ref, xp_ref, w_ref, o_ref, sums_ref):
        it = pl.program_id(1)
        rows = _rows8(cw)
        dnext = jnp.where(it == nt - 1, 0.0, dn_ref[...])
        xprev = jnp.where(it == 0, 0.0, xp_ref[...])
        w = w_ref[...]

        def group(g, accs):
            off = pl.multiple_of(g * SUBLANES, SUBLANES)
            dcur = d_ref[pl.ds(off, SUBLANES), :]
            dnx = d_ref[pl.ds(pl.multiple_of(jnp.minimum(off + SUBLANES, tm - SUBLANES), SUBLANES), SUBLANES), :]
            dnx = jnp.where(g == ng - 1, dnext, dnx)
            xcur = x_ref[pl.ds(off, SUBLANES), :]
            xpv = x_ref[pl.ds(pl.multiple_of(jnp.maximum(off - SUBLANES, 0), SUBLANES), SUBLANES), :]
            xpv = jnp.where(g == 0, xprev, xpv)
            acc = dcur * w[3:4]
            for s in (1, 2, 3):
                acc = acc + _shift_up(dcur, dnx, s, rows) * w[3 - s:4 - s]
            o_ref[pl.ds(off, SUBLANES), :] = acc
            a0, a1, a2, a3, ab = accs
            a0 = a0 + dcur * _shift_down(xcur, xpv, 3, rows)
            a1 = a1 + dcur * _shift_down(xcur, xpv, 2, rows)
            a2 = a2 + dcur * _shift_down(xcur, xpv, 1, rows)
            a3 = a3 + dcur * xcur
            return a0, a1, a2, a3, ab + dcur

        zero = jnp.zeros((SUBLANES, cw), F32)
        accs = lax.fori_loop(0, ng, group, (zero,) * 5)
        sums = jnp.zeros((SUBLANES, cw), F32)
        for k, a in enumerate(accs):
            sums = jnp.where(rows == k, jnp.sum(a, axis=0, keepdims=True), sums)
        _accumulate(sums_ref, sums, it == 0)

    tile = pl.BlockSpec((tm, cw), lambda j, i: (i, j))
    return pl.pallas_call(
        body, name=name, grid=(C // cw, nt),
        in_specs=[tile, pl.BlockSpec((SUBLANES, cw), lambda j, i: (jnp.minimum((i + 1) * hb, last), j)),
                  tile, pl.BlockSpec((SUBLANES, cw), lambda j, i: (jnp.maximum(i * hb - 1, 0), j)),
                  pl.BlockSpec((4, cw), lambda j, i: (0, j))],
        out_specs=[tile, pl.BlockSpec((SUBLANES, cw), lambda j, i: (0, j))],
        out_shape=[jax.ShapeDtypeStruct((T, C), F32), jax.ShapeDtypeStruct((SUBLANES, C), F32)],
        compiler_params=_params(("parallel", "arbitrary"), 7 * tm * cw * 4),
    )(dxc, dxc, z, z, conv_w)


def _rg_coeffs(r, ig, xc, sp):
    la = (-RG_C) * r * sp
    a = jnp.exp(la)
    m = jnp.sqrt(-_expm1(2.0 * la))
    return a, m, m * (ig * xc)


def _rg_scan_fwd(z, r, ig, xc, sp, *, T, C, gate_off, tm=512, cw=256, name):
    rows16 = 2 * SUBLANES
    nq = tm // rows16

    def body(gate_ref, r_ref, i_ref, xc_ref, sp_ref, h_ref, p_ref, carry_ref):
        it = pl.program_id(1)

        @pl.when(it == 0)
        def _():
            carry_ref[...] = jnp.zeros_like(carry_ref)

        rows = _rows8(cw)
        sp_row = sp_ref[...]

        def pair(q, carry):
            base = pl.multiple_of(q * rows16, rows16)
            halves = []
            for half in range(2):
                sl = pl.ds(pl.multiple_of(base + half * SUBLANES, SUBLANES), SUBLANES)
                a, _, b = _rg_coeffs(r_ref[sl, :], i_ref[sl, :], xc_ref[sl, :], sp_row)
                for s in (1, 2, 4):
                    keep = rows >= s
                    sa = jnp.where(keep, pltpu.roll(a, s, 0), 1.0)
                    sb = jnp.where(keep, pltpu.roll(b, s, 0), 0.0)
                    b = b + a * sb
                    a = a * sa
                h = b + a * carry
                h_ref[sl, :] = h
                halves.append(h * _gelu(gate_ref[sl, :]))
                carry = h[SUBLANES - 1:SUBLANES, :]
            p_ref[pl.ds(base, rows16), :] = jnp.concatenate(halves, axis=0).astype(p_ref.dtype)
            return carry

        last = lax.fori_loop(0, nq, pair, carry_ref[0:1, :], unroll=2)
        carry_ref[...] = jnp.broadcast_to(last, carry_ref.shape)

    tile = pl.BlockSpec((tm, cw), lambda j, i: (i, j))
    gate_blk = gate_off // cw
    return pl.pallas_call(
        body, name=name, grid=(C // cw, T // tm),
        in_specs=[pl.BlockSpec((tm, cw), lambda j, i: (i, gate_blk + j)), tile, tile, tile,
                  pl.BlockSpec((1, cw), lambda j, i: (0, j))],
        out_specs=[tile, tile],
        out_shape=[jax.ShapeDtypeStruct((T, C), F32), jax.ShapeDtypeStruct((T, C), BF16)],
        scratch_shapes=[pltpu.VMEM((SUBLANES, cw), F32)],
        compiler_params=_params(("parallel", "arbitrary"), 12 * tm * cw * 4),
    )(z, r, ig, xc, sp)


def _rg_scan_bwd(dp, h, z, r, ig, xc, sp, *, T, C, gate_off, tm=512, cw=256, name):
    ng, hb, nt = tm // SUBLANES, tm // SUBLANES, T // tm

    def body(dp_ref, h_ref, hp_ref, gate_ref, r_ref, i_ref, xc_ref, sp_ref,
             dgate_ref, dra_ref, dia_ref, dxc_ref, cra_ref, cia_ref, csp_ref, cg_ref, ca_ref):
        step = pl.program_id(1)

        @pl.when(step == 0)
        def _():
            cg_ref[...] = jnp.zeros_like(cg_ref)
            ca_ref[...] = jnp.zeros_like(ca_ref)

        rows = _rows8(cw)
        sp_row = sp_ref[...]
        hhalo = jnp.where(step == nt - 1, 0.0, hp_ref[...])

        def group(gi, carry):
            g_next, a_next, s_ra, s_ia, s_sp = carry
            g = ng - 1 - gi
            off = pl.multiple_of(g * SUBLANES, SUBLANES)
            sl = pl.ds(off, SUBLANES)
            rr, ii, xx = r_ref[sl, :], i_ref[sl, :], xc_ref[sl, :]
            a, m, _ = _rg_coeffs(rr, ii, xx, sp_row)
            hh = h_ref[sl, :]
            hpv = h_ref[pl.ds(pl.multiple_of(jnp.maximum(off - SUBLANES, 0), SUBLANES), SUBLANES), :]
            hpv = jnp.where(g == 0, hhalo, hpv)
            hprev = _shift_down(hh, hpv, 1, rows)
            gate = gate_ref[sl, :]
            dpv = dp_ref[sl, :]
            d = dpv * _gelu(gate)
            dgate_ref[sl, :] = dpv * hh * _dgelu(gate)
            c = jnp.where(rows < SUBLANES - 1, pltpu.roll(a, SUBLANES - 1, 0), a_next)
            for s in (1, 2, 4):
                keep = rows < SUBLANES - s
                sc = jnp.where(keep, pltpu.roll(c, SUBLANES - s, 0), 1.0)
                sd = jnp.where(keep, pltpu.roll(d, SUBLANES - s, 0), 0.0)
                d = d + c * sd
                c = c * sc
            gg = d + c * g_next
            da = gg * hprev
            dm = gg * (ii * xx)
            di = gg * (m * xx)
            dxc_ref[sl, :] = gg * (m * ii)
            dla = da * a - dm * (a * a / m)
            dra = dla * ((-RG_C) * sp_row) * (rr * (1.0 - rr))
            dia = di * (ii * (1.0 - ii))
            dra_ref[sl, :] = dra
            dia_ref[sl, :] = dia
            return (gg[0:1, :], a[0:1, :], s_ra + dra, s_ia + dia, s_sp + dla * ((-RG_C) * rr))

        zero = jnp.zeros((SUBLANES, cw), F32)
        g_first, a_first, s_ra, s_ia, s_sp = lax.fori_loop(
            0, ng, group, (cg_ref[0:1, :], ca_ref[0:1, :], zero, zero, zero), unroll=2)
        cg_ref[...] = jnp.broadcast_to(g_first, cg_ref.shape)
        ca_ref[...] = jnp.broadcast_to(a_first, ca_ref.shape)
        for ref, acc in ((cra_ref, s_ra), (cia_ref, s_ia), (csp_ref, s_sp)):
            _accumulate(ref, jnp.sum(acc, axis=0, keepdims=True), step == 0)

    tile = pl.BlockSpec((tm, cw), lambda j, i: (nt - 1 - i, j))
    vec = pl.BlockSpec((1, cw), lambda j, i: (0, j))
    gate_blk = gate_off // cw
    return pl.pallas_call(
        body, name=name, grid=(C // cw, nt),
        in_specs=[tile, tile, pl.BlockSpec((SUBLANES, cw), lambda j, i: (jnp.maximum((nt - 1 - i) * hb - 1, 0), j)),
                  pl.BlockSpec((tm, cw), lambda j, i: (nt - 1 - i, gate_blk + j)), tile, tile, tile, vec],
        out_specs=[tile, tile, tile, tile, vec, vec, vec],
        out_shape=[jax.ShapeDtypeStruct((T, C), F32)] * 4 + [jax.ShapeDtypeStruct((1, C), F32)] * 3,
        scratch_shapes=[pltpu.VMEM((SUBLANES, cw), F32), pltpu.VMEM((SUBLANES, cw), F32)],
        compiler_params=_params(("parallel", "arbitrary"), 24 * tm * cw * 4),
    )(dp, h, h, z, r, ig, xc, sp)


def _cscan_tables(lr, li, reverse):
    lam = (lr.reshape(-1), -li.reshape(-1) if reverse else li.reshape(-1))

    def mul(p, q):
        return p[0] * q[0] - p[1] * q[1], p[0] * q[1] + p[1] * q[0]

    pows = [lam]
    for _ in range(SUBLANES - 1):
        pows.append(mul(pows[-1], lam))
    zero = jnp.zeros_like(lam[0])
    tab = jnp.stack([pows[0][0], pows[0][1], pows[1][0], pows[1][1], pows[3][0], pows[3][1], zero, zero])
    if reverse:
        pows = pows[::-1]
    return tab, jnp.stack([p[0] for p in pows]), jnp.stack([p[1] for p in pows])


def _cscan_tile(xr_ref, xi_ref, tab_ref, pwr_ref, pwi_ref, cr_ref, ci_ref, *, reverse, h=None):
    tm, cw = xr_ref.shape
    ng = tm // SUBLANES
    rows = _rows8(cw)
    lam = [(tab_ref[2 * k:2 * k + 1, :], tab_ref[2 * k + 1:2 * k + 2, :]) for k in range(3)]
    pwr, pwi = pwr_ref[...], pwi_ref[...]

    def group(gi, carry):
        g = ng - 1 - gi if reverse else gi
        off = pl.multiple_of(g * SUBLANES, SUBLANES)
        sl = pl.ds(off, SUBLANES)
        xr, xi = xr_ref[sl, :], xi_ref[sl, :]
        for k, s in enumerate((1, 2, 4)):
            shift = SUBLANES - s if reverse else s
            keep = rows < SUBLANES - s if reverse else rows >= s
            sr = jnp.where(keep, pltpu.roll(xr, shift, 0), 0.0)
            si = jnp.where(keep, pltpu.roll(xi, shift, 0), 0.0)
            l_re, l_im = lam[k]
            xr, xi = xr + (l_re * sr - l_im * si), xi + (l_re * si + l_im * sr)
        cr, ci = carry[0], carry[1]
        xr, xi = xr + (pwr * cr - pwi * ci), xi + (pwr * ci + pwi * cr)
        xr_ref[sl, :] = xr
        xi_ref[sl, :] = xi
        edge = slice(0, 1) if reverse else slice(SUBLANES - 1, SUBLANES)
        out = (xr[edge, :], xi[edge, :])
        if h is not None:
            hr_ref, hi_ref, hr_halo, hi_halo = h
            poff = pl.multiple_of(jnp.maximum(off - SUBLANES, 0), SUBLANES)
            hrp = jnp.where(g == 0, hr_halo, hr_ref[pl.ds(poff, SUBLANES), :])
            hip = jnp.where(g == 0, hi_halo, hi_ref[pl.ds(poff, SUBLANES), :])
            hr1 = _shift_down(hr_ref[sl, :], hrp, 1, rows)
            hi1 = _shift_down(hi_ref[sl, :], hip, 1, rows)
            out += (carry[2] + (xr * hr1 + xi * hi1), carry[3] + (xi * hr1 - xr * hi1))
        return out

    init = (cr_ref[0:1, :], ci_ref[0:1, :])
    if h is not None:
        init += (jnp.zeros((SUBLANES, cw), F32),) * 2
    res = lax.fori_loop(0, ng, group, init)
    cr_ref[...] = jnp.broadcast_to(res[0], cr_ref.shape)
    ci_ref[...] = jnp.broadcast_to(res[1], ci_ref.shape)
    return res[2:]


def _s5_fwd(z, u_off, wb_re, wb_im, wc_re, wc_im_neg, d_row, lr, li, *, T, tm=512, name):
    J, ku, kp = wb_re.shape
    nt = T // tm
    tab, pw_re, pw_im = _cscan_tables(lr, li, False)
    u_blk = u_off // ku

    def body(u_ref, wbr_ref, wbi_ref, wcr_ref, wci_ref, d_ref, tab_ref, pwr_ref, pwi_ref,
             hr_ref, hi_ref, y_ref, yg_ref, cr_ref, ci_ref):
        @pl.when(pl.program_id(1) == 0)
        def _():
            cr_ref[...] = jnp.zeros_like(cr_ref)
            ci_ref[...] = jnp.zeros_like(ci_ref)

        u = u_ref[...]
        ub = u.astype(BF16)
        hr_ref[...] = jnp.dot(ub, wbr_ref[...], preferred_element_type=F32)
        hi_ref[...] = jnp.dot(ub, wbi_ref[...], preferred_element_type=F32)
        _cscan_tile(hr_ref, hi_ref, tab_ref, pwr_ref, pwi_ref, cr_ref, ci_ref, reverse=False)
        y = (jnp.dot(hr_ref[...].astype(BF16), wcr_ref[...], preferred_element_type=F32)
             + jnp.dot(hi_ref[...].astype(BF16), wci_ref[...], preferred_element_type=F32) + d_ref[...] * u)
        y_ref[...] = y
        yg_ref[...] = _gelu(y).astype(yg_ref.dtype)

    wb_spec = pl.BlockSpec((None, ku, kp), lambda j, i: (j, 0, 0))
    wc_spec = pl.BlockSpec((None, kp, ku), lambda j, i: (j, 0, 0))
    small = pl.BlockSpec((SUBLANES, kp), lambda j, i: (0, j))
    state = pl.BlockSpec((tm, kp), lambda j, i: (i, j))
    chan = pl.BlockSpec((tm, ku), lambda j, i: (i, j))
    return pl.pallas_call(
        body, name=name, grid=(J, nt),
        in_specs=[pl.BlockSpec((tm, ku), lambda j, i: (i, u_blk + j)), wb_spec, wb_spec, wc_spec, wc_spec,
                  pl.BlockSpec((1, ku), lambda j, i: (0, j)), small, small, small],
        out_specs=[state, state, chan, chan],
        out_shape=[jax.ShapeDtypeStruct((T, J * kp), F32)] * 2
        + [jax.ShapeDtypeStruct((T, J * ku), F32), jax.ShapeDtypeStruct((T, J * ku), BF16)],
        scratch_shapes=[pltpu.VMEM((SUBLANES, kp), F32), pltpu.VMEM((SUBLANES, kp), F32)],
        compiler_params=_params(("parallel", "arbitrary"), 10 * tm * kp * 4),
    )(z, wb_re, wb_im, wc_re, wc_im_neg, d_row, tab, pw_re, pw_im)


def _s5_bwd(dy, z, u_off, h_re, h_im, wb_re, wb_im, wc_re, wc_im_neg, d_row, lr, li, *, T, tm=512, name):
    J, ku, kp = wb_re.shape
    nt, hb = T // tm, tm // SUBLANES
    tab, pw_re, pw_im = _cscan_tables(lr, li, True)
    u_blk = u_off // ku
    contract_rows = (((0,), (0,)), ((), ()))
    contract_cols = (((1,), (1,)), ((), ()))

    def body(dy_ref, u_ref, hr_ref, hrp_ref, hi_ref, hip_ref, wbr_ref, wbi_ref, wcr_ref, wci_ref, d_ref,
             tab_ref, pwr_ref, pwi_ref, du_ref, dlr_ref, dli_ref, dd_ref, dwbr_ref, dwbi_ref, dwcr_ref, dwci_ref,
             gr_ref, gi_ref, cr_ref, ci_ref):
        step = pl.program_id(1)
        first = step == 0

        @pl.when(first)
        def _():
            cr_ref[...] = jnp.zeros_like(cr_ref)
            ci_ref[...] = jnp.zeros_like(ci_ref)

        dy_t, u = dy_ref[...], u_ref[...]
        dyb, ub = dy_t.astype(BF16), u.astype(BF16)
        gr_ref[...] = lax.dot_general(dyb, wcr_ref[...], contract_cols, preferred_element_type=F32)
        gi_ref[...] = lax.dot_general(dyb, wci_ref[...], contract_cols, preferred_element_type=F32)
        hr_halo = jnp.where(step == nt - 1, 0.0, hrp_ref[...])
        hi_halo = jnp.where(step == nt - 1, 0.0, hip_ref[...])
        s_re, s_im = _cscan_tile(gr_ref, gi_ref, tab_ref, pwr_ref, pwi_ref, cr_ref, ci_ref, reverse=True,
                                 h=(hr_ref, hi_ref, hr_halo, hi_halo))
        _accumulate(dlr_ref, jnp.sum(s_re, axis=0, keepdims=True), first)
        _accumulate(dli_ref, jnp.sum(s_im, axis=0, keepdims=True), first)
        grb, gib = gr_ref[...].astype(BF16), gi_ref[...].astype(BF16)
        du = (lax.dot_general(grb, wbr_ref[...], contract_cols, preferred_element_type=F32)
              + lax.dot_general(gib, wbi_ref[...], contract_cols, preferred_element_type=F32) + dy_t * d_ref[...])
        du_ref[...] = du.astype(du_ref.dtype)
        _accumulate(dd_ref, jnp.sum(dy_t * u, axis=0, keepdims=True), first)
        _accumulate(dwbr_ref, lax.dot_general(ub, grb, contract_rows, preferred_element_type=F32), first)
        _accumulate(dwbi_ref, lax.dot_general(ub, gib, contract_rows, preferred_element_type=F32), first)
        _accumulate(dwcr_ref, lax.dot_general(hr_ref[...].astype(BF16), dyb, contract_rows,
                                              preferred_element_type=F32), first)
        _accumulate(dwci_ref, lax.dot_general(hi_ref[...].astype(BF16), dyb, contract_rows,
                                              preferred_element_type=F32), first)

    def tix(i):
        return nt - 1 - i

    wb_spec = pl.BlockSpec((None, ku, kp), lambda j, i: (j, 0, 0))
    wc_spec = pl.BlockSpec((None, kp, ku), lambda j, i: (j, 0, 0))
    small = pl.BlockSpec((SUBLANES, kp), lambda j, i: (0, j))
    state = pl.BlockSpec((tm, kp), lambda j, i: (tix(i), j))
    halo = pl.BlockSpec((SUBLANES, kp), lambda j, i: (jnp.maximum(tix(i) * hb - 1, 0), j))
    chan = pl.BlockSpec((tm, ku), lambda j, i: (tix(i), j))
    svec = pl.BlockSpec((1, kp), lambda j, i: (0, j))
    cvec = pl.BlockSpec((1, ku), lambda j, i: (0, j))
    return pl.pallas_call(
        body, name=name, grid=(J, nt),
        in_specs=[chan, pl.BlockSpec((tm, ku), lambda j, i: (tix(i), u_blk + j)), state, halo, state, halo,
                  wb_spec, wb_spec, wc_spec, wc_spec, cvec, small, small, small],
        out_specs=[chan, svec, svec, cvec, wb_spec, wb_spec, wc_spec, wc_spec],
        out_shape=[jax.ShapeDtypeStruct((T, J * ku), BF16), jax.ShapeDtypeStruct((1, J * kp), F32),
                   jax.ShapeDtypeStruct((1, J * kp), F32), jax.ShapeDtypeStruct((1, J * ku), F32),
                   jax.ShapeDtypeStruct((J, ku, kp), F32), jax.ShapeDtypeStruct((J, ku, kp), F32),
                   jax.ShapeDtypeStruct((J, kp, ku), F32), jax.ShapeDtypeStruct((J, kp, ku), F32)],
        scratch_shapes=[pltpu.VMEM((tm, kp), F32), pltpu.VMEM((tm, kp), F32),
                        pltpu.VMEM((SUBLANES, kp), F32), pltpu.VMEM((SUBLANES, kp), F32)],
        compiler_params=_params(("parallel", "arbitrary"), 12 * tm * kp * 4),
    )(dy, z, h_re, h_re, h_im, h_im, wb_re, wb_im, wc_re, wc_im_neg, d_row, tab, pw_re, pw_im)


def _mesh_pos():
    return lax.axis_index("x"), lax.axis_index("y"), lax.axis_index("c")


def _dev_index(px, py, pc):
    return 4 * px + 2 * py + pc


def _all_gather(shards, name):
    n = len(shards)

    def body(*refs):
        ins, outs = refs[:n], refs[n:2 * n]
        send_sems, recv_sems, local_sems = refs[2 * n:]
        x, y, c = _mesh_pos()
        me, sibling = (x, y, c), (x, y, 1 - c)
        chips = [(1 - x, y), (x, 1 - y), (1 - x, 1 - y)]

        def copy(a, k, block, to, src=None):
            dst = outs[a].at[_dev_index(*block)]
            return pltpu.make_async_remote_copy(
                src_ref=dst if src is None else src, dst_ref=dst, send_sem=send_sems.at[a * 7 + k],
                recv_sem=recv_sems.at[a * 7 + k], device_id=to, device_id_type=MESH)

        mine = [pltpu.make_async_copy(ins[a], outs[a].at[_dev_index(*me)], local_sems.at[a]) for a in range(n)]
        for cp in mine:
            cp.start()
        first = []
        for a in range(n):
            first.append(copy(a, 0, me, sibling, src=ins[a]))
            first += [copy(a, 1 + j, me, (*chip, c), src=ins[a]) for j, chip in enumerate(chips)]
        for cp in first:
            cp.start()
        passed = []
        for j, chip in enumerate(chips):
            for a in range(n):
                copy(a, 1 + j, (*chip, c), me).wait_recv()
                fwd = copy(a, 4 + j, (*chip, c), sibling)
                fwd.start()
                passed.append(fwd)
        for a in range(n):
            copy(a, 0, sibling, me).wait_recv()
            for j, chip in enumerate(chips):
                copy(a, 4 + j, (*chip, 1 - c), me).wait_recv()
        for cp in first + passed:
            cp.wait_send()
        for cp in mine:
            cp.wait()

    return pl.pallas_call(
        body, name=name, in_specs=[ANY] * n, out_specs=[ANY] * n,
        out_shape=[jax.ShapeDtypeStruct((N_DEV,) + s.shape, s.dtype) for s in shards],
        scratch_shapes=[pltpu.SemaphoreType.DMA((7 * n,)), pltpu.SemaphoreType.DMA((7 * n,)),
                        pltpu.SemaphoreType.DMA((n,))],
    )(*shards)


def _exchange_blocks(parts, name):
    n = len(parts)
    relations = [(dx, dy, dc) for dx in (0, 1) for dy in (0, 1) for dc in (0, 1) if (dx, dy, dc) != (0, 0, 0)]

    def body(*refs):
        ins, outs = refs[:n], refs[n:2 * n]
        send_sems, recv_sems, local_sems = refs[2 * n:]
        x, y, c = _mesh_pos()
        me = _dev_index(x, y, c)
        mine = [pltpu.make_async_copy(ins[a].at[me], outs[a].at[me], local_sems.at[a]) for a in range(n)]
        for cp in mine:
            cp.start()
        copies = []
        for k, (dx, dy, dc) in enumerate(relations):
            peer = (x + dx - 2 * x * dx, y + dy - 2 * y * dy, c + dc - 2 * c * dc)
            for a in range(n):
                copies.append((pltpu.make_async_remote_copy(
                    src_ref=ins[a].at[_dev_index(*peer)], dst_ref=outs[a].at[me], send_sem=send_sems.at[a * 7 + k],
                    recv_sem=recv_sems.at[a * 7 + k], device_id=peer, device_id_type=MESH),
                    pltpu.make_async_remote_copy(
                    src_ref=ins[a].at[_dev_index(*peer)], dst_ref=outs[a].at[_dev_index(*peer)],
                    send_sem=send_sems.at[a * 7 + k], recv_sem=recv_sems.at[a * 7 + k], device_id=peer,
                    device_id_type=MESH)))
        for send, _ in copies:
            send.start()
        for _, recv in copies:
            recv.wait_recv()
        for send, _ in copies:
            send.wait_send()
        for cp in mine:
            cp.wait()

    return pl.pallas_call(
        body, name=name, in_specs=[ANY] * n, out_specs=[ANY] * n,
        out_shape=[jax.ShapeDtypeStruct(p.shape, p.dtype) for p in parts],
        scratch_shapes=[pltpu.SemaphoreType.DMA((7 * n,)), pltpu.SemaphoreType.DMA((7 * n,)),
                        pltpu.SemaphoreType.DMA((n,))],
    )(*parts)


HBM = pl.BlockSpec(memory_space=pltpu.HBM)
SEM = pl.BlockSpec(memory_space=pltpu.SEMAPHORE)
EFFECT = pltpu.SideEffectType.DATAFLOW_SIDE_EFFECTING
RELATIONS = [(dx, dy, dc) for dx in (0, 1) for dy in (0, 1) for dc in (0, 1) if (dx, dy, dc) != (0, 0, 0)]


def _peer(rel):
    x, y, c = _mesh_pos()
    dx, dy, dc = rel
    return (x + dx - 2 * x * dx, y + dy - 2 * y * dy, c + dc - 2 * c * dc)


def _split_copy(src_ref, land_ref, send_sems, recv_sems, k, scatter, incoming):
    peer = _peer(RELATIONS[k])
    me = _dev_index(*_mesh_pos())
    src = src_ref.at[_dev_index(*peer)] if scatter else src_ref
    dst = land_ref.at[_dev_index(*peer) if incoming else me]
    return pltpu.make_async_remote_copy(src_ref=src, dst_ref=dst, send_sem=send_sems.at[k], recv_sem=recv_sems.at[k],
                                        device_id=peer, device_id_type=MESH)


def _exchange_start(srcs, lands, *, scatter, name):
    n = len(srcs)

    def body(*refs):
        src_refs, land_refs = refs[:n], refs[n:2 * n]
        send, recv = refs[2 * n:3 * n], refs[3 * n:4 * n]
        token = refs[-1]
        for k in range(len(RELATIONS)):
            for a in range(n):
                _split_copy(src_refs[a], land_refs[a], send[a], recv[a], k, scatter, incoming=False).start()
        token[...] = jnp.zeros_like(token)

    n_rel = len(RELATIONS)
    outs = pl.pallas_call(
        body, name=name, in_specs=[HBM] * (2 * n),
        out_shape=[pltpu.SemaphoreType.DMA((n_rel,))] * (2 * n)
        + [pltpu.HBM(s.shape, s.dtype) for s in srcs] + [pltpu.HBM(s.shape, s.dtype) for s in lands]
        + [jax.ShapeDtypeStruct((SUBLANES, LANES), F32)],
        out_specs=[SEM] * (2 * n) + [HBM] * (2 * n) + [pl.BlockSpec(memory_space=pltpu.VMEM)],
        input_output_aliases={**{a: 2 * n + a for a in range(n)}, **{n + a: 3 * n + a for a in range(n)}},
        compiler_params=pltpu.CompilerParams(has_side_effects=EFFECT),
    )(*[pltpu.with_memory_space_constraint(s, pltpu.HBM) for s in srcs],
      *[pltpu.with_memory_space_constraint(s, pltpu.HBM) for s in lands])
    per_array = [(outs[a], outs[n + a], outs[2 * n + a], outs[3 * n + a]) for a in range(n)]
    return per_array, outs[-1]


def _exchange_wait(handle, after, *, scatter, name):
    send_sems, recv_sems, src_thru, land_thru = handle

    def body(src_ref, land_ref, send, recv, after_ref, src_dead, got_ref):
        for k in range(len(RELATIONS)):
            cp = _split_copy(src_ref, land_ref, send, recv, k, scatter, incoming=True)
            cp.wait_send()
            cp.wait_recv()

    return pl.pallas_call(
        body, name=name, in_specs=[HBM, HBM, SEM, SEM, ANY],
        out_shape=[pltpu.HBM(src_thru.shape, src_thru.dtype), pltpu.HBM(land_thru.shape, land_thru.dtype)],
        out_specs=[HBM, HBM], input_output_aliases={0: 0, 1: 1},
        compiler_params=pltpu.CompilerParams(has_side_effects=EFFECT),
    )(src_thru, land_thru, send_sems, recv_sems, after)[1]


def _landing_zone(own_block):
    me = _dev_index(*_mesh_pos())
    zone = lax.empty((N_DEV,) + own_block.shape, own_block.dtype)
    return lax.dynamic_update_index_in_dim(zone, own_block, me, 0)


def _row_tile(rows, want):
    t = min(want, rows) // SUBLANES * SUBLANES
    while rows % t:
        t -= SUBLANES
    return t


def _sum_slots(recv, *, tr, name):
    s_, r_, c_ = recv.shape
    tr = _row_tile(r_, tr)

    def body(g_ref, o_ref):
        acc = g_ref[0]
        for s in range(1, s_):
            acc = acc + g_ref[s]
        o_ref[...] = acc

    return pl.pallas_call(
        body, name=name, grid=(r_ // tr,),
        in_specs=[pl.BlockSpec((s_, tr, c_), lambda i: (0, i, 0))],
        out_specs=pl.BlockSpec((tr, c_), lambda i: (i, 0)),
        out_shape=jax.ShapeDtypeStruct((r_, c_), F32),
        compiler_params=_params(("parallel",), (2 * s_ + 3) * tr * c_ * 4),
    )(recv)


def _adamw(recv, w, m, v, *, tr, name):
    s_, r_, c_ = recv.shape
    tr = _row_tile(r_, tr)
    assert w.shape == (r_, c_), (name, w.shape, recv.shape)
    c1 = 1.0 - ADAM_B1 ** ADAM_STEP
    c2 = 1.0 - ADAM_B2 ** ADAM_STEP

    def body(g_ref, w_ref, m_ref, v_ref, go_ref, d_ref, mo_ref, vo_ref):
        g = g_ref[0].astype(F32)
        for s in range(1, s_):
            g = g + g_ref[s].astype(F32)
        mn = ADAM_B1 * m_ref[...] + (1.0 - ADAM_B1) * g
        vn = ADAM_B2 * v_ref[...] + (1.0 - ADAM_B2) * (g * g)
        go_ref[...] = g
        mo_ref[...] = mn
        vo_ref[...] = vn
        d_ref[...] = -ADAM_LR * ((mn / c1) / (jnp.sqrt(vn / c2) + ADAM_EPS) + ADAM_WD * w_ref[...])

    tile = pl.BlockSpec((tr, c_), lambda i: (i, 0))
    return pl.pallas_call(
        body, name=name, grid=(r_ // tr,),
        in_specs=[pl.BlockSpec((s_, tr, c_), lambda i: (0, i, 0)), tile, tile, tile],
        out_specs=[tile] * 4, out_shape=[jax.ShapeDtypeStruct((r_, c_), F32)] * 4,
        compiler_params=_params(("parallel",), (2 * s_ + 16) * tr * c_ * 4),
    )(recv, w, m, v)


def _s5_discretise(a_re, a_im, log_dt, b_re, b_im):
    dt = jnp.exp(log_dt)[:, None]
    lr = jnp.minimum(a_re, -1e-4)
    li = a_im
    mag = jnp.exp(lr * dt)
    lbr = mag * jnp.cos(li * dt)
    lbi = mag * jnp.sin(li * dt)
    zr, zi = lbr - 1.0, lbi
    den = lr * lr + li * li
    fr = (zr * lr + zi * li) / den
    fi = (zi * lr - zr * li) / den
    bbr = fr[..., None] * b_re - fi[..., None] * b_im
    bbi = fr[..., None] * b_im + fi[..., None] * b_re
    return lbr, lbi, bbr, bbi


def _softplus_neg(lam):
    return jnp.maximum(-lam, 0.0) + jnp.log(1.0 + jnp.exp(-jnp.abs(lam)))


S5_Q = 8
RG_Q = 2


def _local_step(x, tgt, W, comm):
    T, D = x.shape
    C = D
    G, P, H = W["ssm_b_re"].shape
    S = G * H
    F = W["mlp_b_up"].shape[1]
    n_in = 2 * C + S + 2 * D
    heads, hd = W["rg_wa"].shape[0], W["rg_wa"].shape[1]
    u_off, ga_off, gb_off = 2 * C, 2 * C + S, 2 * C + S + D

    sp, sp_vjp = jax.vjp(_softplus_neg, W["rg_lambda"])
    (lbr, lbi, bbr, bbi), s5_vjp = jax.vjp(_s5_discretise, W["ssm_a_re"], W["ssm_a_im"], W["ssm_log_dt"],
                                           W["ssm_b_re"], W["ssm_b_im"])
    lam_re, lam_im = lbr.reshape(-1), lbi.reshape(-1)
    wa_bd = _bd_pack(W["rg_wa"], RG_Q).astype(BF16)
    wx_bd = _bd_pack(W["rg_wx"], RG_Q).astype(BF16)
    wb_re = _bd_pack(jnp.swapaxes(bbr, 1, 2), S5_Q).astype(BF16)
    wb_im = _bd_pack(jnp.swapaxes(bbi, 1, 2), S5_Q).astype(BF16)
    wc_re = _bd_pack(jnp.swapaxes(W["ssm_c_re"], 1, 2), S5_Q).astype(BF16)
    wc_im_neg = _bd_pack(jnp.swapaxes(-W["ssm_c_im"], 1, 2), S5_Q).astype(BF16)
    d_row = W["ssm_d"].reshape(1, S)
    jr, kr = heads // RG_Q, RG_Q * hd
    js, ku, kp = G // S5_Q, S5_Q * H, S5_Q * P

    x_bf = x.astype(BF16)
    w_in = comm.weight("w_in", None)
    z = _mm(x_bf, w_in, M=T, N=n_in, K=D, tm=512, tn=n_in // 4, tk=D, after=comm.gather_token, name="fwd_in_proj")
    xc = _conv_fwd(z, W["conv_w"], W["conv_b"], T=T, C=C, name="fwd_conv")
    r = _bd([(xc, 0, wa_bd)], T=T, J=jr, kb=kr, nb=kr, extras=[(W["rg_ba"], "vec", 0)],
            epilogue=lambda acc, b: (_sig(acc + b),), name="fwd_rgate")
    ig = _bd([(xc, 0, wx_bd)], T=T, J=jr, kb=kr, nb=kr, extras=[(W["rg_bx"], "vec", 0)],
             epilogue=lambda acc, b: (_sig(acc + b),), name="fwd_igate")
    h, p = _rg_scan_fwd(z, r, ig, xc, sp, T=T, C=C, gate_off=C, name="fwd_rg_scan")
    w_a_out = comm.weight("w_a_out", p)
    y_a = _mm(p, w_a_out, M=T, N=D, K=C, out_dtypes=(BF16,), tm=512, tn=D, tk=C, name="fwd_rg_out")

    h_re, h_im, y_s, yg = _s5_fwd(z, u_off, wb_re, wb_im, wc_re, wc_im_neg, d_row, lam_re, lam_im, T=T, name="fwd_s5")
    w_glu_w, w_glu_v = comm.weight("glu_w", yg), comm.weight("glu_v", yg)
    glu_a = _mm(yg, w_glu_w, M=T, N=D, K=S, out_dtypes=(BF16,), tm=1024, tn=D, tk=S, name="fwd_glu_w")
    glu_b = _mm(yg, w_glu_v, M=T, N=D, K=S, out_dtypes=(BF16,), tm=1024, tn=D, tk=S, name="fwd_glu_v")

    cwm = 1024

    def mix_fn(ga, gb, ya, a, b):
        return (_sig(ga) * ya + _sig(gb) * (a * _sig(b)),)

    mix = _ew(mix_fn, [(z, "tile", ga_off // cwm), (z, "tile", gb_off // cwm), (y_a, "tile", 0), (glu_a, "tile", 0),
                       (glu_b, "tile", 0)], T=T, C=D, n_out=1, out_dtypes=(BF16,), cw=cwm, name="fwd_mix")
    w_out = comm.weight("w_out", mix)
    s1 = _mm(mix, w_out, M=T, N=D, K=D, tm=512, tn=1024, tk=D, extras=[(x, "mn")],
             epilogue=lambda acc, xv: (ALPHA * xv + acc,), name="fwd_out_proj")

    def ln1_fn(s, g, b):
        xhat, _ = _ln_stats(s)
        y = xhat * g + b
        return y, y

    x1, x1_bf = _ew(ln1_fn, [(s1, "tile", 0), (W["ln1_g"], "vec", 0), (W["ln1_b"], "vec", 0)], T=T, C=D, n_out=2,
                    out_dtypes=(F32, BF16), tm=128, name="fwd_ln1")
    w_up = comm.weight("mlp_w_up", x1_bf)

    def mlp_up_fn(acc, b):
        hp = acc + b
        rl = jnp.maximum(hp, 0.0)
        return rl * rl, hp

    hact, hpre = _mm(x1_bf, w_up, M=T, N=F, K=D, tm=512, tn=1024, tk=D, extras=[(W["mlp_b_up"], "n")],
                     epilogue=mlp_up_fn, n_out=2, out_dtypes=(BF16, BF16), name="fwd_mlp_up")
    w_down = comm.weight("mlp_w_down", hact)
    s2 = _mm(hact, w_down, M=T, N=D, K=F, tm=512, tn=1024, tk=2048,
             extras=[(x1, "mn"), (W["mlp_b_down"], "n")], epilogue=lambda acc, xv, b: (ALPHA * xv + acc + b,),
             name="fwd_mlp_down")

    def ln2_fn(s, t, g, b):
        xhat, rstd = _ln_stats(s)
        err = xhat * g + b - t
        dy = err * (1.0 / D)
        ds = _ln_bwd(dy, g, xhat, rstd)
        return ds, ds, 0.5 * dy * err, dy * xhat, dy, ds

    ds2, ds2_bf, loss_cols, d_ln2_g, d_ln2_b, d_b_down = _ew(
        ln2_fn, [(s2, "tile", 0), (tgt, "tile", 0), (W["ln2_g"], "vec", 0), (W["ln2_b"], "vec", 0)],
        T=T, C=D, n_out=2, n_cs=4, out_dtypes=(F32, BF16), tm=128, name="bwd_loss_ln2")
    d_w_down = _mm(hact, ds2_bf, M=F, N=D, K=T, ta=True, out_dtypes=(BF16,), tm=1024, tn=1024, tk=2048, name="bwd_w_down")
    sent = comm.send_grad("mlp_w_down", d_w_down)

    def dhpre_fn(acc, hp):
        dv = acc * (2.0 * jnp.maximum(hp.astype(F32), 0.0))
        return dv, dv

    dhpre, d_b_up = _mm(ds2_bf, w_down, M=T, N=F, K=D, tb=True, tm=512, tn=1024, tk=D, extras=[(hpre, "mn")],
                        epilogue=dhpre_fn, n_cs=1, out_dtypes=(BF16,), after=sent, name="bwd_mlp_down")
    d_w_up = _mm(x1_bf, dhpre, M=D, N=F, K=T, ta=True, out_dtypes=(BF16,), n_split=N_DEV, tm=1024, tn=F // N_DEV, tk=2048, name="bwd_w_up")
    sent = comm.send_grad("mlp_w_up", d_w_up)
    dx1 = _mm(dhpre, w_up, M=T, N=D, K=F, tb=True, tm=512, tn=1024, tk=2048,
              extras=[(ds2, "mn")], epilogue=lambda acc, dv: (ALPHA * dv + acc,), after=sent, name="bwd_mlp_up")

    def ln1_bwd_fn(s, dy, g):
        xhat, rstd = _ln_stats(s)
        ds = _ln_bwd(dy, g, xhat, rstd)
        return ds, ds, dy * xhat, dy

    ds1, ds1_bf, d_ln1_g, d_ln1_b = _ew(ln1_bwd_fn, [(s1, "tile", 0), (dx1, "tile", 0), (W["ln1_g"], "vec", 0)],
                                        T=T, C=D, n_out=2, n_cs=2, out_dtypes=(F32, BF16), tm=128, name="bwd_ln1")
    d_w_out = _mm(mix, ds1_bf, M=D, N=D, K=T, ta=True, out_dtypes=(BF16,), tm=1024, tn=1024, tk=2048, name="bwd_w_out")
    sent = comm.send_grad("w_out", d_w_out)
    dmix = _mm(ds1_bf, w_out, M=T, N=D, K=D, tb=True, tm=512, tn=D, tk=D, after=sent, name="bwd_out_proj")

    def mix_bwd_fn(dm, ga, gb, ya, a, b):
        sa, sb, sv = _sig(ga), _sig(gb), _sig(b)
        yb = a * sv
        dyb = dm * sb
        return (dm * ya * (sa * (1.0 - sa)), dm * yb * (sb * (1.0 - sb)), dm * sa, dyb * sv,
                dyb * a * (sv * (1.0 - sv)))

    dg_a, dg_b, dy_a, dglu_a, dglu_b = _ew(
        mix_bwd_fn, [(dmix, "tile", 0), (z, "tile", ga_off // cwm), (z, "tile", gb_off // cwm), (y_a, "tile", 0),
                     (glu_a, "tile", 0), (glu_b, "tile", 0)], T=T, C=D, n_out=5, out_dtypes=(BF16,) * 5, cw=cwm,
        name="bwd_mix")

    d_w_a_out = _mm(p, dy_a, M=C, N=D, K=T, ta=True, out_dtypes=(BF16,), tm=1024, tn=1024, tk=2048, name="bwd_w_a_out")
    sent = comm.send_grad("w_a_out", d_w_a_out)
    dp = _mm(dy_a, w_a_out, M=T, N=C, K=D, tb=True, tm=512, tn=C, tk=D, after=sent, name="bwd_rg_out")
    dgate, dra, dia, dxc0, d_ba, d_bx, d_sp = _rg_scan_bwd(dp, h, z, r, ig, xc, sp, T=T, C=C, gate_off=C,
                                                          name="bwd_rg_scan")
    dxc = _bd([(dra, 0, wa_bd), (dia, 0, wx_bd)], T=T, J=jr, kb=kr, nb=kr, tw=True, extras=[(dxc0, "tile", 0)],
              epilogue=lambda acc, d0: (acc + d0,), name="bwd_gates")
    d_wa = _bd_unpack(_bdw(xc, 0, dra, 0, T=T, J=jr, kb=kr, nb=kr, name="bwd_w_rgate"), RG_Q)
    d_wx = _bd_unpack(_bdw(xc, 0, dia, 0, T=T, J=jr, kb=kr, nb=kr, name="bwd_w_igate"), RG_Q)
    dxr, conv_sums = _conv_bwd(dxc, z, W["conv_w"], T=T, C=C, name="bwd_conv")
    d_conv_w, d_conv_b = conv_sums[0:4], conv_sums[4:5]
    (d_lambda,) = sp_vjp(d_sp)

    d_glu_w = _mm(yg, dglu_a, M=S, N=D, K=T, ta=True, out_dtypes=(BF16,), n_split=N_DEV, tm=1024, tn=D // N_DEV, tk=2048, name="bwd_w_glu_w")
    d_glu_v = _mm(yg, dglu_b, M=S, N=D, K=T, ta=True, out_dtypes=(BF16,), n_split=N_DEV, tm=1024, tn=D // N_DEV, tk=2048, name="bwd_w_glu_v")
    sent = comm.send_grad("glu_w", d_glu_w, "glu_v", d_glu_v)
    dyg0 = _mm(dglu_a, w_glu_w, M=T, N=S, K=D, tb=True, tm=512, tn=S, tk=D, after=sent, name="bwd_glu_w")
    dy_s = _mm(dglu_b, w_glu_v, M=T, N=S, K=D, tb=True, tm=512, tn=S, tk=D,
               extras=[(dyg0, "mn"), (y_s, "mn")], epilogue=lambda acc, d0, yv: ((acc + d0) * _dgelu(yv),),
               name="bwd_glu_v")
    du, d_lbr, d_lbi, d_ssm_d, d_wb_re, d_wb_im, d_wc_re, d_wc_im_neg = _s5_bwd(
        dy_s, z, u_off, h_re, h_im, wb_re, wb_im, wc_re, wc_im_neg, d_row, lam_re, lam_im, T=T, name="bwd_s5")
    d_bbr = jnp.swapaxes(_bd_unpack(d_wb_re, S5_Q), 1, 2)
    d_bbi = jnp.swapaxes(_bd_unpack(d_wb_im, S5_Q), 1, 2)
    d_a_re, d_a_im, d_log_dt, d_b_re, d_b_im = s5_vjp((d_lbr.reshape(G, P), d_lbi.reshape(G, P), d_bbr, d_bbi))
    d_c_re = jnp.swapaxes(_bd_unpack(d_wc_re, S5_Q), 1, 2)
    d_c_im = -jnp.swapaxes(_bd_unpack(d_wc_im_neg, S5_Q), 1, 2)

    dz = jnp.concatenate([dxr.astype(BF16), dgate.astype(BF16), du, dg_a, dg_b], axis=1)
    d_w_in = _mm(x_bf, dz, M=D, N=n_in, K=T, ta=True, out_dtypes=(BF16,), n_split=N_DEV, tm=1024, tn=n_in // N_DEV, tk=2048, name="bwd_w_in")
    sent = comm.send_grad("w_in", d_w_in)
    grad_x = _mm(dz, w_in, M=T, N=D, K=n_in, tb=True, tm=512, tn=1024, tk=n_in // 4,
                 extras=[(ds1, "mn")], epilogue=lambda acc, dv: (ALPHA * dv + acc,), after=sent, name="bwd_in_proj")

    grads = dict(
        conv_w=d_conv_w, conv_b=d_conv_b, rg_wa=d_wa, rg_ba=d_ba, rg_wx=d_wx, rg_bx=d_bx,
        rg_lambda=d_lambda, ssm_a_re=d_a_re, ssm_a_im=d_a_im, ssm_log_dt=d_log_dt,
        ssm_b_re=d_b_re, ssm_b_im=d_b_im, ssm_c_re=d_c_re, ssm_c_im=d_c_im, ssm_d=d_ssm_d.reshape(G, H),
        ln1_g=d_ln1_g, ln1_b=d_ln1_b, mlp_b_up=d_b_up, mlp_b_down=d_b_down, ln2_g=d_ln2_g, ln2_b=d_ln2_b)
    return jnp.sum(loss_cols), grad_x, grads


BIG = ("w_in", "w_a_out", "glu_w", "glu_v", "w_out", "mlp_w_up", "mlp_w_down")
COL_SHARDED = ("w_in", "glu_w", "glu_v", "mlp_w_up")
SMALL = ("conv_w", "conv_b", "rg_wa", "rg_ba", "rg_wx", "rg_bx", "rg_lambda", "ssm_a_re", "ssm_a_im", "ssm_log_dt",
         "ssm_b_re", "ssm_b_im", "ssm_c_re", "ssm_c_im", "ssm_d", "ln1_g", "ln1_b", "mlp_b_up", "mlp_b_down", "ln2_g",
         "ln2_b")
ORDER = ("w_in", "conv_w", "conv_b", "rg_wa", "rg_ba", "rg_wx", "rg_bx", "rg_lambda", "w_a_out", "ssm_a_re",
         "ssm_a_im", "ssm_log_dt", "ssm_b_re", "ssm_b_im", "ssm_c_re", "ssm_c_im", "ssm_d", "glu_w", "glu_v", "w_out",
         "ln1_g", "ln1_b", "mlp_w_up", "mlp_b_up", "mlp_w_down", "mlp_b_down", "ln2_g", "ln2_b")
TILE_ELEMS = SUBLANES * LANES


def _pack(arrs):
    pieces = []
    for a in arrs:
        flat = a.reshape(-1)
        flat = jnp.pad(flat, (0, (-flat.shape[0]) % TILE_ELEMS))
        pieces.append(flat.reshape(-1, LANES))
    rows = sum(p.shape[0] for p in pieces)
    pad_rows = (-rows) % (N_DEV * SUBLANES)
    if pad_rows:
        pieces.append(jnp.zeros((pad_rows, LANES), pieces[0].dtype))
    return jnp.concatenate(pieces, axis=0)


def _unpack(packed, shapes):
    out, row = [], 0
    for shp in shapes:
        n = math.prod(shp)
        rows = -(-n // TILE_ELEMS) * SUBLANES
        out.append(packed[row:row + rows].reshape(-1)[:n].reshape(shp))
        row += rows
    return out


class _Comm:
    def __init__(self, w):
        first = _all_gather([w["w_in"].astype(BF16), w["conv_w"]], name="gather_w_in")
        self._weights = {"w_in": first[0]}
        self.conv_w = jnp.swapaxes(first[1], 0, 1).reshape(w["conv_w"].shape[0], -1)
        later = [k for k in BIG if k != "w_in"]
        shards = [w[k].astype(BF16) for k in later]
        handles, self.gather_token = _exchange_start(shards, [_landing_zone(s) for s in shards], scatter=False,
                                                     name="gather_weights_start")
        self._gathers = dict(zip(later, handles))
        self._grads = {}

    def weight(self, k, after):
        if k not in self._weights:
            self._weights[k] = _exchange_wait(self._gathers.pop(k), after, scatter=False, name="gather_wait_" + k)
        gk = self._weights[k]
        if k in COL_SHARDED:
            return jnp.swapaxes(gk, 0, 1).reshape(gk.shape[1], -1)
        return gk.reshape(-1, gk.shape[-1])

    def send_grad(self, *names_and_parts):
        names, parts = names_and_parts[0::2], names_and_parts[1::2]
        parts = [p if k in COL_SHARDED else p.reshape(N_DEV, p.shape[0] // N_DEV, p.shape[1])
                 for k, p in zip(names, parts)]
        me = _dev_index(*_mesh_pos())
        lands = [_landing_zone(lax.dynamic_index_in_dim(p, me, 0, keepdims=False)) for p in parts]
        handles, token = _exchange_start(parts, lands, scatter=True, name="grad_start_" + names[0])
        self._grads.update(zip(names, handles))
        return token

    def received_grad(self, k, after):
        return _exchange_wait(self._grads.pop(k), after, scatter=True, name="grad_wait_" + k)


def _step(x, tgt, w, m, v):
    dev = _dev_index(*_mesh_pos())

    comm = _Comm(w)
    small = dict(w)
    small["conv_w"] = comm.conv_w
    for k in ("conv_b", "rg_ba", "rg_bx", "rg_lambda", "ln1_g", "ln1_b", "mlp_b_up", "mlp_b_down", "ln2_g", "ln2_b"):
        small[k] = w[k].reshape(1, -1)

    loss_part, grad_x, grads = _local_step(x, tgt, small, comm)

    out_g, out_d, out_m, out_v = {}, {}, {}, {}
    for k in BIG:
        rk = comm.received_grad(k, grad_x)
        out_g[k], out_d[k], out_m[k], out_v[k] = _adamw(rk, w[k], m[k], v[k], tr=128, name="adamw_" + k)

    small_shapes = [grads[k].shape for k in SMALL]
    (small_recv,) = _exchange_blocks([_pack([grads[k] for k in SMALL]).reshape(N_DEV, -1, LANES)],
                                     name="exchange_small_grads")
    small_block = _sum_slots(small_recv, tr=512, name="sum_small_grads")
    (small_all,) = _all_gather([small_block], name="gather_small_grads")
    g_small = dict(zip(SMALL, _unpack(small_all.reshape(-1, LANES), small_shapes)))
    cw_cols = w["conv_w"].shape[1]
    g_small["conv_w"] = lax.dynamic_slice_in_dim(g_small["conv_w"], dev * cw_cols, cw_cols, axis=1)
    shapes = [w[k].shape for k in SMALL]
    g_pack, w_pack, m_pack, v_pack = [_pack([src[k] for k in SMALL]) for src in (g_small, w, m, v)]
    res = _adamw(g_pack[None], w_pack, m_pack, v_pack, tr=1024, name="adamw_small")
    for dst, packed in zip((out_g, out_d, out_m, out_v), res):
        dst.update(zip(SMALL, _unpack(packed, shapes)))

    loss = lax.psum(loss_part, ("x", "y", "c"))
    return loss, grad_x, out_g, out_d, out_m, out_v


def kernel(x, w_in, conv_w, conv_b, rg_wa, rg_ba, rg_wx, rg_bx, rg_lambda, w_a_out, ssm_a_re, ssm_a_im, ssm_log_dt, ssm_b_re, ssm_b_im, ssm_c_re, ssm_c_im, ssm_d, glu_w, glu_v, w_out, ln1_g, ln1_b, mlp_w_up, mlp_b_up, mlp_w_down, mlp_b_down, ln2_g, ln2_b, loss_target, m_w_in, m_conv_w, m_conv_b, m_rg_wa, m_rg_ba, m_rg_wx, m_rg_bx, m_rg_lambda, m_w_a_out, m_ssm_a_re, m_ssm_a_im, m_ssm_log_dt, m_ssm_b_re, m_ssm_b_im, m_ssm_c_re, m_ssm_c_im, m_ssm_d, m_glu_w, m_glu_v, m_w_out, m_ln1_g, m_ln1_b, m_mlp_w_up, m_mlp_b_up, m_mlp_w_down, m_mlp_b_down, m_ln2_g, m_ln2_b, v_w_in, v_conv_w, v_conv_b, v_rg_wa, v_rg_ba, v_rg_wx, v_rg_bx, v_rg_lambda, v_w_a_out, v_ssm_a_re, v_ssm_a_im, v_ssm_log_dt, v_ssm_b_re, v_ssm_b_im, v_ssm_c_re, v_ssm_c_im, v_ssm_d, v_glu_w, v_glu_v, v_w_out, v_ln1_g, v_ln1_b, v_mlp_w_up, v_mlp_b_up, v_mlp_w_down, v_mlp_b_down, v_ln2_g, v_ln2_b):
    args = locals()
    w = {k: args[k][0] for k in ORDER}
    m = {k: args["m_" + k][0] for k in ORDER}
    v = {k: args["v_" + k][0] for k in ORDER}
    loss, grad_x, out_g, out_d, out_m, out_v = _step(x[0], loss_target[0], w, m, v)
    outs = [loss, grad_x[None]]
    for group in (out_g, out_d, out_m, out_v):
        outs += [group[k].reshape(args[k].shape) for k in ORDER]
    return tuple(outs)
```

```python
import functools
import math

import jax
import jax.numpy as jnp
from jax import lax
from jax.experimental import pallas as pl
from jax.experimental.pallas import tpu as pltpu

F32 = jnp.float32
BF16 = jnp.bfloat16
MESH = pl.DeviceIdType.MESH
N_DEV = 8
SUBLANES = 8
LANES = 128
VMEM_BYTES_V7X = 64 * 2 ** 20
VMEM_CAP = VMEM_BYTES_V7X - 8 * 2 ** 20

ALPHA = 2.0 ** 0.25
LN_EPS = 1e-5
RG_C = 8.0
ADAM_LR, ADAM_B1, ADAM_B2, ADAM_EPS, ADAM_WD, ADAM_STEP = 0.001, 0.9, 0.999, 1e-08, 0.01, 10
GELU_C = math.sqrt(2.0 / math.pi)
GELU_K = 0.044715

ANY = pl.BlockSpec(memory_space=pl.ANY)


def _params(sem, vmem_bytes):
    limit = int(min(max(2 * vmem_bytes, 16 * 2 ** 20), VMEM_CAP))
    return pltpu.CompilerParams(dimension_semantics=sem, vmem_limit_bytes=limit)


def _sig(x):
    return 1.0 / (1.0 + jnp.exp(-x))


def _gelu(x):
    return 0.5 * x * (1.0 + jnp.tanh(GELU_C * (x + GELU_K * x * x * x)))


def _dgelu(x):
    th = jnp.tanh(GELU_C * (x + GELU_K * x * x * x))
    return 0.5 * (1.0 + th) + 0.5 * x * (1.0 - th * th) * (GELU_C * (1.0 + 3.0 * GELU_K * x * x))


def _expm1(x):
    p = x * (1.0 + x * (1 / 2 + x * (1 / 6 + x * (1 / 24 + x * (1 / 120 + x * (1 / 720 + x * (1 / 5040)))))))
    return jnp.where(jnp.abs(x) < 0.25, p, jnp.exp(x) - 1.0)


def _accumulate(ref, val, first):
    @pl.when(first)
    def _():
        ref[...] = val

    @pl.when(jnp.logical_not(first))
    def _():
        ref[...] += val


def _rows8(cw):
    return lax.broadcasted_iota(jnp.int32, (SUBLANES, cw), 0)


def _shift_down(cur, prev, s, rows):
    return jnp.where(rows < s, pltpu.roll(prev, s, 0), pltpu.roll(cur, s, 0))


def _shift_up(cur, nxt, s, rows):
    return jnp.where(rows < SUBLANES - s, pltpu.roll(cur, SUBLANES - s, 0), pltpu.roll(nxt, SUBLANES - s, 0))


def _mm(a, b, *, M, N, K, ta=False, tb=False, b_split=1, n_split=1, a_fn=None, extras=(), epilogue=None,
        n_out=1, n_cs=0, out_dtypes=None, tm=512, tn=512, tk=512, after=None, name):
    tm, tn, tk = min(tm, M), min(tn, N), min(tk, K)
    assert M % tm == 0 and N % tn == 0 and K % tk == 0, (name, M, N, K, tm, tn, tk)
    nk = K // tk
    grid = (N // tn, M // tm, nk)
    a_spec = pl.BlockSpec((tk, tm), lambda j, i, k: (k, i)) if ta else pl.BlockSpec((tm, tk), lambda j, i, k: (i, k))
    if b_split == 1:
        b_spec = pl.BlockSpec((tn, tk), lambda j, i, k: (j, k)) if tb else pl.BlockSpec((tk, tn), lambda j, i, k: (k, j))
    elif tb:
        kb = (K // b_split) // tk
        assert kb * tk * b_split == K, name
        b_spec = pl.BlockSpec((None, tn, tk), lambda j, i, k: (k // kb, j, k % kb))
    else:
        nb = (N // b_split) // tn
        assert nb * tn * b_split == N, name
        b_spec = pl.BlockSpec((None, tk, tn), lambda j, i, k: (j // nb, k, j % nb))
    in_specs = [a_spec, b_spec]
    for arr, kind, *col_off in extras:
        off = col_off[0] if col_off else 0
        in_specs.append(pl.BlockSpec((tm, tn), lambda j, i, k, off=off: (i, off + j)) if kind == "mn"
                        else pl.BlockSpec((1, tn), lambda j, i, k: (0, j)))
    out_dtypes = (F32,) * n_out if out_dtypes is None else out_dtypes
    if n_split == 1:
        out_shape = [jax.ShapeDtypeStruct((M, N), dt) for dt in out_dtypes]
        out_specs = [pl.BlockSpec((tm, tn), lambda j, i, k: (i, j)) for _ in range(n_out)]
    else:
        assert n_out == 1
        nbo = (N // n_split) // tn
        assert nbo * tn * n_split == N, name
        out_shape = [jax.ShapeDtypeStruct((n_split, M, N // n_split), out_dtypes[0])]
        out_specs = [pl.BlockSpec((None, tm, tn), lambda j, i, k: (j // nbo, i, j % nbo))]
    out_shape += [jax.ShapeDtypeStruct((1, N), F32) for _ in range(n_cs)]
    out_specs += [pl.BlockSpec((1, tn), lambda j, i, k: (0, j)) for _ in range(n_cs)]
    ne = len(extras)
    dims = (((0 if ta else 1,), (1 if tb else 0,)), ((), ()))

    n_after = 0 if after is None else 1
    in_specs += [ANY] * n_after

    def body(*refs):
        a_ref, b_ref = refs[0], refs[1]
        ex_refs = refs[2:2 + ne]
        first_out = 2 + ne + n_after
        out_refs = refs[first_out:first_out + n_out]
        cs_refs = refs[first_out + n_out:first_out + n_out + n_cs]
        i, k = pl.program_id(1), pl.program_id(2)

        def product():
            av = a_ref[...]
            if a_fn is not None:
                av = a_fn(av.astype(F32))
            return lax.dot_general(av.astype(BF16), b_ref[...].astype(BF16), dims, preferred_element_type=F32)

        def finish(acc):
            res = (acc,) if epilogue is None else epilogue(acc, *[r[...] for r in ex_refs])
            for r, o in zip(out_refs, res[:n_out]):
                r[...] = o.astype(r.dtype)
            for r, cval in zip(cs_refs, res[n_out:]):
                _accumulate(r, jnp.sum(cval, axis=0, keepdims=True), i == 0)

        if nk == 1:
            finish(product())
            return
        acc_ref = refs[-1]

        @pl.when(k == 0)
        def _():
            acc_ref[...] = jnp.zeros_like(acc_ref)

        acc_ref[...] += product()

        @pl.when(k == nk - 1)
        def _():
            finish(acc_ref[...])

    vmem = 2 * tm * tk * a.dtype.itemsize + 2 * tk * tn * b.dtype.itemsize + (1 + 2 * n_out + 2 * ne + 2) * tm * tn * 4
    outs = pl.pallas_call(
        body, name=name, grid=grid, in_specs=in_specs, out_specs=out_specs, out_shape=out_shape,
        scratch_shapes=[pltpu.VMEM((tm, tn), F32)] if nk > 1 else [],
        compiler_params=_params(("parallel", "arbitrary", "arbitrary"), vmem),
    )(a, b, *[e[0] for e in extras], *([after] if n_after else []))
    return outs[0] if len(outs) == 1 else outs


BD_STEP = 4

def _bd(pairs, *, T, J, kb, nb, tw=False, extras=(), epilogue=None, n_out=1, n_cs=0, out_dtypes=None, tm=512, name):
    jb = BD_STEP
    assert T % tm == 0 and J % jb == 0
    grid = (J // jb, T // tm)
    npair, ne = len(pairs), len(extras)
    in_specs, args = [], []
    for arr, off, w in pairs:
        assert off % jb == 0, name
        in_specs.append(pl.BlockSpec((tm, jb * kb), lambda j, i, off=off // jb: (i, off + j)))
        in_specs.append(pl.BlockSpec((jb,) + tuple(w.shape[1:]), lambda j, i: (j, 0, 0)))
        args += [arr, w]
    for arr, kind, off in extras:
        assert off % jb == 0, name
        in_specs.append(pl.BlockSpec((tm, jb * nb), lambda j, i, off=off // jb: (i, off + j)) if kind == "tile"
                        else pl.BlockSpec((1, jb * nb), lambda j, i, off=off // jb: (0, off + j)))
        args.append(arr)
    out_dtypes = (F32,) * n_out if out_dtypes is None else out_dtypes
    out_shape = [jax.ShapeDtypeStruct((T, J * nb), dt) for dt in out_dtypes]
    out_specs = [pl.BlockSpec((tm, jb * nb), lambda j, i: (i, j)) for _ in range(n_out)]
    out_shape += [jax.ShapeDtypeStruct((1, J * nb), F32) for _ in range(n_cs)]
    out_specs += [pl.BlockSpec((1, jb * nb), lambda j, i: (0, j)) for _ in range(n_cs)]
    dims = (((1,), (1 if tw else 0,)), ((), ()))

    def body(*refs):
        ex_refs = refs[2 * npair:2 * npair + ne]
        out_refs = refs[2 * npair + ne:2 * npair + ne + n_out]
        cs_refs = refs[2 * npair + ne + n_out:]
        i = pl.program_id(1)
        for s in range(jb):
            cols_in, cols_out = pl.ds(s * kb, kb), pl.ds(s * nb, nb)
            acc = None
            for p in range(npair):
                d = lax.dot_general(refs[2 * p][:, cols_in].astype(BF16), refs[2 * p + 1][s].astype(BF16), dims,
                                    preferred_element_type=F32)
                acc = d if acc is None else acc + d
            res = (acc,) if epilogue is None else epilogue(acc, *[r[:, cols_out] for r in ex_refs])
            for r, o in zip(out_refs, res[:n_out]):
                r[:, cols_out] = o.astype(r.dtype)
            for r, cval in zip(cs_refs, res[n_out:]):
                _accumulate(r.at[:, cols_out], jnp.sum(cval, axis=0, keepdims=True), i == 0)

    vmem = jb * (2 * npair * tm * kb + 2 * npair * kb * nb + (2 * n_out + 2 * ne + 3) * tm * nb) * 4
    outs = pl.pallas_call(
        body, name=name, grid=grid, in_specs=in_specs, out_specs=out_specs, out_shape=out_shape,
        compiler_params=_params(("parallel", "arbitrary"), vmem),
    )(*args)
    return outs[0] if len(outs) == 1 else outs


def _bdw(a, a_off, b, b_off, *, T, J, kb, nb, tm=512, name):
    jb = BD_STEP
    assert T % tm == 0 and J % jb == 0 and a_off % jb == 0 and b_off % jb == 0
    a_blk, b_blk = a_off // jb, b_off // jb

    def body(a_ref, b_ref, o_ref):
        i = pl.program_id(1)
        for s in range(jb):
            d = lax.dot_general(a_ref[:, pl.ds(s * kb, kb)].astype(BF16), b_ref[:, pl.ds(s * nb, nb)].astype(BF16),
                                (((0,), (0,)), ((), ())), preferred_element_type=F32)
            _accumulate(o_ref.at[s], d, i == 0)

    return pl.pallas_call(
        body, name=name, grid=(J // jb, T // tm),
        in_specs=[pl.BlockSpec((tm, jb * kb), lambda j, i: (i, a_blk + j)),
                  pl.BlockSpec((tm, jb * nb), lambda j, i: (i, b_blk + j))],
        out_specs=pl.BlockSpec((jb, kb, nb), lambda j, i: (j, 0, 0)),
        out_shape=jax.ShapeDtypeStruct((J, kb, nb), F32),
        compiler_params=_params(("parallel", "arbitrary"), jb * (2 * tm * (kb + nb) + 3 * kb * nb) * 4),
    )(a, b)


def _bd_pack(w, q):
    g, a, b = w.shape
    eye = jnp.eye(q, dtype=w.dtype)
    return jnp.einsum("jqab,qr->jqarb", w.reshape(g // q, q, a, b), eye).reshape(g // q, q * a, q * b)


def _bd_unpack(wp, q):
    j, qa, qb = wp.shape
    a, b = qa // q, qb // q
    w5 = wp.reshape(j, q, a, q, b)
    return jnp.stack([w5[:, r, :, r, :] for r in range(q)], axis=1).reshape(j * q, a, b)


def _ew(fn, ins, *, T, C, n_out, n_cs=0, out_dtypes=None, tm=256, cw=None, name):
    cw = C if cw is None else cw
    assert T % tm == 0 and C % cw == 0
    grid = (C // cw, T // tm)
    in_specs = []
    for arr, kind, off in ins:
        in_specs.append(pl.BlockSpec((tm, cw), lambda j, i, off=off: (i, off + j)) if kind == "tile"
                        else pl.BlockSpec((arr.shape[0], cw), lambda j, i, off=off: (0, off + j)))
    out_dtypes = (F32,) * n_out if out_dtypes is None else out_dtypes
    out_shape = [jax.ShapeDtypeStruct((T, C), dt) for dt in out_dtypes]
    out_specs = [pl.BlockSpec((tm, cw), lambda j, i: (i, j)) for _ in range(n_out)]
    out_shape += [jax.ShapeDtypeStruct((1, C), F32) for _ in range(n_cs)]
    out_specs += [pl.BlockSpec((1, cw), lambda j, i: (0, j)) for _ in range(n_cs)]
    nin = len(ins)

    def body(*refs):
        i = pl.program_id(1)
        res = fn(*[r[...].astype(F32) for r in refs[:nin]])
        for r, o in zip(refs[nin:nin + n_out], res[:n_out]):
            r[...] = o.astype(r.dtype)
        for r, cval in zip(refs[nin + n_out:], res[n_out:]):
            _accumulate(r, jnp.sum(cval, axis=0, keepdims=True), i == 0)

    vmem = (2 * nin + 2 * n_out + 6) * tm * cw * 4
    outs = pl.pallas_call(
        body, name=name, grid=grid, in_specs=in_specs, out_specs=out_specs, out_shape=out_shape,
        compiler_params=_params(("parallel", "arbitrary"), vmem),
    )(*[arr for arr, _, _ in ins])
    return outs[0] if len(outs) == 1 else outs


def _ln_stats(s):
    mu = jnp.mean(s, axis=-1, keepdims=True)
    d = s - mu
    var = jnp.mean(d * d, axis=-1, keepdims=True)
    rstd = lax.rsqrt(var + LN_EPS)
    return d * rstd, rstd


def _ln_bwd(dy, g, xhat, rstd):
    dxh = dy * g
    m1 = jnp.mean(dxh, axis=-1, keepdims=True)
    m2 = jnp.mean(dxh * xhat, axis=-1, keepdims=True)
    return rstd * (dxh - m1 - xhat * m2)


def _conv_fwd(z, conv_w, conv_b, *, T, C, tm=512, cw=1024, name):
    ng, hb = tm // SUBLANES, tm // SUBLANES

    def body(x_ref, halo_ref, w_ref, b_ref, o_ref):
        it = pl.program_id(1)
        rows = _rows8(cw)
        halo = jnp.where(it == 0, 0.0, halo_ref[...])
        w = w_ref[...]
        bias = b_ref[...]

        def group(g, carry):
            off = pl.multiple_of(g * SUBLANES, SUBLANES)
            cur = x_ref[pl.ds(off, SUBLANES), :]
            prev = x_ref[pl.ds(pl.multiple_of(jnp.maximum(off - SUBLANES, 0), SUBLANES), SUBLANES), :]
            prev = jnp.where(g == 0, halo, prev)
            acc = cur * w[3:4] + bias
            for s in (1, 2, 3):
                acc = acc + _shift_down(cur, prev, s, rows) * w[3 - s:4 - s]
            o_ref[pl.ds(off, SUBLANES), :] = acc
            return carry

        lax.fori_loop(0, ng, group, 0)

    return pl.pallas_call(
        body, name=name, grid=(C // cw, T // tm),
        in_specs=[pl.BlockSpec((tm, cw), lambda j, i: (i, j)),
                  pl.BlockSpec((SUBLANES, cw), lambda j, i: (jnp.maximum(i * hb - 1, 0), j)),
                  pl.BlockSpec((4, cw), lambda j, i: (0, j)), pl.BlockSpec((1, cw), lambda j, i: (0, j))],
        out_specs=pl.BlockSpec((tm, cw), lambda j, i: (i, j)),
        out_shape=jax.ShapeDtypeStruct((T, C), F32),
        compiler_params=_params(("parallel", "arbitrary"), 5 * tm * cw * 4),
    )(z, z, conv_w, conv_b)


def _conv_bwd(dxc, z, conv_w, *, T, C, tm=512, cw=512, name):
    ng, hb, last = tm // SUBLANES, tm // SUBLANES, T // SUBLANES - 1
    nt = T // tm

    def body(d_ref, dn_ref, x_ref, xp_ref, w_ref, o_ref, sums_ref):
        it = pl.program_id(1)
        rows = _rows8(cw)
        dnext = jnp.where(it == nt - 1, 0.0, dn_ref[...])
        xprev = jnp.where(it == 0, 0.0, xp_ref[...])
        w = w_ref[...]

        def group(g, accs):
            off = pl.multiple_of(g * SUBLANES, SUBLANES)
            dcur = d_ref[pl.ds(off, SUBLANES), :]
            dnx = d_ref[pl.ds(pl.multiple_of(jnp.minimum(off + SUBLANES, tm - SUBLANES), SUBLANES), SUBLANES), :]
            dnx = jnp.where(g == ng - 1, dnext, dnx)
            xcur = x_ref[pl.ds(off, SUBLANES), :]
            xpv = x_ref[pl.ds(pl.multiple_of(jnp.maximum(off - SUBLANES, 0), SUBLANES), SUBLANES), :]
            xpv = jnp.where(g == 0, xprev, xpv)
            acc = dcur * w[3:4]
            for s in (1, 2, 3):
                acc = acc + _shift_up(dcur, dnx, s, rows) * w[3 - s:4 - s]
            o_ref[pl.ds(off, SUBLANES), :] = acc
            a0, a1, a2, a3, ab = accs
            a0 = a0 + dcur * _shift_down(xcur, xpv, 3, rows)
            a1 = a1 + dcur * _shift_down(xcur, xpv, 2, rows)
            a2 = a2 + dcur * _shift_down(xcur, xpv, 1, rows)
            a3 = a3 + dcur * xcur
            return a0, a1, a2, a3, ab + dcur

        zero = jnp.zeros((SUBLANES, cw), F32)
        accs = lax.fori_loop(0, ng, group, (zero,) * 5)
        sums = jnp.zeros((SUBLANES, cw), F32)
        for k, a in enumerate(accs):
            sums = jnp.where(rows == k, jnp.sum(a, axis=0, keepdims=True), sums)
        _accumulate(sums_ref, sums, it == 0)

    tile = pl.BlockSpec((tm, cw), lambda j, i: (i, j))
    return pl.pallas_call(
        body, name=name, grid=(C // cw, nt),
        in_specs=[tile, pl.BlockSpec((SUBLANES, cw), lambda j, i: (jnp.minimum((i + 1) * hb, last), j)),
                  tile, pl.BlockSpec((SUBLANES, cw), lambda j, i: (jnp.maximum(i * hb - 1, 0), j)),
                  pl.BlockSpec((4, cw), lambda j, i: (0, j))],
        out_specs=[tile, pl.BlockSpec((SUBLANES, cw), lambda j, i: (0, j))],
        out_shape=[jax.ShapeDtypeStruct((T, C), F32), jax.ShapeDtypeStruct((SUBLANES, C), F32)],
        compiler_params=_params(("parallel", "arbitrary"), 7 * tm * cw * 4),
    )(dxc, dxc, z, z, conv_w)


def _rg_coeffs(r, ig, xc, sp):
    la = (-RG_C) * r * sp
    a = jnp.exp(la)
    m = jnp.sqrt(-_expm1(2.0 * la))
    return a, m, m * (ig * xc)


def _rg_scan_fwd(z, r, ig, xc, sp, *, T, C, gate_off, tm=512, cw=256, name):
    rows16 = 2 * SUBLANES
    nq = tm // rows16

    def body(gate_ref, r_ref, i_ref, xc_ref, sp_ref, h_ref, p_ref, carry_ref):
        it = pl.program_id(1)

        @pl.when(it == 0)
        def _():
            carry_ref[...] = jnp.zeros_like(carry_ref)

        rows = _rows8(cw)
        sp_row = sp_ref[...]

        def pair(q, carry):
            base = pl.multiple_of(q * rows16, rows16)
            halves = []
            for half in range(2):
                sl = pl.ds(pl.multiple_of(base + half * SUBLANES, SUBLANES), SUBLANES)
                a, _, b = _rg_coeffs(r_ref[sl, :], i_ref[sl, :], xc_ref[sl, :], sp_row)
                for s in (1, 2, 4):
                    keep = rows >= s
                    sa = jnp.where(keep, pltpu.roll(a, s, 0), 1.0)
                    sb = jnp.where(keep, pltpu.roll(b, s, 0), 0.0)
                    b = b + a * sb
                    a = a * sa
                h = b + a * carry
                h_ref[sl, :] = h
                halves.append(h * _gelu(gate_ref[sl, :]))
                carry = h[SUBLANES - 1:SUBLANES, :]
            p_ref[pl.ds(base, rows16), :] = jnp.concatenate(halves, axis=0).astype(p_ref.dtype)
            return carry

        last = lax.fori_loop(0, nq, pair, carry_ref[0:1, :], unroll=2)
        carry_ref[...] = jnp.broadcast_to(last, carry_ref.shape)

    tile = pl.BlockSpec((tm, cw), lambda j, i: (i, j))
    gate_blk = gate_off // cw
    return pl.pallas_call(
        body, name=name, grid=(C // cw, T // tm),
        in_specs=[pl.BlockSpec((tm, cw), lambda j, i: (i, gate_blk + j)), tile, tile, tile,
                  pl.BlockSpec((1, cw), lambda j, i: (0, j))],
        out_specs=[tile, tile],
        out_shape=[jax.ShapeDtypeStruct((T, C), F32), jax.ShapeDtypeStruct((T, C), BF16)],
        scratch_shapes=[pltpu.VMEM((SUBLANES, cw), F32)],
        compiler_params=_params(("parallel", "arbitrary"), 12 * tm * cw * 4),
    )(z, r, ig, xc, sp)


def _rg_scan_bwd(dh, h, r, ig, xc, sp, *, T, C, tm=512, cw=256, name):
    ng, hb, nt = tm // SUBLANES, tm // SUBLANES, T // tm

    def body(dh_ref, h_ref, hp_ref, r_ref, i_ref, xc_ref, sp_ref,
             dra_ref, dia_ref, dxc_ref, cra_ref, cia_ref, csp_ref, cg_ref, ca_ref):
        step = pl.program_id(1)

        @pl.when(step == 0)
        def _():
            cg_ref[...] = jnp.zeros_like(cg_ref)
            ca_ref[...] = jnp.zeros_like(ca_ref)

        rows = _rows8(cw)
        sp_row = sp_ref[...]
        hhalo = jnp.where(step == nt - 1, 0.0, hp_ref[...])

        def group(gi, carry):
            g_next, a_next, s_ra, s_ia, s_sp = carry
            g = ng - 1 - gi
            off = pl.multiple_of(g * SUBLANES, SUBLANES)
            sl = pl.ds(off, SUBLANES)
            rr, ii, xx = r_ref[sl, :], i_ref[sl, :], xc_ref[sl, :]
            a, m, _ = _rg_coeffs(rr, ii, xx, sp_row)
            hh = h_ref[sl, :]
            hpv = h_ref[pl.ds(pl.multiple_of(jnp.maximum(off - SUBLANES, 0), SUBLANES), SUBLANES), :]
            hpv = jnp.where(g == 0, hhalo, hpv)
            hprev = _shift_down(hh, hpv, 1, rows)
            d = dh_ref[sl, :]
            c = jnp.where(rows < SUBLANES - 1, pltpu.roll(a, SUBLANES - 1, 0), a_next)
            for s in (1, 2, 4):
                keep = rows < SUBLANES - s
                sc = jnp.where(keep, pltpu.roll(c, SUBLANES - s, 0), 1.0)
                sd = jnp.where(keep, pltpu.roll(d, SUBLANES - s, 0), 0.0)
                d = d + c * sd
                c = c * sc
            gg = d + c * g_next
            da = gg * hprev
            dm = gg * (ii * xx)
            di = gg * (m * xx)
            dxc_ref[sl, :] = gg * (m * ii)
            dla = da * a - dm * (a * a / m)
            dra = dla * ((-RG_C) * sp_row) * (rr * (1.0 - rr))
            dia = di * (ii * (1.0 - ii))
            dra_ref[sl, :] = dra
            dia_ref[sl, :] = dia
            return (gg[0:1, :], a[0:1, :], s_ra + dra, s_ia + dia, s_sp + dla * ((-RG_C) * rr))

        zero = jnp.zeros((SUBLANES, cw), F32)
        g_first, a_first, s_ra, s_ia, s_sp = lax.fori_loop(
            0, ng, group, (cg_ref[0:1, :], ca_ref[0:1, :], zero, zero, zero), unroll=2)
        cg_ref[...] = jnp.broadcast_to(g_first, cg_ref.shape)
        ca_ref[...] = jnp.broadcast_to(a_first, ca_ref.shape)
        for ref, acc in ((cra_ref, s_ra), (cia_ref, s_ia), (csp_ref, s_sp)):
            _accumulate(ref, jnp.sum(acc, axis=0, keepdims=True), step == 0)

    tile = pl.BlockSpec((tm, cw), lambda j, i: (nt - 1 - i, j))
    vec = pl.BlockSpec((1, cw), lambda j, i: (0, j))
    return pl.pallas_call(
        body, name=name, grid=(C // cw, nt),
        in_specs=[tile, tile, pl.BlockSpec((SUBLANES, cw), lambda j, i: (jnp.maximum((nt - 1 - i) * hb - 1, 0), j)),
                  tile, tile, tile, vec],
        out_specs=[tile, tile, tile, vec, vec, vec],
        out_shape=[jax.ShapeDtypeStruct((T, C), F32)] * 3 + [jax.ShapeDtypeStruct((1, C), F32)] * 3,
        scratch_shapes=[pltpu.VMEM((SUBLANES, cw), F32), pltpu.VMEM((SUBLANES, cw), F32)],
        compiler_params=_params(("parallel", "arbitrary"), 20 * tm * cw * 4),
    )(dh, h, h, r, ig, xc, sp)


def _cscan_tables(lr, li, reverse):
    lam = (lr.reshape(-1), -li.reshape(-1) if reverse else li.reshape(-1))

    def mul(p, q):
        return p[0] * q[0] - p[1] * q[1], p[0] * q[1] + p[1] * q[0]

    pows = [lam]
    for _ in range(SUBLANES - 1):
        pows.append(mul(pows[-1], lam))
    zero = jnp.zeros_like(lam[0])
    tab = jnp.stack([pows[0][0], pows[0][1], pows[1][0], pows[1][1], pows[3][0], pows[3][1], zero, zero])
    if reverse:
        pows = pows[::-1]
    return tab, jnp.stack([p[0] for p in pows]), jnp.stack([p[1] for p in pows])


def _cscan_tile(xr_ref, xi_ref, tab_ref, pwr_ref, pwi_ref, cr_ref, ci_ref, *, reverse, h=None):
    tm, cw = xr_ref.shape
    ng = tm // SUBLANES
    rows = _rows8(cw)
    lam = [(tab_ref[2 * k:2 * k + 1, :], tab_ref[2 * k + 1:2 * k + 2, :]) for k in range(3)]
    pwr, pwi = pwr_ref[...], pwi_ref[...]

    def group(gi, carry):
        g = ng - 1 - gi if reverse else gi
        off = pl.multiple_of(g * SUBLANES, SUBLANES)
        sl = pl.ds(off, SUBLANES)
        xr, xi = xr_ref[sl, :], xi_ref[sl, :]
        for k, s in enumerate((1, 2, 4)):
            shift = SUBLANES - s if reverse else s
            keep = rows < SUBLANES - s if reverse else rows >= s
            sr = jnp.where(keep, pltpu.roll(xr, shift, 0), 0.0)
            si = jnp.where(keep, pltpu.roll(xi, shift, 0), 0.0)
            l_re, l_im = lam[k]
            xr, xi = xr + (l_re * sr - l_im * si), xi + (l_re * si + l_im * sr)
        cr, ci = carry[0], carry[1]
        xr, xi = xr + (pwr * cr - pwi * ci), xi + (pwr * ci + pwi * cr)
        xr_ref[sl, :] = xr
        xi_ref[sl, :] = xi
        edge = slice(0, 1) if reverse else slice(SUBLANES - 1, SUBLANES)
        out = (xr[edge, :], xi[edge, :])
        if h is not None:
            hr_ref, hi_ref, hr_halo, hi_halo = h
            poff = pl.multiple_of(jnp.maximum(off - SUBLANES, 0), SUBLANES)
            hrp = jnp.where(g == 0, hr_halo, hr_ref[pl.ds(poff, SUBLANES), :])
            hip = jnp.where(g == 0, hi_halo, hi_ref[pl.ds(poff, SUBLANES), :])
            hr1 = _shift_down(hr_ref[sl, :], hrp, 1, rows)
            hi1 = _shift_down(hi_ref[sl, :], hip, 1, rows)
            out += (carry[2] + (xr * hr1 + xi * hi1), carry[3] + (xi * hr1 - xr * hi1))
        return out

    init = (cr_ref[0:1, :], ci_ref[0:1, :])
    if h is not None:
        init += (jnp.zeros((SUBLANES, cw), F32),) * 2
    res = lax.fori_loop(0, ng, group, init)
    cr_ref[...] = jnp.broadcast_to(res[0], cr_ref.shape)
    ci_ref[...] = jnp.broadcast_to(res[1], ci_ref.shape)
    return res[2:]


def _s5_fwd(z, u_off, wb_re, wb_im, wc_re, wc_im_neg, d_row, lr, li, *, T, tm=512, name):
    J, ku, kp = wb_re.shape
    nt = T // tm
    tab, pw_re, pw_im = _cscan_tables(lr, li, False)
    u_blk = u_off // ku

    def body(u_ref, wbr_ref, wbi_ref, wcr_ref, wci_ref, d_ref, tab_ref, pwr_ref, pwi_ref,
             hr_ref, hi_ref, y_ref, yg_ref, cr_ref, ci_ref):
        @pl.when(pl.program_id(1) == 0)
        def _():
            cr_ref[...] = jnp.zeros_like(cr_ref)
            ci_ref[...] = jnp.zeros_like(ci_ref)

        u = u_ref[...]
        ub = u.astype(BF16)
        hr_ref[...] = jnp.dot(ub, wbr_ref[...], preferred_element_type=F32)
        hi_ref[...] = jnp.dot(ub, wbi_ref[...], preferred_element_type=F32)
        _cscan_tile(hr_ref, hi_ref, tab_ref, pwr_ref, pwi_ref, cr_ref, ci_ref, reverse=False)
        y = (jnp.dot(hr_ref[...].astype(BF16), wcr_ref[...], preferred_element_type=F32)
             + jnp.dot(hi_ref[...].astype(BF16), wci_ref[...], preferred_element_type=F32) + d_ref[...] * u)
        y_ref[...] = y
        yg_ref[...] = _gelu(y).astype(yg_ref.dtype)

    wb_spec = pl.BlockSpec((None, ku, kp), lambda j, i: (j, 0, 0))
    wc_spec = pl.BlockSpec((None, kp, ku), lambda j, i: (j, 0, 0))
    small = pl.BlockSpec((SUBLANES, kp), lambda j, i: (0, j))
    state = pl.BlockSpec((tm, kp), lambda j, i: (i, j))
    chan = pl.BlockSpec((tm, ku), lambda j, i: (i, j))
    return pl.pallas_call(
        body, name=name, grid=(J, nt),
        in_specs=[pl.BlockSpec((tm, ku), lambda j, i: (i, u_blk + j)), wb_spec, wb_spec, wc_spec, wc_spec,
                  pl.BlockSpec((1, ku), lambda j, i: (0, j)), small, small, small],
        out_specs=[state, state, chan, chan],
        out_shape=[jax.ShapeDtypeStruct((T, J * kp), F32)] * 2
        + [jax.ShapeDtypeStruct((T, J * ku), F32), jax.ShapeDtypeStruct((T, J * ku), BF16)],
        scratch_shapes=[pltpu.VMEM((SUBLANES, kp), F32), pltpu.VMEM((SUBLANES, kp), F32)],
        compiler_params=_params(("parallel", "arbitrary"), 10 * tm * kp * 4),
    )(z, wb_re, wb_im, wc_re, wc_im_neg, d_row, tab, pw_re, pw_im)


def _s5_bwd(dy, z, u_off, h_re, h_im, wb_re, wb_im, wc_re, wc_im_neg, d_row, lr, li, *, T, tm=512, name):
    J, ku, kp = wb_re.shape
    nt, hb = T // tm, tm // SUBLANES
    tab, pw_re, pw_im = _cscan_tables(lr, li, True)
    u_blk = u_off // ku
    contract_rows = (((0,), (0,)), ((), ()))
    contract_cols = (((1,), (1,)), ((), ()))

    def body(dy_ref, u_ref, hr_ref, hrp_ref, hi_ref, hip_ref, wbr_ref, wbi_ref, wcr_ref, wci_ref, d_ref,
             tab_ref, pwr_ref, pwi_ref, du_ref, dlr_ref, dli_ref, dd_ref, dwbr_ref, dwbi_ref, dwcr_ref, dwci_ref,
             gr_ref, gi_ref, cr_ref, ci_ref):
        step = pl.program_id(1)
        first = step == 0

        @pl.when(first)
        def _():
            cr_ref[...] = jnp.zeros_like(cr_ref)
            ci_ref[...] = jnp.zeros_like(ci_ref)

        dy_t, u = dy_ref[...], u_ref[...]
        dyb, ub = dy_t.astype(BF16), u.astype(BF16)
        gr_ref[...] = lax.dot_general(dyb, wcr_ref[...], contract_cols, preferred_element_type=F32)
        gi_ref[...] = lax.dot_general(dyb, wci_ref[...], contract_cols, preferred_element_type=F32)
        hr_halo = jnp.where(step == nt - 1, 0.0, hrp_ref[...])
        hi_halo = jnp.where(step == nt - 1, 0.0, hip_ref[...])
        s_re, s_im = _cscan_tile(gr_ref, gi_ref, tab_ref, pwr_ref, pwi_ref, cr_ref, ci_ref, reverse=True,
                                 h=(hr_ref, hi_ref, hr_halo, hi_halo))
        _accumulate(dlr_ref, jnp.sum(s_re, axis=0, keepdims=True), first)
        _accumulate(dli_ref, jnp.sum(s_im, axis=0, keepdims=True), first)
        grb, gib = gr_ref[...].astype(BF16), gi_ref[...].astype(BF16)
        du = (lax.dot_general(grb, wbr_ref[...], contract_cols, preferred_element_type=F32)
              + lax.dot_general(gib, wbi_ref[...], contract_cols, preferred_element_type=F32) + dy_t * d_ref[...])
        du_ref[...] = du.astype(du_ref.dtype)
        _accumulate(dd_ref, jnp.sum(dy_t * u, axis=0, keepdims=True), first)
        _accumulate(dwbr_ref, lax.dot_general(ub, grb, contract_rows, preferred_element_type=F32), first)
        _accumulate(dwbi_ref, lax.dot_general(ub, gib, contract_rows, preferred_element_type=F32), first)
        _accumulate(dwcr_ref, lax.dot_general(hr_ref[...].astype(BF16), dyb, contract_rows,
                                              preferred_element_type=F32), first)
        _accumulate(dwci_ref, lax.dot_general(hi_ref[...].astype(BF16), dyb, contract_rows,
                                              preferred_element_type=F32), first)

    def tix(i):
        return nt - 1 - i

    wb_spec = pl.BlockSpec((None, ku, kp), lambda j, i: (j, 0, 0))
    wc_spec = pl.BlockSpec((None, kp, ku), lambda j, i: (j, 0, 0))
    small = pl.BlockSpec((SUBLANES, kp), lambda j, i: (0, j))
    state = pl.BlockSpec((tm, kp), lambda j, i: (tix(i), j))
    halo = pl.BlockSpec((SUBLANES, kp), lambda j, i: (jnp.maximum(tix(i) * hb - 1, 0), j))
    chan = pl.BlockSpec((tm, ku), lambda j, i: (tix(i), j))
    svec = pl.BlockSpec((1, kp), lambda j, i: (0, j))
    cvec = pl.BlockSpec((1, ku), lambda j, i: (0, j))
    return pl.pallas_call(
        body, name=name, grid=(J, nt),
        in_specs=[chan, pl.BlockSpec((tm, ku), lambda j, i: (tix(i), u_blk + j)), state, halo, state, halo,
                  wb_spec, wb_spec, wc_spec, wc_spec, cvec, small, small, small],
        out_specs=[chan, svec, svec, cvec, wb_spec, wb_spec, wc_spec, wc_spec],
        out_shape=[jax.ShapeDtypeStruct((T, J * ku), BF16), jax.ShapeDtypeStruct((1, J * kp), F32),
                   jax.ShapeDtypeStruct((1, J * kp), F32), jax.ShapeDtypeStruct((1, J * ku), F32),
                   jax.ShapeDtypeStruct((J, ku, kp), F32), jax.ShapeDtypeStruct((J, ku, kp), F32),
                   jax.ShapeDtypeStruct((J, kp, ku), F32), jax.ShapeDtypeStruct((J, kp, ku), F32)],
        scratch_shapes=[pltpu.VMEM((tm, kp), F32), pltpu.VMEM((tm, kp), F32),
                        pltpu.VMEM((SUBLANES, kp), F32), pltpu.VMEM((SUBLANES, kp), F32)],
        compiler_params=_params(("parallel", "arbitrary"), 12 * tm * kp * 4),
    )(dy, z, h_re, h_re, h_im, h_im, wb_re, wb_im, wc_re, wc_im_neg, d_row, tab, pw_re, pw_im)


def _mesh_pos():
    return lax.axis_index("x"), lax.axis_index("y"), lax.axis_index("c")


def _dev_index(px, py, pc):
    return 4 * px + 2 * py + pc


def _all_gather(shards, name):
    n = len(shards)

    def body(*refs):
        ins, outs = refs[:n], refs[n:2 * n]
        send_sems, recv_sems, local_sems = refs[2 * n:]
        x, y, c = _mesh_pos()
        me, sibling = (x, y, c), (x, y, 1 - c)
        chips = [(1 - x, y), (x, 1 - y), (1 - x, 1 - y)]

        def copy(a, k, block, to, src=None):
            dst = outs[a].at[_dev_index(*block)]
            return pltpu.make_async_remote_copy(
                src_ref=dst if src is None else src, dst_ref=dst, send_sem=send_sems.at[a * 7 + k],
                recv_sem=recv_sems.at[a * 7 + k], device_id=to, device_id_type=MESH)

        mine = [pltpu.make_async_copy(ins[a], outs[a].at[_dev_index(*me)], local_sems.at[a]) for a in range(n)]
        for cp in mine:
            cp.start()
        first = []
        for a in range(n):
            first.append(copy(a, 0, me, sibling, src=ins[a]))
            first += [copy(a, 1 + j, me, (*chip, c), src=ins[a]) for j, chip in enumerate(chips)]
        for cp in first:
            cp.start()
        passed = []
        for j, chip in enumerate(chips):
            for a in range(n):
                copy(a, 1 + j, (*chip, c), me).wait_recv()
                fwd = copy(a, 4 + j, (*chip, c), sibling)
                fwd.start()
                passed.append(fwd)
        for a in range(n):
            copy(a, 0, sibling, me).wait_recv()
            for j, chip in enumerate(chips):
                copy(a, 4 + j, (*chip, 1 - c), me).wait_recv()
        for cp in first + passed:
            cp.wait_send()
        for cp in mine:
            cp.wait()

    return pl.pallas_call(
        body, name=name, in_specs=[ANY] * n, out_specs=[ANY] * n,
        out_shape=[jax.ShapeDtypeStruct((N_DEV,) + s.shape, s.dtype) for s in shards],
        scratch_shapes=[pltpu.SemaphoreType.DMA((7 * n,)), pltpu.SemaphoreType.DMA((7 * n,)),
                        pltpu.SemaphoreType.DMA((n,))],
    )(*shards)


def _exchange_blocks(parts, name):
    n = len(parts)
    relations = [(dx, dy, dc) for dx in (0, 1) for dy in (0, 1) for dc in (0, 1) if (dx, dy, dc) != (0, 0, 0)]

    def body(*refs):
        ins, outs = refs[:n], refs[n:2 * n]
        send_sems, recv_sems, local_sems = refs[2 * n:]
        x, y, c = _mesh_pos()
        me = _dev_index(x, y, c)
        mine = [pltpu.make_async_copy(ins[a].at[me], outs[a].at[me], local_sems.at[a]) for a in range(n)]
        for cp in mine:
            cp.start()
        copies = []
        for k, (dx, dy, dc) in enumerate(relations):
            peer = (x + dx - 2 * x * dx, y + dy - 2 * y * dy, c + dc - 2 * c * dc)
            for a in range(n):
                copies.append((pltpu.make_async_remote_copy(
                    src_ref=ins[a].at[_dev_index(*peer)], dst_ref=outs[a].at[me], send_sem=send_sems.at[a * 7 + k],
                    recv_sem=recv_sems.at[a * 7 + k], device_id=peer, device_id_type=MESH),
                    pltpu.make_async_remote_copy(
                    src_ref=ins[a].at[_dev_index(*peer)], dst_ref=outs[a].at[_dev_index(*peer)],
                    send_sem=send_sems.at[a * 7 + k], recv_sem=recv_sems.at[a * 7 + k], device_id=peer,
                    device_id_type=MESH)))
        for send, _ in copies:
            send.start()
        for _, recv in copies:
            recv.wait_recv()
        for send, _ in copies:
            send.wait_send()
        for cp in mine:
            cp.wait()

    return pl.pallas_call(
        body, name=name, in_specs=[ANY] * n, out_specs=[ANY] * n,
        out_shape=[jax.ShapeDtypeStruct(p.shape, p.dtype) for p in parts],
        scratch_shapes=[pltpu.SemaphoreType.DMA((7 * n,)), pltpu.SemaphoreType.DMA((7 * n,)),
                        pltpu.SemaphoreType.DMA((n,))],
    )(*parts)


HBM = pl.BlockSpec(memory_space=pltpu.HBM)
SEM = pl.BlockSpec(memory_space=pltpu.SEMAPHORE)
EFFECT = pltpu.SideEffectType.DATAFLOW_SIDE_EFFECTING
RELATIONS = [(dx, dy, dc) for dx in (0, 1) for dy in (0, 1) for dc in (0, 1) if (dx, dy, dc) != (0, 0, 0)]


def _peer(rel):
    x, y, c = _mesh_pos()
    dx, dy, dc = rel
    return (x + dx - 2 * x * dx, y + dy - 2 * y * dy, c + dc - 2 * c * dc)


def _split_copy(src_ref, land_ref, send_sems, recv_sems, k, scatter, incoming):
    peer = _peer(RELATIONS[k])
    me = _dev_index(*_mesh_pos())
    src = src_ref.at[_dev_index(*peer)] if scatter else src_ref
    dst = land_ref.at[_dev_index(*peer) if incoming else me]
    return pltpu.make_async_remote_copy(src_ref=src, dst_ref=dst, send_sem=send_sems.at[k], recv_sem=recv_sems.at[k],
                                        device_id=peer, device_id_type=MESH)


def _exchange_start(srcs, lands, *, scatter, name):
    n = len(srcs)

    def body(*refs):
        src_refs, land_refs = refs[:n], refs[n:2 * n]
        send, recv = refs[2 * n:3 * n], refs[3 * n:4 * n]
        token = refs[-1]
        for k in range(len(RELATIONS)):
            for a in range(n):
                _split_copy(src_refs[a], land_refs[a], send[a], recv[a], k, scatter, incoming=False).start()
        token[...] = jnp.zeros_like(token)

    n_rel = len(RELATIONS)
    outs = pl.pallas_call(
        body, name=name, in_specs=[HBM] * (2 * n),
        out_shape=[pltpu.SemaphoreType.DMA((n_rel,))] * (2 * n)
        + [pltpu.HBM(s.shape, s.dtype) for s in srcs] + [pltpu.HBM(s.shape, s.dtype) for s in lands]
        + [jax.ShapeDtypeStruct((SUBLANES, LANES), F32)],
        out_specs=[SEM] * (2 * n) + [HBM] * (2 * n) + [pl.BlockSpec(memory_space=pltpu.VMEM)],
        input_output_aliases={**{a: 2 * n + a for a in range(n)}, **{n + a: 3 * n + a for a in range(n)}},
        compiler_params=pltpu.CompilerParams(has_side_effects=EFFECT),
    )(*[pltpu.with_memory_space_constraint(s, pltpu.HBM) for s in srcs],
      *[pltpu.with_memory_space_constraint(s, pltpu.HBM) for s in lands])
    per_array = [(outs[a], outs[n + a], outs[2 * n + a], outs[3 * n + a]) for a in range(n)]
    return per_array, outs[-1]


def _exchange_wait(handle, after, *, scatter, name):
    send_sems, recv_sems, src_thru, land_thru = handle

    def body(src_ref, land_ref, send, recv, after_ref, src_dead, got_ref):
        for k in range(len(RELATIONS)):
            cp = _split_copy(src_ref, land_ref, send, recv, k, scatter, incoming=True)
            cp.wait_send()
            cp.wait_recv()

    return pl.pallas_call(
        body, name=name, in_specs=[HBM, HBM, SEM, SEM, ANY],
        out_shape=[pltpu.HBM(src_thru.shape, src_thru.dtype), pltpu.HBM(land_thru.shape, land_thru.dtype)],
        out_specs=[HBM, HBM], input_output_aliases={0: 0, 1: 1},
        compiler_params=pltpu.CompilerParams(has_side_effects=EFFECT),
    )(src_thru, land_thru, send_sems, recv_sems, after)[1]


def _landing_zone(own_block):
    me = _dev_index(*_mesh_pos())
    zone = lax.empty((N_DEV,) + own_block.shape, own_block.dtype)
    return lax.dynamic_update_index_in_dim(zone, own_block, me, 0)


def _row_tile(rows, want):
    t = min(want, rows) // SUBLANES * SUBLANES
    while rows % t:
        t -= SUBLANES
    return t


def _sum_slots(recv, *, tr, name):
    s_, r_, c_ = recv.shape
    tr = _row_tile(r_, tr)

    def body(g_ref, o_ref):
        acc = g_ref[0]
        for s in range(1, s_):
            acc = acc + g_ref[s]
        o_ref[...] = acc

    return pl.pallas_call(
        body, name=name, grid=(r_ // tr,),
        in_specs=[pl.BlockSpec((s_, tr, c_), lambda i: (0, i, 0))],
        out_specs=pl.BlockSpec((tr, c_), lambda i: (i, 0)),
        out_shape=jax.ShapeDtypeStruct((r_, c_), F32),
        compiler_params=_params(("parallel",), (2 * s_ + 3) * tr * c_ * 4),
    )(recv)


def _adamw(recv, w, m, v, *, tr, name):
    s_, r_, c_ = recv.shape
    tr = _row_tile(r_, tr)
    assert w.shape == (r_, c_), (name, w.shape, recv.shape)
    c1 = 1.0 - ADAM_B1 ** ADAM_STEP
    c2 = 1.0 - ADAM_B2 ** ADAM_STEP

    def body(g_ref, w_ref, m_ref, v_ref, go_ref, d_ref, mo_ref, vo_ref):
        g = g_ref[0].astype(F32)
        for s in range(1, s_):
            g = g + g_ref[s].astype(F32)
        mn = ADAM_B1 * m_ref[...] + (1.0 - ADAM_B1) * g
        vn = ADAM_B2 * v_ref[...] + (1.0 - ADAM_B2) * (g * g)
        go_ref[...] = g
        mo_ref[...] = mn
        vo_ref[...] = vn
        d_ref[...] = -ADAM_LR * ((mn / c1) / (jnp.sqrt(vn / c2) + ADAM_EPS) + ADAM_WD * w_ref[...])

    tile = pl.BlockSpec((tr, c_), lambda i: (i, 0))
    return pl.pallas_call(
        body, name=name, grid=(r_ // tr,),
        in_specs=[pl.BlockSpec((s_, tr, c_), lambda i: (0, i, 0)), tile, tile, tile],
        out_specs=[tile] * 4, out_shape=[jax.ShapeDtypeStruct((r_, c_), F32)] * 4,
        compiler_params=_params(("parallel",), (2 * s_ + 16) * tr * c_ * 4),
    )(recv, w, m, v)


def _s5_discretise(a_re, a_im, log_dt, b_re, b_im):
    dt = jnp.exp(log_dt)[:, None]
    lr = jnp.minimum(a_re, -1e-4)
    li = a_im
    mag = jnp.exp(lr * dt)
    lbr = mag * jnp.cos(li * dt)
    lbi = mag * jnp.sin(li * dt)
    zr, zi = lbr - 1.0, lbi
    den = lr * lr + li * li
    fr = (zr * lr + zi * li) / den
    fi = (zi * lr - zr * li) / den
    bbr = fr[..., None] * b_re - fi[..., None] * b_im
    bbi = fr[..., None] * b_im + fi[..., None] * b_re
    return lbr, lbi, bbr, bbi


def _softplus_neg(lam):
    return jnp.maximum(-lam, 0.0) + jnp.log(1.0 + jnp.exp(-jnp.abs(lam)))


S5_Q = 8
RG_Q = 2


def _local_step(x, tgt, W, comm):
    T, D = x.shape
    C = D
    G, P, H = W["ssm_b_re"].shape
    S = G * H
    F = W["mlp_b_up"].shape[1]
    n_in = 2 * C + S + 2 * D
    heads, hd = W["rg_wa"].shape[0], W["rg_wa"].shape[1]
    u_off, ga_off, gb_off = 2 * C, 2 * C + S, 2 * C + S + D

    sp, sp_vjp = jax.vjp(_softplus_neg, W["rg_lambda"])
    (lbr, lbi, bbr, bbi), s5_vjp = jax.vjp(_s5_discretise, W["ssm_a_re"], W["ssm_a_im"], W["ssm_log_dt"],
                                           W["ssm_b_re"], W["ssm_b_im"])
    lam_re, lam_im = lbr.reshape(-1), lbi.reshape(-1)
    wa_bd = _bd_pack(W["rg_wa"], RG_Q).astype(BF16)
    wx_bd = _bd_pack(W["rg_wx"], RG_Q).astype(BF16)
    wb_re = _bd_pack(jnp.swapaxes(bbr, 1, 2), S5_Q).astype(BF16)
    wb_im = _bd_pack(jnp.swapaxes(bbi, 1, 2), S5_Q).astype(BF16)
    wc_re = _bd_pack(jnp.swapaxes(W["ssm_c_re"], 1, 2), S5_Q).astype(BF16)
    wc_im_neg = _bd_pack(jnp.swapaxes(-W["ssm_c_im"], 1, 2), S5_Q).astype(BF16)
    d_row = W["ssm_d"].reshape(1, S)
    jr, kr = heads // RG_Q, RG_Q * hd
    js, ku, kp = G // S5_Q, S5_Q * H, S5_Q * P

    x_bf = x.astype(BF16)
    w_in = comm.weight("w_in", None)
    z = _mm(x_bf, w_in, M=T, N=n_in, K=D, tm=512, tn=n_in // 4, tk=D, after=comm.gather_token, name="fwd_in_proj")
    xc = _conv_fwd(z, W["conv_w"], W["conv_b"], T=T, C=C, name="fwd_conv")
    r = _bd([(xc, 0, wa_bd)], T=T, J=jr, kb=kr, nb=kr, extras=[(W["rg_ba"], "vec", 0)],
            epilogue=lambda acc, b: (_sig(acc + b),), name="fwd_rgate")
    ig = _bd([(xc, 0, wx_bd)], T=T, J=jr, kb=kr, nb=kr, extras=[(W["rg_bx"], "vec", 0)],
             epilogue=lambda acc, b: (_sig(acc + b),), name="fwd_igate")
    h, p = _rg_scan_fwd(z, r, ig, xc, sp, T=T, C=C, gate_off=C, name="fwd_rg_scan")
    w_a_out = comm.weight("w_a_out", p)
    y_a = _mm(p, w_a_out, M=T, N=D, K=C, out_dtypes=(BF16,), tm=512, tn=D, tk=C, name="fwd_rg_out")

    h_re, h_im, y_s, yg = _s5_fwd(z, u_off, wb_re, wb_im, wc_re, wc_im_neg, d_row, lam_re, lam_im, T=T, name="fwd_s5")
    w_glu_w, w_glu_v = comm.weight("glu_w", yg), comm.weight("glu_v", yg)
    glu_a = _mm(yg, w_glu_w, M=T, N=D, K=S, out_dtypes=(BF16,), tm=1024, tn=D, tk=S, name="fwd_glu_w")
    glu_b = _mm(yg, w_glu_v, M=T, N=D, K=S, out_dtypes=(BF16,), tm=1024, tn=D, tk=S, name="fwd_glu_v")

    cwm = 1024

    def mix_fn(ga, gb, ya, a, b):
        return (_sig(ga) * ya + _sig(gb) * (a * _sig(b)),)

    mix = _ew(mix_fn, [(z, "tile", ga_off // cwm), (z, "tile", gb_off // cwm), (y_a, "tile", 0), (glu_a, "tile", 0),
                       (glu_b, "tile", 0)], T=T, C=D, n_out=1, out_dtypes=(BF16,), cw=cwm, name="fwd_mix")
    w_out = comm.weight("w_out", mix)
    def out_ln1_fn(acc, xv, g, b):
        s = ALPHA * xv + acc
        xhat, _ = _ln_stats(s)
        y = xhat * g + b
        return s, y, y

    s1, x1, x1_bf = _mm(mix, w_out, M=T, N=D, K=D, tm=256, tn=D, tk=D,
                        extras=[(x, "mn"), (W["ln1_g"], "n"), (W["ln1_b"], "n")], epilogue=out_ln1_fn, n_out=3,
                        out_dtypes=(F32, F32, BF16), name="fwd_out_proj_ln1")
    w_up = comm.weight("mlp_w_up", x1_bf)

    def mlp_up_fn(acc, b):
        hp = acc + b
        rl = jnp.maximum(hp, 0.0)
        return rl * rl, hp

    hact, hpre = _mm(x1_bf, w_up, M=T, N=F, K=D, tm=512, tn=1024, tk=D, extras=[(W["mlp_b_up"], "n")],
                     epilogue=mlp_up_fn, n_out=2, out_dtypes=(BF16, BF16), name="fwd_mlp_up")
    w_down = comm.weight("mlp_w_down", hact)
    s2 = _mm(hact, w_down, M=T, N=D, K=F, tm=512, tn=1024, tk=4096,
             extras=[(x1, "mn"), (W["mlp_b_down"], "n")], epilogue=lambda acc, xv, b: (ALPHA * xv + acc + b,),
             name="fwd_mlp_down")

    def ln2_fn(s, t, g, b):
        xhat, rstd = _ln_stats(s)
        err = xhat * g + b - t
        dy = err * (1.0 / D)
        ds = _ln_bwd(dy, g, xhat, rstd)
        return ds, ds, 0.5 * dy * err, dy * xhat, dy, ds

    ds2, ds2_bf, loss_cols, d_ln2_g, d_ln2_b, d_b_down = _ew(
        ln2_fn, [(s2, "tile", 0), (tgt, "tile", 0), (W["ln2_g"], "vec", 0), (W["ln2_b"], "vec", 0)],
        T=T, C=D, n_out=2, n_cs=4, out_dtypes=(F32, BF16), tm=128, name="bwd_loss_ln2")
    d_w_down = _mm(hact, ds2_bf, M=F, N=D, K=T, ta=True, out_dtypes=(BF16,), tm=1024, tn=1024, tk=4096, name="bwd_w_down")
    sent = comm.send_grad("mlp_w_down", d_w_down)

    def dhpre_fn(acc, hp):
        dv = acc * (2.0 * jnp.maximum(hp.astype(F32), 0.0))
        return dv, dv

    dhpre, d_b_up = _mm(ds2_bf, w_down, M=T, N=F, K=D, tb=True, tm=512, tn=1024, tk=D, extras=[(hpre, "mn")],
                        epilogue=dhpre_fn, n_cs=1, out_dtypes=(BF16,), after=sent, name="bwd_mlp_down")
    d_w_up = _mm(x1_bf, dhpre, M=D, N=F, K=T, ta=True, out_dtypes=(BF16,), n_split=N_DEV, tm=1024, tn=F // N_DEV, tk=4096, name="bwd_w_up")
    sent = comm.send_grad("mlp_w_up", d_w_up)
    dx1 = _mm(dhpre, w_up, M=T, N=D, K=F, tb=True, tm=512, tn=1024, tk=4096,
              extras=[(ds2, "mn")], epilogue=lambda acc, dv: (ALPHA * dv + acc,), after=sent, name="bwd_mlp_up")

    def ln1_bwd_fn(s, dy, g):
        xhat, rstd = _ln_stats(s)
        ds = _ln_bwd(dy, g, xhat, rstd)
        return ds, ds, dy * xhat, dy

    ds1, ds1_bf, d_ln1_g, d_ln1_b = _ew(ln1_bwd_fn, [(s1, "tile", 0), (dx1, "tile", 0), (W["ln1_g"], "vec", 0)],
                                        T=T, C=D, n_out=2, n_cs=2, out_dtypes=(F32, BF16), tm=128, name="bwd_ln1")
    d_w_out = _mm(mix, ds1_bf, M=D, N=D, K=T, ta=True, out_dtypes=(BF16,), tm=1024, tn=1024, tk=4096, name="bwd_w_out")
    sent = comm.send_grad("w_out", d_w_out)
    def mix_bwd_fn(dm, ga, gb, ya, a, b):
        ya, a, b = ya.astype(F32), a.astype(F32), b.astype(F32)
        sa, sb, sv = _sig(ga), _sig(gb), _sig(b)
        yb = a * sv
        dyb = dm * sb
        return (dm * ya * (sa * (1.0 - sa)), dm * yb * (sb * (1.0 - sb)), dm * sa, dyb * sv,
                dyb * a * (sv * (1.0 - sv)))

    dg_a, dg_b, dy_a, dglu_a, dglu_b = _mm(
        ds1_bf, w_out, M=T, N=D, K=D, tb=True, tm=512, tn=cwm, tk=D,
        extras=[(z, "mn", ga_off // cwm), (z, "mn", gb_off // cwm), (y_a, "mn"), (glu_a, "mn"), (glu_b, "mn")],
        epilogue=mix_bwd_fn, n_out=5, out_dtypes=(BF16,) * 5, after=sent, name="bwd_out_proj_mix")

    d_w_a_out = _mm(p, dy_a, M=C, N=D, K=T, ta=True, out_dtypes=(BF16,), tm=1024, tn=1024, tk=4096, name="bwd_w_a_out")
    sent = comm.send_grad("w_a_out", d_w_a_out)
    def dp_fn(dp, hv, gate):
        th = jnp.tanh(GELU_C * (gate + GELU_K * gate * gate * gate))
        gelu = 0.5 * gate * (1.0 + th)
        dgelu = 0.5 * (1.0 + th) + 0.5 * gate * (1.0 - th * th) * (GELU_C * (1.0 + 3.0 * GELU_K * gate * gate))
        return dp * gelu, dp * hv * dgelu

    dh, dgate = _mm(dy_a, w_a_out, M=T, N=C, K=D, tb=True, tm=256, tn=C, tk=D, extras=[(h, "mn"), (z, "mn", 1)],
                    epilogue=dp_fn, n_out=2, out_dtypes=(F32, BF16), after=sent, name="bwd_rg_out")
    dra, dia, dxc0, d_ba, d_bx, d_sp = _rg_scan_bwd(dh, h, r, ig, xc, sp, T=T, C=C, name="bwd_rg_scan")
    dxc = _bd([(dra, 0, wa_bd), (dia, 0, wx_bd)], T=T, J=jr, kb=kr, nb=kr, tw=True, extras=[(dxc0, "tile", 0)],
              epilogue=lambda acc, d0: (acc + d0,), name="bwd_gates")
    d_wa = _bd_unpack(_bdw(xc, 0, dra, 0, T=T, J=jr, kb=kr, nb=kr, name="bwd_w_rgate"), RG_Q)
    d_wx = _bd_unpack(_bdw(xc, 0, dia, 0, T=T, J=jr, kb=kr, nb=kr, name="bwd_w_igate"), RG_Q)
    dxr, conv_sums = _conv_bwd(dxc, z, W["conv_w"], T=T, C=C, name="bwd_conv")
    d_conv_w, d_conv_b = conv_sums[0:4], conv_sums[4:5]
    (d_lambda,) = sp_vjp(d_sp)

    d_glu_w = _mm(yg, dglu_a, M=S, N=D, K=T, ta=True, out_dtypes=(BF16,), n_split=N_DEV, tm=1024, tn=D // N_DEV, tk=4096, name="bwd_w_glu_w")
    d_glu_v = _mm(yg, dglu_b, M=S, N=D, K=T, ta=True, out_dtypes=(BF16,), n_split=N_DEV, tm=1024, tn=D // N_DEV, tk=4096, name="bwd_w_glu_v")
    sent = comm.send_grad("glu_w", d_glu_w, "glu_v", d_glu_v)
    dyg0 = _mm(dglu_a, w_glu_w, M=T, N=S, K=D, tb=True, tm=512, tn=S, tk=D, after=sent, name="bwd_glu_w")
    dy_s = _mm(dglu_b, w_glu_v, M=T, N=S, K=D, tb=True, tm=512, tn=S, tk=D,
               extras=[(dyg0, "mn"), (y_s, "mn")], epilogue=lambda acc, d0, yv: ((acc + d0) * _dgelu(yv),),
               name="bwd_glu_v")
    du, d_lbr, d_lbi, d_ssm_d, d_wb_re, d_wb_im, d_wc_re, d_wc_im_neg = _s5_bwd(
        dy_s, z, u_off, h_re, h_im, wb_re, wb_im, wc_re, wc_im_neg, d_row, lam_re, lam_im, T=T, name="bwd_s5")
    d_bbr = jnp.swapaxes(_bd_unpack(d_wb_re, S5_Q), 1, 2)
    d_bbi = jnp.swapaxes(_bd_unpack(d_wb_im, S5_Q), 1, 2)
    d_a_re, d_a_im, d_log_dt, d_b_re, d_b_im = s5_vjp((d_lbr.reshape(G, P), d_lbi.reshape(G, P), d_bbr, d_bbi))
    d_c_re = jnp.swapaxes(_bd_unpack(d_wc_re, S5_Q), 1, 2)
    d_c_im = -jnp.swapaxes(_bd_unpack(d_wc_im_neg, S5_Q), 1, 2)

    dz = jnp.concatenate([dxr.astype(BF16), dgate.astype(BF16), du, dg_a, dg_b], axis=1)
    d_w_in = _mm(x_bf, dz, M=D, N=n_in, K=T, ta=True, out_dtypes=(BF16,), n_split=N_DEV, tm=1024, tn=n_in // N_DEV, tk=4096, name="bwd_w_in")
    sent = comm.send_grad("w_in", d_w_in)
    grad_x = _mm(dz, w_in, M=T, N=D, K=n_in, tb=True, tm=512, tn=1024, tk=n_in // 2,
                 extras=[(ds1, "mn")], epilogue=lambda acc, dv: (ALPHA * dv + acc,), after=sent, name="bwd_in_proj")

    grads = dict(
        conv_w=d_conv_w, conv_b=d_conv_b, rg_wa=d_wa, rg_ba=d_ba, rg_wx=d_wx, rg_bx=d_bx,
        rg_lambda=d_lambda, ssm_a_re=d_a_re, ssm_a_im=d_a_im, ssm_log_dt=d_log_dt,
        ssm_b_re=d_b_re, ssm_b_im=d_b_im, ssm_c_re=d_c_re, ssm_c_im=d_c_im, ssm_d=d_ssm_d.reshape(G, H),
        ln1_g=d_ln1_g, ln1_b=d_ln1_b, mlp_b_up=d_b_up, mlp_b_down=d_b_down, ln2_g=d_ln2_g, ln2_b=d_ln2_b)
    return jnp.sum(loss_cols), grad_x, grads


BIG = ("w_in", "w_a_out", "glu_w", "glu_v", "w_out", "mlp_w_up", "mlp_w_down")
COL_SHARDED = ("w_in", "glu_w", "glu_v", "mlp_w_up")
SMALL = ("conv_w", "conv_b", "rg_wa", "rg_ba", "rg_wx", "rg_bx", "rg_lambda", "ssm_a_re", "ssm_a_im", "ssm_log_dt",
         "ssm_b_re", "ssm_b_im", "ssm_c_re", "ssm_c_im", "ssm_d", "ln1_g", "ln1_b", "mlp_b_up", "mlp_b_down", "ln2_g",
         "ln2_b")
ORDER = ("w_in", "conv_w", "conv_b", "rg_wa", "rg_ba", "rg_wx", "rg_bx", "rg_lambda", "w_a_out", "ssm_a_re",
         "ssm_a_im", "ssm_log_dt", "ssm_b_re", "ssm_b_im", "ssm_c_re", "ssm_c_im", "ssm_d", "glu_w", "glu_v", "w_out",
         "ln1_g", "ln1_b", "mlp_w_up", "mlp_b_up", "mlp_w_down", "mlp_b_down", "ln2_g", "ln2_b")
TILE_ELEMS = SUBLANES * LANES


def _pack(arrs):
    pieces = []
    for a in arrs:
        flat = a.reshape(-1)
        flat = jnp.pad(flat, (0, (-flat.shape[0]) % TILE_ELEMS))
        pieces.append(flat.reshape(-1, LANES))
    rows = sum(p.shape[0] for p in pieces)
    pad_rows = (-rows) % (N_DEV * SUBLANES)
    if pad_rows:
        pieces.append(jnp.zeros((pad_rows, LANES), pieces[0].dtype))
    return jnp.concatenate(pieces, axis=0)


def _unpack(packed, shapes):
    out, row = [], 0
    for shp in shapes:
        n = math.prod(shp)
        rows = -(-n // TILE_ELEMS) * SUBLANES
        out.append(packed[row:row + rows].reshape(-1)[:n].reshape(shp))
        row += rows
    return out


class _Comm:
    def __init__(self, w):
        first = _all_gather([w["w_in"].astype(BF16), w["conv_w"]], name="gather_w_in")
        self._weights = {"w_in": first[0]}
        self.conv_w = jnp.swapaxes(first[1], 0, 1).reshape(w["conv_w"].shape[0], -1)
        later = [k for k in BIG if k != "w_in"]
        shards = [w[k].astype(BF16) for k in later]
        handles, self.gather_token = _exchange_start(shards, [_landing_zone(s) for s in shards], scatter=False,
                                                     name="gather_weights_start")
        self._gathers = dict(zip(later, handles))
        self._grads = {}

    def weight(self, k, after):
        if k not in self._weights:
            self._weights[k] = _exchange_wait(self._gathers.pop(k), after, scatter=False, name="gather_wait_" + k)
        gk = self._weights[k]
        if k in COL_SHARDED:
            return jnp.swapaxes(gk, 0, 1).reshape(gk.shape[1], -1)
        return gk.reshape(-1, gk.shape[-1])

    def send_grad(self, *names_and_parts):
        names, parts = names_and_parts[0::2], names_and_parts[1::2]
        parts = [p if k in COL_SHARDED else p.reshape(N_DEV, p.shape[0] // N_DEV, p.shape[1])
                 for k, p in zip(names, parts)]
        me = _dev_index(*_mesh_pos())
        lands = [_landing_zone(lax.dynamic_index_in_dim(p, me, 0, keepdims=False)) for p in parts]
        handles, token = _exchange_start(parts, lands, scatter=True, name="grad_start_" + names[0])
        self._grads.update(zip(names, handles))
        return token

    def received_grad(self, k, after):
        return _exchange_wait(self._grads.pop(k), after, scatter=True, name="grad_wait_" + k)


def _step(x, tgt, w, m, v):
    dev = _dev_index(*_mesh_pos())

    comm = _Comm(w)
    small = dict(w)
    small["conv_w"] = comm.conv_w
    for k in ("conv_b", "rg_ba", "rg_bx", "rg_lambda", "ln1_g", "ln1_b", "mlp_b_up", "mlp_b_down", "ln2_g", "ln2_b"):
        small[k] = w[k].reshape(1, -1)

    loss_part, grad_x, grads = _local_step(x, tgt, small, comm)

    out_g, out_d, out_m, out_v = {}, {}, {}, {}
    for k in BIG:
        rk = comm.received_grad(k, grad_x)
        out_g[k], out_d[k], out_m[k], out_v[k] = _adamw(rk, w[k], m[k], v[k], tr=128, name="adamw_" + k)

    small_shapes = [grads[k].shape for k in SMALL]
    (small_recv,) = _exchange_blocks([_pack([grads[k] for k in SMALL]).reshape(N_DEV, -1, LANES)],
                                     name="exchange_small_grads")
    small_block = _sum_slots(small_recv, tr=512, name="sum_small_grads")
    (small_all,) = _all_gather([small_block], name="gather_small_grads")
    g_small = dict(zip(SMALL, _unpack(small_all.reshape(-1, LANES), small_shapes)))
    cw_cols = w["conv_w"].shape[1]
    g_small["conv_w"] = lax.dynamic_slice_in_dim(g_small["conv_w"], dev * cw_cols, cw_cols, axis=1)
    shapes = [w[k].shape for k in SMALL]
    g_pack, w_pack, m_pack, v_pack = [_pack([src[k] for k in SMALL]) for src in (g_small, w, m, v)]
    res = _adamw(g_pack[None], w_pack, m_pack, v_pack, tr=1024, name="adamw_small")
    for dst, packed in zip((out_g, out_d, out_m, out_v), res):
        dst.update(zip(SMALL, _unpack(packed, shapes)))

    loss = lax.psum(loss_part, ("x", "y", "c"))
    return loss, grad_x, out_g, out_d, out_m, out_v


def kernel(x, w_in, conv_w, conv_b, rg_wa, rg_ba, rg_wx, rg_bx, rg_lambda, w_a_out, ssm_a_re, ssm_a_im, ssm_log_dt, ssm_b_re, ssm_b_im, ssm_c_re, ssm_c_im, ssm_d, glu_w, glu_v, w_out, ln1_g, ln1_b, mlp_w_up, mlp_b_up, mlp_w_down, mlp_b_down, ln2_g, ln2_b, loss_target, m_w_in, m_conv_w, m_conv_b, m_rg_wa, m_rg_ba, m_rg_wx, m_rg_bx, m_rg_lambda, m_w_a_out, m_ssm_a_re, m_ssm_a_im, m_ssm_log_dt, m_ssm_b_re, m_ssm_b_im, m_ssm_c_re, m_ssm_c_im, m_ssm_d, m_glu_w, m_glu_v, m_w_out, m_ln1_g, m_ln1_b, m_mlp_w_up, m_mlp_b_up, m_mlp_w_down, m_mlp_b_down, m_ln2_g, m_ln2_b, v_w_in, v_conv_w, v_conv_b, v_rg_wa, v_rg_ba, v_rg_wx, v_rg_bx, v_rg_lambda, v_w_a_out, v_ssm_a_re, v_ssm_a_im, v_ssm_log_dt, v_ssm_b_re, v_ssm_b_im, v_ssm_c_re, v_ssm_c_im, v_ssm_d, v_glu_w, v_glu_v, v_w_out, v_ln1_g, v_ln1_b, v_mlp_w_up, v_mlp_b_up, v_mlp_w_down, v_mlp_b_down, v_ln2_g, v_ln2_b):
    args = locals()
    w = {k: args[k][0] for k in ORDER}
    m = {k: args["m_" + k][0] for k in ORDER}
    v = {k: args["v_" + k][0] for k in ORDER}
    loss, grad_x, out_g, out_d, out_m, out_v = _step(x[0], loss_target[0], w, m, v)
    outs = [loss, grad_x[None]]
    for group in (out_g, out_d, out_m, out_v):
        outs += [group[k].reshape(args[k].shape) for k in ORDER]
    return tuple(outs)
```

```python
import functools
import math

import jax
import jax.numpy as jnp
from jax import lax
from jax.experimental import pallas as pl
from jax.experimental.pallas import tpu as pltpu

F32 = jnp.float32
BF16 = jnp.bfloat16
MESH = pl.DeviceIdType.MESH
N_DEV = 8
SUBLANES = 8
LANES = 128
VMEM_BYTES_V7X = 64 * 2 ** 20
VMEM_CAP = VMEM_BYTES_V7X - 8 * 2 ** 20

ALPHA = 2.0 ** 0.25
LN_EPS = 1e-5
RG_C = 8.0
ADAM_LR, ADAM_B1, ADAM_B2, ADAM_EPS, ADAM_WD, ADAM_STEP = 0.001, 0.9, 0.999, 1e-08, 0.01, 10
GELU_C = math.sqrt(2.0 / math.pi)
GELU_K = 0.044715

ANY = pl.BlockSpec(memory_space=pl.ANY)


def _params(sem, vmem_bytes):
    limit = int(min(max(2 * vmem_bytes, 16 * 2 ** 20), VMEM_CAP))
    return pltpu.CompilerParams(dimension_semantics=sem, vmem_limit_bytes=limit)


def _sig(x):
    return 1.0 / (1.0 + jnp.exp(-x))


def _gelu(x):
    return 0.5 * x * (1.0 + jnp.tanh(GELU_C * (x + GELU_K * x * x * x)))


def _dgelu(x):
    th = jnp.tanh(GELU_C * (x + GELU_K * x * x * x))
    return 0.5 * (1.0 + th) + 0.5 * x * (1.0 - th * th) * (GELU_C * (1.0 + 3.0 * GELU_K * x * x))


def _expm1(x):
    p = x * (1.0 + x * (1 / 2 + x * (1 / 6 + x * (1 / 24 + x * (1 / 120 + x * (1 / 720 + x * (1 / 5040)))))))
    return jnp.where(jnp.abs(x) < 0.25, p, jnp.exp(x) - 1.0)


def _accumulate(ref, val, first):
    @pl.when(first)
    def _():
        ref[...] = val

    @pl.when(jnp.logical_not(first))
    def _():
        ref[...] += val


def _rows8(cw):
    return lax.broadcasted_iota(jnp.int32, (SUBLANES, cw), 0)


def _shift_down(cur, prev, s, rows):
    return jnp.where(rows < s, pltpu.roll(prev, s, 0), pltpu.roll(cur, s, 0))


def _shift_up(cur, nxt, s, rows):
    return jnp.where(rows < SUBLANES - s, pltpu.roll(cur, SUBLANES - s, 0), pltpu.roll(nxt, SUBLANES - s, 0))


def _mm(a, b, *, M, N, K, ta=False, tb=False, b_split=1, n_split=1, a_fn=None, extras=(), epilogue=None,
        n_out=1, n_cs=0, out_dtypes=None, tm=512, tn=512, tk=512, after=None, name):
    tm, tn, tk = min(tm, M), min(tn, N), min(tk, K)
    assert M % tm == 0 and N % tn == 0 and K % tk == 0, (name, M, N, K, tm, tn, tk)
    nk = K // tk
    grid = (N // tn, M // tm, nk)
    a_spec = pl.BlockSpec((tk, tm), lambda j, i, k: (k, i)) if ta else pl.BlockSpec((tm, tk), lambda j, i, k: (i, k))
    if b_split == 1:
        b_spec = pl.BlockSpec((tn, tk), lambda j, i, k: (j, k)) if tb else pl.BlockSpec((tk, tn), lambda j, i, k: (k, j))
    elif tb:
        kb = (K // b_split) // tk
        assert kb * tk * b_split == K, name
        b_spec = pl.BlockSpec((None, tn, tk), lambda j, i, k: (k // kb, j, k % kb))
    else:
        nb = (N // b_split) // tn
        assert nb * tn * b_split == N, name
        b_spec = pl.BlockSpec((None, tk, tn), lambda j, i, k: (j // nb, k, j % nb))
    in_specs = [a_spec, b_spec]
    for arr, kind, *col_off in extras:
        off = col_off[0] if col_off else 0
        in_specs.append(pl.BlockSpec((tm, tn), lambda j, i, k, off=off: (i, off + j)) if kind == "mn"
                        else pl.BlockSpec((1, tn), lambda j, i, k: (0, j)))
    out_dtypes = (F32,) * n_out if out_dtypes is None else out_dtypes
    if n_split == 1:
        out_shape = [jax.ShapeDtypeStruct((M, N), dt) for dt in out_dtypes]
        out_specs = [pl.BlockSpec((tm, tn), lambda j, i, k: (i, j)) for _ in range(n_out)]
    else:
        assert n_out == 1
        nbo = (N // n_split) // tn
        assert nbo * tn * n_split == N, name
        out_shape = [jax.ShapeDtypeStruct((n_split, M, N // n_split), out_dtypes[0])]
        out_specs = [pl.BlockSpec((None, tm, tn), lambda j, i, k: (j // nbo, i, j % nbo))]
    out_shape += [jax.ShapeDtypeStruct((1, N), F32) for _ in range(n_cs)]
    out_specs += [pl.BlockSpec((1, tn), lambda j, i, k: (0, j)) for _ in range(n_cs)]
    ne = len(extras)
    dims = (((0 if ta else 1,), (1 if tb else 0,)), ((), ()))

    n_after = 0 if after is None else 1
    in_specs += [ANY] * n_after

    def body(*refs):
        a_ref, b_ref = refs[0], refs[1]
        ex_refs = refs[2:2 + ne]
        first_out = 2 + ne + n_after
        out_refs = refs[first_out:first_out + n_out]
        cs_refs = refs[first_out + n_out:first_out + n_out + n_cs]
        i, k = pl.program_id(1), pl.program_id(2)

        def product():
            av = a_ref[...]
            if a_fn is not None:
                av = a_fn(av.astype(F32))
            return lax.dot_general(av.astype(BF16), b_ref[...].astype(BF16), dims, preferred_element_type=F32)

        def finish(acc):
            res = (acc,) if epilogue is None else epilogue(acc, *[r[...] for r in ex_refs])
            for r, o in zip(out_refs, res[:n_out]):
                r[...] = o.astype(r.dtype)
            for r, cval in zip(cs_refs, res[n_out:]):
                _accumulate(r, jnp.sum(cval, axis=0, keepdims=True), i == 0)

        if nk == 1:
            finish(product())
            return
        acc_ref = refs[-1]

        @pl.when(k == 0)
        def _():
            acc_ref[...] = jnp.zeros_like(acc_ref)

        acc_ref[...] += product()

        @pl.when(k == nk - 1)
        def _():
            finish(acc_ref[...])

    vmem = 2 * tm * tk * a.dtype.itemsize + 2 * tk * tn * b.dtype.itemsize + (1 + 2 * n_out + 2 * ne + 2) * tm * tn * 4
    outs = pl.pallas_call(
        body, name=name, grid=grid, in_specs=in_specs, out_specs=out_specs, out_shape=out_shape,
        scratch_shapes=[pltpu.VMEM((tm, tn), F32)] if nk > 1 else [],
        compiler_params=_params(("parallel", "arbitrary", "arbitrary"), vmem),
    )(a, b, *[e[0] for e in extras], *([after] if n_after else []))
    return outs[0] if len(outs) == 1 else outs


BD_STEP = 4

def _bd(pairs, *, T, J, kb, nb, tw=False, extras=(), epilogue=None, n_out=1, n_cs=0, out_dtypes=None, tm=512, name):
    jb = BD_STEP
    assert T % tm == 0 and J % jb == 0
    grid = (J // jb, T // tm)
    npair, ne = len(pairs), len(extras)
    in_specs, args = [], []
    for arr, off, w in pairs:
        assert off % jb == 0, name
        in_specs.append(pl.BlockSpec((tm, jb * kb), lambda j, i, off=off // jb: (i, off + j)))
        in_specs.append(pl.BlockSpec((jb,) + tuple(w.shape[1:]), lambda j, i: (j, 0, 0)))
        args += [arr, w]
    for arr, kind, off in extras:
        assert off % jb == 0, name
        in_specs.append(pl.BlockSpec((tm, jb * nb), lambda j, i, off=off // jb: (i, off + j)) if kind == "tile"
                        else pl.BlockSpec((1, jb * nb), lambda j, i, off=off // jb: (0, off + j)))
        args.append(arr)
    out_dtypes = (F32,) * n_out if out_dtypes is None else out_dtypes
    out_shape = [jax.ShapeDtypeStruct((T, J * nb), dt) for dt in out_dtypes]
    out_specs = [pl.BlockSpec((tm, jb * nb), lambda j, i: (i, j)) for _ in range(n_out)]
    out_shape += [jax.ShapeDtypeStruct((1, J * nb), F32) for _ in range(n_cs)]
    out_specs += [pl.BlockSpec((1, jb * nb), lambda j, i: (0, j)) for _ in range(n_cs)]
    dims = (((1,), (1 if tw else 0,)), ((), ()))

    def body(*refs):
        ex_refs = refs[2 * npair:2 * npair + ne]
        out_refs = refs[2 * npair + ne:2 * npair + ne + n_out]
        cs_refs = refs[2 * npair + ne + n_out:]
        i = pl.program_id(1)
        for s in range(jb):
            cols_in, cols_out = pl.ds(s * kb, kb), pl.ds(s * nb, nb)
            acc = None
            for p in range(npair):
                d = lax.dot_general(refs[2 * p][:, cols_in].astype(BF16), refs[2 * p + 1][s].astype(BF16), dims,
                                    preferred_element_type=F32)
                acc = d if acc is None else acc + d
            res = (acc,) if epilogue is None else epilogue(acc, *[r[:, cols_out] for r in ex_refs])
            for r, o in zip(out_refs, res[:n_out]):
                r[:, cols_out] = o.astype(r.dtype)
            for r, cval in zip(cs_refs, res[n_out:]):
                _accumulate(r.at[:, cols_out], jnp.sum(cval, axis=0, keepdims=True), i == 0)

    vmem = jb * (2 * npair * tm * kb + 2 * npair * kb * nb + (2 * n_out + 2 * ne + 3) * tm * nb) * 4
    outs = pl.pallas_call(
        body, name=name, grid=grid, in_specs=in_specs, out_specs=out_specs, out_shape=out_shape,
        compiler_params=_params(("parallel", "arbitrary"), vmem),
    )(*args)
    return outs[0] if len(outs) == 1 else outs


def _bdw(a, a_off, b, b_off, *, T, J, kb, nb, tm=512, name):
    jb = BD_STEP
    assert T % tm == 0 and J % jb == 0 and a_off % jb == 0 and b_off % jb == 0
    a_blk, b_blk = a_off // jb, b_off // jb

    def body(a_ref, b_ref, o_ref):
        i = pl.program_id(1)
        for s in range(jb):
            d = lax.dot_general(a_ref[:, pl.ds(s * kb, kb)].astype(BF16), b_ref[:, pl.ds(s * nb, nb)].astype(BF16),
                                (((0,), (0,)), ((), ())), preferred_element_type=F32)
            _accumulate(o_ref.at[s], d, i == 0)

    return pl.pallas_call(
        body, name=name, grid=(J // jb, T // tm),
        in_specs=[pl.BlockSpec((tm, jb * kb), lambda j, i: (i, a_blk + j)),
                  pl.BlockSpec((tm, jb * nb), lambda j, i: (i, b_blk + j))],
        out_specs=pl.BlockSpec((jb, kb, nb), lambda j, i: (j, 0, 0)),
        out_shape=jax.ShapeDtypeStruct((J, kb, nb), F32),
        compiler_params=_params(("parallel", "arbitrary"), jb * (2 * tm * (kb + nb) + 3 * kb * nb) * 4),
    )(a, b)


def _bd_pack(w, q):
    g, a, b = w.shape
    eye = jnp.eye(q, dtype=w.dtype)
    return jnp.einsum("jqab,qr->jqarb", w.reshape(g // q, q, a, b), eye).reshape(g // q, q * a, q * b)


def _bd_unpack(wp, q):
    j, qa, qb = wp.shape
    a, b = qa // q, qb // q
    w5 = wp.reshape(j, q, a, q, b)
    return jnp.stack([w5[:, r, :, r, :] for r in range(q)], axis=1).reshape(j * q, a, b)


def _ew(fn, ins, *, T, C, n_out, n_cs=0, out_dtypes=None, tm=256, cw=None, name):
    cw = C if cw is None else cw
    assert T % tm == 0 and C % cw == 0
    grid = (C // cw, T // tm)
    in_specs = []
    for arr, kind, off in ins:
        in_specs.append(pl.BlockSpec((tm, cw), lambda j, i, off=off: (i, off + j)) if kind == "tile"
                        else pl.BlockSpec((arr.shape[0], cw), lambda j, i, off=off: (0, off + j)))
    out_dtypes = (F32,) * n_out if out_dtypes is None else out_dtypes
    out_shape = [jax.ShapeDtypeStruct((T, C), dt) for dt in out_dtypes]
    out_specs = [pl.BlockSpec((tm, cw), lambda j, i: (i, j)) for _ in range(n_out)]
    out_shape += [jax.ShapeDtypeStruct((1, C), F32) for _ in range(n_cs)]
    out_specs += [pl.BlockSpec((1, cw), lambda j, i: (0, j)) for _ in range(n_cs)]
    nin = len(ins)

    def body(*refs):
        i = pl.program_id(1)
        res = fn(*[r[...].astype(F32) for r in refs[:nin]])
        for r, o in zip(refs[nin:nin + n_out], res[:n_out]):
            r[...] = o.astype(r.dtype)
        for r, cval in zip(refs[nin + n_out:], res[n_out:]):
            _accumulate(r, jnp.sum(cval, axis=0, keepdims=True), i == 0)

    vmem = (2 * nin + 2 * n_out + 6) * tm * cw * 4
    outs = pl.pallas_call(
        body, name=name, grid=grid, in_specs=in_specs, out_specs=out_specs, out_shape=out_shape,
        compiler_params=_params(("parallel", "arbitrary"), vmem),
    )(*[arr for arr, _, _ in ins])
    return outs[0] if len(outs) == 1 else outs


def _ln_stats(s):
    mu = jnp.mean(s, axis=-1, keepdims=True)
    d = s - mu
    var = jnp.mean(d * d, axis=-1, keepdims=True)
    rstd = lax.rsqrt(var + LN_EPS)
    return d * rstd, rstd


def _ln_bwd(dy, g, xhat, rstd):
    dxh = dy * g
    m1 = jnp.mean(dxh, axis=-1, keepdims=True)
    m2 = jnp.mean(dxh * xhat, axis=-1, keepdims=True)
    return rstd * (dxh - m1 - xhat * m2)


def _conv_fwd(z, conv_w, conv_b, *, T, C, tm=512, cw=1024, name):
    ng, hb = tm // SUBLANES, tm // SUBLANES

    def body(x_ref, halo_ref, w_ref, b_ref, o_ref):
        it = pl.program_id(1)
        rows = _rows8(cw)
        halo = jnp.where(it == 0, 0.0, halo_ref[...])
        w = w_ref[...]
        bias = b_ref[...]

        def group(g, carry):
            off = pl.multiple_of(g * SUBLANES, SUBLANES)
            cur = x_ref[pl.ds(off, SUBLANES), :]
            prev = x_ref[pl.ds(pl.multiple_of(jnp.maximum(off - SUBLANES, 0), SUBLANES), SUBLANES), :]
            prev = jnp.where(g == 0, halo, prev)
            acc = cur * w[3:4] + bias
            for s in (1, 2, 3):
                acc = acc + _shift_down(cur, prev, s, rows) * w[3 - s:4 - s]
            o_ref[pl.ds(off, SUBLANES), :] = acc
            return carry

        lax.fori_loop(0, ng, group, 0)

    return pl.pallas_call(
        body, name=name, grid=(C // cw, T // tm),
        in_specs=[pl.BlockSpec((tm, cw), lambda j, i: (i, j)),
                  pl.BlockSpec((SUBLANES, cw), lambda j, i: (jnp.maximum(i * hb - 1, 0), j)),
                  pl.BlockSpec((4, cw), lambda j, i: (0, j)), pl.BlockSpec((1, cw), lambda j, i: (0, j))],
        out_specs=pl.BlockSpec((tm, cw), lambda j, i: (i, j)),
        out_shape=jax.ShapeDtypeStruct((T, C), F32),
        compiler_params=_params(("parallel", "arbitrary"), 5 * tm * cw * 4),
    )(z, z, conv_w, conv_b)


def _conv_bwd(dxc, z, conv_w, *, T, C, tm=512, cw=512, name):
    ng, hb, last = tm // SUBLANES, tm // SUBLANES, T // SUBLANES - 1
    nt = T // tm

    def body(d_ref, dn_ref, x_ref, xp_ref, w_ref, o_ref, sums_ref):
        it = pl.program_id(1)
        rows = _rows8(cw)
        dnext = jnp.where(it == nt - 1, 0.0, dn_ref[...])
        xprev = jnp.where(it == 0, 0.0, xp_ref[...])
        w = w_ref[...]

        def group(g, accs):
            off = pl.multiple_of(g * SUBLANES, SUBLANES)
            dcur = d_ref[pl.ds(off, SUBLANES), :]
            dnx = d_ref[pl.ds(pl.multiple_of(jnp.minimum(off + SUBLANES, tm - SUBLANES), SUBLANES), SUBLANES), :]
            dnx = jnp.where(g == ng - 1, dnext, dnx)
            xcur = x_ref[pl.ds(off, SUBLANES), :]
            xpv = x_ref[pl.ds(pl.multiple_of(jnp.maximum(off - SUBLANES, 0), SUBLANES), SUBLANES), :]
            xpv = jnp.where(g == 0, xprev, xpv)
            acc = dcur * w[3:4]
            for s in (1, 2, 3):
                acc = acc + _shift_up(dcur, dnx, s, rows) * w[3 - s:4 - s]
            o_ref[pl.ds(off, SUBLANES), :] = acc
            a0, a1, a2, a3, ab = accs
            a0 = a0 + dcur * _shift_down(xcur, xpv, 3, rows)
            a1 = a1 + dcur * _shift_down(xcur, xpv, 2, rows)
            a2 = a2 + dcur * _shift_down(xcur, xpv, 1, rows)
            a3 = a3 + dcur * xcur
            return a0, a1, a2, a3, ab + dcur

        zero = jnp.zeros((SUBLANES, cw), F32)
        accs = lax.fori_loop(0, ng, group, (zero,) * 5)
        sums = jnp.zeros((SUBLANES, cw), F32)
        for k, a in enumerate(accs):
            sums = jnp.where(rows == k, jnp.sum(a, axis=0, keepdims=True), sums)
        _accumulate(sums_ref, sums, it == 0)

    tile = pl.BlockSpec((tm, cw), lambda j, i: (i, j))
    return pl.pallas_call(
        body, name=name, grid=(C // cw, nt),
        in_specs=[tile, pl.BlockSpec((SUBLANES, cw), lambda j, i: (jnp.minimum((i + 1) * hb, last), j)),
                  tile, pl.BlockSpec((SUBLANES, cw), lambda j, i: (jnp.maximum(i * hb - 1, 0), j)),
                  pl.BlockSpec((4, cw), lambda j, i: (0, j))],
        out_specs=[tile, pl.BlockSpec((SUBLANES, cw), lambda j, i: (0, j))],
        out_shape=[jax.ShapeDtypeStruct((T, C), F32), jax.ShapeDtypeStruct((SUBLANES, C), F32)],
        compiler_params=_params(("parallel", "arbitrary"), 7 * tm * cw * 4),
    )(dxc, dxc, z, z, conv_w)


def _rg_coeffs(r, ig, xc, sp):
    la = (-RG_C) * r * sp
    a = jnp.exp(la)
    m = jnp.sqrt(-_expm1(2.0 * la))
    return a, m, m * (ig * xc)


def _rg_scan_fwd(z, r, ig, xc, sp, *, T, C, gate_off, tm=512, cw=256, name):
    rows16 = 2 * SUBLANES
    nq = tm // rows16

    def body(gate_ref, r_ref, i_ref, xc_ref, sp_ref, h_ref, p_ref, carry_ref):
        it = pl.program_id(1)

        @pl.when(it == 0)
        def _():
            carry_ref[...] = jnp.zeros_like(carry_ref)

        rows = _rows8(cw)
        sp_row = sp_ref[...]

        def pair(q, carry):
            base = pl.multiple_of(q * rows16, rows16)
            halves = []
            for half in range(2):
                sl = pl.ds(pl.multiple_of(base + half * SUBLANES, SUBLANES), SUBLANES)
                a, _, b = _rg_coeffs(r_ref[sl, :], i_ref[sl, :], xc_ref[sl, :], sp_row)
                for s in (1, 2, 4):
                    keep = rows >= s
                    sa = jnp.where(keep, pltpu.roll(a, s, 0), 1.0)
                    sb = jnp.where(keep, pltpu.roll(b, s, 0), 0.0)
                    b = b + a * sb
                    a = a * sa
                h = b + a * carry
                h_ref[sl, :] = h
                halves.append(h * _gelu(gate_ref[sl, :]))
                carry = h[SUBLANES - 1:SUBLANES, :]
            p_ref[pl.ds(base, rows16), :] = jnp.concatenate(halves, axis=0).astype(p_ref.dtype)
            return carry

        last = lax.fori_loop(0, nq, pair, carry_ref[0:1, :], unroll=2)
        carry_ref[...] = jnp.broadcast_to(last, carry_ref.shape)

    tile = pl.BlockSpec((tm, cw), lambda j, i: (i, j))
    gate_blk = gate_off // cw
    return pl.pallas_call(
        body, name=name, grid=(C // cw, T // tm),
        in_specs=[pl.BlockSpec((tm, cw), lambda j, i: (i, gate_blk + j)), tile, tile, tile,
                  pl.BlockSpec((1, cw), lambda j, i: (0, j))],
        out_specs=[tile, tile],
        out_shape=[jax.ShapeDtypeStruct((T, C), F32), jax.ShapeDtypeStruct((T, C), BF16)],
        scratch_shapes=[pltpu.VMEM((SUBLANES, cw), F32)],
        compiler_params=_params(("parallel", "arbitrary"), 12 * tm * cw * 4),
    )(z, r, ig, xc, sp)


def _rg_scan_bwd(dh, h, r, ig, xc, sp, *, T, C, tm=512, cw=256, name):
    ng, hb, nt = tm // SUBLANES, tm // SUBLANES, T // tm

    def body(dh_ref, h_ref, hp_ref, r_ref, i_ref, xc_ref, sp_ref,
             dra_ref, dia_ref, dxc_ref, cra_ref, cia_ref, csp_ref, cg_ref, ca_ref):
        step = pl.program_id(1)

        @pl.when(step == 0)
        def _():
            cg_ref[...] = jnp.zeros_like(cg_ref)
            ca_ref[...] = jnp.zeros_like(ca_ref)

        rows = _rows8(cw)
        sp_row = sp_ref[...]
        hhalo = jnp.where(step == nt - 1, 0.0, hp_ref[...])

        def group(gi, carry):
            g_next, a_next, s_ra, s_ia, s_sp = carry
            g = ng - 1 - gi
            off = pl.multiple_of(g * SUBLANES, SUBLANES)
            sl = pl.ds(off, SUBLANES)
            rr, ii, xx = r_ref[sl, :], i_ref[sl, :], xc_ref[sl, :]
            a, m, _ = _rg_coeffs(rr, ii, xx, sp_row)
            hh = h_ref[sl, :]
            hpv = h_ref[pl.ds(pl.multiple_of(jnp.maximum(off - SUBLANES, 0), SUBLANES), SUBLANES), :]
            hpv = jnp.where(g == 0, hhalo, hpv)
            hprev = _shift_down(hh, hpv, 1, rows)
            d = dh_ref[sl, :]
            c = jnp.where(rows < SUBLANES - 1, pltpu.roll(a, SUBLANES - 1, 0), a_next)
            for s in (1, 2, 4):
                keep = rows < SUBLANES - s
                sc = jnp.where(keep, pltpu.roll(c, SUBLANES - s, 0), 1.0)
                sd = jnp.where(keep, pltpu.roll(d, SUBLANES - s, 0), 0.0)
                d = d + c * sd
                c = c * sc
            gg = d + c * g_next
            da = gg * hprev
            dm = gg * (ii * xx)
            di = gg * (m * xx)
            dxc_ref[sl, :] = gg * (m * ii)
            dla = da * a - dm * (a * a / m)
            dra = dla * ((-RG_C) * sp_row) * (rr * (1.0 - rr))
            dia = di * (ii * (1.0 - ii))
            dra_ref[sl, :] = dra
            dia_ref[sl, :] = dia
            return (gg[0:1, :], a[0:1, :], s_ra + dra, s_ia + dia, s_sp + dla * ((-RG_C) * rr))

        zero = jnp.zeros((SUBLANES, cw), F32)
        g_first, a_first, s_ra, s_ia, s_sp = lax.fori_loop(
            0, ng, group, (cg_ref[0:1, :], ca_ref[0:1, :], zero, zero, zero), unroll=2)
        cg_ref[...] = jnp.broadcast_to(g_first, cg_ref.shape)
        ca_ref[...] = jnp.broadcast_to(a_first, ca_ref.shape)
        for ref, acc in ((cra_ref, s_ra), (cia_ref, s_ia), (csp_ref, s_sp)):
            _accumulate(ref, jnp.sum(acc, axis=0, keepdims=True), step == 0)

    tile = pl.BlockSpec((tm, cw), lambda j, i: (nt - 1 - i, j))
    vec = pl.BlockSpec((1, cw), lambda j, i: (0, j))
    return pl.pallas_call(
        body, name=name, grid=(C // cw, nt),
        in_specs=[tile, tile, pl.BlockSpec((SUBLANES, cw), lambda j, i: (jnp.maximum((nt - 1 - i) * hb - 1, 0), j)),
                  tile, tile, tile, vec],
        out_specs=[tile, tile, tile, vec, vec, vec],
        out_shape=[jax.ShapeDtypeStruct((T, C), F32)] * 3 + [jax.ShapeDtypeStruct((1, C), F32)] * 3,
        scratch_shapes=[pltpu.VMEM((SUBLANES, cw), F32), pltpu.VMEM((SUBLANES, cw), F32)],
        compiler_params=_params(("parallel", "arbitrary"), 20 * tm * cw * 4),
    )(dh, h, h, r, ig, xc, sp)


def _cscan_tables(lr, li, reverse):
    lam = (lr.reshape(-1), -li.reshape(-1) if reverse else li.reshape(-1))

    def mul(p, q):
        return p[0] * q[0] - p[1] * q[1], p[0] * q[1] + p[1] * q[0]

    pows = [lam]
    for _ in range(SUBLANES - 1):
        pows.append(mul(pows[-1], lam))
    zero = jnp.zeros_like(lam[0])
    tab = jnp.stack([pows[0][0], pows[0][1], pows[1][0], pows[1][1], pows[3][0], pows[3][1], zero, zero])
    if reverse:
        pows = pows[::-1]
    return tab, jnp.stack([p[0] for p in pows]), jnp.stack([p[1] for p in pows])


def _power_slabs(lr, li, n, reverse):
    pr, pi = lr.reshape(1, -1), (-li if reverse else li).reshape(1, -1)
    while pr.shape[0] < n:
        tr, ti = pr[-1:], pi[-1:]
        pr, pi = (jnp.concatenate([pr, pr * tr - pi * ti], axis=0), jnp.concatenate([pi, pr * ti + pi * tr], axis=0))
    top_re, top_im = pr[-1], -pi[-1] if reverse else pi[-1]
    if reverse:
        pr, pi = pr[::-1], pi[::-1]
    return jnp.repeat(pr, SUBLANES, axis=0), jnp.repeat(pi, SUBLANES, axis=0), top_re, top_im


def _rows_to_segments(src_ref, dst_ref):
    seg = src_ref.shape[0] // SUBLANES
    for g in range(seg):
        dst_ref[pl.ds(g * SUBLANES, SUBLANES), :] = src_ref[pl.ds(g, SUBLANES, stride=seg), :].astype(dst_ref.dtype)


def _segments_to_rows(src_ref, dst_ref):
    seg = src_ref.shape[0] // SUBLANES
    for r in range(SUBLANES):
        dst_ref[pl.ds(r * seg, seg), :] = src_ref[pl.ds(r, seg, stride=SUBLANES), :].astype(dst_ref.dtype)


def _seg_scan_tile(xr_ref, xi_ref, pbr_ref, pbi_ref, tab_ref, pwr_ref, pwi_ref, cr_ref, ci_ref, *, reverse, h=None):
    tm, cw = xr_ref.shape
    seg = tm // SUBLANES
    rows = _rows8(cw)
    first = tm - SUBLANES if reverse else 0
    l_re, l_im = pbr_ref[first:first + 1, :], pbi_ref[first:first + 1, :]

    def slab(g):
        return pl.ds(pl.multiple_of(g * SUBLANES, SUBLANES), SUBLANES)

    def local(k, state):
        sl = slab(seg - 1 - k if reverse else k)
        sr, si = state
        nr = xr_ref[sl, :] + (l_re * sr - l_im * si)
        ni = xi_ref[sl, :] + (l_re * si + l_im * sr)
        xr_ref[sl, :] = nr
        xi_ref[sl, :] = ni
        return nr, ni

    zero = jnp.zeros((SUBLANES, cw), F32)
    er, ei = lax.fori_loop(0, seg, local, (zero, zero), unroll=2)

    for k, s in enumerate((1, 2, 4)):
        shift = SUBLANES - s if reverse else s
        keep = rows < SUBLANES - s if reverse else rows >= s
        sr = jnp.where(keep, pltpu.roll(er, shift, 0), 0.0)
        si = jnp.where(keep, pltpu.roll(ei, shift, 0), 0.0)
        m_re, m_im = tab_ref[2 * k:2 * k + 1, :], tab_ref[2 * k + 1:2 * k + 2, :]
        er, ei = er + (m_re * sr - m_im * si), ei + (m_re * si + m_im * sr)
    cin_r, cin_i = cr_ref[0:1, :], ci_ref[0:1, :]
    pwr, pwi = pwr_ref[...], pwi_ref[...]
    er, ei = er + (pwr * cin_r - pwi * cin_i), ei + (pwr * cin_i + pwi * cin_r)
    if reverse:
        ent_r = jnp.where(rows == SUBLANES - 1, cin_r, pltpu.roll(er, SUBLANES - 1, 0))
        ent_i = jnp.where(rows == SUBLANES - 1, cin_i, pltpu.roll(ei, SUBLANES - 1, 0))
        out_r, out_i = er[0:1, :], ei[0:1, :]
    else:
        ent_r = jnp.where(rows == 0, cin_r, pltpu.roll(er, 1, 0))
        ent_i = jnp.where(rows == 0, cin_i, pltpu.roll(ei, 1, 0))
        out_r, out_i = er[SUBLANES - 1:SUBLANES, :], ei[SUBLANES - 1:SUBLANES, :]
    cr_ref[...] = jnp.broadcast_to(out_r, cr_ref.shape)
    ci_ref[...] = jnp.broadcast_to(out_i, ci_ref.shape)

    if h is not None:
        hr_ref, hi_ref, hr_last, hi_last = h
        hr_wrap = _shift_down(hr_ref[pl.ds(tm - SUBLANES, SUBLANES), :], hr_last, 1, rows)
        hi_wrap = _shift_down(hi_ref[pl.ds(tm - SUBLANES, SUBLANES), :], hi_last, 1, rows)

    def fix(g, sums):
        sl = slab(g)
        pr, pi = pbr_ref[sl, :], pbi_ref[sl, :]
        nr = xr_ref[sl, :] + (pr * ent_r - pi * ent_i)
        ni = xi_ref[sl, :] + (pr * ent_i + pi * ent_r)
        xr_ref[sl, :] = nr
        xi_ref[sl, :] = ni
        if h is None:
            return sums
        before = slab(jnp.maximum(g - 1, 0))
        hr1 = jnp.where(g == 0, hr_wrap, hr_ref[before, :])
        hi1 = jnp.where(g == 0, hi_wrap, hi_ref[before, :])
        return sums[0] + (nr * hr1 + ni * hi1), sums[1] + (ni * hr1 - nr * hi1)

    return lax.fori_loop(0, seg, fix, (zero, zero) if h is not None else (), unroll=2)


def _s5_fwd(z, u_off, wb_re, wb_im, wc_re, wc_im_neg, d_row, lr, li, *, T, tm=512, name):
    J, ku, kp = wb_re.shape
    nt = T // tm
    pb_re, pb_im, top_re, top_im = _power_slabs(lr, li, tm // SUBLANES, False)
    tab, pw_re, pw_im = _cscan_tables(top_re, top_im, False)
    u_blk = u_off // ku

    def body(u_ref, wbr_ref, wbi_ref, wcr_ref, wci_ref, d_ref, pbr_ref, pbi_ref, tab_ref, pwr_ref, pwi_ref,
             hr_ref, hi_ref, y_ref, yg_ref, cr_ref, ci_ref, us_ref, ys_ref):
        @pl.when(pl.program_id(1) == 0)
        def _():
            cr_ref[...] = jnp.zeros_like(cr_ref)
            ci_ref[...] = jnp.zeros_like(ci_ref)

        _rows_to_segments(u_ref, us_ref)
        u = us_ref[...]
        ub = u.astype(BF16)
        hr_ref[...] = jnp.dot(ub, wbr_ref[...], preferred_element_type=F32)
        hi_ref[...] = jnp.dot(ub, wbi_ref[...], preferred_element_type=F32)
        _seg_scan_tile(hr_ref, hi_ref, pbr_ref, pbi_ref, tab_ref, pwr_ref, pwi_ref, cr_ref, ci_ref, reverse=False)
        y = (jnp.dot(hr_ref[...].astype(BF16), wcr_ref[...], preferred_element_type=F32)
             + jnp.dot(hi_ref[...].astype(BF16), wci_ref[...], preferred_element_type=F32) + d_ref[...] * u)
        ys_ref[...] = y
        _segments_to_rows(ys_ref, y_ref)
        ys_ref[...] = _gelu(y)
        _segments_to_rows(ys_ref, yg_ref)

    wb_spec = pl.BlockSpec((None, ku, kp), lambda j, i: (j, 0, 0))
    wc_spec = pl.BlockSpec((None, kp, ku), lambda j, i: (j, 0, 0))
    small = pl.BlockSpec((SUBLANES, kp), lambda j, i: (0, j))
    slabs = pl.BlockSpec((tm, kp), lambda j, i: (0, j))
    state = pl.BlockSpec((tm, kp), lambda j, i: (i, j))
    chan = pl.BlockSpec((tm, ku), lambda j, i: (i, j))
    return pl.pallas_call(
        body, name=name, grid=(J, nt),
        in_specs=[pl.BlockSpec((tm, ku), lambda j, i: (i, u_blk + j)), wb_spec, wb_spec, wc_spec, wc_spec,
                  pl.BlockSpec((1, ku), lambda j, i: (0, j)), slabs, slabs, small, small, small],
        out_specs=[state, state, chan, chan],
        out_shape=[jax.ShapeDtypeStruct((T, J * kp), F32)] * 2
        + [jax.ShapeDtypeStruct((T, J * ku), F32), jax.ShapeDtypeStruct((T, J * ku), BF16)],
        scratch_shapes=[pltpu.VMEM((SUBLANES, kp), F32), pltpu.VMEM((SUBLANES, kp), F32),
                        pltpu.VMEM((tm, ku), F32), pltpu.VMEM((tm, ku), F32)],
        compiler_params=_params(("parallel", "arbitrary"), 14 * tm * kp * 4),
    )(z, wb_re, wb_im, wc_re, wc_im_neg, d_row, pb_re, pb_im, tab, pw_re, pw_im)


def _s5_bwd(dy, z, u_off, h_re, h_im, wb_re, wb_im, wc_re, wc_im_neg, d_row, lr, li, *, T, tm=512, name):
    J, ku, kp = wb_re.shape
    nt, hb = T // tm, tm // SUBLANES
    pb_re, pb_im, top_re, top_im = _power_slabs(lr, li, tm // SUBLANES, True)
    tab, pw_re, pw_im = _cscan_tables(top_re, top_im, True)
    u_blk = u_off // ku
    contract_rows = (((0,), (0,)), ((), ()))
    contract_cols = (((1,), (1,)), ((), ()))

    def body(dy_ref, u_ref, hr_ref, hrp_ref, hi_ref, hip_ref, wbr_ref, wbi_ref, wcr_ref, wci_ref, d_ref,
             pbr_ref, pbi_ref, tab_ref, pwr_ref, pwi_ref,
             du_ref, dlr_ref, dli_ref, dd_ref, dwbr_ref, dwbi_ref, dwcr_ref, dwci_ref,
             gr_ref, gi_ref, cr_ref, ci_ref, dys_ref, us_ref):
        step = pl.program_id(1)
        first = step == 0

        @pl.when(first)
        def _():
            cr_ref[...] = jnp.zeros_like(cr_ref)
            ci_ref[...] = jnp.zeros_like(ci_ref)

        _rows_to_segments(dy_ref, dys_ref)
        _rows_to_segments(u_ref, us_ref)
        dy_t, u = dys_ref[...], us_ref[...]
        dyb, ub = dy_t.astype(BF16), u.astype(BF16)
        gr_ref[...] = lax.dot_general(dyb, wcr_ref[...], contract_cols, preferred_element_type=F32)
        gi_ref[...] = lax.dot_general(dyb, wci_ref[...], contract_cols, preferred_element_type=F32)
        hr_last = jnp.where(step == nt - 1, 0.0, hrp_ref[...])
        hi_last = jnp.where(step == nt - 1, 0.0, hip_ref[...])
        s_re, s_im = _seg_scan_tile(gr_ref, gi_ref, pbr_ref, pbi_ref, tab_ref, pwr_ref, pwi_ref, cr_ref, ci_ref,
                                    reverse=True, h=(hr_ref, hi_ref, hr_last, hi_last))
        _accumulate(dlr_ref, jnp.sum(s_re, axis=0, keepdims=True), first)
        _accumulate(dli_ref, jnp.sum(s_im, axis=0, keepdims=True), first)
        grb, gib = gr_ref[...].astype(BF16), gi_ref[...].astype(BF16)
        du = (lax.dot_general(grb, wbr_ref[...], contract_cols, preferred_element_type=F32)
              + lax.dot_general(gib, wbi_ref[...], contract_cols, preferred_element_type=F32) + dy_t * d_ref[...])
        dys_ref[...] = du
        _segments_to_rows(dys_ref, du_ref)
        _accumulate(dd_ref, jnp.sum(dy_t * u, axis=0, keepdims=True), first)
        _accumulate(dwbr_ref, lax.dot_general(ub, grb, contract_rows, preferred_element_type=F32), first)
        _accumulate(dwbi_ref, lax.dot_general(ub, gib, contract_rows, preferred_element_type=F32), first)
        _accumulate(dwcr_ref, lax.dot_general(hr_ref[...].astype(BF16), dyb, contract_rows,
                                              preferred_element_type=F32), first)
        _accumulate(dwci_ref, lax.dot_general(hi_ref[...].astype(BF16), dyb, contract_rows,
                                              preferred_element_type=F32), first)

    def tix(i):
        return nt - 1 - i

    wb_spec = pl.BlockSpec((None, ku, kp), lambda j, i: (j, 0, 0))
    wc_spec = pl.BlockSpec((None, kp, ku), lambda j, i: (j, 0, 0))
    small = pl.BlockSpec((SUBLANES, kp), lambda j, i: (0, j))
    state = pl.BlockSpec((tm, kp), lambda j, i: (tix(i), j))
    halo = pl.BlockSpec((SUBLANES, kp), lambda j, i: (jnp.maximum(tix(i) * hb - 1, 0), j))
    chan = pl.BlockSpec((tm, ku), lambda j, i: (tix(i), j))
    svec = pl.BlockSpec((1, kp), lambda j, i: (0, j))
    cvec = pl.BlockSpec((1, ku), lambda j, i: (0, j))
    slabs = pl.BlockSpec((tm, kp), lambda j, i: (0, j))
    return pl.pallas_call(
        body, name=name, grid=(J, nt),
        in_specs=[chan, pl.BlockSpec((tm, ku), lambda j, i: (tix(i), u_blk + j)), state, halo, state, halo,
                  wb_spec, wb_spec, wc_spec, wc_spec, cvec, slabs, slabs, small, small, small],
        out_specs=[chan, svec, svec, cvec, wb_spec, wb_spec, wc_spec, wc_spec],
        out_shape=[jax.ShapeDtypeStruct((T, J * ku), BF16), jax.ShapeDtypeStruct((1, J * kp), F32),
                   jax.ShapeDtypeStruct((1, J * kp), F32), jax.ShapeDtypeStruct((1, J * ku), F32),
                   jax.ShapeDtypeStruct((J, ku, kp), F32), jax.ShapeDtypeStruct((J, ku, kp), F32),
                   jax.ShapeDtypeStruct((J, kp, ku), F32), jax.ShapeDtypeStruct((J, kp, ku), F32)],
        scratch_shapes=[pltpu.VMEM((tm, kp), F32), pltpu.VMEM((tm, kp), F32),
                        pltpu.VMEM((SUBLANES, kp), F32), pltpu.VMEM((SUBLANES, kp), F32),
                        pltpu.VMEM((tm, ku), F32), pltpu.VMEM((tm, ku), F32)],
        compiler_params=_params(("parallel", "arbitrary"), 16 * tm * kp * 4),
    )(dy, z, h_re, h_re, h_im, h_im, wb_re, wb_im, wc_re, wc_im_neg, d_row, pb_re, pb_im, tab, pw_re, pw_im)


def _mesh_pos():
    return lax.axis_index("x"), lax.axis_index("y"), lax.axis_index("c")


def _dev_index(px, py, pc):
    return 4 * px + 2 * py + pc


def _all_gather(shards, name):
    n = len(shards)

    def body(*refs):
        ins, outs = refs[:n], refs[n:2 * n]
        send_sems, recv_sems, local_sems = refs[2 * n:]
        x, y, c = _mesh_pos()
        me, sibling = (x, y, c), (x, y, 1 - c)
        chips = [(1 - x, y), (x, 1 - y), (1 - x, 1 - y)]

        def copy(a, k, block, to, src=None):
            dst = outs[a].at[_dev_index(*block)]
            return pltpu.make_async_remote_copy(
                src_ref=dst if src is None else src, dst_ref=dst, send_sem=send_sems.at[a * 7 + k],
                recv_sem=recv_sems.at[a * 7 + k], device_id=to, device_id_type=MESH)

        mine = [pltpu.make_async_copy(ins[a], outs[a].at[_dev_index(*me)], local_sems.at[a]) for a in range(n)]
        for cp in mine:
            cp.start()
        first = []
        for a in range(n):
            first.append(copy(a, 0, me, sibling, src=ins[a]))
            first += [copy(a, 1 + j, me, (*chip, c), src=ins[a]) for j, chip in enumerate(chips)]
        for cp in first:
            cp.start()
        passed = []
        for j, chip in enumerate(chips):
            for a in range(n):
                copy(a, 1 + j, (*chip, c), me).wait_recv()
                fwd = copy(a, 4 + j, (*chip, c), sibling)
                fwd.start()
                passed.append(fwd)
        for a in range(n):
            copy(a, 0, sibling, me).wait_recv()
            for j, chip in enumerate(chips):
                copy(a, 4 + j, (*chip, 1 - c), me).wait_recv()
        for cp in first + passed:
            cp.wait_send()
        for cp in mine:
            cp.wait()

    return pl.pallas_call(
        body, name=name, in_specs=[ANY] * n, out_specs=[ANY] * n,
        out_shape=[jax.ShapeDtypeStruct((N_DEV,) + s.shape, s.dtype) for s in shards],
        scratch_shapes=[pltpu.SemaphoreType.DMA((7 * n,)), pltpu.SemaphoreType.DMA((7 * n,)),
                        pltpu.SemaphoreType.DMA((n,))],
    )(*shards)


def _exchange_blocks(parts, name):
    n = len(parts)
    relations = [(dx, dy, dc) for dx in (0, 1) for dy in (0, 1) for dc in (0, 1) if (dx, dy, dc) != (0, 0, 0)]

    def body(*refs):
        ins, outs = refs[:n], refs[n:2 * n]
        send_sems, recv_sems, local_sems = refs[2 * n:]
        x, y, c = _mesh_pos()
        me = _dev_index(x, y, c)
        mine = [pltpu.make_async_copy(ins[a].at[me], outs[a].at[me], local_sems.at[a]) for a in range(n)]
        for cp in mine:
            cp.start()
        copies = []
        for k, (dx, dy, dc) in enumerate(relations):
            peer = (x + dx - 2 * x * dx, y + dy - 2 * y * dy, c + dc - 2 * c * dc)
            for a in range(n):
                copies.append((pltpu.make_async_remote_copy(
                    src_ref=ins[a].at[_dev_index(*peer)], dst_ref=outs[a].at[me], send_sem=send_sems.at[a * 7 + k],
                    recv_sem=recv_sems.at[a * 7 + k], device_id=peer, device_id_type=MESH),
                    pltpu.make_async_remote_copy(
                    src_ref=ins[a].at[_dev_index(*peer)], dst_ref=outs[a].at[_dev_index(*peer)],
                    send_sem=send_sems.at[a * 7 + k], recv_sem=recv_sems.at[a * 7 + k], device_id=peer,
                    device_id_type=MESH)))
        for send, _ in copies:
            send.start()
        for _, recv in copies:
            recv.wait_recv()
        for send, _ in copies:
            send.wait_send()
        for cp in mine:
            cp.wait()

    return pl.pallas_call(
        body, name=name, in_specs=[ANY] * n, out_specs=[ANY] * n,
        out_shape=[jax.ShapeDtypeStruct(p.shape, p.dtype) for p in parts],
        scratch_shapes=[pltpu.SemaphoreType.DMA((7 * n,)), pltpu.SemaphoreType.DMA((7 * n,)),
                        pltpu.SemaphoreType.DMA((n,))],
    )(*parts)


HBM = pl.BlockSpec(memory_space=pltpu.HBM)
SEM = pl.BlockSpec(memory_space=pltpu.SEMAPHORE)
EFFECT = pltpu.SideEffectType.DATAFLOW_SIDE_EFFECTING
RELATIONS = [(dx, dy, dc) for dx in (0, 1) for dy in (0, 1) for dc in (0, 1) if (dx, dy, dc) != (0, 0, 0)]


def _peer(rel):
    x, y, c = _mesh_pos()
    dx, dy, dc = rel
    return (x + dx - 2 * x * dx, y + dy - 2 * y * dy, c + dc - 2 * c * dc)


def _split_copy(src_ref, land_ref, send_sems, recv_sems, k, scatter, incoming):
    peer = _peer(RELATIONS[k])
    me = _dev_index(*_mesh_pos())
    src = src_ref.at[_dev_index(*peer)] if scatter else src_ref
    dst = land_ref.at[_dev_index(*peer) if incoming else me]
    return pltpu.make_async_remote_copy(src_ref=src, dst_ref=dst, send_sem=send_sems.at[k], recv_sem=recv_sems.at[k],
                                        device_id=peer, device_id_type=MESH)


def _exchange_start(srcs, lands, *, scatter, name):
    n = len(srcs)

    def body(*refs):
        src_refs, land_refs = refs[:n], refs[n:2 * n]
        send, recv = refs[2 * n:3 * n], refs[3 * n:4 * n]
        token = refs[-1]
        for k in range(len(RELATIONS)):
            for a in range(n):
                _split_copy(src_refs[a], land_refs[a], send[a], recv[a], k, scatter, incoming=False).start()
        token[...] = jnp.zeros_like(token)

    n_rel = len(RELATIONS)
    outs = pl.pallas_call(
        body, name=name, in_specs=[HBM] * (2 * n),
        out_shape=[pltpu.SemaphoreType.DMA((n_rel,))] * (2 * n)
        + [pltpu.HBM(s.shape, s.dtype) for s in srcs] + [pltpu.HBM(s.shape, s.dtype) for s in lands]
        + [jax.ShapeDtypeStruct((SUBLANES, LANES), F32)],
        out_specs=[SEM] * (2 * n) + [HBM] * (2 * n) + [pl.BlockSpec(memory_space=pltpu.VMEM)],
        input_output_aliases={**{a: 2 * n + a for a in range(n)}, **{n + a: 3 * n + a for a in range(n)}},
        compiler_params=pltpu.CompilerParams(has_side_effects=EFFECT),
    )(*[pltpu.with_memory_space_constraint(s, pltpu.HBM) for s in srcs],
      *[pltpu.with_memory_space_constraint(s, pltpu.HBM) for s in lands])
    per_array = [(outs[a], outs[n + a], outs[2 * n + a], outs[3 * n + a]) for a in range(n)]
    return per_array, outs[-1]


def _exchange_wait(handle, after, *, scatter, name):
    send_sems, recv_sems, src_thru, land_thru = handle

    def body(src_ref, land_ref, send, recv, after_ref, src_dead, got_ref):
        for k in range(len(RELATIONS)):
            cp = _split_copy(src_ref, land_ref, send, recv, k, scatter, incoming=True)
            cp.wait_send()
            cp.wait_recv()

    return pl.pallas_call(
        body, name=name, in_specs=[HBM, HBM, SEM, SEM, ANY],
        out_shape=[pltpu.HBM(src_thru.shape, src_thru.dtype), pltpu.HBM(land_thru.shape, land_thru.dtype)],
        out_specs=[HBM, HBM], input_output_aliases={0: 0, 1: 1},
        compiler_params=pltpu.CompilerParams(has_side_effects=EFFECT),
    )(src_thru, land_thru, send_sems, recv_sems, after)[1]


def _landing_zone(own_block):
    me = _dev_index(*_mesh_pos())
    zone = lax.empty((N_DEV,) + own_block.shape, own_block.dtype)
    return lax.dynamic_update_index_in_dim(zone, own_block, me, 0)


def _row_tile(rows, want):
    t = min(want, rows) // SUBLANES * SUBLANES
    while rows % t:
        t -= SUBLANES
    return t


def _sum_slots(recv, *, tr, name):
    s_, r_, c_ = recv.shape
    tr = _row_tile(r_, tr)

    def body(g_ref, o_ref):
        acc = g_ref[0]
        for s in range(1, s_):
            acc = acc + g_ref[s]
        o_ref[...] = acc

    return pl.pallas_call(
        body, name=name, grid=(r_ // tr,),
        in_specs=[pl.BlockSpec((s_, tr, c_), lambda i: (0, i, 0))],
        out_specs=pl.BlockSpec((tr, c_), lambda i: (i, 0)),
        out_shape=jax.ShapeDtypeStruct((r_, c_), F32),
        compiler_params=_params(("parallel",), (2 * s_ + 3) * tr * c_ * 4),
    )(recv)


def _adamw(recv, w, m, v, *, tr, name):
    s_, r_, c_ = recv.shape
    tr = _row_tile(r_, tr)
    assert w.shape == (r_, c_), (name, w.shape, recv.shape)
    c1 = 1.0 - ADAM_B1 ** ADAM_STEP
    c2 = 1.0 - ADAM_B2 ** ADAM_STEP

    def body(g_ref, w_ref, m_ref, v_ref, go_ref, d_ref, mo_ref, vo_ref):
        g = g_ref[0].astype(F32)
        for s in range(1, s_):
            g = g + g_ref[s].astype(F32)
        mn = ADAM_B1 * m_ref[...] + (1.0 - ADAM_B1) * g
        vn = ADAM_B2 * v_ref[...] + (1.0 - ADAM_B2) * (g * g)
        go_ref[...] = g
        mo_ref[...] = mn
        vo_ref[...] = vn
        d_ref[...] = -ADAM_LR * ((mn / c1) / (jnp.sqrt(vn / c2) + ADAM_EPS) + ADAM_WD * w_ref[...])

    tile = pl.BlockSpec((tr, c_), lambda i: (i, 0))
    return pl.pallas_call(
        body, name=name, grid=(r_ // tr,),
        in_specs=[pl.BlockSpec((s_, tr, c_), lambda i: (0, i, 0)), tile, tile, tile],
        out_specs=[tile] * 4, out_shape=[jax.ShapeDtypeStruct((r_, c_), F32)] * 4,
        compiler_params=_params(("parallel",), (2 * s_ + 16) * tr * c_ * 4),
    )(recv, w, m, v)


def _s5_discretise(a_re, a_im, log_dt, b_re, b_im):
    dt = jnp.exp(log_dt)[:, None]
    lr = jnp.minimum(a_re, -1e-4)
    li = a_im
    mag = jnp.exp(lr * dt)
    lbr = mag * jnp.cos(li * dt)
    lbi = mag * jnp.sin(li * dt)
    zr, zi = lbr - 1.0, lbi
    den = lr * lr + li * li
    fr = (zr * lr + zi * li) / den
    fi = (zi * lr - zr * li) / den
    bbr = fr[..., None] * b_re - fi[..., None] * b_im
    bbi = fr[..., None] * b_im + fi[..., None] * b_re
    return lbr, lbi, bbr, bbi


def _softplus_neg(lam):
    return jnp.maximum(-lam, 0.0) + jnp.log(1.0 + jnp.exp(-jnp.abs(lam)))


S5_Q = 8
RG_Q = 2


def _local_step(x, tgt, W, comm):
    T, D = x.shape
    C = D
    G, P, H = W["ssm_b_re"].shape
    S = G * H
    F = W["mlp_b_up"].shape[1]
    n_in = 2 * C + S + 2 * D
    heads, hd = W["rg_wa"].shape[0], W["rg_wa"].shape[1]
    u_off, ga_off, gb_off = 2 * C, 2 * C + S, 2 * C + S + D

    sp, sp_vjp = jax.vjp(_softplus_neg, W["rg_lambda"])
    (lbr, lbi, bbr, bbi), s5_vjp = jax.vjp(_s5_discretise, W["ssm_a_re"], W["ssm_a_im"], W["ssm_log_dt"],
                                           W["ssm_b_re"], W["ssm_b_im"])
    lam_re, lam_im = lbr.reshape(-1), lbi.reshape(-1)
    wa_bd = _bd_pack(W["rg_wa"], RG_Q).astype(BF16)
    wx_bd = _bd_pack(W["rg_wx"], RG_Q).astype(BF16)
    wb_re = _bd_pack(jnp.swapaxes(bbr, 1, 2), S5_Q).astype(BF16)
    wb_im = _bd_pack(jnp.swapaxes(bbi, 1, 2), S5_Q).astype(BF16)
    wc_re = _bd_pack(jnp.swapaxes(W["ssm_c_re"], 1, 2), S5_Q).astype(BF16)
    wc_im_neg = _bd_pack(jnp.swapaxes(-W["ssm_c_im"], 1, 2), S5_Q).astype(BF16)
    d_row = W["ssm_d"].reshape(1, S)
    jr, kr = heads // RG_Q, RG_Q * hd
    js, ku, kp = G // S5_Q, S5_Q * H, S5_Q * P

    x_bf = x.astype(BF16)
    w_in = comm.weight("w_in", None)
    z = _mm(x_bf, w_in, M=T, N=n_in, K=D, tm=512, tn=n_in // 4, tk=D, after=comm.gather_token, name="fwd_in_proj")
    xc = _conv_fwd(z, W["conv_w"], W["conv_b"], T=T, C=C, name="fwd_conv")
    r = _bd([(xc, 0, wa_bd)], T=T, J=jr, kb=kr, nb=kr, extras=[(W["rg_ba"], "vec", 0)],
            epilogue=lambda acc, b: (_sig(acc + b),), name="fwd_rgate")
    ig = _bd([(xc, 0, wx_bd)], T=T, J=jr, kb=kr, nb=kr, extras=[(W["rg_bx"], "vec", 0)],
             epilogue=lambda acc, b: (_sig(acc + b),), name="fwd_igate")
    h, p = _rg_scan_fwd(z, r, ig, xc, sp, T=T, C=C, gate_off=C, name="fwd_rg_scan")
    w_a_out = comm.weight("w_a_out", p)
    y_a = _mm(p, w_a_out, M=T, N=D, K=C, out_dtypes=(BF16,), tm=512, tn=D, tk=C, name="fwd_rg_out")

    h_re, h_im, y_s, yg = _s5_fwd(z, u_off, wb_re, wb_im, wc_re, wc_im_neg, d_row, lam_re, lam_im, T=T, name="fwd_s5")
    w_glu_w, w_glu_v = comm.weight("glu_w", yg), comm.weight("glu_v", yg)
    glu_a = _mm(yg, w_glu_w, M=T, N=D, K=S, out_dtypes=(BF16,), tm=1024, tn=D, tk=S, name="fwd_glu_w")
    glu_b = _mm(yg, w_glu_v, M=T, N=D, K=S, out_dtypes=(BF16,), tm=1024, tn=D, tk=S, name="fwd_glu_v")

    cwm = 1024

    def mix_fn(ga, gb, ya, a, b):
        return (_sig(ga) * ya + _sig(gb) * (a * _sig(b)),)

    mix = _ew(mix_fn, [(z, "tile", ga_off // cwm), (z, "tile", gb_off // cwm), (y_a, "tile", 0), (glu_a, "tile", 0),
                       (glu_b, "tile", 0)], T=T, C=D, n_out=1, out_dtypes=(BF16,), cw=cwm, name="fwd_mix")
    w_out = comm.weight("w_out", mix)
    def out_ln1_fn(acc, xv, g, b):
        s = ALPHA * xv + acc
        xhat, _ = _ln_stats(s)
        y = xhat * g + b
        return s, y, y

    s1, x1, x1_bf = _mm(mix, w_out, M=T, N=D, K=D, tm=256, tn=D, tk=D,
                        extras=[(x, "mn"), (W["ln1_g"], "n"), (W["ln1_b"], "n")], epilogue=out_ln1_fn, n_out=3,
                        out_dtypes=(F32, F32, BF16), name="fwd_out_proj_ln1")
    w_up = comm.weight("mlp_w_up", x1_bf)

    def mlp_up_fn(acc, b):
        hp = acc + b
        rl = jnp.maximum(hp, 0.0)
        return rl * rl, hp

    hact, hpre = _mm(x1_bf, w_up, M=T, N=F, K=D, tm=1024, tn=1024, tk=D, extras=[(W["mlp_b_up"], "n")],
                     epilogue=mlp_up_fn, n_out=2, out_dtypes=(BF16, BF16), name="fwd_mlp_up")
    w_down = comm.weight("mlp_w_down", hact)
    s2 = _mm(hact, w_down, M=T, N=D, K=F, tm=512, tn=1024, tk=4096,
             extras=[(x1, "mn"), (W["mlp_b_down"], "n")], epilogue=lambda acc, xv, b: (ALPHA * xv + acc + b,),
             name="fwd_mlp_down")

    def ln2_fn(s, t, g, b):
        xhat, rstd = _ln_stats(s)
        err = xhat * g + b - t
        dy = err * (1.0 / D)
        ds = _ln_bwd(dy, g, xhat, rstd)
        return ds, ds, 0.5 * dy * err, dy * xhat, dy, ds

    ds2, ds2_bf, loss_cols, d_ln2_g, d_ln2_b, d_b_down = _ew(
        ln2_fn, [(s2, "tile", 0), (tgt, "tile", 0), (W["ln2_g"], "vec", 0), (W["ln2_b"], "vec", 0)],
        T=T, C=D, n_out=2, n_cs=4, out_dtypes=(F32, BF16), tm=128, name="bwd_loss_ln2")
    d_w_down = _mm(hact, ds2_bf, M=F, N=D, K=T, ta=True, out_dtypes=(BF16,), tm=1024, tn=1024, tk=4096, name="bwd_w_down")
    sent = comm.send_grad("mlp_w_down", d_w_down)

    def dhpre_fn(acc, hp):
        dv = acc * (2.0 * jnp.maximum(hp.astype(F32), 0.0))
        return dv, dv

    dhpre, d_b_up = _mm(ds2_bf, w_down, M=T, N=F, K=D, tb=True, tm=1024, tn=1024, tk=D, extras=[(hpre, "mn")],
                        epilogue=dhpre_fn, n_cs=1, out_dtypes=(BF16,), after=sent, name="bwd_mlp_down")
    d_w_up = _mm(x1_bf, dhpre, M=D, N=F, K=T, ta=True, out_dtypes=(BF16,), n_split=N_DEV, tm=1024, tn=F // N_DEV, tk=4096, name="bwd_w_up")
    sent = comm.send_grad("mlp_w_up", d_w_up)
    dx1 = _mm(dhpre, w_up, M=T, N=D, K=F, tb=True, tm=512, tn=1024, tk=4096,
              extras=[(ds2, "mn")], epilogue=lambda acc, dv: (ALPHA * dv + acc,), after=sent, name="bwd_mlp_up")

    def ln1_bwd_fn(s, dy, g):
        xhat, rstd = _ln_stats(s)
        ds = _ln_bwd(dy, g, xhat, rstd)
        return ds, ds, dy * xhat, dy

    ds1, ds1_bf, d_ln1_g, d_ln1_b = _ew(ln1_bwd_fn, [(s1, "tile", 0), (dx1, "tile", 0), (W["ln1_g"], "vec", 0)],
                                        T=T, C=D, n_out=2, n_cs=2, out_dtypes=(F32, BF16), tm=128, name="bwd_ln1")
    d_w_out = _mm(mix, ds1_bf, M=D, N=D, K=T, ta=True, out_dtypes=(BF16,), tm=1024, tn=1024, tk=4096, name="bwd_w_out")
    sent = comm.send_grad("w_out", d_w_out)
    def mix_bwd_fn(dm, ga, gb, ya, a, b):
        ya, a, b = ya.astype(F32), a.astype(F32), b.astype(F32)
        sa, sb, sv = _sig(ga), _sig(gb), _sig(b)
        yb = a * sv
        dyb = dm * sb
        return (dm * ya * (sa * (1.0 - sa)), dm * yb * (sb * (1.0 - sb)), dm * sa, dyb * sv,
                dyb * a * (sv * (1.0 - sv)))

    dg_a, dg_b, dy_a, dglu_a, dglu_b = _mm(
        ds1_bf, w_out, M=T, N=D, K=D, tb=True, tm=512, tn=cwm, tk=D,
        extras=[(z, "mn", ga_off // cwm), (z, "mn", gb_off // cwm), (y_a, "mn"), (glu_a, "mn"), (glu_b, "mn")],
        epilogue=mix_bwd_fn, n_out=5, out_dtypes=(BF16,) * 5, after=sent, name="bwd_out_proj_mix")

    d_w_a_out = _mm(p, dy_a, M=C, N=D, K=T, ta=True, out_dtypes=(BF16,), tm=1024, tn=1024, tk=4096, name="bwd_w_a_out")
    sent = comm.send_grad("w_a_out", d_w_a_out)
    def dp_fn(dp, hv, gate):
        th = jnp.tanh(GELU_C * (gate + GELU_K * gate * gate * gate))
        gelu = 0.5 * gate * (1.0 + th)
        dgelu = 0.5 * (1.0 + th) + 0.5 * gate * (1.0 - th * th) * (GELU_C * (1.0 + 3.0 * GELU_K * gate * gate))
        return dp * gelu, dp * hv * dgelu

    dh, dgate = _mm(dy_a, w_a_out, M=T, N=C, K=D, tb=True, tm=256, tn=C, tk=D, extras=[(h, "mn"), (z, "mn", 1)],
                    epilogue=dp_fn, n_out=2, out_dtypes=(F32, BF16), after=sent, name="bwd_rg_out")
    dra, dia, dxc0, d_ba, d_bx, d_sp = _rg_scan_bwd(dh, h, r, ig, xc, sp, T=T, C=C, name="bwd_rg_scan")
    dxc = _bd([(dra, 0, wa_bd), (dia, 0, wx_bd)], T=T, J=jr, kb=kr, nb=kr, tw=True, extras=[(dxc0, "tile", 0)],
              epilogue=lambda acc, d0: (acc + d0,), name="bwd_gates")
    d_wa = _bd_unpack(_bdw(xc, 0, dra, 0, T=T, J=jr, kb=kr, nb=kr, name="bwd_w_rgate"), RG_Q)
    d_wx = _bd_unpack(_bdw(xc, 0, dia, 0, T=T, J=jr, kb=kr, nb=kr, name="bwd_w_igate"), RG_Q)
    dxr, conv_sums = _conv_bwd(dxc, z, W["conv_w"], T=T, C=C, name="bwd_conv")
    d_conv_w, d_conv_b = conv_sums[0:4], conv_sums[4:5]
    (d_lambda,) = sp_vjp(d_sp)

    d_glu_w = _mm(yg, dglu_a, M=S, N=D, K=T, ta=True, out_dtypes=(BF16,), n_split=N_DEV, tm=1024, tn=D // N_DEV, tk=4096, name="bwd_w_glu_w")
    d_glu_v = _mm(yg, dglu_b, M=S, N=D, K=T, ta=True, out_dtypes=(BF16,), n_split=N_DEV, tm=1024, tn=D // N_DEV, tk=4096, name="bwd_w_glu_v")
    sent = comm.send_grad("glu_w", d_glu_w, "glu_v", d_glu_v)
    dyg0 = _mm(dglu_a, w_glu_w, M=T, N=S, K=D, tb=True, tm=512, tn=S, tk=D, after=sent, name="bwd_glu_w")
    dy_s = _mm(dglu_b, w_glu_v, M=T, N=S, K=D, tb=True, tm=512, tn=S, tk=D,
               extras=[(dyg0, "mn"), (y_s, "mn")], epilogue=lambda acc, d0, yv: ((acc + d0) * _dgelu(yv),),
               name="bwd_glu_v")
    du, d_lbr, d_lbi, d_ssm_d, d_wb_re, d_wb_im, d_wc_re, d_wc_im_neg = _s5_bwd(
        dy_s, z, u_off, h_re, h_im, wb_re, wb_im, wc_re, wc_im_neg, d_row, lam_re, lam_im, T=T, name="bwd_s5")
    d_bbr = jnp.swapaxes(_bd_unpack(d_wb_re, S5_Q), 1, 2)
    d_bbi = jnp.swapaxes(_bd_unpack(d_wb_im, S5_Q), 1, 2)
    d_a_re, d_a_im, d_log_dt, d_b_re, d_b_im = s5_vjp((d_lbr.reshape(G, P), d_lbi.reshape(G, P), d_bbr, d_bbi))
    d_c_re = jnp.swapaxes(_bd_unpack(d_wc_re, S5_Q), 1, 2)
    d_c_im = -jnp.swapaxes(_bd_unpack(d_wc_im_neg, S5_Q), 1, 2)

    dz = jnp.concatenate([dxr.astype(BF16), dgate.astype(BF16), du, dg_a, dg_b], axis=1)
    d_w_in = _mm(x_bf, dz, M=D, N=n_in, K=T, ta=True, out_dtypes=(BF16,), n_split=N_DEV, tm=1024, tn=n_in // N_DEV, tk=4096, name="bwd_w_in")
    sent = comm.send_grad("w_in", d_w_in)
    grad_x = _mm(dz, w_in, M=T, N=D, K=n_in, tb=True, tm=512, tn=1024, tk=n_in // 2,
                 extras=[(ds1, "mn")], epilogue=lambda acc, dv: (ALPHA * dv + acc,), after=sent, name="bwd_in_proj")

    grads = dict(
        conv_w=d_conv_w, conv_b=d_conv_b, rg_wa=d_wa, rg_ba=d_ba, rg_wx=d_wx, rg_bx=d_bx,
        rg_lambda=d_lambda, ssm_a_re=d_a_re, ssm_a_im=d_a_im, ssm_log_dt=d_log_dt,
        ssm_b_re=d_b_re, ssm_b_im=d_b_im, ssm_c_re=d_c_re, ssm_c_im=d_c_im, ssm_d=d_ssm_d.reshape(G, H),
        ln1_g=d_ln1_g, ln1_b=d_ln1_b, mlp_b_up=d_b_up, mlp_b_down=d_b_down, ln2_g=d_ln2_g, ln2_b=d_ln2_b)
    return jnp.sum(loss_cols), grad_x, grads


BIG = ("w_in", "w_a_out", "glu_w", "glu_v", "w_out", "mlp_w_up", "mlp_w_down")
COL_SHARDED = ("w_in", "glu_w", "glu_v", "mlp_w_up")
SMALL = ("conv_w", "conv_b", "rg_wa", "rg_ba", "rg_wx", "rg_bx", "rg_lambda", "ssm_a_re", "ssm_a_im", "ssm_log_dt",
         "ssm_b_re", "ssm_b_im", "ssm_c_re", "ssm_c_im", "ssm_d", "ln1_g", "ln1_b", "mlp_b_up", "mlp_b_down", "ln2_g",
         "ln2_b")
ORDER = ("w_in", "conv_w", "conv_b", "rg_wa", "rg_ba", "rg_wx", "rg_bx", "rg_lambda", "w_a_out", "ssm_a_re",
         "ssm_a_im", "ssm_log_dt", "ssm_b_re", "ssm_b_im", "ssm_c_re", "ssm_c_im", "ssm_d", "glu_w", "glu_v", "w_out",
         "ln1_g", "ln1_b", "mlp_w_up", "mlp_b_up", "mlp_w_down", "mlp_b_down", "ln2_g", "ln2_b")
TILE_ELEMS = SUBLANES * LANES


def _pack(arrs):
    pieces = []
    for a in arrs:
        flat = a.reshape(-1)
        flat = jnp.pad(flat, (0, (-flat.shape[0]) % TILE_ELEMS))
        pieces.append(flat.reshape(-1, LANES))
    rows = sum(p.shape[0] for p in pieces)
    pad_rows = (-rows) % (N_DEV * SUBLANES)
    if pad_rows:
        pieces.append(jnp.zeros((pad_rows, LANES), pieces[0].dtype))
    return jnp.concatenate(pieces, axis=0)


def _unpack(packed, shapes):
    out, row = [], 0
    for shp in shapes:
        n = math.prod(shp)
        rows = -(-n // TILE_ELEMS) * SUBLANES
        out.append(packed[row:row + rows].reshape(-1)[:n].reshape(shp))
        row += rows
    return out


class _Comm:
    def __init__(self, w):
        first = _all_gather([w["w_in"].astype(BF16), w["conv_w"]], name="gather_w_in")
        self._weights = {"w_in": first[0]}
        self.conv_w = jnp.swapaxes(first[1], 0, 1).reshape(w["conv_w"].shape[0], -1)
        later = [k for k in BIG if k != "w_in"]
        shards = [w[k].astype(BF16) for k in later]
        handles, self.gather_token = _exchange_start(shards, [_landing_zone(s) for s in shards], scatter=False,
                                                     name="gather_weights_start")
        self._gathers = dict(zip(later, handles))
        self._grads = {}

    def weight(self, k, after):
        if k not in self._weights:
            self._weights[k] = _exchange_wait(self._gathers.pop(k), after, scatter=False, name="gather_wait_" + k)
        gk = self._weights[k]
        if k in COL_SHARDED:
            return jnp.swapaxes(gk, 0, 1).reshape(gk.shape[1], -1)
        return gk.reshape(-1, gk.shape[-1])

    def send_grad(self, *names_and_parts):
        names, parts = names_and_parts[0::2], names_and_parts[1::2]
        parts = [p if k in COL_SHARDED else p.reshape(N_DEV, p.shape[0] // N_DEV, p.shape[1])
                 for k, p in zip(names, parts)]
        me = _dev_index(*_mesh_pos())
        lands = [_landing_zone(lax.dynamic_index_in_dim(p, me, 0, keepdims=False)) for p in parts]
        handles, token = _exchange_start(parts, lands, scatter=True, name="grad_start_" + names[0])
        self._grads.update(zip(names, handles))
        return token

    def received_grad(self, k, after):
        return _exchange_wait(self._grads.pop(k), after, scatter=True, name="grad_wait_" + k)


def _step(x, tgt, w, m, v):
    dev = _dev_index(*_mesh_pos())

    comm = _Comm(w)
    small = dict(w)
    small["conv_w"] = comm.conv_w
    for k in ("conv_b", "rg_ba", "rg_bx", "rg_lambda", "ln1_g", "ln1_b", "mlp_b_up", "mlp_b_down", "ln2_g", "ln2_b"):
        small[k] = w[k].reshape(1, -1)

    loss_part, grad_x, grads = _local_step(x, tgt, small, comm)

    out_g, out_d, out_m, out_v = {}, {}, {}, {}
    for k in BIG:
        rk = comm.received_grad(k, grad_x)
        out_g[k], out_d[k], out_m[k], out_v[k] = _adamw(rk, w[k], m[k], v[k], tr=128, name="adamw_" + k)

    small_shapes = [grads[k].shape for k in SMALL]
    (small_recv,) = _exchange_blocks([_pack([grads[k] for k in SMALL]).reshape(N_DEV, -1, LANES)],
                                     name="exchange_small_grads")
    small_block = _sum_slots(small_recv, tr=512, name="sum_small_grads")
    (small_all,) = _all_gather([small_block], name="gather_small_grads")
    g_small = dict(zip(SMALL, _unpack(small_all.reshape(-1, LANES), small_shapes)))
    cw_cols = w["conv_w"].shape[1]
    g_small["conv_w"] = lax.dynamic_slice_in_dim(g_small["conv_w"], dev * cw_cols, cw_cols, axis=1)
    shapes = [w[k].shape for k in SMALL]
    g_pack, w_pack, m_pack, v_pack = [_pack([src[k] for k in SMALL]) for src in (g_small, w, m, v)]
    res = _adamw(g_pack[None], w_pack, m_pack, v_pack, tr=1024, name="adamw_small")
    for dst, packed in zip((out_g, out_d, out_m, out_v), res):
        dst.update(zip(SMALL, _unpack(packed, shapes)))

    loss = lax.psum(loss_part, ("x", "y", "c"))
    return loss, grad_x, out_g, out_d, out_m, out_v


def kernel(x, w_in, conv_w, conv_b, rg_wa, rg_ba, rg_wx, rg_bx, rg_lambda, w_a_out, ssm_a_re, ssm_a_im, ssm_log_dt, ssm_b_re, ssm_b_im, ssm_c_re, ssm_c_im, ssm_d, glu_w, glu_v, w_out, ln1_g, ln1_b, mlp_w_up, mlp_b_up, mlp_w_down, mlp_b_down, ln2_g, ln2_b, loss_target, m_w_in, m_conv_w, m_conv_b, m_rg_wa, m_rg_ba, m_rg_wx, m_rg_bx, m_rg_lambda, m_w_a_out, m_ssm_a_re, m_ssm_a_im, m_ssm_log_dt, m_ssm_b_re, m_ssm_b_im, m_ssm_c_re, m_ssm_c_im, m_ssm_d, m_glu_w, m_glu_v, m_w_out, m_ln1_g, m_ln1_b, m_mlp_w_up, m_mlp_b_up, m_mlp_w_down, m_mlp_b_down, m_ln2_g, m_ln2_b, v_w_in, v_conv_w, v_conv_b, v_rg_wa, v_rg_ba, v_rg_wx, v_rg_bx, v_rg_lambda, v_w_a_out, v_ssm_a_re, v_ssm_a_im, v_ssm_log_dt, v_ssm_b_re, v_ssm_b_im, v_ssm_c_re, v_ssm_c_im, v_ssm_d, v_glu_w, v_glu_v, v_w_out, v_ln1_g, v_ln1_b, v_mlp_w_up, v_mlp_b_up, v_mlp_w_down, v_mlp_b_down, v_ln2_g, v_ln2_b):
    args = locals()
    w = {k: args[k][0] for k in ORDER}
    m = {k: args["m_" + k][0] for k in ORDER}
    v = {k: args["v_" + k][0] for k in ORDER}
    loss, grad_x, out_g, out_d, out_m, out_v = _step(x[0], loss_target[0], w, m, v)
    outs = [loss, grad_x[None]]
    for group in (out_g, out_d, out_m, out_v):
        outs += [group[k].reshape(args[k].shape) for k in ORDER]
    return tuple(outs)
```

```python
import functools
import math

import jax
import jax.numpy as jnp
from jax import lax
from jax.experimental import pallas as pl
from jax.experimental.pallas import tpu as pltpu

F32 = jnp.float32
BF16 = jnp.bfloat16
MESH = pl.DeviceIdType.MESH
N_DEV = 8
SUBLANES = 8
LANES = 128
VMEM_BYTES_V7X = 64 * 2 ** 20
VMEM_CAP = VMEM_BYTES_V7X - 8 * 2 ** 20

ALPHA = 2.0 ** 0.25
LN_EPS = 1e-5
RG_C = 8.0
ADAM_LR, ADAM_B1, ADAM_B2, ADAM_EPS, ADAM_WD, ADAM_STEP = 0.001, 0.9, 0.999, 1e-08, 0.01, 10
GELU_C = math.sqrt(2.0 / math.pi)
GELU_K = 0.044715

ANY = pl.BlockSpec(memory_space=pl.ANY)


def _params(sem, vmem_bytes):
    limit = int(min(max(2 * vmem_bytes, 16 * 2 ** 20), VMEM_CAP))
    return pltpu.CompilerParams(dimension_semantics=sem, vmem_limit_bytes=limit)


def _sig(x):
    return 1.0 / (1.0 + jnp.exp(-x))


def _gelu(x):
    return 0.5 * x * (1.0 + jnp.tanh(GELU_C * (x + GELU_K * x * x * x)))


def _dgelu(x):
    th = jnp.tanh(GELU_C * (x + GELU_K * x * x * x))
    return 0.5 * (1.0 + th) + 0.5 * x * (1.0 - th * th) * (GELU_C * (1.0 + 3.0 * GELU_K * x * x))


def _expm1(x):
    p = x * (1.0 + x * (1 / 2 + x * (1 / 6 + x * (1 / 24 + x * (1 / 120 + x * (1 / 720 + x * (1 / 5040)))))))
    return jnp.where(jnp.abs(x) < 0.25, p, jnp.exp(x) - 1.0)


def _accumulate(ref, val, first):
    @pl.when(first)
    def _():
        ref[...] = val

    @pl.when(jnp.logical_not(first))
    def _():
        ref[...] += val


def _rows8(cw):
    return lax.broadcasted_iota(jnp.int32, (SUBLANES, cw), 0)


def _shift_down(cur, prev, s, rows):
    return jnp.where(rows < s, pltpu.roll(prev, s, 0), pltpu.roll(cur, s, 0))


def _shift_up(cur, nxt, s, rows):
    return jnp.where(rows < SUBLANES - s, pltpu.roll(cur, SUBLANES - s, 0), pltpu.roll(nxt, SUBLANES - s, 0))


def _mm(a, b, *, M, N, K, ta=False, tb=False, b_split=1, n_split=1, a_fn=None, extras=(), epilogue=None,
        n_out=1, n_cs=0, out_dtypes=None, tm=512, tn=512, tk=512, after=None, name):
    tm, tn, tk = min(tm, M), min(tn, N), min(tk, K)
    assert M % tm == 0 and N % tn == 0 and K % tk == 0, (name, M, N, K, tm, tn, tk)
    nk = K // tk
    grid = (N // tn, M // tm, nk)
    a_spec = pl.BlockSpec((tk, tm), lambda j, i, k: (k, i)) if ta else pl.BlockSpec((tm, tk), lambda j, i, k: (i, k))
    if b_split == 1:
        b_spec = pl.BlockSpec((tn, tk), lambda j, i, k: (j, k)) if tb else pl.BlockSpec((tk, tn), lambda j, i, k: (k, j))
    elif tb:
        kb = (K // b_split) // tk
        assert kb * tk * b_split == K, name
        b_spec = pl.BlockSpec((None, tn, tk), lambda j, i, k: (k // kb, j, k % kb))
    else:
        nb = (N // b_split) // tn
        assert nb * tn * b_split == N, name
        b_spec = pl.BlockSpec((None, tk, tn), lambda j, i, k: (j // nb, k, j % nb))
    in_specs = [a_spec, b_spec]
    for arr, kind, *col_off in extras:
        off = col_off[0] if col_off else 0
        in_specs.append(pl.BlockSpec((tm, tn), lambda j, i, k, off=off: (i, off + j)) if kind == "mn"
                        else pl.BlockSpec((1, tn), lambda j, i, k: (0, j)))
    out_dtypes = (F32,) * n_out if out_dtypes is None else out_dtypes
    if n_split == 1:
        out_shape = [jax.ShapeDtypeStruct((M, N), dt) for dt in out_dtypes]
        out_specs = [pl.BlockSpec((tm, tn), lambda j, i, k: (i, j)) for _ in range(n_out)]
    else:
        assert n_out == 1
        nbo = (N // n_split) // tn
        assert nbo * tn * n_split == N, name
        out_shape = [jax.ShapeDtypeStruct((n_split, M, N // n_split), out_dtypes[0])]
        out_specs = [pl.BlockSpec((None, tm, tn), lambda j, i, k: (j // nbo, i, j % nbo))]
    out_shape += [jax.ShapeDtypeStruct((1, N), F32) for _ in range(n_cs)]
    out_specs += [pl.BlockSpec((1, tn), lambda j, i, k: (0, j)) for _ in range(n_cs)]
    ne = len(extras)
    dims = (((0 if ta else 1,), (1 if tb else 0,)), ((), ()))

    n_after = 0 if after is None else 1
    in_specs += [ANY] * n_after

    def body(*refs):
        a_ref, b_ref = refs[0], refs[1]
        ex_refs = refs[2:2 + ne]
        first_out = 2 + ne + n_after
        out_refs = refs[first_out:first_out + n_out]
        cs_refs = refs[first_out + n_out:first_out + n_out + n_cs]
        i, k = pl.program_id(1), pl.program_id(2)

        def product():
            av = a_ref[...]
            if a_fn is not None:
                av = a_fn(av.astype(F32))
            return lax.dot_general(av.astype(BF16), b_ref[...].astype(BF16), dims, preferred_element_type=F32)

        def finish(acc):
            res = (acc,) if epilogue is None else epilogue(acc, *[r[...] for r in ex_refs])
            for r, o in zip(out_refs, res[:n_out]):
                r[...] = o.astype(r.dtype)
            for r, cval in zip(cs_refs, res[n_out:]):
                _accumulate(r, jnp.sum(cval, axis=0, keepdims=True), i == 0)

        if nk == 1:
            finish(product())
            return
        acc_ref = refs[-1]

        @pl.when(k == 0)
        def _():
            acc_ref[...] = jnp.zeros_like(acc_ref)

        acc_ref[...] += product()

        @pl.when(k == nk - 1)
        def _():
            finish(acc_ref[...])

    vmem = 2 * tm * tk * a.dtype.itemsize + 2 * tk * tn * b.dtype.itemsize + (1 + 2 * n_out + 2 * ne + 2) * tm * tn * 4
    outs = pl.pallas_call(
        body, name=name, grid=grid, in_specs=in_specs, out_specs=out_specs, out_shape=out_shape,
        scratch_shapes=[pltpu.VMEM((tm, tn), F32)] if nk > 1 else [],
        compiler_params=_params(("parallel", "arbitrary", "arbitrary"), vmem),
    )(a, b, *[e[0] for e in extras], *([after] if n_after else []))
    return outs[0] if len(outs) == 1 else outs


BD_STEP = 4

def _bd(pairs, *, T, J, kb, nb, tw=False, extras=(), epilogue=None, n_out=1, n_cs=0, out_dtypes=None, tm=512, name):
    jb = BD_STEP
    assert T % tm == 0 and J % jb == 0
    grid = (J // jb, T // tm)
    npair, ne = len(pairs), len(extras)
    in_specs, args = [], []
    for arr, off, w in pairs:
        assert off % jb == 0, name
        in_specs.append(pl.BlockSpec((tm, jb * kb), lambda j, i, off=off // jb: (i, off + j)))
        in_specs.append(pl.BlockSpec((jb,) + tuple(w.shape[1:]), lambda j, i: (j, 0, 0)))
        args += [arr, w]
    for arr, kind, off in extras:
        assert off % jb == 0, name
        in_specs.append(pl.BlockSpec((tm, jb * nb), lambda j, i, off=off // jb: (i, off + j)) if kind == "tile"
                        else pl.BlockSpec((1, jb * nb), lambda j, i, off=off // jb: (0, off + j)))
        args.append(arr)
    out_dtypes = (F32,) * n_out if out_dtypes is None else out_dtypes
    out_shape = [jax.ShapeDtypeStruct((T, J * nb), dt) for dt in out_dtypes]
    out_specs = [pl.BlockSpec((tm, jb * nb), lambda j, i: (i, j)) for _ in range(n_out)]
    out_shape += [jax.ShapeDtypeStruct((1, J * nb), F32) for _ in range(n_cs)]
    out_specs += [pl.BlockSpec((1, jb * nb), lambda j, i: (0, j)) for _ in range(n_cs)]
    dims = (((1,), (1 if tw else 0,)), ((), ()))

    def body(*refs):
        ex_refs = refs[2 * npair:2 * npair + ne]
        out_refs = refs[2 * npair + ne:2 * npair + ne + n_out]
        cs_refs = refs[2 * npair + ne + n_out:]
        i = pl.program_id(1)
        for s in range(jb):
            cols_in, cols_out = pl.ds(s * kb, kb), pl.ds(s * nb, nb)
            acc = None
            for p in range(npair):
                d = lax.dot_general(refs[2 * p][:, cols_in].astype(BF16), refs[2 * p + 1][s].astype(BF16), dims,
                                    preferred_element_type=F32)
                acc = d if acc is None else acc + d
            res = (acc,) if epilogue is None else epilogue(acc, *[r[:, cols_out] for r in ex_refs])
            for r, o in zip(out_refs, res[:n_out]):
                r[:, cols_out] = o.astype(r.dtype)
            for r, cval in zip(cs_refs, res[n_out:]):
                _accumulate(r.at[:, cols_out], jnp.sum(cval, axis=0, keepdims=True), i == 0)

    vmem = jb * (2 * npair * tm * kb + 2 * npair * kb * nb + (2 * n_out + 2 * ne + 3) * tm * nb) * 4
    outs = pl.pallas_call(
        body, name=name, grid=grid, in_specs=in_specs, out_specs=out_specs, out_shape=out_shape,
        compiler_params=_params(("parallel", "arbitrary"), vmem),
    )(*args)
    return outs[0] if len(outs) == 1 else outs


def _bdw(a, a_off, b, b_off, *, T, J, kb, nb, tm=512, name):
    jb = BD_STEP
    assert T % tm == 0 and J % jb == 0 and a_off % jb == 0 and b_off % jb == 0
    a_blk, b_blk = a_off // jb, b_off // jb

    def body(a_ref, b_ref, o_ref):
        i = pl.program_id(1)
        for s in range(jb):
            d = lax.dot_general(a_ref[:, pl.ds(s * kb, kb)].astype(BF16), b_ref[:, pl.ds(s * nb, nb)].astype(BF16),
                                (((0,), (0,)), ((), ())), preferred_element_type=F32)
            _accumulate(o_ref.at[s], d, i == 0)

    return pl.pallas_call(
        body, name=name, grid=(J // jb, T // tm),
        in_specs=[pl.BlockSpec((tm, jb * kb), lambda j, i: (i, a_blk + j)),
                  pl.BlockSpec((tm, jb * nb), lambda j, i: (i, b_blk + j))],
        out_specs=pl.BlockSpec((jb, kb, nb), lambda j, i: (j, 0, 0)),
        out_shape=jax.ShapeDtypeStruct((J, kb, nb), F32),
        compiler_params=_params(("parallel", "arbitrary"), jb * (2 * tm * (kb + nb) + 3 * kb * nb) * 4),
    )(a, b)


def _bd_pack(w, q):
    g, a, b = w.shape
    eye = jnp.eye(q, dtype=w.dtype)
    return jnp.einsum("jqab,qr->jqarb", w.reshape(g // q, q, a, b), eye).reshape(g // q, q * a, q * b)


def _bd_unpack(wp, q):
    j, qa, qb = wp.shape
    a, b = qa // q, qb // q
    w5 = wp.reshape(j, q, a, q, b)
    return jnp.stack([w5[:, r, :, r, :] for r in range(q)], axis=1).reshape(j * q, a, b)


def _ew(fn, ins, *, T, C, n_out, n_cs=0, out_dtypes=None, tm=256, cw=None, name):
    cw = C if cw is None else cw
    assert T % tm == 0 and C % cw == 0
    grid = (C // cw, T // tm)
    in_specs = []
    for arr, kind, off in ins:
        in_specs.append(pl.BlockSpec((tm, cw), lambda j, i, off=off: (i, off + j)) if kind == "tile"
                        else pl.BlockSpec((arr.shape[0], cw), lambda j, i, off=off: (0, off + j)))
    out_dtypes = (F32,) * n_out if out_dtypes is None else out_dtypes
    out_shape = [jax.ShapeDtypeStruct((T, C), dt) for dt in out_dtypes]
    out_specs = [pl.BlockSpec((tm, cw), lambda j, i: (i, j)) for _ in range(n_out)]
    out_shape += [jax.ShapeDtypeStruct((1, C), F32) for _ in range(n_cs)]
    out_specs += [pl.BlockSpec((1, cw), lambda j, i: (0, j)) for _ in range(n_cs)]
    nin = len(ins)

    def body(*refs):
        i = pl.program_id(1)
        res = fn(*[r[...].astype(F32) for r in refs[:nin]])
        for r, o in zip(refs[nin:nin + n_out], res[:n_out]):
            r[...] = o.astype(r.dtype)
        for r, cval in zip(refs[nin + n_out:], res[n_out:]):
            _accumulate(r, jnp.sum(cval, axis=0, keepdims=True), i == 0)

    vmem = (2 * nin + 2 * n_out + 6) * tm * cw * 4
    outs = pl.pallas_call(
        body, name=name, grid=grid, in_specs=in_specs, out_specs=out_specs, out_shape=out_shape,
        compiler_params=_params(("parallel", "arbitrary"), vmem),
    )(*[arr for arr, _, _ in ins])
    return outs[0] if len(outs) == 1 else outs


def _ln_stats(s):
    mu = jnp.mean(s, axis=-1, keepdims=True)
    d = s - mu
    var = jnp.mean(d * d, axis=-1, keepdims=True)
    rstd = lax.rsqrt(var + LN_EPS)
    return d * rstd, rstd


def _ln_bwd(dy, g, xhat, rstd):
    dxh = dy * g
    m1 = jnp.mean(dxh, axis=-1, keepdims=True)
    m2 = jnp.mean(dxh * xhat, axis=-1, keepdims=True)
    return rstd * (dxh - m1 - xhat * m2)


def _conv_fwd(z, conv_w, conv_b, *, T, C, tm=512, cw=1024, name):
    ng, hb = tm // SUBLANES, tm // SUBLANES

    def body(x_ref, halo_ref, w_ref, b_ref, o_ref):
        it = pl.program_id(1)
        rows = _rows8(cw)
        halo = jnp.where(it == 0, 0.0, halo_ref[...])
        w = w_ref[...]
        bias = b_ref[...]

        def group(g, carry):
            off = pl.multiple_of(g * SUBLANES, SUBLANES)
            cur = x_ref[pl.ds(off, SUBLANES), :]
            prev = x_ref[pl.ds(pl.multiple_of(jnp.maximum(off - SUBLANES, 0), SUBLANES), SUBLANES), :]
            prev = jnp.where(g == 0, halo, prev)
            acc = cur * w[3:4] + bias
            for s in (1, 2, 3):
                acc = acc + _shift_down(cur, prev, s, rows) * w[3 - s:4 - s]
            o_ref[pl.ds(off, SUBLANES), :] = acc
            return carry

        lax.fori_loop(0, ng, group, 0, unroll=2)

    return pl.pallas_call(
        body, name=name, grid=(C // cw, T // tm),
        in_specs=[pl.BlockSpec((tm, cw), lambda j, i: (i, j)),
                  pl.BlockSpec((SUBLANES, cw), lambda j, i: (jnp.maximum(i * hb - 1, 0), j)),
                  pl.BlockSpec((4, cw), lambda j, i: (0, j)), pl.BlockSpec((1, cw), lambda j, i: (0, j))],
        out_specs=pl.BlockSpec((tm, cw), lambda j, i: (i, j)),
        out_shape=jax.ShapeDtypeStruct((T, C), F32),
        compiler_params=_params(("parallel", "arbitrary"), 5 * tm * cw * 4),
    )(z, z, conv_w, conv_b)


def _conv_bwd(dxc, z, conv_w, *, T, C, tm=512, cw=512, name):
    ng, hb, last = tm // SUBLANES, tm // SUBLANES, T // SUBLANES - 1
    nt = T // tm

    def body(d_ref, dn_ref, x_ref, xp_ref, w_ref, o_ref, sums_ref):
        it = pl.program_id(1)
        rows = _rows8(cw)
        dnext = jnp.where(it == nt - 1, 0.0, dn_ref[...])
        xprev = jnp.where(it == 0, 0.0, xp_ref[...])
        w = w_ref[...]

        def group(g, accs):
            off = pl.multiple_of(g * SUBLANES, SUBLANES)
            dcur = d_ref[pl.ds(off, SUBLANES), :]
            dnx = d_ref[pl.ds(pl.multiple_of(jnp.minimum(off + SUBLANES, tm - SUBLANES), SUBLANES), SUBLANES), :]
            dnx = jnp.where(g == ng - 1, dnext, dnx)
            xcur = x_ref[pl.ds(off, SUBLANES), :]
            xpv = x_ref[pl.ds(pl.multiple_of(jnp.maximum(off - SUBLANES, 0), SUBLANES), SUBLANES), :]
            xpv = jnp.where(g == 0, xprev, xpv)
            acc = dcur * w[3:4]
            for s in (1, 2, 3):
                acc = acc + _shift_up(dcur, dnx, s, rows) * w[3 - s:4 - s]
            o_ref[pl.ds(off, SUBLANES), :] = acc
            a0, a1, a2, a3, ab = accs
            a0 = a0 + dcur * _shift_down(xcur, xpv, 3, rows)
            a1 = a1 + dcur * _shift_down(xcur, xpv, 2, rows)
            a2 = a2 + dcur * _shift_down(xcur, xpv, 1, rows)
            a3 = a3 + dcur * xcur
            return a0, a1, a2, a3, ab + dcur

        zero = jnp.zeros((SUBLANES, cw), F32)
        accs = lax.fori_loop(0, ng, group, (zero,) * 5, unroll=2)
        sums = jnp.zeros((SUBLANES, cw), F32)
        for k, a in enumerate(accs):
            sums = jnp.where(rows == k, jnp.sum(a, axis=0, keepdims=True), sums)
        _accumulate(sums_ref, sums, it == 0)

    tile = pl.BlockSpec((tm, cw), lambda j, i: (i, j))
    return pl.pallas_call(
        body, name=name, grid=(C // cw, nt),
        in_specs=[tile, pl.BlockSpec((SUBLANES, cw), lambda j, i: (jnp.minimum((i + 1) * hb, last), j)),
                  tile, pl.BlockSpec((SUBLANES, cw), lambda j, i: (jnp.maximum(i * hb - 1, 0), j)),
                  pl.BlockSpec((4, cw), lambda j, i: (0, j))],
        out_specs=[tile, pl.BlockSpec((SUBLANES, cw), lambda j, i: (0, j))],
        out_shape=[jax.ShapeDtypeStruct((T, C), F32), jax.ShapeDtypeStruct((SUBLANES, C), F32)],
        compiler_params=_params(("parallel", "arbitrary"), 7 * tm * cw * 4),
    )(dxc, dxc, z, z, conv_w)


def _rg_coeffs(r, ig, xc, sp):
    la = (-RG_C) * r * sp
    a = jnp.exp(la)
    m = jnp.sqrt(-_expm1(2.0 * la))
    return a, m, m * (ig * xc)


def _rg_scan_fwd(z, ri, xc, sp, *, T, C, gate_off, tm=512, cw=256, name):
    rows16 = 2 * SUBLANES
    nq = tm // rows16

    def body(gate_ref, r_ref, i_ref, xc_ref, sp_ref, h_ref, p_ref, carry_ref):
        it = pl.program_id(1)

        @pl.when(it == 0)
        def _():
            carry_ref[...] = jnp.zeros_like(carry_ref)

        rows = _rows8(cw)
        sp_row = sp_ref[...]

        def pair(q, carry):
            base = pl.multiple_of(q * rows16, rows16)
            halves = []
            for half in range(2):
                sl = pl.ds(pl.multiple_of(base + half * SUBLANES, SUBLANES), SUBLANES)
                a, _, b = _rg_coeffs(r_ref[sl, :], i_ref[sl, :], xc_ref[sl, :], sp_row)
                for s in (1, 2, 4):
                    keep = rows >= s
                    sa = jnp.where(keep, pltpu.roll(a, s, 0), 1.0)
                    sb = jnp.where(keep, pltpu.roll(b, s, 0), 0.0)
                    b = b + a * sb
                    a = a * sa
                h = b + a * carry
                h_ref[sl, :] = h
                halves.append(h * _gelu(gate_ref[sl, :]))
                carry = h[SUBLANES - 1:SUBLANES, :]
            p_ref[pl.ds(base, rows16), :] = jnp.concatenate(halves, axis=0).astype(p_ref.dtype)
            return carry

        last = lax.fori_loop(0, nq, pair, carry_ref[0:1, :], unroll=2)
        carry_ref[...] = jnp.broadcast_to(last, carry_ref.shape)

    tile = pl.BlockSpec((tm, cw), lambda j, i: (i, j))
    gate_blk = gate_off // cw
    return pl.pallas_call(
        body, name=name, grid=(C // cw, T // tm),
        in_specs=[pl.BlockSpec((tm, cw), lambda j, i: (i, gate_blk + j)),
                  pl.BlockSpec((tm, cw), lambda j, i: (i, 2 * j)), pl.BlockSpec((tm, cw), lambda j, i: (i, 2 * j + 1)),
                  tile, pl.BlockSpec((1, cw), lambda j, i: (0, j))],
        out_specs=[tile, tile],
        out_shape=[jax.ShapeDtypeStruct((T, C), F32), jax.ShapeDtypeStruct((T, C), BF16)],
        scratch_shapes=[pltpu.VMEM((SUBLANES, cw), F32)],
        compiler_params=_params(("parallel", "arbitrary"), 12 * tm * cw * 4),
    )(z, ri, ri, xc, sp)


def _rg_scan_bwd(dh, h, ri, xc, sp, *, T, C, tm=512, cw=256, name):
    ng, hb, nt = tm // SUBLANES, tm // SUBLANES, T // tm

    def body(dh_ref, h_ref, hp_ref, r_ref, i_ref, xc_ref, sp_ref,
             drai_ref, dxc_ref, crai_ref, csp_ref, cg_ref, ca_ref):
        step = pl.program_id(1)

        @pl.when(step == 0)
        def _():
            cg_ref[...] = jnp.zeros_like(cg_ref)
            ca_ref[...] = jnp.zeros_like(ca_ref)

        rows = _rows8(cw)
        sp_row = sp_ref[...]
        hhalo = jnp.where(step == nt - 1, 0.0, hp_ref[...])

        def group(gi, carry):
            g_next, a_next, s_ra, s_ia, s_sp = carry
            g = ng - 1 - gi
            off = pl.multiple_of(g * SUBLANES, SUBLANES)
            sl = pl.ds(off, SUBLANES)
            rr, ii, xx = r_ref[sl, :], i_ref[sl, :], xc_ref[sl, :]
            a, m, _ = _rg_coeffs(rr, ii, xx, sp_row)
            hh = h_ref[sl, :]
            hpv = h_ref[pl.ds(pl.multiple_of(jnp.maximum(off - SUBLANES, 0), SUBLANES), SUBLANES), :]
            hpv = jnp.where(g == 0, hhalo, hpv)
            hprev = _shift_down(hh, hpv, 1, rows)
            d = dh_ref[sl, :]
            c = jnp.where(rows < SUBLANES - 1, pltpu.roll(a, SUBLANES - 1, 0), a_next)
            for s in (1, 2, 4):
                keep = rows < SUBLANES - s
                sc = jnp.where(keep, pltpu.roll(c, SUBLANES - s, 0), 1.0)
                sd = jnp.where(keep, pltpu.roll(d, SUBLANES - s, 0), 0.0)
                d = d + c * sd
                c = c * sc
            gg = d + c * g_next
            da = gg * hprev
            dm = gg * (ii * xx)
            di = gg * (m * xx)
            dxc_ref[sl, :] = gg * (m * ii)
            dla = da * a - dm * (a * a / m)
            dra = dla * ((-RG_C) * sp_row) * (rr * (1.0 - rr))
            dia = di * (ii * (1.0 - ii))
            drai_ref[sl, pl.ds(0, cw)] = dra
            drai_ref[sl, pl.ds(cw, cw)] = dia
            return (gg[0:1, :], a[0:1, :], s_ra + dra, s_ia + dia, s_sp + dla * ((-RG_C) * rr))

        zero = jnp.zeros((SUBLANES, cw), F32)
        g_first, a_first, s_ra, s_ia, s_sp = lax.fori_loop(
            0, ng, group, (cg_ref[0:1, :], ca_ref[0:1, :], zero, zero, zero), unroll=2)
        cg_ref[...] = jnp.broadcast_to(g_first, cg_ref.shape)
        ca_ref[...] = jnp.broadcast_to(a_first, ca_ref.shape)
        for ref, acc in ((crai_ref.at[:, pl.ds(0, cw)], s_ra), (crai_ref.at[:, pl.ds(cw, cw)], s_ia), (csp_ref, s_sp)):
            _accumulate(ref, jnp.sum(acc, axis=0, keepdims=True), step == 0)

    tile = pl.BlockSpec((tm, cw), lambda j, i: (nt - 1 - i, j))
    wide = pl.BlockSpec((tm, 2 * cw), lambda j, i: (nt - 1 - i, j))
    vec = pl.BlockSpec((1, cw), lambda j, i: (0, j))
    return pl.pallas_call(
        body, name=name, grid=(C // cw, nt),
        in_specs=[tile, tile, pl.BlockSpec((SUBLANES, cw), lambda j, i: (jnp.maximum((nt - 1 - i) * hb - 1, 0), j)),
                  pl.BlockSpec((tm, cw), lambda j, i: (nt - 1 - i, 2 * j)),
                  pl.BlockSpec((tm, cw), lambda j, i: (nt - 1 - i, 2 * j + 1)), tile, vec],
        out_specs=[wide, tile, pl.BlockSpec((1, 2 * cw), lambda j, i: (0, j)), vec],
        out_shape=[jax.ShapeDtypeStruct((T, 2 * C), F32), jax.ShapeDtypeStruct((T, C), F32),
                   jax.ShapeDtypeStruct((1, 2 * C), F32), jax.ShapeDtypeStruct((1, C), F32)],
        scratch_shapes=[pltpu.VMEM((SUBLANES, cw), F32), pltpu.VMEM((SUBLANES, cw), F32)],
        compiler_params=_params(("parallel", "arbitrary"), 20 * tm * cw * 4),
    )(dh, h, h, ri, ri, xc, sp)


def _cscan_tables(lr, li, reverse):
    lam = (lr.reshape(-1), -li.reshape(-1) if reverse else li.reshape(-1))

    def mul(p, q):
        return p[0] * q[0] - p[1] * q[1], p[0] * q[1] + p[1] * q[0]

    pows = [lam]
    for _ in range(SUBLANES - 1):
        pows.append(mul(pows[-1], lam))
    zero = jnp.zeros_like(lam[0])
    tab = jnp.stack([pows[0][0], pows[0][1], pows[1][0], pows[1][1], pows[3][0], pows[3][1], zero, zero])
    if reverse:
        pows = pows[::-1]
    return tab, jnp.stack([p[0] for p in pows]), jnp.stack([p[1] for p in pows])


def _power_slabs(lr, li, n):
    pr, pi = lr.reshape(1, -1), li.reshape(1, -1)
    while pr.shape[0] < n:
        tr, ti = pr[-1:], pi[-1:]
        pr, pi = (jnp.concatenate([pr, pr * tr - pi * ti], axis=0), jnp.concatenate([pi, pr * ti + pi * tr], axis=0))
    return jnp.repeat(pr, SUBLANES, axis=0), jnp.repeat(pi, SUBLANES, axis=0), pr[-1], pi[-1]


def _rows_to_segments(src_ref, dst_ref):
    seg = src_ref.shape[0] // SUBLANES
    for g in range(seg):
        dst_ref[pl.ds(g * SUBLANES, SUBLANES), :] = src_ref[pl.ds(g, SUBLANES, stride=seg), :].astype(dst_ref.dtype)


def _segments_to_rows(src_ref, dst_ref):
    seg = src_ref.shape[0] // SUBLANES
    for r in range(SUBLANES):
        dst_ref[pl.ds(r * seg, seg), :] = src_ref[pl.ds(r, seg, stride=SUBLANES), :].astype(dst_ref.dtype)


def _seg_scan_tile(xr_ref, xi_ref, pbr_ref, pbi_ref, tab_ref, pwr_ref, pwi_ref, cr_ref, ci_ref, *, reverse, h=None):
    tm, cw = xr_ref.shape
    seg = tm // SUBLANES
    rows = _rows8(cw)
    sign = -1.0 if reverse else 1.0
    l_re, l_im = pbr_ref[0:1, :], sign * pbi_ref[0:1, :]

    def slab(g):
        return pl.ds(pl.multiple_of(g * SUBLANES, SUBLANES), SUBLANES)

    def local(k, state):
        sl = slab(seg - 1 - k if reverse else k)
        sr, si = state
        nr = xr_ref[sl, :] + (l_re * sr - l_im * si)
        ni = xi_ref[sl, :] + (l_re * si + l_im * sr)
        xr_ref[sl, :] = nr
        xi_ref[sl, :] = ni
        return nr, ni

    zero = jnp.zeros((SUBLANES, cw), F32)
    er, ei = lax.fori_loop(0, seg, local, (zero, zero), unroll=2)

    for k, s in enumerate((1, 2, 4)):
        shift = SUBLANES - s if reverse else s
        keep = rows < SUBLANES - s if reverse else rows >= s
        sr = jnp.where(keep, pltpu.roll(er, shift, 0), 0.0)
        si = jnp.where(keep, pltpu.roll(ei, shift, 0), 0.0)
        m_re, m_im = tab_ref[2 * k:2 * k + 1, :], tab_ref[2 * k + 1:2 * k + 2, :]
        er, ei = er + (m_re * sr - m_im * si), ei + (m_re * si + m_im * sr)
    cin_r, cin_i = cr_ref[0:1, :], ci_ref[0:1, :]
    pwr, pwi = pwr_ref[...], pwi_ref[...]
    er, ei = er + (pwr * cin_r - pwi * cin_i), ei + (pwr * cin_i + pwi * cin_r)
    if reverse:
        ent_r = jnp.where(rows == SUBLANES - 1, cin_r, pltpu.roll(er, SUBLANES - 1, 0))
        ent_i = jnp.where(rows == SUBLANES - 1, cin_i, pltpu.roll(ei, SUBLANES - 1, 0))
        out_r, out_i = er[0:1, :], ei[0:1, :]
    else:
        ent_r = jnp.where(rows == 0, cin_r, pltpu.roll(er, 1, 0))
        ent_i = jnp.where(rows == 0, cin_i, pltpu.roll(ei, 1, 0))
        out_r, out_i = er[SUBLANES - 1:SUBLANES, :], ei[SUBLANES - 1:SUBLANES, :]
    cr_ref[...] = jnp.broadcast_to(out_r, cr_ref.shape)
    ci_ref[...] = jnp.broadcast_to(out_i, ci_ref.shape)

    if h is not None:
        hr_ref, hi_ref, hr_last, hi_last = h
        hr_wrap = _shift_down(hr_ref[pl.ds(tm - SUBLANES, SUBLANES), :], hr_last, 1, rows)
        hi_wrap = _shift_down(hi_ref[pl.ds(tm - SUBLANES, SUBLANES), :], hi_last, 1, rows)

    def fix(g, sums):
        sl = slab(g)
        power = slab(seg - 1 - g) if reverse else sl
        pr, pi = pbr_ref[power, :], sign * pbi_ref[power, :]
        nr = xr_ref[sl, :] + (pr * ent_r - pi * ent_i)
        ni = xi_ref[sl, :] + (pr * ent_i + pi * ent_r)
        xr_ref[sl, :] = nr
        xi_ref[sl, :] = ni
        if h is None:
            return sums
        before = slab(jnp.maximum(g - 1, 0))
        hr1 = jnp.where(g == 0, hr_wrap, hr_ref[before, :])
        hi1 = jnp.where(g == 0, hi_wrap, hi_ref[before, :])
        return sums[0] + (nr * hr1 + ni * hi1), sums[1] + (ni * hr1 - nr * hi1)

    return lax.fori_loop(0, seg, fix, (zero, zero) if h is not None else (), unroll=2)


S5_TILE = 512


def _s5_fwd(z, u_off, wb_re, wb_im, wc_re, wc_im_neg, d_row, powers, *, T, tm=S5_TILE, name):
    J, ku, kp = wb_re.shape
    nt = T // tm
    pb_re, pb_im, top_re, top_im = powers
    tab, pw_re, pw_im = _cscan_tables(top_re, top_im, False)
    u_blk = u_off // ku

    def body(u_ref, wbr_ref, wbi_ref, wcr_ref, wci_ref, d_ref, pbr_ref, pbi_ref, tab_ref, pwr_ref, pwi_ref,
             hr_ref, hi_ref, y_ref, yg_ref, cr_ref, ci_ref, us_ref, ys_ref):
        @pl.when(pl.program_id(1) == 0)
        def _():
            cr_ref[...] = jnp.zeros_like(cr_ref)
            ci_ref[...] = jnp.zeros_like(ci_ref)

        _rows_to_segments(u_ref, us_ref)
        u = us_ref[...]
        ub = u.astype(BF16)
        hr_ref[...] = jnp.dot(ub, wbr_ref[...], preferred_element_type=F32)
        hi_ref[...] = jnp.dot(ub, wbi_ref[...], preferred_element_type=F32)
        _seg_scan_tile(hr_ref, hi_ref, pbr_ref, pbi_ref, tab_ref, pwr_ref, pwi_ref, cr_ref, ci_ref, reverse=False)
        y = (jnp.dot(hr_ref[...].astype(BF16), wcr_ref[...], preferred_element_type=F32)
             + jnp.dot(hi_ref[...].astype(BF16), wci_ref[...], preferred_element_type=F32) + d_ref[...] * u)
        ys_ref[...] = y
        _segments_to_rows(ys_ref, y_ref)
        ys_ref[...] = _gelu(y)
        _segments_to_rows(ys_ref, yg_ref)

    wb_spec = pl.BlockSpec((None, ku, kp), lambda j, i: (j, 0, 0))
    wc_spec = pl.BlockSpec((None, kp, ku), lambda j, i: (j, 0, 0))
    small = pl.BlockSpec((SUBLANES, kp), lambda j, i: (0, j))
    slabs = pl.BlockSpec((tm, kp), lambda j, i: (0, j))
    state = pl.BlockSpec((tm, kp), lambda j, i: (i, j))
    chan = pl.BlockSpec((tm, ku), lambda j, i: (i, j))
    return pl.pallas_call(
        body, name=name, grid=(J, nt),
        in_specs=[pl.BlockSpec((tm, ku), lambda j, i: (i, u_blk + j)), wb_spec, wb_spec, wc_spec, wc_spec,
                  pl.BlockSpec((1, ku), lambda j, i: (0, j)), slabs, slabs, small, small, small],
        out_specs=[state, state, chan, chan],
        out_shape=[jax.ShapeDtypeStruct((T, J * kp), F32)] * 2
        + [jax.ShapeDtypeStruct((T, J * ku), F32), jax.ShapeDtypeStruct((T, J * ku), BF16)],
        scratch_shapes=[pltpu.VMEM((SUBLANES, kp), F32), pltpu.VMEM((SUBLANES, kp), F32),
                        pltpu.VMEM((tm, ku), F32), pltpu.VMEM((tm, ku), F32)],
        compiler_params=_params(("parallel", "arbitrary"), 14 * tm * kp * 4),
    )(z, wb_re, wb_im, wc_re, wc_im_neg, d_row, pb_re, pb_im, tab, pw_re, pw_im)


def _s5_bwd(dy, z, u_off, h_re, h_im, wb_re, wb_im, wc_re, wc_im_neg, d_row, powers, *, T, tm=S5_TILE, name):
    J, ku, kp = wb_re.shape
    nt, hb = T // tm, tm // SUBLANES
    pb_re, pb_im, top_re, top_im = powers
    tab, pw_re, pw_im = _cscan_tables(top_re, top_im, True)
    u_blk = u_off // ku
    contract_rows = (((0,), (0,)), ((), ()))
    contract_cols = (((1,), (1,)), ((), ()))

    def body(dy_ref, u_ref, hr_ref, hrp_ref, hi_ref, hip_ref, wbr_ref, wbi_ref, wcr_ref, wci_ref, d_ref,
             pbr_ref, pbi_ref, tab_ref, pwr_ref, pwi_ref,
             du_ref, dlr_ref, dli_ref, dd_ref, dwbr_ref, dwbi_ref, dwcr_ref, dwci_ref,
             gr_ref, gi_ref, cr_ref, ci_ref, dys_ref, us_ref):
        step = pl.program_id(1)
        first = step == 0

        @pl.when(first)
        def _():
            cr_ref[...] = jnp.zeros_like(cr_ref)
            ci_ref[...] = jnp.zeros_like(ci_ref)

        _rows_to_segments(dy_ref, dys_ref)
        _rows_to_segments(u_ref, us_ref)
        dy_t, u = dys_ref[...], us_ref[...]
        dyb, ub = dy_t.astype(BF16), u.astype(BF16)
        gr_ref[...] = lax.dot_general(dyb, wcr_ref[...], contract_cols, preferred_element_type=F32)
        gi_ref[...] = lax.dot_general(dyb, wci_ref[...], contract_cols, preferred_element_type=F32)
        hr_last = jnp.where(step == nt - 1, 0.0, hrp_ref[...])
        hi_last = jnp.where(step == nt - 1, 0.0, hip_ref[...])
        s_re, s_im = _seg_scan_tile(gr_ref, gi_ref, pbr_ref, pbi_ref, tab_ref, pwr_ref, pwi_ref, cr_ref, ci_ref,
                                    reverse=True, h=(hr_ref, hi_ref, hr_last, hi_last))
        _accumulate(dlr_ref, jnp.sum(s_re, axis=0, keepdims=True), first)
        _accumulate(dli_ref, jnp.sum(s_im, axis=0, keepdims=True), first)
        grb, gib = gr_ref[...].astype(BF16), gi_ref[...].astype(BF16)
        du = (lax.dot_general(grb, wbr_ref[...], contract_cols, preferred_element_type=F32)
              + lax.dot_general(gib, wbi_ref[...], contract_cols, preferred_element_type=F32) + dy_t * d_ref[...])
        dys_ref[...] = du
        _segments_to_rows(dys_ref, du_ref)
        _accumulate(dd_ref, jnp.sum(dy_t * u, axis=0, keepdims=True), first)
        _accumulate(dwbr_ref, lax.dot_general(ub, grb, contract_rows, preferred_element_type=F32), first)
        _accumulate(dwbi_ref, lax.dot_general(ub, gib, contract_rows, preferred_element_type=F32), first)
        _accumulate(dwcr_ref, lax.dot_general(hr_ref[...].astype(BF16), dyb, contract_rows,
                                              preferred_element_type=F32), first)
        _accumulate(dwci_ref, lax.dot_general(hi_ref[...].astype(BF16), dyb, contract_rows,
                                              preferred_element_type=F32), first)

    def tix(i):
        return nt - 1 - i

    wb_spec = pl.BlockSpec((None, ku, kp), lambda j, i: (j, 0, 0))
    wc_spec = pl.BlockSpec((None, kp, ku), lambda j, i: (j, 0, 0))
    small = pl.BlockSpec((SUBLANES, kp), lambda j, i: (0, j))
    state = pl.BlockSpec((tm, kp), lambda j, i: (tix(i), j))
    halo = pl.BlockSpec((SUBLANES, kp), lambda j, i: (jnp.maximum(tix(i) * hb - 1, 0), j))
    chan = pl.BlockSpec((tm, ku), lambda j, i: (tix(i), j))
    svec = pl.BlockSpec((1, kp), lambda j, i: (0, j))
    cvec = pl.BlockSpec((1, ku), lambda j, i: (0, j))
    slabs = pl.BlockSpec((tm, kp), lambda j, i: (0, j))
    return pl.pallas_call(
        body, name=name, grid=(J, nt),
        in_specs=[chan, pl.BlockSpec((tm, ku), lambda j, i: (tix(i), u_blk + j)), state, halo, state, halo,
                  wb_spec, wb_spec, wc_spec, wc_spec, cvec, slabs, slabs, small, small, small],
        out_specs=[chan, svec, svec, cvec, wb_spec, wb_spec, wc_spec, wc_spec],
        out_shape=[jax.ShapeDtypeStruct((T, J * ku), BF16), jax.ShapeDtypeStruct((1, J * kp), F32),
                   jax.ShapeDtypeStruct((1, J * kp), F32), jax.ShapeDtypeStruct((1, J * ku), F32),
                   jax.ShapeDtypeStruct((J, ku, kp), F32), jax.ShapeDtypeStruct((J, ku, kp), F32),
                   jax.ShapeDtypeStruct((J, kp, ku), F32), jax.ShapeDtypeStruct((J, kp, ku), F32)],
        scratch_shapes=[pltpu.VMEM((tm, kp), F32), pltpu.VMEM((tm, kp), F32),
                        pltpu.VMEM((SUBLANES, kp), F32), pltpu.VMEM((SUBLANES, kp), F32),
                        pltpu.VMEM((tm, ku), F32), pltpu.VMEM((tm, ku), F32)],
        compiler_params=_params(("parallel", "arbitrary"), 16 * tm * kp * 4),
    )(dy, z, h_re, h_re, h_im, h_im, wb_re, wb_im, wc_re, wc_im_neg, d_row, pb_re, pb_im, tab, pw_re, pw_im)


def _mesh_pos():
    return lax.axis_index("x"), lax.axis_index("y"), lax.axis_index("c")


def _dev_index(px, py, pc):
    return 4 * px + 2 * py + pc


def _all_gather(shards, name):
    n = len(shards)

    def body(*refs):
        ins, outs = refs[:n], refs[n:2 * n]
        send_sems, recv_sems, local_sems = refs[2 * n:]
        x, y, c = _mesh_pos()
        me, sibling = (x, y, c), (x, y, 1 - c)
        chips = [(1 - x, y), (x, 1 - y), (1 - x, 1 - y)]

        def copy(a, k, block, to, src=None):
            dst = outs[a].at[_dev_index(*block)]
            return pltpu.make_async_remote_copy(
                src_ref=dst if src is None else src, dst_ref=dst, send_sem=send_sems.at[a * 7 + k],
                recv_sem=recv_sems.at[a * 7 + k], device_id=to, device_id_type=MESH)

        mine = [pltpu.make_async_copy(ins[a], outs[a].at[_dev_index(*me)], local_sems.at[a]) for a in range(n)]
        for cp in mine:
            cp.start()
        first = []
        for a in range(n):
            first.append(copy(a, 0, me, sibling, src=ins[a]))
            first += [copy(a, 1 + j, me, (*chip, c), src=ins[a]) for j, chip in enumerate(chips)]
        for cp in first:
            cp.start()
        passed = []
        for j, chip in enumerate(chips):
            for a in range(n):
                copy(a, 1 + j, (*chip, c), me).wait_recv()
                fwd = copy(a, 4 + j, (*chip, c), sibling)
                fwd.start()
                passed.append(fwd)
        for a in range(n):
            copy(a, 0, sibling, me).wait_recv()
            for j, chip in enumerate(chips):
                copy(a, 4 + j, (*chip, 1 - c), me).wait_recv()
        for cp in first + passed:
            cp.wait_send()
        for cp in mine:
            cp.wait()

    return pl.pallas_call(
        body, name=name, in_specs=[ANY] * n, out_specs=[ANY] * n,
        out_shape=[jax.ShapeDtypeStruct((N_DEV,) + s.shape, s.dtype) for s in shards],
        scratch_shapes=[pltpu.SemaphoreType.DMA((7 * n,)), pltpu.SemaphoreType.DMA((7 * n,)),
                        pltpu.SemaphoreType.DMA((n,))],
    )(*shards)


def _exchange_blocks(parts, name):
    n = len(parts)
    relations = [(dx, dy, dc) for dx in (0, 1) for dy in (0, 1) for dc in (0, 1) if (dx, dy, dc) != (0, 0, 0)]

    def body(*refs):
        ins, outs = refs[:n], refs[n:2 * n]
        send_sems, recv_sems, local_sems = refs[2 * n:]
        x, y, c = _mesh_pos()
        me = _dev_index(x, y, c)
        mine = [pltpu.make_async_copy(ins[a].at[me], outs[a].at[me], local_sems.at[a]) for a in range(n)]
        for cp in mine:
            cp.start()
        copies = []
        for k, (dx, dy, dc) in enumerate(relations):
            peer = (x + dx - 2 * x * dx, y + dy - 2 * y * dy, c + dc - 2 * c * dc)
            for a in range(n):
                copies.append((pltpu.make_async_remote_copy(
                    src_ref=ins[a].at[_dev_index(*peer)], dst_ref=outs[a].at[me], send_sem=send_sems.at[a * 7 + k],
                    recv_sem=recv_sems.at[a * 7 + k], device_id=peer, device_id_type=MESH),
                    pltpu.make_async_remote_copy(
                    src_ref=ins[a].at[_dev_index(*peer)], dst_ref=outs[a].at[_dev_index(*peer)],
                    send_sem=send_sems.at[a * 7 + k], recv_sem=recv_sems.at[a * 7 + k], device_id=peer,
                    device_id_type=MESH)))
        for send, _ in copies:
            send.start()
        for _, recv in copies:
            recv.wait_recv()
        for send, _ in copies:
            send.wait_send()
        for cp in mine:
            cp.wait()

    return pl.pallas_call(
        body, name=name, in_specs=[ANY] * n, out_specs=[ANY] * n,
        out_shape=[jax.ShapeDtypeStruct(p.shape, p.dtype) for p in parts],
        scratch_shapes=[pltpu.SemaphoreType.DMA((7 * n,)), pltpu.SemaphoreType.DMA((7 * n,)),
                        pltpu.SemaphoreType.DMA((n,))],
    )(*parts)


HBM = pl.BlockSpec(memory_space=pltpu.HBM)
SEM = pl.BlockSpec(memory_space=pltpu.SEMAPHORE)
EFFECT = pltpu.SideEffectType.DATAFLOW_SIDE_EFFECTING
RELATIONS = [(dx, dy, dc) for dx in (0, 1) for dy in (0, 1) for dc in (0, 1) if (dx, dy, dc) != (0, 0, 0)]


def _peer(rel):
    x, y, c = _mesh_pos()
    dx, dy, dc = rel
    return (x + dx - 2 * x * dx, y + dy - 2 * y * dy, c + dc - 2 * c * dc)


def _split_copy(src_ref, land_ref, send_sems, recv_sems, k, scatter, incoming):
    peer = _peer(RELATIONS[k])
    me = _dev_index(*_mesh_pos())
    src = src_ref.at[_dev_index(*peer)] if scatter else src_ref
    dst = land_ref.at[_dev_index(*peer) if incoming else me]
    return pltpu.make_async_remote_copy(src_ref=src, dst_ref=dst, send_sem=send_sems.at[k], recv_sem=recv_sems.at[k],
                                        device_id=peer, device_id_type=MESH)


def _exchange_start(srcs, lands, *, scatter, name):
    n = len(srcs)

    def body(*refs):
        src_refs, land_refs = refs[:n], refs[n:2 * n]
        send, recv = refs[2 * n:3 * n], refs[3 * n:4 * n]
        token = refs[-1]
        for k in range(len(RELATIONS)):
            for a in range(n):
                _split_copy(src_refs[a], land_refs[a], send[a], recv[a], k, scatter, incoming=False).start()
        token[...] = jnp.zeros_like(token)

    n_rel = len(RELATIONS)
    outs = pl.pallas_call(
        body, name=name, in_specs=[HBM] * (2 * n),
        out_shape=[pltpu.SemaphoreType.DMA((n_rel,))] * (2 * n)
        + [pltpu.HBM(s.shape, s.dtype) for s in srcs] + [pltpu.HBM(s.shape, s.dtype) for s in lands]
        + [jax.ShapeDtypeStruct((SUBLANES, LANES), F32)],
        out_specs=[SEM] * (2 * n) + [HBM] * (2 * n) + [pl.BlockSpec(memory_space=pltpu.VMEM)],
        input_output_aliases={**{a: 2 * n + a for a in range(n)}, **{n + a: 3 * n + a for a in range(n)}},
        compiler_params=pltpu.CompilerParams(has_side_effects=EFFECT),
    )(*[pltpu.with_memory_space_constraint(s, pltpu.HBM) for s in srcs],
      *[pltpu.with_memory_space_constraint(s, pltpu.HBM) for s in lands])
    per_array = [(outs[a], outs[n + a], outs[2 * n + a], outs[3 * n + a]) for a in range(n)]
    return per_array, outs[-1]


def _exchange_wait(handle, after, *, scatter, name):
    send_sems, recv_sems, src_thru, land_thru = handle

    def body(src_ref, land_ref, send, recv, after_ref, src_dead, got_ref):
        for k in range(len(RELATIONS)):
            cp = _split_copy(src_ref, land_ref, send, recv, k, scatter, incoming=True)
            cp.wait_send()
            cp.wait_recv()

    return pl.pallas_call(
        body, name=name, in_specs=[HBM, HBM, SEM, SEM, ANY],
        out_shape=[pltpu.HBM(src_thru.shape, src_thru.dtype), pltpu.HBM(land_thru.shape, land_thru.dtype)],
        out_specs=[HBM, HBM], input_output_aliases={0: 0, 1: 1},
        compiler_params=pltpu.CompilerParams(has_side_effects=EFFECT),
    )(src_thru, land_thru, send_sems, recv_sems, after)[1]


def _landing_zone(own_block):
    me = _dev_index(*_mesh_pos())
    zone = lax.empty((N_DEV,) + own_block.shape, own_block.dtype)
    return lax.dynamic_update_index_in_dim(zone, own_block, me, 0)


def _row_tile(rows, want):
    t = min(want, rows) // SUBLANES * SUBLANES
    while rows % t:
        t -= SUBLANES
    return t


def _sum_slots(recv, *, tr, name):
    s_, r_, c_ = recv.shape
    tr = _row_tile(r_, tr)

    def body(g_ref, o_ref):
        acc = g_ref[0]
        for s in range(1, s_):
            acc = acc + g_ref[s]
        o_ref[...] = acc

    return pl.pallas_call(
        body, name=name, grid=(r_ // tr,),
        in_specs=[pl.BlockSpec((s_, tr, c_), lambda i: (0, i, 0))],
        out_specs=pl.BlockSpec((tr, c_), lambda i: (i, 0)),
        out_shape=jax.ShapeDtypeStruct((r_, c_), F32),
        compiler_params=_params(("parallel",), (2 * s_ + 3) * tr * c_ * 4),
    )(recv)


def _adamw(recv, w, m, v, *, tr, name):
    s_, r_, c_ = recv.shape
    tr = _row_tile(r_, tr)
    assert w.shape == (r_, c_), (name, w.shape, recv.shape)
    c1 = 1.0 - ADAM_B1 ** ADAM_STEP
    c2 = 1.0 - ADAM_B2 ** ADAM_STEP

    def body(g_ref, w_ref, m_ref, v_ref, go_ref, d_ref, mo_ref, vo_ref):
        g = g_ref[0].astype(F32)
        for s in range(1, s_):
            g = g + g_ref[s].astype(F32)
        mn = ADAM_B1 * m_ref[...] + (1.0 - ADAM_B1) * g
        vn = ADAM_B2 * v_ref[...] + (1.0 - ADAM_B2) * (g * g)
        go_ref[...] = g
        mo_ref[...] = mn
        vo_ref[...] = vn
        d_ref[...] = -ADAM_LR * ((mn / c1) / (jnp.sqrt(vn / c2) + ADAM_EPS) + ADAM_WD * w_ref[...])

    tile = pl.BlockSpec((tr, c_), lambda i: (i, 0))
    return pl.pallas_call(
        body, name=name, grid=(r_ // tr,),
        in_specs=[pl.BlockSpec((s_, tr, c_), lambda i: (0, i, 0)), tile, tile, tile],
        out_specs=[tile] * 4, out_shape=[jax.ShapeDtypeStruct((r_, c_), F32)] * 4,
        compiler_params=_params(("parallel",), (2 * s_ + 16) * tr * c_ * 4),
    )(recv, w, m, v)


def _s5_discretise(a_re, a_im, log_dt, b_re, b_im):
    dt = jnp.exp(log_dt)[:, None]
    lr = jnp.minimum(a_re, -1e-4)
    li = a_im
    mag = jnp.exp(lr * dt)
    lbr = mag * jnp.cos(li * dt)
    lbi = mag * jnp.sin(li * dt)
    zr, zi = lbr - 1.0, lbi
    den = lr * lr + li * li
    fr = (zr * lr + zi * li) / den
    fi = (zi * lr - zr * li) / den
    bbr = fr[..., None] * b_re - fi[..., None] * b_im
    bbi = fr[..., None] * b_im + fi[..., None] * b_re
    return lbr, lbi, bbr, bbi


def _softplus_neg(lam):
    return jnp.maximum(-lam, 0.0) + jnp.log(1.0 + jnp.exp(-jnp.abs(lam)))


S5_Q = 8
RG_Q = 2


def _local_step(x, tgt, W, comm):
    T, D = x.shape
    C = D
    G, P, H = W["ssm_b_re"].shape
    S = G * H
    F = W["mlp_b_up"].shape[1]
    n_in = 2 * C + S + 2 * D
    heads, hd = W["rg_wa"].shape[0], W["rg_wa"].shape[1]
    u_off, ga_off, gb_off = 2 * C, 2 * C + S, 2 * C + S + D

    sp, sp_vjp = jax.vjp(_softplus_neg, W["rg_lambda"])
    (lbr, lbi, bbr, bbi), s5_vjp = jax.vjp(_s5_discretise, W["ssm_a_re"], W["ssm_a_im"], W["ssm_log_dt"],
                                           W["ssm_b_re"], W["ssm_b_im"])
    lam_re, lam_im = lbr.reshape(-1), lbi.reshape(-1)
    jr, kr = heads // RG_Q, RG_Q * hd
    w_ri = jnp.concatenate([_bd_pack(W["rg_wa"], RG_Q), _bd_pack(W["rg_wx"], RG_Q)], axis=2).astype(BF16)
    b_ri = jnp.concatenate([W["rg_ba"].reshape(jr, kr), W["rg_bx"].reshape(jr, kr)], axis=1).reshape(1, -1)
    wb_re = _bd_pack(jnp.swapaxes(bbr, 1, 2), S5_Q).astype(BF16)
    wb_im = _bd_pack(jnp.swapaxes(bbi, 1, 2), S5_Q).astype(BF16)
    wc_re = _bd_pack(jnp.swapaxes(W["ssm_c_re"], 1, 2), S5_Q).astype(BF16)
    wc_im_neg = _bd_pack(jnp.swapaxes(-W["ssm_c_im"], 1, 2), S5_Q).astype(BF16)
    d_row = W["ssm_d"].reshape(1, S)

    x_bf = x.astype(BF16)
    w_in = comm.weight("w_in", None)
    z = _mm(x_bf, w_in, M=T, N=n_in, K=D, tm=512, tn=n_in // 4, tk=D, after=comm.gather_token, name="fwd_in_proj")
    xc = _conv_fwd(z, W["conv_w"], W["conv_b"], T=T, C=C, name="fwd_conv")
    ri = _bd([(xc, 0, w_ri)], T=T, J=jr, kb=kr, nb=2 * kr, extras=[(b_ri, "vec", 0)],
             epilogue=lambda acc, b: (_sig(acc + b),), name="fwd_gates")
    h, p = _rg_scan_fwd(z, ri, xc, sp, T=T, C=C, gate_off=C, cw=kr, name="fwd_rg_scan")
    w_a_out = comm.weight("w_a_out", p)
    y_a = _mm(p, w_a_out, M=T, N=D, K=C, out_dtypes=(BF16,), tm=512, tn=D, tk=C, name="fwd_rg_out")

    powers = _power_slabs(lam_re, lam_im, S5_TILE // SUBLANES)
    h_re, h_im, y_s, yg = _s5_fwd(z, u_off, wb_re, wb_im, wc_re, wc_im_neg, d_row, powers, T=T, name="fwd_s5")
    w_glu_w, w_glu_v = comm.weight("glu_w", yg), comm.weight("glu_v", yg)
    glu_a = _mm(yg, w_glu_w, M=T, N=D, K=S, out_dtypes=(BF16,), tm=1024, tn=D, tk=S, name="fwd_glu_w")
    glu_b = _mm(yg, w_glu_v, M=T, N=D, K=S, out_dtypes=(BF16,), tm=1024, tn=D, tk=S, name="fwd_glu_v")

    cwm = 1024

    def mix_fn(ga, gb, ya, a, b):
        return (_sig(ga) * ya + _sig(gb) * (a * _sig(b)),)

    mix = _ew(mix_fn, [(z, "tile", ga_off // cwm), (z, "tile", gb_off // cwm), (y_a, "tile", 0), (glu_a, "tile", 0),
                       (glu_b, "tile", 0)], T=T, C=D, n_out=1, out_dtypes=(BF16,), cw=cwm, name="fwd_mix")
    w_out = comm.weight("w_out", mix)
    def out_ln1_fn(acc, xv, g, b):
        s = ALPHA * xv + acc
        xhat, _ = _ln_stats(s)
        y = xhat * g + b
        return s, y, y

    s1, x1, x1_bf = _mm(mix, w_out, M=T, N=D, K=D, tm=256, tn=D, tk=D,
                        extras=[(x, "mn"), (W["ln1_g"], "n"), (W["ln1_b"], "n")], epilogue=out_ln1_fn, n_out=3,
                        out_dtypes=(F32, F32, BF16), name="fwd_out_proj_ln1")
    w_up = comm.weight("mlp_w_up", x1_bf)

    def mlp_up_fn(acc, b):
        hp = acc + b
        rl = jnp.maximum(hp, 0.0)
        return rl * rl, hp

    hact, hpre = _mm(x1_bf, w_up, M=T, N=F, K=D, tm=1024, tn=1024, tk=D, extras=[(W["mlp_b_up"], "n")],
                     epilogue=mlp_up_fn, n_out=2, out_dtypes=(BF16, BF16), name="fwd_mlp_up")
    w_down = comm.weight("mlp_w_down", hact)
    s2 = _mm(hact, w_down, M=T, N=D, K=F, tm=512, tn=1024, tk=4096,
             extras=[(x1, "mn"), (W["mlp_b_down"], "n")], epilogue=lambda acc, xv, b: (ALPHA * xv + acc + b,),
             name="fwd_mlp_down")

    def ln2_fn(s, t, g, b):
        xhat, rstd = _ln_stats(s)
        err = xhat * g + b - t
        dy = err * (1.0 / D)
        ds = _ln_bwd(dy, g, xhat, rstd)
        return ds, ds, 0.5 * dy * err, dy * xhat, dy, ds

    ds2, ds2_bf, loss_cols, d_ln2_g, d_ln2_b, d_b_down = _ew(
        ln2_fn, [(s2, "tile", 0), (tgt, "tile", 0), (W["ln2_g"], "vec", 0), (W["ln2_b"], "vec", 0)],
        T=T, C=D, n_out=2, n_cs=4, out_dtypes=(F32, BF16), tm=256, name="bwd_loss_ln2")
    d_w_down = _mm(hact, ds2_bf, M=F, N=D, K=T, ta=True, out_dtypes=(BF16,), tm=1024, tn=1024, tk=4096, name="bwd_w_down")
    sent = comm.send_grad("mlp_w_down", d_w_down)

    def dhpre_fn(acc, hp):
        dv = acc * (2.0 * jnp.maximum(hp.astype(F32), 0.0))
        return dv, dv

    dhpre, d_b_up = _mm(ds2_bf, w_down, M=T, N=F, K=D, tb=True, tm=1024, tn=1024, tk=D, extras=[(hpre, "mn")],
                        epilogue=dhpre_fn, n_cs=1, out_dtypes=(BF16,), after=sent, name="bwd_mlp_down")
    d_w_up = _mm(x1_bf, dhpre, M=D, N=F, K=T, ta=True, out_dtypes=(BF16,), n_split=N_DEV, tm=1024, tn=F // N_DEV, tk=4096, name="bwd_w_up")
    sent = comm.send_grad("mlp_w_up", d_w_up)
    dx1 = _mm(dhpre, w_up, M=T, N=D, K=F, tb=True, tm=512, tn=1024, tk=4096,
              extras=[(ds2, "mn")], epilogue=lambda acc, dv: (ALPHA * dv + acc,), after=sent, name="bwd_mlp_up")

    def ln1_bwd_fn(s, dy, g):
        xhat, rstd = _ln_stats(s)
        ds = _ln_bwd(dy, g, xhat, rstd)
        return ds, ds, dy * xhat, dy

    ds1, ds1_bf, d_ln1_g, d_ln1_b = _ew(ln1_bwd_fn, [(s1, "tile", 0), (dx1, "tile", 0), (W["ln1_g"], "vec", 0)],
                                        T=T, C=D, n_out=2, n_cs=2, out_dtypes=(F32, BF16), tm=256, name="bwd_ln1")
    d_w_out = _mm(mix, ds1_bf, M=D, N=D, K=T, ta=True, out_dtypes=(BF16,), tm=1024, tn=1024, tk=4096, name="bwd_w_out")
    sent = comm.send_grad("w_out", d_w_out)
    def mix_bwd_fn(dm, ga, gb, ya, a, b):
        ya, a, b = ya.astype(F32), a.astype(F32), b.astype(F32)
        sa, sb, sv = _sig(ga), _sig(gb), _sig(b)
        yb = a * sv
        dyb = dm * sb
        return (dm * ya * (sa * (1.0 - sa)), dm * yb * (sb * (1.0 - sb)), dm * sa, dyb * sv,
                dyb * a * (sv * (1.0 - sv)))

    dg_a, dg_b, dy_a, dglu_a, dglu_b = _mm(
        ds1_bf, w_out, M=T, N=D, K=D, tb=True, tm=512, tn=cwm, tk=D,
        extras=[(z, "mn", ga_off // cwm), (z, "mn", gb_off // cwm), (y_a, "mn"), (glu_a, "mn"), (glu_b, "mn")],
        epilogue=mix_bwd_fn, n_out=5, out_dtypes=(BF16,) * 5, after=sent, name="bwd_out_proj_mix")

    d_w_a_out = _mm(p, dy_a, M=C, N=D, K=T, ta=True, out_dtypes=(BF16,), tm=1024, tn=1024, tk=4096, name="bwd_w_a_out")
    sent = comm.send_grad("w_a_out", d_w_a_out)
    def dp_fn(dp, hv, gate):
        th = jnp.tanh(GELU_C * (gate + GELU_K * gate * gate * gate))
        gelu = 0.5 * gate * (1.0 + th)
        dgelu = 0.5 * (1.0 + th) + 0.5 * gate * (1.0 - th * th) * (GELU_C * (1.0 + 3.0 * GELU_K * gate * gate))
        return dp * gelu, dp * hv * dgelu

    dh, dgate = _mm(dy_a, w_a_out, M=T, N=C, K=D, tb=True, tm=256, tn=C, tk=D, extras=[(h, "mn"), (z, "mn", 1)],
                    epilogue=dp_fn, n_out=2, out_dtypes=(F32, BF16), after=sent, name="bwd_rg_out")
    drai, dxc0, d_b_ri, d_sp = _rg_scan_bwd(dh, h, ri, xc, sp, T=T, C=C, cw=kr, name="bwd_rg_scan")
    dxc = _bd([(drai, 0, w_ri)], T=T, J=jr, kb=2 * kr, nb=kr, tw=True, extras=[(dxc0, "tile", 0)],
              epilogue=lambda acc, d0: (acc + d0,), name="bwd_gates")
    d_w_ri = _bdw(xc, 0, drai, 0, T=T, J=jr, kb=kr, nb=2 * kr, name="bwd_w_gates")
    d_wa, d_wx = _bd_unpack(d_w_ri[:, :, :kr], RG_Q), _bd_unpack(d_w_ri[:, :, kr:], RG_Q)
    d_b_ri = d_b_ri.reshape(jr, 2 * kr)
    d_ba, d_bx = d_b_ri[:, :kr].reshape(1, -1), d_b_ri[:, kr:].reshape(1, -1)
    dxr, conv_sums = _conv_bwd(dxc, z, W["conv_w"], T=T, C=C, name="bwd_conv")
    d_conv_w, d_conv_b = conv_sums[0:4], conv_sums[4:5]
    (d_lambda,) = sp_vjp(d_sp)

    d_glu_w = _mm(yg, dglu_a, M=S, N=D, K=T, ta=True, out_dtypes=(BF16,), n_split=N_DEV, tm=1024, tn=D // N_DEV, tk=4096, name="bwd_w_glu_w")
    d_glu_v = _mm(yg, dglu_b, M=S, N=D, K=T, ta=True, out_dtypes=(BF16,), n_split=N_DEV, tm=1024, tn=D // N_DEV, tk=4096, name="bwd_w_glu_v")
    sent = comm.send_grad("glu_w", d_glu_w, "glu_v", d_glu_v)
    dyg0 = _mm(dglu_a, w_glu_w, M=T, N=S, K=D, tb=True, tm=512, tn=S, tk=D, after=sent, name="bwd_glu_w")
    dy_s = _mm(dglu_b, w_glu_v, M=T, N=S, K=D, tb=True, tm=512, tn=S, tk=D,
               extras=[(dyg0, "mn"), (y_s, "mn")], epilogue=lambda acc, d0, yv: ((acc + d0) * _dgelu(yv),),
               name="bwd_glu_v")
    du, d_lbr, d_lbi, d_ssm_d, d_wb_re, d_wb_im, d_wc_re, d_wc_im_neg = _s5_bwd(
        dy_s, z, u_off, h_re, h_im, wb_re, wb_im, wc_re, wc_im_neg, d_row, powers, T=T, name="bwd_s5")
    d_bbr = jnp.swapaxes(_bd_unpack(d_wb_re, S5_Q), 1, 2)
    d_bbi = jnp.swapaxes(_bd_unpack(d_wb_im, S5_Q), 1, 2)
    d_a_re, d_a_im, d_log_dt, d_b_re, d_b_im = s5_vjp((d_lbr.reshape(G, P), d_lbi.reshape(G, P), d_bbr, d_bbi))
    d_c_re = jnp.swapaxes(_bd_unpack(d_wc_re, S5_Q), 1, 2)
    d_c_im = -jnp.swapaxes(_bd_unpack(d_wc_im_neg, S5_Q), 1, 2)

    dz = jnp.concatenate([dxr.astype(BF16), dgate.astype(BF16), du, dg_a, dg_b], axis=1)
    d_w_in = _mm(x_bf, dz, M=D, N=n_in, K=T, ta=True, out_dtypes=(BF16,), n_split=N_DEV, tm=1024, tn=n_in // N_DEV, tk=4096, name="bwd_w_in")
    sent = comm.send_grad("w_in", d_w_in)
    grad_x = _mm(dz, w_in, M=T, N=D, K=n_in, tb=True, tm=512, tn=1024, tk=n_in // 2,
                 extras=[(ds1, "mn")], epilogue=lambda acc, dv: (ALPHA * dv + acc,), after=sent, name="bwd_in_proj")

    grads = dict(
        conv_w=d_conv_w, conv_b=d_conv_b, rg_wa=d_wa, rg_ba=d_ba, rg_wx=d_wx, rg_bx=d_bx,
        rg_lambda=d_lambda, ssm_a_re=d_a_re, ssm_a_im=d_a_im, ssm_log_dt=d_log_dt,
        ssm_b_re=d_b_re, ssm_b_im=d_b_im, ssm_c_re=d_c_re, ssm_c_im=d_c_im, ssm_d=d_ssm_d.reshape(G, H),
        ln1_g=d_ln1_g, ln1_b=d_ln1_b, mlp_b_up=d_b_up, mlp_b_down=d_b_down, ln2_g=d_ln2_g, ln2_b=d_ln2_b)
    return jnp.sum(loss_cols), grad_x, grads


BIG = ("w_in", "w_a_out", "glu_w", "glu_v", "w_out", "mlp_w_up", "mlp_w_down")
COL_SHARDED = ("w_in", "glu_w", "glu_v", "mlp_w_up")
SMALL = ("conv_w", "conv_b", "rg_wa", "rg_ba", "rg_wx", "rg_bx", "rg_lambda", "ssm_a_re", "ssm_a_im", "ssm_log_dt",
         "ssm_b_re", "ssm_b_im", "ssm_c_re", "ssm_c_im", "ssm_d", "ln1_g", "ln1_b", "mlp_b_up", "mlp_b_down", "ln2_g",
         "ln2_b")
ORDER = ("w_in", "conv_w", "conv_b", "rg_wa", "rg_ba", "rg_wx", "rg_bx", "rg_lambda", "w_a_out", "ssm_a_re",
         "ssm_a_im", "ssm_log_dt", "ssm_b_re", "ssm_b_im", "ssm_c_re", "ssm_c_im", "ssm_d", "glu_w", "glu_v", "w_out",
         "ln1_g", "ln1_b", "mlp_w_up", "mlp_b_up", "mlp_w_down", "mlp_b_down", "ln2_g", "ln2_b")
TILE_ELEMS = SUBLANES * LANES


def _pack(arrs):
    pieces = []
    for a in arrs:
        flat = a.reshape(-1)
        flat = jnp.pad(flat, (0, (-flat.shape[0]) % TILE_ELEMS))
        pieces.append(flat.reshape(-1, LANES))
    rows = sum(p.shape[0] for p in pieces)
    pad_rows = (-rows) % (N_DEV * SUBLANES)
    if pad_rows:
        pieces.append(jnp.zeros((pad_rows, LANES), pieces[0].dtype))
    return jnp.concatenate(pieces, axis=0)


def _unpack(packed, shapes):
    out, row = [], 0
    for shp in shapes:
        n = math.prod(shp)
        rows = -(-n // TILE_ELEMS) * SUBLANES
        out.append(packed[row:row + rows].reshape(-1)[:n].reshape(shp))
        row += rows
    return out


class _Comm:
    def __init__(self, w):
        first = _all_gather([w["w_in"].astype(BF16), w["conv_w"]], name="gather_w_in")
        self._weights = {"w_in": first[0]}
        self.conv_w = jnp.swapaxes(first[1], 0, 1).reshape(w["conv_w"].shape[0], -1)
        later = [k for k in BIG if k != "w_in"]
        shards = [w[k].astype(BF16) for k in later]
        handles, self.gather_token = _exchange_start(shards, [_landing_zone(s) for s in shards], scatter=False,
                                                     name="gather_weights_start")
        self._gathers = dict(zip(later, handles))
        self._grads = {}

    def weight(self, k, after):
        if k not in self._weights:
            self._weights[k] = _exchange_wait(self._gathers.pop(k), after, scatter=False, name="gather_wait_" + k)
        gk = self._weights[k]
        if k in COL_SHARDED:
            return jnp.swapaxes(gk, 0, 1).reshape(gk.shape[1], -1)
        return gk.reshape(-1, gk.shape[-1])

    def send_grad(self, *names_and_parts):
        names, parts = names_and_parts[0::2], names_and_parts[1::2]
        parts = [p if k in COL_SHARDED else p.reshape(N_DEV, p.shape[0] // N_DEV, p.shape[1])
                 for k, p in zip(names, parts)]
        me = _dev_index(*_mesh_pos())
        lands = [_landing_zone(lax.dynamic_index_in_dim(p, me, 0, keepdims=False)) for p in parts]
        handles, token = _exchange_start(parts, lands, scatter=True, name="grad_start_" + names[0])
        self._grads.update(zip(names, handles))
        return token

    def received_grad(self, k, after):
        return _exchange_wait(self._grads.pop(k), after, scatter=True, name="grad_wait_" + k)


def _step(x, tgt, w, m, v):
    dev = _dev_index(*_mesh_pos())

    comm = _Comm(w)
    small = dict(w)
    small["conv_w"] = comm.conv_w
    for k in ("conv_b", "rg_ba", "rg_bx", "rg_lambda", "ln1_g", "ln1_b", "mlp_b_up", "mlp_b_down", "ln2_g", "ln2_b"):
        small[k] = w[k].reshape(1, -1)

    loss_part, grad_x, grads = _local_step(x, tgt, small, comm)

    out_g, out_d, out_m, out_v = {}, {}, {}, {}
    for k in BIG:
        rk = comm.received_grad(k, grad_x)
        out_g[k], out_d[k], out_m[k], out_v[k] = _adamw(rk, w[k], m[k], v[k], tr=128, name="adamw_" + k)

    small_shapes = [grads[k].shape for k in SMALL]
    (small_recv,) = _exchange_blocks([_pack([grads[k] for k in SMALL]).reshape(N_DEV, -1, LANES)],
                                     name="exchange_small_grads")
    small_block = _sum_slots(small_recv, tr=512, name="sum_small_grads")
    (small_all,) = _all_gather([small_block], name="gather_small_grads")
    g_small = dict(zip(SMALL, _unpack(small_all.reshape(-1, LANES), small_shapes)))
    cw_cols = w["conv_w"].shape[1]
    g_small["conv_w"] = lax.dynamic_slice_in_dim(g_small["conv_w"], dev * cw_cols, cw_cols, axis=1)
    shapes = [w[k].shape for k in SMALL]
    g_pack, w_pack, m_pack, v_pack = [_pack([src[k] for k in SMALL]) for src in (g_small, w, m, v)]
    res = _adamw(g_pack[None], w_pack, m_pack, v_pack, tr=1024, name="adamw_small")
    for dst, packed in zip((out_g, out_d, out_m, out_v), res):
        dst.update(zip(SMALL, _unpack(packed, shapes)))

    loss = lax.psum(loss_part, ("x", "y", "c"))
    return loss, grad_x, out_g, out_d, out_m, out_v


def kernel(x, w_in, conv_w, conv_b, rg_wa, rg_ba, rg_wx, rg_bx, rg_lambda, w_a_out, ssm_a_re, ssm_a_im, ssm_log_dt, ssm_b_re, ssm_b_im, ssm_c_re, ssm_c_im, ssm_d, glu_w, glu_v, w_out, ln1_g, ln1_b, mlp_w_up, mlp_b_up, mlp_w_down, mlp_b_down, ln2_g, ln2_b, loss_target, m_w_in, m_conv_w, m_conv_b, m_rg_wa, m_rg_ba, m_rg_wx, m_rg_bx, m_rg_lambda, m_w_a_out, m_ssm_a_re, m_ssm_a_im, m_ssm_log_dt, m_ssm_b_re, m_ssm_b_im, m_ssm_c_re, m_ssm_c_im, m_ssm_d, m_glu_w, m_glu_v, m_w_out, m_ln1_g, m_ln1_b, m_mlp_w_up, m_mlp_b_up, m_mlp_w_down, m_mlp_b_down, m_ln2_g, m_ln2_b, v_w_in, v_conv_w, v_conv_b, v_rg_wa, v_rg_ba, v_rg_wx, v_rg_bx, v_rg_lambda, v_w_a_out, v_ssm_a_re, v_ssm_a_im, v_ssm_log_dt, v_ssm_b_re, v_ssm_b_im, v_ssm_c_re, v_ssm_c_im, v_ssm_d, v_glu_w, v_glu_v, v_w_out, v_ln1_g, v_ln1_b, v_mlp_w_up, v_mlp_b_up, v_mlp_w_down, v_mlp_b_down, v_ln2_g, v_ln2_b):
    args = locals()
    w = {k: args[k][0] for k in ORDER}
    m = {k: args["m_" + k][0] for k in ORDER}
    v = {k: args["v_" + k][0] for k in ORDER}
    loss, grad_x, out_g, out_d, out_m, out_v = _step(x[0], loss_target[0], w, m, v)
    outs = [loss, grad_x[None]]
    for group in (out_g, out_d, out_m, out_v):
        outs += [group[k].reshape(args[k].shape) for k in ORDER]
    return tuple(outs)
```

```python
import functools
import math

import jax
import jax.numpy as jnp
from jax import lax
from jax.experimental import pallas as pl
from jax.experimental.pallas import tpu as pltpu

F32 = jnp.float32
BF16 = jnp.bfloat16
MESH = pl.DeviceIdType.MESH
N_DEV = 8
SUBLANES = 8
LANES = 128
VMEM_BYTES_V7X = 64 * 2 ** 20
VMEM_CAP = VMEM_BYTES_V7X - 8 * 2 ** 20

ALPHA = 2.0 ** 0.25
LN_EPS = 1e-5
RG_C = 8.0
ADAM_LR, ADAM_B1, ADAM_B2, ADAM_EPS, ADAM_WD, ADAM_STEP = 0.001, 0.9, 0.999, 1e-08, 0.01, 10
GELU_C = math.sqrt(2.0 / math.pi)
GELU_K = 0.044715

ANY = pl.BlockSpec(memory_space=pl.ANY)


def _params(sem, vmem_bytes):
    limit = int(min(max(2 * vmem_bytes, 16 * 2 ** 20), VMEM_CAP))
    return pltpu.CompilerParams(dimension_semantics=sem, vmem_limit_bytes=limit)


def _sig(x):
    return 1.0 / (1.0 + jnp.exp(-x))


def _gelu(x):
    return 0.5 * x * (1.0 + jnp.tanh(GELU_C * (x + GELU_K * x * x * x)))


def _dgelu(x):
    th = jnp.tanh(GELU_C * (x + GELU_K * x * x * x))
    return 0.5 * (1.0 + th) + 0.5 * x * (1.0 - th * th) * (GELU_C * (1.0 + 3.0 * GELU_K * x * x))


def _one_minus_exp(x, exp_half_x):
    p = x * (1.0 + x * (1 / 2 + x * (1 / 6 + x * (1 / 24 + x * (1 / 120)))))
    return jnp.where(x > -1 / 16, -p, 1.0 - exp_half_x * exp_half_x)


def _accumulate(ref, val, first):
    @pl.when(first)
    def _():
        ref[...] = val

    @pl.when(jnp.logical_not(first))
    def _():
        ref[...] += val


def _rows8(cw):
    return lax.broadcasted_iota(jnp.int32, (SUBLANES, cw), 0)


def _shift_down(cur, prev, s, rows):
    return jnp.where(rows < s, pltpu.roll(prev, s, 0), pltpu.roll(cur, s, 0))


def _shift_up(cur, nxt, s, rows):
    return jnp.where(rows < SUBLANES - s, pltpu.roll(cur, SUBLANES - s, 0), pltpu.roll(nxt, SUBLANES - s, 0))


def _mm(a, b, *, M, N, K, ta=False, tb=False, b_split=1, n_split=1, a_fn=None, extras=(), epilogue=None,
        n_out=1, n_cs=0, out_dtypes=None, tm=512, tn=512, tk=512, after=None, name):
    tm, tn, tk = min(tm, M), min(tn, N), min(tk, K)
    assert M % tm == 0 and N % tn == 0 and K % tk == 0, (name, M, N, K, tm, tn, tk)
    nk = K // tk
    grid = (N // tn, M // tm, nk)
    a_spec = pl.BlockSpec((tk, tm), lambda j, i, k: (k, i)) if ta else pl.BlockSpec((tm, tk), lambda j, i, k: (i, k))
    if b_split == 1:
        b_spec = pl.BlockSpec((tn, tk), lambda j, i, k: (j, k)) if tb else pl.BlockSpec((tk, tn), lambda j, i, k: (k, j))
    elif tb:
        kb = (K // b_split) // tk
        assert kb * tk * b_split == K, name
        b_spec = pl.BlockSpec((None, tn, tk), lambda j, i, k: (k // kb, j, k % kb))
    else:
        nb = (N // b_split) // tn
        assert nb * tn * b_split == N, name
        b_spec = pl.BlockSpec((None, tk, tn), lambda j, i, k: (j // nb, k, j % nb))
    in_specs = [a_spec, b_spec]
    for arr, kind, *col_off in extras:
        off = col_off[0] if col_off else 0
        in_specs.append(pl.BlockSpec((tm, tn), lambda j, i, k, off=off: (i, off + j)) if kind == "mn"
                        else pl.BlockSpec((1, tn), lambda j, i, k: (0, j)))
    out_dtypes = (F32,) * n_out if out_dtypes is None else out_dtypes
    if n_split == 1:
        out_shape = [jax.ShapeDtypeStruct((M, N), dt) for dt in out_dtypes]
        out_specs = [pl.BlockSpec((tm, tn), lambda j, i, k: (i, j)) for _ in range(n_out)]
    else:
        assert n_out == 1
        nbo = (N // n_split) // tn
        assert nbo * tn * n_split == N, name
        out_shape = [jax.ShapeDtypeStruct((n_split, M, N // n_split), out_dtypes[0])]
        out_specs = [pl.BlockSpec((None, tm, tn), lambda j, i, k: (j // nbo, i, j % nbo))]
    out_shape += [jax.ShapeDtypeStruct((1, N), F32) for _ in range(n_cs)]
    out_specs += [pl.BlockSpec((1, tn), lambda j, i, k: (0, j)) for _ in range(n_cs)]
    ne = len(extras)
    dims = (((0 if ta else 1,), (1 if tb else 0,)), ((), ()))

    n_after = 0 if after is None else 1
    in_specs += [ANY] * n_after

    def body(*refs):
        a_ref, b_ref = refs[0], refs[1]
        ex_refs = refs[2:2 + ne]
        first_out = 2 + ne + n_after
        out_refs = refs[first_out:first_out + n_out]
        cs_refs = refs[first_out + n_out:first_out + n_out + n_cs]
        i, k = pl.program_id(1), pl.program_id(2)

        def product():
            av = a_ref[...]
            if a_fn is not None:
                av = a_fn(av.astype(F32))
            return lax.dot_general(av.astype(BF16), b_ref[...].astype(BF16), dims, preferred_element_type=F32)

        def finish(acc):
            res = (acc,) if epilogue is None else epilogue(acc, *[r[...] for r in ex_refs])
            for r, o in zip(out_refs, res[:n_out]):
                r[...] = o.astype(r.dtype)
            for r, cval in zip(cs_refs, res[n_out:]):
                _accumulate(r, jnp.sum(cval, axis=0, keepdims=True), i == 0)

        if nk == 1:
            finish(product())
            return
        acc_ref = refs[-1]

        @pl.when(k == 0)
        def _():
            acc_ref[...] = jnp.zeros_like(acc_ref)

        acc_ref[...] += product()

        @pl.when(k == nk - 1)
        def _():
            finish(acc_ref[...])

    vmem = 2 * tm * tk * a.dtype.itemsize + 2 * tk * tn * b.dtype.itemsize + (1 + 2 * n_out + 2 * ne + 2) * tm * tn * 4
    outs = pl.pallas_call(
        body, name=name, grid=grid, in_specs=in_specs, out_specs=out_specs, out_shape=out_shape,
        scratch_shapes=[pltpu.VMEM((tm, tn), F32)] if nk > 1 else [],
        compiler_params=_params(("parallel", "arbitrary", "arbitrary"), vmem),
    )(a, b, *[e[0] for e in extras], *([after] if n_after else []))
    return outs[0] if len(outs) == 1 else outs


BD_STEP = 4

def _bd(pairs, *, T, J, kb, nb, tw=False, extras=(), epilogue=None, n_out=1, n_cs=0, out_dtypes=None, tm=512, name):
    jb = BD_STEP
    assert T % tm == 0 and J % jb == 0
    grid = (J // jb, T // tm)
    npair, ne = len(pairs), len(extras)
    in_specs, args = [], []
    for arr, off, w in pairs:
        assert off % jb == 0, name
        in_specs.append(pl.BlockSpec((tm, jb * kb), lambda j, i, off=off // jb: (i, off + j)))
        in_specs.append(pl.BlockSpec((jb,) + tuple(w.shape[1:]), lambda j, i: (j, 0, 0)))
        args += [arr, w]
    for arr, kind, off in extras:
        assert off % jb == 0, name
        in_specs.append(pl.BlockSpec((tm, jb * nb), lambda j, i, off=off // jb: (i, off + j)) if kind == "tile"
                        else pl.BlockSpec((1, jb * nb), lambda j, i, off=off // jb: (0, off + j)))
        args.append(arr)
    out_dtypes = (F32,) * n_out if out_dtypes is None else out_dtypes
    out_shape = [jax.ShapeDtypeStruct((T, J * nb), dt) for dt in out_dtypes]
    out_specs = [pl.BlockSpec((tm, jb * nb), lambda j, i: (i, j)) for _ in range(n_out)]
    out_shape += [jax.ShapeDtypeStruct((1, J * nb), F32) for _ in range(n_cs)]
    out_specs += [pl.BlockSpec((1, jb * nb), lambda j, i: (0, j)) for _ in range(n_cs)]
    dims = (((1,), (1 if tw else 0,)), ((), ()))

    def body(*refs):
        ex_refs = refs[2 * npair:2 * npair + ne]
        out_refs = refs[2 * npair + ne:2 * npair + ne + n_out]
        cs_refs = refs[2 * npair + ne + n_out:]
        i = pl.program_id(1)
        for s in range(jb):
            cols_in, cols_out = pl.ds(s * kb, kb), pl.ds(s * nb, nb)
            acc = None
            for p in range(npair):
                d = lax.dot_general(refs[2 * p][:, cols_in].astype(BF16), refs[2 * p + 1][s].astype(BF16), dims,
                                    preferred_element_type=F32)
                acc = d if acc is None else acc + d
            res = (acc,) if epilogue is None else epilogue(acc, *[r[:, cols_out] for r in ex_refs])
            for r, o in zip(out_refs, res[:n_out]):
                r[:, cols_out] = o.astype(r.dtype)
            for r, cval in zip(cs_refs, res[n_out:]):
                _accumulate(r.at[:, cols_out], jnp.sum(cval, axis=0, keepdims=True), i == 0)

    vmem = jb * (2 * npair * tm * kb + 2 * npair * kb * nb + (2 * n_out + 2 * ne + 3) * tm * nb) * 4
    outs = pl.pallas_call(
        body, name=name, grid=grid, in_specs=in_specs, out_specs=out_specs, out_shape=out_shape,
        compiler_params=_params(("parallel", "arbitrary"), vmem),
    )(*args)
    return outs[0] if len(outs) == 1 else outs


def _bdw(a, a_off, b, b_off, *, T, J, kb, nb, tm=512, name):
    jb = BD_STEP
    assert T % tm == 0 and J % jb == 0 and a_off % jb == 0 and b_off % jb == 0
    a_blk, b_blk = a_off // jb, b_off // jb

    def body(a_ref, b_ref, o_ref):
        i = pl.program_id(1)
        for s in range(jb):
            d = lax.dot_general(a_ref[:, pl.ds(s * kb, kb)].astype(BF16), b_ref[:, pl.ds(s * nb, nb)].astype(BF16),
                                (((0,), (0,)), ((), ())), preferred_element_type=F32)
            _accumulate(o_ref.at[s], d, i == 0)

    return pl.pallas_call(
        body, name=name, grid=(J // jb, T // tm),
        in_specs=[pl.BlockSpec((tm, jb * kb), lambda j, i: (i, a_blk + j)),
                  pl.BlockSpec((tm, jb * nb), lambda j, i: (i, b_blk + j))],
        out_specs=pl.BlockSpec((jb, kb, nb), lambda j, i: (j, 0, 0)),
        out_shape=jax.ShapeDtypeStruct((J, kb, nb), F32),
        compiler_params=_params(("parallel", "arbitrary"), jb * (2 * tm * (kb + nb) + 3 * kb * nb) * 4),
    )(a, b)


def _bd_pack(w, q):
    g, a, b = w.shape
    eye = jnp.eye(q, dtype=w.dtype)
    return jnp.einsum("jqab,qr->jqarb", w.reshape(g // q, q, a, b), eye).reshape(g // q, q * a, q * b)


def _bd_unpack(wp, q):
    j, qa, qb = wp.shape
    a, b = qa // q, qb // q
    w5 = wp.reshape(j, q, a, q, b)
    return jnp.stack([w5[:, r, :, r, :] for r in range(q)], axis=1).reshape(j * q, a, b)


def _ew(fn, ins, *, T, C, n_out, n_cs=0, out_dtypes=None, tm=256, cw=None, name):
    cw = C if cw is None else cw
    assert T % tm == 0 and C % cw == 0
    grid = (C // cw, T // tm)
    in_specs = []
    for arr, kind, off in ins:
        in_specs.append(pl.BlockSpec((tm, cw), lambda j, i, off=off: (i, off + j)) if kind == "tile"
                        else pl.BlockSpec((arr.shape[0], cw), lambda j, i, off=off: (0, off + j)))
    out_dtypes = (F32,) * n_out if out_dtypes is None else out_dtypes
    out_shape = [jax.ShapeDtypeStruct((T, C), dt) for dt in out_dtypes]
    out_specs = [pl.BlockSpec((tm, cw), lambda j, i: (i, j)) for _ in range(n_out)]
    out_shape += [jax.ShapeDtypeStruct((1, C), F32) for _ in range(n_cs)]
    out_specs += [pl.BlockSpec((1, cw), lambda j, i: (0, j)) for _ in range(n_cs)]
    nin = len(ins)

    def body(*refs):
        i = pl.program_id(1)
        res = fn(*[r[...].astype(F32) for r in refs[:nin]])
        for r, o in zip(refs[nin:nin + n_out], res[:n_out]):
            r[...] = o.astype(r.dtype)
        for r, cval in zip(refs[nin + n_out:], res[n_out:]):
            _accumulate(r, jnp.sum(cval, axis=0, keepdims=True), i == 0)

    vmem = (2 * nin + 2 * n_out + 6) * tm * cw * 4
    outs = pl.pallas_call(
        body, name=name, grid=grid, in_specs=in_specs, out_specs=out_specs, out_shape=out_shape,
        compiler_params=_params(("parallel", "arbitrary"), vmem),
    )(*[arr for arr, _, _ in ins])
    return outs[0] if len(outs) == 1 else outs


def _ln_stats(s):
    mu = jnp.mean(s, axis=-1, keepdims=True)
    d = s - mu
    var = jnp.mean(d * d, axis=-1, keepdims=True)
    rstd = lax.rsqrt(var + LN_EPS)
    return d * rstd, rstd


def _ln_bwd(dy, g, xhat, rstd):
    dxh = dy * g
    m1 = jnp.mean(dxh, axis=-1, keepdims=True)
    m2 = jnp.mean(dxh * xhat, axis=-1, keepdims=True)
    return rstd * (dxh - m1 - xhat * m2)


def _conv_fwd(z, conv_w, conv_b, *, T, C, tm=512, cw=1024, name):
    ng, hb = tm // SUBLANES, tm // SUBLANES

    def body(x_ref, halo_ref, w_ref, b_ref, o_ref):
        it = pl.program_id(1)
        rows = _rows8(cw)
        halo = jnp.where(it == 0, 0.0, halo_ref[...])
        w = w_ref[...]
        bias = b_ref[...]

        def group(g, carry):
            off = pl.multiple_of(g * SUBLANES, SUBLANES)
            cur = x_ref[pl.ds(off, SUBLANES), :]
            prev = x_ref[pl.ds(pl.multiple_of(jnp.maximum(off - SUBLANES, 0), SUBLANES), SUBLANES), :]
            prev = jnp.where(g == 0, halo, prev)
            acc = cur * w[3:4] + bias
            for s in (1, 2, 3):
                acc = acc + _shift_down(cur, prev, s, rows) * w[3 - s:4 - s]
            o_ref[pl.ds(off, SUBLANES), :] = acc
            return carry

        lax.fori_loop(0, ng, group, 0, unroll=2)

    return pl.pallas_call(
        body, name=name, grid=(C // cw, T // tm),
        in_specs=[pl.BlockSpec((tm, cw), lambda j, i: (i, j)),
                  pl.BlockSpec((SUBLANES, cw), lambda j, i: (jnp.maximum(i * hb - 1, 0), j)),
                  pl.BlockSpec((4, cw), lambda j, i: (0, j)), pl.BlockSpec((1, cw), lambda j, i: (0, j))],
        out_specs=pl.BlockSpec((tm, cw), lambda j, i: (i, j)),
        out_shape=jax.ShapeDtypeStruct((T, C), F32),
        compiler_params=_params(("parallel", "arbitrary"), 5 * tm * cw * 4),
    )(z, z, conv_w, conv_b)


def _conv_bwd(dxc, z, conv_w, *, T, C, tm=512, cw=512, name):
    ng, hb, last = tm // SUBLANES, tm // SUBLANES, T // SUBLANES - 1
    nt = T // tm

    def body(d_ref, dn_ref, x_ref, w_ref, o_ref, sums_ref):
        it = pl.program_id(1)
        rows = _rows8(cw)
        dnext = jnp.where(it == nt - 1, 0.0, dn_ref[...])
        w = w_ref[...]

        def group(g, accs):
            off = pl.multiple_of(g * SUBLANES, SUBLANES)
            dcur = d_ref[pl.ds(off, SUBLANES), :]
            dnx = d_ref[pl.ds(pl.multiple_of(jnp.minimum(off + SUBLANES, tm - SUBLANES), SUBLANES), SUBLANES), :]
            dnx = jnp.where(g == ng - 1, dnext, dnx)
            xcur = x_ref[pl.ds(off, SUBLANES), :]
            acc = dcur * w[3:4]
            taps = [accs[3] + dcur * xcur]
            for s in (1, 2, 3):
                ahead = _shift_up(dcur, dnx, s, rows)
                acc = acc + ahead * w[3 - s:4 - s]
                taps.append(accs[3 - s] + ahead * xcur)
            o_ref[pl.ds(off, SUBLANES), :] = acc
            return taps[3], taps[2], taps[1], taps[0], accs[4] + dcur

        zero = jnp.zeros((SUBLANES, cw), F32)
        accs = lax.fori_loop(0, ng, group, (zero,) * 5, unroll=2)
        sums = jnp.zeros((SUBLANES, cw), F32)
        for k, a in enumerate(accs):
            sums = jnp.where(rows == k, jnp.sum(a, axis=0, keepdims=True), sums)
        _accumulate(sums_ref, sums, it == 0)

    tile = pl.BlockSpec((tm, cw), lambda j, i: (i, j))
    return pl.pallas_call(
        body, name=name, grid=(C // cw, nt),
        in_specs=[tile, pl.BlockSpec((SUBLANES, cw), lambda j, i: (jnp.minimum((i + 1) * hb, last), j)),
                  tile, pl.BlockSpec((4, cw), lambda j, i: (0, j))],
        out_specs=[tile, pl.BlockSpec((SUBLANES, cw), lambda j, i: (0, j))],
        out_shape=[jax.ShapeDtypeStruct((T, C), F32), jax.ShapeDtypeStruct((SUBLANES, C), F32)],
        compiler_params=_params(("parallel", "arbitrary"), 7 * tm * cw * 4),
    )(dxc, dxc, z, conv_w)


def _rg_coeffs(r, ig, xc, sp):
    la = (-RG_C) * r * sp
    a = jnp.exp(la)
    m = jnp.sqrt(_one_minus_exp(2.0 * la, a))
    return a, m, m * (ig * xc)


def _rg_scan_fwd(z, ri, xc, sp, *, T, C, gate_off, tm=512, cw=256, name):
    rows16 = 2 * SUBLANES
    nq = tm // rows16

    def body(gate_ref, r_ref, i_ref, xc_ref, sp_ref, h_ref, p_ref, a_ref, m_ref, carry_ref):
        it = pl.program_id(1)

        @pl.when(it == 0)
        def _():
            carry_ref[...] = jnp.zeros_like(carry_ref)

        rows = _rows8(cw)
        sp_row = sp_ref[...]

        def pair(q, carry):
            base = pl.multiple_of(q * rows16, rows16)
            halves = []
            for half in range(2):
                sl = pl.ds(pl.multiple_of(base + half * SUBLANES, SUBLANES), SUBLANES)
                a, m, b = _rg_coeffs(r_ref[sl, :], i_ref[sl, :], xc_ref[sl, :], sp_row)
                a_ref[sl, :] = a
                m_ref[sl, :] = m
                for s in (1, 2, 4):
                    keep = rows >= s
                    sa = jnp.where(keep, pltpu.roll(a, s, 0), 1.0)
                    sb = jnp.where(keep, pltpu.roll(b, s, 0), 0.0)
                    b = b + a * sb
                    a = a * sa
                h = b + a * carry
                h_ref[sl, :] = h
                halves.append(h * _gelu(gate_ref[sl, :]))
                carry = h[SUBLANES - 1:SUBLANES, :]
            p_ref[pl.ds(base, rows16), :] = jnp.concatenate(halves, axis=0).astype(p_ref.dtype)
            return carry

        last = lax.fori_loop(0, nq, pair, carry_ref[0:1, :], unroll=2)
        carry_ref[...] = jnp.broadcast_to(last, carry_ref.shape)

    tile = pl.BlockSpec((tm, cw), lambda j, i: (i, j))
    gate_blk = gate_off // cw
    return pl.pallas_call(
        body, name=name, grid=(C // cw, T // tm),
        in_specs=[pl.BlockSpec((tm, cw), lambda j, i: (i, gate_blk + j)),
                  pl.BlockSpec((tm, cw), lambda j, i: (i, 2 * j)), pl.BlockSpec((tm, cw), lambda j, i: (i, 2 * j + 1)),
                  tile, pl.BlockSpec((1, cw), lambda j, i: (0, j))],
        out_specs=[tile, tile, tile, tile],
        out_shape=[jax.ShapeDtypeStruct((T, C), F32), jax.ShapeDtypeStruct((T, C), BF16),
                   jax.ShapeDtypeStruct((T, C), F32), jax.ShapeDtypeStruct((T, C), F32)],
        scratch_shapes=[pltpu.VMEM((SUBLANES, cw), F32)],
        compiler_params=_params(("parallel", "arbitrary"), 16 * tm * cw * 4),
    )(z, ri, ri, xc, sp)


def _rg_scan_bwd(dh, h, ri, xc, a_fwd, m_fwd, sp, *, T, C, tm=512, cw=256, name):
    ng, hb, nt = tm // SUBLANES, tm // SUBLANES, T // tm

    def body(dh_ref, h_ref, hp_ref, r_ref, i_ref, xc_ref, a_ref, m_ref, sp_ref,
             drai_ref, dxc_ref, crai_ref, csp_ref, cg_ref, ca_ref):
        step = pl.program_id(1)

        @pl.when(step == 0)
        def _():
            cg_ref[...] = jnp.zeros_like(cg_ref)
            ca_ref[...] = jnp.zeros_like(ca_ref)

        rows = _rows8(cw)
        sp_row = sp_ref[...]
        hhalo = jnp.where(step == nt - 1, 0.0, hp_ref[...])

        def group(gi, carry):
            g_next, a_next, s_ra, s_ia, s_sp = carry
            g = ng - 1 - gi
            off = pl.multiple_of(g * SUBLANES, SUBLANES)
            sl = pl.ds(off, SUBLANES)
            rr, ii, xx = r_ref[sl, :], i_ref[sl, :], xc_ref[sl, :]
            a, m = a_ref[sl, :], m_ref[sl, :]
            hh = h_ref[sl, :]
            hpv = h_ref[pl.ds(pl.multiple_of(jnp.maximum(off - SUBLANES, 0), SUBLANES), SUBLANES), :]
            hpv = jnp.where(g == 0, hhalo, hpv)
            hprev = _shift_down(hh, hpv, 1, rows)
            d = dh_ref[sl, :]
            c = jnp.where(rows < SUBLANES - 1, pltpu.roll(a, SUBLANES - 1, 0), a_next)
            for s in (1, 2, 4):
                keep = rows < SUBLANES - s
                sc = jnp.where(keep, pltpu.roll(c, SUBLANES - s, 0), 1.0)
                sd = jnp.where(keep, pltpu.roll(d, SUBLANES - s, 0), 0.0)
                d = d + c * sd
                c = c * sc
            gg = d + c * g_next
            da = gg * hprev
            dm = gg * (ii * xx)
            di = gg * (m * xx)
            dxc_ref[sl, :] = gg * (m * ii)
            dla = da * a - dm * (a * a / m)
            dra = dla * ((-RG_C) * sp_row) * (rr * (1.0 - rr))
            dia = di * (ii * (1.0 - ii))
            drai_ref[sl, pl.ds(0, cw)] = dra
            drai_ref[sl, pl.ds(cw, cw)] = dia
            return (gg[0:1, :], a[0:1, :], s_ra + dra, s_ia + dia, s_sp + dla * ((-RG_C) * rr))

        zero = jnp.zeros((SUBLANES, cw), F32)
        g_first, a_first, s_ra, s_ia, s_sp = lax.fori_loop(
            0, ng, group, (cg_ref[0:1, :], ca_ref[0:1, :], zero, zero, zero), unroll=2)
        cg_ref[...] = jnp.broadcast_to(g_first, cg_ref.shape)
        ca_ref[...] = jnp.broadcast_to(a_first, ca_ref.shape)
        for ref, acc in ((crai_ref.at[:, pl.ds(0, cw)], s_ra), (crai_ref.at[:, pl.ds(cw, cw)], s_ia), (csp_ref, s_sp)):
            _accumulate(ref, jnp.sum(acc, axis=0, keepdims=True), step == 0)

    tile = pl.BlockSpec((tm, cw), lambda j, i: (nt - 1 - i, j))
    wide = pl.BlockSpec((tm, 2 * cw), lambda j, i: (nt - 1 - i, j))
    vec = pl.BlockSpec((1, cw), lambda j, i: (0, j))
    return pl.pallas_call(
        body, name=name, grid=(C // cw, nt),
        in_specs=[tile, tile, pl.BlockSpec((SUBLANES, cw), lambda j, i: (jnp.maximum((nt - 1 - i) * hb - 1, 0), j)),
                  pl.BlockSpec((tm, cw), lambda j, i: (nt - 1 - i, 2 * j)),
                  pl.BlockSpec((tm, cw), lambda j, i: (nt - 1 - i, 2 * j + 1)), tile, tile, tile, vec],
        out_specs=[wide, tile, pl.BlockSpec((1, 2 * cw), lambda j, i: (0, j)), vec],
        out_shape=[jax.ShapeDtypeStruct((T, 2 * C), F32), jax.ShapeDtypeStruct((T, C), F32),
                   jax.ShapeDtypeStruct((1, 2 * C), F32), jax.ShapeDtypeStruct((1, C), F32)],
        scratch_shapes=[pltpu.VMEM((SUBLANES, cw), F32), pltpu.VMEM((SUBLANES, cw), F32)],
        compiler_params=_params(("parallel", "arbitrary"), 24 * tm * cw * 4),
    )(dh, h, h, ri, ri, xc, a_fwd, m_fwd, sp)


def _cscan_tables(lr, li, reverse):
    lam = (lr.reshape(-1), -li.reshape(-1) if reverse else li.reshape(-1))

    def mul(p, q):
        return p[0] * q[0] - p[1] * q[1], p[0] * q[1] + p[1] * q[0]

    pows = [lam]
    for _ in range(SUBLANES - 1):
        pows.append(mul(pows[-1], lam))
    zero = jnp.zeros_like(lam[0])
    tab = jnp.stack([pows[0][0], pows[0][1], pows[1][0], pows[1][1], pows[3][0], pows[3][1], zero, zero])
    if reverse:
        pows = pows[::-1]
    return tab, jnp.stack([p[0] for p in pows]), jnp.stack([p[1] for p in pows])


def _power_slabs(lr, li, n):
    pr, pi = lr.reshape(1, -1), li.reshape(1, -1)
    while pr.shape[0] < n:
        tr, ti = pr[-1:], pi[-1:]
        pr, pi = (jnp.concatenate([pr, pr * tr - pi * ti], axis=0), jnp.concatenate([pi, pr * ti + pi * tr], axis=0))
    return jnp.repeat(pr, SUBLANES, axis=0), jnp.repeat(pi, SUBLANES, axis=0), pr[-1], pi[-1]


def _rows_to_segments(src_ref, dst_ref):
    seg = src_ref.shape[0] // SUBLANES
    for g in range(seg):
        dst_ref[pl.ds(g * SUBLANES, SUBLANES), :] = src_ref[pl.ds(g, SUBLANES, stride=seg), :].astype(dst_ref.dtype)


def _segments_to_rows(src_ref, dst_ref):
    seg = src_ref.shape[0] // SUBLANES
    for r in range(SUBLANES):
        dst_ref[pl.ds(r * seg, seg), :] = src_ref[pl.ds(r, seg, stride=SUBLANES), :].astype(dst_ref.dtype)


def _seg_scan_tile(xr_ref, xi_ref, pbr_ref, pbi_ref, tab_ref, pwr_ref, pwi_ref, cr_ref, ci_ref, *, reverse, h=None):
    tm, cw = xr_ref.shape
    seg = tm // SUBLANES
    rows = _rows8(cw)
    sign = -1.0 if reverse else 1.0
    l_re, l_im = pbr_ref[0:1, :], sign * pbi_ref[0:1, :]

    def slab(g):
        return pl.ds(pl.multiple_of(g * SUBLANES, SUBLANES), SUBLANES)

    def local(k, state):
        sl = slab(seg - 1 - k if reverse else k)
        sr, si = state
        nr = xr_ref[sl, :] + (l_re * sr - l_im * si)
        ni = xi_ref[sl, :] + (l_re * si + l_im * sr)
        xr_ref[sl, :] = nr
        xi_ref[sl, :] = ni
        return nr, ni

    zero = jnp.zeros((SUBLANES, cw), F32)
    er, ei = lax.fori_loop(0, seg, local, (zero, zero), unroll=2)

    for k, s in enumerate((1, 2, 4)):
        shift = SUBLANES - s if reverse else s
        keep = rows < SUBLANES - s if reverse else rows >= s
        sr = jnp.where(keep, pltpu.roll(er, shift, 0), 0.0)
        si = jnp.where(keep, pltpu.roll(ei, shift, 0), 0.0)
        m_re, m_im = tab_ref[2 * k:2 * k + 1, :], tab_ref[2 * k + 1:2 * k + 2, :]
        er, ei = er + (m_re * sr - m_im * si), ei + (m_re * si + m_im * sr)
    cin_r, cin_i = cr_ref[0:1, :], ci_ref[0:1, :]
    pwr, pwi = pwr_ref[...], pwi_ref[...]
    er, ei = er + (pwr * cin_r - pwi * cin_i), ei + (pwr * cin_i + pwi * cin_r)
    if reverse:
        ent_r = jnp.where(rows == SUBLANES - 1, cin_r, pltpu.roll(er, SUBLANES - 1, 0))
        ent_i = jnp.where(rows == SUBLANES - 1, cin_i, pltpu.roll(ei, SUBLANES - 1, 0))
        out_r, out_i = er[0:1, :], ei[0:1, :]
    else:
        ent_r = jnp.where(rows == 0, cin_r, pltpu.roll(er, 1, 0))
        ent_i = jnp.where(rows == 0, cin_i, pltpu.roll(ei, 1, 0))
        out_r, out_i = er[SUBLANES - 1:SUBLANES, :], ei[SUBLANES - 1:SUBLANES, :]
    cr_ref[...] = jnp.broadcast_to(out_r, cr_ref.shape)
    ci_ref[...] = jnp.broadcast_to(out_i, ci_ref.shape)

    if h is not None:
        hr_ref, hi_ref, hr_last, hi_last = h
        hr_wrap = _shift_down(hr_ref[pl.ds(tm - SUBLANES, SUBLANES), :], hr_last, 1, rows)
        hi_wrap = _shift_down(hi_ref[pl.ds(tm - SUBLANES, SUBLANES), :], hi_last, 1, rows)

    def fix(g, sums):
        sl = slab(g)
        power = slab(seg - 1 - g) if reverse else sl
        pr, pi = pbr_ref[power, :], sign * pbi_ref[power, :]
        nr = xr_ref[sl, :] + (pr * ent_r - pi * ent_i)
        ni = xi_ref[sl, :] + (pr * ent_i + pi * ent_r)
        xr_ref[sl, :] = nr
        xi_ref[sl, :] = ni
        if h is None:
            return sums
        before = slab(jnp.maximum(g - 1, 0))
        hr1 = jnp.where(g == 0, hr_wrap, hr_ref[before, :])
        hi1 = jnp.where(g == 0, hi_wrap, hi_ref[before, :])
        return sums[0] + (nr * hr1 + ni * hi1), sums[1] + (ni * hr1 - nr * hi1)

    return lax.fori_loop(0, seg, fix, (zero, zero) if h is not None else (), unroll=2)


S5_TILE = 512


def _s5_fwd(z, u_off, wb_re, wb_im, wc_re, wc_im_neg, d_row, powers, *, T, tm=S5_TILE, name):
    J, ku, kp = wb_re.shape
    nt = T // tm
    pb_re, pb_im, top_re, top_im = powers
    tab, pw_re, pw_im = _cscan_tables(top_re, top_im, False)
    u_blk = u_off // ku

    def body(u_ref, wbr_ref, wbi_ref, wcr_ref, wci_ref, d_ref, pbr_ref, pbi_ref, tab_ref, pwr_ref, pwi_ref,
             hr_ref, hi_ref, y_ref, yg_ref, cr_ref, ci_ref, us_ref, ys_ref):
        @pl.when(pl.program_id(1) == 0)
        def _():
            cr_ref[...] = jnp.zeros_like(cr_ref)
            ci_ref[...] = jnp.zeros_like(ci_ref)

        _rows_to_segments(u_ref, us_ref)
        u = us_ref[...]
        ub = u.astype(BF16)
        hr_ref[...] = jnp.dot(ub, wbr_ref[...], preferred_element_type=F32)
        hi_ref[...] = jnp.dot(ub, wbi_ref[...], preferred_element_type=F32)
        _seg_scan_tile(hr_ref, hi_ref, pbr_ref, pbi_ref, tab_ref, pwr_ref, pwi_ref, cr_ref, ci_ref, reverse=False)
        y = (jnp.dot(hr_ref[...].astype(BF16), wcr_ref[...], preferred_element_type=F32)
             + jnp.dot(hi_ref[...].astype(BF16), wci_ref[...], preferred_element_type=F32) + d_ref[...] * u)
        ys_ref[...] = y
        _segments_to_rows(ys_ref, y_ref)
        ys_ref[...] = _gelu(y)
        _segments_to_rows(ys_ref, yg_ref)

    wb_spec = pl.BlockSpec((None, ku, kp), lambda j, i: (j, 0, 0))
    wc_spec = pl.BlockSpec((None, kp, ku), lambda j, i: (j, 0, 0))
    small = pl.BlockSpec((SUBLANES, kp), lambda j, i: (0, j))
    slabs = pl.BlockSpec((tm, kp), lambda j, i: (0, j))
    state = pl.BlockSpec((tm, kp), lambda j, i: (i, j))
    chan = pl.BlockSpec((tm, ku), lambda j, i: (i, j))
    return pl.pallas_call(
        body, name=name, grid=(J, nt),
        in_specs=[pl.BlockSpec((tm, ku), lambda j, i: (i, u_blk + j)), wb_spec, wb_spec, wc_spec, wc_spec,
                  pl.BlockSpec((1, ku), lambda j, i: (0, j)), slabs, slabs, small, small, small],
        out_specs=[state, state, chan, chan],
        out_shape=[jax.ShapeDtypeStruct((T, J * kp), F32)] * 2
        + [jax.ShapeDtypeStruct((T, J * ku), F32), jax.ShapeDtypeStruct((T, J * ku), BF16)],
        scratch_shapes=[pltpu.VMEM((SUBLANES, kp), F32), pltpu.VMEM((SUBLANES, kp), F32),
                        pltpu.VMEM((tm, ku), F32), pltpu.VMEM((tm, ku), F32)],
        compiler_params=_params(("parallel", "arbitrary"), 14 * tm * kp * 4),
    )(z, wb_re, wb_im, wc_re, wc_im_neg, d_row, pb_re, pb_im, tab, pw_re, pw_im)


def _s5_bwd(dy, z, u_off, h_re, h_im, wb_re, wb_im, wc_re, wc_im_neg, d_row, powers, *, T, tm=S5_TILE, name):
    J, ku, kp = wb_re.shape
    nt, hb = T // tm, tm // SUBLANES
    pb_re, pb_im, top_re, top_im = powers
    tab, pw_re, pw_im = _cscan_tables(top_re, top_im, True)
    u_blk = u_off // ku
    contract_rows = (((0,), (0,)), ((), ()))
    contract_cols = (((1,), (1,)), ((), ()))

    def body(dy_ref, u_ref, hr_ref, hrp_ref, hi_ref, hip_ref, wbr_ref, wbi_ref, wcr_ref, wci_ref, d_ref,
             pbr_ref, pbi_ref, tab_ref, pwr_ref, pwi_ref,
             du_ref, dlr_ref, dli_ref, dd_ref, dwbr_ref, dwbi_ref, dwcr_ref, dwci_ref,
             gr_ref, gi_ref, cr_ref, ci_ref, dys_ref, us_ref):
        step = pl.program_id(1)
        first = step == 0

        @pl.when(first)
        def _():
            cr_ref[...] = jnp.zeros_like(cr_ref)
            ci_ref[...] = jnp.zeros_like(ci_ref)

        _rows_to_segments(dy_ref, dys_ref)
        _rows_to_segments(u_ref, us_ref)
        dy_t, u = dys_ref[...], us_ref[...]
        dyb, ub = dy_t.astype(BF16), u.astype(BF16)
        gr_ref[...] = lax.dot_general(dyb, wcr_ref[...], contract_cols, preferred_element_type=F32)
        gi_ref[...] = lax.dot_general(dyb, wci_ref[...], contract_cols, preferred_element_type=F32)
        hr_last = jnp.where(step == nt - 1, 0.0, hrp_ref[...])
        hi_last = jnp.where(step == nt - 1, 0.0, hip_ref[...])
        s_re, s_im = _seg_scan_tile(gr_ref, gi_ref, pbr_ref, pbi_ref, tab_ref, pwr_ref, pwi_ref, cr_ref, ci_ref,
                                    reverse=True, h=(hr_ref, hi_ref, hr_last, hi_last))
        _accumulate(dlr_ref, jnp.sum(s_re, axis=0, keepdims=True), first)
        _accumulate(dli_ref, jnp.sum(s_im, axis=0, keepdims=True), first)
        grb, gib = gr_ref[...].astype(BF16), gi_ref[...].astype(BF16)
        du = (lax.dot_general(grb, wbr_ref[...], contract_cols, preferred_element_type=F32)
              + lax.dot_general(gib, wbi_ref[...], contract_cols, preferred_element_type=F32) + dy_t * d_ref[...])
        dys_ref[...] = du
        _segments_to_rows(dys_ref, du_ref)
        _accumulate(dd_ref, jnp.sum(dy_t * u, axis=0, keepdims=True), first)
        _accumulate(dwbr_ref, lax.dot_general(ub, grb, contract_rows, preferred_element_type=F32), first)
        _accumulate(dwbi_ref, lax.dot_general(ub, gib, contract_rows, preferred_element_type=F32), first)
        _accumulate(dwcr_ref, lax.dot_general(hr_ref[...].astype(BF16), dyb, contract_rows,
                                              preferred_element_type=F32), first)
        _accumulate(dwci_ref, lax.dot_general(hi_ref[...].astype(BF16), dyb, contract_rows,
                                              preferred_element_type=F32), first)

    def tix(i):
        return nt - 1 - i

    wb_spec = pl.BlockSpec((None, ku, kp), lambda j, i: (j, 0, 0))
    wc_spec = pl.BlockSpec((None, kp, ku), lambda j, i: (j, 0, 0))
    small = pl.BlockSpec((SUBLANES, kp), lambda j, i: (0, j))
    state = pl.BlockSpec((tm, kp), lambda j, i: (tix(i), j))
    halo = pl.BlockSpec((SUBLANES, kp), lambda j, i: (jnp.maximum(tix(i) * hb - 1, 0), j))
    chan = pl.BlockSpec((tm, ku), lambda j, i: (tix(i), j))
    svec = pl.BlockSpec((1, kp), lambda j, i: (0, j))
    cvec = pl.BlockSpec((1, ku), lambda j, i: (0, j))
    slabs = pl.BlockSpec((tm, kp), lambda j, i: (0, j))
    return pl.pallas_call(
        body, name=name, grid=(J, nt),
        in_specs=[chan, pl.BlockSpec((tm, ku), lambda j, i: (tix(i), u_blk + j)), state, halo, state, halo,
                  wb_spec, wb_spec, wc_spec, wc_spec, cvec, slabs, slabs, small, small, small],
        out_specs=[chan, svec, svec, cvec, wb_spec, wb_spec, wc_spec, wc_spec],
        out_shape=[jax.ShapeDtypeStruct((T, J * ku), BF16), jax.ShapeDtypeStruct((1, J * kp), F32),
                   jax.ShapeDtypeStruct((1, J * kp), F32), jax.ShapeDtypeStruct((1, J * ku), F32),
                   jax.ShapeDtypeStruct((J, ku, kp), F32), jax.ShapeDtypeStruct((J, ku, kp), F32),
                   jax.ShapeDtypeStruct((J, kp, ku), F32), jax.ShapeDtypeStruct((J, kp, ku), F32)],
        scratch_shapes=[pltpu.VMEM((tm, kp), F32), pltpu.VMEM((tm, kp), F32),
                        pltpu.VMEM((SUBLANES, kp), F32), pltpu.VMEM((SUBLANES, kp), F32),
                        pltpu.VMEM((tm, ku), F32), pltpu.VMEM((tm, ku), F32)],
        compiler_params=_params(("parallel", "arbitrary"), 16 * tm * kp * 4),
    )(dy, z, h_re, h_re, h_im, h_im, wb_re, wb_im, wc_re, wc_im_neg, d_row, pb_re, pb_im, tab, pw_re, pw_im)


def _mesh_pos():
    return lax.axis_index("x"), lax.axis_index("y"), lax.axis_index("c")


def _dev_index(px, py, pc):
    return 4 * px + 2 * py + pc


def _all_gather(shards, name):
    n = len(shards)

    def body(*refs):
        ins, outs = refs[:n], refs[n:2 * n]
        send_sems, recv_sems, local_sems = refs[2 * n:]
        x, y, c = _mesh_pos()
        me, sibling = (x, y, c), (x, y, 1 - c)
        chips = [(1 - x, y), (x, 1 - y), (1 - x, 1 - y)]

        def copy(a, k, block, to, src=None):
            dst = outs[a].at[_dev_index(*block)]
            return pltpu.make_async_remote_copy(
                src_ref=dst if src is None else src, dst_ref=dst, send_sem=send_sems.at[a * 7 + k],
                recv_sem=recv_sems.at[a * 7 + k], device_id=to, device_id_type=MESH)

        mine = [pltpu.make_async_copy(ins[a], outs[a].at[_dev_index(*me)], local_sems.at[a]) for a in range(n)]
        for cp in mine:
            cp.start()
        first = []
        for a in range(n):
            first.append(copy(a, 0, me, sibling, src=ins[a]))
            first += [copy(a, 1 + j, me, (*chip, c), src=ins[a]) for j, chip in enumerate(chips)]
        for cp in first:
            cp.start()
        passed = []
        for j, chip in enumerate(chips):
            for a in range(n):
                copy(a, 1 + j, (*chip, c), me).wait_recv()
                fwd = copy(a, 4 + j, (*chip, c), sibling)
                fwd.start()
                passed.append(fwd)
        for a in range(n):
            copy(a, 0, sibling, me).wait_recv()
            for j, chip in enumerate(chips):
                copy(a, 4 + j, (*chip, 1 - c), me).wait_recv()
        for cp in first + passed:
            cp.wait_send()
        for cp in mine:
            cp.wait()

    return pl.pallas_call(
        body, name=name, in_specs=[ANY] * n, out_specs=[ANY] * n,
        out_shape=[jax.ShapeDtypeStruct((N_DEV,) + s.shape, s.dtype) for s in shards],
        scratch_shapes=[pltpu.SemaphoreType.DMA((7 * n,)), pltpu.SemaphoreType.DMA((7 * n,)),
                        pltpu.SemaphoreType.DMA((n,))],
    )(*shards)


def _exchange_blocks(parts, name):
    n = len(parts)
    relations = [(dx, dy, dc) for dx in (0, 1) for dy in (0, 1) for dc in (0, 1) if (dx, dy, dc) != (0, 0, 0)]

    def body(*refs):
        ins, outs = refs[:n], refs[n:2 * n]
        send_sems, recv_sems, local_sems = refs[2 * n:]
        x, y, c = _mesh_pos()
        me = _dev_index(x, y, c)
        mine = [pltpu.make_async_copy(ins[a].at[me], outs[a].at[me], local_sems.at[a]) for a in range(n)]
        for cp in mine:
            cp.start()
        copies = []
        for k, (dx, dy, dc) in enumerate(relations):
            peer = (x + dx - 2 * x * dx, y + dy - 2 * y * dy, c + dc - 2 * c * dc)
            for a in range(n):
                copies.append((pltpu.make_async_remote_copy(
                    src_ref=ins[a].at[_dev_index(*peer)], dst_ref=outs[a].at[me], send_sem=send_sems.at[a * 7 + k],
                    recv_sem=recv_sems.at[a * 7 + k], device_id=peer, device_id_type=MESH),
                    pltpu.make_async_remote_copy(
                    src_ref=ins[a].at[_dev_index(*peer)], dst_ref=outs[a].at[_dev_index(*peer)],
                    send_sem=send_sems.at[a * 7 + k], recv_sem=recv_sems.at[a * 7 + k], device_id=peer,
                    device_id_type=MESH)))
        for send, _ in copies:
            send.start()
        for _, recv in copies:
            recv.wait_recv()
        for send, _ in copies:
            send.wait_send()
        for cp in mine:
            cp.wait()

    return pl.pallas_call(
        body, name=name, in_specs=[ANY] * n, out_specs=[ANY] * n,
        out_shape=[jax.ShapeDtypeStruct(p.shape, p.dtype) for p in parts],
        scratch_shapes=[pltpu.SemaphoreType.DMA((7 * n,)), pltpu.SemaphoreType.DMA((7 * n,)),
                        pltpu.SemaphoreType.DMA((n,))],
    )(*parts)


HBM = pl.BlockSpec(memory_space=pltpu.HBM)
SEM = pl.BlockSpec(memory_space=pltpu.SEMAPHORE)
EFFECT = pltpu.SideEffectType.DATAFLOW_SIDE_EFFECTING
RELATIONS = [(dx, dy, dc) for dx in (0, 1) for dy in (0, 1) for dc in (0, 1) if (dx, dy, dc) != (0, 0, 0)]


def _peer(rel):
    x, y, c = _mesh_pos()
    dx, dy, dc = rel
    return (x + dx - 2 * x * dx, y + dy - 2 * y * dy, c + dc - 2 * c * dc)


def _split_copy(src_ref, land_ref, send_sems, recv_sems, k, scatter, incoming):
    peer = _peer(RELATIONS[k])
    me = _dev_index(*_mesh_pos())
    src = src_ref.at[_dev_index(*peer)] if scatter else src_ref
    dst = land_ref.at[_dev_index(*peer) if incoming else me]
    return pltpu.make_async_remote_copy(src_ref=src, dst_ref=dst, send_sem=send_sems.at[k], recv_sem=recv_sems.at[k],
                                        device_id=peer, device_id_type=MESH)


def _exchange_start(srcs, lands, *, scatter, name):
    n = len(srcs)

    def body(*refs):
        src_refs, land_refs = refs[:n], refs[n:2 * n]
        send, recv = refs[2 * n:3 * n], refs[3 * n:4 * n]
        token = refs[-1]
        for k in range(len(RELATIONS)):
            for a in range(n):
                _split_copy(src_refs[a], land_refs[a], send[a], recv[a], k, scatter, incoming=False).start()
        token[...] = jnp.zeros_like(token)

    n_rel = len(RELATIONS)
    outs = pl.pallas_call(
        body, name=name, in_specs=[HBM] * (2 * n),
        out_shape=[pltpu.SemaphoreType.DMA((n_rel,))] * (2 * n)
        + [pltpu.HBM(s.shape, s.dtype) for s in srcs] + [pltpu.HBM(s.shape, s.dtype) for s in lands]
        + [jax.ShapeDtypeStruct((SUBLANES, LANES), F32)],
        out_specs=[SEM] * (2 * n) + [HBM] * (2 * n) + [pl.BlockSpec(memory_space=pltpu.VMEM)],
        input_output_aliases={**{a: 2 * n + a for a in range(n)}, **{n + a: 3 * n + a for a in range(n)}},
        compiler_params=pltpu.CompilerParams(has_side_effects=EFFECT),
    )(*[pltpu.with_memory_space_constraint(s, pltpu.HBM) for s in srcs],
      *[pltpu.with_memory_space_constraint(s, pltpu.HBM) for s in lands])
    per_array = [(outs[a], outs[n + a], outs[2 * n + a], outs[3 * n + a]) for a in range(n)]
    return per_array, outs[-1]


def _exchange_wait(handle, after, *, scatter, name):
    send_sems, recv_sems, src_thru, land_thru = handle

    def body(src_ref, land_ref, send, recv, after_ref, src_dead, got_ref):
        for k in range(len(RELATIONS)):
            cp = _split_copy(src_ref, land_ref, send, recv, k, scatter, incoming=True)
            cp.wait_send()
            cp.wait_recv()

    return pl.pallas_call(
        body, name=name, in_specs=[HBM, HBM, SEM, SEM, ANY],
        out_shape=[pltpu.HBM(src_thru.shape, src_thru.dtype), pltpu.HBM(land_thru.shape, land_thru.dtype)],
        out_specs=[HBM, HBM], input_output_aliases={0: 0, 1: 1},
        compiler_params=pltpu.CompilerParams(has_side_effects=EFFECT),
    )(src_thru, land_thru, send_sems, recv_sems, after)[1]


def _landing_zone(own_block):
    me = _dev_index(*_mesh_pos())
    zone = lax.empty((N_DEV,) + own_block.shape, own_block.dtype)
    return lax.dynamic_update_index_in_dim(zone, own_block, me, 0)


def _row_tile(rows, want):
    t = min(want, rows) // SUBLANES * SUBLANES
    while rows % t:
        t -= SUBLANES
    return t


def _sum_slots(recv, *, tr, name):
    s_, r_, c_ = recv.shape
    tr = _row_tile(r_, tr)

    def body(g_ref, o_ref):
        acc = g_ref[0]
        for s in range(1, s_):
            acc = acc + g_ref[s]
        o_ref[...] = acc

    return pl.pallas_call(
        body, name=name, grid=(r_ // tr,),
        in_specs=[pl.BlockSpec((s_, tr, c_), lambda i: (0, i, 0))],
        out_specs=pl.BlockSpec((tr, c_), lambda i: (i, 0)),
        out_shape=jax.ShapeDtypeStruct((r_, c_), F32),
        compiler_params=_params(("parallel",), (2 * s_ + 3) * tr * c_ * 4),
    )(recv)


def _adamw(recv, w, m, v, *, tr, name):
    s_, r_, c_ = recv.shape
    tr = _row_tile(r_, tr)
    assert w.shape == (r_, c_), (name, w.shape, recv.shape)
    c1 = 1.0 - ADAM_B1 ** ADAM_STEP
    c2 = 1.0 - ADAM_B2 ** ADAM_STEP

    def body(g_ref, w_ref, m_ref, v_ref, go_ref, d_ref, mo_ref, vo_ref):
        g = g_ref[0].astype(F32)
        for s in range(1, s_):
            g = g + g_ref[s].astype(F32)
        mn = ADAM_B1 * m_ref[...] + (1.0 - ADAM_B1) * g
        vn = ADAM_B2 * v_ref[...] + (1.0 - ADAM_B2) * (g * g)
        go_ref[...] = g
        mo_ref[...] = mn
        vo_ref[...] = vn
        d_ref[...] = -ADAM_LR * ((mn / c1) / (jnp.sqrt(vn / c2) + ADAM_EPS) + ADAM_WD * w_ref[...])

    tile = pl.BlockSpec((tr, c_), lambda i: (i, 0))
    return pl.pallas_call(
        body, name=name, grid=(r_ // tr,),
        in_specs=[pl.BlockSpec((s_, tr, c_), lambda i: (0, i, 0)), tile, tile, tile],
        out_specs=[tile] * 4, out_shape=[jax.ShapeDtypeStruct((r_, c_), F32)] * 4,
        compiler_params=_params(("parallel",), (2 * s_ + 16) * tr * c_ * 4),
    )(recv, w, m, v)


def _s5_discretise(a_re, a_im, log_dt, b_re, b_im):
    dt = jnp.exp(log_dt)[:, None]
    lr = jnp.minimum(a_re, -1e-4)
    li = a_im
    mag = jnp.exp(lr * dt)
    lbr = mag * jnp.cos(li * dt)
    lbi = mag * jnp.sin(li * dt)
    zr, zi = lbr - 1.0, lbi
    den = lr * lr + li * li
    fr = (zr * lr + zi * li) / den
    fi = (zi * lr - zr * li) / den
    bbr = fr[..., None] * b_re - fi[..., None] * b_im
    bbi = fr[..., None] * b_im + fi[..., None] * b_re
    return lbr, lbi, bbr, bbi


def _softplus_neg(lam):
    return jnp.maximum(-lam, 0.0) + jnp.log(1.0 + jnp.exp(-jnp.abs(lam)))


S5_Q = 8
RG_Q = 2


def _local_step(x, tgt, W, comm):
    T, D = x.shape
    C = D
    G, P, H = W["ssm_b_re"].shape
    S = G * H
    F = W["mlp_b_up"].shape[1]
    n_in = 2 * C + S + 2 * D
    heads, hd = W["rg_wa"].shape[0], W["rg_wa"].shape[1]
    u_off, ga_off, gb_off = 2 * C, 2 * C + S, 2 * C + S + D

    sp, sp_vjp = jax.vjp(_softplus_neg, W["rg_lambda"])
    (lbr, lbi, bbr, bbi), s5_vjp = jax.vjp(_s5_discretise, W["ssm_a_re"], W["ssm_a_im"], W["ssm_log_dt"],
                                           W["ssm_b_re"], W["ssm_b_im"])
    lam_re, lam_im = lbr.reshape(-1), lbi.reshape(-1)
    jr, kr = heads // RG_Q, RG_Q * hd
    w_ri = jnp.concatenate([_bd_pack(W["rg_wa"], RG_Q), _bd_pack(W["rg_wx"], RG_Q)], axis=2).astype(BF16)
    b_ri = jnp.concatenate([W["rg_ba"].reshape(jr, kr), W["rg_bx"].reshape(jr, kr)], axis=1).reshape(1, -1)
    wb_re = _bd_pack(jnp.swapaxes(bbr, 1, 2), S5_Q).astype(BF16)
    wb_im = _bd_pack(jnp.swapaxes(bbi, 1, 2), S5_Q).astype(BF16)
    wc_re = _bd_pack(jnp.swapaxes(W["ssm_c_re"], 1, 2), S5_Q).astype(BF16)
    wc_im_neg = _bd_pack(jnp.swapaxes(-W["ssm_c_im"], 1, 2), S5_Q).astype(BF16)
    d_row = W["ssm_d"].reshape(1, S)

    x_bf = x.astype(BF16)
    w_in = comm.weight("w_in", None)
    z = _mm(x_bf, w_in, M=T, N=n_in, K=D, tm=512, tn=n_in // 4, tk=D, after=comm.gather_token, name="fwd_in_proj")
    xc = _conv_fwd(z, W["conv_w"], W["conv_b"], T=T, C=C, name="fwd_conv")
    ri = _bd([(xc, 0, w_ri)], T=T, J=jr, kb=kr, nb=2 * kr, extras=[(b_ri, "vec", 0)],
             epilogue=lambda acc, b: (_sig(acc + b),), name="fwd_gates")
    h, p, a_fwd, m_fwd = _rg_scan_fwd(z, ri, xc, sp, T=T, C=C, gate_off=C, cw=kr, name="fwd_rg_scan")
    w_a_out = comm.weight("w_a_out", p)
    y_a = _mm(p, w_a_out, M=T, N=D, K=C, out_dtypes=(BF16,), tm=512, tn=D, tk=C, name="fwd_rg_out")

    powers = _power_slabs(lam_re, lam_im, S5_TILE // SUBLANES)
    h_re, h_im, y_s, yg = _s5_fwd(z, u_off, wb_re, wb_im, wc_re, wc_im_neg, d_row, powers, T=T, name="fwd_s5")
    w_glu_w, w_glu_v = comm.weight("glu_w", yg), comm.weight("glu_v", yg)
    glu_a = _mm(yg, w_glu_w, M=T, N=D, K=S, out_dtypes=(BF16,), tm=1024, tn=D, tk=S, name="fwd_glu_w")
    cwm = 1024

    def mix_fn(b, ga, gb, ya, a):
        return b, _sig(ga) * ya.astype(F32) + _sig(gb) * (a.astype(F32) * _sig(b))

    glu_b, mix = _mm(yg, w_glu_v, M=T, N=D, K=S, tm=512, tn=cwm, tk=S,
                     extras=[(z, "mn", ga_off // cwm), (z, "mn", gb_off // cwm), (y_a, "mn"), (glu_a, "mn")],
                     epilogue=mix_fn, n_out=2, out_dtypes=(BF16, BF16), name="fwd_glu_v_mix")
    w_out = comm.weight("w_out", mix)
    def out_ln1_fn(acc, xv, g, b):
        s = ALPHA * xv + acc
        xhat, _ = _ln_stats(s)
        y = xhat * g + b
        return s, y, y

    s1, x1, x1_bf = _mm(mix, w_out, M=T, N=D, K=D, tm=256, tn=D, tk=D,
                        extras=[(x, "mn"), (W["ln1_g"], "n"), (W["ln1_b"], "n")], epilogue=out_ln1_fn, n_out=3,
                        out_dtypes=(F32, F32, BF16), name="fwd_out_proj_ln1")
    w_up = comm.weight("mlp_w_up", x1_bf)

    def mlp_up_fn(acc, b):
        hp = acc + b
        rl = jnp.maximum(hp, 0.0)
        return rl * rl, hp

    hact, hpre = _mm(x1_bf, w_up, M=T, N=F, K=D, tm=1024, tn=1024, tk=D, extras=[(W["mlp_b_up"], "n")],
                     epilogue=mlp_up_fn, n_out=2, out_dtypes=(BF16, BF16), name="fwd_mlp_up")
    w_down = comm.weight("mlp_w_down", hact)
    s2 = _mm(hact, w_down, M=T, N=D, K=F, tm=512, tn=1024, tk=4096,
             extras=[(x1, "mn"), (W["mlp_b_down"], "n")], epilogue=lambda acc, xv, b: (ALPHA * xv + acc + b,),
             name="fwd_mlp_down")

    def ln2_fn(s, t, g, b):
        xhat, rstd = _ln_stats(s)
        err = xhat * g + b - t
        dy = err * (1.0 / D)
        ds = _ln_bwd(dy, g, xhat, rstd)
        return ds, ds, 0.5 * dy * err, dy * xhat, dy, ds

    ds2, ds2_bf, loss_cols, d_ln2_g, d_ln2_b, d_b_down = _ew(
        ln2_fn, [(s2, "tile", 0), (tgt, "tile", 0), (W["ln2_g"], "vec", 0), (W["ln2_b"], "vec", 0)],
        T=T, C=D, n_out=2, n_cs=4, out_dtypes=(F32, BF16), tm=256, name="bwd_loss_ln2")
    d_w_down = _mm(hact, ds2_bf, M=F, N=D, K=T, ta=True, out_dtypes=(BF16,), tm=1024, tn=1024, tk=4096, name="bwd_w_down")
    sent = comm.send_grad("mlp_w_down", d_w_down)

    def dhpre_fn(acc, hp):
        dv = acc * (2.0 * jnp.maximum(hp.astype(F32), 0.0))
        return dv, dv

    dhpre, d_b_up = _mm(ds2_bf, w_down, M=T, N=F, K=D, tb=True, tm=1024, tn=1024, tk=D, extras=[(hpre, "mn")],
                        epilogue=dhpre_fn, n_cs=1, out_dtypes=(BF16,), after=sent, name="bwd_mlp_down")
    d_w_up = _mm(x1_bf, dhpre, M=D, N=F, K=T, ta=True, out_dtypes=(BF16,), n_split=N_DEV, tm=1024, tn=F // N_DEV, tk=4096, name="bwd_w_up")
    sent = comm.send_grad("mlp_w_up", d_w_up)
    dx1 = _mm(dhpre, w_up, M=T, N=D, K=F, tb=True, tm=512, tn=1024, tk=4096,
              extras=[(ds2, "mn")], epilogue=lambda acc, dv: (ALPHA * dv + acc,), after=sent, name="bwd_mlp_up")

    def ln1_bwd_fn(s, dy, g):
        xhat, rstd = _ln_stats(s)
        ds = _ln_bwd(dy, g, xhat, rstd)
        return ds, ds, dy * xhat, dy

    ds1, ds1_bf, d_ln1_g, d_ln1_b = _ew(ln1_bwd_fn, [(s1, "tile", 0), (dx1, "tile", 0), (W["ln1_g"], "vec", 0)],
                                        T=T, C=D, n_out=2, n_cs=2, out_dtypes=(F32, BF16), tm=256, name="bwd_ln1")
    d_w_out = _mm(mix, ds1_bf, M=D, N=D, K=T, ta=True, out_dtypes=(BF16,), tm=1024, tn=1024, tk=4096, name="bwd_w_out")
    sent = comm.send_grad("w_out", d_w_out)
    def mix_bwd_fn(dm, ga, gb, ya, a, b):
        ya, a, b = ya.astype(F32), a.astype(F32), b.astype(F32)
        sa, sb, sv = _sig(ga), _sig(gb), _sig(b)
        yb = a * sv
        dyb = dm * sb
        return (dm * ya * (sa * (1.0 - sa)), dm * yb * (sb * (1.0 - sb)), dm * sa, dyb * sv,
                dyb * a * (sv * (1.0 - sv)))

    dg_a, dg_b, dy_a, dglu_a, dglu_b = _mm(
        ds1_bf, w_out, M=T, N=D, K=D, tb=True, tm=512, tn=cwm, tk=D,
        extras=[(z, "mn", ga_off // cwm), (z, "mn", gb_off // cwm), (y_a, "mn"), (glu_a, "mn"), (glu_b, "mn")],
        epilogue=mix_bwd_fn, n_out=5, out_dtypes=(BF16,) * 5, after=sent, name="bwd_out_proj_mix")

    d_w_a_out = _mm(p, dy_a, M=C, N=D, K=T, ta=True, out_dtypes=(BF16,), tm=1024, tn=1024, tk=4096, name="bwd_w_a_out")
    sent = comm.send_grad("w_a_out", d_w_a_out)
    def dp_fn(dp, hv, gate):
        th = jnp.tanh(GELU_C * (gate + GELU_K * gate * gate * gate))
        gelu = 0.5 * gate * (1.0 + th)
        dgelu = 0.5 * (1.0 + th) + 0.5 * gate * (1.0 - th * th) * (GELU_C * (1.0 + 3.0 * GELU_K * gate * gate))
        return dp * gelu, dp * hv * dgelu

    dh, dgate = _mm(dy_a, w_a_out, M=T, N=C, K=D, tb=True, tm=256, tn=C, tk=D, extras=[(h, "mn"), (z, "mn", 1)],
                    epilogue=dp_fn, n_out=2, out_dtypes=(F32, BF16), after=sent, name="bwd_rg_out")
    drai, dxc0, d_b_ri, d_sp = _rg_scan_bwd(dh, h, ri, xc, a_fwd, m_fwd, sp, T=T, C=C, cw=kr, name="bwd_rg_scan")
    dxc = _bd([(drai, 0, w_ri)], T=T, J=jr, kb=2 * kr, nb=kr, tw=True, extras=[(dxc0, "tile", 0)],
              epilogue=lambda acc, d0: (acc + d0,), name="bwd_gates")
    d_w_ri = _bdw(xc, 0, drai, 0, T=T, J=jr, kb=kr, nb=2 * kr, name="bwd_w_gates")
    d_wa, d_wx = _bd_unpack(d_w_ri[:, :, :kr], RG_Q), _bd_unpack(d_w_ri[:, :, kr:], RG_Q)
    d_b_ri = d_b_ri.reshape(jr, 2 * kr)
    d_ba, d_bx = d_b_ri[:, :kr].reshape(1, -1), d_b_ri[:, kr:].reshape(1, -1)
    dxr, conv_sums = _conv_bwd(dxc, z, W["conv_w"], T=T, C=C, name="bwd_conv")
    d_conv_w, d_conv_b = conv_sums[0:4], conv_sums[4:5]
    (d_lambda,) = sp_vjp(d_sp)

    d_glu_w = _mm(yg, dglu_a, M=S, N=D, K=T, ta=True, out_dtypes=(BF16,), n_split=N_DEV, tm=1024, tn=D // N_DEV, tk=4096, name="bwd_w_glu_w")
    d_glu_v = _mm(yg, dglu_b, M=S, N=D, K=T, ta=True, out_dtypes=(BF16,), n_split=N_DEV, tm=1024, tn=D // N_DEV, tk=4096, name="bwd_w_glu_v")
    sent = comm.send_grad("glu_w", d_glu_w, "glu_v", d_glu_v)
    dyg0 = _mm(dglu_a, w_glu_w, M=T, N=S, K=D, tb=True, tm=512, tn=S, tk=D, after=sent, name="bwd_glu_w")
    dy_s = _mm(dglu_b, w_glu_v, M=T, N=S, K=D, tb=True, tm=512, tn=S, tk=D,
               extras=[(dyg0, "mn"), (y_s, "mn")], epilogue=lambda acc, d0, yv: ((acc + d0) * _dgelu(yv),),
               name="bwd_glu_v")
    du, d_lbr, d_lbi, d_ssm_d, d_wb_re, d_wb_im, d_wc_re, d_wc_im_neg = _s5_bwd(
        dy_s, z, u_off, h_re, h_im, wb_re, wb_im, wc_re, wc_im_neg, d_row, powers, T=T, name="bwd_s5")
    d_bbr = jnp.swapaxes(_bd_unpack(d_wb_re, S5_Q), 1, 2)
    d_bbi = jnp.swapaxes(_bd_unpack(d_wb_im, S5_Q), 1, 2)
    d_a_re, d_a_im, d_log_dt, d_b_re, d_b_im = s5_vjp((d_lbr.reshape(G, P), d_lbi.reshape(G, P), d_bbr, d_bbi))
    d_c_re = jnp.swapaxes(_bd_unpack(d_wc_re, S5_Q), 1, 2)
    d_c_im = -jnp.swapaxes(_bd_unpack(d_wc_im_neg, S5_Q), 1, 2)

    dz = jnp.concatenate([dxr.astype(BF16), dgate.astype(BF16), du, dg_a, dg_b], axis=1)
    d_w_in = _mm(x_bf, dz, M=D, N=n_in, K=T, ta=True, out_dtypes=(BF16,), n_split=N_DEV, tm=1024, tn=n_in // N_DEV, tk=4096, name="bwd_w_in")
    sent = comm.send_grad("w_in", d_w_in)
    grad_x = _mm(dz, w_in, M=T, N=D, K=n_in, tb=True, tm=512, tn=1024, tk=n_in // 2,
                 extras=[(ds1, "mn")], epilogue=lambda acc, dv: (ALPHA * dv + acc,), after=sent, name="bwd_in_proj")

    grads = dict(
        conv_w=d_conv_w, conv_b=d_conv_b, rg_wa=d_wa, rg_ba=d_ba, rg_wx=d_wx, rg_bx=d_bx,
        rg_lambda=d_lambda, ssm_a_re=d_a_re, ssm_a_im=d_a_im, ssm_log_dt=d_log_dt,
        ssm_b_re=d_b_re, ssm_b_im=d_b_im, ssm_c_re=d_c_re, ssm_c_im=d_c_im, ssm_d=d_ssm_d.reshape(G, H),
        ln1_g=d_ln1_g, ln1_b=d_ln1_b, mlp_b_up=d_b_up, mlp_b_down=d_b_down, ln2_g=d_ln2_g, ln2_b=d_ln2_b)
    return jnp.sum(loss_cols), grad_x, grads


BIG = ("w_in", "w_a_out", "glu_w", "glu_v", "w_out", "mlp_w_up", "mlp_w_down")
COL_SHARDED = ("w_in", "glu_w", "glu_v", "mlp_w_up")
SMALL = ("conv_w", "conv_b", "rg_wa", "rg_ba", "rg_wx", "rg_bx", "rg_lambda", "ssm_a_re", "ssm_a_im", "ssm_log_dt",
         "ssm_b_re", "ssm_b_im", "ssm_c_re", "ssm_c_im", "ssm_d", "ln1_g", "ln1_b", "mlp_b_up", "mlp_b_down", "ln2_g",
         "ln2_b")
ORDER = ("w_in", "conv_w", "conv_b", "rg_wa", "rg_ba", "rg_wx", "rg_bx", "rg_lambda", "w_a_out", "ssm_a_re",
         "ssm_a_im", "ssm_log_dt", "ssm_b_re", "ssm_b_im", "ssm_c_re", "ssm_c_im", "ssm_d", "glu_w", "glu_v", "w_out",
         "ln1_g", "ln1_b", "mlp_w_up", "mlp_b_up", "mlp_w_down", "mlp_b_down", "ln2_g", "ln2_b")
TILE_ELEMS = SUBLANES * LANES


def _pack(arrs):
    pieces = []
    for a in arrs:
        flat = a.reshape(-1)
        flat = jnp.pad(flat, (0, (-flat.shape[0]) % TILE_ELEMS))
        pieces.append(flat.reshape(-1, LANES))
    rows = sum(p.shape[0] for p in pieces)
    pad_rows = (-rows) % (N_DEV * SUBLANES)
    if pad_rows:
        pieces.append(jnp.zeros((pad_rows, LANES), pieces[0].dtype))
    return jnp.concatenate(pieces, axis=0)


def _unpack(packed, shapes):
    out, row = [], 0
    for shp in shapes:
        n = math.prod(shp)
        rows = -(-n // TILE_ELEMS) * SUBLANES
        out.append(packed[row:row + rows].reshape(-1)[:n].reshape(shp))
        row += rows
    return out


class _Comm:
    def __init__(self, w):
        first = _all_gather([w["w_in"].astype(BF16), w["conv_w"]], name="gather_w_in")
        self._weights = {"w_in": first[0]}
        self.conv_w = jnp.swapaxes(first[1], 0, 1).reshape(w["conv_w"].shape[0], -1)
        later = [k for k in BIG if k != "w_in"]
        shards = [w[k].astype(BF16) for k in later]
        handles, self.gather_token = _exchange_start(shards, [_landing_zone(s) for s in shards], scatter=False,
                                                     name="gather_weights_start")
        self._gathers = dict(zip(later, handles))
        self._grads = {}

    def weight(self, k, after):
        if k not in self._weights:
            self._weights[k] = _exchange_wait(self._gathers.pop(k), after, scatter=False, name="gather_wait_" + k)
        gk = self._weights[k]
        if k in COL_SHARDED:
            return jnp.swapaxes(gk, 0, 1).reshape(gk.shape[1], -1)
        return gk.reshape(-1, gk.shape[-1])

    def send_grad(self, *names_and_parts):
        names, parts = names_and_parts[0::2], names_and_parts[1::2]
        parts = [p if k in COL_SHARDED else p.reshape(N_DEV, p.shape[0] // N_DEV, p.shape[1])
                 for k, p in zip(names, parts)]
        me = _dev_index(*_mesh_pos())
        lands = [_landing_zone(lax.dynamic_index_in_dim(p, me, 0, keepdims=False)) for p in parts]
        handles, token = _exchange_start(parts, lands, scatter=True, name="grad_start_" + names[0])
        self._grads.update(zip(names, handles))
        return token

    def received_grad(self, k, after):
        return _exchange_wait(self._grads.pop(k), after, scatter=True, name="grad_wait_" + k)


def _step(x, tgt, w, m, v):
    dev = _dev_index(*_mesh_pos())

    comm = _Comm(w)
    small = dict(w)
    small["conv_w"] = comm.conv_w
    for k in ("conv_b", "rg_ba", "rg_bx", "rg_lambda", "ln1_g", "ln1_b", "mlp_b_up", "mlp_b_down", "ln2_g", "ln2_b"):
        small[k] = w[k].reshape(1, -1)

    loss_part, grad_x, grads = _local_step(x, tgt, small, comm)

    out_g, out_d, out_m, out_v = {}, {}, {}, {}
    for k in BIG:
        rk = comm.received_grad(k, grad_x)
        out_g[k], out_d[k], out_m[k], out_v[k] = _adamw(rk, w[k], m[k], v[k], tr=128, name="adamw_" + k)

    small_shapes = [grads[k].shape for k in SMALL]
    (small_recv,) = _exchange_blocks([_pack([grads[k] for k in SMALL]).reshape(N_DEV, -1, LANES)],
                                     name="exchange_small_grads")
    small_block = _sum_slots(small_recv, tr=512, name="sum_small_grads")
    (small_all,) = _all_gather([small_block], name="gather_small_grads")
    g_small = dict(zip(SMALL, _unpack(small_all.reshape(-1, LANES), small_shapes)))
    cw_cols = w["conv_w"].shape[1]
    g_small["conv_w"] = lax.dynamic_slice_in_dim(g_small["conv_w"], dev * cw_cols, cw_cols, axis=1)
    shapes = [w[k].shape for k in SMALL]
    g_pack, w_pack, m_pack, v_pack = [_pack([src[k] for k in SMALL]) for src in (g_small, w, m, v)]
    res = _adamw(g_pack[None], w_pack, m_pack, v_pack, tr=1024, name="adamw_small")
    for dst, packed in zip((out_g, out_d, out_m, out_v), res):
        dst.update(zip(SMALL, _unpack(packed, shapes)))

    loss = lax.psum(loss_part, ("x", "y", "c"))
    return loss, grad_x, out_g, out_d, out_m, out_v


def kernel(x, w_in, conv_w, conv_b, rg_wa, rg_ba, rg_wx, rg_bx, rg_lambda, w_a_out, ssm_a_re, ssm_a_im, ssm_log_dt, ssm_b_re, ssm_b_im, ssm_c_re, ssm_c_im, ssm_d, glu_w, glu_v, w_out, ln1_g, ln1_b, mlp_w_up, mlp_b_up, mlp_w_down, mlp_b_down, ln2_g, ln2_b, loss_target, m_w_in, m_conv_w, m_conv_b, m_rg_wa, m_rg_ba, m_rg_wx, m_rg_bx, m_rg_lambda, m_w_a_out, m_ssm_a_re, m_ssm_a_im, m_ssm_log_dt, m_ssm_b_re, m_ssm_b_im, m_ssm_c_re, m_ssm_c_im, m_ssm_d, m_glu_w, m_glu_v, m_w_out, m_ln1_g, m_ln1_b, m_mlp_w_up, m_mlp_b_up, m_mlp_w_down, m_mlp_b_down, m_ln2_g, m_ln2_b, v_w_in, v_conv_w, v_conv_b, v_rg_wa, v_rg_ba, v_rg_wx, v_rg_bx, v_rg_lambda, v_w_a_out, v_ssm_a_re, v_ssm_a_im, v_ssm_log_dt, v_ssm_b_re, v_ssm_b_im, v_ssm_c_re, v_ssm_c_im, v_ssm_d, v_glu_w, v_glu_v, v_w_out, v_ln1_g, v_ln1_b, v_mlp_w_up, v_mlp_b_up, v_mlp_w_down, v_mlp_b_down, v_ln2_g, v_ln2_b):
    args = locals()
    w = {k: args[k][0] for k in ORDER}
    m = {k: args["m_" + k][0] for k in ORDER}
    v = {k: args["v_" + k][0] for k in ORDER}
    loss, grad_x, out_g, out_d, out_m, out_v = _step(x[0], loss_target[0], w, m, v)
    outs = [loss, grad_x[None]]
    for group in (out_g, out_d, out_m, out_v):
        outs += [group[k].reshape(args[k].shape) for k in ORDER]
    return tuple(outs)
```

```python
import functools
import math

import jax
import jax.numpy as jnp
from jax import lax
from jax.experimental import pallas as pl
from jax.experimental.pallas import tpu as pltpu

F32 = jnp.float32
BF16 = jnp.bfloat16
MESH = pl.DeviceIdType.MESH
N_DEV = 8
SUBLANES = 8
LANES = 128
VMEM_BYTES_V7X = 64 * 2 ** 20
VMEM_CAP = VMEM_BYTES_V7X - 8 * 2 ** 20

ALPHA = 2.0 ** 0.25
LN_EPS = 1e-5
RG_C = 8.0
ADAM_LR, ADAM_B1, ADAM_B2, ADAM_EPS, ADAM_WD, ADAM_STEP = 0.001, 0.9, 0.999, 1e-08, 0.01, 10
GELU_C = math.sqrt(2.0 / math.pi)
GELU_K = 0.044715

ANY = pl.BlockSpec(memory_space=pl.ANY)


def _params(sem, vmem_bytes):
    limit = int(min(max(2 * vmem_bytes, 16 * 2 ** 20), VMEM_CAP))
    return pltpu.CompilerParams(dimension_semantics=sem, vmem_limit_bytes=limit)


def _sig(x):
    return 1.0 / (1.0 + jnp.exp(-x))


def _gelu(x):
    return 0.5 * x * (1.0 + jnp.tanh(GELU_C * (x + GELU_K * x * x * x)))


def _dgelu(x):
    th = jnp.tanh(GELU_C * (x + GELU_K * x * x * x))
    return 0.5 * (1.0 + th) + 0.5 * x * (1.0 - th * th) * (GELU_C * (1.0 + 3.0 * GELU_K * x * x))


def _one_minus_exp(x, exp_half_x):
    p = x * (1.0 + x * (1 / 2 + x * (1 / 6 + x * (1 / 24 + x * (1 / 120)))))
    return jnp.where(x > -1 / 16, -p, 1.0 - exp_half_x * exp_half_x)


def _accumulate(ref, val, first):
    @pl.when(first)
    def _():
        ref[...] = val

    @pl.when(jnp.logical_not(first))
    def _():
        ref[...] += val


def _rows8(cw):
    return lax.broadcasted_iota(jnp.int32, (SUBLANES, cw), 0)


def _shift_down(cur, prev, s, rows):
    return jnp.where(rows < s, pltpu.roll(prev, s, 0), pltpu.roll(cur, s, 0))


def _shift_up(cur, nxt, s, rows):
    return jnp.where(rows < SUBLANES - s, pltpu.roll(cur, SUBLANES - s, 0), pltpu.roll(nxt, SUBLANES - s, 0))


def _mm(a, b, *, M, N, K, ta=False, tb=False, b_split=1, n_split=1, a_fn=None, extras=(), epilogue=None,
        n_out=1, n_cs=0, out_dtypes=None, tm=512, tn=512, tk=512, after=None, name):
    tm, tn, tk = min(tm, M), min(tn, N), min(tk, K)
    assert M % tm == 0 and N % tn == 0 and K % tk == 0, (name, M, N, K, tm, tn, tk)
    nk = K // tk
    grid = (N // tn, M // tm, nk)
    a_spec = pl.BlockSpec((tk, tm), lambda j, i, k: (k, i)) if ta else pl.BlockSpec((tm, tk), lambda j, i, k: (i, k))
    if b_split == 1:
        b_spec = pl.BlockSpec((tn, tk), lambda j, i, k: (j, k)) if tb else pl.BlockSpec((tk, tn), lambda j, i, k: (k, j))
    elif tb:
        kb = (K // b_split) // tk
        assert kb * tk * b_split == K, name
        b_spec = pl.BlockSpec((None, tn, tk), lambda j, i, k: (k // kb, j, k % kb))
    else:
        nb = (N // b_split) // tn
        assert nb * tn * b_split == N, name
        b_spec = pl.BlockSpec((None, tk, tn), lambda j, i, k: (j // nb, k, j % nb))
    in_specs = [a_spec, b_spec]
    for arr, kind, *col_off in extras:
        off = col_off[0] if col_off else 0
        in_specs.append(pl.BlockSpec((tm, tn), lambda j, i, k, off=off: (i, off + j)) if kind == "mn"
                        else pl.BlockSpec((1, tn), lambda j, i, k: (0, j)))
    out_dtypes = (F32,) * n_out if out_dtypes is None else out_dtypes
    if n_split == 1:
        out_shape = [jax.ShapeDtypeStruct((M, N), dt) for dt in out_dtypes]
        out_specs = [pl.BlockSpec((tm, tn), lambda j, i, k: (i, j)) for _ in range(n_out)]
    else:
        assert n_out == 1
        nbo = (N // n_split) // tn
        assert nbo * tn * n_split == N, name
        out_shape = [jax.ShapeDtypeStruct((n_split, M, N // n_split), out_dtypes[0])]
        out_specs = [pl.BlockSpec((None, tm, tn), lambda j, i, k: (j // nbo, i, j % nbo))]
    out_shape += [jax.ShapeDtypeStruct((1, N), F32) for _ in range(n_cs)]
    out_specs += [pl.BlockSpec((1, tn), lambda j, i, k: (0, j)) for _ in range(n_cs)]
    ne = len(extras)
    dims = (((0 if ta else 1,), (1 if tb else 0,)), ((), ()))

    n_after = 0 if after is None else 1
    in_specs += [ANY] * n_after

    def body(*refs):
        a_ref, b_ref = refs[0], refs[1]
        ex_refs = refs[2:2 + ne]
        first_out = 2 + ne + n_after
        out_refs = refs[first_out:first_out + n_out]
        cs_refs = refs[first_out + n_out:first_out + n_out + n_cs]
        i, k = pl.program_id(1), pl.program_id(2)

        def product():
            av = a_ref[...]
            if a_fn is not None:
                av = a_fn(av.astype(F32))
            return lax.dot_general(av.astype(BF16), b_ref[...].astype(BF16), dims, preferred_element_type=F32)

        def finish(acc):
            res = (acc,) if epilogue is None else epilogue(acc, *[r[...] for r in ex_refs])
            for r, o in zip(out_refs, res[:n_out]):
                r[...] = o.astype(r.dtype)
            for r, cval in zip(cs_refs, res[n_out:]):
                _accumulate(r, jnp.sum(cval, axis=0, keepdims=True), i == 0)

        if nk == 1:
            finish(product())
            return
        acc_ref = refs[-1]

        @pl.when(k == 0)
        def _():
            acc_ref[...] = jnp.zeros_like(acc_ref)

        acc_ref[...] += product()

        @pl.when(k == nk - 1)
        def _():
            finish(acc_ref[...])

    vmem = 2 * tm * tk * a.dtype.itemsize + 2 * tk * tn * b.dtype.itemsize + (1 + 2 * n_out + 2 * ne + 2) * tm * tn * 4
    outs = pl.pallas_call(
        body, name=name, grid=grid, in_specs=in_specs, out_specs=out_specs, out_shape=out_shape,
        scratch_shapes=[pltpu.VMEM((tm, tn), F32)] if nk > 1 else [],
        compiler_params=_params(("parallel", "arbitrary", "arbitrary"), vmem),
    )(a, b, *[e[0] for e in extras], *([after] if n_after else []))
    return outs[0] if len(outs) == 1 else outs


BD_STEP = 4

def _bd(pairs, *, T, J, kb, nb, tw=False, extras=(), epilogue=None, n_out=1, n_cs=0, out_dtypes=None, tm=512, name):
    jb = BD_STEP
    assert T % tm == 0 and J % jb == 0
    grid = (J // jb, T // tm)
    npair, ne = len(pairs), len(extras)
    in_specs, args = [], []
    for arr, off, w in pairs:
        assert off % jb == 0, name
        in_specs.append(pl.BlockSpec((tm, jb * kb), lambda j, i, off=off // jb: (i, off + j)))
        in_specs.append(pl.BlockSpec((jb,) + tuple(w.shape[1:]), lambda j, i: (j, 0, 0)))
        args += [arr, w]
    for arr, kind, off in extras:
        assert off % jb == 0, name
        in_specs.append(pl.BlockSpec((tm, jb * nb), lambda j, i, off=off // jb: (i, off + j)) if kind == "tile"
                        else pl.BlockSpec((1, jb * nb), lambda j, i, off=off // jb: (0, off + j)))
        args.append(arr)
    out_dtypes = (F32,) * n_out if out_dtypes is None else out_dtypes
    out_shape = [jax.ShapeDtypeStruct((T, J * nb), dt) for dt in out_dtypes]
    out_specs = [pl.BlockSpec((tm, jb * nb), lambda j, i: (i, j)) for _ in range(n_out)]
    out_shape += [jax.ShapeDtypeStruct((1, J * nb), F32) for _ in range(n_cs)]
    out_specs += [pl.BlockSpec((1, jb * nb), lambda j, i: (0, j)) for _ in range(n_cs)]
    dims = (((1,), (1 if tw else 0,)), ((), ()))

    def body(*refs):
        ex_refs = refs[2 * npair:2 * npair + ne]
        out_refs = refs[2 * npair + ne:2 * npair + ne + n_out]
        cs_refs = refs[2 * npair + ne + n_out:]
        i = pl.program_id(1)
        for s in range(jb):
            cols_in, cols_out = pl.ds(s * kb, kb), pl.ds(s * nb, nb)
            acc = None
            for p in range(npair):
                d = lax.dot_general(refs[2 * p][:, cols_in].astype(BF16), refs[2 * p + 1][s].astype(BF16), dims,
                                    preferred_element_type=F32)
                acc = d if acc is None else acc + d
            res = (acc,) if epilogue is None else epilogue(acc, *[r[:, cols_out] for r in ex_refs])
            for r, o in zip(out_refs, res[:n_out]):
                r[:, cols_out] = o.astype(r.dtype)
            for r, cval in zip(cs_refs, res[n_out:]):
                _accumulate(r.at[:, cols_out], jnp.sum(cval, axis=0, keepdims=True), i == 0)

    vmem = jb * (2 * npair * tm * kb + 2 * npair * kb * nb + (2 * n_out + 2 * ne + 3) * tm * nb) * 4
    outs = pl.pallas_call(
        body, name=name, grid=grid, in_specs=in_specs, out_specs=out_specs, out_shape=out_shape,
        compiler_params=_params(("parallel", "arbitrary"), vmem),
    )(*args)
    return outs[0] if len(outs) == 1 else outs


def _bdw(a, a_off, b, b_off, *, T, J, kb, nb, tm=512, name):
    jb = BD_STEP
    assert T % tm == 0 and J % jb == 0 and a_off % jb == 0 and b_off % jb == 0
    a_blk, b_blk = a_off // jb, b_off // jb

    def body(a_ref, b_ref, o_ref):
        i = pl.program_id(1)
        for s in range(jb):
            d = lax.dot_general(a_ref[:, pl.ds(s * kb, kb)].astype(BF16), b_ref[:, pl.ds(s * nb, nb)].astype(BF16),
                                (((0,), (0,)), ((), ())), preferred_element_type=F32)
            _accumulate(o_ref.at[s], d, i == 0)

    return pl.pallas_call(
        body, name=name, grid=(J // jb, T // tm),
        in_specs=[pl.BlockSpec((tm, jb * kb), lambda j, i: (i, a_blk + j)),
                  pl.BlockSpec((tm, jb * nb), lambda j, i: (i, b_blk + j))],
        out_specs=pl.BlockSpec((jb, kb, nb), lambda j, i: (j, 0, 0)),
        out_shape=jax.ShapeDtypeStruct((J, kb, nb), F32),
        compiler_params=_params(("parallel", "arbitrary"), jb * (2 * tm * (kb + nb) + 3 * kb * nb) * 4),
    )(a, b)


def _bd_pack(w, q):
    g, a, b = w.shape
    eye = jnp.eye(q, dtype=w.dtype)
    return jnp.einsum("jqab,qr->jqarb", w.reshape(g // q, q, a, b), eye).reshape(g // q, q * a, q * b)


def _bd_unpack(wp, q):
    j, qa, qb = wp.shape
    a, b = qa // q, qb // q
    w5 = wp.reshape(j, q, a, q, b)
    return jnp.stack([w5[:, r, :, r, :] for r in range(q)], axis=1).reshape(j * q, a, b)


def _ew(fn, ins, *, T, C, n_out, n_cs=0, out_dtypes=None, tm=256, cw=None, name):
    cw = C if cw is None else cw
    assert T % tm == 0 and C % cw == 0
    grid = (C // cw, T // tm)
    in_specs = []
    for arr, kind, off in ins:
        in_specs.append(pl.BlockSpec((tm, cw), lambda j, i, off=off: (i, off + j)) if kind == "tile"
                        else pl.BlockSpec((arr.shape[0], cw), lambda j, i, off=off: (0, off + j)))
    out_dtypes = (F32,) * n_out if out_dtypes is None else out_dtypes
    out_shape = [jax.ShapeDtypeStruct((T, C), dt) for dt in out_dtypes]
    out_specs = [pl.BlockSpec((tm, cw), lambda j, i: (i, j)) for _ in range(n_out)]
    out_shape += [jax.ShapeDtypeStruct((1, C), F32) for _ in range(n_cs)]
    out_specs += [pl.BlockSpec((1, cw), lambda j, i: (0, j)) for _ in range(n_cs)]
    nin = len(ins)

    def body(*refs):
        i = pl.program_id(1)
        res = fn(*[r[...].astype(F32) for r in refs[:nin]])
        for r, o in zip(refs[nin:nin + n_out], res[:n_out]):
            r[...] = o.astype(r.dtype)
        for r, cval in zip(refs[nin + n_out:], res[n_out:]):
            _accumulate(r, jnp.sum(cval, axis=0, keepdims=True), i == 0)

    vmem = (2 * nin + 2 * n_out + 6) * tm * cw * 4
    outs = pl.pallas_call(
        body, name=name, grid=grid, in_specs=in_specs, out_specs=out_specs, out_shape=out_shape,
        compiler_params=_params(("parallel", "arbitrary"), vmem),
    )(*[arr for arr, _, _ in ins])
    return outs[0] if len(outs) == 1 else outs


def _ln_stats(s):
    mu = jnp.mean(s, axis=-1, keepdims=True)
    d = s - mu
    var = jnp.mean(d * d, axis=-1, keepdims=True)
    rstd = lax.rsqrt(var + LN_EPS)
    return d * rstd, rstd


def _ln_bwd(dy, g, xhat, rstd):
    dxh = dy * g
    m1 = jnp.mean(dxh, axis=-1, keepdims=True)
    m2 = jnp.mean(dxh * xhat, axis=-1, keepdims=True)
    return rstd * (dxh - m1 - xhat * m2)


def _conv_fwd(z, conv_w, conv_b, *, T, C, tm=512, cw=1024, after=None, name):
    ng, hb = tm // SUBLANES, tm // SUBLANES
    n_after = 0 if after is None else 1

    def body(x_ref, halo_ref, w_ref, b_ref, *rest):
        o_ref = rest[-1]
        it = pl.program_id(1)
        rows = _rows8(cw)
        halo = jnp.where(it == 0, 0.0, halo_ref[...])
        w = w_ref[...]
        bias = b_ref[...]

        def group(g, carry):
            off = pl.multiple_of(g * SUBLANES, SUBLANES)
            cur = x_ref[pl.ds(off, SUBLANES), :]
            prev = x_ref[pl.ds(pl.multiple_of(jnp.maximum(off - SUBLANES, 0), SUBLANES), SUBLANES), :]
            prev = jnp.where(g == 0, halo, prev)
            acc = cur * w[3:4] + bias
            for s in (1, 2, 3):
                acc = acc + _shift_down(cur, prev, s, rows) * w[3 - s:4 - s]
            o_ref[pl.ds(off, SUBLANES), :] = acc
            return carry

        lax.fori_loop(0, ng, group, 0, unroll=2)

    return pl.pallas_call(
        body, name=name, grid=(C // cw, T // tm),
        in_specs=[pl.BlockSpec((tm, cw), lambda j, i: (i, j)),
                  pl.BlockSpec((SUBLANES, cw), lambda j, i: (jnp.maximum(i * hb - 1, 0), j)),
                  pl.BlockSpec((4, cw), lambda j, i: (0, j)), pl.BlockSpec((1, cw), lambda j, i: (0, j))]
        + [ANY] * n_after,
        out_specs=pl.BlockSpec((tm, cw), lambda j, i: (i, j)),
        out_shape=jax.ShapeDtypeStruct((T, C), F32),
        compiler_params=_params(("parallel", "arbitrary"), 5 * tm * cw * 4),
    )(z, z, conv_w, conv_b, *([after] if n_after else []))


def _conv_bwd(dxc, z, conv_w, *, T, C, tm=512, cw=512, name):
    ng, hb, last = tm // SUBLANES, tm // SUBLANES, T // SUBLANES - 1
    nt = T // tm

    def body(d_ref, dn_ref, x_ref, w_ref, o_ref, sums_ref):
        it = pl.program_id(1)
        rows = _rows8(cw)
        dnext = jnp.where(it == nt - 1, 0.0, dn_ref[...])
        w = w_ref[...]

        def group(g, accs):
            off = pl.multiple_of(g * SUBLANES, SUBLANES)
            dcur = d_ref[pl.ds(off, SUBLANES), :]
            dnx = d_ref[pl.ds(pl.multiple_of(jnp.minimum(off + SUBLANES, tm - SUBLANES), SUBLANES), SUBLANES), :]
            dnx = jnp.where(g == ng - 1, dnext, dnx)
            xcur = x_ref[pl.ds(off, SUBLANES), :]
            acc = dcur * w[3:4]
            taps = [accs[3] + dcur * xcur]
            for s in (1, 2, 3):
                ahead = _shift_up(dcur, dnx, s, rows)
                acc = acc + ahead * w[3 - s:4 - s]
                taps.append(accs[3 - s] + ahead * xcur)
            o_ref[pl.ds(off, SUBLANES), :] = acc
            return taps[3], taps[2], taps[1], taps[0], accs[4] + dcur

        zero = jnp.zeros((SUBLANES, cw), F32)
        accs = lax.fori_loop(0, ng, group, (zero,) * 5, unroll=2)
        sums = jnp.zeros((SUBLANES, cw), F32)
        for k, a in enumerate(accs):
            sums = jnp.where(rows == k, jnp.sum(a, axis=0, keepdims=True), sums)
        _accumulate(sums_ref, sums, it == 0)

    tile = pl.BlockSpec((tm, cw), lambda j, i: (i, j))
    return pl.pallas_call(
        body, name=name, grid=(C // cw, nt),
        in_specs=[tile, pl.BlockSpec((SUBLANES, cw), lambda j, i: (jnp.minimum((i + 1) * hb, last), j)),
                  tile, pl.BlockSpec((4, cw), lambda j, i: (0, j))],
        out_specs=[tile, pl.BlockSpec((SUBLANES, cw), lambda j, i: (0, j))],
        out_shape=[jax.ShapeDtypeStruct((T, C), F32), jax.ShapeDtypeStruct((SUBLANES, C), F32)],
        compiler_params=_params(("parallel", "arbitrary"), 7 * tm * cw * 4),
    )(dxc, dxc, z, conv_w)


def _rg_coeffs(r, ig, xc, sp):
    la = (-RG_C) * r * sp
    a = jnp.exp(la)
    m = jnp.sqrt(_one_minus_exp(2.0 * la, a))
    return a, m, m * (ig * xc)


def _rg_scan_fwd(z, ri, xc, sp, *, T, C, gate_off, tm=512, cw=256, name):
    rows16 = 2 * SUBLANES
    nq = tm // rows16

    def body(gate_ref, r_ref, i_ref, xc_ref, sp_ref, h_ref, p_ref, a_ref, m_ref, carry_ref):
        it = pl.program_id(1)

        @pl.when(it == 0)
        def _():
            carry_ref[...] = jnp.zeros_like(carry_ref)

        rows = _rows8(cw)
        sp_row = sp_ref[...]

        def pair(q, carry):
            base = pl.multiple_of(q * rows16, rows16)
            halves = []
            for half in range(2):
                sl = pl.ds(pl.multiple_of(base + half * SUBLANES, SUBLANES), SUBLANES)
                a, m, b = _rg_coeffs(r_ref[sl, :], i_ref[sl, :], xc_ref[sl, :], sp_row)
                a_ref[sl, :] = a
                m_ref[sl, :] = m
                for s in (1, 2, 4):
                    keep = rows >= s
                    sa = jnp.where(keep, pltpu.roll(a, s, 0), 1.0)
                    sb = jnp.where(keep, pltpu.roll(b, s, 0), 0.0)
                    b = b + a * sb
                    a = a * sa
                h = b + a * carry
                h_ref[sl, :] = h
                halves.append(h * _gelu(gate_ref[sl, :]))
                carry = h[SUBLANES - 1:SUBLANES, :]
            p_ref[pl.ds(base, rows16), :] = jnp.concatenate(halves, axis=0).astype(p_ref.dtype)
            return carry

        last = lax.fori_loop(0, nq, pair, carry_ref[0:1, :], unroll=2)
        carry_ref[...] = jnp.broadcast_to(last, carry_ref.shape)

    tile = pl.BlockSpec((tm, cw), lambda j, i: (i, j))
    gate_blk = gate_off // cw
    return pl.pallas_call(
        body, name=name, grid=(C // cw, T // tm),
        in_specs=[pl.BlockSpec((tm, cw), lambda j, i: (i, gate_blk + j)),
                  pl.BlockSpec((tm, cw), lambda j, i: (i, 2 * j)), pl.BlockSpec((tm, cw), lambda j, i: (i, 2 * j + 1)),
                  tile, pl.BlockSpec((1, cw), lambda j, i: (0, j))],
        out_specs=[tile, tile, tile, tile],
        out_shape=[jax.ShapeDtypeStruct((T, C), F32), jax.ShapeDtypeStruct((T, C), BF16),
                   jax.ShapeDtypeStruct((T, C), F32), jax.ShapeDtypeStruct((T, C), F32)],
        scratch_shapes=[pltpu.VMEM((SUBLANES, cw), F32)],
        compiler_params=_params(("parallel", "arbitrary"), 16 * tm * cw * 4),
    )(z, ri, ri, xc, sp)


def _rg_scan_bwd(dh, h, ri, xc, a_fwd, m_fwd, sp, *, T, C, tm=512, cw=256, name):
    ng, hb, nt = tm // SUBLANES, tm // SUBLANES, T // tm

    def body(dh_ref, h_ref, hp_ref, r_ref, i_ref, xc_ref, a_ref, m_ref, sp_ref,
             drai_ref, dxc_ref, crai_ref, csp_ref, cg_ref, ca_ref):
        step = pl.program_id(1)

        @pl.when(step == 0)
        def _():
            cg_ref[...] = jnp.zeros_like(cg_ref)
            ca_ref[...] = jnp.zeros_like(ca_ref)

        rows = _rows8(cw)
        sp_row = sp_ref[...]
        hhalo = jnp.where(step == nt - 1, 0.0, hp_ref[...])

        def group(gi, carry):
            g_next, a_next, s_ra, s_ia, s_sp = carry
            g = ng - 1 - gi
            off = pl.multiple_of(g * SUBLANES, SUBLANES)
            sl = pl.ds(off, SUBLANES)
            rr, ii, xx = r_ref[sl, :], i_ref[sl, :], xc_ref[sl, :]
            a, m = a_ref[sl, :], m_ref[sl, :]
            hh = h_ref[sl, :]
            hpv = h_ref[pl.ds(pl.multiple_of(jnp.maximum(off - SUBLANES, 0), SUBLANES), SUBLANES), :]
            hpv = jnp.where(g == 0, hhalo, hpv)
            hprev = _shift_down(hh, hpv, 1, rows)
            d = dh_ref[sl, :]
            c = jnp.where(rows < SUBLANES - 1, pltpu.roll(a, SUBLANES - 1, 0), a_next)
            for s in (1, 2, 4):
                keep = rows < SUBLANES - s
                sc = jnp.where(keep, pltpu.roll(c, SUBLANES - s, 0), 1.0)
                sd = jnp.where(keep, pltpu.roll(d, SUBLANES - s, 0), 0.0)
                d = d + c * sd
                c = c * sc
            gg = d + c * g_next
            da = gg * hprev
            dm = gg * (ii * xx)
            di = gg * (m * xx)
            dxc_ref[sl, :] = gg * (m * ii)
            dla = da * a - dm * (a * a / m)
            dra = dla * ((-RG_C) * sp_row) * (rr * (1.0 - rr))
            dia = di * (ii * (1.0 - ii))
            drai_ref[sl, pl.ds(0, cw)] = dra
            drai_ref[sl, pl.ds(cw, cw)] = dia
            return (gg[0:1, :], a[0:1, :], s_ra + dra, s_ia + dia, s_sp + dla * ((-RG_C) * rr))

        zero = jnp.zeros((SUBLANES, cw), F32)
        g_first, a_first, s_ra, s_ia, s_sp = lax.fori_loop(
            0, ng, group, (cg_ref[0:1, :], ca_ref[0:1, :], zero, zero, zero), unroll=2)
        cg_ref[...] = jnp.broadcast_to(g_first, cg_ref.shape)
        ca_ref[...] = jnp.broadcast_to(a_first, ca_ref.shape)
        for ref, acc in ((crai_ref.at[:, pl.ds(0, cw)], s_ra), (crai_ref.at[:, pl.ds(cw, cw)], s_ia), (csp_ref, s_sp)):
            _accumulate(ref, jnp.sum(acc, axis=0, keepdims=True), step == 0)

    tile = pl.BlockSpec((tm, cw), lambda j, i: (nt - 1 - i, j))
    wide = pl.BlockSpec((tm, 2 * cw), lambda j, i: (nt - 1 - i, j))
    vec = pl.BlockSpec((1, cw), lambda j, i: (0, j))
    return pl.pallas_call(
        body, name=name, grid=(C // cw, nt),
        in_specs=[tile, tile, pl.BlockSpec((SUBLANES, cw), lambda j, i: (jnp.maximum((nt - 1 - i) * hb - 1, 0), j)),
                  pl.BlockSpec((tm, cw), lambda j, i: (nt - 1 - i, 2 * j)),
                  pl.BlockSpec((tm, cw), lambda j, i: (nt - 1 - i, 2 * j + 1)), tile, tile, tile, vec],
        out_specs=[wide, tile, pl.BlockSpec((1, 2 * cw), lambda j, i: (0, j)), vec],
        out_shape=[jax.ShapeDtypeStruct((T, 2 * C), F32), jax.ShapeDtypeStruct((T, C), F32),
                   jax.ShapeDtypeStruct((1, 2 * C), F32), jax.ShapeDtypeStruct((1, C), F32)],
        scratch_shapes=[pltpu.VMEM((SUBLANES, cw), F32), pltpu.VMEM((SUBLANES, cw), F32)],
        compiler_params=_params(("parallel", "arbitrary"), 24 * tm * cw * 4),
    )(dh, h, h, ri, ri, xc, a_fwd, m_fwd, sp)


def _cscan_tables(lr, li, reverse):
    lam = (lr.reshape(-1), -li.reshape(-1) if reverse else li.reshape(-1))

    def mul(p, q):
        return p[0] * q[0] - p[1] * q[1], p[0] * q[1] + p[1] * q[0]

    pows = [lam]
    for _ in range(SUBLANES - 1):
        pows.append(mul(pows[-1], lam))
    zero = jnp.zeros_like(lam[0])
    tab = jnp.stack([pows[0][0], pows[0][1], pows[1][0], pows[1][1], pows[3][0], pows[3][1], zero, zero])
    if reverse:
        pows = pows[::-1]
    return tab, jnp.stack([p[0] for p in pows]), jnp.stack([p[1] for p in pows])


def _power_slabs(lr, li, n):
    pr, pi = lr.reshape(1, -1), li.reshape(1, -1)
    while pr.shape[0] < n:
        tr, ti = pr[-1:], pi[-1:]
        pr, pi = (jnp.concatenate([pr, pr * tr - pi * ti], axis=0), jnp.concatenate([pi, pr * ti + pi * tr], axis=0))
    return jnp.repeat(pr, SUBLANES, axis=0), jnp.repeat(pi, SUBLANES, axis=0), pr[-1], pi[-1]


def _rows_to_segments(src_ref, dst_ref):
    seg = src_ref.shape[0] // SUBLANES
    for g in range(seg):
        dst_ref[pl.ds(g * SUBLANES, SUBLANES), :] = src_ref[pl.ds(g, SUBLANES, stride=seg), :].astype(dst_ref.dtype)


def _segments_to_rows(src_ref, dst_ref):
    seg = src_ref.shape[0] // SUBLANES
    for r in range(SUBLANES):
        dst_ref[pl.ds(r * seg, seg), :] = src_ref[pl.ds(r, seg, stride=SUBLANES), :].astype(dst_ref.dtype)


def _seg_scan_tile(xr_ref, xi_ref, pbr_ref, pbi_ref, tab_ref, pwr_ref, pwi_ref, cr_ref, ci_ref, *, reverse, h=None):
    tm, cw = xr_ref.shape
    seg = tm // SUBLANES
    rows = _rows8(cw)
    sign = -1.0 if reverse else 1.0
    l_re, l_im = pbr_ref[0:1, :], sign * pbi_ref[0:1, :]

    def slab(g):
        return pl.ds(pl.multiple_of(g * SUBLANES, SUBLANES), SUBLANES)

    def local(k, state):
        sl = slab(seg - 1 - k if reverse else k)
        sr, si = state
        nr = xr_ref[sl, :] + (l_re * sr - l_im * si)
        ni = xi_ref[sl, :] + (l_re * si + l_im * sr)
        xr_ref[sl, :] = nr
        xi_ref[sl, :] = ni
        return nr, ni

    zero = jnp.zeros((SUBLANES, cw), F32)
    er, ei = lax.fori_loop(0, seg, local, (zero, zero), unroll=2)

    for k, s in enumerate((1, 2, 4)):
        shift = SUBLANES - s if reverse else s
        keep = rows < SUBLANES - s if reverse else rows >= s
        sr = jnp.where(keep, pltpu.roll(er, shift, 0), 0.0)
        si = jnp.where(keep, pltpu.roll(ei, shift, 0), 0.0)
        m_re, m_im = tab_ref[2 * k:2 * k + 1, :], tab_ref[2 * k + 1:2 * k + 2, :]
        er, ei = er + (m_re * sr - m_im * si), ei + (m_re * si + m_im * sr)
    cin_r, cin_i = cr_ref[0:1, :], ci_ref[0:1, :]
    pwr, pwi = pwr_ref[...], pwi_ref[...]
    er, ei = er + (pwr * cin_r - pwi * cin_i), ei + (pwr * cin_i + pwi * cin_r)
    if reverse:
        ent_r = jnp.where(rows == SUBLANES - 1, cin_r, pltpu.roll(er, SUBLANES - 1, 0))
        ent_i = jnp.where(rows == SUBLANES - 1, cin_i, pltpu.roll(ei, SUBLANES - 1, 0))
        out_r, out_i = er[0:1, :], ei[0:1, :]
    else:
        ent_r = jnp.where(rows == 0, cin_r, pltpu.roll(er, 1, 0))
        ent_i = jnp.where(rows == 0, cin_i, pltpu.roll(ei, 1, 0))
        out_r, out_i = er[SUBLANES - 1:SUBLANES, :], ei[SUBLANES - 1:SUBLANES, :]
    cr_ref[...] = jnp.broadcast_to(out_r, cr_ref.shape)
    ci_ref[...] = jnp.broadcast_to(out_i, ci_ref.shape)

    if h is not None:
        hr_ref, hi_ref, hr_last, hi_last = h
        hr_wrap = _shift_down(hr_ref[pl.ds(tm - SUBLANES, SUBLANES), :], hr_last, 1, rows)
        hi_wrap = _shift_down(hi_ref[pl.ds(tm - SUBLANES, SUBLANES), :], hi_last, 1, rows)

    def fix(g, sums):
        sl = slab(g)
        power = slab(seg - 1 - g) if reverse else sl
        pr, pi = pbr_ref[power, :], sign * pbi_ref[power, :]
        nr = xr_ref[sl, :] + (pr * ent_r - pi * ent_i)
        ni = xi_ref[sl, :] + (pr * ent_i + pi * ent_r)
        xr_ref[sl, :] = nr
        xi_ref[sl, :] = ni
        if h is None:
            return sums
        before = slab(jnp.maximum(g - 1, 0))
        hr1 = jnp.where(g == 0, hr_wrap, hr_ref[before, :])
        hi1 = jnp.where(g == 0, hi_wrap, hi_ref[before, :])
        return sums[0] + (nr * hr1 + ni * hi1), sums[1] + (ni * hr1 - nr * hi1)

    return lax.fori_loop(0, seg, fix, (zero, zero) if h is not None else (), unroll=2)


S5_TILE = 512


def _s5_fwd(z, u_off, wb_re, wb_im, wc_re, wc_im_neg, d_row, powers, *, T, tm=S5_TILE, name):
    J, ku, kp = wb_re.shape
    nt = T // tm
    pb_re, pb_im, top_re, top_im = powers
    tab, pw_re, pw_im = _cscan_tables(top_re, top_im, False)
    u_blk = u_off // ku

    def body(u_ref, wbr_ref, wbi_ref, wcr_ref, wci_ref, d_ref, pbr_ref, pbi_ref, tab_ref, pwr_ref, pwi_ref,
             hr_ref, hi_ref, y_ref, yg_ref, cr_ref, ci_ref, us_ref, ys_ref):
        @pl.when(pl.program_id(1) == 0)
        def _():
            cr_ref[...] = jnp.zeros_like(cr_ref)
            ci_ref[...] = jnp.zeros_like(ci_ref)

        _rows_to_segments(u_ref, us_ref)
        u = us_ref[...]
        ub = u.astype(BF16)
        hr_ref[...] = jnp.dot(ub, wbr_ref[...], preferred_element_type=F32)
        hi_ref[...] = jnp.dot(ub, wbi_ref[...], preferred_element_type=F32)
        _seg_scan_tile(hr_ref, hi_ref, pbr_ref, pbi_ref, tab_ref, pwr_ref, pwi_ref, cr_ref, ci_ref, reverse=False)
        y = (jnp.dot(hr_ref[...].astype(BF16), wcr_ref[...], preferred_element_type=F32)
             + jnp.dot(hi_ref[...].astype(BF16), wci_ref[...], preferred_element_type=F32) + d_ref[...] * u)
        ys_ref[...] = y
        _segments_to_rows(ys_ref, y_ref)
        ys_ref[...] = _gelu(y)
        _segments_to_rows(ys_ref, yg_ref)

    wb_spec = pl.BlockSpec((None, ku, kp), lambda j, i: (j, 0, 0))
    wc_spec = pl.BlockSpec((None, kp, ku), lambda j, i: (j, 0, 0))
    small = pl.BlockSpec((SUBLANES, kp), lambda j, i: (0, j))
    slabs = pl.BlockSpec((tm, kp), lambda j, i: (0, j))
    state = pl.BlockSpec((tm, kp), lambda j, i: (i, j))
    chan = pl.BlockSpec((tm, ku), lambda j, i: (i, j))
    return pl.pallas_call(
        body, name=name, grid=(J, nt),
        in_specs=[pl.BlockSpec((tm, ku), lambda j, i: (i, u_blk + j)), wb_spec, wb_spec, wc_spec, wc_spec,
                  pl.BlockSpec((1, ku), lambda j, i: (0, j)), slabs, slabs, small, small, small],
        out_specs=[state, state, chan, chan],
        out_shape=[jax.ShapeDtypeStruct((T, J * kp), F32)] * 2
        + [jax.ShapeDtypeStruct((T, J * ku), F32), jax.ShapeDtypeStruct((T, J * ku), BF16)],
        scratch_shapes=[pltpu.VMEM((SUBLANES, kp), F32), pltpu.VMEM((SUBLANES, kp), F32),
                        pltpu.VMEM((tm, ku), F32), pltpu.VMEM((tm, ku), F32)],
        compiler_params=_params(("parallel", "arbitrary"), 14 * tm * kp * 4),
    )(z, wb_re, wb_im, wc_re, wc_im_neg, d_row, pb_re, pb_im, tab, pw_re, pw_im)


def _s5_bwd(dy, z, u_off, h_re, h_im, wb_re, wb_im, wc_re, wc_im_neg, d_row, powers, *, T, tm=S5_TILE, name):
    J, ku, kp = wb_re.shape
    nt, hb = T // tm, tm // SUBLANES
    pb_re, pb_im, top_re, top_im = powers
    tab, pw_re, pw_im = _cscan_tables(top_re, top_im, True)
    u_blk = u_off // ku
    contract_rows = (((0,), (0,)), ((), ()))
    contract_cols = (((1,), (1,)), ((), ()))

    def body(dy_ref, u_ref, hr_ref, hrp_ref, hi_ref, hip_ref, wbr_ref, wbi_ref, wcr_ref, wci_ref, d_ref,
             pbr_ref, pbi_ref, tab_ref, pwr_ref, pwi_ref,
             du_ref, dlr_ref, dli_ref, dd_ref, dwbr_ref, dwbi_ref, dwcr_ref, dwci_ref,
             gr_ref, gi_ref, cr_ref, ci_ref, dys_ref, us_ref):
        step = pl.program_id(1)
        first = step == 0

        @pl.when(first)
        def _():
            cr_ref[...] = jnp.zeros_like(cr_ref)
            ci_ref[...] = jnp.zeros_like(ci_ref)

        _rows_to_segments(dy_ref, dys_ref)
        _rows_to_segments(u_ref, us_ref)
        dy_t, u = dys_ref[...], us_ref[...]
        dyb, ub = dy_t.astype(BF16), u.astype(BF16)
        gr_ref[...] = lax.dot_general(dyb, wcr_ref[...], contract_cols, preferred_element_type=F32)
        gi_ref[...] = lax.dot_general(dyb, wci_ref[...], contract_cols, preferred_element_type=F32)
        hr_last = jnp.where(step == nt - 1, 0.0, hrp_ref[...])
        hi_last = jnp.where(step == nt - 1, 0.0, hip_ref[...])
        s_re, s_im = _seg_scan_tile(gr_ref, gi_ref, pbr_ref, pbi_ref, tab_ref, pwr_ref, pwi_ref, cr_ref, ci_ref,
                                    reverse=True, h=(hr_ref, hi_ref, hr_last, hi_last))
        _accumulate(dlr_ref, jnp.sum(s_re, axis=0, keepdims=True), first)
        _accumulate(dli_ref, jnp.sum(s_im, axis=0, keepdims=True), first)
        grb, gib = gr_ref[...].astype(BF16), gi_ref[...].astype(BF16)
        du = (lax.dot_general(grb, wbr_ref[...], contract_cols, preferred_element_type=F32)
              + lax.dot_general(gib, wbi_ref[...], contract_cols, preferred_element_type=F32) + dy_t * d_ref[...])
        dys_ref[...] = du
        _segments_to_rows(dys_ref, du_ref)
        _accumulate(dd_ref, jnp.sum(dy_t * u, axis=0, keepdims=True), first)
        _accumulate(dwbr_ref, lax.dot_general(ub, grb, contract_rows, preferred_element_type=F32), first)
        _accumulate(dwbi_ref, lax.dot_general(ub, gib, contract_rows, preferred_element_type=F32), first)
        _accumulate(dwcr_ref, lax.dot_general(hr_ref[...].astype(BF16), dyb, contract_rows,
                                              preferred_element_type=F32), first)
        _accumulate(dwci_ref, lax.dot_general(hi_ref[...].astype(BF16), dyb, contract_rows,
                                              preferred_element_type=F32), first)

    def tix(i):
        return nt - 1 - i

    wb_spec = pl.BlockSpec((None, ku, kp), lambda j, i: (j, 0, 0))
    wc_spec = pl.BlockSpec((None, kp, ku), lambda j, i: (j, 0, 0))
    small = pl.BlockSpec((SUBLANES, kp), lambda j, i: (0, j))
    state = pl.BlockSpec((tm, kp), lambda j, i: (tix(i), j))
    halo = pl.BlockSpec((SUBLANES, kp), lambda j, i: (jnp.maximum(tix(i) * hb - 1, 0), j))
    chan = pl.BlockSpec((tm, ku), lambda j, i: (tix(i), j))
    svec = pl.BlockSpec((1, kp), lambda j, i: (0, j))
    cvec = pl.BlockSpec((1, ku), lambda j, i: (0, j))
    slabs = pl.BlockSpec((tm, kp), lambda j, i: (0, j))
    return pl.pallas_call(
        body, name=name, grid=(J, nt),
        in_specs=[chan, pl.BlockSpec((tm, ku), lambda j, i: (tix(i), u_blk + j)), state, halo, state, halo,
                  wb_spec, wb_spec, wc_spec, wc_spec, cvec, slabs, slabs, small, small, small],
        out_specs=[chan, svec, svec, cvec, wb_spec, wb_spec, wc_spec, wc_spec],
        out_shape=[jax.ShapeDtypeStruct((T, J * ku), BF16), jax.ShapeDtypeStruct((1, J * kp), F32),
                   jax.ShapeDtypeStruct((1, J * kp), F32), jax.ShapeDtypeStruct((1, J * ku), F32),
                   jax.ShapeDtypeStruct((J, ku, kp), F32), jax.ShapeDtypeStruct((J, ku, kp), F32),
                   jax.ShapeDtypeStruct((J, kp, ku), F32), jax.ShapeDtypeStruct((J, kp, ku), F32)],
        scratch_shapes=[pltpu.VMEM((tm, kp), F32), pltpu.VMEM((tm, kp), F32),
                        pltpu.VMEM((SUBLANES, kp), F32), pltpu.VMEM((SUBLANES, kp), F32),
                        pltpu.VMEM((tm, ku), F32), pltpu.VMEM((tm, ku), F32)],
        compiler_params=_params(("parallel", "arbitrary"), 16 * tm * kp * 4),
    )(dy, z, h_re, h_re, h_im, h_im, wb_re, wb_im, wc_re, wc_im_neg, d_row, pb_re, pb_im, tab, pw_re, pw_im)


def _mesh_pos():
    return lax.axis_index("x"), lax.axis_index("y"), lax.axis_index("c")


def _dev_index(px, py, pc):
    return 4 * px + 2 * py + pc


def _all_gather(shards, name):
    n = len(shards)

    def body(*refs):
        ins, outs = refs[:n], refs[n:2 * n]
        send_sems, recv_sems, local_sems = refs[2 * n:]
        x, y, c = _mesh_pos()
        me, sibling = (x, y, c), (x, y, 1 - c)
        chips = [(1 - x, y), (x, 1 - y), (1 - x, 1 - y)]

        def copy(a, k, block, to, src=None):
            dst = outs[a].at[_dev_index(*block)]
            return pltpu.make_async_remote_copy(
                src_ref=dst if src is None else src, dst_ref=dst, send_sem=send_sems.at[a * 7 + k],
                recv_sem=recv_sems.at[a * 7 + k], device_id=to, device_id_type=MESH)

        mine = [pltpu.make_async_copy(ins[a], outs[a].at[_dev_index(*me)], local_sems.at[a]) for a in range(n)]
        for cp in mine:
            cp.start()
        first = []
        for a in range(n):
            first.append(copy(a, 0, me, sibling, src=ins[a]))
            first += [copy(a, 1 + j, me, (*chip, c), src=ins[a]) for j, chip in enumerate(chips)]
        for cp in first:
            cp.start()
        passed = []
        for j, chip in enumerate(chips):
            for a in range(n):
                copy(a, 1 + j, (*chip, c), me).wait_recv()
                fwd = copy(a, 4 + j, (*chip, c), sibling)
                fwd.start()
                passed.append(fwd)
        for a in range(n):
            copy(a, 0, sibling, me).wait_recv()
            for j, chip in enumerate(chips):
                copy(a, 4 + j, (*chip, 1 - c), me).wait_recv()
        for cp in first + passed:
            cp.wait_send()
        for cp in mine:
            cp.wait()

    return pl.pallas_call(
        body, name=name, in_specs=[ANY] * n, out_specs=[ANY] * n,
        out_shape=[jax.ShapeDtypeStruct((N_DEV,) + s.shape, s.dtype) for s in shards],
        scratch_shapes=[pltpu.SemaphoreType.DMA((7 * n,)), pltpu.SemaphoreType.DMA((7 * n,)),
                        pltpu.SemaphoreType.DMA((n,))],
    )(*shards)


def _exchange_blocks(parts, name):
    n = len(parts)
    relations = [(dx, dy, dc) for dx in (0, 1) for dy in (0, 1) for dc in (0, 1) if (dx, dy, dc) != (0, 0, 0)]

    def body(*refs):
        ins, outs = refs[:n], refs[n:2 * n]
        send_sems, recv_sems, local_sems = refs[2 * n:]
        x, y, c = _mesh_pos()
        me = _dev_index(x, y, c)
        mine = [pltpu.make_async_copy(ins[a].at[me], outs[a].at[me], local_sems.at[a]) for a in range(n)]
        for cp in mine:
            cp.start()
        copies = []
        for k, (dx, dy, dc) in enumerate(relations):
            peer = (x + dx - 2 * x * dx, y + dy - 2 * y * dy, c + dc - 2 * c * dc)
            for a in range(n):
                copies.append((pltpu.make_async_remote_copy(
                    src_ref=ins[a].at[_dev_index(*peer)], dst_ref=outs[a].at[me], send_sem=send_sems.at[a * 7 + k],
                    recv_sem=recv_sems.at[a * 7 + k], device_id=peer, device_id_type=MESH),
                    pltpu.make_async_remote_copy(
                    src_ref=ins[a].at[_dev_index(*peer)], dst_ref=outs[a].at[_dev_index(*peer)],
                    send_sem=send_sems.at[a * 7 + k], recv_sem=recv_sems.at[a * 7 + k], device_id=peer,
                    device_id_type=MESH)))
        for send, _ in copies:
            send.start()
        for _, recv in copies:
            recv.wait_recv()
        for send, _ in copies:
            send.wait_send()
        for cp in mine:
            cp.wait()

    return pl.pallas_call(
        body, name=name, in_specs=[ANY] * n, out_specs=[ANY] * n,
        out_shape=[jax.ShapeDtypeStruct(p.shape, p.dtype) for p in parts],
        scratch_shapes=[pltpu.SemaphoreType.DMA((7 * n,)), pltpu.SemaphoreType.DMA((7 * n,)),
                        pltpu.SemaphoreType.DMA((n,))],
    )(*parts)


HBM = pl.BlockSpec(memory_space=pltpu.HBM)
SEM = pl.BlockSpec(memory_space=pltpu.SEMAPHORE)
EFFECT = pltpu.SideEffectType.DATAFLOW_SIDE_EFFECTING
RELATIONS = [(dx, dy, dc) for dx in (0, 1) for dy in (0, 1) for dc in (0, 1) if (dx, dy, dc) != (0, 0, 0)]


def _peer(rel):
    x, y, c = _mesh_pos()
    dx, dy, dc = rel
    return (x + dx - 2 * x * dx, y + dy - 2 * y * dy, c + dc - 2 * c * dc)


def _split_copy(src_ref, land_ref, send_sems, recv_sems, k, scatter, incoming):
    peer = _peer(RELATIONS[k])
    me = _dev_index(*_mesh_pos())
    src = src_ref.at[_dev_index(*peer)] if scatter else src_ref
    dst = land_ref.at[_dev_index(*peer) if incoming else me]
    return pltpu.make_async_remote_copy(src_ref=src, dst_ref=dst, send_sem=send_sems.at[k], recv_sem=recv_sems.at[k],
                                        device_id=peer, device_id_type=MESH)


def _exchange_start(srcs, lands, *, scatter, after=None, name):
    n = len(srcs)
    n_after = 0 if after is None else 1

    def body(*refs):
        src_refs, land_refs = refs[:n], refs[n:2 * n]
        first_out = 2 * n + n_after
        send, recv = refs[first_out:first_out + n], refs[first_out + n:first_out + 2 * n]
        token = refs[-1]
        for k in range(len(RELATIONS)):
            for a in range(n):
                _split_copy(src_refs[a], land_refs[a], send[a], recv[a], k, scatter, incoming=False).start()
        token[...] = jnp.zeros_like(token)

    n_rel = len(RELATIONS)
    outs = pl.pallas_call(
        body, name=name, in_specs=[HBM] * (2 * n) + [ANY] * n_after,
        out_shape=[pltpu.SemaphoreType.DMA((n_rel,))] * (2 * n)
        + [pltpu.HBM(s.shape, s.dtype) for s in srcs] + [pltpu.HBM(s.shape, s.dtype) for s in lands]
        + [jax.ShapeDtypeStruct((SUBLANES, LANES), F32)],
        out_specs=[SEM] * (2 * n) + [HBM] * (2 * n) + [pl.BlockSpec(memory_space=pltpu.VMEM)],
        input_output_aliases={**{a: 2 * n + a for a in range(n)}, **{n + a: 3 * n + a for a in range(n)}},
        compiler_params=pltpu.CompilerParams(has_side_effects=EFFECT),
    )(*[pltpu.with_memory_space_constraint(s, pltpu.HBM) for s in srcs],
      *[pltpu.with_memory_space_constraint(s, pltpu.HBM) for s in lands], *([after] if n_after else []))
    per_array = [(outs[a], outs[n + a], outs[2 * n + a], outs[3 * n + a]) for a in range(n)]
    return per_array, outs[-1]


def _exchange_wait(handle, after, *, scatter, name):
    send_sems, recv_sems, src_thru, land_thru = handle

    def body(src_ref, land_ref, send, recv, after_ref, src_dead, got_ref):
        for k in range(len(RELATIONS)):
            cp = _split_copy(src_ref, land_ref, send, recv, k, scatter, incoming=True)
            cp.wait_send()
            cp.wait_recv()

    return pl.pallas_call(
        body, name=name, in_specs=[HBM, HBM, SEM, SEM, ANY],
        out_shape=[pltpu.HBM(src_thru.shape, src_thru.dtype), pltpu.HBM(land_thru.shape, land_thru.dtype)],
        out_specs=[HBM, HBM], input_output_aliases={0: 0, 1: 1},
        compiler_params=pltpu.CompilerParams(has_side_effects=EFFECT),
    )(src_thru, land_thru, send_sems, recv_sems, after)[1]


def _landing_zone(own_block):
    me = _dev_index(*_mesh_pos())
    zone = lax.empty((N_DEV,) + own_block.shape, own_block.dtype)
    return lax.dynamic_update_index_in_dim(zone, own_block, me, 0)


def _row_tile(rows, want):
    t = min(want, rows) // SUBLANES * SUBLANES
    while rows % t:
        t -= SUBLANES
    return t


def _sum_slots(recv, *, tr, name):
    s_, r_, c_ = recv.shape
    tr = _row_tile(r_, tr)

    def body(g_ref, o_ref):
        acc = g_ref[0]
        for s in range(1, s_):
            acc = acc + g_ref[s]
        o_ref[...] = acc

    return pl.pallas_call(
        body, name=name, grid=(r_ // tr,),
        in_specs=[pl.BlockSpec((s_, tr, c_), lambda i: (0, i, 0))],
        out_specs=pl.BlockSpec((tr, c_), lambda i: (i, 0)),
        out_shape=jax.ShapeDtypeStruct((r_, c_), F32),
        compiler_params=_params(("parallel",), (2 * s_ + 3) * tr * c_ * 4),
    )(recv)


def _adamw(recv, w, m, v, *, tr, name):
    s_, r_, c_ = recv.shape
    tr = _row_tile(r_, tr)
    assert w.shape == (r_, c_), (name, w.shape, recv.shape)
    c1 = 1.0 - ADAM_B1 ** ADAM_STEP
    c2 = 1.0 - ADAM_B2 ** ADAM_STEP

    def body(g_ref, w_ref, m_ref, v_ref, go_ref, d_ref, mo_ref, vo_ref):
        g = g_ref[0].astype(F32)
        for s in range(1, s_):
            g = g + g_ref[s].astype(F32)
        mn = ADAM_B1 * m_ref[...] + (1.0 - ADAM_B1) * g
        vn = ADAM_B2 * v_ref[...] + (1.0 - ADAM_B2) * (g * g)
        go_ref[...] = g
        mo_ref[...] = mn
        vo_ref[...] = vn
        d_ref[...] = -ADAM_LR * ((mn / c1) / (jnp.sqrt(vn / c2) + ADAM_EPS) + ADAM_WD * w_ref[...])

    tile = pl.BlockSpec((tr, c_), lambda i: (i, 0))
    return pl.pallas_call(
        body, name=name, grid=(r_ // tr,),
        in_specs=[pl.BlockSpec((s_, tr, c_), lambda i: (0, i, 0)), tile, tile, tile],
        out_specs=[tile] * 4, out_shape=[jax.ShapeDtypeStruct((r_, c_), F32)] * 4,
        compiler_params=_params(("parallel",), (2 * s_ + 16) * tr * c_ * 4),
    )(recv, w, m, v)


def _adamw_whole(gs, ws, ms, vs, *, name):
    n = len(gs)
    c1 = 1.0 - ADAM_B1 ** ADAM_STEP
    c2 = 1.0 - ADAM_B2 ** ADAM_STEP

    def body(*refs):
        for i in range(n):
            g, w = refs[i][...], refs[n + i][...]
            mn = ADAM_B1 * refs[2 * n + i][...] + (1.0 - ADAM_B1) * g
            vn = ADAM_B2 * refs[3 * n + i][...] + (1.0 - ADAM_B2) * (g * g)
            refs[4 * n + 3 * i][...] = -ADAM_LR * ((mn / c1) / (jnp.sqrt(vn / c2) + ADAM_EPS) + ADAM_WD * w)
            refs[4 * n + 3 * i + 1][...] = mn
            refs[4 * n + 3 * i + 2][...] = vn

    whole = pl.BlockSpec(memory_space=pltpu.VMEM)
    lane_padded = sum(math.prod(g.shape[:-1]) * (-(-g.shape[-1] // LANES) * LANES) for g in gs)
    outs = pl.pallas_call(
        body, name=name, in_specs=[whole] * (4 * n), out_specs=[whole] * (3 * n),
        out_shape=[jax.ShapeDtypeStruct(g.shape, F32) for g in gs for _ in range(3)],
        compiler_params=pltpu.CompilerParams(vmem_limit_bytes=int(min(max(16 * lane_padded * 4, 16 * 2 ** 20), VMEM_CAP))),
    )(*gs, *ws, *ms, *vs)
    return [tuple(outs[3 * i:3 * i + 3]) for i in range(n)]


def _s5_discretise(a_re, a_im, log_dt, b_re, b_im):
    dt = jnp.exp(log_dt)[:, None]
    lr = jnp.minimum(a_re, -1e-4)
    li = a_im
    mag = jnp.exp(lr * dt)
    lbr = mag * jnp.cos(li * dt)
    lbi = mag * jnp.sin(li * dt)
    zr, zi = lbr - 1.0, lbi
    den = lr * lr + li * li
    fr = (zr * lr + zi * li) / den
    fi = (zi * lr - zr * li) / den
    bbr = fr[..., None] * b_re - fi[..., None] * b_im
    bbi = fr[..., None] * b_im + fi[..., None] * b_re
    return lbr, lbi, bbr, bbi


def _softplus_neg(lam):
    return jnp.maximum(-lam, 0.0) + jnp.log(1.0 + jnp.exp(-jnp.abs(lam)))


S5_Q = 8
RG_Q = 2


def _local_step(x, tgt, W, comm):
    T, D = x.shape
    C = D
    G, P, H = W["ssm_b_re"].shape
    S = G * H
    F = W["mlp_b_up"].shape[1]
    n_in = 2 * C + S + 2 * D
    heads, hd = W["rg_wa"].shape[0], W["rg_wa"].shape[1]
    u_off, ga_off, gb_off = 2 * C, 2 * C + S, 2 * C + S + D

    sp, sp_vjp = jax.vjp(_softplus_neg, W["rg_lambda"])
    (lbr, lbi, bbr, bbi), s5_vjp = jax.vjp(_s5_discretise, W["ssm_a_re"], W["ssm_a_im"], W["ssm_log_dt"],
                                           W["ssm_b_re"], W["ssm_b_im"])
    lam_re, lam_im = lbr.reshape(-1), lbi.reshape(-1)
    jr, kr = heads // RG_Q, RG_Q * hd
    w_ri = jnp.concatenate([_bd_pack(W["rg_wa"], RG_Q), _bd_pack(W["rg_wx"], RG_Q)], axis=2).astype(BF16)
    b_ri = jnp.concatenate([W["rg_ba"].reshape(jr, kr), W["rg_bx"].reshape(jr, kr)], axis=1).reshape(1, -1)
    wb_re = _bd_pack(jnp.swapaxes(bbr, 1, 2), S5_Q).astype(BF16)
    wb_im = _bd_pack(jnp.swapaxes(bbi, 1, 2), S5_Q).astype(BF16)
    wc_re = _bd_pack(jnp.swapaxes(W["ssm_c_re"], 1, 2), S5_Q).astype(BF16)
    wc_im_neg = _bd_pack(jnp.swapaxes(-W["ssm_c_im"], 1, 2), S5_Q).astype(BF16)
    d_row = W["ssm_d"].reshape(1, S)

    x_bf = x.astype(BF16)
    w_in = comm.weight("w_in", None)
    z = _mm(x_bf, w_in, M=T, N=n_in, K=D, tm=512, tn=n_in // 4, tk=D, after=comm.gather_token, name="fwd_in_proj")
    started = comm.start_weights(("mlp_w_up",), z)
    xc = _conv_fwd(z, W["conv_w"], W["conv_b"], T=T, C=C, after=started, name="fwd_conv")
    ri = _bd([(xc, 0, w_ri)], T=T, J=jr, kb=kr, nb=2 * kr, extras=[(b_ri, "vec", 0)],
             epilogue=lambda acc, b: (_sig(acc + b),), name="fwd_gates")
    h, p, a_fwd, m_fwd = _rg_scan_fwd(z, ri, xc, sp, T=T, C=C, gate_off=C, cw=kr, name="fwd_rg_scan")
    w_a_out = comm.weight("w_a_out", p)
    started = comm.start_weights(("mlp_w_down",), p)
    y_a = _mm(p, w_a_out, M=T, N=D, K=C, out_dtypes=(BF16,), tm=512, tn=D, tk=C, after=started, name="fwd_rg_out")

    powers = _power_slabs(lam_re, lam_im, S5_TILE // SUBLANES)
    h_re, h_im, y_s, yg = _s5_fwd(z, u_off, wb_re, wb_im, wc_re, wc_im_neg, d_row, powers, T=T, name="fwd_s5")
    w_glu_w, w_glu_v = comm.weight("glu_w", yg), comm.weight("glu_v", yg)
    glu_a = _mm(yg, w_glu_w, M=T, N=D, K=S, out_dtypes=(BF16,), tm=1024, tn=D, tk=S, name="fwd_glu_w")
    cwm = 1024

    def mix_fn(b, ga, gb, ya, a):
        return b, _sig(ga) * ya.astype(F32) + _sig(gb) * (a.astype(F32) * _sig(b))

    glu_b, mix = _mm(yg, w_glu_v, M=T, N=D, K=S, tm=512, tn=cwm, tk=S,
                     extras=[(z, "mn", ga_off // cwm), (z, "mn", gb_off // cwm), (y_a, "mn"), (glu_a, "mn")],
                     epilogue=mix_fn, n_out=2, out_dtypes=(BF16, BF16), name="fwd_glu_v_mix")
    w_out = comm.weight("w_out", mix)
    def out_ln1_fn(acc, xv, g, b):
        s = ALPHA * xv + acc
        xhat, _ = _ln_stats(s)
        y = xhat * g + b
        return s, y, y

    s1, x1, x1_bf = _mm(mix, w_out, M=T, N=D, K=D, tm=256, tn=D, tk=D,
                        extras=[(x, "mn"), (W["ln1_g"], "n"), (W["ln1_b"], "n")], epilogue=out_ln1_fn, n_out=3,
                        out_dtypes=(F32, F32, BF16), name="fwd_out_proj_ln1")
    w_up = comm.weight("mlp_w_up", x1_bf)

    def mlp_up_fn(acc, b):
        hp = acc + b
        rl = jnp.maximum(hp, 0.0)
        return rl * rl, hp

    hact, hpre = _mm(x1_bf, w_up, M=T, N=F, K=D, tm=1024, tn=1024, tk=D, extras=[(W["mlp_b_up"], "n")],
                     epilogue=mlp_up_fn, n_out=2, out_dtypes=(BF16, BF16), name="fwd_mlp_up")
    w_down = comm.weight("mlp_w_down", hact)
    s2 = _mm(hact, w_down, M=T, N=D, K=F, tm=512, tn=1024, tk=4096,
             extras=[(x1, "mn"), (W["mlp_b_down"], "n")], epilogue=lambda acc, xv, b: (ALPHA * xv + acc + b,),
             name="fwd_mlp_down")

    def ln2_fn(s, t, g, b):
        xhat, rstd = _ln_stats(s)
        err = xhat * g + b - t
        dy = err * (1.0 / D)
        ds = _ln_bwd(dy, g, xhat, rstd)
        return ds, ds, 0.5 * dy * err, dy * xhat, dy, ds

    ds2, ds2_bf, loss_cols, d_ln2_g, d_ln2_b, d_b_down = _ew(
        ln2_fn, [(s2, "tile", 0), (tgt, "tile", 0), (W["ln2_g"], "vec", 0), (W["ln2_b"], "vec", 0)],
        T=T, C=D, n_out=2, n_cs=4, out_dtypes=(F32, BF16), tm=256, name="bwd_loss_ln2")
    d_w_down = _mm(hact, ds2_bf, M=F, N=D, K=T, ta=True, out_dtypes=(BF16,), tm=1024, tn=1024, tk=4096, name="bwd_w_down")
    sent = comm.send_grad("mlp_w_down", d_w_down)

    def dhpre_fn(acc, hp):
        dv = acc * (2.0 * jnp.maximum(hp.astype(F32), 0.0))
        return dv, dv

    dhpre, d_b_up = _mm(ds2_bf, w_down, M=T, N=F, K=D, tb=True, tm=1024, tn=1024, tk=D, extras=[(hpre, "mn")],
                        epilogue=dhpre_fn, n_cs=1, out_dtypes=(BF16,), after=sent, name="bwd_mlp_down")
    d_w_up = _mm(x1_bf, dhpre, M=D, N=F, K=T, ta=True, out_dtypes=(BF16,), n_split=N_DEV, tm=1024, tn=F // N_DEV, tk=4096, name="bwd_w_up")
    sent = comm.send_grad("mlp_w_up", d_w_up)
    dx1 = _mm(dhpre, w_up, M=T, N=D, K=F, tb=True, tm=512, tn=1024, tk=4096,
              extras=[(ds2, "mn")], epilogue=lambda acc, dv: (ALPHA * dv + acc,), after=sent, name="bwd_mlp_up")

    def ln1_bwd_fn(s, dy, g):
        xhat, rstd = _ln_stats(s)
        ds = _ln_bwd(dy, g, xhat, rstd)
        return ds, ds, dy * xhat, dy

    ds1, ds1_bf, d_ln1_g, d_ln1_b = _ew(ln1_bwd_fn, [(s1, "tile", 0), (dx1, "tile", 0), (W["ln1_g"], "vec", 0)],
                                        T=T, C=D, n_out=2, n_cs=2, out_dtypes=(F32, BF16), tm=256, name="bwd_ln1")
    d_w_out = _mm(mix, ds1_bf, M=D, N=D, K=T, ta=True, out_dtypes=(BF16,), tm=1024, tn=1024, tk=4096, name="bwd_w_out")
    sent = comm.send_grad("w_out", d_w_out)
    def mix_bwd_fn(dm, ga, gb, ya, a, b):
        ya, a, b = ya.astype(F32), a.astype(F32), b.astype(F32)
        sa, sb, sv = _sig(ga), _sig(gb), _sig(b)
        yb = a * sv
        dyb = dm * sb
        return (dm * ya * (sa * (1.0 - sa)), dm * yb * (sb * (1.0 - sb)), dm * sa, dyb * sv,
                dyb * a * (sv * (1.0 - sv)))

    dg_a, dg_b, dy_a, dglu_a, dglu_b = _mm(
        ds1_bf, w_out, M=T, N=D, K=D, tb=True, tm=512, tn=cwm, tk=D,
        extras=[(z, "mn", ga_off // cwm), (z, "mn", gb_off // cwm), (y_a, "mn"), (glu_a, "mn"), (glu_b, "mn")],
        epilogue=mix_bwd_fn, n_out=5, out_dtypes=(BF16,) * 5, after=sent, name="bwd_out_proj_mix")

    d_w_a_out = _mm(p, dy_a, M=C, N=D, K=T, ta=True, out_dtypes=(BF16,), tm=1024, tn=1024, tk=4096, name="bwd_w_a_out")
    sent = comm.send_grad("w_a_out", d_w_a_out)
    def dp_fn(dp, hv, gate):
        th = jnp.tanh(GELU_C * (gate + GELU_K * gate * gate * gate))
        gelu = 0.5 * gate * (1.0 + th)
        dgelu = 0.5 * (1.0 + th) + 0.5 * gate * (1.0 - th * th) * (GELU_C * (1.0 + 3.0 * GELU_K * gate * gate))
        return dp * gelu, dp * hv * dgelu

    dh, dgate = _mm(dy_a, w_a_out, M=T, N=C, K=D, tb=True, tm=256, tn=C, tk=D, extras=[(h, "mn"), (z, "mn", 1)],
                    epilogue=dp_fn, n_out=2, out_dtypes=(F32, BF16), after=sent, name="bwd_rg_out")
    drai, dxc0, d_b_ri, d_sp = _rg_scan_bwd(dh, h, ri, xc, a_fwd, m_fwd, sp, T=T, C=C, cw=kr, name="bwd_rg_scan")
    dxc = _bd([(drai, 0, w_ri)], T=T, J=jr, kb=2 * kr, nb=kr, tw=True, extras=[(dxc0, "tile", 0)],
              epilogue=lambda acc, d0: (acc + d0,), name="bwd_gates")
    d_w_ri = _bdw(xc, 0, drai, 0, T=T, J=jr, kb=kr, nb=2 * kr, name="bwd_w_gates")
    d_wa, d_wx = _bd_unpack(d_w_ri[:, :, :kr], RG_Q), _bd_unpack(d_w_ri[:, :, kr:], RG_Q)
    d_b_ri = d_b_ri.reshape(jr, 2 * kr)
    d_ba, d_bx = d_b_ri[:, :kr].reshape(1, -1), d_b_ri[:, kr:].reshape(1, -1)
    dxr, conv_sums = _conv_bwd(dxc, z, W["conv_w"], T=T, C=C, name="bwd_conv")
    d_conv_w, d_conv_b = conv_sums[0:4], conv_sums[4:5]
    (d_lambda,) = sp_vjp(d_sp)

    d_glu_w = _mm(yg, dglu_a, M=S, N=D, K=T, ta=True, out_dtypes=(BF16,), n_split=N_DEV, tm=1024, tn=D // N_DEV, tk=4096, name="bwd_w_glu_w")
    d_glu_v = _mm(yg, dglu_b, M=S, N=D, K=T, ta=True, out_dtypes=(BF16,), n_split=N_DEV, tm=1024, tn=D // N_DEV, tk=4096, name="bwd_w_glu_v")
    sent = comm.send_grad("glu_w", d_glu_w, "glu_v", d_glu_v)
    dyg0 = _mm(dglu_a, w_glu_w, M=T, N=S, K=D, tb=True, tm=512, tn=S, tk=D, after=sent, name="bwd_glu_w")
    dy_s = _mm(dglu_b, w_glu_v, M=T, N=S, K=D, tb=True, tm=512, tn=S, tk=D,
               extras=[(dyg0, "mn"), (y_s, "mn")], epilogue=lambda acc, d0, yv: ((acc + d0) * _dgelu(yv),),
               name="bwd_glu_v")
    du, d_lbr, d_lbi, d_ssm_d, d_wb_re, d_wb_im, d_wc_re, d_wc_im_neg = _s5_bwd(
        dy_s, z, u_off, h_re, h_im, wb_re, wb_im, wc_re, wc_im_neg, d_row, powers, T=T, name="bwd_s5")
    d_bbr = jnp.swapaxes(_bd_unpack(d_wb_re, S5_Q), 1, 2)
    d_bbi = jnp.swapaxes(_bd_unpack(d_wb_im, S5_Q), 1, 2)
    d_a_re, d_a_im, d_log_dt, d_b_re, d_b_im = s5_vjp((d_lbr.reshape(G, P), d_lbi.reshape(G, P), d_bbr, d_bbi))
    d_c_re = jnp.swapaxes(_bd_unpack(d_wc_re, S5_Q), 1, 2)
    d_c_im = -jnp.swapaxes(_bd_unpack(d_wc_im_neg, S5_Q), 1, 2)

    dz = jnp.concatenate([dxr.astype(BF16), dgate.astype(BF16), du, dg_a, dg_b], axis=1)
    d_w_in = _mm(x_bf, dz, M=D, N=n_in, K=T, ta=True, out_dtypes=(BF16,), n_split=N_DEV, tm=1024, tn=n_in // N_DEV, tk=4096, name="bwd_w_in")
    sent = comm.send_grad("w_in", d_w_in)
    grad_x = _mm(dz, w_in, M=T, N=D, K=n_in, tb=True, tm=512, tn=1024, tk=n_in // 2,
                 extras=[(ds1, "mn")], epilogue=lambda acc, dv: (ALPHA * dv + acc,), after=sent, name="bwd_in_proj")

    grads = dict(
        conv_w=d_conv_w, conv_b=d_conv_b, rg_wa=d_wa, rg_ba=d_ba, rg_wx=d_wx, rg_bx=d_bx,
        rg_lambda=d_lambda, ssm_a_re=d_a_re, ssm_a_im=d_a_im, ssm_log_dt=d_log_dt,
        ssm_b_re=d_b_re, ssm_b_im=d_b_im, ssm_c_re=d_c_re, ssm_c_im=d_c_im, ssm_d=d_ssm_d.reshape(G, H),
        ln1_g=d_ln1_g, ln1_b=d_ln1_b, mlp_b_up=d_b_up, mlp_b_down=d_b_down, ln2_g=d_ln2_g, ln2_b=d_ln2_b)
    return jnp.sum(loss_cols), grad_x, grads


BIG = ("w_in", "w_a_out", "glu_w", "glu_v", "w_out", "mlp_w_up", "mlp_w_down")
COL_SHARDED = ("w_in", "glu_w", "glu_v", "mlp_w_up")
SMALL = ("conv_w", "conv_b", "rg_wa", "rg_ba", "rg_wx", "rg_bx", "rg_lambda", "ssm_a_re", "ssm_a_im", "ssm_log_dt",
         "ssm_b_re", "ssm_b_im", "ssm_c_re", "ssm_c_im", "ssm_d", "ln1_g", "ln1_b", "mlp_b_up", "mlp_b_down", "ln2_g",
         "ln2_b")
ORDER = ("w_in", "conv_w", "conv_b", "rg_wa", "rg_ba", "rg_wx", "rg_bx", "rg_lambda", "w_a_out", "ssm_a_re",
         "ssm_a_im", "ssm_log_dt", "ssm_b_re", "ssm_b_im", "ssm_c_re", "ssm_c_im", "ssm_d", "glu_w", "glu_v", "w_out",
         "ln1_g", "ln1_b", "mlp_w_up", "mlp_b_up", "mlp_w_down", "mlp_b_down", "ln2_g", "ln2_b")
TILE_ELEMS = SUBLANES * LANES


def _pack(arrs):
    pieces = []
    for a in arrs:
        flat = a.reshape(-1)
        flat = jnp.pad(flat, (0, (-flat.shape[0]) % TILE_ELEMS))
        pieces.append(flat.reshape(-1, LANES))
    rows = sum(p.shape[0] for p in pieces)
    pad_rows = (-rows) % (N_DEV * SUBLANES)
    if pad_rows:
        pieces.append(jnp.zeros((pad_rows, LANES), pieces[0].dtype))
    return jnp.concatenate(pieces, axis=0)


def _unpack(packed, shapes):
    out, row = [], 0
    for shp in shapes:
        n = math.prod(shp)
        rows = -(-n // TILE_ELEMS) * SUBLANES
        out.append(packed[row:row + rows].reshape(-1)[:n].reshape(shp))
        row += rows
    return out


class _Comm:
    def __init__(self, w):
        first = _all_gather([w["w_in"].astype(BF16), w["conv_w"]], name="gather_w_in")
        self._weights = {"w_in": first[0]}
        self.conv_w = jnp.swapaxes(first[1], 0, 1).reshape(w["conv_w"].shape[0], -1)
        self._shards = {k: w[k].astype(BF16) for k in BIG if k != "w_in"}
        self._gathers, self._grads = {}, {}
        self.gather_token = self.start_weights(("w_a_out", "glu_w", "glu_v", "w_out"), None)

    def start_weights(self, names, after):
        shards = [self._shards.pop(k) for k in names]
        handles, token = _exchange_start(shards, [_landing_zone(s) for s in shards], scatter=False, after=after,
                                         name="gather_start_" + names[0])
        self._gathers.update(zip(names, handles))
        return token

    def weight(self, k, after):
        if k not in self._weights:
            self._weights[k] = _exchange_wait(self._gathers.pop(k), after, scatter=False, name="gather_wait_" + k)
        gk = self._weights[k]
        if k in COL_SHARDED:
            return jnp.swapaxes(gk, 0, 1).reshape(gk.shape[1], -1)
        return gk.reshape(-1, gk.shape[-1])

    def send_grad(self, *names_and_parts):
        names, parts = names_and_parts[0::2], names_and_parts[1::2]
        parts = [p if k in COL_SHARDED else p.reshape(N_DEV, p.shape[0] // N_DEV, p.shape[1])
                 for k, p in zip(names, parts)]
        me = _dev_index(*_mesh_pos())
        lands = [_landing_zone(lax.dynamic_index_in_dim(p, me, 0, keepdims=False)) for p in parts]
        handles, token = _exchange_start(parts, lands, scatter=True, name="grad_start_" + names[0])
        self._grads.update(zip(names, handles))
        return token

    def received_grad(self, k, after):
        return _exchange_wait(self._grads.pop(k), after, scatter=True, name="grad_wait_" + k)


SMALL_GROUPS = (("rg_wa", "rg_wx"), ("ssm_b_re",), ("ssm_b_im",),
                tuple(k for k in SMALL if k not in ("rg_wa", "rg_wx", "ssm_b_re", "ssm_b_im")))


def _step(x, tgt, w, m, v, raw_w, raw_m, raw_v):
    dev = _dev_index(*_mesh_pos())

    comm = _Comm(w)
    small = dict(w)
    small["conv_w"] = comm.conv_w
    for k in ("conv_b", "rg_ba", "rg_bx", "rg_lambda", "ln1_g", "ln1_b", "mlp_b_up", "mlp_b_down", "ln2_g", "ln2_b"):
        small[k] = w[k].reshape(1, -1)

    loss_part, grad_x, grads = _local_step(x, tgt, small, comm)

    out_g, out_d, out_m, out_v = {}, {}, {}, {}
    for k in BIG:
        rk = comm.received_grad(k, grad_x)
        out_g[k], out_d[k], out_m[k], out_v[k] = _adamw(rk, w[k], m[k], v[k], tr=128, name="adamw_" + k)

    small_shapes = [grads[k].shape for k in SMALL]
    (small_recv,) = _exchange_blocks([_pack([grads[k] for k in SMALL]).reshape(N_DEV, -1, LANES)],
                                     name="exchange_small_grads")
    small_block = _sum_slots(small_recv, tr=512, name="sum_small_grads")
    (small_all,) = _all_gather([small_block], name="gather_small_grads")
    g_small = dict(zip(SMALL, _unpack(small_all.reshape(-1, LANES), small_shapes)))
    cw_cols = w["conv_w"].shape[1]
    g_small["conv_w"] = lax.dynamic_slice_in_dim(g_small["conv_w"], dev * cw_cols, cw_cols, axis=1)
    for group in SMALL_GROUPS:
        gs = [g_small[k].reshape(raw_w[k].shape) for k in group]
        res = _adamw_whole(gs, [raw_w[k] for k in group], [raw_m[k] for k in group], [raw_v[k] for k in group],
                           name="adamw_" + group[0])
        for k, gk, (dk, mk, vk) in zip(group, gs, res):
            out_g[k], out_d[k], out_m[k], out_v[k] = gk, dk, mk, vk

    loss = lax.psum(loss_part, ("x", "y", "c"))
    return loss, grad_x, out_g, out_d, out_m, out_v


def kernel(x, w_in, conv_w, conv_b, rg_wa, rg_ba, rg_wx, rg_bx, rg_lambda, w_a_out, ssm_a_re, ssm_a_im, ssm_log_dt, ssm_b_re, ssm_b_im, ssm_c_re, ssm_c_im, ssm_d, glu_w, glu_v, w_out, ln1_g, ln1_b, mlp_w_up, mlp_b_up, mlp_w_down, mlp_b_down, ln2_g, ln2_b, loss_target, m_w_in, m_conv_w, m_conv_b, m_rg_wa, m_rg_ba, m_rg_wx, m_rg_bx, m_rg_lambda, m_w_a_out, m_ssm_a_re, m_ssm_a_im, m_ssm_log_dt, m_ssm_b_re, m_ssm_b_im, m_ssm_c_re, m_ssm_c_im, m_ssm_d, m_glu_w, m_glu_v, m_w_out, m_ln1_g, m_ln1_b, m_mlp_w_up, m_mlp_b_up, m_mlp_w_down, m_mlp_b_down, m_ln2_g, m_ln2_b, v_w_in, v_conv_w, v_conv_b, v_rg_wa, v_rg_ba, v_rg_wx, v_rg_bx, v_rg_lambda, v_w_a_out, v_ssm_a_re, v_ssm_a_im, v_ssm_log_dt, v_ssm_b_re, v_ssm_b_im, v_ssm_c_re, v_ssm_c_im, v_ssm_d, v_glu_w, v_glu_v, v_w_out, v_ln1_g, v_ln1_b, v_mlp_w_up, v_mlp_b_up, v_mlp_w_down, v_mlp_b_down, v_ln2_g, v_ln2_b):
    args = locals()
    w = {k: args[k][0] for k in ORDER}
    m = {k: args["m_" + k][0] for k in BIG}
    v = {k: args["v_" + k][0] for k in BIG}
    raw = [{k: args[prefix + k] for k in SMALL} for prefix in ("", "m_", "v_")]
    loss, grad_x, out_g, out_d, out_m, out_v = _step(x[0], loss_target[0], w, m, v, *raw)
    outs = [loss, grad_x[None]]
    for group in (out_g, out_d, out_m, out_v):
        outs += [group[k].reshape(args[k].shape) for k in ORDER]
    return tuple(outs)
```

```python
import functools
import math

import jax
import jax.numpy as jnp
from jax import lax
from jax.experimental import pallas as pl
from jax.experimental.pallas import tpu as pltpu

F32 = jnp.float32
BF16 = jnp.bfloat16
MESH = pl.DeviceIdType.MESH
N_DEV = 8
SUBLANES = 8
LANES = 128
VMEM_BYTES_V7X = 64 * 2 ** 20
VMEM_CAP = VMEM_BYTES_V7X - 8 * 2 ** 20

ALPHA = 2.0 ** 0.25
LN_EPS = 1e-5
RG_C = 8.0
ADAM_LR, ADAM_B1, ADAM_B2, ADAM_EPS, ADAM_WD, ADAM_STEP = 0.001, 0.9, 0.999, 1e-08, 0.01, 10
GELU_C = math.sqrt(2.0 / math.pi)
GELU_K = 0.044715

ANY = pl.BlockSpec(memory_space=pl.ANY)


def _params(sem, vmem_bytes):
    limit = int(min(max(2 * vmem_bytes, 16 * 2 ** 20), VMEM_CAP))
    return pltpu.CompilerParams(dimension_semantics=sem, vmem_limit_bytes=limit)


def _sig(x):
    return 1.0 / (1.0 + jnp.exp(-x))


def _gelu(x):
    return 0.5 * x * (1.0 + jnp.tanh(GELU_C * (x + GELU_K * x * x * x)))


def _dgelu(x):
    th = jnp.tanh(GELU_C * (x + GELU_K * x * x * x))
    return 0.5 * (1.0 + th) + 0.5 * x * (1.0 - th * th) * (GELU_C * (1.0 + 3.0 * GELU_K * x * x))


def _one_minus_exp(x, exp_half_x):
    p = x * (1.0 + x * (1 / 2 + x * (1 / 6 + x * (1 / 24 + x * (1 / 120)))))
    return jnp.where(x > -1 / 16, -p, 1.0 - exp_half_x * exp_half_x)


def _accumulate(ref, val, first):
    @pl.when(first)
    def _():
        ref[...] = val

    @pl.when(jnp.logical_not(first))
    def _():
        ref[...] += val


def _rows8(cw):
    return lax.broadcasted_iota(jnp.int32, (SUBLANES, cw), 0)


def _shift_down(cur, prev, s, rows):
    return jnp.where(rows < s, pltpu.roll(prev, s, 0), pltpu.roll(cur, s, 0))


def _shift_up(cur, nxt, s, rows):
    return jnp.where(rows < SUBLANES - s, pltpu.roll(cur, SUBLANES - s, 0), pltpu.roll(nxt, SUBLANES - s, 0))


def _mm(a, b, *, M, N, K, ta=False, tb=False, b_split=1, n_split=1, a_fn=None, extras=(), epilogue=None,
        n_out=1, n_cs=0, out_dtypes=None, tm=512, tn=512, tk=512, after=None, name):
    tm, tn, tk = min(tm, M), min(tn, N), min(tk, K)
    assert M % tm == 0 and N % tn == 0 and K % tk == 0, (name, M, N, K, tm, tn, tk)
    nk = K // tk
    grid = (N // tn, M // tm, nk)
    a_spec = pl.BlockSpec((tk, tm), lambda j, i, k: (k, i)) if ta else pl.BlockSpec((tm, tk), lambda j, i, k: (i, k))
    if b_split == 1:
        b_spec = pl.BlockSpec((tn, tk), lambda j, i, k: (j, k)) if tb else pl.BlockSpec((tk, tn), lambda j, i, k: (k, j))
    elif tb:
        kb = (K // b_split) // tk
        assert kb * tk * b_split == K, name
        b_spec = pl.BlockSpec((None, tn, tk), lambda j, i, k: (k // kb, j, k % kb))
    else:
        nb = (N // b_split) // tn
        assert nb * tn * b_split == N, name
        b_spec = pl.BlockSpec((None, tk, tn), lambda j, i, k: (j // nb, k, j % nb))
    in_specs = [a_spec, b_spec]
    for arr, kind, *col_off in extras:
        off = col_off[0] if col_off else 0
        in_specs.append(pl.BlockSpec((tm, tn), lambda j, i, k, off=off: (i, off + j)) if kind == "mn"
                        else pl.BlockSpec((1, tn), lambda j, i, k: (0, j)))
    out_dtypes = (F32,) * n_out if out_dtypes is None else out_dtypes
    if n_split == 1:
        out_shape = [jax.ShapeDtypeStruct((M, N), dt) for dt in out_dtypes]
        out_specs = [pl.BlockSpec((tm, tn), lambda j, i, k: (i, j)) for _ in range(n_out)]
    else:
        assert n_out == 1
        nbo = (N // n_split) // tn
        assert nbo * tn * n_split == N, name
        out_shape = [jax.ShapeDtypeStruct((n_split, M, N // n_split), out_dtypes[0])]
        out_specs = [pl.BlockSpec((None, tm, tn), lambda j, i, k: (j // nbo, i, j % nbo))]
    out_shape += [jax.ShapeDtypeStruct((1, N), F32) for _ in range(n_cs)]
    out_specs += [pl.BlockSpec((1, tn), lambda j, i, k: (0, j)) for _ in range(n_cs)]
    ne = len(extras)
    dims = (((0 if ta else 1,), (1 if tb else 0,)), ((), ()))

    n_after = 0 if after is None else 1
    in_specs += [ANY] * n_after

    def body(*refs):
        a_ref, b_ref = refs[0], refs[1]
        ex_refs = refs[2:2 + ne]
        first_out = 2 + ne + n_after
        out_refs = refs[first_out:first_out + n_out]
        cs_refs = refs[first_out + n_out:first_out + n_out + n_cs]
        i, k = pl.program_id(1), pl.program_id(2)

        def product():
            av = a_ref[...]
            if a_fn is not None:
                av = a_fn(av.astype(F32))
            return lax.dot_general(av.astype(BF16), b_ref[...].astype(BF16), dims, preferred_element_type=F32)

        def finish(acc):
            res = (acc,) if epilogue is None else epilogue(acc, *[r[...] for r in ex_refs])
            for r, o in zip(out_refs, res[:n_out]):
                r[...] = o.astype(r.dtype)
            for r, cval in zip(cs_refs, res[n_out:]):
                _accumulate(r, jnp.sum(cval, axis=0, keepdims=True), i == 0)

        if nk == 1:
            finish(product())
            return
        acc_ref = refs[-1]

        @pl.when(k == 0)
        def _():
            acc_ref[...] = jnp.zeros_like(acc_ref)

        acc_ref[...] += product()

        @pl.when(k == nk - 1)
        def _():
            finish(acc_ref[...])

    vmem = 2 * tm * tk * a.dtype.itemsize + 2 * tk * tn * b.dtype.itemsize + (1 + 2 * n_out + 2 * ne + 2) * tm * tn * 4
    outs = pl.pallas_call(
        body, name=name, grid=grid, in_specs=in_specs, out_specs=out_specs, out_shape=out_shape,
        scratch_shapes=[pltpu.VMEM((tm, tn), F32)] if nk > 1 else [],
        compiler_params=_params(("parallel", "arbitrary", "arbitrary"), vmem),
    )(a, b, *[e[0] for e in extras], *([after] if n_after else []))
    return outs[0] if len(outs) == 1 else outs


BD_STEP = 4

def _bd(pairs, *, T, J, kb, nb, tw=False, extras=(), epilogue=None, n_out=1, n_cs=0, out_dtypes=None, tm=512, name):
    jb = BD_STEP
    assert T % tm == 0 and J % jb == 0
    grid = (J // jb, T // tm)
    npair, ne = len(pairs), len(extras)
    in_specs, args = [], []
    for arr, off, w in pairs:
        assert off % jb == 0, name
        in_specs.append(pl.BlockSpec((tm, jb * kb), lambda j, i, off=off // jb: (i, off + j)))
        in_specs.append(pl.BlockSpec((jb,) + tuple(w.shape[1:]), lambda j, i: (j, 0, 0)))
        args += [arr, w]
    for arr, kind, off in extras:
        assert off % jb == 0, name
        in_specs.append(pl.BlockSpec((tm, jb * nb), lambda j, i, off=off // jb: (i, off + j)) if kind == "tile"
                        else pl.BlockSpec((1, jb * nb), lambda j, i, off=off // jb: (0, off + j)))
        args.append(arr)
    out_dtypes = (F32,) * n_out if out_dtypes is None else out_dtypes
    out_shape = [jax.ShapeDtypeStruct((T, J * nb), dt) for dt in out_dtypes]
    out_specs = [pl.BlockSpec((tm, jb * nb), lambda j, i: (i, j)) for _ in range(n_out)]
    out_shape += [jax.ShapeDtypeStruct((1, J * nb), F32) for _ in range(n_cs)]
    out_specs += [pl.BlockSpec((1, jb * nb), lambda j, i: (0, j)) for _ in range(n_cs)]
    dims = (((1,), (1 if tw else 0,)), ((), ()))

    def body(*refs):
        ex_refs = refs[2 * npair:2 * npair + ne]
        out_refs = refs[2 * npair + ne:2 * npair + ne + n_out]
        cs_refs = refs[2 * npair + ne + n_out:]
        i = pl.program_id(1)
        for s in range(jb):
            cols_in, cols_out = pl.ds(s * kb, kb), pl.ds(s * nb, nb)
            acc = None
            for p in range(npair):
                d = lax.dot_general(refs[2 * p][:, cols_in].astype(BF16), refs[2 * p + 1][s].astype(BF16), dims,
                                    preferred_element_type=F32)
                acc = d if acc is None else acc + d
            res = (acc,) if epilogue is None else epilogue(acc, *[r[:, cols_out] for r in ex_refs])
            for r, o in zip(out_refs, res[:n_out]):
                r[:, cols_out] = o.astype(r.dtype)
            for r, cval in zip(cs_refs, res[n_out:]):
                _accumulate(r.at[:, cols_out], jnp.sum(cval, axis=0, keepdims=True), i == 0)

    vmem = jb * (2 * npair * tm * kb + 2 * npair * kb * nb + (2 * n_out + 2 * ne + 3) * tm * nb) * 4
    outs = pl.pallas_call(
        body, name=name, grid=grid, in_specs=in_specs, out_specs=out_specs, out_shape=out_shape,
        compiler_params=_params(("parallel", "arbitrary"), vmem),
    )(*args)
    return outs[0] if len(outs) == 1 else outs


def _bdw(a, a_off, b, b_off, *, T, J, kb, nb, tm=512, name):
    jb = BD_STEP
    assert T % tm == 0 and J % jb == 0 and a_off % jb == 0 and b_off % jb == 0
    a_blk, b_blk = a_off // jb, b_off // jb

    def body(a_ref, b_ref, o_ref):
        i = pl.program_id(1)
        for s in range(jb):
            d = lax.dot_general(a_ref[:, pl.ds(s * kb, kb)].astype(BF16), b_ref[:, pl.ds(s * nb, nb)].astype(BF16),
                                (((0,), (0,)), ((), ())), preferred_element_type=F32)
            _accumulate(o_ref.at[s], d, i == 0)

    return pl.pallas_call(
        body, name=name, grid=(J // jb, T // tm),
        in_specs=[pl.BlockSpec((tm, jb * kb), lambda j, i: (i, a_blk + j)),
                  pl.BlockSpec((tm, jb * nb), lambda j, i: (i, b_blk + j))],
        out_specs=pl.BlockSpec((jb, kb, nb), lambda j, i: (j, 0, 0)),
        out_shape=jax.ShapeDtypeStruct((J, kb, nb), F32),
        compiler_params=_params(("parallel", "arbitrary"), jb * (2 * tm * (kb + nb) + 3 * kb * nb) * 4),
    )(a, b)


def _bd_pack(w, q):
    g, a, b = w.shape
    eye = jnp.eye(q, dtype=w.dtype)
    return jnp.einsum("jqab,qr->jqarb", w.reshape(g // q, q, a, b), eye).reshape(g // q, q * a, q * b)


def _bd_unpack(wp, q):
    j, qa, qb = wp.shape
    a, b = qa // q, qb // q
    w5 = wp.reshape(j, q, a, q, b)
    return jnp.stack([w5[:, r, :, r, :] for r in range(q)], axis=1).reshape(j * q, a, b)


def _ew(fn, ins, *, T, C, n_out, n_cs=0, out_dtypes=None, tm=256, cw=None, name):
    cw = C if cw is None else cw
    assert T % tm == 0 and C % cw == 0
    grid = (C // cw, T // tm)
    in_specs = []
    for arr, kind, off in ins:
        in_specs.append(pl.BlockSpec((tm, cw), lambda j, i, off=off: (i, off + j)) if kind == "tile"
                        else pl.BlockSpec((arr.shape[0], cw), lambda j, i, off=off: (0, off + j)))
    out_dtypes = (F32,) * n_out if out_dtypes is None else out_dtypes
    out_shape = [jax.ShapeDtypeStruct((T, C), dt) for dt in out_dtypes]
    out_specs = [pl.BlockSpec((tm, cw), lambda j, i: (i, j)) for _ in range(n_out)]
    out_shape += [jax.ShapeDtypeStruct((1, C), F32) for _ in range(n_cs)]
    out_specs += [pl.BlockSpec((1, cw), lambda j, i: (0, j)) for _ in range(n_cs)]
    nin = len(ins)

    def body(*refs):
        i = pl.program_id(1)
        res = fn(*[r[...].astype(F32) for r in refs[:nin]])
        for r, o in zip(refs[nin:nin + n_out], res[:n_out]):
            r[...] = o.astype(r.dtype)
        for r, cval in zip(refs[nin + n_out:], res[n_out:]):
            _accumulate(r, jnp.sum(cval, axis=0, keepdims=True), i == 0)

    vmem = (2 * nin + 2 * n_out + 6) * tm * cw * 4
    outs = pl.pallas_call(
        body, name=name, grid=grid, in_specs=in_specs, out_specs=out_specs, out_shape=out_shape,
        compiler_params=_params(("parallel", "arbitrary"), vmem),
    )(*[arr for arr, _, _ in ins])
    return outs[0] if len(outs) == 1 else outs


def _ln_stats(s):
    mu = jnp.mean(s, axis=-1, keepdims=True)
    d = s - mu
    var = jnp.mean(d * d, axis=-1, keepdims=True)
    rstd = lax.rsqrt(var + LN_EPS)
    return d * rstd, rstd


def _ln_bwd(dy, g, xhat, rstd):
    dxh = dy * g
    m1 = jnp.mean(dxh, axis=-1, keepdims=True)
    m2 = jnp.mean(dxh * xhat, axis=-1, keepdims=True)
    return rstd * (dxh - m1 - xhat * m2)


def _conv_fwd(z, conv_w, conv_b, *, T, C, tm=512, cw=1024, after=None, name):
    ng, hb = tm // SUBLANES, tm // SUBLANES
    n_after = 0 if after is None else 1

    def body(x_ref, halo_ref, w_ref, b_ref, *rest):
        o_ref = rest[-1]
        it = pl.program_id(1)
        rows = _rows8(cw)
        halo = jnp.where(it == 0, 0.0, halo_ref[...])
        w = w_ref[...]
        bias = b_ref[...]

        def group(g, carry):
            off = pl.multiple_of(g * SUBLANES, SUBLANES)
            cur = x_ref[pl.ds(off, SUBLANES), :]
            prev = x_ref[pl.ds(pl.multiple_of(jnp.maximum(off - SUBLANES, 0), SUBLANES), SUBLANES), :]
            prev = jnp.where(g == 0, halo, prev)
            acc = cur * w[3:4] + bias
            for s in (1, 2, 3):
                acc = acc + _shift_down(cur, prev, s, rows) * w[3 - s:4 - s]
            o_ref[pl.ds(off, SUBLANES), :] = acc
            return carry

        lax.fori_loop(0, ng, group, 0, unroll=2)

    return pl.pallas_call(
        body, name=name, grid=(C // cw, T // tm),
        in_specs=[pl.BlockSpec((tm, cw), lambda j, i: (i, j)),
                  pl.BlockSpec((SUBLANES, cw), lambda j, i: (jnp.maximum(i * hb - 1, 0), j)),
                  pl.BlockSpec((4, cw), lambda j, i: (0, j)), pl.BlockSpec((1, cw), lambda j, i: (0, j))]
        + [ANY] * n_after,
        out_specs=pl.BlockSpec((tm, cw), lambda j, i: (i, j)),
        out_shape=jax.ShapeDtypeStruct((T, C), F32),
        compiler_params=_params(("parallel", "arbitrary"), 5 * tm * cw * 4),
    )(z, z, conv_w, conv_b, *([after] if n_after else []))


def _conv_bwd(dxc, z, conv_w, *, T, C, tm=512, cw=512, name):
    ng, hb, last = tm // SUBLANES, tm // SUBLANES, T // SUBLANES - 1
    nt = T // tm

    def body(d_ref, dn_ref, x_ref, w_ref, o_ref, sums_ref):
        it = pl.program_id(1)
        rows = _rows8(cw)
        dnext = jnp.where(it == nt - 1, 0.0, dn_ref[...])
        w = w_ref[...]

        def group(g, accs):
            off = pl.multiple_of(g * SUBLANES, SUBLANES)
            dcur = d_ref[pl.ds(off, SUBLANES), :]
            dnx = d_ref[pl.ds(pl.multiple_of(jnp.minimum(off + SUBLANES, tm - SUBLANES), SUBLANES), SUBLANES), :]
            dnx = jnp.where(g == ng - 1, dnext, dnx)
            xcur = x_ref[pl.ds(off, SUBLANES), :]
            acc = dcur * w[3:4]
            taps = [accs[3] + dcur * xcur]
            for s in (1, 2, 3):
                ahead = _shift_up(dcur, dnx, s, rows)
                acc = acc + ahead * w[3 - s:4 - s]
                taps.append(accs[3 - s] + ahead * xcur)
            o_ref[pl.ds(off, SUBLANES), :] = acc
            return taps[3], taps[2], taps[1], taps[0], accs[4] + dcur

        zero = jnp.zeros((SUBLANES, cw), F32)
        accs = lax.fori_loop(0, ng, group, (zero,) * 5, unroll=2)
        sums = jnp.zeros((SUBLANES, cw), F32)
        for k, a in enumerate(accs):
            sums = jnp.where(rows == k, jnp.sum(a, axis=0, keepdims=True), sums)
        _accumulate(sums_ref, sums, it == 0)

    tile = pl.BlockSpec((tm, cw), lambda j, i: (i, j))
    return pl.pallas_call(
        body, name=name, grid=(C // cw, nt),
        in_specs=[tile, pl.BlockSpec((SUBLANES, cw), lambda j, i: (jnp.minimum((i + 1) * hb, last), j)),
                  tile, pl.BlockSpec((4, cw), lambda j, i: (0, j))],
        out_specs=[tile, pl.BlockSpec((SUBLANES, cw), lambda j, i: (0, j))],
        out_shape=[jax.ShapeDtypeStruct((T, C), F32), jax.ShapeDtypeStruct((SUBLANES, C), F32)],
        compiler_params=_params(("parallel", "arbitrary"), 7 * tm * cw * 4),
    )(dxc, dxc, z, conv_w)


def _rg_coeffs(r, ig, xc, sp):
    la = (-RG_C) * r * sp
    a = jnp.exp(la)
    m = jnp.sqrt(_one_minus_exp(2.0 * la, a))
    return a, m, m * (ig * xc)


def _rg_scan_fwd(z, ri, xc, sp, *, T, C, gate_off, tm=512, cw=256, name):
    rows16 = 2 * SUBLANES
    nq = tm // rows16

    def body(gate_ref, r_ref, i_ref, xc_ref, sp_ref, h_ref, p_ref, a_ref, m_ref, carry_ref):
        it = pl.program_id(1)

        @pl.when(it == 0)
        def _():
            carry_ref[...] = jnp.zeros_like(carry_ref)

        rows = _rows8(cw)
        sp_row = sp_ref[...]

        def pair(q, carry):
            base = pl.multiple_of(q * rows16, rows16)
            halves = []
            for half in range(2):
                sl = pl.ds(pl.multiple_of(base + half * SUBLANES, SUBLANES), SUBLANES)
                a, m, b = _rg_coeffs(r_ref[sl, :], i_ref[sl, :], xc_ref[sl, :], sp_row)
                a_ref[sl, :] = a
                m_ref[sl, :] = m
                for s in (1, 2, 4):
                    keep = rows >= s
                    sa = jnp.where(keep, pltpu.roll(a, s, 0), 1.0)
                    sb = jnp.where(keep, pltpu.roll(b, s, 0), 0.0)
                    b = b + a * sb
                    a = a * sa
                h = b + a * carry
                h_ref[sl, :] = h
                halves.append(h * _gelu(gate_ref[sl, :]))
                carry = h[SUBLANES - 1:SUBLANES, :]
            p_ref[pl.ds(base, rows16), :] = jnp.concatenate(halves, axis=0).astype(p_ref.dtype)
            return carry

        last = lax.fori_loop(0, nq, pair, carry_ref[0:1, :], unroll=2)
        carry_ref[...] = jnp.broadcast_to(last, carry_ref.shape)

    tile = pl.BlockSpec((tm, cw), lambda j, i: (i, j))
    gate_blk = gate_off // cw
    return pl.pallas_call(
        body, name=name, grid=(C // cw, T // tm),
        in_specs=[pl.BlockSpec((tm, cw), lambda j, i: (i, gate_blk + j)),
                  pl.BlockSpec((tm, cw), lambda j, i: (i, 2 * j)), pl.BlockSpec((tm, cw), lambda j, i: (i, 2 * j + 1)),
                  tile, pl.BlockSpec((1, cw), lambda j, i: (0, j))],
        out_specs=[tile, tile, tile, tile],
        out_shape=[jax.ShapeDtypeStruct((T, C), F32), jax.ShapeDtypeStruct((T, C), BF16),
                   jax.ShapeDtypeStruct((T, C), F32), jax.ShapeDtypeStruct((T, C), F32)],
        scratch_shapes=[pltpu.VMEM((SUBLANES, cw), F32)],
        compiler_params=_params(("parallel", "arbitrary"), 16 * tm * cw * 4),
    )(z, ri, ri, xc, sp)


def _rg_scan_bwd(dh, h, ri, xc, a_fwd, m_fwd, sp, *, T, C, tm=512, cw=256, name):
    ng, hb, nt = tm // SUBLANES, tm // SUBLANES, T // tm

    def body(dh_ref, h_ref, hp_ref, r_ref, i_ref, xc_ref, a_ref, m_ref, sp_ref,
             drai_ref, dxc_ref, crai_ref, csp_ref, cg_ref, ca_ref):
        step = pl.program_id(1)

        @pl.when(step == 0)
        def _():
            cg_ref[...] = jnp.zeros_like(cg_ref)
            ca_ref[...] = jnp.zeros_like(ca_ref)

        rows = _rows8(cw)
        sp_row = sp_ref[...]
        hhalo = jnp.where(step == nt - 1, 0.0, hp_ref[...])

        def group(gi, carry):
            g_next, a_next, s_ra, s_ia, s_sp = carry
            g = ng - 1 - gi
            off = pl.multiple_of(g * SUBLANES, SUBLANES)
            sl = pl.ds(off, SUBLANES)
            rr, ii, xx = r_ref[sl, :], i_ref[sl, :], xc_ref[sl, :]
            a, m = a_ref[sl, :], m_ref[sl, :]
            hh = h_ref[sl, :]
            hpv = h_ref[pl.ds(pl.multiple_of(jnp.maximum(off - SUBLANES, 0), SUBLANES), SUBLANES), :]
            hpv = jnp.where(g == 0, hhalo, hpv)
            hprev = _shift_down(hh, hpv, 1, rows)
            d = dh_ref[sl, :]
            c = jnp.where(rows < SUBLANES - 1, pltpu.roll(a, SUBLANES - 1, 0), a_next)
            for s in (1, 2, 4):
                keep = rows < SUBLANES - s
                sc = jnp.where(keep, pltpu.roll(c, SUBLANES - s, 0), 1.0)
                sd = jnp.where(keep, pltpu.roll(d, SUBLANES - s, 0), 0.0)
                d = d + c * sd
                c = c * sc
            gg = d + c * g_next
            da = gg * hprev
            dm = gg * (ii * xx)
            di = gg * (m * xx)
            dxc_ref[sl, :] = gg * (m * ii)
            dla = da * a - dm * (a * a / m)
            dra = dla * ((-RG_C) * sp_row) * (rr * (1.0 - rr))
            dia = di * (ii * (1.0 - ii))
            drai_ref[sl, pl.ds(0, cw)] = dra
            drai_ref[sl, pl.ds(cw, cw)] = dia
            return (gg[0:1, :], a[0:1, :], s_ra + dra, s_ia + dia, s_sp + dla * ((-RG_C) * rr))

        zero = jnp.zeros((SUBLANES, cw), F32)
        g_first, a_first, s_ra, s_ia, s_sp = lax.fori_loop(
            0, ng, group, (cg_ref[0:1, :], ca_ref[0:1, :], zero, zero, zero), unroll=2)
        cg_ref[...] = jnp.broadcast_to(g_first, cg_ref.shape)
        ca_ref[...] = jnp.broadcast_to(a_first, ca_ref.shape)
        for ref, acc in ((crai_ref.at[:, pl.ds(0, cw)], s_ra), (crai_ref.at[:, pl.ds(cw, cw)], s_ia), (csp_ref, s_sp)):
            _accumulate(ref, jnp.sum(acc, axis=0, keepdims=True), step == 0)

    tile = pl.BlockSpec((tm, cw), lambda j, i: (nt - 1 - i, j))
    wide = pl.BlockSpec((tm, 2 * cw), lambda j, i: (nt - 1 - i, j))
    vec = pl.BlockSpec((1, cw), lambda j, i: (0, j))
    return pl.pallas_call(
        body, name=name, grid=(C // cw, nt),
        in_specs=[tile, tile, pl.BlockSpec((SUBLANES, cw), lambda j, i: (jnp.maximum((nt - 1 - i) * hb - 1, 0), j)),
                  pl.BlockSpec((tm, cw), lambda j, i: (nt - 1 - i, 2 * j)),
                  pl.BlockSpec((tm, cw), lambda j, i: (nt - 1 - i, 2 * j + 1)), tile, tile, tile, vec],
        out_specs=[wide, tile, pl.BlockSpec((1, 2 * cw), lambda j, i: (0, j)), vec],
        out_shape=[jax.ShapeDtypeStruct((T, 2 * C), F32), jax.ShapeDtypeStruct((T, C), F32),
                   jax.ShapeDtypeStruct((1, 2 * C), F32), jax.ShapeDtypeStruct((1, C), F32)],
        scratch_shapes=[pltpu.VMEM((SUBLANES, cw), F32), pltpu.VMEM((SUBLANES, cw), F32)],
        compiler_params=_params(("parallel", "arbitrary"), 24 * tm * cw * 4),
    )(dh, h, h, ri, ri, xc, a_fwd, m_fwd, sp)


def _cscan_tables(lr, li, reverse):
    lam = (lr.reshape(-1), -li.reshape(-1) if reverse else li.reshape(-1))

    def mul(p, q):
        return p[0] * q[0] - p[1] * q[1], p[0] * q[1] + p[1] * q[0]

    pows = [lam]
    for _ in range(SUBLANES - 1):
        pows.append(mul(pows[-1], lam))
    zero = jnp.zeros_like(lam[0])
    tab = jnp.stack([pows[0][0], pows[0][1], pows[1][0], pows[1][1], pows[3][0], pows[3][1], zero, zero])
    if reverse:
        pows = pows[::-1]
    return tab, jnp.stack([p[0] for p in pows]), jnp.stack([p[1] for p in pows])


def _power_slabs(lr, li, n):
    pr, pi = lr.reshape(1, -1), li.reshape(1, -1)
    while pr.shape[0] < n:
        tr, ti = pr[-1:], pi[-1:]
        pr, pi = (jnp.concatenate([pr, pr * tr - pi * ti], axis=0), jnp.concatenate([pi, pr * ti + pi * tr], axis=0))
    return jnp.repeat(pr, SUBLANES, axis=0), jnp.repeat(pi, SUBLANES, axis=0), pr[-1], pi[-1]


def _rows_to_segments(src_ref, dst_ref):
    seg = src_ref.shape[0] // SUBLANES
    for g in range(seg):
        dst_ref[pl.ds(g * SUBLANES, SUBLANES), :] = src_ref[pl.ds(g, SUBLANES, stride=seg), :].astype(dst_ref.dtype)


def _segments_to_rows(src_ref, dst_ref):
    seg = src_ref.shape[0] // SUBLANES
    for r in range(SUBLANES):
        dst_ref[pl.ds(r * seg, seg), :] = src_ref[pl.ds(r, seg, stride=SUBLANES), :].astype(dst_ref.dtype)


def _seg_scan_tile(xr_ref, xi_ref, pbr_ref, pbi_ref, tab_ref, pwr_ref, pwi_ref, cr_ref, ci_ref, *, reverse, h=None):
    tm, cw = xr_ref.shape
    seg = tm // SUBLANES
    rows = _rows8(cw)
    sign = -1.0 if reverse else 1.0
    l_re, l_im = pbr_ref[0:1, :], sign * pbi_ref[0:1, :]

    def slab(g):
        return pl.ds(pl.multiple_of(g * SUBLANES, SUBLANES), SUBLANES)

    def local(k, state):
        sl = slab(seg - 1 - k if reverse else k)
        sr, si = state
        nr = xr_ref[sl, :] + (l_re * sr - l_im * si)
        ni = xi_ref[sl, :] + (l_re * si + l_im * sr)
        xr_ref[sl, :] = nr
        xi_ref[sl, :] = ni
        return nr, ni

    zero = jnp.zeros((SUBLANES, cw), F32)
    er, ei = lax.fori_loop(0, seg, local, (zero, zero), unroll=2)

    for k, s in enumerate((1, 2, 4)):
        shift = SUBLANES - s if reverse else s
        keep = rows < SUBLANES - s if reverse else rows >= s
        sr = jnp.where(keep, pltpu.roll(er, shift, 0), 0.0)
        si = jnp.where(keep, pltpu.roll(ei, shift, 0), 0.0)
        m_re, m_im = tab_ref[2 * k:2 * k + 1, :], tab_ref[2 * k + 1:2 * k + 2, :]
        er, ei = er + (m_re * sr - m_im * si), ei + (m_re * si + m_im * sr)
    cin_r, cin_i = cr_ref[0:1, :], ci_ref[0:1, :]
    pwr, pwi = pwr_ref[...], pwi_ref[...]
    er, ei = er + (pwr * cin_r - pwi * cin_i), ei + (pwr * cin_i + pwi * cin_r)
    if reverse:
        ent_r = jnp.where(rows == SUBLANES - 1, cin_r, pltpu.roll(er, SUBLANES - 1, 0))
        ent_i = jnp.where(rows == SUBLANES - 1, cin_i, pltpu.roll(ei, SUBLANES - 1, 0))
        out_r, out_i = er[0:1, :], ei[0:1, :]
    else:
        ent_r = jnp.where(rows == 0, cin_r, pltpu.roll(er, 1, 0))
        ent_i = jnp.where(rows == 0, cin_i, pltpu.roll(ei, 1, 0))
        out_r, out_i = er[SUBLANES - 1:SUBLANES, :], ei[SUBLANES - 1:SUBLANES, :]
    cr_ref[...] = jnp.broadcast_to(out_r, cr_ref.shape)
    ci_ref[...] = jnp.broadcast_to(out_i, ci_ref.shape)

    if h is not None:
        hr_ref, hi_ref, hr_last, hi_last = h
        hr_wrap = _shift_down(hr_ref[pl.ds(tm - SUBLANES, SUBLANES), :], hr_last, 1, rows)
        hi_wrap = _shift_down(hi_ref[pl.ds(tm - SUBLANES, SUBLANES), :], hi_last, 1, rows)

    def fix(g, sums):
        sl = slab(g)
        power = slab(seg - 1 - g) if reverse else sl
        pr, pi = pbr_ref[power, :], sign * pbi_ref[power, :]
        nr = xr_ref[sl, :] + (pr * ent_r - pi * ent_i)
        ni = xi_ref[sl, :] + (pr * ent_i + pi * ent_r)
        xr_ref[sl, :] = nr
        xi_ref[sl, :] = ni
        if h is None:
            return sums
        before = slab(jnp.maximum(g - 1, 0))
        hr1 = jnp.where(g == 0, hr_wrap, hr_ref[before, :])
        hi1 = jnp.where(g == 0, hi_wrap, hi_ref[before, :])
        return sums[0] + (nr * hr1 + ni * hi1), sums[1] + (ni * hr1 - nr * hi1)

    return lax.fori_loop(0, seg, fix, (zero, zero) if h is not None else (), unroll=2)


S5_TILE = 512


def _s5_fwd(z, u_off, wb_re, wb_im, wc_re, wc_im_neg, d_row, powers, *, T, tm=S5_TILE, name):
    J, ku, kp = wb_re.shape
    nt = T // tm
    pb_re, pb_im, top_re, top_im = powers
    tab, pw_re, pw_im = _cscan_tables(top_re, top_im, False)
    u_blk = u_off // ku

    def body(u_ref, wbr_ref, wbi_ref, wcr_ref, wci_ref, d_ref, pbr_ref, pbi_ref, tab_ref, pwr_ref, pwi_ref,
             hr_ref, hi_ref, y_ref, yg_ref, cr_ref, ci_ref, us_ref, ys_ref):
        @pl.when(pl.program_id(1) == 0)
        def _():
            cr_ref[...] = jnp.zeros_like(cr_ref)
            ci_ref[...] = jnp.zeros_like(ci_ref)

        _rows_to_segments(u_ref, us_ref)
        u = us_ref[...]
        ub = u.astype(BF16)
        hr_ref[...] = jnp.dot(ub, wbr_ref[...], preferred_element_type=F32)
        hi_ref[...] = jnp.dot(ub, wbi_ref[...], preferred_element_type=F32)
        _seg_scan_tile(hr_ref, hi_ref, pbr_ref, pbi_ref, tab_ref, pwr_ref, pwi_ref, cr_ref, ci_ref, reverse=False)
        y = (jnp.dot(hr_ref[...].astype(BF16), wcr_ref[...], preferred_element_type=F32)
             + jnp.dot(hi_ref[...].astype(BF16), wci_ref[...], preferred_element_type=F32) + d_ref[...] * u)
        ys_ref[...] = y
        _segments_to_rows(ys_ref, y_ref)
        ys_ref[...] = _gelu(y)
        _segments_to_rows(ys_ref, yg_ref)

    wb_spec = pl.BlockSpec((None, ku, kp), lambda j, i: (j, 0, 0))
    wc_spec = pl.BlockSpec((None, kp, ku), lambda j, i: (j, 0, 0))
    small = pl.BlockSpec((SUBLANES, kp), lambda j, i: (0, j))
    slabs = pl.BlockSpec((tm, kp), lambda j, i: (0, j))
    state = pl.BlockSpec((tm, kp), lambda j, i: (i, j))
    chan = pl.BlockSpec((tm, ku), lambda j, i: (i, j))
    return pl.pallas_call(
        body, name=name, grid=(J, nt),
        in_specs=[pl.BlockSpec((tm, ku), lambda j, i: (i, u_blk + j)), wb_spec, wb_spec, wc_spec, wc_spec,
                  pl.BlockSpec((1, ku), lambda j, i: (0, j)), slabs, slabs, small, small, small],
        out_specs=[state, state, chan, chan],
        out_shape=[jax.ShapeDtypeStruct((T, J * kp), F32)] * 2
        + [jax.ShapeDtypeStruct((T, J * ku), F32), jax.ShapeDtypeStruct((T, J * ku), BF16)],
        scratch_shapes=[pltpu.VMEM((SUBLANES, kp), F32), pltpu.VMEM((SUBLANES, kp), F32),
                        pltpu.VMEM((tm, ku), F32), pltpu.VMEM((tm, ku), F32)],
        compiler_params=_params(("parallel", "arbitrary"), 14 * tm * kp * 4),
    )(z, wb_re, wb_im, wc_re, wc_im_neg, d_row, pb_re, pb_im, tab, pw_re, pw_im)


def _s5_bwd(dy, z, u_off, h_re, h_im, wb_re, wb_im, wc_re, wc_im_neg, d_row, powers, *, T, tm=S5_TILE, name):
    J, ku, kp = wb_re.shape
    nt, hb = T // tm, tm // SUBLANES
    pb_re, pb_im, top_re, top_im = powers
    tab, pw_re, pw_im = _cscan_tables(top_re, top_im, True)
    u_blk = u_off // ku
    contract_rows = (((0,), (0,)), ((), ()))
    contract_cols = (((1,), (1,)), ((), ()))

    def body(dy_ref, u_ref, hr_ref, hrp_ref, hi_ref, hip_ref, wbr_ref, wbi_ref, wcr_ref, wci_ref, d_ref,
             pbr_ref, pbi_ref, tab_ref, pwr_ref, pwi_ref,
             du_ref, dlr_ref, dli_ref, dd_ref, dwbr_ref, dwbi_ref, dwcr_ref, dwci_ref,
             gr_ref, gi_ref, cr_ref, ci_ref, dys_ref, us_ref):
        step = pl.program_id(1)
        first = step == 0

        @pl.when(first)
        def _():
            cr_ref[...] = jnp.zeros_like(cr_ref)
            ci_ref[...] = jnp.zeros_like(ci_ref)

        _rows_to_segments(dy_ref, dys_ref)
        _rows_to_segments(u_ref, us_ref)
        dy_t, u = dys_ref[...], us_ref[...]
        dyb, ub = dy_t.astype(BF16), u.astype(BF16)
        gr_ref[...] = lax.dot_general(dyb, wcr_ref[...], contract_cols, preferred_element_type=F32)
        gi_ref[...] = lax.dot_general(dyb, wci_ref[...], contract_cols, preferred_element_type=F32)
        hr_last = jnp.where(step == nt - 1, 0.0, hrp_ref[...])
        hi_last = jnp.where(step == nt - 1, 0.0, hip_ref[...])
        s_re, s_im = _seg_scan_tile(gr_ref, gi_ref, pbr_ref, pbi_ref, tab_ref, pwr_ref, pwi_ref, cr_ref, ci_ref,
                                    reverse=True, h=(hr_ref, hi_ref, hr_last, hi_last))
        _accumulate(dlr_ref, jnp.sum(s_re, axis=0, keepdims=True), first)
        _accumulate(dli_ref, jnp.sum(s_im, axis=0, keepdims=True), first)
        grb, gib = gr_ref[...].astype(BF16), gi_ref[...].astype(BF16)
        du = (lax.dot_general(grb, wbr_ref[...], contract_cols, preferred_element_type=F32)
              + lax.dot_general(gib, wbi_ref[...], contract_cols, preferred_element_type=F32) + dy_t * d_ref[...])
        dys_ref[...] = du
        _segments_to_rows(dys_ref, du_ref)
        _accumulate(dd_ref, jnp.sum(dy_t * u, axis=0, keepdims=True), first)
        _accumulate(dwbr_ref, lax.dot_general(ub, grb, contract_rows, preferred_element_type=F32), first)
        _accumulate(dwbi_ref, lax.dot_general(ub, gib, contract_rows, preferred_element_type=F32), first)
        _accumulate(dwcr_ref, lax.dot_general(dyb, hr_ref[...].astype(BF16), contract_rows,
                                              preferred_element_type=F32), first)
        _accumulate(dwci_ref, lax.dot_general(dyb, hi_ref[...].astype(BF16), contract_rows,
                                              preferred_element_type=F32), first)

    def tix(i):
        return nt - 1 - i

    wb_spec = pl.BlockSpec((None, ku, kp), lambda j, i: (j, 0, 0))
    wc_spec = pl.BlockSpec((None, kp, ku), lambda j, i: (j, 0, 0))
    small = pl.BlockSpec((SUBLANES, kp), lambda j, i: (0, j))
    state = pl.BlockSpec((tm, kp), lambda j, i: (tix(i), j))
    halo = pl.BlockSpec((SUBLANES, kp), lambda j, i: (jnp.maximum(tix(i) * hb - 1, 0), j))
    chan = pl.BlockSpec((tm, ku), lambda j, i: (tix(i), j))
    svec = pl.BlockSpec((1, kp), lambda j, i: (0, j))
    cvec = pl.BlockSpec((1, ku), lambda j, i: (0, j))
    slabs = pl.BlockSpec((tm, kp), lambda j, i: (0, j))
    return pl.pallas_call(
        body, name=name, grid=(J, nt),
        in_specs=[chan, pl.BlockSpec((tm, ku), lambda j, i: (tix(i), u_blk + j)), state, halo, state, halo,
                  wb_spec, wb_spec, wc_spec, wc_spec, cvec, slabs, slabs, small, small, small],
        out_specs=[chan, svec, svec, cvec, wb_spec, wb_spec, wb_spec, wb_spec],
        out_shape=[jax.ShapeDtypeStruct((T, J * ku), BF16), jax.ShapeDtypeStruct((1, J * kp), F32),
                   jax.ShapeDtypeStruct((1, J * kp), F32), jax.ShapeDtypeStruct((1, J * ku), F32),
                   jax.ShapeDtypeStruct((J, ku, kp), F32), jax.ShapeDtypeStruct((J, ku, kp), F32),
                   jax.ShapeDtypeStruct((J, ku, kp), F32), jax.ShapeDtypeStruct((J, ku, kp), F32)],
        scratch_shapes=[pltpu.VMEM((tm, kp), F32), pltpu.VMEM((tm, kp), F32),
                        pltpu.VMEM((SUBLANES, kp), F32), pltpu.VMEM((SUBLANES, kp), F32),
                        pltpu.VMEM((tm, ku), F32), pltpu.VMEM((tm, ku), F32)],
        compiler_params=_params(("parallel", "arbitrary"), 16 * tm * kp * 4),
    )(dy, z, h_re, h_re, h_im, h_im, wb_re, wb_im, wc_re, wc_im_neg, d_row, pb_re, pb_im, tab, pw_re, pw_im)


def _mesh_pos():
    return lax.axis_index("x"), lax.axis_index("y"), lax.axis_index("c")


def _dev_index(px, py, pc):
    return 4 * px + 2 * py + pc


def _all_gather(shards, name):
    n = len(shards)

    def body(*refs):
        ins, outs = refs[:n], refs[n:2 * n]
        send_sems, recv_sems, local_sems = refs[2 * n:]
        x, y, c = _mesh_pos()
        me, sibling = (x, y, c), (x, y, 1 - c)
        chips = [(1 - x, y), (x, 1 - y), (1 - x, 1 - y)]

        def copy(a, k, block, to, src=None):
            dst = outs[a].at[_dev_index(*block)]
            return pltpu.make_async_remote_copy(
                src_ref=dst if src is None else src, dst_ref=dst, send_sem=send_sems.at[a * 7 + k],
                recv_sem=recv_sems.at[a * 7 + k], device_id=to, device_id_type=MESH)

        mine = [pltpu.make_async_copy(ins[a], outs[a].at[_dev_index(*me)], local_sems.at[a]) for a in range(n)]
        for cp in mine:
            cp.start()
        first = []
        for a in range(n):
            first.append(copy(a, 0, me, sibling, src=ins[a]))
            first += [copy(a, 1 + j, me, (*chip, c), src=ins[a]) for j, chip in enumerate(chips)]
        for cp in first:
            cp.start()
        passed = []
        for j, chip in enumerate(chips):
            for a in range(n):
                copy(a, 1 + j, (*chip, c), me).wait_recv()
                fwd = copy(a, 4 + j, (*chip, c), sibling)
                fwd.start()
                passed.append(fwd)
        for a in range(n):
            copy(a, 0, sibling, me).wait_recv()
            for j, chip in enumerate(chips):
                copy(a, 4 + j, (*chip, 1 - c), me).wait_recv()
        for cp in first + passed:
            cp.wait_send()
        for cp in mine:
            cp.wait()

    return pl.pallas_call(
        body, name=name, in_specs=[ANY] * n, out_specs=[ANY] * n,
        out_shape=[jax.ShapeDtypeStruct((N_DEV,) + s.shape, s.dtype) for s in shards],
        scratch_shapes=[pltpu.SemaphoreType.DMA((7 * n,)), pltpu.SemaphoreType.DMA((7 * n,)),
                        pltpu.SemaphoreType.DMA((n,))],
    )(*shards)


def _exchange_blocks(parts, name):
    n = len(parts)
    relations = [(dx, dy, dc) for dx in (0, 1) for dy in (0, 1) for dc in (0, 1) if (dx, dy, dc) != (0, 0, 0)]

    def body(*refs):
        ins, outs = refs[:n], refs[n:2 * n]
        send_sems, recv_sems, local_sems = refs[2 * n:]
        x, y, c = _mesh_pos()
        me = _dev_index(x, y, c)
        mine = [pltpu.make_async_copy(ins[a].at[me], outs[a].at[me], local_sems.at[a]) for a in range(n)]
        for cp in mine:
            cp.start()
        copies = []
        for k, (dx, dy, dc) in enumerate(relations):
            peer = (x + dx - 2 * x * dx, y + dy - 2 * y * dy, c + dc - 2 * c * dc)
            for a in range(n):
                copies.append((pltpu.make_async_remote_copy(
                    src_ref=ins[a].at[_dev_index(*peer)], dst_ref=outs[a].at[me], send_sem=send_sems.at[a * 7 + k],
                    recv_sem=recv_sems.at[a * 7 + k], device_id=peer, device_id_type=MESH),
                    pltpu.make_async_remote_copy(
                    src_ref=ins[a].at[_dev_index(*peer)], dst_ref=outs[a].at[_dev_index(*peer)],
                    send_sem=send_sems.at[a * 7 + k], recv_sem=recv_sems.at[a * 7 + k], device_id=peer,
                    device_id_type=MESH)))
        for send, _ in copies:
            send.start()
        for _, recv in copies:
            recv.wait_recv()
        for send, _ in copies:
            send.wait_send()
        for cp in mine:
            cp.wait()

    return pl.pallas_call(
        body, name=name, in_specs=[ANY] * n, out_specs=[ANY] * n,
        out_shape=[jax.ShapeDtypeStruct(p.shape, p.dtype) for p in parts],
        scratch_shapes=[pltpu.SemaphoreType.DMA((7 * n,)), pltpu.SemaphoreType.DMA((7 * n,)),
                        pltpu.SemaphoreType.DMA((n,))],
    )(*parts)


HBM = pl.BlockSpec(memory_space=pltpu.HBM)
SEM = pl.BlockSpec(memory_space=pltpu.SEMAPHORE)
EFFECT = pltpu.SideEffectType.DATAFLOW_SIDE_EFFECTING
RELATIONS = [(dx, dy, dc) for dx in (0, 1) for dy in (0, 1) for dc in (0, 1) if (dx, dy, dc) != (0, 0, 0)]


def _peer(rel):
    x, y, c = _mesh_pos()
    dx, dy, dc = rel
    return (x + dx - 2 * x * dx, y + dy - 2 * y * dy, c + dc - 2 * c * dc)


def _split_copy(src_ref, land_ref, send_sems, recv_sems, k, scatter, incoming):
    peer = _peer(RELATIONS[k])
    me = _dev_index(*_mesh_pos())
    src = src_ref.at[_dev_index(*peer)] if scatter else src_ref
    dst = land_ref.at[_dev_index(*peer) if incoming else me]
    return pltpu.make_async_remote_copy(src_ref=src, dst_ref=dst, send_sem=send_sems.at[k], recv_sem=recv_sems.at[k],
                                        device_id=peer, device_id_type=MESH)


def _exchange_start(srcs, lands, *, scatter, after=None, name):
    n = len(srcs)
    n_after = 0 if after is None else 1

    def body(*refs):
        src_refs, land_refs = refs[:n], refs[n:2 * n]
        first_out = 2 * n + n_after
        send, recv = refs[first_out:first_out + n], refs[first_out + n:first_out + 2 * n]
        token = refs[-1]
        for k in range(len(RELATIONS)):
            for a in range(n):
                _split_copy(src_refs[a], land_refs[a], send[a], recv[a], k, scatter, incoming=False).start()
        token[...] = jnp.zeros_like(token)

    n_rel = len(RELATIONS)
    outs = pl.pallas_call(
        body, name=name, in_specs=[HBM] * (2 * n) + [ANY] * n_after,
        out_shape=[pltpu.SemaphoreType.DMA((n_rel,))] * (2 * n)
        + [pltpu.HBM(s.shape, s.dtype) for s in srcs] + [pltpu.HBM(s.shape, s.dtype) for s in lands]
        + [jax.ShapeDtypeStruct((SUBLANES, LANES), F32)],
        out_specs=[SEM] * (2 * n) + [HBM] * (2 * n) + [pl.BlockSpec(memory_space=pltpu.VMEM)],
        input_output_aliases={**{a: 2 * n + a for a in range(n)}, **{n + a: 3 * n + a for a in range(n)}},
        compiler_params=pltpu.CompilerParams(has_side_effects=EFFECT),
    )(*[pltpu.with_memory_space_constraint(s, pltpu.HBM) for s in srcs],
      *[pltpu.with_memory_space_constraint(s, pltpu.HBM) for s in lands], *([after] if n_after else []))
    per_array = [(outs[a], outs[n + a], outs[2 * n + a], outs[3 * n + a]) for a in range(n)]
    return per_array, outs[-1]


def _exchange_wait(handle, after, *, scatter, name):
    send_sems, recv_sems, src_thru, land_thru = handle

    def body(src_ref, land_ref, send, recv, after_ref, src_dead, got_ref):
        for k in range(len(RELATIONS)):
            cp = _split_copy(src_ref, land_ref, send, recv, k, scatter, incoming=True)
            cp.wait_send()
            cp.wait_recv()

    return pl.pallas_call(
        body, name=name, in_specs=[HBM, HBM, SEM, SEM, ANY],
        out_shape=[pltpu.HBM(src_thru.shape, src_thru.dtype), pltpu.HBM(land_thru.shape, land_thru.dtype)],
        out_specs=[HBM, HBM], input_output_aliases={0: 0, 1: 1},
        compiler_params=pltpu.CompilerParams(has_side_effects=EFFECT),
    )(src_thru, land_thru, send_sems, recv_sems, after)[1]


def _landing_zone(own_block):
    me = _dev_index(*_mesh_pos())
    zone = lax.empty((N_DEV,) + own_block.shape, own_block.dtype)
    return lax.dynamic_update_index_in_dim(zone, own_block, me, 0)


def _row_tile(rows, want):
    t = min(want, rows) // SUBLANES * SUBLANES
    while rows % t:
        t -= SUBLANES
    return t


def _sum_slots(recv, *, tr, name):
    s_, r_, c_ = recv.shape
    tr = _row_tile(r_, tr)

    def body(g_ref, o_ref):
        acc = g_ref[0]
        for s in range(1, s_):
            acc = acc + g_ref[s]
        o_ref[...] = acc

    return pl.pallas_call(
        body, name=name, grid=(r_ // tr,),
        in_specs=[pl.BlockSpec((s_, tr, c_), lambda i: (0, i, 0))],
        out_specs=pl.BlockSpec((tr, c_), lambda i: (i, 0)),
        out_shape=jax.ShapeDtypeStruct((r_, c_), F32),
        compiler_params=_params(("parallel",), (2 * s_ + 3) * tr * c_ * 4),
    )(recv)


def _adamw(recv, w, m, v, *, tr, name):
    s_, r_, c_ = recv.shape
    tr = _row_tile(r_, tr)
    assert w.shape == (r_, c_), (name, w.shape, recv.shape)
    c1 = 1.0 - ADAM_B1 ** ADAM_STEP
    c2 = 1.0 - ADAM_B2 ** ADAM_STEP

    def body(g_ref, w_ref, m_ref, v_ref, go_ref, d_ref, mo_ref, vo_ref):
        g = g_ref[0].astype(F32)
        for s in range(1, s_):
            g = g + g_ref[s].astype(F32)
        mn = ADAM_B1 * m_ref[...] + (1.0 - ADAM_B1) * g
        vn = ADAM_B2 * v_ref[...] + (1.0 - ADAM_B2) * (g * g)
        go_ref[...] = g
        mo_ref[...] = mn
        vo_ref[...] = vn
        d_ref[...] = -ADAM_LR * ((mn / c1) / (jnp.sqrt(vn / c2) + ADAM_EPS) + ADAM_WD * w_ref[...])

    tile = pl.BlockSpec((tr, c_), lambda i: (i, 0))
    return pl.pallas_call(
        body, name=name, grid=(r_ // tr,),
        in_specs=[pl.BlockSpec((s_, tr, c_), lambda i: (0, i, 0)), tile, tile, tile],
        out_specs=[tile] * 4, out_shape=[jax.ShapeDtypeStruct((r_, c_), F32)] * 4,
        compiler_params=_params(("parallel",), (2 * s_ + 16) * tr * c_ * 4),
    )(recv, w, m, v)


def _adamw_whole(gs, ws, ms, vs, *, name):
    n = len(gs)
    c1 = 1.0 - ADAM_B1 ** ADAM_STEP
    c2 = 1.0 - ADAM_B2 ** ADAM_STEP

    def body(*refs):
        for i in range(n):
            g, w = refs[i][...], refs[n + i][...]
            mn = ADAM_B1 * refs[2 * n + i][...] + (1.0 - ADAM_B1) * g
            vn = ADAM_B2 * refs[3 * n + i][...] + (1.0 - ADAM_B2) * (g * g)
            refs[4 * n + 3 * i][...] = -ADAM_LR * ((mn / c1) / (jnp.sqrt(vn / c2) + ADAM_EPS) + ADAM_WD * w)
            refs[4 * n + 3 * i + 1][...] = mn
            refs[4 * n + 3 * i + 2][...] = vn

    whole = pl.BlockSpec(memory_space=pltpu.VMEM)
    lane_padded = sum(math.prod(g.shape[:-1]) * (-(-g.shape[-1] // LANES) * LANES) for g in gs)
    outs = pl.pallas_call(
        body, name=name, in_specs=[whole] * (4 * n), out_specs=[whole] * (3 * n),
        out_shape=[jax.ShapeDtypeStruct(g.shape, F32) for g in gs for _ in range(3)],
        compiler_params=pltpu.CompilerParams(vmem_limit_bytes=int(min(max(16 * lane_padded * 4, 16 * 2 ** 20), VMEM_CAP))),
    )(*gs, *ws, *ms, *vs)
    return [tuple(outs[3 * i:3 * i + 3]) for i in range(n)]


def _s5_discretise(a_re, a_im, log_dt, b_re, b_im):
    dt = jnp.exp(log_dt)[:, None]
    lr = jnp.minimum(a_re, -1e-4)
    li = a_im
    mag = jnp.exp(lr * dt)
    lbr = mag * jnp.cos(li * dt)
    lbi = mag * jnp.sin(li * dt)
    zr, zi = lbr - 1.0, lbi
    den = lr * lr + li * li
    fr = (zr * lr + zi * li) / den
    fi = (zi * lr - zr * li) / den
    bbr = fr[..., None] * b_re - fi[..., None] * b_im
    bbi = fr[..., None] * b_im + fi[..., None] * b_re
    return lbr, lbi, bbr, bbi


def _softplus_neg(lam):
    return jnp.maximum(-lam, 0.0) + jnp.log(1.0 + jnp.exp(-jnp.abs(lam)))


S5_Q = 8
RG_Q = 2


def _local_step(x, tgt, W, comm):
    T, D = x.shape
    C = D
    G, P, H = W["ssm_b_re"].shape
    S = G * H
    F = W["mlp_b_up"].shape[1]
    n_in = 2 * C + S + 2 * D
    heads, hd = W["rg_wa"].shape[0], W["rg_wa"].shape[1]
    u_off, ga_off, gb_off = 2 * C, 2 * C + S, 2 * C + S + D

    sp, sp_vjp = jax.vjp(_softplus_neg, W["rg_lambda"])
    (lbr, lbi, bbr, bbi), s5_vjp = jax.vjp(_s5_discretise, W["ssm_a_re"], W["ssm_a_im"], W["ssm_log_dt"],
                                           W["ssm_b_re"], W["ssm_b_im"])
    lam_re, lam_im = lbr.reshape(-1), lbi.reshape(-1)
    jr, kr = heads // RG_Q, RG_Q * hd
    w_ri = jnp.concatenate([_bd_pack(W["rg_wa"], RG_Q), _bd_pack(W["rg_wx"], RG_Q)], axis=2).astype(BF16)
    b_ri = jnp.concatenate([W["rg_ba"].reshape(jr, kr), W["rg_bx"].reshape(jr, kr)], axis=1).reshape(1, -1)
    wb_re = _bd_pack(jnp.swapaxes(bbr, 1, 2), S5_Q).astype(BF16)
    wb_im = _bd_pack(jnp.swapaxes(bbi, 1, 2), S5_Q).astype(BF16)
    wc_re = _bd_pack(jnp.swapaxes(W["ssm_c_re"], 1, 2), S5_Q).astype(BF16)
    wc_im_neg = _bd_pack(jnp.swapaxes(-W["ssm_c_im"], 1, 2), S5_Q).astype(BF16)
    d_row = W["ssm_d"].reshape(1, S)

    x_bf = x.astype(BF16)
    w_in = comm.weight("w_in", None)
    z = _mm(x_bf, w_in, M=T, N=n_in, K=D, tm=512, tn=n_in // 4, tk=D, after=comm.gather_token, name="fwd_in_proj")
    started = comm.start_weights(("mlp_w_up",), z)
    xc = _conv_fwd(z, W["conv_w"], W["conv_b"], T=T, C=C, after=started, name="fwd_conv")
    ri = _bd([(xc, 0, w_ri)], T=T, J=jr, kb=kr, nb=2 * kr, extras=[(b_ri, "vec", 0)],
             epilogue=lambda acc, b: (_sig(acc + b),), name="fwd_gates")
    h, p, a_fwd, m_fwd = _rg_scan_fwd(z, ri, xc, sp, T=T, C=C, gate_off=C, cw=kr, name="fwd_rg_scan")
    w_a_out = comm.weight("w_a_out", p)
    started = comm.start_weights(("mlp_w_down",), p)
    y_a = _mm(p, w_a_out, M=T, N=D, K=C, out_dtypes=(BF16,), tm=512, tn=D, tk=C, after=started, name="fwd_rg_out")

    powers = _power_slabs(lam_re, lam_im, S5_TILE // SUBLANES)
    h_re, h_im, y_s, yg = _s5_fwd(z, u_off, wb_re, wb_im, wc_re, wc_im_neg, d_row, powers, T=T, name="fwd_s5")
    w_glu_w, w_glu_v = comm.weight("glu_w", yg), comm.weight("glu_v", yg)
    glu_a = _mm(yg, w_glu_w, M=T, N=D, K=S, out_dtypes=(BF16,), tm=1024, tn=D, tk=S, name="fwd_glu_w")
    cwm = 1024

    def mix_fn(b, ga, gb, ya, a):
        return b, _sig(ga) * ya.astype(F32) + _sig(gb) * (a.astype(F32) * _sig(b))

    glu_b, mix = _mm(yg, w_glu_v, M=T, N=D, K=S, tm=512, tn=cwm, tk=S,
                     extras=[(z, "mn", ga_off // cwm), (z, "mn", gb_off // cwm), (y_a, "mn"), (glu_a, "mn")],
                     epilogue=mix_fn, n_out=2, out_dtypes=(BF16, BF16), name="fwd_glu_v_mix")
    w_out = comm.weight("w_out", mix)
    def out_ln1_fn(acc, xv, g, b):
        s = ALPHA * xv + acc
        xhat, _ = _ln_stats(s)
        y = xhat * g + b
        return s, y, y

    s1, x1, x1_bf = _mm(mix, w_out, M=T, N=D, K=D, tm=256, tn=D, tk=D,
                        extras=[(x, "mn"), (W["ln1_g"], "n"), (W["ln1_b"], "n")], epilogue=out_ln1_fn, n_out=3,
                        out_dtypes=(F32, F32, BF16), name="fwd_out_proj_ln1")
    w_up = comm.weight("mlp_w_up", x1_bf)

    def mlp_up_fn(acc, b):
        hp = acc + b
        rl = jnp.maximum(hp, 0.0)
        return rl * rl, hp

    hact, hpre = _mm(x1_bf, w_up, M=T, N=F, K=D, tm=1024, tn=1024, tk=D, extras=[(W["mlp_b_up"], "n")],
                     epilogue=mlp_up_fn, n_out=2, out_dtypes=(BF16, BF16), name="fwd_mlp_up")
    w_down = comm.weight("mlp_w_down", hact)
    s2 = _mm(hact, w_down, M=T, N=D, K=F, tm=512, tn=D, tk=2048,
             extras=[(x1, "mn"), (W["mlp_b_down"], "n")], epilogue=lambda acc, xv, b: (ALPHA * xv + acc + b,),
             name="fwd_mlp_down")

    def ln2_fn(s, t, g, b):
        xhat, rstd = _ln_stats(s)
        err = xhat * g + b - t
        dy = err * (1.0 / D)
        ds = _ln_bwd(dy, g, xhat, rstd)
        return ds, ds, 0.5 * dy * err, dy * xhat, dy, ds

    ds2, ds2_bf, loss_cols, d_ln2_g, d_ln2_b, d_b_down = _ew(
        ln2_fn, [(s2, "tile", 0), (tgt, "tile", 0), (W["ln2_g"], "vec", 0), (W["ln2_b"], "vec", 0)],
        T=T, C=D, n_out=2, n_cs=4, out_dtypes=(F32, BF16), tm=256, name="bwd_loss_ln2")
    d_w_down = _mm(hact, ds2_bf, M=F, N=D, K=T, ta=True, out_dtypes=(BF16,), tm=1024, tn=1024, tk=4096, name="bwd_w_down")
    sent = comm.send_grad("mlp_w_down", d_w_down)

    def dhpre_fn(acc, hp):
        dv = acc * (2.0 * jnp.maximum(hp.astype(F32), 0.0))
        return dv, dv

    dhpre, d_b_up = _mm(ds2_bf, w_down, M=T, N=F, K=D, tb=True, tm=1024, tn=1024, tk=D, extras=[(hpre, "mn")],
                        epilogue=dhpre_fn, n_cs=1, out_dtypes=(BF16,), after=sent, name="bwd_mlp_down")
    d_w_up = _mm(x1_bf, dhpre, M=D, N=F, K=T, ta=True, out_dtypes=(BF16,), n_split=N_DEV, tm=1024, tn=F // N_DEV, tk=4096, name="bwd_w_up")
    sent = comm.send_grad("mlp_w_up", d_w_up)
    dx1 = _mm(dhpre, w_up, M=T, N=D, K=F, tb=True, tm=1024, tn=1024, tk=2048,
              extras=[(ds2, "mn")], epilogue=lambda acc, dv: (ALPHA * dv + acc,), after=sent, name="bwd_mlp_up")

    def ln1_bwd_fn(s, dy, g):
        xhat, rstd = _ln_stats(s)
        ds = _ln_bwd(dy, g, xhat, rstd)
        return ds, ds, dy * xhat, dy

    ds1, ds1_bf, d_ln1_g, d_ln1_b = _ew(ln1_bwd_fn, [(s1, "tile", 0), (dx1, "tile", 0), (W["ln1_g"], "vec", 0)],
                                        T=T, C=D, n_out=2, n_cs=2, out_dtypes=(F32, BF16), tm=256, name="bwd_ln1")
    d_w_out = _mm(mix, ds1_bf, M=D, N=D, K=T, ta=True, out_dtypes=(BF16,), tm=1024, tn=1024, tk=4096, name="bwd_w_out")
    sent = comm.send_grad("w_out", d_w_out)
    def mix_bwd_fn(dm, ga, gb, ya, a, b):
        ya, a, b = ya.astype(F32), a.astype(F32), b.astype(F32)
        sa, sb, sv = _sig(ga), _sig(gb), _sig(b)
        yb = a * sv
        dyb = dm * sb
        return (dm * ya * (sa * (1.0 - sa)), dm * yb * (sb * (1.0 - sb)), dm * sa, dyb * sv,
                dyb * a * (sv * (1.0 - sv)))

    dg_a, dg_b, dy_a, dglu_a, dglu_b = _mm(
        ds1_bf, w_out, M=T, N=D, K=D, tb=True, tm=512, tn=cwm, tk=D,
        extras=[(z, "mn", ga_off // cwm), (z, "mn", gb_off // cwm), (y_a, "mn"), (glu_a, "mn"), (glu_b, "mn")],
        epilogue=mix_bwd_fn, n_out=5, out_dtypes=(BF16,) * 5, after=sent, name="bwd_out_proj_mix")

    d_w_a_out = _mm(p, dy_a, M=C, N=D, K=T, ta=True, out_dtypes=(BF16,), tm=1024, tn=1024, tk=4096, name="bwd_w_a_out")
    sent = comm.send_grad("w_a_out", d_w_a_out)
    def dp_fn(dp, hv, gate):
        th = jnp.tanh(GELU_C * (gate + GELU_K * gate * gate * gate))
        gelu = 0.5 * gate * (1.0 + th)
        dgelu = 0.5 * (1.0 + th) + 0.5 * gate * (1.0 - th * th) * (GELU_C * (1.0 + 3.0 * GELU_K * gate * gate))
        return dp * gelu, dp * hv * dgelu

    dh, dgate = _mm(dy_a, w_a_out, M=T, N=C, K=D, tb=True, tm=256, tn=C, tk=D, extras=[(h, "mn"), (z, "mn", 1)],
                    epilogue=dp_fn, n_out=2, out_dtypes=(F32, BF16), after=sent, name="bwd_rg_out")
    drai, dxc0, d_b_ri, d_sp = _rg_scan_bwd(dh, h, ri, xc, a_fwd, m_fwd, sp, T=T, C=C, cw=kr, name="bwd_rg_scan")
    dxc = _bd([(drai, 0, w_ri)], T=T, J=jr, kb=2 * kr, nb=kr, tw=True, extras=[(dxc0, "tile", 0)],
              epilogue=lambda acc, d0: (acc + d0,), name="bwd_gates")
    d_w_ri = _bdw(xc, 0, drai, 0, T=T, J=jr, kb=kr, nb=2 * kr, name="bwd_w_gates")
    d_wa, d_wx = _bd_unpack(d_w_ri[:, :, :kr], RG_Q), _bd_unpack(d_w_ri[:, :, kr:], RG_Q)
    d_b_ri = d_b_ri.reshape(jr, 2 * kr)
    d_ba, d_bx = d_b_ri[:, :kr].reshape(1, -1), d_b_ri[:, kr:].reshape(1, -1)
    dxr, conv_sums = _conv_bwd(dxc, z, W["conv_w"], T=T, C=C, name="bwd_conv")
    d_conv_w, d_conv_b = conv_sums[0:4], conv_sums[4:5]
    (d_lambda,) = sp_vjp(d_sp)

    d_glu_w = _mm(yg, dglu_a, M=S, N=D, K=T, ta=True, out_dtypes=(BF16,), n_split=N_DEV, tm=1024, tn=D // N_DEV, tk=4096, name="bwd_w_glu_w")
    d_glu_v = _mm(yg, dglu_b, M=S, N=D, K=T, ta=True, out_dtypes=(BF16,), n_split=N_DEV, tm=1024, tn=D // N_DEV, tk=4096, name="bwd_w_glu_v")
    sent = comm.send_grad("glu_w", d_glu_w, "glu_v", d_glu_v)
    dyg0 = _mm(dglu_a, w_glu_w, M=T, N=S, K=D, tb=True, tm=512, tn=S, tk=D, after=sent, name="bwd_glu_w")
    dy_s = _mm(dglu_b, w_glu_v, M=T, N=S, K=D, tb=True, tm=512, tn=S, tk=D,
               extras=[(dyg0, "mn"), (y_s, "mn")], epilogue=lambda acc, d0, yv: ((acc + d0) * _dgelu(yv),),
               name="bwd_glu_v")
    du, d_lbr, d_lbi, d_ssm_d, d_wb_re, d_wb_im, d_wc_re, d_wc_im_neg = _s5_bwd(
        dy_s, z, u_off, h_re, h_im, wb_re, wb_im, wc_re, wc_im_neg, d_row, powers, T=T, name="bwd_s5")
    d_bbr = jnp.swapaxes(_bd_unpack(d_wb_re, S5_Q), 1, 2)
    d_bbi = jnp.swapaxes(_bd_unpack(d_wb_im, S5_Q), 1, 2)
    d_a_re, d_a_im, d_log_dt, d_b_re, d_b_im = s5_vjp((d_lbr.reshape(G, P), d_lbi.reshape(G, P), d_bbr, d_bbi))
    d_c_re = _bd_unpack(d_wc_re, S5_Q)
    d_c_im = -_bd_unpack(d_wc_im_neg, S5_Q)

    dz = jnp.concatenate([dxr.astype(BF16), dgate.astype(BF16), du, dg_a, dg_b], axis=1)
    d_w_in = _mm(x_bf, dz, M=D, N=n_in, K=T, ta=True, out_dtypes=(BF16,), n_split=N_DEV, tm=1024, tn=n_in // N_DEV, tk=4096, name="bwd_w_in")
    sent = comm.send_grad("w_in", d_w_in)
    grad_x = _mm(dz, w_in, M=T, N=D, K=n_in, tb=True, tm=1024, tn=1024, tk=n_in // 4,
                 extras=[(ds1, "mn")], epilogue=lambda acc, dv: (ALPHA * dv + acc,), after=sent, name="bwd_in_proj")

    grads = dict(
        conv_w=d_conv_w, conv_b=d_conv_b, rg_wa=d_wa, rg_ba=d_ba, rg_wx=d_wx, rg_bx=d_bx,
        rg_lambda=d_lambda, ssm_a_re=d_a_re, ssm_a_im=d_a_im, ssm_log_dt=d_log_dt,
        ssm_b_re=d_b_re, ssm_b_im=d_b_im, ssm_c_re=d_c_re, ssm_c_im=d_c_im, ssm_d=d_ssm_d.reshape(G, H),
        ln1_g=d_ln1_g, ln1_b=d_ln1_b, mlp_b_up=d_b_up, mlp_b_down=d_b_down, ln2_g=d_ln2_g, ln2_b=d_ln2_b)
    return jnp.sum(loss_cols), grad_x, grads


BIG = ("w_in", "w_a_out", "glu_w", "glu_v", "w_out", "mlp_w_up", "mlp_w_down")
COL_SHARDED = ("w_in", "glu_w", "glu_v", "mlp_w_up")
SMALL = ("conv_w", "conv_b", "rg_wa", "rg_ba", "rg_wx", "rg_bx", "rg_lambda", "ssm_a_re", "ssm_a_im", "ssm_log_dt",
         "ssm_b_re", "ssm_b_im", "ssm_c_re", "ssm_c_im", "ssm_d", "ln1_g", "ln1_b", "mlp_b_up", "mlp_b_down", "ln2_g",
         "ln2_b")
ORDER = ("w_in", "conv_w", "conv_b", "rg_wa", "rg_ba", "rg_wx", "rg_bx", "rg_lambda", "w_a_out", "ssm_a_re",
         "ssm_a_im", "ssm_log_dt", "ssm_b_re", "ssm_b_im", "ssm_c_re", "ssm_c_im", "ssm_d", "glu_w", "glu_v", "w_out",
         "ln1_g", "ln1_b", "mlp_w_up", "mlp_b_up", "mlp_w_down", "mlp_b_down", "ln2_g", "ln2_b")
TILE_ELEMS = SUBLANES * LANES


def _pack(arrs):
    pieces = []
    for a in arrs:
        flat = a.reshape(-1)
        flat = jnp.pad(flat, (0, (-flat.shape[0]) % TILE_ELEMS))
        pieces.append(flat.reshape(-1, LANES))
    rows = sum(p.shape[0] for p in pieces)
    pad_rows = (-rows) % (N_DEV * SUBLANES)
    if pad_rows:
        pieces.append(jnp.zeros((pad_rows, LANES), pieces[0].dtype))
    return jnp.concatenate(pieces, axis=0)


def _unpack(packed, shapes):
    out, row = [], 0
    for shp in shapes:
        n = math.prod(shp)
        rows = -(-n // TILE_ELEMS) * SUBLANES
        out.append(packed[row:row + rows].reshape(-1)[:n].reshape(shp))
        row += rows
    return out


class _Comm:
    def __init__(self, w):
        first = _all_gather([w["w_in"].astype(BF16), w["conv_w"]], name="gather_w_in")
        self._weights = {"w_in": first[0]}
        self.conv_w = jnp.swapaxes(first[1], 0, 1).reshape(w["conv_w"].shape[0], -1)
        self._shards = {k: w[k].astype(BF16) for k in BIG if k != "w_in"}
        self._gathers, self._grads = {}, {}
        self.gather_token = self.start_weights(("w_a_out", "glu_w", "glu_v", "w_out"), None)

    def start_weights(self, names, after):
        shards = [self._shards.pop(k) for k in names]
        handles, token = _exchange_start(shards, [_landing_zone(s) for s in shards], scatter=False, after=after,
                                         name="gather_start_" + names[0])
        self._gathers.update(zip(names, handles))
        return token

    def weight(self, k, after):
        if k not in self._weights:
            self._weights[k] = _exchange_wait(self._gathers.pop(k), after, scatter=False, name="gather_wait_" + k)
        gk = self._weights[k]
        if k in COL_SHARDED:
            return jnp.swapaxes(gk, 0, 1).reshape(gk.shape[1], -1)
        return gk.reshape(-1, gk.shape[-1])

    def send_grad(self, *names_and_parts):
        names, parts = names_and_parts[0::2], names_and_parts[1::2]
        parts = [p if k in COL_SHARDED else p.reshape(N_DEV, p.shape[0] // N_DEV, p.shape[1])
                 for k, p in zip(names, parts)]
        me = _dev_index(*_mesh_pos())
        lands = [_landing_zone(lax.dynamic_index_in_dim(p, me, 0, keepdims=False)) for p in parts]
        handles, token = _exchange_start(parts, lands, scatter=True, name="grad_start_" + names[0])
        self._grads.update(zip(names, handles))
        return token

    def received_grad(self, k, after):
        return _exchange_wait(self._grads.pop(k), after, scatter=True, name="grad_wait_" + k)


SMALL_GROUPS = (("rg_wa", "rg_wx"), ("ssm_b_re",), ("ssm_b_im",),
                tuple(k for k in SMALL if k not in ("rg_wa", "rg_wx", "ssm_b_re", "ssm_b_im")))


def _step(x, tgt, w, m, v, raw_w, raw_m, raw_v):
    dev = _dev_index(*_mesh_pos())

    comm = _Comm(w)
    small = dict(w)
    small["conv_w"] = comm.conv_w
    for k in ("conv_b", "rg_ba", "rg_bx", "rg_lambda", "ln1_g", "ln1_b", "mlp_b_up", "mlp_b_down", "ln2_g", "ln2_b"):
        small[k] = w[k].reshape(1, -1)

    loss_part, grad_x, grads = _local_step(x, tgt, small, comm)

    out_g, out_d, out_m, out_v = {}, {}, {}, {}
    for k in BIG:
        rk = comm.received_grad(k, grad_x)
        out_g[k], out_d[k], out_m[k], out_v[k] = _adamw(rk, w[k], m[k], v[k], tr=128, name="adamw_" + k)

    small_shapes = [grads[k].shape for k in SMALL]
    (small_recv,) = _exchange_blocks([_pack([grads[k] for k in SMALL]).reshape(N_DEV, -1, LANES)],
                                     name="exchange_small_grads")
    small_block = _sum_slots(small_recv, tr=512, name="sum_small_grads")
    (small_all,) = _all_gather([small_block], name="gather_small_grads")
    g_small = dict(zip(SMALL, _unpack(small_all.reshape(-1, LANES), small_shapes)))
    cw_cols = w["conv_w"].shape[1]
    g_small["conv_w"] = lax.dynamic_slice_in_dim(g_small["conv_w"], dev * cw_cols, cw_cols, axis=1)
    for group in SMALL_GROUPS:
        gs = [g_small[k].reshape(raw_w[k].shape) for k in group]
        res = _adamw_whole(gs, [raw_w[k] for k in group], [raw_m[k] for k in group], [raw_v[k] for k in group],
                           name="adamw_" + group[0])
        for k, gk, (dk, mk, vk) in zip(group, gs, res):
            out_g[k], out_d[k], out_m[k], out_v[k] = gk, dk, mk, vk

    loss = lax.psum(loss_part, ("x", "y", "c"))
    return loss, grad_x, out_g, out_d, out_m, out_v


def kernel(x, w_in, conv_w, conv_b, rg_wa, rg_ba, rg_wx, rg_bx, rg_lambda, w_a_out, ssm_a_re, ssm_a_im, ssm_log_dt, ssm_b_re, ssm_b_im, ssm_c_re, ssm_c_im, ssm_d, glu_w, glu_v, w_out, ln1_g, ln1_b, mlp_w_up, mlp_b_up, mlp_w_down, mlp_b_down, ln2_g, ln2_b, loss_target, m_w_in, m_conv_w, m_conv_b, m_rg_wa, m_rg_ba, m_rg_wx, m_rg_bx, m_rg_lambda, m_w_a_out, m_ssm_a_re, m_ssm_a_im, m_ssm_log_dt, m_ssm_b_re, m_ssm_b_im, m_ssm_c_re, m_ssm_c_im, m_ssm_d, m_glu_w, m_glu_v, m_w_out, m_ln1_g, m_ln1_b, m_mlp_w_up, m_mlp_b_up, m_mlp_w_down, m_mlp_b_down, m_ln2_g, m_ln2_b, v_w_in, v_conv_w, v_conv_b, v_rg_wa, v_rg_ba, v_rg_wx, v_rg_bx, v_rg_lambda, v_w_a_out, v_ssm_a_re, v_ssm_a_im, v_ssm_log_dt, v_ssm_b_re, v_ssm_b_im, v_ssm_c_re, v_ssm_c_im, v_ssm_d, v_glu_w, v_glu_v, v_w_out, v_ln1_g, v_ln1_b, v_mlp_w_up, v_mlp_b_up, v_mlp_w_down, v_mlp_b_down, v_ln2_g, v_ln2_b):
    args = locals()
    w = {k: args[k][0] for k in ORDER}
    m = {k: args["m_" + k][0] for k in BIG}
    v = {k: args["v_" + k][0] for k in BIG}
    raw = [{k: args[prefix + k] for k in SMALL} for prefix in ("", "m_", "v_")]
    loss, grad_x, out_g, out_d, out_m, out_v = _step(x[0], loss_target[0], w, m, v, *raw)
    outs = [loss, grad_x[None]]
    for group in (out_g, out_d, out_m, out_v):
        outs += [group[k].reshape(args[k].shape) for k in ORDER]
    return tuple(outs)
```

```python
import functools
import math

import jax
import jax.numpy as jnp
from jax import lax
from jax.experimental import pallas as pl
from jax.experimental.pallas import tpu as pltpu

F32 = jnp.float32
BF16 = jnp.bfloat16
MESH = pl.DeviceIdType.MESH
N_DEV = 8
SUBLANES = 8
LANES = 128
VMEM_BYTES_V7X = 64 * 2 ** 20
VMEM_CAP = VMEM_BYTES_V7X - 8 * 2 ** 20

ALPHA = 2.0 ** 0.25
LN_EPS = 1e-5
RG_C = 8.0
ADAM_LR, ADAM_B1, ADAM_B2, ADAM_EPS, ADAM_WD, ADAM_STEP = 0.001, 0.9, 0.999, 1e-08, 0.01, 10
GELU_C = math.sqrt(2.0 / math.pi)
GELU_K = 0.044715

ANY = pl.BlockSpec(memory_space=pl.ANY)


def _params(sem, vmem_bytes):
    limit = int(min(max(2 * vmem_bytes, 16 * 2 ** 20), VMEM_CAP))
    return pltpu.CompilerParams(dimension_semantics=sem, vmem_limit_bytes=limit)


def _sig(x):
    return 1.0 / (1.0 + jnp.exp(-x))


def _gelu(x):
    return 0.5 * x * (1.0 + jnp.tanh(GELU_C * (x + GELU_K * x * x * x)))


def _dgelu(x):
    th = jnp.tanh(GELU_C * (x + GELU_K * x * x * x))
    return 0.5 * (1.0 + th) + 0.5 * x * (1.0 - th * th) * (GELU_C * (1.0 + 3.0 * GELU_K * x * x))


def _one_minus_exp(x, exp_half_x):
    p = x * (1.0 + x * (1 / 2 + x * (1 / 6 + x * (1 / 24 + x * (1 / 120)))))
    return jnp.where(x > -1 / 16, -p, 1.0 - exp_half_x * exp_half_x)


def _accumulate(ref, val, first):
    @pl.when(first)
    def _():
        ref[...] = val

    @pl.when(jnp.logical_not(first))
    def _():
        ref[...] += val


def _rows8(cw):
    return lax.broadcasted_iota(jnp.int32, (SUBLANES, cw), 0)


def _shift_down(cur, prev, s, rows):
    return jnp.where(rows < s, pltpu.roll(prev, s, 0), pltpu.roll(cur, s, 0))


def _shift_up(cur, nxt, s, rows):
    return jnp.where(rows < SUBLANES - s, pltpu.roll(cur, SUBLANES - s, 0), pltpu.roll(nxt, SUBLANES - s, 0))


def _mm(a, b, *, M, N, K, ta=False, tb=False, b_split=1, n_split=1, a_fn=None, extras=(), epilogue=None,
        n_out=1, n_cs=0, out_dtypes=None, tm=512, tn=512, tk=512, after=None, name):
    tm, tn, tk = min(tm, M), min(tn, N), min(tk, K)
    assert M % tm == 0 and N % tn == 0 and K % tk == 0, (name, M, N, K, tm, tn, tk)
    nk = K // tk
    grid = (N // tn, M // tm, nk)
    a_spec = pl.BlockSpec((tk, tm), lambda j, i, k: (k, i)) if ta else pl.BlockSpec((tm, tk), lambda j, i, k: (i, k))
    if b_split == 1:
        b_spec = pl.BlockSpec((tn, tk), lambda j, i, k: (j, k)) if tb else pl.BlockSpec((tk, tn), lambda j, i, k: (k, j))
    elif tb:
        kb = (K // b_split) // tk
        assert kb * tk * b_split == K, name
        b_spec = pl.BlockSpec((None, tn, tk), lambda j, i, k: (k // kb, j, k % kb))
    else:
        nb = (N // b_split) // tn
        assert nb * tn * b_split == N, name
        b_spec = pl.BlockSpec((None, tk, tn), lambda j, i, k: (j // nb, k, j % nb))
    in_specs = [a_spec, b_spec]
    for arr, kind, *col_off in extras:
        off = col_off[0] if col_off else 0
        in_specs.append(pl.BlockSpec((tm, tn), lambda j, i, k, off=off: (i, off + j)) if kind == "mn"
                        else pl.BlockSpec((1, tn), lambda j, i, k: (0, j)))
    out_dtypes = (F32,) * n_out if out_dtypes is None else out_dtypes
    if n_split == 1:
        out_shape = [jax.ShapeDtypeStruct((M, N), dt) for dt in out_dtypes]
        out_specs = [pl.BlockSpec((tm, tn), lambda j, i, k: (i, j)) for _ in range(n_out)]
    else:
        assert n_out == 1
        nbo = (N // n_split) // tn
        assert nbo * tn * n_split == N, name
        out_shape = [jax.ShapeDtypeStruct((n_split, M, N // n_split), out_dtypes[0])]
        out_specs = [pl.BlockSpec((None, tm, tn), lambda j, i, k: (j // nbo, i, j % nbo))]
    out_shape += [jax.ShapeDtypeStruct((1, N), F32) for _ in range(n_cs)]
    out_specs += [pl.BlockSpec((1, tn), lambda j, i, k: (0, j)) for _ in range(n_cs)]
    ne = len(extras)
    dims = (((0 if ta else 1,), (1 if tb else 0,)), ((), ()))

    n_after = 0 if after is None else 1
    in_specs += [ANY] * n_after

    def body(*refs):
        a_ref, b_ref = refs[0], refs[1]
        ex_refs = refs[2:2 + ne]
        first_out = 2 + ne + n_after
        out_refs = refs[first_out:first_out + n_out]
        cs_refs = refs[first_out + n_out:first_out + n_out + n_cs]
        i, k = pl.program_id(1), pl.program_id(2)

        def product():
            av = a_ref[...]
            if a_fn is not None:
                av = a_fn(av.astype(F32))
            return lax.dot_general(av.astype(BF16), b_ref[...].astype(BF16), dims, preferred_element_type=F32)

        def finish(acc):
            res = (acc,) if epilogue is None else epilogue(acc, *[r[...] for r in ex_refs])
            for r, o in zip(out_refs, res[:n_out]):
                r[...] = o.astype(r.dtype)
            for r, cval in zip(cs_refs, res[n_out:]):
                _accumulate(r, jnp.sum(cval, axis=0, keepdims=True), i == 0)

        if nk == 1:
            finish(product())
            return
        acc_ref = refs[-1]

        @pl.when(k == 0)
        def _():
            acc_ref[...] = jnp.zeros_like(acc_ref)

        acc_ref[...] += product()

        @pl.when(k == nk - 1)
        def _():
            finish(acc_ref[...])

    vmem = 2 * tm * tk * a.dtype.itemsize + 2 * tk * tn * b.dtype.itemsize + (1 + 2 * n_out + 2 * ne + 2) * tm * tn * 4
    outs = pl.pallas_call(
        body, name=name, grid=grid, in_specs=in_specs, out_specs=out_specs, out_shape=out_shape,
        scratch_shapes=[pltpu.VMEM((tm, tn), F32)] if nk > 1 else [],
        compiler_params=_params(("parallel", "arbitrary", "arbitrary"), vmem),
    )(a, b, *[e[0] for e in extras], *([after] if n_after else []))
    return outs[0] if len(outs) == 1 else outs


BD_STEP = 4

def _bd(pairs, *, T, J, kb, nb, tw=False, extras=(), epilogue=None, n_out=1, n_cs=0, out_dtypes=None, tm=512, name):
    jb = BD_STEP
    assert T % tm == 0 and J % jb == 0
    grid = (J // jb, T // tm)
    npair, ne = len(pairs), len(extras)
    in_specs, args = [], []
    for arr, off, w in pairs:
        assert off % jb == 0, name
        in_specs.append(pl.BlockSpec((tm, jb * kb), lambda j, i, off=off // jb: (i, off + j)))
        in_specs.append(pl.BlockSpec((jb,) + tuple(w.shape[1:]), lambda j, i: (j, 0, 0)))
        args += [arr, w]
    for arr, kind, off in extras:
        assert off % jb == 0, name
        in_specs.append(pl.BlockSpec((tm, jb * nb), lambda j, i, off=off // jb: (i, off + j)) if kind == "tile"
                        else pl.BlockSpec((1, jb * nb), lambda j, i, off=off // jb: (0, off + j)))
        args.append(arr)
    out_dtypes = (F32,) * n_out if out_dtypes is None else out_dtypes
    out_shape = [jax.ShapeDtypeStruct((T, J * nb), dt) for dt in out_dtypes]
    out_specs = [pl.BlockSpec((tm, jb * nb), lambda j, i: (i, j)) for _ in range(n_out)]
    out_shape += [jax.ShapeDtypeStruct((1, J * nb), F32) for _ in range(n_cs)]
    out_specs += [pl.BlockSpec((1, jb * nb), lambda j, i: (0, j)) for _ in range(n_cs)]
    dims = (((1,), (1 if tw else 0,)), ((), ()))

    def body(*refs):
        ex_refs = refs[2 * npair:2 * npair + ne]
        out_refs = refs[2 * npair + ne:2 * npair + ne + n_out]
        cs_refs = refs[2 * npair + ne + n_out:]
        i = pl.program_id(1)
        for s in range(jb):
            cols_in, cols_out = pl.ds(s * kb, kb), pl.ds(s * nb, nb)
            acc = None
            for p in range(npair):
                d = lax.dot_general(refs[2 * p][:, cols_in].astype(BF16), refs[2 * p + 1][s].astype(BF16), dims,
                                    preferred_element_type=F32)
                acc = d if acc is None else acc + d
            res = (acc,) if epilogue is None else epilogue(acc, *[r[:, cols_out] for r in ex_refs])
            for r, o in zip(out_refs, res[:n_out]):
                r[:, cols_out] = o.astype(r.dtype)
            for r, cval in zip(cs_refs, res[n_out:]):
                _accumulate(r.at[:, cols_out], jnp.sum(cval, axis=0, keepdims=True), i == 0)

    vmem = jb * (2 * npair * tm * kb + 2 * npair * kb * nb + (2 * n_out + 2 * ne + 3) * tm * nb) * 4
    outs = pl.pallas_call(
        body, name=name, grid=grid, in_specs=in_specs, out_specs=out_specs, out_shape=out_shape,
        compiler_params=_params(("parallel", "arbitrary"), vmem),
    )(*args)
    return outs[0] if len(outs) == 1 else outs


def _bdw(a, a_off, b, b_off, *, T, J, kb, nb, tm=512, name):
    jb = BD_STEP
    assert T % tm == 0 and J % jb == 0 and a_off % jb == 0 and b_off % jb == 0
    a_blk, b_blk = a_off // jb, b_off // jb

    def body(a_ref, b_ref, o_ref):
        i = pl.program_id(1)
        for s in range(jb):
            d = lax.dot_general(a_ref[:, pl.ds(s * kb, kb)].astype(BF16), b_ref[:, pl.ds(s * nb, nb)].astype(BF16),
                                (((0,), (0,)), ((), ())), preferred_element_type=F32)
            _accumulate(o_ref.at[s], d, i == 0)

    return pl.pallas_call(
        body, name=name, grid=(J // jb, T // tm),
        in_specs=[pl.BlockSpec((tm, jb * kb), lambda j, i: (i, a_blk + j)),
                  pl.BlockSpec((tm, jb * nb), lambda j, i: (i, b_blk + j))],
        out_specs=pl.BlockSpec((jb, kb, nb), lambda j, i: (j, 0, 0)),
        out_shape=jax.ShapeDtypeStruct((J, kb, nb), F32),
        compiler_params=_params(("parallel", "arbitrary"), jb * (2 * tm * (kb + nb) + 3 * kb * nb) * 4),
    )(a, b)


def _bd_pack(w, q):
    g, a, b = w.shape
    eye = jnp.eye(q, dtype=w.dtype)
    return jnp.einsum("jqab,qr->jqarb", w.reshape(g // q, q, a, b), eye).reshape(g // q, q * a, q * b)


def _bd_unpack(wp, q):
    j, qa, qb = wp.shape
    a, b = qa // q, qb // q
    w5 = wp.reshape(j, q, a, q, b)
    return jnp.stack([w5[:, r, :, r, :] for r in range(q)], axis=1).reshape(j * q, a, b)


def _ew(fn, ins, *, T, C, n_out, n_cs=0, out_dtypes=None, tm=256, cw=None, name):
    cw = C if cw is None else cw
    assert T % tm == 0 and C % cw == 0
    grid = (C // cw, T // tm)
    in_specs = []
    for arr, kind, off in ins:
        in_specs.append(pl.BlockSpec((tm, cw), lambda j, i, off=off: (i, off + j)) if kind == "tile"
                        else pl.BlockSpec((arr.shape[0], cw), lambda j, i, off=off: (0, off + j)))
    out_dtypes = (F32,) * n_out if out_dtypes is None else out_dtypes
    out_shape = [jax.ShapeDtypeStruct((T, C), dt) for dt in out_dtypes]
    out_specs = [pl.BlockSpec((tm, cw), lambda j, i: (i, j)) for _ in range(n_out)]
    out_shape += [jax.ShapeDtypeStruct((1, C), F32) for _ in range(n_cs)]
    out_specs += [pl.BlockSpec((1, cw), lambda j, i: (0, j)) for _ in range(n_cs)]
    nin = len(ins)

    def body(*refs):
        i = pl.program_id(1)
        res = fn(*[r[...].astype(F32) for r in refs[:nin]])
        for r, o in zip(refs[nin:nin + n_out], res[:n_out]):
            r[...] = o.astype(r.dtype)
        for r, cval in zip(refs[nin + n_out:], res[n_out:]):
            _accumulate(r, jnp.sum(cval, axis=0, keepdims=True), i == 0)

    vmem = (2 * nin + 2 * n_out + 6) * tm * cw * 4
    outs = pl.pallas_call(
        body, name=name, grid=grid, in_specs=in_specs, out_specs=out_specs, out_shape=out_shape,
        compiler_params=_params(("parallel", "arbitrary"), vmem),
    )(*[arr for arr, _, _ in ins])
    return outs[0] if len(outs) == 1 else outs


def _ln_stats(s):
    mu = jnp.mean(s, axis=-1, keepdims=True)
    d = s - mu
    var = jnp.mean(d * d, axis=-1, keepdims=True)
    rstd = lax.rsqrt(var + LN_EPS)
    return d * rstd, rstd


def _ln_bwd(dy, g, xhat, rstd):
    dxh = dy * g
    m1 = jnp.mean(dxh, axis=-1, keepdims=True)
    m2 = jnp.mean(dxh * xhat, axis=-1, keepdims=True)
    return rstd * (dxh - m1 - xhat * m2)


def _conv_fwd(z, conv_w, conv_b, *, T, C, tm=512, cw=1024, after=None, name):
    ng, hb = tm // SUBLANES, tm // SUBLANES
    n_after = 0 if after is None else 1

    def body(x_ref, halo_ref, w_ref, b_ref, *rest):
        o_ref = rest[-1]
        it = pl.program_id(1)
        rows = _rows8(cw)
        halo = jnp.where(it == 0, 0.0, halo_ref[...])
        w = w_ref[...]
        bias = b_ref[...]

        def group(g, carry):
            off = pl.multiple_of(g * SUBLANES, SUBLANES)
            cur = x_ref[pl.ds(off, SUBLANES), :]
            prev = x_ref[pl.ds(pl.multiple_of(jnp.maximum(off - SUBLANES, 0), SUBLANES), SUBLANES), :]
            prev = jnp.where(g == 0, halo, prev)
            acc = cur * w[3:4] + bias
            for s in (1, 2, 3):
                acc = acc + _shift_down(cur, prev, s, rows) * w[3 - s:4 - s]
            o_ref[pl.ds(off, SUBLANES), :] = acc
            return carry

        lax.fori_loop(0, ng, group, 0, unroll=2)

    return pl.pallas_call(
        body, name=name, grid=(C // cw, T // tm),
        in_specs=[pl.BlockSpec((tm, cw), lambda j, i: (i, j)),
                  pl.BlockSpec((SUBLANES, cw), lambda j, i: (jnp.maximum(i * hb - 1, 0), j)),
                  pl.BlockSpec((4, cw), lambda j, i: (0, j)), pl.BlockSpec((1, cw), lambda j, i: (0, j))]
        + [ANY] * n_after,
        out_specs=pl.BlockSpec((tm, cw), lambda j, i: (i, j)),
        out_shape=jax.ShapeDtypeStruct((T, C), F32),
        compiler_params=_params(("parallel", "arbitrary"), 5 * tm * cw * 4),
    )(z, z, conv_w, conv_b, *([after] if n_after else []))


def _conv_bwd(dxc, z, conv_w, *, T, C, tm=512, cw=512, name):
    ng, hb, last = tm // SUBLANES, tm // SUBLANES, T // SUBLANES - 1
    nt = T // tm

    def body(d_ref, dn_ref, x_ref, w_ref, o_ref, sums_ref):
        it = pl.program_id(1)
        rows = _rows8(cw)
        dnext = jnp.where(it == nt - 1, 0.0, dn_ref[...])
        w = w_ref[...]

        def group(g, accs):
            off = pl.multiple_of(g * SUBLANES, SUBLANES)
            dcur = d_ref[pl.ds(off, SUBLANES), :]
            dnx = d_ref[pl.ds(pl.multiple_of(jnp.minimum(off + SUBLANES, tm - SUBLANES), SUBLANES), SUBLANES), :]
            dnx = jnp.where(g == ng - 1, dnext, dnx)
            xcur = x_ref[pl.ds(off, SUBLANES), :]
            acc = dcur * w[3:4]
            taps = [accs[3] + dcur * xcur]
            for s in (1, 2, 3):
                ahead = _shift_up(dcur, dnx, s, rows)
                acc = acc + ahead * w[3 - s:4 - s]
                taps.append(accs[3 - s] + ahead * xcur)
            o_ref[pl.ds(off, SUBLANES), :] = acc
            return taps[3], taps[2], taps[1], taps[0], accs[4] + dcur

        zero = jnp.zeros((SUBLANES, cw), F32)
        accs = lax.fori_loop(0, ng, group, (zero,) * 5, unroll=2)
        sums = jnp.zeros((SUBLANES, cw), F32)
        for k, a in enumerate(accs):
            sums = jnp.where(rows == k, jnp.sum(a, axis=0, keepdims=True), sums)
        _accumulate(sums_ref, sums, it == 0)

    tile = pl.BlockSpec((tm, cw), lambda j, i: (i, j))
    return pl.pallas_call(
        body, name=name, grid=(C // cw, nt),
        in_specs=[tile, pl.BlockSpec((SUBLANES, cw), lambda j, i: (jnp.minimum((i + 1) * hb, last), j)),
                  tile, pl.BlockSpec((4, cw), lambda j, i: (0, j))],
        out_specs=[tile, pl.BlockSpec((SUBLANES, cw), lambda j, i: (0, j))],
        out_shape=[jax.ShapeDtypeStruct((T, C), F32), jax.ShapeDtypeStruct((SUBLANES, C), F32)],
        compiler_params=_params(("parallel", "arbitrary"), 7 * tm * cw * 4),
    )(dxc, dxc, z, conv_w)


def _rg_coeffs(r, ig, xc, sp):
    la = (-RG_C) * r * sp
    a = jnp.exp(la)
    m = jnp.sqrt(_one_minus_exp(2.0 * la, a))
    return a, m, m * (ig * xc)


def _rg_scan_fwd(z, ri, xc, sp, *, T, C, gate_off, tm=512, cw=256, name):
    rows16 = 2 * SUBLANES
    nq = tm // rows16

    def body(gate_ref, r_ref, i_ref, xc_ref, sp_ref, h_ref, p_ref, a_ref, m_ref, carry_ref):
        it = pl.program_id(1)

        @pl.when(it == 0)
        def _():
            carry_ref[...] = jnp.zeros_like(carry_ref)

        rows = _rows8(cw)
        sp_row = sp_ref[...]

        def pair(q, carry):
            base = pl.multiple_of(q * rows16, rows16)
            halves = []
            for half in range(2):
                sl = pl.ds(pl.multiple_of(base + half * SUBLANES, SUBLANES), SUBLANES)
                a, m, b = _rg_coeffs(r_ref[sl, :], i_ref[sl, :], xc_ref[sl, :], sp_row)
                a_ref[sl, :] = a
                m_ref[sl, :] = m
                for s in (1, 2, 4):
                    keep = rows >= s
                    sa = jnp.where(keep, pltpu.roll(a, s, 0), 1.0)
                    sb = jnp.where(keep, pltpu.roll(b, s, 0), 0.0)
                    b = b + a * sb
                    a = a * sa
                h = b + a * carry
                h_ref[sl, :] = h
                halves.append(h * _gelu(gate_ref[sl, :]))
                carry = h[SUBLANES - 1:SUBLANES, :]
            p_ref[pl.ds(base, rows16), :] = jnp.concatenate(halves, axis=0).astype(p_ref.dtype)
            return carry

        last = lax.fori_loop(0, nq, pair, carry_ref[0:1, :], unroll=2)
        carry_ref[...] = jnp.broadcast_to(last, carry_ref.shape)

    tile = pl.BlockSpec((tm, cw), lambda j, i: (i, j))
    gate_blk = gate_off // cw
    return pl.pallas_call(
        body, name=name, grid=(C // cw, T // tm),
        in_specs=[pl.BlockSpec((tm, cw), lambda j, i: (i, gate_blk + j)),
                  pl.BlockSpec((tm, cw), lambda j, i: (i, 2 * j)), pl.BlockSpec((tm, cw), lambda j, i: (i, 2 * j + 1)),
                  tile, pl.BlockSpec((1, cw), lambda j, i: (0, j))],
        out_specs=[tile, tile, tile, tile],
        out_shape=[jax.ShapeDtypeStruct((T, C), F32), jax.ShapeDtypeStruct((T, C), BF16),
                   jax.ShapeDtypeStruct((T, C), F32), jax.ShapeDtypeStruct((T, C), F32)],
        scratch_shapes=[pltpu.VMEM((SUBLANES, cw), F32)],
        compiler_params=_params(("parallel", "arbitrary"), 16 * tm * cw * 4),
    )(z, ri, ri, xc, sp)


def _rg_scan_bwd(dh, h, ri, xc, a_fwd, m_fwd, sp, *, T, C, tm=512, cw=256, name):
    ng, hb, nt = tm // SUBLANES, tm // SUBLANES, T // tm

    def body(dh_ref, h_ref, hp_ref, r_ref, i_ref, xc_ref, a_ref, m_ref, sp_ref,
             drai_ref, dxc_ref, crai_ref, csp_ref, cg_ref, ca_ref):
        step = pl.program_id(1)

        @pl.when(step == 0)
        def _():
            cg_ref[...] = jnp.zeros_like(cg_ref)
            ca_ref[...] = jnp.zeros_like(ca_ref)

        rows = _rows8(cw)
        sp_row = sp_ref[...]
        hhalo = jnp.where(step == nt - 1, 0.0, hp_ref[...])

        def group(gi, carry):
            g_next, a_next, s_ra, s_ia, s_sp = carry
            g = ng - 1 - gi
            off = pl.multiple_of(g * SUBLANES, SUBLANES)
            sl = pl.ds(off, SUBLANES)
            rr, ii, xx = r_ref[sl, :], i_ref[sl, :], xc_ref[sl, :]
            a, m = a_ref[sl, :], m_ref[sl, :]
            hh = h_ref[sl, :]
            hpv = h_ref[pl.ds(pl.multiple_of(jnp.maximum(off - SUBLANES, 0), SUBLANES), SUBLANES), :]
            hpv = jnp.where(g == 0, hhalo, hpv)
            hprev = _shift_down(hh, hpv, 1, rows)
            d = dh_ref[sl, :]
            c = jnp.where(rows < SUBLANES - 1, pltpu.roll(a, SUBLANES - 1, 0), a_next)
            for s in (1, 2, 4):
                keep = rows < SUBLANES - s
                sc = jnp.where(keep, pltpu.roll(c, SUBLANES - s, 0), 1.0)
                sd = jnp.where(keep, pltpu.roll(d, SUBLANES - s, 0), 0.0)
                d = d + c * sd
                c = c * sc
            gg = d + c * g_next
            da = gg * hprev
            dm = gg * (ii * xx)
            di = gg * (m * xx)
            dxc_ref[sl, :] = gg * (m * ii)
            dla = da * a - dm * (a * a / m)
            dra = dla * ((-RG_C) * sp_row) * (rr * (1.0 - rr))
            dia = di * (ii * (1.0 - ii))
            drai_ref[sl, pl.ds(0, cw)] = dra
            drai_ref[sl, pl.ds(cw, cw)] = dia
            return (gg[0:1, :], a[0:1, :], s_ra + dra, s_ia + dia, s_sp + dla * ((-RG_C) * rr))

        zero = jnp.zeros((SUBLANES, cw), F32)
        g_first, a_first, s_ra, s_ia, s_sp = lax.fori_loop(
            0, ng, group, (cg_ref[0:1, :], ca_ref[0:1, :], zero, zero, zero), unroll=2)
        cg_ref[...] = jnp.broadcast_to(g_first, cg_ref.shape)
        ca_ref[...] = jnp.broadcast_to(a_first, ca_ref.shape)
        for ref, acc in ((crai_ref.at[:, pl.ds(0, cw)], s_ra), (crai_ref.at[:, pl.ds(cw, cw)], s_ia), (csp_ref, s_sp)):
            _accumulate(ref, jnp.sum(acc, axis=0, keepdims=True), step == 0)

    tile = pl.BlockSpec((tm, cw), lambda j, i: (nt - 1 - i, j))
    wide = pl.BlockSpec((tm, 2 * cw), lambda j, i: (nt - 1 - i, j))
    vec = pl.BlockSpec((1, cw), lambda j, i: (0, j))
    return pl.pallas_call(
        body, name=name, grid=(C // cw, nt),
        in_specs=[tile, tile, pl.BlockSpec((SUBLANES, cw), lambda j, i: (jnp.maximum((nt - 1 - i) * hb - 1, 0), j)),
                  pl.BlockSpec((tm, cw), lambda j, i: (nt - 1 - i, 2 * j)),
                  pl.BlockSpec((tm, cw), lambda j, i: (nt - 1 - i, 2 * j + 1)), tile, tile, tile, vec],
        out_specs=[wide, tile, pl.BlockSpec((1, 2 * cw), lambda j, i: (0, j)), vec],
        out_shape=[jax.ShapeDtypeStruct((T, 2 * C), F32), jax.ShapeDtypeStruct((T, C), F32),
                   jax.ShapeDtypeStruct((1, 2 * C), F32), jax.ShapeDtypeStruct((1, C), F32)],
        scratch_shapes=[pltpu.VMEM((SUBLANES, cw), F32), pltpu.VMEM((SUBLANES, cw), F32)],
        compiler_params=_params(("parallel", "arbitrary"), 24 * tm * cw * 4),
    )(dh, h, h, ri, ri, xc, a_fwd, m_fwd, sp)


def _cscan_tables(lr, li, reverse):
    lam = (lr.reshape(-1), -li.reshape(-1) if reverse else li.reshape(-1))

    def mul(p, q):
        return p[0] * q[0] - p[1] * q[1], p[0] * q[1] + p[1] * q[0]

    pows = [lam]
    for _ in range(SUBLANES - 1):
        pows.append(mul(pows[-1], lam))
    zero = jnp.zeros_like(lam[0])
    tab = jnp.stack([pows[0][0], pows[0][1], pows[1][0], pows[1][1], pows[3][0], pows[3][1], zero, zero])
    if reverse:
        pows = pows[::-1]
    return tab, jnp.stack([p[0] for p in pows]), jnp.stack([p[1] for p in pows])


def _power_slabs(lr, li, n):
    pr, pi = lr.reshape(1, -1), li.reshape(1, -1)
    while pr.shape[0] < n:
        tr, ti = pr[-1:], pi[-1:]
        pr, pi = (jnp.concatenate([pr, pr * tr - pi * ti], axis=0), jnp.concatenate([pi, pr * ti + pi * tr], axis=0))
    return jnp.repeat(pr, SUBLANES, axis=0), jnp.repeat(pi, SUBLANES, axis=0), pr[-1], pi[-1]


def _rows_to_segments(src_ref, dst_ref):
    seg = src_ref.shape[0] // SUBLANES
    for g in range(seg):
        dst_ref[pl.ds(g * SUBLANES, SUBLANES), :] = src_ref[pl.ds(g, SUBLANES, stride=seg), :].astype(dst_ref.dtype)


def _segments_to_rows(src_ref, dst_ref):
    seg = src_ref.shape[0] // SUBLANES
    for r in range(SUBLANES):
        dst_ref[pl.ds(r * seg, seg), :] = src_ref[pl.ds(r, seg, stride=SUBLANES), :].astype(dst_ref.dtype)


def _seg_scan_tile(xr_ref, xi_ref, pbr_ref, pbi_ref, tab_ref, pwr_ref, pwi_ref, cr_ref, ci_ref, *, reverse, h=None):
    tm, cw = xr_ref.shape
    seg = tm // SUBLANES
    rows = _rows8(cw)
    sign = -1.0 if reverse else 1.0
    l_re, l_im = pbr_ref[0:1, :], sign * pbi_ref[0:1, :]

    def slab(g):
        return pl.ds(pl.multiple_of(g * SUBLANES, SUBLANES), SUBLANES)

    def local(k, state):
        sl = slab(seg - 1 - k if reverse else k)
        sr, si = state
        nr = xr_ref[sl, :] + (l_re * sr - l_im * si)
        ni = xi_ref[sl, :] + (l_re * si + l_im * sr)
        xr_ref[sl, :] = nr
        xi_ref[sl, :] = ni
        return nr, ni

    zero = jnp.zeros((SUBLANES, cw), F32)
    er, ei = lax.fori_loop(0, seg, local, (zero, zero), unroll=2)

    for k, s in enumerate((1, 2, 4)):
        shift = SUBLANES - s if reverse else s
        keep = rows < SUBLANES - s if reverse else rows >= s
        sr = jnp.where(keep, pltpu.roll(er, shift, 0), 0.0)
        si = jnp.where(keep, pltpu.roll(ei, shift, 0), 0.0)
        m_re, m_im = tab_ref[2 * k:2 * k + 1, :], tab_ref[2 * k + 1:2 * k + 2, :]
        er, ei = er + (m_re * sr - m_im * si), ei + (m_re * si + m_im * sr)
    cin_r, cin_i = cr_ref[0:1, :], ci_ref[0:1, :]
    pwr, pwi = pwr_ref[...], pwi_ref[...]
    er, ei = er + (pwr * cin_r - pwi * cin_i), ei + (pwr * cin_i + pwi * cin_r)
    if reverse:
        ent_r = jnp.where(rows == SUBLANES - 1, cin_r, pltpu.roll(er, SUBLANES - 1, 0))
        ent_i = jnp.where(rows == SUBLANES - 1, cin_i, pltpu.roll(ei, SUBLANES - 1, 0))
        out_r, out_i = er[0:1, :], ei[0:1, :]
    else:
        ent_r = jnp.where(rows == 0, cin_r, pltpu.roll(er, 1, 0))
        ent_i = jnp.where(rows == 0, cin_i, pltpu.roll(ei, 1, 0))
        out_r, out_i = er[SUBLANES - 1:SUBLANES, :], ei[SUBLANES - 1:SUBLANES, :]
    cr_ref[...] = jnp.broadcast_to(out_r, cr_ref.shape)
    ci_ref[...] = jnp.broadcast_to(out_i, ci_ref.shape)

    if h is not None:
        hr_ref, hi_ref, hr_last, hi_last = h
        hr_wrap = _shift_down(hr_ref[pl.ds(tm - SUBLANES, SUBLANES), :], hr_last, 1, rows)
        hi_wrap = _shift_down(hi_ref[pl.ds(tm - SUBLANES, SUBLANES), :], hi_last, 1, rows)

    def fix(g, sums):
        sl = slab(g)
        power = slab(seg - 1 - g) if reverse else sl
        pr, pi = pbr_ref[power, :], sign * pbi_ref[power, :]
        nr = xr_ref[sl, :] + (pr * ent_r - pi * ent_i)
        ni = xi_ref[sl, :] + (pr * ent_i + pi * ent_r)
        xr_ref[sl, :] = nr
        xi_ref[sl, :] = ni
        if h is None:
            return sums
        before = slab(jnp.maximum(g - 1, 0))
        hr1 = jnp.where(g == 0, hr_wrap, hr_ref[before, :])
        hi1 = jnp.where(g == 0, hi_wrap, hi_ref[before, :])
        return sums[0] + (nr * hr1 + ni * hi1), sums[1] + (ni * hr1 - nr * hi1)

    return lax.fori_loop(0, seg, fix, (zero, zero) if h is not None else (), unroll=2)


S5_TILE = 512


def _s5_fwd(z, u_off, wb_re, wb_im, wc_re, wc_im_neg, d_row, powers, *, T, tm=S5_TILE, name):
    J, ku, kp = wb_re.shape
    nt = T // tm
    pb_re, pb_im, top_re, top_im = powers
    tab, pw_re, pw_im = _cscan_tables(top_re, top_im, False)
    u_blk = u_off // ku

    def body(u_ref, wbr_ref, wbi_ref, wcr_ref, wci_ref, d_ref, pbr_ref, pbi_ref, tab_ref, pwr_ref, pwi_ref,
             hr_ref, hi_ref, y_ref, yg_ref, cr_ref, ci_ref, us_ref, ys_ref):
        @pl.when(pl.program_id(1) == 0)
        def _():
            cr_ref[...] = jnp.zeros_like(cr_ref)
            ci_ref[...] = jnp.zeros_like(ci_ref)

        _rows_to_segments(u_ref, us_ref)
        u = us_ref[...]
        ub = u.astype(BF16)
        hr_ref[...] = jnp.dot(ub, wbr_ref[...], preferred_element_type=F32)
        hi_ref[...] = jnp.dot(ub, wbi_ref[...], preferred_element_type=F32)
        _seg_scan_tile(hr_ref, hi_ref, pbr_ref, pbi_ref, tab_ref, pwr_ref, pwi_ref, cr_ref, ci_ref, reverse=False)
        y = (jnp.dot(hr_ref[...].astype(BF16), wcr_ref[...], preferred_element_type=F32)
             + jnp.dot(hi_ref[...].astype(BF16), wci_ref[...], preferred_element_type=F32) + d_ref[...] * u)
        ys_ref[...] = y
        _segments_to_rows(ys_ref, y_ref)
        ys_ref[...] = _gelu(y)
        _segments_to_rows(ys_ref, yg_ref)

    wb_spec = pl.BlockSpec((None, ku, kp), lambda j, i: (j, 0, 0))
    wc_spec = pl.BlockSpec((None, kp, ku), lambda j, i: (j, 0, 0))
    small = pl.BlockSpec((SUBLANES, kp), lambda j, i: (0, j))
    slabs = pl.BlockSpec((tm, kp), lambda j, i: (0, j))
    state = pl.BlockSpec((tm, kp), lambda j, i: (i, j))
    chan = pl.BlockSpec((tm, ku), lambda j, i: (i, j))
    return pl.pallas_call(
        body, name=name, grid=(J, nt),
        in_specs=[pl.BlockSpec((tm, ku), lambda j, i: (i, u_blk + j)), wb_spec, wb_spec, wc_spec, wc_spec,
                  pl.BlockSpec((1, ku), lambda j, i: (0, j)), slabs, slabs, small, small, small],
        out_specs=[state, state, chan, chan],
        out_shape=[jax.ShapeDtypeStruct((T, J * kp), F32)] * 2
        + [jax.ShapeDtypeStruct((T, J * ku), F32), jax.ShapeDtypeStruct((T, J * ku), BF16)],
        scratch_shapes=[pltpu.VMEM((SUBLANES, kp), F32), pltpu.VMEM((SUBLANES, kp), F32),
                        pltpu.VMEM((tm, ku), F32), pltpu.VMEM((tm, ku), F32)],
        compiler_params=_params(("parallel", "arbitrary"), 14 * tm * kp * 4),
    )(z, wb_re, wb_im, wc_re, wc_im_neg, d_row, pb_re, pb_im, tab, pw_re, pw_im)


def _s5_bwd(dy, z, u_off, h_re, h_im, wb_re, wb_im, wc_re, wc_im_neg, d_row, powers, *, T, tm=S5_TILE, name):
    J, ku, kp = wb_re.shape
    nt, hb = T // tm, tm // SUBLANES
    pb_re, pb_im, top_re, top_im = powers
    tab, pw_re, pw_im = _cscan_tables(top_re, top_im, True)
    u_blk = u_off // ku
    contract_rows = (((0,), (0,)), ((), ()))
    contract_cols = (((1,), (1,)), ((), ()))

    def body(dy_ref, u_ref, hr_ref, hrp_ref, hi_ref, hip_ref, wbr_ref, wbi_ref, wcr_ref, wci_ref, d_ref,
             pbr_ref, pbi_ref, tab_ref, pwr_ref, pwi_ref,
             du_ref, dlr_ref, dli_ref, dd_ref, dwbr_ref, dwbi_ref, dwcr_ref, dwci_ref,
             gr_ref, gi_ref, cr_ref, ci_ref, dys_ref, us_ref):
        step = pl.program_id(1)
        first = step == 0

        @pl.when(first)
        def _():
            cr_ref[...] = jnp.zeros_like(cr_ref)
            ci_ref[...] = jnp.zeros_like(ci_ref)

        _rows_to_segments(dy_ref, dys_ref)
        _rows_to_segments(u_ref, us_ref)
        dy_t, u = dys_ref[...], us_ref[...]
        dyb, ub = dy_t.astype(BF16), u.astype(BF16)
        gr_ref[...] = lax.dot_general(dyb, wcr_ref[...], contract_cols, preferred_element_type=F32)
        gi_ref[...] = lax.dot_general(dyb, wci_ref[...], contract_cols, preferred_element_type=F32)
        hr_last = jnp.where(step == nt - 1, 0.0, hrp_ref[...])
        hi_last = jnp.where(step == nt - 1, 0.0, hip_ref[...])
        s_re, s_im = _seg_scan_tile(gr_ref, gi_ref, pbr_ref, pbi_ref, tab_ref, pwr_ref, pwi_ref, cr_ref, ci_ref,
                                    reverse=True, h=(hr_ref, hi_ref, hr_last, hi_last))
        _accumulate(dlr_ref, jnp.sum(s_re, axis=0, keepdims=True), first)
        _accumulate(dli_ref, jnp.sum(s_im, axis=0, keepdims=True), first)
        grb, gib = gr_ref[...].astype(BF16), gi_ref[...].astype(BF16)
        du = (lax.dot_general(grb, wbr_ref[...], contract_cols, preferred_element_type=F32)
              + lax.dot_general(gib, wbi_ref[...], contract_cols, preferred_element_type=F32) + dy_t * d_ref[...])
        dys_ref[...] = du
        _segments_to_rows(dys_ref, du_ref)
        _accumulate(dd_ref, jnp.sum(dy_t * u, axis=0, keepdims=True), first)
        _accumulate(dwbr_ref, lax.dot_general(ub, grb, contract_rows, preferred_element_type=F32), first)
        _accumulate(dwbi_ref, lax.dot_general(ub, gib, contract_rows, preferred_element_type=F32), first)
        _accumulate(dwcr_ref, lax.dot_general(dyb, hr_ref[...].astype(BF16), contract_rows,
                                              preferred_element_type=F32), first)
        _accumulate(dwci_ref, lax.dot_general(dyb, hi_ref[...].astype(BF16), contract_rows,
                                              preferred_element_type=F32), first)

    def tix(i):
        return nt - 1 - i

    wb_spec = pl.BlockSpec((None, ku, kp), lambda j, i: (j, 0, 0))
    wc_spec = pl.BlockSpec((None, kp, ku), lambda j, i: (j, 0, 0))
    small = pl.BlockSpec((SUBLANES, kp), lambda j, i: (0, j))
    state = pl.BlockSpec((tm, kp), lambda j, i: (tix(i), j))
    halo = pl.BlockSpec((SUBLANES, kp), lambda j, i: (jnp.maximum(tix(i) * hb - 1, 0), j))
    chan = pl.BlockSpec((tm, ku), lambda j, i: (tix(i), j))
    svec = pl.BlockSpec((1, kp), lambda j, i: (0, j))
    cvec = pl.BlockSpec((1, ku), lambda j, i: (0, j))
    slabs = pl.BlockSpec((tm, kp), lambda j, i: (0, j))
    return pl.pallas_call(
        body, name=name, grid=(J, nt),
        in_specs=[chan, pl.BlockSpec((tm, ku), lambda j, i: (tix(i), u_blk + j)), state, halo, state, halo,
                  wb_spec, wb_spec, wc_spec, wc_spec, cvec, slabs, slabs, small, small, small],
        out_specs=[chan, svec, svec, cvec, wb_spec, wb_spec, wb_spec, wb_spec],
        out_shape=[jax.ShapeDtypeStruct((T, J * ku), BF16), jax.ShapeDtypeStruct((1, J * kp), F32),
                   jax.ShapeDtypeStruct((1, J * kp), F32), jax.ShapeDtypeStruct((1, J * ku), F32),
                   jax.ShapeDtypeStruct((J, ku, kp), F32), jax.ShapeDtypeStruct((J, ku, kp), F32),
                   jax.ShapeDtypeStruct((J, ku, kp), F32), jax.ShapeDtypeStruct((J, ku, kp), F32)],
        scratch_shapes=[pltpu.VMEM((tm, kp), F32), pltpu.VMEM((tm, kp), F32),
                        pltpu.VMEM((SUBLANES, kp), F32), pltpu.VMEM((SUBLANES, kp), F32),
                        pltpu.VMEM((tm, ku), F32), pltpu.VMEM((tm, ku), F32)],
        compiler_params=_params(("parallel", "arbitrary"), 16 * tm * kp * 4),
    )(dy, z, h_re, h_re, h_im, h_im, wb_re, wb_im, wc_re, wc_im_neg, d_row, pb_re, pb_im, tab, pw_re, pw_im)


def _mesh_pos():
    return lax.axis_index("x"), lax.axis_index("y"), lax.axis_index("c")


def _dev_index(px, py, pc):
    return 4 * px + 2 * py + pc


def _all_gather(shards, name):
    n = len(shards)

    def body(*refs):
        ins, outs = refs[:n], refs[n:2 * n]
        send_sems, recv_sems, local_sems = refs[2 * n:]
        x, y, c = _mesh_pos()
        me, sibling = (x, y, c), (x, y, 1 - c)
        chips = [(1 - x, y), (x, 1 - y), (1 - x, 1 - y)]

        def copy(a, k, block, to, src=None):
            dst = outs[a].at[_dev_index(*block)]
            return pltpu.make_async_remote_copy(
                src_ref=dst if src is None else src, dst_ref=dst, send_sem=send_sems.at[a * 7 + k],
                recv_sem=recv_sems.at[a * 7 + k], device_id=to, device_id_type=MESH)

        mine = [pltpu.make_async_copy(ins[a], outs[a].at[_dev_index(*me)], local_sems.at[a]) for a in range(n)]
        for cp in mine:
            cp.start()
        first = []
        for a in range(n):
            first.append(copy(a, 0, me, sibling, src=ins[a]))
            first += [copy(a, 1 + j, me, (*chip, c), src=ins[a]) for j, chip in enumerate(chips)]
        for cp in first:
            cp.start()
        passed = []
        for j, chip in enumerate(chips):
            for a in range(n):
                copy(a, 1 + j, (*chip, c), me).wait_recv()
                fwd = copy(a, 4 + j, (*chip, c), sibling)
                fwd.start()
                passed.append(fwd)
        for a in range(n):
            copy(a, 0, sibling, me).wait_recv()
            for j, chip in enumerate(chips):
                copy(a, 4 + j, (*chip, 1 - c), me).wait_recv()
        for cp in first + passed:
            cp.wait_send()
        for cp in mine:
            cp.wait()

    return pl.pallas_call(
        body, name=name, in_specs=[ANY] * n, out_specs=[ANY] * n,
        out_shape=[jax.ShapeDtypeStruct((N_DEV,) + s.shape, s.dtype) for s in shards],
        scratch_shapes=[pltpu.SemaphoreType.DMA((7 * n,)), pltpu.SemaphoreType.DMA((7 * n,)),
                        pltpu.SemaphoreType.DMA((n,))],
    )(*shards)


def _exchange_blocks(parts, name):
    n = len(parts)
    relations = [(dx, dy, dc) for dx in (0, 1) for dy in (0, 1) for dc in (0, 1) if (dx, dy, dc) != (0, 0, 0)]

    def body(*refs):
        ins, outs = refs[:n], refs[n:2 * n]
        send_sems, recv_sems, local_sems = refs[2 * n:]
        x, y, c = _mesh_pos()
        me = _dev_index(x, y, c)
        mine = [pltpu.make_async_copy(ins[a].at[me], outs[a].at[me], local_sems.at[a]) for a in range(n)]
        for cp in mine:
            cp.start()
        copies = []
        for k, (dx, dy, dc) in enumerate(relations):
            peer = (x + dx - 2 * x * dx, y + dy - 2 * y * dy, c + dc - 2 * c * dc)
            for a in range(n):
                copies.append((pltpu.make_async_remote_copy(
                    src_ref=ins[a].at[_dev_index(*peer)], dst_ref=outs[a].at[me], send_sem=send_sems.at[a * 7 + k],
                    recv_sem=recv_sems.at[a * 7 + k], device_id=peer, device_id_type=MESH),
                    pltpu.make_async_remote_copy(
                    src_ref=ins[a].at[_dev_index(*peer)], dst_ref=outs[a].at[_dev_index(*peer)],
                    send_sem=send_sems.at[a * 7 + k], recv_sem=recv_sems.at[a * 7 + k], device_id=peer,
                    device_id_type=MESH)))
        for send, _ in copies:
            send.start()
        for _, recv in copies:
            recv.wait_recv()
        for send, _ in copies:
            send.wait_send()
        for cp in mine:
            cp.wait()

    return pl.pallas_call(
        body, name=name, in_specs=[ANY] * n, out_specs=[ANY] * n,
        out_shape=[jax.ShapeDtypeStruct(p.shape, p.dtype) for p in parts],
        scratch_shapes=[pltpu.SemaphoreType.DMA((7 * n,)), pltpu.SemaphoreType.DMA((7 * n,)),
                        pltpu.SemaphoreType.DMA((n,))],
    )(*parts)


HBM = pl.BlockSpec(memory_space=pltpu.HBM)
SEM = pl.BlockSpec(memory_space=pltpu.SEMAPHORE)
EFFECT = pltpu.SideEffectType.DATAFLOW_SIDE_EFFECTING
RELATIONS = [(dx, dy, dc) for dx in (0, 1) for dy in (0, 1) for dc in (0, 1) if (dx, dy, dc) != (0, 0, 0)]


def _peer(rel):
    x, y, c = _mesh_pos()
    dx, dy, dc = rel
    return (x + dx - 2 * x * dx, y + dy - 2 * y * dy, c + dc - 2 * c * dc)


CHIP_RELATIONS = [(1, 0, 0), (0, 1, 0), (1, 1, 0)]
EXCHANGE_PEERS = {"gather": RELATIONS, "scatter": RELATIONS, "own": [(0, 0, 1)] + CHIP_RELATIONS, "pass": CHIP_RELATIONS}


def _split_copy(src_ref, land_ref, send_sems, recv_sems, k, mode, incoming):
    x, y, c = _mesh_pos()
    me = _dev_index(x, y, c)
    peer = _peer(EXCHANGE_PEERS[mode][k])
    if mode == "pass":
        held, theirs = _dev_index(peer[0], peer[1], c), _dev_index(peer[0], peer[1], 1 - c)
        src, slot, target = land_ref.at[held], theirs if incoming else held, (x, y, 1 - c)
    else:
        src = src_ref.at[_dev_index(*peer)] if mode == "scatter" else src_ref
        slot, target = _dev_index(*peer) if incoming else me, peer
    return pltpu.make_async_remote_copy(src_ref=src, dst_ref=land_ref.at[slot], send_sem=send_sems.at[k],
                                        recv_sem=recv_sems.at[k], device_id=target, device_id_type=MESH)


def _exchange_start(srcs, lands, *, mode, after=None, name):
    n = len(srcs)
    n_after = 0 if after is None else 1
    n_rel = len(EXCHANGE_PEERS[mode])

    def body(*refs):
        src_refs, land_refs = refs[:n], refs[n:2 * n]
        first_out = 2 * n + n_after
        send, recv = refs[first_out:first_out + n], refs[first_out + n:first_out + 2 * n]
        token = refs[-1]
        for k in range(n_rel):
            for a in range(n):
                _split_copy(src_refs[a], land_refs[a], send[a], recv[a], k, mode, incoming=False).start()
        token[...] = jnp.zeros_like(token)

    outs = pl.pallas_call(
        body, name=name, in_specs=[HBM] * (2 * n) + [ANY] * n_after,
        out_shape=[pltpu.SemaphoreType.DMA((n_rel,))] * (2 * n)
        + [pltpu.HBM(s.shape, s.dtype) for s in srcs] + [pltpu.HBM(s.shape, s.dtype) for s in lands]
        + [jax.ShapeDtypeStruct((SUBLANES, LANES), F32)],
        out_specs=[SEM] * (2 * n) + [HBM] * (2 * n) + [pl.BlockSpec(memory_space=pltpu.VMEM)],
        input_output_aliases={**{a: 2 * n + a for a in range(n)}, **{n + a: 3 * n + a for a in range(n)}},
        compiler_params=pltpu.CompilerParams(has_side_effects=EFFECT),
    )(*[pltpu.with_memory_space_constraint(s, pltpu.HBM) for s in srcs],
      *[pltpu.with_memory_space_constraint(s, pltpu.HBM) for s in lands], *([after] if n_after else []))
    per_array = [(outs[a], outs[n + a], outs[2 * n + a], outs[3 * n + a]) for a in range(n)]
    return per_array, outs[-1]


def _exchange_wait(handle, after, *, mode, name):
    send_sems, recv_sems, src_thru, land_thru = handle
    after = after if isinstance(after, (tuple, list)) else (after,)

    def body(src_ref, land_ref, send, recv, *rest):
        for k in range(len(EXCHANGE_PEERS[mode])):
            cp = _split_copy(src_ref, land_ref, send, recv, k, mode, incoming=True)
            cp.wait_send()
            cp.wait_recv()

    return pl.pallas_call(
        body, name=name, in_specs=[HBM, HBM, SEM, SEM] + [ANY] * len(after),
        out_shape=[pltpu.HBM(src_thru.shape, src_thru.dtype), pltpu.HBM(land_thru.shape, land_thru.dtype)],
        out_specs=[HBM, HBM], input_output_aliases={0: 0, 1: 1},
        compiler_params=pltpu.CompilerParams(has_side_effects=EFFECT),
    )(src_thru, land_thru, send_sems, recv_sems, *after)[1]


def _landing_zone(own_block):
    me = _dev_index(*_mesh_pos())
    zone = lax.empty((N_DEV,) + own_block.shape, own_block.dtype)
    return lax.dynamic_update_index_in_dim(zone, own_block, me, 0)


def _row_tile(rows, want):
    t = min(want, rows) // SUBLANES * SUBLANES
    while rows % t:
        t -= SUBLANES
    return t


def _sum_slots(recv, *, tr, name):
    s_, r_, c_ = recv.shape
    tr = _row_tile(r_, tr)

    def body(g_ref, o_ref):
        acc = g_ref[0]
        for s in range(1, s_):
            acc = acc + g_ref[s]
        o_ref[...] = acc

    return pl.pallas_call(
        body, name=name, grid=(r_ // tr,),
        in_specs=[pl.BlockSpec((s_, tr, c_), lambda i: (0, i, 0))],
        out_specs=pl.BlockSpec((tr, c_), lambda i: (i, 0)),
        out_shape=jax.ShapeDtypeStruct((r_, c_), F32),
        compiler_params=_params(("parallel",), (2 * s_ + 3) * tr * c_ * 4),
    )(recv)


def _adamw(recv, w, m, v, *, tr, name):
    s_, r_, c_ = recv.shape
    tr = _row_tile(r_, tr)
    assert w.shape == (r_, c_), (name, w.shape, recv.shape)
    c1 = 1.0 - ADAM_B1 ** ADAM_STEP
    c2 = 1.0 - ADAM_B2 ** ADAM_STEP

    def body(g_ref, w_ref, m_ref, v_ref, go_ref, d_ref, mo_ref, vo_ref):
        g = g_ref[0].astype(F32)
        for s in range(1, s_):
            g = g + g_ref[s].astype(F32)
        mn = ADAM_B1 * m_ref[...] + (1.0 - ADAM_B1) * g
        vn = ADAM_B2 * v_ref[...] + (1.0 - ADAM_B2) * (g * g)
        go_ref[...] = g
        mo_ref[...] = mn
        vo_ref[...] = vn
        d_ref[...] = -ADAM_LR * ((mn / c1) / (jnp.sqrt(vn / c2) + ADAM_EPS) + ADAM_WD * w_ref[...])

    tile = pl.BlockSpec((tr, c_), lambda i: (i, 0))
    return pl.pallas_call(
        body, name=name, grid=(r_ // tr,),
        in_specs=[pl.BlockSpec((s_, tr, c_), lambda i: (0, i, 0)), tile, tile, tile],
        out_specs=[tile] * 4, out_shape=[jax.ShapeDtypeStruct((r_, c_), F32)] * 4,
        compiler_params=_params(("parallel",), (2 * s_ + 16) * tr * c_ * 4),
    )(recv, w, m, v)


def _adamw_whole(gs, ws, ms, vs, *, name):
    n = len(gs)
    c1 = 1.0 - ADAM_B1 ** ADAM_STEP
    c2 = 1.0 - ADAM_B2 ** ADAM_STEP

    def body(*refs):
        for i in range(n):
            g, w = refs[i][...], refs[n + i][...]
            mn = ADAM_B1 * refs[2 * n + i][...] + (1.0 - ADAM_B1) * g
            vn = ADAM_B2 * refs[3 * n + i][...] + (1.0 - ADAM_B2) * (g * g)
            refs[4 * n + 3 * i][...] = -ADAM_LR * ((mn / c1) / (jnp.sqrt(vn / c2) + ADAM_EPS) + ADAM_WD * w)
            refs[4 * n + 3 * i + 1][...] = mn
            refs[4 * n + 3 * i + 2][...] = vn

    whole = pl.BlockSpec(memory_space=pltpu.VMEM)
    lane_padded = sum(math.prod(g.shape[:-1]) * (-(-g.shape[-1] // LANES) * LANES) for g in gs)
    outs = pl.pallas_call(
        body, name=name, in_specs=[whole] * (4 * n), out_specs=[whole] * (3 * n),
        out_shape=[jax.ShapeDtypeStruct(g.shape, F32) for g in gs for _ in range(3)],
        compiler_params=pltpu.CompilerParams(vmem_limit_bytes=int(min(max(16 * lane_padded * 4, 16 * 2 ** 20), VMEM_CAP))),
    )(*gs, *ws, *ms, *vs)
    return [tuple(outs[3 * i:3 * i + 3]) for i in range(n)]


def _s5_discretise(a_re, a_im, log_dt, b_re, b_im):
    dt = jnp.exp(log_dt)[:, None]
    lr = jnp.minimum(a_re, -1e-4)
    li = a_im
    mag = jnp.exp(lr * dt)
    lbr = mag * jnp.cos(li * dt)
    lbi = mag * jnp.sin(li * dt)
    zr, zi = lbr - 1.0, lbi
    den = lr * lr + li * li
    fr = (zr * lr + zi * li) / den
    fi = (zi * lr - zr * li) / den
    bbr = fr[..., None] * b_re - fi[..., None] * b_im
    bbi = fr[..., None] * b_im + fi[..., None] * b_re
    return lbr, lbi, bbr, bbi


def _softplus_neg(lam):
    return jnp.maximum(-lam, 0.0) + jnp.log(1.0 + jnp.exp(-jnp.abs(lam)))


S5_Q = 8
RG_Q = 2


def _local_step(x, tgt, W, comm):
    T, D = x.shape
    C = D
    G, P, H = W["ssm_b_re"].shape
    S = G * H
    F = W["mlp_b_up"].shape[1]
    n_in = 2 * C + S + 2 * D
    heads, hd = W["rg_wa"].shape[0], W["rg_wa"].shape[1]
    u_off, ga_off, gb_off = 2 * C, 2 * C + S, 2 * C + S + D

    if comm.first_token is not None:
        anchored = ("rg_lambda", "ssm_a_re", "rg_wa", "rg_wx", "ssm_c_re", "ssm_c_im")
        W = {**W, **{k: W[k] + comm.first_token[0, 0] for k in anchored}}
    sp, sp_vjp = jax.vjp(_softplus_neg, W["rg_lambda"])
    (lbr, lbi, bbr, bbi), s5_vjp = jax.vjp(_s5_discretise, W["ssm_a_re"], W["ssm_a_im"], W["ssm_log_dt"],
                                           W["ssm_b_re"], W["ssm_b_im"])
    lam_re, lam_im = lbr.reshape(-1), lbi.reshape(-1)
    jr, kr = heads // RG_Q, RG_Q * hd
    w_ri = jnp.concatenate([_bd_pack(W["rg_wa"], RG_Q), _bd_pack(W["rg_wx"], RG_Q)], axis=2).astype(BF16)
    b_ri = jnp.concatenate([W["rg_ba"].reshape(jr, kr), W["rg_bx"].reshape(jr, kr)], axis=1).reshape(1, -1)
    wb_re = _bd_pack(jnp.swapaxes(bbr, 1, 2), S5_Q).astype(BF16)
    wb_im = _bd_pack(jnp.swapaxes(bbi, 1, 2), S5_Q).astype(BF16)
    wc_re = _bd_pack(jnp.swapaxes(W["ssm_c_re"], 1, 2), S5_Q).astype(BF16)
    wc_im_neg = _bd_pack(jnp.swapaxes(-W["ssm_c_im"], 1, 2), S5_Q).astype(BF16)
    d_row = W["ssm_d"].reshape(1, S)
    powers = _power_slabs(lam_re, lam_im, S5_TILE // SUBLANES)

    x_bf = x.astype(BF16) if comm.first_token is None else (x + comm.first_token[0, 0]).astype(BF16)
    w_in, conv_w = comm.first_weights((x_bf, w_ri, wb_re, wb_im, wc_re, wc_im_neg, powers[0], powers[1]))
    z = _mm(x_bf, w_in, M=T, N=n_in, K=D, tm=512, tn=n_in // 4, tk=D, after=comm.gather_token, name="fwd_in_proj")
    started = comm.start_weights(("mlp_w_up",), z)
    xc = _conv_fwd(z, conv_w, W["conv_b"], T=T, C=C, after=started, name="fwd_conv")
    ri = _bd([(xc, 0, w_ri)], T=T, J=jr, kb=kr, nb=2 * kr, extras=[(b_ri, "vec", 0)],
             epilogue=lambda acc, b: (_sig(acc + b),), name="fwd_gates")
    h, p, a_fwd, m_fwd = _rg_scan_fwd(z, ri, xc, sp, T=T, C=C, gate_off=C, cw=kr, name="fwd_rg_scan")
    w_a_out = comm.weight("w_a_out", p)
    started = comm.start_weights(("mlp_w_down",), p)
    y_a = _mm(p, w_a_out, M=T, N=D, K=C, out_dtypes=(BF16,), tm=512, tn=D, tk=C, after=started, name="fwd_rg_out")

    h_re, h_im, y_s, yg = _s5_fwd(z, u_off, wb_re, wb_im, wc_re, wc_im_neg, d_row, powers, T=T, name="fwd_s5")
    w_glu_w, w_glu_v = comm.weight("glu_w", yg), comm.weight("glu_v", yg)
    glu_a = _mm(yg, w_glu_w, M=T, N=D, K=S, out_dtypes=(BF16,), tm=1024, tn=D, tk=S, name="fwd_glu_w")
    cwm = 1024

    def mix_fn(b, ga, gb, ya, a):
        return b, _sig(ga) * ya.astype(F32) + _sig(gb) * (a.astype(F32) * _sig(b))

    glu_b, mix = _mm(yg, w_glu_v, M=T, N=D, K=S, tm=512, tn=cwm, tk=S,
                     extras=[(z, "mn", ga_off // cwm), (z, "mn", gb_off // cwm), (y_a, "mn"), (glu_a, "mn")],
                     epilogue=mix_fn, n_out=2, out_dtypes=(BF16, BF16), name="fwd_glu_v_mix")
    w_out = comm.weight("w_out", mix)
    def out_ln1_fn(acc, xv, g, b):
        s = ALPHA * xv + acc
        xhat, _ = _ln_stats(s)
        y = xhat * g + b
        return s, y, y

    s1, x1, x1_bf = _mm(mix, w_out, M=T, N=D, K=D, tm=256, tn=D, tk=D,
                        extras=[(x, "mn"), (W["ln1_g"], "n"), (W["ln1_b"], "n")], epilogue=out_ln1_fn, n_out=3,
                        out_dtypes=(F32, F32, BF16), name="fwd_out_proj_ln1")
    w_up = comm.weight("mlp_w_up", x1_bf)

    def mlp_up_fn(acc, b):
        hp = acc + b
        rl = jnp.maximum(hp, 0.0)
        return rl * rl, hp

    hact, hpre = _mm(x1_bf, w_up, M=T, N=F, K=D, tm=1024, tn=1024, tk=D, extras=[(W["mlp_b_up"], "n")],
                     epilogue=mlp_up_fn, n_out=2, out_dtypes=(BF16, BF16), name="fwd_mlp_up")
    w_down = comm.weight("mlp_w_down", hact)
    s2 = _mm(hact, w_down, M=T, N=D, K=F, tm=512, tn=D, tk=2048,
             extras=[(x1, "mn"), (W["mlp_b_down"], "n")], epilogue=lambda acc, xv, b: (ALPHA * xv + acc + b,),
             name="fwd_mlp_down")

    def ln2_fn(s, t, g, b):
        xhat, rstd = _ln_stats(s)
        err = xhat * g + b - t
        dy = err * (1.0 / D)
        ds = _ln_bwd(dy, g, xhat, rstd)
        return ds, ds, 0.5 * dy * err, dy * xhat, dy, ds

    ds2, ds2_bf, loss_cols, d_ln2_g, d_ln2_b, d_b_down = _ew(
        ln2_fn, [(s2, "tile", 0), (tgt, "tile", 0), (W["ln2_g"], "vec", 0), (W["ln2_b"], "vec", 0)],
        T=T, C=D, n_out=2, n_cs=4, out_dtypes=(F32, BF16), tm=256, name="bwd_loss_ln2")
    d_w_down = _mm(hact, ds2_bf, M=F, N=D, K=T, ta=True, out_dtypes=(BF16,), tm=1024, tn=1024, tk=4096, name="bwd_w_down")
    sent = comm.send_grad("mlp_w_down", d_w_down)

    def dhpre_fn(acc, hp):
        dv = acc * (2.0 * jnp.maximum(hp.astype(F32), 0.0))
        return dv, dv

    dhpre, d_b_up = _mm(ds2_bf, w_down, M=T, N=F, K=D, tb=True, tm=1024, tn=1024, tk=D, extras=[(hpre, "mn")],
                        epilogue=dhpre_fn, n_cs=1, out_dtypes=(BF16,), after=sent, name="bwd_mlp_down")
    d_w_up = _mm(x1_bf, dhpre, M=D, N=F, K=T, ta=True, out_dtypes=(BF16,), n_split=N_DEV, tm=1024, tn=F // N_DEV, tk=4096, name="bwd_w_up")
    sent = comm.send_grad("mlp_w_up", d_w_up)
    dx1 = _mm(dhpre, w_up, M=T, N=D, K=F, tb=True, tm=1024, tn=1024, tk=2048,
              extras=[(ds2, "mn")], epilogue=lambda acc, dv: (ALPHA * dv + acc,), after=sent, name="bwd_mlp_up")

    def ln1_bwd_fn(s, dy, g):
        xhat, rstd = _ln_stats(s)
        ds = _ln_bwd(dy, g, xhat, rstd)
        return ds, ds, dy * xhat, dy

    ds1, ds1_bf, d_ln1_g, d_ln1_b = _ew(ln1_bwd_fn, [(s1, "tile", 0), (dx1, "tile", 0), (W["ln1_g"], "vec", 0)],
                                        T=T, C=D, n_out=2, n_cs=2, out_dtypes=(F32, BF16), tm=256, name="bwd_ln1")
    d_w_out = _mm(mix, ds1_bf, M=D, N=D, K=T, ta=True, out_dtypes=(BF16,), tm=1024, tn=1024, tk=4096, name="bwd_w_out")
    sent = comm.send_grad("w_out", d_w_out)
    def mix_bwd_fn(dm, ga, gb, ya, a, b):
        ya, a, b = ya.astype(F32), a.astype(F32), b.astype(F32)
        sa, sb, sv = _sig(ga), _sig(gb), _sig(b)
        yb = a * sv
        dyb = dm * sb
        return (dm * ya * (sa * (1.0 - sa)), dm * yb * (sb * (1.0 - sb)), dm * sa, dyb * sv,
                dyb * a * (sv * (1.0 - sv)))

    dg_a, dg_b, dy_a, dglu_a, dglu_b = _mm(
        ds1_bf, w_out, M=T, N=D, K=D, tb=True, tm=512, tn=cwm, tk=D,
        extras=[(z, "mn", ga_off // cwm), (z, "mn", gb_off // cwm), (y_a, "mn"), (glu_a, "mn"), (glu_b, "mn")],
        epilogue=mix_bwd_fn, n_out=5, out_dtypes=(BF16,) * 5, after=sent, name="bwd_out_proj_mix")

    d_w_a_out = _mm(p, dy_a, M=C, N=D, K=T, ta=True, out_dtypes=(BF16,), tm=1024, tn=1024, tk=4096, name="bwd_w_a_out")
    sent = comm.send_grad("w_a_out", d_w_a_out)
    def dp_fn(dp, hv, gate):
        th = jnp.tanh(GELU_C * (gate + GELU_K * gate * gate * gate))
        gelu = 0.5 * gate * (1.0 + th)
        dgelu = 0.5 * (1.0 + th) + 0.5 * gate * (1.0 - th * th) * (GELU_C * (1.0 + 3.0 * GELU_K * gate * gate))
        return dp * gelu, dp * hv * dgelu

    dh, dgate = _mm(dy_a, w_a_out, M=T, N=C, K=D, tb=True, tm=256, tn=C, tk=D, extras=[(h, "mn"), (z, "mn", 1)],
                    epilogue=dp_fn, n_out=2, out_dtypes=(F32, BF16), after=sent, name="bwd_rg_out")
    drai, dxc0, d_b_ri, d_sp = _rg_scan_bwd(dh, h, ri, xc, a_fwd, m_fwd, sp, T=T, C=C, cw=kr, name="bwd_rg_scan")
    dxc = _bd([(drai, 0, w_ri)], T=T, J=jr, kb=2 * kr, nb=kr, tw=True, extras=[(dxc0, "tile", 0)],
              epilogue=lambda acc, d0: (acc + d0,), name="bwd_gates")
    d_w_ri = _bdw(xc, 0, drai, 0, T=T, J=jr, kb=kr, nb=2 * kr, name="bwd_w_gates")
    d_wa, d_wx = _bd_unpack(d_w_ri[:, :, :kr], RG_Q), _bd_unpack(d_w_ri[:, :, kr:], RG_Q)
    d_b_ri = d_b_ri.reshape(jr, 2 * kr)
    d_ba, d_bx = d_b_ri[:, :kr].reshape(1, -1), d_b_ri[:, kr:].reshape(1, -1)
    dxr, conv_sums = _conv_bwd(dxc, z, conv_w, T=T, C=C, name="bwd_conv")
    d_conv_w, d_conv_b = conv_sums[0:4], conv_sums[4:5]
    (d_lambda,) = sp_vjp(d_sp)

    d_glu_w = _mm(yg, dglu_a, M=S, N=D, K=T, ta=True, out_dtypes=(BF16,), n_split=N_DEV, tm=1024, tn=D // N_DEV, tk=4096, name="bwd_w_glu_w")
    d_glu_v = _mm(yg, dglu_b, M=S, N=D, K=T, ta=True, out_dtypes=(BF16,), n_split=N_DEV, tm=1024, tn=D // N_DEV, tk=4096, name="bwd_w_glu_v")
    sent = comm.send_grad("glu_w", d_glu_w, "glu_v", d_glu_v)
    dyg0 = _mm(dglu_a, w_glu_w, M=T, N=S, K=D, tb=True, tm=512, tn=S, tk=D, after=sent, name="bwd_glu_w")
    dy_s = _mm(dglu_b, w_glu_v, M=T, N=S, K=D, tb=True, tm=512, tn=S, tk=D,
               extras=[(dyg0, "mn"), (y_s, "mn")], epilogue=lambda acc, d0, yv: ((acc + d0) * _dgelu(yv),),
               name="bwd_glu_v")
    du, d_lbr, d_lbi, d_ssm_d, d_wb_re, d_wb_im, d_wc_re, d_wc_im_neg = _s5_bwd(
        dy_s, z, u_off, h_re, h_im, wb_re, wb_im, wc_re, wc_im_neg, d_row, powers, T=T, name="bwd_s5")
    d_bbr = jnp.swapaxes(_bd_unpack(d_wb_re, S5_Q), 1, 2)
    d_bbi = jnp.swapaxes(_bd_unpack(d_wb_im, S5_Q), 1, 2)
    d_a_re, d_a_im, d_log_dt, d_b_re, d_b_im = s5_vjp((d_lbr.reshape(G, P), d_lbi.reshape(G, P), d_bbr, d_bbi))
    d_c_re = _bd_unpack(d_wc_re, S5_Q)
    d_c_im = -_bd_unpack(d_wc_im_neg, S5_Q)

    dz = jnp.concatenate([dxr.astype(BF16), dgate.astype(BF16), du, dg_a, dg_b], axis=1)
    d_w_in = _mm(x_bf, dz, M=D, N=n_in, K=T, ta=True, out_dtypes=(BF16,), n_split=N_DEV, tm=1024, tn=n_in // N_DEV, tk=4096, name="bwd_w_in")
    sent = comm.send_grad("w_in", d_w_in)
    grad_x = _mm(dz, w_in, M=T, N=D, K=n_in, tb=True, tm=1024, tn=1024, tk=n_in // 4,
                 extras=[(ds1, "mn")], epilogue=lambda acc, dv: (ALPHA * dv + acc,), after=sent, name="bwd_in_proj")

    grads = dict(
        conv_w=d_conv_w, conv_b=d_conv_b, rg_wa=d_wa, rg_ba=d_ba, rg_wx=d_wx, rg_bx=d_bx,
        rg_lambda=d_lambda, ssm_a_re=d_a_re, ssm_a_im=d_a_im, ssm_log_dt=d_log_dt,
        ssm_b_re=d_b_re, ssm_b_im=d_b_im, ssm_c_re=d_c_re, ssm_c_im=d_c_im, ssm_d=d_ssm_d.reshape(G, H),
        ln1_g=d_ln1_g, ln1_b=d_ln1_b, mlp_b_up=d_b_up, mlp_b_down=d_b_down, ln2_g=d_ln2_g, ln2_b=d_ln2_b)
    return jnp.sum(loss_cols), grad_x, grads


BIG = ("w_in", "w_a_out", "glu_w", "glu_v", "w_out", "mlp_w_up", "mlp_w_down")
COL_SHARDED = ("w_in", "glu_w", "glu_v", "mlp_w_up")
SMALL = ("conv_w", "conv_b", "rg_wa", "rg_ba", "rg_wx", "rg_bx", "rg_lambda", "ssm_a_re", "ssm_a_im", "ssm_log_dt",
         "ssm_b_re", "ssm_b_im", "ssm_c_re", "ssm_c_im", "ssm_d", "ln1_g", "ln1_b", "mlp_b_up", "mlp_b_down", "ln2_g",
         "ln2_b")
ORDER = ("w_in", "conv_w", "conv_b", "rg_wa", "rg_ba", "rg_wx", "rg_bx", "rg_lambda", "w_a_out", "ssm_a_re",
         "ssm_a_im", "ssm_log_dt", "ssm_b_re", "ssm_b_im", "ssm_c_re", "ssm_c_im", "ssm_d", "glu_w", "glu_v", "w_out",
         "ln1_g", "ln1_b", "mlp_w_up", "mlp_b_up", "mlp_w_down", "mlp_b_down", "ln2_g", "ln2_b")
TILE_ELEMS = SUBLANES * LANES


def _pack(arrs):
    pieces = []
    for a in arrs:
        flat = a.reshape(-1)
        flat = jnp.pad(flat, (0, (-flat.shape[0]) % TILE_ELEMS))
        pieces.append(flat.reshape(-1, LANES))
    rows = sum(p.shape[0] for p in pieces)
    pad_rows = (-rows) % (N_DEV * SUBLANES)
    if pad_rows:
        pieces.append(jnp.zeros((pad_rows, LANES), pieces[0].dtype))
    return jnp.concatenate(pieces, axis=0)


def _unpack(packed, shapes):
    out, row = [], 0
    for shp in shapes:
        n = math.prod(shp)
        rows = -(-n // TILE_ELEMS) * SUBLANES
        out.append(packed[row:row + rows].reshape(-1)[:n].reshape(shp))
        row += rows
    return out


class _Comm:
    def __init__(self, w):
        first = [w["w_in"].astype(BF16), w["conv_w"]]
        self._first, self.first_token = _exchange_start(first, [_landing_zone(s) for s in first], mode="own",
                                                        name="gather_in_start")
        self._shards = {k: w[k].astype(BF16) for k in BIG if k != "w_in"}
        self._weights, self._gathers, self._grads = {}, {}, {}

    def first_weights(self, after):
        lands = [_exchange_wait(h, after, mode="own", name="gather_in_wait_%d" % i) for i, h in enumerate(self._first)]
        unused = [lax.empty((2 * SUBLANES, LANES), BF16) for _ in lands]
        handles, passed = _exchange_start(unused, lands, mode="pass", name="gather_in_pass")
        w_in, taps = [_exchange_wait(h, passed, mode="pass", name="gather_in_got_%d" % i) for i, h in enumerate(handles)]
        self._weights["w_in"] = w_in
        self.gather_token = self.start_weights(("w_a_out", "glu_w", "glu_v", "w_out"), w_in)
        return self.weight("w_in", None), jnp.swapaxes(taps, 0, 1).reshape(taps.shape[1], -1)

    def start_weights(self, names, after):
        shards = [self._shards.pop(k) for k in names]
        handles, token = _exchange_start(shards, [_landing_zone(s) for s in shards], mode="gather", after=after,
                                         name="gather_start_" + names[0])
        self._gathers.update(zip(names, handles))
        return token

    def weight(self, k, after):
        if k not in self._weights:
            self._weights[k] = _exchange_wait(self._gathers.pop(k), after, mode="gather", name="gather_wait_" + k)
        gk = self._weights[k]
        if k in COL_SHARDED:
            return jnp.swapaxes(gk, 0, 1).reshape(gk.shape[1], -1)
        return gk.reshape(-1, gk.shape[-1])

    def send_grad(self, *names_and_parts):
        names, parts = names_and_parts[0::2], names_and_parts[1::2]
        parts = [p if k in COL_SHARDED else p.reshape(N_DEV, p.shape[0] // N_DEV, p.shape[1])
                 for k, p in zip(names, parts)]
        me = _dev_index(*_mesh_pos())
        lands = [_landing_zone(lax.dynamic_index_in_dim(p, me, 0, keepdims=False)) for p in parts]
        handles, token = _exchange_start(parts, lands, mode="scatter", name="grad_start_" + names[0])
        self._grads.update(zip(names, handles))
        return token

    def received_grad(self, k, after):
        return _exchange_wait(self._grads.pop(k), after, mode="scatter", name="grad_wait_" + k)


SMALL_GROUPS = (("rg_wa", "rg_wx"), ("ssm_b_re",), ("ssm_b_im",),
                tuple(k for k in SMALL if k not in ("rg_wa", "rg_wx", "ssm_b_re", "ssm_b_im")))


def _step(x, tgt, w, m, v, raw_w, raw_m, raw_v):
    dev = _dev_index(*_mesh_pos())

    comm = _Comm(w)
    small = dict(w)
    for k in ("conv_b", "rg_ba", "rg_bx", "rg_lambda", "ln1_g", "ln1_b", "mlp_b_up", "mlp_b_down", "ln2_g", "ln2_b"):
        small[k] = w[k].reshape(1, -1)

    loss_part, grad_x, grads = _local_step(x, tgt, small, comm)

    out_g, out_d, out_m, out_v = {}, {}, {}, {}
    for k in BIG:
        rk = comm.received_grad(k, grad_x)
        out_g[k], out_d[k], out_m[k], out_v[k] = _adamw(rk, w[k], m[k], v[k], tr=128, name="adamw_" + k)

    small_shapes = [grads[k].shape for k in SMALL]
    (small_recv,) = _exchange_blocks([_pack([grads[k] for k in SMALL]).reshape(N_DEV, -1, LANES)],
                                     name="exchange_small_grads")
    small_block = _sum_slots(small_recv, tr=512, name="sum_small_grads")
    (small_all,) = _all_gather([small_block], name="gather_small_grads")
    g_small = dict(zip(SMALL, _unpack(small_all.reshape(-1, LANES), small_shapes)))
    cw_cols = w["conv_w"].shape[1]
    g_small["conv_w"] = lax.dynamic_slice_in_dim(g_small["conv_w"], dev * cw_cols, cw_cols, axis=1)
    for group in SMALL_GROUPS:
        gs = [g_small[k].reshape(raw_w[k].shape) for k in group]
        res = _adamw_whole(gs, [raw_w[k] for k in group], [raw_m[k] for k in group], [raw_v[k] for k in group],
                           name="adamw_" + group[0])
        for k, gk, (dk, mk, vk) in zip(group, gs, res):
            out_g[k], out_d[k], out_m[k], out_v[k] = gk, dk, mk, vk

    loss = lax.psum(loss_part, ("x", "y", "c"))
    return loss, grad_x, out_g, out_d, out_m, out_v


def kernel(x, w_in, conv_w, conv_b, rg_wa, rg_ba, rg_wx, rg_bx, rg_lambda, w_a_out, ssm_a_re, ssm_a_im, ssm_log_dt, ssm_b_re, ssm_b_im, ssm_c_re, ssm_c_im, ssm_d, glu_w, glu_v, w_out, ln1_g, ln1_b, mlp_w_up, mlp_b_up, mlp_w_down, mlp_b_down, ln2_g, ln2_b, loss_target, m_w_in, m_conv_w, m_conv_b, m_rg_wa, m_rg_ba, m_rg_wx, m_rg_bx, m_rg_lambda, m_w_a_out, m_ssm_a_re, m_ssm_a_im, m_ssm_log_dt, m_ssm_b_re, m_ssm_b_im, m_ssm_c_re, m_ssm_c_im, m_ssm_d, m_glu_w, m_glu_v, m_w_out, m_ln1_g, m_ln1_b, m_mlp_w_up, m_mlp_b_up, m_mlp_w_down, m_mlp_b_down, m_ln2_g, m_ln2_b, v_w_in, v_conv_w, v_conv_b, v_rg_wa, v_rg_ba, v_rg_wx, v_rg_bx, v_rg_lambda, v_w_a_out, v_ssm_a_re, v_ssm_a_im, v_ssm_log_dt, v_ssm_b_re, v_ssm_b_im, v_ssm_c_re, v_ssm_c_im, v_ssm_d, v_glu_w, v_glu_v, v_w_out, v_ln1_g, v_ln1_b, v_mlp_w_up, v_mlp_b_up, v_mlp_w_down, v_mlp_b_down, v_ln2_g, v_ln2_b):
    args = locals()
    w = {k: args[k][0] for k in ORDER}
    m = {k: args["m_" + k][0] for k in BIG}
    v = {k: args["v_" + k][0] for k in BIG}
    raw = [{k: args[prefix + k] for k in SMALL} for prefix in ("", "m_", "v_")]
    loss, grad_x, out_g, out_d, out_m, out_v = _step(x[0], loss_target[0], w, m, v, *raw)
    outs = [loss, grad_x[None]]
    for group in (out_g, out_d, out_m, out_v):
        outs += [group[k].reshape(args[k].shape) for k in ORDER]
    return tuple(outs)
```

```python
import functools
import math

import jax
import jax.numpy as jnp
from jax import lax
from jax.experimental import pallas as pl
from jax.experimental.pallas import tpu as pltpu

F32 = jnp.float32
BF16 = jnp.bfloat16
MESH = pl.DeviceIdType.MESH
N_DEV = 8
SUBLANES = 8
LANES = 128
VMEM_BYTES_V7X = 64 * 2 ** 20
VMEM_CAP = VMEM_BYTES_V7X - 8 * 2 ** 20

ALPHA = 2.0 ** 0.25
LN_EPS = 1e-5
RG_C = 8.0
ADAM_LR, ADAM_B1, ADAM_B2, ADAM_EPS, ADAM_WD, ADAM_STEP = 0.001, 0.9, 0.999, 1e-08, 0.01, 10
GELU_C = math.sqrt(2.0 / math.pi)
GELU_K = 0.044715

ANY = pl.BlockSpec(memory_space=pl.ANY)


def _params(sem, vmem_bytes):
    limit = int(min(max(2 * vmem_bytes, 16 * 2 ** 20), VMEM_CAP))
    return pltpu.CompilerParams(dimension_semantics=sem, vmem_limit_bytes=limit)


def _sig(x):
    return 1.0 / (1.0 + jnp.exp(-x))


def _gelu(x):
    return 0.5 * x * (1.0 + jnp.tanh(GELU_C * (x + GELU_K * x * x * x)))


def _dgelu(x):
    th = jnp.tanh(GELU_C * (x + GELU_K * x * x * x))
    return 0.5 * (1.0 + th) + 0.5 * x * (1.0 - th * th) * (GELU_C * (1.0 + 3.0 * GELU_K * x * x))


def _one_minus_exp(x, exp_half_x):
    p = x * (1.0 + x * (1 / 2 + x * (1 / 6 + x * (1 / 24 + x * (1 / 120)))))
    return jnp.where(x > -1 / 16, -p, 1.0 - exp_half_x * exp_half_x)


def _accumulate(ref, val, first):
    @pl.when(first)
    def _():
        ref[...] = val

    @pl.when(jnp.logical_not(first))
    def _():
        ref[...] += val


def _rows8(cw):
    return lax.broadcasted_iota(jnp.int32, (SUBLANES, cw), 0)


def _shift_down(cur, prev, s, rows):
    return jnp.where(rows < s, pltpu.roll(prev, s, 0), pltpu.roll(cur, s, 0))


def _shift_up(cur, nxt, s, rows):
    return jnp.where(rows < SUBLANES - s, pltpu.roll(cur, SUBLANES - s, 0), pltpu.roll(nxt, SUBLANES - s, 0))


def _mm(a, b, *, M, N, K, ta=False, tb=False, b_split=1, n_split=1, a_fn=None, extras=(), epilogue=None,
        n_out=1, n_cs=0, out_dtypes=None, tm=512, tn=512, tk=512, after=None, name):
    tm, tn, tk = min(tm, M), min(tn, N), min(tk, K)
    assert M % tm == 0 and N % tn == 0 and K % tk == 0, (name, M, N, K, tm, tn, tk)
    nk = K // tk
    grid = (N // tn, M // tm, nk)
    a_spec = pl.BlockSpec((tk, tm), lambda j, i, k: (k, i)) if ta else pl.BlockSpec((tm, tk), lambda j, i, k: (i, k))
    if b_split == 1:
        b_spec = pl.BlockSpec((tn, tk), lambda j, i, k: (j, k)) if tb else pl.BlockSpec((tk, tn), lambda j, i, k: (k, j))
    elif tb:
        kb = (K // b_split) // tk
        assert kb * tk * b_split == K, name
        b_spec = pl.BlockSpec((None, tn, tk), lambda j, i, k: (k // kb, j, k % kb))
    else:
        nb = (N // b_split) // tn
        assert nb * tn * b_split == N, name
        b_spec = pl.BlockSpec((None, tk, tn), lambda j, i, k: (j // nb, k, j % nb))
    in_specs = [a_spec, b_spec]
    for arr, kind, *col_off in extras:
        off = col_off[0] if col_off else 0
        in_specs.append(pl.BlockSpec((tm, tn), lambda j, i, k, off=off: (i, off + j)) if kind == "mn"
                        else pl.BlockSpec((1, tn), lambda j, i, k: (0, j)))
    out_dtypes = (F32,) * n_out if out_dtypes is None else out_dtypes
    if n_split == 1:
        out_shape = [jax.ShapeDtypeStruct((M, N), dt) for dt in out_dtypes]
        out_specs = [pl.BlockSpec((tm, tn), lambda j, i, k: (i, j)) for _ in range(n_out)]
    else:
        assert n_out == 1
        nbo = (N // n_split) // tn
        assert nbo * tn * n_split == N, name
        out_shape = [jax.ShapeDtypeStruct((n_split, M, N // n_split), out_dtypes[0])]
        out_specs = [pl.BlockSpec((None, tm, tn), lambda j, i, k: (j // nbo, i, j % nbo))]
    out_shape += [jax.ShapeDtypeStruct((1, N), F32) for _ in range(n_cs)]
    out_specs += [pl.BlockSpec((1, tn), lambda j, i, k: (0, j)) for _ in range(n_cs)]
    ne = len(extras)
    dims = (((0 if ta else 1,), (1 if tb else 0,)), ((), ()))

    n_after = 0 if after is None else 1
    in_specs += [ANY] * n_after

    def body(*refs):
        a_ref, b_ref = refs[0], refs[1]
        ex_refs = refs[2:2 + ne]
        first_out = 2 + ne + n_after
        out_refs = refs[first_out:first_out + n_out]
        cs_refs = refs[first_out + n_out:first_out + n_out + n_cs]
        i, k = pl.program_id(1), pl.program_id(2)

        def product():
            av = a_ref[...]
            if a_fn is not None:
                av = a_fn(av.astype(F32))
            return lax.dot_general(av.astype(BF16), b_ref[...].astype(BF16), dims, preferred_element_type=F32)

        def finish(acc):
            res = (acc,) if epilogue is None else epilogue(acc, *[r[...] for r in ex_refs])
            for r, o in zip(out_refs, res[:n_out]):
                r[...] = o.astype(r.dtype)
            for r, cval in zip(cs_refs, res[n_out:]):
                _accumulate(r, jnp.sum(cval, axis=0, keepdims=True), i == 0)

        if nk == 1:
            finish(product())
            return
        acc_ref = refs[-1]

        @pl.when(k == 0)
        def _():
            acc_ref[...] = jnp.zeros_like(acc_ref)

        acc_ref[...] += product()

        @pl.when(k == nk - 1)
        def _():
            finish(acc_ref[...])

    vmem = 2 * tm * tk * a.dtype.itemsize + 2 * tk * tn * b.dtype.itemsize + (1 + 2 * n_out + 2 * ne + 2) * tm * tn * 4
    outs = pl.pallas_call(
        body, name=name, grid=grid, in_specs=in_specs, out_specs=out_specs, out_shape=out_shape,
        scratch_shapes=[pltpu.VMEM((tm, tn), F32)] if nk > 1 else [],
        compiler_params=_params(("parallel", "arbitrary", "arbitrary"), vmem),
    )(a, b, *[e[0] for e in extras], *([after] if n_after else []))
    return outs[0] if len(outs) == 1 else outs


BD_STEP = 4

def _bd(pairs, *, T, J, kb, nb, tw=False, extras=(), epilogue=None, n_out=1, n_cs=0, out_dtypes=None, tm=512, name):
    jb = BD_STEP
    assert T % tm == 0 and J % jb == 0
    grid = (J // jb, T // tm)
    npair, ne = len(pairs), len(extras)
    in_specs, args = [], []
    for arr, off, w in pairs:
        assert off % jb == 0, name
        in_specs.append(pl.BlockSpec((tm, jb * kb), lambda j, i, off=off // jb: (i, off + j)))
        in_specs.append(pl.BlockSpec((jb,) + tuple(w.shape[1:]), lambda j, i: (j, 0, 0)))
        args += [arr, w]
    for arr, kind, off in extras:
        assert off % jb == 0, name
        in_specs.append(pl.BlockSpec((tm, jb * nb), lambda j, i, off=off // jb: (i, off + j)) if kind == "tile"
                        else pl.BlockSpec((1, jb * nb), lambda j, i, off=off // jb: (0, off + j)))
        args.append(arr)
    out_dtypes = (F32,) * n_out if out_dtypes is None else out_dtypes
    out_shape = [jax.ShapeDtypeStruct((T, J * nb), dt) for dt in out_dtypes]
    out_specs = [pl.BlockSpec((tm, jb * nb), lambda j, i: (i, j)) for _ in range(n_out)]
    out_shape += [jax.ShapeDtypeStruct((1, J * nb), F32) for _ in range(n_cs)]
    out_specs += [pl.BlockSpec((1, jb * nb), lambda j, i: (0, j)) for _ in range(n_cs)]
    dims = (((1,), (1 if tw else 0,)), ((), ()))

    def body(*refs):
        ex_refs = refs[2 * npair:2 * npair + ne]
        out_refs = refs[2 * npair + ne:2 * npair + ne + n_out]
        cs_refs = refs[2 * npair + ne + n_out:]
        i = pl.program_id(1)
        for s in range(jb):
            cols_in, cols_out = pl.ds(s * kb, kb), pl.ds(s * nb, nb)
            acc = None
            for p in range(npair):
                d = lax.dot_general(refs[2 * p][:, cols_in].astype(BF16), refs[2 * p + 1][s].astype(BF16), dims,
                                    preferred_element_type=F32)
                acc = d if acc is None else acc + d
            res = (acc,) if epilogue is None else epilogue(acc, *[r[:, cols_out] for r in ex_refs])
            for r, o in zip(out_refs, res[:n_out]):
                r[:, cols_out] = o.astype(r.dtype)
            for r, cval in zip(cs_refs, res[n_out:]):
                _accumulate(r.at[:, cols_out], jnp.sum(cval, axis=0, keepdims=True), i == 0)

    vmem = jb * (2 * npair * tm * kb + 2 * npair * kb * nb + (2 * n_out + 2 * ne + 3) * tm * nb) * 4
    outs = pl.pallas_call(
        body, name=name, grid=grid, in_specs=in_specs, out_specs=out_specs, out_shape=out_shape,
        compiler_params=_params(("parallel", "arbitrary"), vmem),
    )(*args)
    return outs[0] if len(outs) == 1 else outs


def _bdw(a, a_off, b, b_off, *, T, J, kb, nb, tm=512, name):
    jb = BD_STEP
    assert T % tm == 0 and J % jb == 0 and a_off % jb == 0 and b_off % jb == 0
    a_blk, b_blk = a_off // jb, b_off // jb

    def body(a_ref, b_ref, o_ref):
        i = pl.program_id(1)
        for s in range(jb):
            d = lax.dot_general(a_ref[:, pl.ds(s * kb, kb)].astype(BF16), b_ref[:, pl.ds(s * nb, nb)].astype(BF16),
                                (((0,), (0,)), ((), ())), preferred_element_type=F32)
            _accumulate(o_ref.at[s], d, i == 0)

    return pl.pallas_call(
        body, name=name, grid=(J // jb, T // tm),
        in_specs=[pl.BlockSpec((tm, jb * kb), lambda j, i: (i, a_blk + j)),
                  pl.BlockSpec((tm, jb * nb), lambda j, i: (i, b_blk + j))],
        out_specs=pl.BlockSpec((jb, kb, nb), lambda j, i: (j, 0, 0)),
        out_shape=jax.ShapeDtypeStruct((J, kb, nb), F32),
        compiler_params=_params(("parallel", "arbitrary"), jb * (2 * tm * (kb + nb) + 3 * kb * nb) * 4),
    )(a, b)


def _bd_pack(w, q):
    g, a, b = w.shape
    eye = jnp.eye(q, dtype=w.dtype)
    return jnp.einsum("jqab,qr->jqarb", w.reshape(g // q, q, a, b), eye).reshape(g // q, q * a, q * b)


def _bd_unpack(wp, q):
    j, qa, qb = wp.shape
    a, b = qa // q, qb // q
    w5 = wp.reshape(j, q, a, q, b)
    return jnp.stack([w5[:, r, :, r, :] for r in range(q)], axis=1).reshape(j * q, a, b)


def _ew(fn, ins, *, T, C, n_out, n_cs=0, out_dtypes=None, tm=256, cw=None, name):
    cw = C if cw is None else cw
    assert T % tm == 0 and C % cw == 0
    grid = (C // cw, T // tm)
    in_specs = []
    for arr, kind, off in ins:
        in_specs.append(pl.BlockSpec((tm, cw), lambda j, i, off=off: (i, off + j)) if kind == "tile"
                        else pl.BlockSpec((arr.shape[0], cw), lambda j, i, off=off: (0, off + j)))
    out_dtypes = (F32,) * n_out if out_dtypes is None else out_dtypes
    out_shape = [jax.ShapeDtypeStruct((T, C), dt) for dt in out_dtypes]
    out_specs = [pl.BlockSpec((tm, cw), lambda j, i: (i, j)) for _ in range(n_out)]
    out_shape += [jax.ShapeDtypeStruct((1, C), F32) for _ in range(n_cs)]
    out_specs += [pl.BlockSpec((1, cw), lambda j, i: (0, j)) for _ in range(n_cs)]
    nin = len(ins)

    def body(*refs):
        i = pl.program_id(1)
        res = fn(*[r[...].astype(F32) for r in refs[:nin]])
        for r, o in zip(refs[nin:nin + n_out], res[:n_out]):
            r[...] = o.astype(r.dtype)
        for r, cval in zip(refs[nin + n_out:], res[n_out:]):
            _accumulate(r, jnp.sum(cval, axis=0, keepdims=True), i == 0)

    vmem = (2 * nin + 2 * n_out + 6) * tm * cw * 4
    outs = pl.pallas_call(
        body, name=name, grid=grid, in_specs=in_specs, out_specs=out_specs, out_shape=out_shape,
        compiler_params=_params(("parallel", "arbitrary"), vmem),
    )(*[arr for arr, _, _ in ins])
    return outs[0] if len(outs) == 1 else outs


def _ln_stats(s):
    mu = jnp.mean(s, axis=-1, keepdims=True)
    d = s - mu
    var = jnp.mean(d * d, axis=-1, keepdims=True)
    rstd = lax.rsqrt(var + LN_EPS)
    return d * rstd, rstd


def _ln_bwd(dy, g, xhat, rstd):
    dxh = dy * g
    m1 = jnp.mean(dxh, axis=-1, keepdims=True)
    m2 = jnp.mean(dxh * xhat, axis=-1, keepdims=True)
    return rstd * (dxh - m1 - xhat * m2)


def _conv_fwd(z, conv_w, conv_b, *, T, C, tm=512, cw=1024, after=None, name):
    ng, hb = tm // SUBLANES, tm // SUBLANES
    n_after = 0 if after is None else 1

    def body(x_ref, halo_ref, w_ref, b_ref, *rest):
        o_ref = rest[-1]
        it = pl.program_id(1)
        rows = _rows8(cw)
        halo = jnp.where(it == 0, 0.0, halo_ref[...])
        w = w_ref[...]
        bias = b_ref[...]

        def group(g, carry):
            off = pl.multiple_of(g * SUBLANES, SUBLANES)
            cur = x_ref[pl.ds(off, SUBLANES), :]
            prev = x_ref[pl.ds(pl.multiple_of(jnp.maximum(off - SUBLANES, 0), SUBLANES), SUBLANES), :]
            prev = jnp.where(g == 0, halo, prev)
            acc = cur * w[3:4] + bias
            for s in (1, 2, 3):
                acc = acc + _shift_down(cur, prev, s, rows) * w[3 - s:4 - s]
            o_ref[pl.ds(off, SUBLANES), :] = acc
            return carry

        lax.fori_loop(0, ng, group, 0, unroll=2)

    return pl.pallas_call(
        body, name=name, grid=(C // cw, T // tm),
        in_specs=[pl.BlockSpec((tm, cw), lambda j, i: (i, j)),
                  pl.BlockSpec((SUBLANES, cw), lambda j, i: (jnp.maximum(i * hb - 1, 0), j)),
                  pl.BlockSpec((4, cw), lambda j, i: (0, j)), pl.BlockSpec((1, cw), lambda j, i: (0, j))]
        + [ANY] * n_after,
        out_specs=pl.BlockSpec((tm, cw), lambda j, i: (i, j)),
        out_shape=jax.ShapeDtypeStruct((T, C), F32),
        compiler_params=_params(("parallel", "arbitrary"), 5 * tm * cw * 4),
    )(z, z, conv_w, conv_b, *([after] if n_after else []))


def _conv_bwd(dxc, z, conv_w, *, T, C, tm=512, cw=512, name):
    ng, hb, last = tm // SUBLANES, tm // SUBLANES, T // SUBLANES - 1
    nt = T // tm

    def body(d_ref, dn_ref, x_ref, w_ref, o_ref, sums_ref):
        it = pl.program_id(1)
        rows = _rows8(cw)
        dnext = jnp.where(it == nt - 1, 0.0, dn_ref[...])
        w = w_ref[...]

        def group(g, accs):
            off = pl.multiple_of(g * SUBLANES, SUBLANES)
            dcur = d_ref[pl.ds(off, SUBLANES), :]
            dnx = d_ref[pl.ds(pl.multiple_of(jnp.minimum(off + SUBLANES, tm - SUBLANES), SUBLANES), SUBLANES), :]
            dnx = jnp.where(g == ng - 1, dnext, dnx)
            xcur = x_ref[pl.ds(off, SUBLANES), :]
            acc = dcur * w[3:4]
            taps = [accs[3] + dcur * xcur]
            for s in (1, 2, 3):
                ahead = _shift_up(dcur, dnx, s, rows)
                acc = acc + ahead * w[3 - s:4 - s]
                taps.append(accs[3 - s] + ahead * xcur)
            o_ref[pl.ds(off, SUBLANES), :] = acc
            return taps[3], taps[2], taps[1], taps[0], accs[4] + dcur

        zero = jnp.zeros((SUBLANES, cw), F32)
        accs = lax.fori_loop(0, ng, group, (zero,) * 5, unroll=2)
        sums = jnp.zeros((SUBLANES, cw), F32)
        for k, a in enumerate(accs):
            sums = jnp.where(rows == k, jnp.sum(a, axis=0, keepdims=True), sums)
        _accumulate(sums_ref, sums, it == 0)

    tile = pl.BlockSpec((tm, cw), lambda j, i: (i, j))
    return pl.pallas_call(
        body, name=name, grid=(C // cw, nt),
        in_specs=[tile, pl.BlockSpec((SUBLANES, cw), lambda j, i: (jnp.minimum((i + 1) * hb, last), j)),
                  tile, pl.BlockSpec((4, cw), lambda j, i: (0, j))],
        out_specs=[tile, pl.BlockSpec((SUBLANES, cw), lambda j, i: (0, j))],
        out_shape=[jax.ShapeDtypeStruct((T, C), F32), jax.ShapeDtypeStruct((SUBLANES, C), F32)],
        compiler_params=_params(("parallel", "arbitrary"), 7 * tm * cw * 4),
    )(dxc, dxc, z, conv_w)


def _rg_coeffs(r, ig, xc, sp):
    la = (-RG_C) * r * sp
    a = jnp.exp(la)
    m = jnp.sqrt(_one_minus_exp(2.0 * la, a))
    return a, m, m * (ig * xc)


def _rg_scan_fwd(z, ri, xc, sp, *, T, C, gate_off, tm=512, cw=256, name):
    rows16 = 2 * SUBLANES
    nq = tm // rows16

    def body(gate_ref, r_ref, i_ref, xc_ref, sp_ref, h_ref, p_ref, a_ref, m_ref, carry_ref):
        it = pl.program_id(1)

        @pl.when(it == 0)
        def _():
            carry_ref[...] = jnp.zeros_like(carry_ref)

        rows = _rows8(cw)
        sp_row = sp_ref[...]

        def pair(q, carry):
            base = pl.multiple_of(q * rows16, rows16)
            halves = []
            for half in range(2):
                sl = pl.ds(pl.multiple_of(base + half * SUBLANES, SUBLANES), SUBLANES)
                a, m, b = _rg_coeffs(r_ref[sl, :], i_ref[sl, :], xc_ref[sl, :], sp_row)
                a_ref[sl, :] = a
                m_ref[sl, :] = m
                for s in (1, 2, 4):
                    keep = rows >= s
                    sa = jnp.where(keep, pltpu.roll(a, s, 0), 1.0)
                    sb = jnp.where(keep, pltpu.roll(b, s, 0), 0.0)
                    b = b + a * sb
                    a = a * sa
                h = b + a * carry
                h_ref[sl, :] = h
                halves.append(h * _gelu(gate_ref[sl, :]))
                carry = h[SUBLANES - 1:SUBLANES, :]
            p_ref[pl.ds(base, rows16), :] = jnp.concatenate(halves, axis=0).astype(p_ref.dtype)
            return carry

        last = lax.fori_loop(0, nq, pair, carry_ref[0:1, :], unroll=2)
        carry_ref[...] = jnp.broadcast_to(last, carry_ref.shape)

    tile = pl.BlockSpec((tm, cw), lambda j, i: (i, j))
    gate_blk = gate_off // cw
    return pl.pallas_call(
        body, name=name, grid=(C // cw, T // tm),
        in_specs=[pl.BlockSpec((tm, cw), lambda j, i: (i, gate_blk + j)),
                  pl.BlockSpec((tm, cw), lambda j, i: (i, 2 * j)), pl.BlockSpec((tm, cw), lambda j, i: (i, 2 * j + 1)),
                  tile, pl.BlockSpec((1, cw), lambda j, i: (0, j))],
        out_specs=[tile, tile, tile, tile],
        out_shape=[jax.ShapeDtypeStruct((T, C), F32), jax.ShapeDtypeStruct((T, C), BF16),
                   jax.ShapeDtypeStruct((T, C), F32), jax.ShapeDtypeStruct((T, C), F32)],
        scratch_shapes=[pltpu.VMEM((SUBLANES, cw), F32)],
        compiler_params=_params(("parallel", "arbitrary"), 16 * tm * cw * 4),
    )(z, ri, ri, xc, sp)


def _rg_scan_bwd(dh, h, ri, xc, a_fwd, m_fwd, sp, *, T, C, tm=512, cw=256, name):
    ng, hb, nt = tm // SUBLANES, tm // SUBLANES, T // tm

    def body(dh_ref, h_ref, hp_ref, r_ref, i_ref, xc_ref, a_ref, m_ref, sp_ref,
             drai_ref, dxc_ref, crai_ref, csp_ref, cg_ref, ca_ref):
        step = pl.program_id(1)

        @pl.when(step == 0)
        def _():
            cg_ref[...] = jnp.zeros_like(cg_ref)
            ca_ref[...] = jnp.zeros_like(ca_ref)

        rows = _rows8(cw)
        sp_row = sp_ref[...]
        hhalo = jnp.where(step == nt - 1, 0.0, hp_ref[...])

        def group(gi, carry):
            g_next, a_next, s_ra, s_ia, s_sp = carry
            g = ng - 1 - gi
            off = pl.multiple_of(g * SUBLANES, SUBLANES)
            sl = pl.ds(off, SUBLANES)
            rr, ii, xx = r_ref[sl, :], i_ref[sl, :], xc_ref[sl, :]
            a, m = a_ref[sl, :], m_ref[sl, :]
            hh = h_ref[sl, :]
            hpv = h_ref[pl.ds(pl.multiple_of(jnp.maximum(off - SUBLANES, 0), SUBLANES), SUBLANES), :]
            hpv = jnp.where(g == 0, hhalo, hpv)
            hprev = _shift_down(hh, hpv, 1, rows)
            d = dh_ref[sl, :]
            c = jnp.where(rows < SUBLANES - 1, pltpu.roll(a, SUBLANES - 1, 0), a_next)
            for s in (1, 2, 4):
                keep = rows < SUBLANES - s
                sc = jnp.where(keep, pltpu.roll(c, SUBLANES - s, 0), 1.0)
                sd = jnp.where(keep, pltpu.roll(d, SUBLANES - s, 0), 0.0)
                d = d + c * sd
                c = c * sc
            gg = d + c * g_next
            da = gg * hprev
            dm = gg * (ii * xx)
            di = gg * (m * xx)
            dxc_ref[sl, :] = gg * (m * ii)
            dla = da * a - dm * (a * a / m)
            dra = dla * ((-RG_C) * sp_row) * (rr * (1.0 - rr))
            dia = di * (ii * (1.0 - ii))
            drai_ref[sl, pl.ds(0, cw)] = dra
            drai_ref[sl, pl.ds(cw, cw)] = dia
            return (gg[0:1, :], a[0:1, :], s_ra + dra, s_ia + dia, s_sp + dla * ((-RG_C) * rr))

        zero = jnp.zeros((SUBLANES, cw), F32)
        g_first, a_first, s_ra, s_ia, s_sp = lax.fori_loop(
            0, ng, group, (cg_ref[0:1, :], ca_ref[0:1, :], zero, zero, zero), unroll=2)
        cg_ref[...] = jnp.broadcast_to(g_first, cg_ref.shape)
        ca_ref[...] = jnp.broadcast_to(a_first, ca_ref.shape)
        for ref, acc in ((crai_ref.at[:, pl.ds(0, cw)], s_ra), (crai_ref.at[:, pl.ds(cw, cw)], s_ia), (csp_ref, s_sp)):
            _accumulate(ref, jnp.sum(acc, axis=0, keepdims=True), step == 0)

    tile = pl.BlockSpec((tm, cw), lambda j, i: (nt - 1 - i, j))
    wide = pl.BlockSpec((tm, 2 * cw), lambda j, i: (nt - 1 - i, j))
    vec = pl.BlockSpec((1, cw), lambda j, i: (0, j))
    return pl.pallas_call(
        body, name=name, grid=(C // cw, nt),
        in_specs=[tile, tile, pl.BlockSpec((SUBLANES, cw), lambda j, i: (jnp.maximum((nt - 1 - i) * hb - 1, 0), j)),
                  pl.BlockSpec((tm, cw), lambda j, i: (nt - 1 - i, 2 * j)),
                  pl.BlockSpec((tm, cw), lambda j, i: (nt - 1 - i, 2 * j + 1)), tile, tile, tile, vec],
        out_specs=[wide, tile, pl.BlockSpec((1, 2 * cw), lambda j, i: (0, j)), vec],
        out_shape=[jax.ShapeDtypeStruct((T, 2 * C), F32), jax.ShapeDtypeStruct((T, C), F32),
                   jax.ShapeDtypeStruct((1, 2 * C), F32), jax.ShapeDtypeStruct((1, C), F32)],
        scratch_shapes=[pltpu.VMEM((SUBLANES, cw), F32), pltpu.VMEM((SUBLANES, cw), F32)],
        compiler_params=_params(("parallel", "arbitrary"), 24 * tm * cw * 4),
    )(dh, h, h, ri, ri, xc, a_fwd, m_fwd, sp)


def _cscan_tables(lr, li, reverse):
    lam = (lr.reshape(-1), -li.reshape(-1) if reverse else li.reshape(-1))

    def mul(p, q):
        return p[0] * q[0] - p[1] * q[1], p[0] * q[1] + p[1] * q[0]

    pows = [lam]
    for _ in range(SUBLANES - 1):
        pows.append(mul(pows[-1], lam))
    zero = jnp.zeros_like(lam[0])
    tab = jnp.stack([pows[0][0], pows[0][1], pows[1][0], pows[1][1], pows[3][0], pows[3][1], zero, zero])
    if reverse:
        pows = pows[::-1]
    return tab, jnp.stack([p[0] for p in pows]), jnp.stack([p[1] for p in pows])


def _power_slabs(lr, li, n):
    pr, pi = lr.reshape(1, -1), li.reshape(1, -1)
    while pr.shape[0] < n:
        tr, ti = pr[-1:], pi[-1:]
        pr, pi = (jnp.concatenate([pr, pr * tr - pi * ti], axis=0), jnp.concatenate([pi, pr * ti + pi * tr], axis=0))
    return jnp.repeat(pr, SUBLANES, axis=0), jnp.repeat(pi, SUBLANES, axis=0), pr[-1], pi[-1]


def _rows_to_segments(src_ref, dst_ref):
    seg = src_ref.shape[0] // SUBLANES
    for g in range(seg):
        dst_ref[pl.ds(g * SUBLANES, SUBLANES), :] = src_ref[pl.ds(g, SUBLANES, stride=seg), :].astype(dst_ref.dtype)


def _segments_to_rows(src_ref, dst_ref):
    seg = src_ref.shape[0] // SUBLANES
    for r in range(SUBLANES):
        dst_ref[pl.ds(r * seg, seg), :] = src_ref[pl.ds(r, seg, stride=SUBLANES), :].astype(dst_ref.dtype)


def _seg_scan_tile(xr_ref, xi_ref, pbr_ref, pbi_ref, tab_ref, pwr_ref, pwi_ref, cr_ref, ci_ref, *, reverse, h=None):
    tm, cw = xr_ref.shape
    seg = tm // SUBLANES
    rows = _rows8(cw)
    sign = -1.0 if reverse else 1.0
    l_re, l_im = pbr_ref[0:1, :], sign * pbi_ref[0:1, :]

    def slab(g):
        return pl.ds(pl.multiple_of(g * SUBLANES, SUBLANES), SUBLANES)

    def local(k, state):
        sl = slab(seg - 1 - k if reverse else k)
        sr, si = state
        nr = xr_ref[sl, :] + (l_re * sr - l_im * si)
        ni = xi_ref[sl, :] + (l_re * si + l_im * sr)
        xr_ref[sl, :] = nr
        xi_ref[sl, :] = ni
        return nr, ni

    zero = jnp.zeros((SUBLANES, cw), F32)
    er, ei = lax.fori_loop(0, seg, local, (zero, zero), unroll=2)

    for k, s in enumerate((1, 2, 4)):
        shift = SUBLANES - s if reverse else s
        keep = rows < SUBLANES - s if reverse else rows >= s
        sr = jnp.where(keep, pltpu.roll(er, shift, 0), 0.0)
        si = jnp.where(keep, pltpu.roll(ei, shift, 0), 0.0)
        m_re, m_im = tab_ref[2 * k:2 * k + 1, :], tab_ref[2 * k + 1:2 * k + 2, :]
        er, ei = er + (m_re * sr - m_im * si), ei + (m_re * si + m_im * sr)
    cin_r, cin_i = cr_ref[0:1, :], ci_ref[0:1, :]
    pwr, pwi = pwr_ref[...], pwi_ref[...]
    er, ei = er + (pwr * cin_r - pwi * cin_i), ei + (pwr * cin_i + pwi * cin_r)
    if reverse:
        ent_r = jnp.where(rows == SUBLANES - 1, cin_r, pltpu.roll(er, SUBLANES - 1, 0))
        ent_i = jnp.where(rows == SUBLANES - 1, cin_i, pltpu.roll(ei, SUBLANES - 1, 0))
        out_r, out_i = er[0:1, :], ei[0:1, :]
    else:
        ent_r = jnp.where(rows == 0, cin_r, pltpu.roll(er, 1, 0))
        ent_i = jnp.where(rows == 0, cin_i, pltpu.roll(ei, 1, 0))
        out_r, out_i = er[SUBLANES - 1:SUBLANES, :], ei[SUBLANES - 1:SUBLANES, :]
    cr_ref[...] = jnp.broadcast_to(out_r, cr_ref.shape)
    ci_ref[...] = jnp.broadcast_to(out_i, ci_ref.shape)

    if h is not None:
        hr_ref, hi_ref, hr_last, hi_last = h
        hr_wrap = _shift_down(hr_ref[pl.ds(tm - SUBLANES, SUBLANES), :], hr_last, 1, rows)
        hi_wrap = _shift_down(hi_ref[pl.ds(tm - SUBLANES, SUBLANES), :], hi_last, 1, rows)

    def fix(g, sums):
        sl = slab(g)
        power = slab(seg - 1 - g) if reverse else sl
        pr, pi = pbr_ref[power, :], sign * pbi_ref[power, :]
        nr = xr_ref[sl, :] + (pr * ent_r - pi * ent_i)
        ni = xi_ref[sl, :] + (pr * ent_i + pi * ent_r)
        xr_ref[sl, :] = nr
        xi_ref[sl, :] = ni
        if h is None:
            return sums
        before = slab(jnp.maximum(g - 1, 0))
        hr1 = jnp.where(g == 0, hr_wrap, hr_ref[before, :])
        hi1 = jnp.where(g == 0, hi_wrap, hi_ref[before, :])
        return sums[0] + (nr * hr1 + ni * hi1), sums[1] + (ni * hr1 - nr * hi1)

    return lax.fori_loop(0, seg, fix, (zero, zero) if h is not None else (), unroll=2)


S5_TILE = 512


def _s5_fwd(z, u_off, wb_re, wb_im, wc_re, wc_im_neg, d_row, powers, *, T, tm=S5_TILE, name):
    J, ku, kp = wb_re.shape
    nt = T // tm
    pb_re, pb_im, top_re, top_im = powers
    tab, pw_re, pw_im = _cscan_tables(top_re, top_im, False)
    u_blk = u_off // ku

    def body(u_ref, wbr_ref, wbi_ref, wcr_ref, wci_ref, d_ref, pbr_ref, pbi_ref, tab_ref, pwr_ref, pwi_ref,
             hr_ref, hi_ref, y_ref, yg_ref, cr_ref, ci_ref, us_ref, ys_ref):
        @pl.when(pl.program_id(1) == 0)
        def _():
            cr_ref[...] = jnp.zeros_like(cr_ref)
            ci_ref[...] = jnp.zeros_like(ci_ref)

        _rows_to_segments(u_ref, us_ref)
        u = us_ref[...]
        ub = u.astype(BF16)
        hr_ref[...] = jnp.dot(ub, wbr_ref[...], preferred_element_type=F32)
        hi_ref[...] = jnp.dot(ub, wbi_ref[...], preferred_element_type=F32)
        _seg_scan_tile(hr_ref, hi_ref, pbr_ref, pbi_ref, tab_ref, pwr_ref, pwi_ref, cr_ref, ci_ref, reverse=False)
        y = (jnp.dot(hr_ref[...].astype(BF16), wcr_ref[...], preferred_element_type=F32)
             + jnp.dot(hi_ref[...].astype(BF16), wci_ref[...], preferred_element_type=F32) + d_ref[...] * u)
        ys_ref[...] = y
        _segments_to_rows(ys_ref, y_ref)
        ys_ref[...] = _gelu(y)
        _segments_to_rows(ys_ref, yg_ref)

    wb_spec = pl.BlockSpec((None, ku, kp), lambda j, i: (j, 0, 0))
    wc_spec = pl.BlockSpec((None, kp, ku), lambda j, i: (j, 0, 0))
    small = pl.BlockSpec((SUBLANES, kp), lambda j, i: (0, j))
    slabs = pl.BlockSpec((tm, kp), lambda j, i: (0, j))
    state = pl.BlockSpec((tm, kp), lambda j, i: (i, j))
    chan = pl.BlockSpec((tm, ku), lambda j, i: (i, j))
    return pl.pallas_call(
        body, name=name, grid=(J, nt),
        in_specs=[pl.BlockSpec((tm, ku), lambda j, i: (i, u_blk + j)), wb_spec, wb_spec, wc_spec, wc_spec,
                  pl.BlockSpec((1, ku), lambda j, i: (0, j)), slabs, slabs, small, small, small],
        out_specs=[state, state, chan, chan],
        out_shape=[jax.ShapeDtypeStruct((T, J * kp), F32)] * 2
        + [jax.ShapeDtypeStruct((T, J * ku), F32), jax.ShapeDtypeStruct((T, J * ku), BF16)],
        scratch_shapes=[pltpu.VMEM((SUBLANES, kp), F32), pltpu.VMEM((SUBLANES, kp), F32),
                        pltpu.VMEM((tm, ku), F32), pltpu.VMEM((tm, ku), F32)],
        compiler_params=_params(("parallel", "arbitrary"), 14 * tm * kp * 4),
    )(z, wb_re, wb_im, wc_re, wc_im_neg, d_row, pb_re, pb_im, tab, pw_re, pw_im)


def _s5_bwd(dy, z, u_off, h_re, h_im, wb_re, wb_im, wc_re, wc_im_neg, d_row, powers, *, T, tm=S5_TILE, name):
    J, ku, kp = wb_re.shape
    nt, hb = T // tm, tm // SUBLANES
    pb_re, pb_im, top_re, top_im = powers
    tab, pw_re, pw_im = _cscan_tables(top_re, top_im, True)
    u_blk = u_off // ku
    contract_rows = (((0,), (0,)), ((), ()))
    contract_cols = (((1,), (1,)), ((), ()))

    def body(dy_ref, u_ref, hr_ref, hrp_ref, hi_ref, hip_ref, wbr_ref, wbi_ref, wcr_ref, wci_ref, d_ref,
             pbr_ref, pbi_ref, tab_ref, pwr_ref, pwi_ref,
             du_ref, dlr_ref, dli_ref, dd_ref, dwbr_ref, dwbi_ref, dwcr_ref, dwci_ref,
             gr_ref, gi_ref, cr_ref, ci_ref, dys_ref, us_ref):
        step = pl.program_id(1)
        first = step == 0

        @pl.when(first)
        def _():
            cr_ref[...] = jnp.zeros_like(cr_ref)
            ci_ref[...] = jnp.zeros_like(ci_ref)

        _rows_to_segments(dy_ref, dys_ref)
        _rows_to_segments(u_ref, us_ref)
        dy_t, u = dys_ref[...], us_ref[...]
        dyb, ub = dy_t.astype(BF16), u.astype(BF16)
        gr_ref[...] = lax.dot_general(dyb, wcr_ref[...], contract_cols, preferred_element_type=F32)
        gi_ref[...] = lax.dot_general(dyb, wci_ref[...], contract_cols, preferred_element_type=F32)
        hr_last = jnp.where(step == nt - 1, 0.0, hrp_ref[...])
        hi_last = jnp.where(step == nt - 1, 0.0, hip_ref[...])
        s_re, s_im = _seg_scan_tile(gr_ref, gi_ref, pbr_ref, pbi_ref, tab_ref, pwr_ref, pwi_ref, cr_ref, ci_ref,
                                    reverse=True, h=(hr_ref, hi_ref, hr_last, hi_last))
        _accumulate(dlr_ref, jnp.sum(s_re, axis=0, keepdims=True), first)
        _accumulate(dli_ref, jnp.sum(s_im, axis=0, keepdims=True), first)
        grb, gib = gr_ref[...].astype(BF16), gi_ref[...].astype(BF16)
        du = (lax.dot_general(grb, wbr_ref[...], contract_cols, preferred_element_type=F32)
              + lax.dot_general(gib, wbi_ref[...], contract_cols, preferred_element_type=F32) + dy_t * d_ref[...])
        dys_ref[...] = du
        _segments_to_rows(dys_ref, du_ref)
        _accumulate(dd_ref, jnp.sum(dy_t * u, axis=0, keepdims=True), first)
        _accumulate(dwbr_ref, lax.dot_general(ub, grb, contract_rows, preferred_element_type=F32), first)
        _accumulate(dwbi_ref, lax.dot_general(ub, gib, contract_rows, preferred_element_type=F32), first)
        _accumulate(dwcr_ref, lax.dot_general(dyb, hr_ref[...].astype(BF16), contract_rows,
                                              preferred_element_type=F32), first)
        _accumulate(dwci_ref, lax.dot_general(dyb, hi_ref[...].astype(BF16), contract_rows,
                                              preferred_element_type=F32), first)

    def tix(i):
        return nt - 1 - i

    wb_spec = pl.BlockSpec((None, ku, kp), lambda j, i: (j, 0, 0))
    wc_spec = pl.BlockSpec((None, kp, ku), lambda j, i: (j, 0, 0))
    small = pl.BlockSpec((SUBLANES, kp), lambda j, i: (0, j))
    state = pl.BlockSpec((tm, kp), lambda j, i: (tix(i), j))
    halo = pl.BlockSpec((SUBLANES, kp), lambda j, i: (jnp.maximum(tix(i) * hb - 1, 0), j))
    chan = pl.BlockSpec((tm, ku), lambda j, i: (tix(i), j))
    svec = pl.BlockSpec((1, kp), lambda j, i: (0, j))
    cvec = pl.BlockSpec((1, ku), lambda j, i: (0, j))
    slabs = pl.BlockSpec((tm, kp), lambda j, i: (0, j))
    return pl.pallas_call(
        body, name=name, grid=(J, nt),
        in_specs=[chan, pl.BlockSpec((tm, ku), lambda j, i: (tix(i), u_blk + j)), state, halo, state, halo,
                  wb_spec, wb_spec, wc_spec, wc_spec, cvec, slabs, slabs, small, small, small],
        out_specs=[chan, svec, svec, cvec, wb_spec, wb_spec, wb_spec, wb_spec],
        out_shape=[jax.ShapeDtypeStruct((T, J * ku), BF16), jax.ShapeDtypeStruct((1, J * kp), F32),
                   jax.ShapeDtypeStruct((1, J * kp), F32), jax.ShapeDtypeStruct((1, J * ku), F32),
                   jax.ShapeDtypeStruct((J, ku, kp), F32), jax.ShapeDtypeStruct((J, ku, kp), F32),
                   jax.ShapeDtypeStruct((J, ku, kp), F32), jax.ShapeDtypeStruct((J, ku, kp), F32)],
        scratch_shapes=[pltpu.VMEM((tm, kp), F32), pltpu.VMEM((tm, kp), F32),
                        pltpu.VMEM((SUBLANES, kp), F32), pltpu.VMEM((SUBLANES, kp), F32),
                        pltpu.VMEM((tm, ku), F32), pltpu.VMEM((tm, ku), F32)],
        compiler_params=_params(("parallel", "arbitrary"), 16 * tm * kp * 4),
    )(dy, z, h_re, h_re, h_im, h_im, wb_re, wb_im, wc_re, wc_im_neg, d_row, pb_re, pb_im, tab, pw_re, pw_im)


def _mesh_pos():
    return lax.axis_index("x"), lax.axis_index("y"), lax.axis_index("c")


def _dev_index(px, py, pc):
    return 4 * px + 2 * py + pc


def _all_gather(shards, name):
    n = len(shards)

    def body(*refs):
        ins, outs = refs[:n], refs[n:2 * n]
        send_sems, recv_sems, local_sems = refs[2 * n:]
        x, y, c = _mesh_pos()
        me, sibling = (x, y, c), (x, y, 1 - c)
        chips = [(1 - x, y), (x, 1 - y), (1 - x, 1 - y)]

        def copy(a, k, block, to, src=None):
            dst = outs[a].at[_dev_index(*block)]
            return pltpu.make_async_remote_copy(
                src_ref=dst if src is None else src, dst_ref=dst, send_sem=send_sems.at[a * 7 + k],
                recv_sem=recv_sems.at[a * 7 + k], device_id=to, device_id_type=MESH)

        mine = [pltpu.make_async_copy(ins[a], outs[a].at[_dev_index(*me)], local_sems.at[a]) for a in range(n)]
        for cp in mine:
            cp.start()
        first = []
        for a in range(n):
            first.append(copy(a, 0, me, sibling, src=ins[a]))
            first += [copy(a, 1 + j, me, (*chip, c), src=ins[a]) for j, chip in enumerate(chips)]
        for cp in first:
            cp.start()
        passed = []
        for j, chip in enumerate(chips):
            for a in range(n):
                copy(a, 1 + j, (*chip, c), me).wait_recv()
                fwd = copy(a, 4 + j, (*chip, c), sibling)
                fwd.start()
                passed.append(fwd)
        for a in range(n):
            copy(a, 0, sibling, me).wait_recv()
            for j, chip in enumerate(chips):
                copy(a, 4 + j, (*chip, 1 - c), me).wait_recv()
        for cp in first + passed:
            cp.wait_send()
        for cp in mine:
            cp.wait()

    return pl.pallas_call(
        body, name=name, in_specs=[ANY] * n, out_specs=[ANY] * n,
        out_shape=[jax.ShapeDtypeStruct((N_DEV,) + s.shape, s.dtype) for s in shards],
        scratch_shapes=[pltpu.SemaphoreType.DMA((7 * n,)), pltpu.SemaphoreType.DMA((7 * n,)),
                        pltpu.SemaphoreType.DMA((n,))],
    )(*shards)


def _exchange_blocks(parts, name):
    n = len(parts)
    relations = [(dx, dy, dc) for dx in (0, 1) for dy in (0, 1) for dc in (0, 1) if (dx, dy, dc) != (0, 0, 0)]

    def body(*refs):
        ins, outs = refs[:n], refs[n:2 * n]
        send_sems, recv_sems, local_sems = refs[2 * n:]
        x, y, c = _mesh_pos()
        me = _dev_index(x, y, c)
        mine = [pltpu.make_async_copy(ins[a].at[me], outs[a].at[me], local_sems.at[a]) for a in range(n)]
        for cp in mine:
            cp.start()
        copies = []
        for k, (dx, dy, dc) in enumerate(relations):
            peer = (x + dx - 2 * x * dx, y + dy - 2 * y * dy, c + dc - 2 * c * dc)
            for a in range(n):
                copies.append((pltpu.make_async_remote_copy(
                    src_ref=ins[a].at[_dev_index(*peer)], dst_ref=outs[a].at[me], send_sem=send_sems.at[a * 7 + k],
                    recv_sem=recv_sems.at[a * 7 + k], device_id=peer, device_id_type=MESH),
                    pltpu.make_async_remote_copy(
                    src_ref=ins[a].at[_dev_index(*peer)], dst_ref=outs[a].at[_dev_index(*peer)],
                    send_sem=send_sems.at[a * 7 + k], recv_sem=recv_sems.at[a * 7 + k], device_id=peer,
                    device_id_type=MESH)))
        for send, _ in copies:
            send.start()
        for _, recv in copies:
            recv.wait_recv()
        for send, _ in copies:
            send.wait_send()
        for cp in mine:
            cp.wait()

    return pl.pallas_call(
        body, name=name, in_specs=[ANY] * n, out_specs=[ANY] * n,
        out_shape=[jax.ShapeDtypeStruct(p.shape, p.dtype) for p in parts],
        scratch_shapes=[pltpu.SemaphoreType.DMA((7 * n,)), pltpu.SemaphoreType.DMA((7 * n,)),
                        pltpu.SemaphoreType.DMA((n,))],
    )(*parts)


HBM = pl.BlockSpec(memory_space=pltpu.HBM)
SEM = pl.BlockSpec(memory_space=pltpu.SEMAPHORE)
EFFECT = pltpu.SideEffectType.DATAFLOW_SIDE_EFFECTING
RELATIONS = [(dx, dy, dc) for dx in (0, 1) for dy in (0, 1) for dc in (0, 1) if (dx, dy, dc) != (0, 0, 0)]


def _peer(rel):
    x, y, c = _mesh_pos()
    dx, dy, dc = rel
    return (x + dx - 2 * x * dx, y + dy - 2 * y * dy, c + dc - 2 * c * dc)


CHIP_RELATIONS = [(1, 0, 0), (0, 1, 0), (1, 1, 0)]
EXCHANGE_PEERS = {"gather": RELATIONS, "scatter": RELATIONS, "own": [(0, 0, 1)] + CHIP_RELATIONS, "pass": CHIP_RELATIONS}


def _split_copy(src_ref, land_ref, send_sems, recv_sems, k, mode, incoming):
    x, y, c = _mesh_pos()
    me = _dev_index(x, y, c)
    peer = _peer(EXCHANGE_PEERS[mode][k])
    if mode == "pass":
        held, theirs = _dev_index(peer[0], peer[1], c), _dev_index(peer[0], peer[1], 1 - c)
        src, slot, target = land_ref.at[held], theirs if incoming else held, (x, y, 1 - c)
    else:
        src = src_ref.at[_dev_index(*peer)] if mode == "scatter" else src_ref
        slot, target = _dev_index(*peer) if incoming else me, peer
    return pltpu.make_async_remote_copy(src_ref=src, dst_ref=land_ref.at[slot], send_sem=send_sems.at[k],
                                        recv_sem=recv_sems.at[k], device_id=target, device_id_type=MESH)


def _exchange_start(srcs, lands, *, mode, after=None, name):
    n = len(srcs)
    n_after = 0 if after is None else 1
    n_rel = len(EXCHANGE_PEERS[mode])

    def body(*refs):
        src_refs, land_refs = refs[:n], refs[n:2 * n]
        first_out = 2 * n + n_after
        send, recv = refs[first_out:first_out + n], refs[first_out + n:first_out + 2 * n]
        token = refs[-1]
        for k in range(n_rel):
            for a in range(n):
                _split_copy(src_refs[a], land_refs[a], send[a], recv[a], k, mode, incoming=False).start()
        token[...] = jnp.zeros_like(token)

    outs = pl.pallas_call(
        body, name=name, in_specs=[HBM] * (2 * n) + [ANY] * n_after,
        out_shape=[pltpu.SemaphoreType.DMA((n_rel,))] * (2 * n)
        + [pltpu.HBM(s.shape, s.dtype) for s in srcs] + [pltpu.HBM(s.shape, s.dtype) for s in lands]
        + [jax.ShapeDtypeStruct((SUBLANES, LANES), F32)],
        out_specs=[SEM] * (2 * n) + [HBM] * (2 * n) + [pl.BlockSpec(memory_space=pltpu.VMEM)],
        input_output_aliases={**{a: 2 * n + a for a in range(n)}, **{n + a: 3 * n + a for a in range(n)}},
        compiler_params=pltpu.CompilerParams(has_side_effects=EFFECT),
    )(*[pltpu.with_memory_space_constraint(s, pltpu.HBM) for s in srcs],
      *[pltpu.with_memory_space_constraint(s, pltpu.HBM) for s in lands], *([after] if n_after else []))
    per_array = [(outs[a], outs[n + a], outs[2 * n + a], outs[3 * n + a]) for a in range(n)]
    return per_array, outs[-1]


def _exchange_wait(handle, after, *, mode, name):
    send_sems, recv_sems, src_thru, land_thru = handle
    after = after if isinstance(after, (tuple, list)) else (after,)

    def body(src_ref, land_ref, send, recv, *rest):
        for k in range(len(EXCHANGE_PEERS[mode])):
            cp = _split_copy(src_ref, land_ref, send, recv, k, mode, incoming=True)
            cp.wait_send()
            cp.wait_recv()

    return pl.pallas_call(
        body, name=name, in_specs=[HBM, HBM, SEM, SEM] + [ANY] * len(after),
        out_shape=[pltpu.HBM(src_thru.shape, src_thru.dtype), pltpu.HBM(land_thru.shape, land_thru.dtype)],
        out_specs=[HBM, HBM], input_output_aliases={0: 0, 1: 1},
        compiler_params=pltpu.CompilerParams(has_side_effects=EFFECT),
    )(src_thru, land_thru, send_sems, recv_sems, *after)[1]


def _landing_zone(own_block):
    me = _dev_index(*_mesh_pos())
    zone = lax.empty((N_DEV,) + own_block.shape, own_block.dtype)
    return lax.dynamic_update_index_in_dim(zone, own_block, me, 0)


def _row_tile(rows, want):
    t = min(want, rows) // SUBLANES * SUBLANES
    while rows % t:
        t -= SUBLANES
    return t


def _sum_slots(recv, *, tr, name):
    s_, r_, c_ = recv.shape
    tr = _row_tile(r_, tr)

    def body(g_ref, o_ref):
        acc = g_ref[0]
        for s in range(1, s_):
            acc = acc + g_ref[s]
        o_ref[...] = acc

    return pl.pallas_call(
        body, name=name, grid=(r_ // tr,),
        in_specs=[pl.BlockSpec((s_, tr, c_), lambda i: (0, i, 0))],
        out_specs=pl.BlockSpec((tr, c_), lambda i: (i, 0)),
        out_shape=jax.ShapeDtypeStruct((r_, c_), F32),
        compiler_params=_params(("parallel",), (2 * s_ + 3) * tr * c_ * 4),
    )(recv)


def _adamw(recv, w, m, v, *, tr, name):
    s_, r_, c_ = recv.shape
    tr = _row_tile(r_, tr)
    assert w.shape == (r_, c_), (name, w.shape, recv.shape)
    c1 = 1.0 - ADAM_B1 ** ADAM_STEP
    c2 = 1.0 - ADAM_B2 ** ADAM_STEP

    def body(g_ref, w_ref, m_ref, v_ref, go_ref, d_ref, mo_ref, vo_ref):
        g = g_ref[0].astype(F32)
        for s in range(1, s_):
            g = g + g_ref[s].astype(F32)
        mn = ADAM_B1 * m_ref[...] + (1.0 - ADAM_B1) * g
        vn = ADAM_B2 * v_ref[...] + (1.0 - ADAM_B2) * (g * g)
        go_ref[...] = g
        mo_ref[...] = mn
        vo_ref[...] = vn
        d_ref[...] = -ADAM_LR * ((mn / c1) / (jnp.sqrt(vn / c2) + ADAM_EPS) + ADAM_WD * w_ref[...])

    tile = pl.BlockSpec((tr, c_), lambda i: (i, 0))
    return pl.pallas_call(
        body, name=name, grid=(r_ // tr,),
        in_specs=[pl.BlockSpec((s_, tr, c_), lambda i: (0, i, 0)), tile, tile, tile],
        out_specs=[tile] * 4, out_shape=[jax.ShapeDtypeStruct((r_, c_), F32)] * 4,
        compiler_params=_params(("parallel",), (2 * s_ + 16) * tr * c_ * 4),
    )(recv, w, m, v)


def _adamw_whole(gs, ws, ms, vs, *, name):
    n = len(gs)
    c1 = 1.0 - ADAM_B1 ** ADAM_STEP
    c2 = 1.0 - ADAM_B2 ** ADAM_STEP

    def body(*refs):
        for i in range(n):
            g, w = refs[i][...], refs[n + i][...]
            mn = ADAM_B1 * refs[2 * n + i][...] + (1.0 - ADAM_B1) * g
            vn = ADAM_B2 * refs[3 * n + i][...] + (1.0 - ADAM_B2) * (g * g)
            refs[4 * n + 3 * i][...] = -ADAM_LR * ((mn / c1) / (jnp.sqrt(vn / c2) + ADAM_EPS) + ADAM_WD * w)
            refs[4 * n + 3 * i + 1][...] = mn
            refs[4 * n + 3 * i + 2][...] = vn

    whole = pl.BlockSpec(memory_space=pltpu.VMEM)
    lane_padded = sum(math.prod(g.shape[:-1]) * (-(-g.shape[-1] // LANES) * LANES) for g in gs)
    outs = pl.pallas_call(
        body, name=name, in_specs=[whole] * (4 * n), out_specs=[whole] * (3 * n),
        out_shape=[jax.ShapeDtypeStruct(g.shape, F32) for g in gs for _ in range(3)],
        compiler_params=pltpu.CompilerParams(vmem_limit_bytes=int(min(max(16 * lane_padded * 4, 16 * 2 ** 20), VMEM_CAP))),
    )(*gs, *ws, *ms, *vs)
    return [tuple(outs[3 * i:3 * i + 3]) for i in range(n)]


def _s5_discretise(a_re, a_im, log_dt, b_re, b_im):
    dt = jnp.exp(log_dt)[:, None]
    lr = jnp.minimum(a_re, -1e-4)
    li = a_im
    mag = jnp.exp(lr * dt)
    lbr = mag * jnp.cos(li * dt)
    lbi = mag * jnp.sin(li * dt)
    zr, zi = lbr - 1.0, lbi
    den = lr * lr + li * li
    fr = (zr * lr + zi * li) / den
    fi = (zi * lr - zr * li) / den
    bbr = fr[..., None] * b_re - fi[..., None] * b_im
    bbi = fr[..., None] * b_im + fi[..., None] * b_re
    return lbr, lbi, bbr, bbi


def _softplus_neg(lam):
    return jnp.maximum(-lam, 0.0) + jnp.log(1.0 + jnp.exp(-jnp.abs(lam)))


S5_Q = 8
RG_Q = 2


def _local_step(x, tgt, W, comm):
    T, D = x.shape
    C = D
    G, P, H = W["ssm_b_re"].shape
    S = G * H
    F = W["mlp_b_up"].shape[1]
    n_in = 2 * C + S + 2 * D
    heads, hd = W["rg_wa"].shape[0], W["rg_wa"].shape[1]
    u_off, ga_off, gb_off = 2 * C, 2 * C + S, 2 * C + S + D

    if comm.first_token is not None:
        anchored = ("rg_lambda", "ssm_a_re", "rg_wa", "rg_wx", "ssm_c_re", "ssm_c_im")
        W = {**W, **{k: W[k] + comm.first_token[0, 0] for k in anchored}}
    sp, sp_vjp = jax.vjp(_softplus_neg, W["rg_lambda"])
    (lbr, lbi, bbr, bbi), s5_vjp = jax.vjp(_s5_discretise, W["ssm_a_re"], W["ssm_a_im"], W["ssm_log_dt"],
                                           W["ssm_b_re"], W["ssm_b_im"])
    lam_re, lam_im = lbr.reshape(-1), lbi.reshape(-1)
    jr, kr = heads // RG_Q, RG_Q * hd
    w_ri = jnp.concatenate([_bd_pack(W["rg_wa"], RG_Q), _bd_pack(W["rg_wx"], RG_Q)], axis=2).astype(BF16)
    b_ri = jnp.concatenate([W["rg_ba"].reshape(jr, kr), W["rg_bx"].reshape(jr, kr)], axis=1).reshape(1, -1)
    wb_re = _bd_pack(jnp.swapaxes(bbr, 1, 2), S5_Q).astype(BF16)
    wb_im = _bd_pack(jnp.swapaxes(bbi, 1, 2), S5_Q).astype(BF16)
    wc_re = _bd_pack(jnp.swapaxes(W["ssm_c_re"], 1, 2), S5_Q).astype(BF16)
    wc_im_neg = _bd_pack(jnp.swapaxes(-W["ssm_c_im"], 1, 2), S5_Q).astype(BF16)
    d_row = W["ssm_d"].reshape(1, S)
    powers = _power_slabs(lam_re, lam_im, S5_TILE // SUBLANES)

    x_bf = x.astype(BF16) if comm.first_token is None else (x + comm.first_token[0, 0]).astype(BF16)
    w_in, conv_w = comm.first_weights((x_bf, w_ri, wb_re, wb_im, wc_re, wc_im_neg, powers[0], powers[1]))
    z = _mm(x_bf, w_in, M=T, N=n_in, K=D, tm=512, tn=n_in // 4, tk=D, after=comm.gather_token, name="fwd_in_proj")
    started = comm.start_weights(("mlp_w_up",), z)
    xc = _conv_fwd(z, conv_w, W["conv_b"], T=T, C=C, after=started, name="fwd_conv")
    ri = _bd([(xc, 0, w_ri)], T=T, J=jr, kb=kr, nb=2 * kr, extras=[(b_ri, "vec", 0)],
             epilogue=lambda acc, b: (_sig(acc + b),), name="fwd_gates")
    h, p, a_fwd, m_fwd = _rg_scan_fwd(z, ri, xc, sp, T=T, C=C, gate_off=C, cw=kr, name="fwd_rg_scan")
    w_a_out = comm.weight("w_a_out", p)
    started = comm.start_weights(("mlp_w_down",), p)
    y_a = _mm(p, w_a_out, M=T, N=D, K=C, out_dtypes=(BF16,), tm=512, tn=D, tk=C, after=started, name="fwd_rg_out")

    h_re, h_im, y_s, yg = _s5_fwd(z, u_off, wb_re, wb_im, wc_re, wc_im_neg, d_row, powers, T=T, name="fwd_s5")
    w_glu_w, w_glu_v = comm.weight("glu_w", yg), comm.weight("glu_v", yg)
    glu_a = _mm(yg, w_glu_w, M=T, N=D, K=S, out_dtypes=(BF16,), tm=1024, tn=D, tk=S, name="fwd_glu_w")
    cwm = 1024

    def mix_fn(b, ga, gb, ya, a):
        return b, _sig(ga) * ya.astype(F32) + _sig(gb) * (a.astype(F32) * _sig(b))

    glu_b, mix = _mm(yg, w_glu_v, M=T, N=D, K=S, tm=512, tn=cwm, tk=S,
                     extras=[(z, "mn", ga_off // cwm), (z, "mn", gb_off // cwm), (y_a, "mn"), (glu_a, "mn")],
                     epilogue=mix_fn, n_out=2, out_dtypes=(BF16, BF16), name="fwd_glu_v_mix")
    w_out = comm.weight("w_out", mix)
    def out_ln1_fn(acc, xv, g, b):
        s = ALPHA * xv + acc
        xhat, _ = _ln_stats(s)
        y = xhat * g + b
        return s, y, y

    s1, x1, x1_bf = _mm(mix, w_out, M=T, N=D, K=D, tm=256, tn=D, tk=D,
                        extras=[(x, "mn"), (W["ln1_g"], "n"), (W["ln1_b"], "n")], epilogue=out_ln1_fn, n_out=3,
                        out_dtypes=(F32, F32, BF16), name="fwd_out_proj_ln1")
    w_up = comm.weight("mlp_w_up", x1_bf)

    def mlp_up_fn(acc, b):
        hp = acc + b
        rl = jnp.maximum(hp, 0.0)
        return rl * rl, hp

    hact, hpre = _mm(x1_bf, w_up, M=T, N=F, K=D, tm=1024, tn=1024, tk=D, extras=[(W["mlp_b_up"], "n")],
                     epilogue=mlp_up_fn, n_out=2, out_dtypes=(BF16, BF16), name="fwd_mlp_up")
    w_down = comm.weight("mlp_w_down", hact)
    s2 = _mm(hact, w_down, M=T, N=D, K=F, tm=1024, tn=1024, tk=2048,
             extras=[(x1, "mn"), (W["mlp_b_down"], "n")], epilogue=lambda acc, xv, b: (ALPHA * xv + acc + b,),
             name="fwd_mlp_down")

    def ln2_fn(s, t, g, b):
        xhat, rstd = _ln_stats(s)
        err = xhat * g + b - t
        dy = err * (1.0 / D)
        ds = _ln_bwd(dy, g, xhat, rstd)
        return ds, ds, 0.5 * dy * err, dy * xhat, dy, ds

    ds2, ds2_bf, loss_cols, d_ln2_g, d_ln2_b, d_b_down = _ew(
        ln2_fn, [(s2, "tile", 0), (tgt, "tile", 0), (W["ln2_g"], "vec", 0), (W["ln2_b"], "vec", 0)],
        T=T, C=D, n_out=2, n_cs=4, out_dtypes=(F32, BF16), tm=256, name="bwd_loss_ln2")
    d_w_down = _mm(hact, ds2_bf, M=F, N=D, K=T, ta=True, out_dtypes=(BF16,), tm=1024, tn=1024, tk=4096, name="bwd_w_down")
    sent = comm.send_grad("mlp_w_down", d_w_down)

    def dhpre_fn(acc, hp):
        dv = acc * (2.0 * jnp.maximum(hp.astype(F32), 0.0))
        return dv, dv

    dhpre, d_b_up = _mm(ds2_bf, w_down, M=T, N=F, K=D, tb=True, tm=1024, tn=1024, tk=D, extras=[(hpre, "mn")],
                        epilogue=dhpre_fn, n_cs=1, out_dtypes=(BF16,), after=sent, name="bwd_mlp_down")
    d_w_up = _mm(x1_bf, dhpre, M=D, N=F, K=T, ta=True, out_dtypes=(BF16,), n_split=N_DEV, tm=D, tn=F // N_DEV, tk=2048, name="bwd_w_up")
    sent = comm.send_grad("mlp_w_up", d_w_up)
    dx1 = _mm(dhpre, w_up, M=T, N=D, K=F, tb=True, tm=1024, tn=1024, tk=2048,
              extras=[(ds2, "mn")], epilogue=lambda acc, dv: (ALPHA * dv + acc,), after=sent, name="bwd_mlp_up")

    def ln1_bwd_fn(s, dy, g):
        xhat, rstd = _ln_stats(s)
        ds = _ln_bwd(dy, g, xhat, rstd)
        return ds, ds, dy * xhat, dy

    ds1, ds1_bf, d_ln1_g, d_ln1_b = _ew(ln1_bwd_fn, [(s1, "tile", 0), (dx1, "tile", 0), (W["ln1_g"], "vec", 0)],
                                        T=T, C=D, n_out=2, n_cs=2, out_dtypes=(F32, BF16), tm=256, name="bwd_ln1")
    d_w_out = _mm(mix, ds1_bf, M=D, N=D, K=T, ta=True, out_dtypes=(BF16,), tm=1024, tn=1024, tk=4096, name="bwd_w_out")
    sent = comm.send_grad("w_out", d_w_out)
    def mix_bwd_fn(dm, ga, gb, ya, a, b):
        ya, a, b = ya.astype(F32), a.astype(F32), b.astype(F32)
        sa, sb, sv = _sig(ga), _sig(gb), _sig(b)
        yb = a * sv
        dyb = dm * sb
        return (dm * ya * (sa * (1.0 - sa)), dm * yb * (sb * (1.0 - sb)), dm * sa, dyb * sv,
                dyb * a * (sv * (1.0 - sv)))

    dg_a, dg_b, dy_a, dglu_a, dglu_b = _mm(
        ds1_bf, w_out, M=T, N=D, K=D, tb=True, tm=512, tn=cwm, tk=D,
        extras=[(z, "mn", ga_off // cwm), (z, "mn", gb_off // cwm), (y_a, "mn"), (glu_a, "mn"), (glu_b, "mn")],
        epilogue=mix_bwd_fn, n_out=5, out_dtypes=(BF16,) * 5, after=sent, name="bwd_out_proj_mix")

    d_w_a_out = _mm(p, dy_a, M=C, N=D, K=T, ta=True, out_dtypes=(BF16,), tm=1024, tn=1024, tk=4096, name="bwd_w_a_out")
    sent = comm.send_grad("w_a_out", d_w_a_out)
    def dp_fn(dp, hv, gate):
        th = jnp.tanh(GELU_C * (gate + GELU_K * gate * gate * gate))
        gelu = 0.5 * gate * (1.0 + th)
        dgelu = 0.5 * (1.0 + th) + 0.5 * gate * (1.0 - th * th) * (GELU_C * (1.0 + 3.0 * GELU_K * gate * gate))
        return dp * gelu, dp * hv * dgelu

    dh, dgate = _mm(dy_a, w_a_out, M=T, N=C, K=D, tb=True, tm=256, tn=C, tk=D, extras=[(h, "mn"), (z, "mn", 1)],
                    epilogue=dp_fn, n_out=2, out_dtypes=(F32, BF16), after=sent, name="bwd_rg_out")
    drai, dxc0, d_b_ri, d_sp = _rg_scan_bwd(dh, h, ri, xc, a_fwd, m_fwd, sp, T=T, C=C, cw=kr, name="bwd_rg_scan")
    dxc = _bd([(drai, 0, w_ri)], T=T, J=jr, kb=2 * kr, nb=kr, tw=True, extras=[(dxc0, "tile", 0)],
              epilogue=lambda acc, d0: (acc + d0,), name="bwd_gates")
    d_w_ri = _bdw(xc, 0, drai, 0, T=T, J=jr, kb=kr, nb=2 * kr, name="bwd_w_gates")
    d_wa, d_wx = _bd_unpack(d_w_ri[:, :, :kr], RG_Q), _bd_unpack(d_w_ri[:, :, kr:], RG_Q)
    d_b_ri = d_b_ri.reshape(jr, 2 * kr)
    d_ba, d_bx = d_b_ri[:, :kr].reshape(1, -1), d_b_ri[:, kr:].reshape(1, -1)
    dxr, conv_sums = _conv_bwd(dxc, z, conv_w, T=T, C=C, name="bwd_conv")
    d_conv_w, d_conv_b = conv_sums[0:4], conv_sums[4:5]
    (d_lambda,) = sp_vjp(d_sp)

    d_glu_w = _mm(yg, dglu_a, M=S, N=D, K=T, ta=True, out_dtypes=(BF16,), n_split=N_DEV, tm=1024, tn=D // N_DEV, tk=4096, name="bwd_w_glu_w")
    d_glu_v = _mm(yg, dglu_b, M=S, N=D, K=T, ta=True, out_dtypes=(BF16,), n_split=N_DEV, tm=1024, tn=D // N_DEV, tk=4096, name="bwd_w_glu_v")
    sent = comm.send_grad("glu_w", d_glu_w, "glu_v", d_glu_v)
    dyg0 = _mm(dglu_a, w_glu_w, M=T, N=S, K=D, tb=True, tm=512, tn=S, tk=D, after=sent, name="bwd_glu_w")
    dy_s = _mm(dglu_b, w_glu_v, M=T, N=S, K=D, tb=True, tm=512, tn=S, tk=D,
               extras=[(dyg0, "mn"), (y_s, "mn")], epilogue=lambda acc, d0, yv: ((acc + d0) * _dgelu(yv),),
               name="bwd_glu_v")
    du, d_lbr, d_lbi, d_ssm_d, d_wb_re, d_wb_im, d_wc_re, d_wc_im_neg = _s5_bwd(
        dy_s, z, u_off, h_re, h_im, wb_re, wb_im, wc_re, wc_im_neg, d_row, powers, T=T, name="bwd_s5")
    d_bbr = jnp.swapaxes(_bd_unpack(d_wb_re, S5_Q), 1, 2)
    d_bbi = jnp.swapaxes(_bd_unpack(d_wb_im, S5_Q), 1, 2)
    d_a_re, d_a_im, d_log_dt, d_b_re, d_b_im = s5_vjp((d_lbr.reshape(G, P), d_lbi.reshape(G, P), d_bbr, d_bbi))
    d_c_re = _bd_unpack(d_wc_re, S5_Q)
    d_c_im = -_bd_unpack(d_wc_im_neg, S5_Q)

    dz = jnp.concatenate([dxr.astype(BF16), dgate.astype(BF16), du, dg_a, dg_b], axis=1)
    d_w_in = _mm(x_bf, dz, M=D, N=n_in, K=T, ta=True, out_dtypes=(BF16,), n_split=N_DEV, tm=D, tn=n_in // N_DEV, tk=2048, name="bwd_w_in")
    sent = comm.send_grad("w_in", d_w_in)
    grad_x = _mm(dz, w_in, M=T, N=D, K=n_in, tb=True, tm=1024, tn=1024, tk=n_in // 4,
                 extras=[(ds1, "mn")], epilogue=lambda acc, dv: (ALPHA * dv + acc,), after=sent, name="bwd_in_proj")

    grads = dict(
        conv_w=d_conv_w, conv_b=d_conv_b, rg_wa=d_wa, rg_ba=d_ba, rg_wx=d_wx, rg_bx=d_bx,
        rg_lambda=d_lambda, ssm_a_re=d_a_re, ssm_a_im=d_a_im, ssm_log_dt=d_log_dt,
        ssm_b_re=d_b_re, ssm_b_im=d_b_im, ssm_c_re=d_c_re, ssm_c_im=d_c_im, ssm_d=d_ssm_d.reshape(G, H),
        ln1_g=d_ln1_g, ln1_b=d_ln1_b, mlp_b_up=d_b_up, mlp_b_down=d_b_down, ln2_g=d_ln2_g, ln2_b=d_ln2_b)
    return jnp.sum(loss_cols), grad_x, grads


BIG = ("w_in", "w_a_out", "glu_w", "glu_v", "w_out", "mlp_w_up", "mlp_w_down")
COL_SHARDED = ("w_in", "glu_w", "glu_v", "mlp_w_up")
SMALL = ("conv_w", "conv_b", "rg_wa", "rg_ba", "rg_wx", "rg_bx", "rg_lambda", "ssm_a_re", "ssm_a_im", "ssm_log_dt",
         "ssm_b_re", "ssm_b_im", "ssm_c_re", "ssm_c_im", "ssm_d", "ln1_g", "ln1_b", "mlp_b_up", "mlp_b_down", "ln2_g",
         "ln2_b")
ORDER = ("w_in", "conv_w", "conv_b", "rg_wa", "rg_ba", "rg_wx", "rg_bx", "rg_lambda", "w_a_out", "ssm_a_re",
         "ssm_a_im", "ssm_log_dt", "ssm_b_re", "ssm_b_im", "ssm_c_re", "ssm_c_im", "ssm_d", "glu_w", "glu_v", "w_out",
         "ln1_g", "ln1_b", "mlp_w_up", "mlp_b_up", "mlp_w_down", "mlp_b_down", "ln2_g", "ln2_b")
TILE_ELEMS = SUBLANES * LANES


def _pack(arrs):
    pieces = []
    for a in arrs:
        flat = a.reshape(-1)
        flat = jnp.pad(flat, (0, (-flat.shape[0]) % TILE_ELEMS))
        pieces.append(flat.reshape(-1, LANES))
    rows = sum(p.shape[0] for p in pieces)
    pad_rows = (-rows) % (N_DEV * SUBLANES)
    if pad_rows:
        pieces.append(jnp.zeros((pad_rows, LANES), pieces[0].dtype))
    return jnp.concatenate(pieces, axis=0)


def _unpack(packed, shapes):
    out, row = [], 0
    for shp in shapes:
        n = math.prod(shp)
        rows = -(-n // TILE_ELEMS) * SUBLANES
        out.append(packed[row:row + rows].reshape(-1)[:n].reshape(shp))
        row += rows
    return out


class _Comm:
    def __init__(self, w):
        first = [w["w_in"].astype(BF16), w["conv_w"]]
        self._first, self.first_token = _exchange_start(first, [_landing_zone(s) for s in first], mode="own",
                                                        name="gather_in_start")
        self._shards = {k: w[k].astype(BF16) for k in BIG if k != "w_in"}
        self._weights, self._gathers, self._grads = {}, {}, {}

    def first_weights(self, after):
        lands = [_exchange_wait(h, after, mode="own", name="gather_in_wait_%d" % i) for i, h in enumerate(self._first)]
        unused = [lax.empty((2 * SUBLANES, LANES), BF16) for _ in lands]
        handles, passed = _exchange_start(unused, lands, mode="pass", name="gather_in_pass")
        w_in, taps = [_exchange_wait(h, passed, mode="pass", name="gather_in_got_%d" % i) for i, h in enumerate(handles)]
        self._weights["w_in"] = w_in
        self.gather_token = self.start_weights(("w_a_out", "glu_w", "glu_v", "w_out"), w_in)
        return self.weight("w_in", None), jnp.swapaxes(taps, 0, 1).reshape(taps.shape[1], -1)

    def start_weights(self, names, after):
        shards = [self._shards.pop(k) for k in names]
        handles, token = _exchange_start(shards, [_landing_zone(s) for s in shards], mode="gather", after=after,
                                         name="gather_start_" + names[0])
        self._gathers.update(zip(names, handles))
        return token

    def weight(self, k, after):
        if k not in self._weights:
            self._weights[k] = _exchange_wait(self._gathers.pop(k), after, mode="gather", name="gather_wait_" + k)
        gk = self._weights[k]
        if k in COL_SHARDED:
            return jnp.swapaxes(gk, 0, 1).reshape(gk.shape[1], -1)
        return gk.reshape(-1, gk.shape[-1])

    def send_grad(self, *names_and_parts):
        names, parts = names_and_parts[0::2], names_and_parts[1::2]
        parts = [p if k in COL_SHARDED else p.reshape(N_DEV, p.shape[0] // N_DEV, p.shape[1])
                 for k, p in zip(names, parts)]
        me = _dev_index(*_mesh_pos())
        lands = [_landing_zone(lax.dynamic_index_in_dim(p, me, 0, keepdims=False)) for p in parts]
        handles, token = _exchange_start(parts, lands, mode="scatter", name="grad_start_" + names[0])
        self._grads.update(zip(names, handles))
        return token

    def received_grad(self, k, after):
        return _exchange_wait(self._grads.pop(k), after, mode="scatter", name="grad_wait_" + k)


SMALL_GROUPS = (("rg_wa", "rg_wx"), ("ssm_b_re",), ("ssm_b_im",),
                tuple(k for k in SMALL if k not in ("rg_wa", "rg_wx", "ssm_b_re", "ssm_b_im")))


def _step(x, tgt, w, m, v, raw_w, raw_m, raw_v):
    dev = _dev_index(*_mesh_pos())

    comm = _Comm(w)
    small = dict(w)
    for k in ("conv_b", "rg_ba", "rg_bx", "rg_lambda", "ln1_g", "ln1_b", "mlp_b_up", "mlp_b_down", "ln2_g", "ln2_b"):
        small[k] = w[k].reshape(1, -1)

    loss_part, grad_x, grads = _local_step(x, tgt, small, comm)

    out_g, out_d, out_m, out_v = {}, {}, {}, {}
    for k in BIG:
        rk = comm.received_grad(k, grad_x)
        out_g[k], out_d[k], out_m[k], out_v[k] = _adamw(rk, w[k], m[k], v[k], tr=128, name="adamw_" + k)

    small_shapes = [grads[k].shape for k in SMALL]
    (small_recv,) = _exchange_blocks([_pack([grads[k] for k in SMALL]).reshape(N_DEV, -1, LANES)],
                                     name="exchange_small_grads")
    small_block = _sum_slots(small_recv, tr=512, name="sum_small_grads")
    (small_all,) = _all_gather([small_block], name="gather_small_grads")
    g_small = dict(zip(SMALL, _unpack(small_all.reshape(-1, LANES), small_shapes)))
    cw_cols = w["conv_w"].shape[1]
    g_small["conv_w"] = lax.dynamic_slice_in_dim(g_small["conv_w"], dev * cw_cols, cw_cols, axis=1)
    for group in SMALL_GROUPS:
        gs = [g_small[k].reshape(raw_w[k].shape) for k in group]
        res = _adamw_whole(gs, [raw_w[k] for k in group], [raw_m[k] for k in group], [raw_v[k] for k in group],
                           name="adamw_" + group[0])
        for k, gk, (dk, mk, vk) in zip(group, gs, res):
            out_g[k], out_d[k], out_m[k], out_v[k] = gk, dk, mk, vk

    loss = lax.psum(loss_part, ("x", "y", "c"))
    return loss, grad_x, out_g, out_d, out_m, out_v


def kernel(x, w_in, conv_w, conv_b, rg_wa, rg_ba, rg_wx, rg_bx, rg_lambda, w_a_out, ssm_a_re, ssm_a_im, ssm_log_dt, ssm_b_re, ssm_b_im, ssm_c_re, ssm_c_im, ssm_d, glu_w, glu_v, w_out, ln1_g, ln1_b, mlp_w_up, mlp_b_up, mlp_w_down, mlp_b_down, ln2_g, ln2_b, loss_target, m_w_in, m_conv_w, m_conv_b, m_rg_wa, m_rg_ba, m_rg_wx, m_rg_bx, m_rg_lambda, m_w_a_out, m_ssm_a_re, m_ssm_a_im, m_ssm_log_dt, m_ssm_b_re, m_ssm_b_im, m_ssm_c_re, m_ssm_c_im, m_ssm_d, m_glu_w, m_glu_v, m_w_out, m_ln1_g, m_ln1_b, m_mlp_w_up, m_mlp_b_up, m_mlp_w_down, m_mlp_b_down, m_ln2_g, m_ln2_b, v_w_in, v_conv_w, v_conv_b, v_rg_wa, v_rg_ba, v_rg_wx, v_rg_bx, v_rg_lambda, v_w_a_out, v_ssm_a_re, v_ssm_a_im, v_ssm_log_dt, v_ssm_b_re, v_ssm_b_im, v_ssm_c_re, v_ssm_c_im, v_ssm_d, v_glu_w, v_glu_v, v_w_out, v_ln1_g, v_ln1_b, v_mlp_w_up, v_mlp_b_up, v_mlp_w_down, v_mlp_b_down, v_ln2_g, v_ln2_b):
    args = locals()
    w = {k: args[k][0] for k in ORDER}
    m = {k: args["m_" + k][0] for k in BIG}
    v = {k: args["v_" + k][0] for k in BIG}
    raw = [{k: args[prefix + k] for k in SMALL} for prefix in ("", "m_", "v_")]
    loss, grad_x, out_g, out_d, out_m, out_v = _step(x[0], loss_target[0], w, m, v, *raw)
    outs = [loss, grad_x[None]]
    for group in (out_g, out_d, out_m, out_v):
        outs += [group[k].reshape(args[k].shape) for k in ORDER]
    return tuple(outs)
```

```python
import functools
import math

import jax
import jax.numpy as jnp
from jax import lax
from jax.experimental import pallas as pl
from jax.experimental.pallas import tpu as pltpu

F32 = jnp.float32
BF16 = jnp.bfloat16
MESH = pl.DeviceIdType.MESH
N_DEV = 8
SUBLANES = 8
LANES = 128
VMEM_BYTES_V7X = 64 * 2 ** 20
VMEM_CAP = VMEM_BYTES_V7X - 8 * 2 ** 20

ALPHA = 2.0 ** 0.25
LN_EPS = 1e-5
RG_C = 8.0
ADAM_LR, ADAM_B1, ADAM_B2, ADAM_EPS, ADAM_WD, ADAM_STEP = 0.001, 0.9, 0.999, 1e-08, 0.01, 10
GELU_C = math.sqrt(2.0 / math.pi)
GELU_K = 0.044715

ANY = pl.BlockSpec(memory_space=pl.ANY)


def _params(sem, vmem_bytes):
    limit = int(min(max(2 * vmem_bytes, 16 * 2 ** 20), VMEM_CAP))
    return pltpu.CompilerParams(dimension_semantics=sem, vmem_limit_bytes=limit)


def _sig(x):
    return 1.0 / (1.0 + jnp.exp(-x))


def _gelu(x):
    return 0.5 * x * (1.0 + jnp.tanh(GELU_C * (x + GELU_K * x * x * x)))


def _dgelu(x):
    th = jnp.tanh(GELU_C * (x + GELU_K * x * x * x))
    return 0.5 * (1.0 + th) + 0.5 * x * (1.0 - th * th) * (GELU_C * (1.0 + 3.0 * GELU_K * x * x))


def _one_minus_exp(x, exp_half_x):
    p = x * (1.0 + x * (1 / 2 + x * (1 / 6 + x * (1 / 24 + x * (1 / 120)))))
    return jnp.where(x > -1 / 16, -p, 1.0 - exp_half_x * exp_half_x)


def _accumulate(ref, val, first):
    @pl.when(first)
    def _():
        ref[...] = val

    @pl.when(jnp.logical_not(first))
    def _():
        ref[...] += val


def _rows8(cw):
    return lax.broadcasted_iota(jnp.int32, (SUBLANES, cw), 0)


def _shift_down(cur, prev, s, rows):
    return jnp.where(rows < s, pltpu.roll(prev, s, 0), pltpu.roll(cur, s, 0))


def _shift_up(cur, nxt, s, rows):
    return jnp.where(rows < SUBLANES - s, pltpu.roll(cur, SUBLANES - s, 0), pltpu.roll(nxt, SUBLANES - s, 0))


def _mm(a, b, *, M, N, K, ta=False, tb=False, b_split=1, n_split=1, a_fn=None, extras=(), epilogue=None,
        n_out=1, n_cs=0, out_dtypes=None, tm=512, tn=512, tk=512, after=None, into=None, name):
    tm, tn, tk = min(tm, M), min(tn, N), min(tk, K)
    assert M % tm == 0 and N % tn == 0 and K % tk == 0, (name, M, N, K, tm, tn, tk)
    nk = K // tk
    grid = (N // tn, M // tm, nk)
    a_spec = pl.BlockSpec((tk, tm), lambda j, i, k: (k, i)) if ta else pl.BlockSpec((tm, tk), lambda j, i, k: (i, k))
    if b_split == 1:
        b_spec = pl.BlockSpec((tn, tk), lambda j, i, k: (j, k)) if tb else pl.BlockSpec((tk, tn), lambda j, i, k: (k, j))
    elif tb:
        kb = (K // b_split) // tk
        assert kb * tk * b_split == K, name
        b_spec = pl.BlockSpec((None, tn, tk), lambda j, i, k: (k // kb, j, k % kb))
    else:
        nb = (N // b_split) // tn
        assert nb * tn * b_split == N, name
        b_spec = pl.BlockSpec((None, tk, tn), lambda j, i, k: (j // nb, k, j % nb))
    in_specs = [a_spec, b_spec]
    for arr, kind, *col_off in extras:
        off = col_off[0] if col_off else 0
        in_specs.append(pl.BlockSpec((tm, tn), lambda j, i, k, off=off: (i, off + j)) if kind == "mn"
                        else pl.BlockSpec((1, tn), lambda j, i, k: (0, j)))
    out_dtypes = (F32,) * n_out if out_dtypes is None else out_dtypes
    if n_split == 1:
        out_shape = [jax.ShapeDtypeStruct((M, N), dt) for dt in out_dtypes]
        out_specs = [pl.BlockSpec((tm, tn), lambda j, i, k: (i, j)) for _ in range(n_out)]
    else:
        assert n_out == 1
        nbo = (N // n_split) // tn
        assert nbo * tn * n_split == N, name
        out_shape = [jax.ShapeDtypeStruct((n_split, M, N // n_split), out_dtypes[0])]
        out_specs = [pl.BlockSpec((None, tm, tn), lambda j, i, k: (j // nbo, i, j % nbo))]
    out_shape += [jax.ShapeDtypeStruct((1, N), F32) for _ in range(n_cs)]
    out_specs += [pl.BlockSpec((1, tn), lambda j, i, k: (0, j)) for _ in range(n_cs)]
    ne = len(extras)
    dims = (((0 if ta else 1,), (1 if tb else 0,)), ((), ()))

    n_after = 0 if after is None else 1
    in_specs += [ANY] * n_after
    aliases, tail = {}, [] if after is None else [after]
    if into is not None:
        buf, which, col_blk = into
        assert n_split == 1 and buf.shape[0] == M and buf.dtype == out_dtypes[which], name
        aliases = {len(in_specs): which}
        in_specs.append(ANY)
        tail.append(buf)
        out_shape[which] = jax.ShapeDtypeStruct(buf.shape, buf.dtype)
        out_specs[which] = pl.BlockSpec((tm, tn), lambda j, i, k: (i, col_blk + j))

    def body(*refs):
        a_ref, b_ref = refs[0], refs[1]
        ex_refs = refs[2:2 + ne]
        first_out = 2 + ne + len(tail)
        out_refs = refs[first_out:first_out + n_out]
        cs_refs = refs[first_out + n_out:first_out + n_out + n_cs]
        i, k = pl.program_id(1), pl.program_id(2)

        def product():
            av = a_ref[...]
            if a_fn is not None:
                av = a_fn(av.astype(F32))
            return lax.dot_general(av.astype(BF16), b_ref[...].astype(BF16), dims, preferred_element_type=F32)

        def finish(acc):
            res = (acc,) if epilogue is None else epilogue(acc, *[r[...] for r in ex_refs])
            for r, o in zip(out_refs, res[:n_out]):
                r[...] = o.astype(r.dtype)
            for r, cval in zip(cs_refs, res[n_out:]):
                _accumulate(r, jnp.sum(cval, axis=0, keepdims=True), i == 0)

        if nk == 1:
            finish(product())
            return
        acc_ref = refs[-1]

        @pl.when(k == 0)
        def _():
            acc_ref[...] = jnp.zeros_like(acc_ref)

        acc_ref[...] += product()

        @pl.when(k == nk - 1)
        def _():
            finish(acc_ref[...])

    vmem = 2 * tm * tk * a.dtype.itemsize + 2 * tk * tn * b.dtype.itemsize + (1 + 2 * n_out + 2 * ne + 2) * tm * tn * 4
    outs = pl.pallas_call(
        body, name=name, grid=grid, in_specs=in_specs, out_specs=out_specs, out_shape=out_shape,
        scratch_shapes=[pltpu.VMEM((tm, tn), F32)] if nk > 1 else [], input_output_aliases=aliases,
        compiler_params=_params(("parallel", "arbitrary", "arbitrary"), vmem),
    )(a, b, *[e[0] for e in extras], *tail)
    return outs[0] if len(outs) == 1 else outs


BD_STEP = 4

def _bd(pairs, *, T, J, kb, nb, tw=False, extras=(), epilogue=None, n_out=1, n_cs=0, out_dtypes=None, tm=512, name):
    jb = BD_STEP
    assert T % tm == 0 and J % jb == 0
    grid = (J // jb, T // tm)
    npair, ne = len(pairs), len(extras)
    in_specs, args = [], []
    for arr, off, w in pairs:
        assert off % jb == 0, name
        in_specs.append(pl.BlockSpec((tm, jb * kb), lambda j, i, off=off // jb: (i, off + j)))
        in_specs.append(pl.BlockSpec((jb,) + tuple(w.shape[1:]), lambda j, i: (j, 0, 0)))
        args += [arr, w]
    for arr, kind, off in extras:
        assert off % jb == 0, name
        in_specs.append(pl.BlockSpec((tm, jb * nb), lambda j, i, off=off // jb: (i, off + j)) if kind == "tile"
                        else pl.BlockSpec((1, jb * nb), lambda j, i, off=off // jb: (0, off + j)))
        args.append(arr)
    out_dtypes = (F32,) * n_out if out_dtypes is None else out_dtypes
    out_shape = [jax.ShapeDtypeStruct((T, J * nb), dt) for dt in out_dtypes]
    out_specs = [pl.BlockSpec((tm, jb * nb), lambda j, i: (i, j)) for _ in range(n_out)]
    out_shape += [jax.ShapeDtypeStruct((1, J * nb), F32) for _ in range(n_cs)]
    out_specs += [pl.BlockSpec((1, jb * nb), lambda j, i: (0, j)) for _ in range(n_cs)]
    dims = (((1,), (1 if tw else 0,)), ((), ()))

    def body(*refs):
        ex_refs = refs[2 * npair:2 * npair + ne]
        out_refs = refs[2 * npair + ne:2 * npair + ne + n_out]
        cs_refs = refs[2 * npair + ne + n_out:]
        i = pl.program_id(1)
        for s in range(jb):
            cols_in, cols_out = pl.ds(s * kb, kb), pl.ds(s * nb, nb)
            acc = None
            for p in range(npair):
                d = lax.dot_general(refs[2 * p][:, cols_in].astype(BF16), refs[2 * p + 1][s].astype(BF16), dims,
                                    preferred_element_type=F32)
                acc = d if acc is None else acc + d
            res = (acc,) if epilogue is None else epilogue(acc, *[r[:, cols_out] for r in ex_refs])
            for r, o in zip(out_refs, res[:n_out]):
                r[:, cols_out] = o.astype(r.dtype)
            for r, cval in zip(cs_refs, res[n_out:]):
                _accumulate(r.at[:, cols_out], jnp.sum(cval, axis=0, keepdims=True), i == 0)

    vmem = jb * (2 * npair * tm * kb + 2 * npair * kb * nb + (2 * n_out + 2 * ne + 3) * tm * nb) * 4
    outs = pl.pallas_call(
        body, name=name, grid=grid, in_specs=in_specs, out_specs=out_specs, out_shape=out_shape,
        compiler_params=_params(("parallel", "arbitrary"), vmem),
    )(*args)
    return outs[0] if len(outs) == 1 else outs


def _bdw(a, a_off, b, b_off, *, T, J, kb, nb, tm=512, name):
    jb = BD_STEP
    assert T % tm == 0 and J % jb == 0 and a_off % jb == 0 and b_off % jb == 0
    a_blk, b_blk = a_off // jb, b_off // jb

    def body(a_ref, b_ref, o_ref):
        i = pl.program_id(1)
        for s in range(jb):
            d = lax.dot_general(a_ref[:, pl.ds(s * kb, kb)].astype(BF16), b_ref[:, pl.ds(s * nb, nb)].astype(BF16),
                                (((0,), (0,)), ((), ())), preferred_element_type=F32)
            _accumulate(o_ref.at[s], d, i == 0)

    return pl.pallas_call(
        body, name=name, grid=(J // jb, T // tm),
        in_specs=[pl.BlockSpec((tm, jb * kb), lambda j, i: (i, a_blk + j)),
                  pl.BlockSpec((tm, jb * nb), lambda j, i: (i, b_blk + j))],
        out_specs=pl.BlockSpec((jb, kb, nb), lambda j, i: (j, 0, 0)),
        out_shape=jax.ShapeDtypeStruct((J, kb, nb), F32),
        compiler_params=_params(("parallel", "arbitrary"), jb * (2 * tm * (kb + nb) + 3 * kb * nb) * 4),
    )(a, b)


def _bd_pack(w, q):
    g, a, b = w.shape
    eye = jnp.eye(q, dtype=w.dtype)
    return jnp.einsum("jqab,qr->jqarb", w.reshape(g // q, q, a, b), eye).reshape(g // q, q * a, q * b)


def _bd_unpack(wp, q):
    j, qa, qb = wp.shape
    a, b = qa // q, qb // q
    w5 = wp.reshape(j, q, a, q, b)
    return jnp.stack([w5[:, r, :, r, :] for r in range(q)], axis=1).reshape(j * q, a, b)


def _ew(fn, ins, *, T, C, n_out, n_cs=0, out_dtypes=None, tm=256, cw=None, name):
    cw = C if cw is None else cw
    assert T % tm == 0 and C % cw == 0
    grid = (C // cw, T // tm)
    in_specs = []
    for arr, kind, off in ins:
        in_specs.append(pl.BlockSpec((tm, cw), lambda j, i, off=off: (i, off + j)) if kind == "tile"
                        else pl.BlockSpec((arr.shape[0], cw), lambda j, i, off=off: (0, off + j)))
    out_dtypes = (F32,) * n_out if out_dtypes is None else out_dtypes
    out_shape = [jax.ShapeDtypeStruct((T, C), dt) for dt in out_dtypes]
    out_specs = [pl.BlockSpec((tm, cw), lambda j, i: (i, j)) for _ in range(n_out)]
    out_shape += [jax.ShapeDtypeStruct((1, C), F32) for _ in range(n_cs)]
    out_specs += [pl.BlockSpec((1, cw), lambda j, i: (0, j)) for _ in range(n_cs)]
    nin = len(ins)

    def body(*refs):
        i = pl.program_id(1)
        res = fn(*[r[...].astype(F32) for r in refs[:nin]])
        for r, o in zip(refs[nin:nin + n_out], res[:n_out]):
            r[...] = o.astype(r.dtype)
        for r, cval in zip(refs[nin + n_out:], res[n_out:]):
            _accumulate(r, jnp.sum(cval, axis=0, keepdims=True), i == 0)

    vmem = (2 * nin + 2 * n_out + 6) * tm * cw * 4
    outs = pl.pallas_call(
        body, name=name, grid=grid, in_specs=in_specs, out_specs=out_specs, out_shape=out_shape,
        compiler_params=_params(("parallel", "arbitrary"), vmem),
    )(*[arr for arr, _, _ in ins])
    return outs[0] if len(outs) == 1 else outs


def _ln_stats(s):
    mu = jnp.mean(s, axis=-1, keepdims=True)
    d = s - mu
    var = jnp.mean(d * d, axis=-1, keepdims=True)
    rstd = lax.rsqrt(var + LN_EPS)
    return d * rstd, rstd


def _ln_bwd(dy, g, xhat, rstd):
    dxh = dy * g
    m1 = jnp.mean(dxh, axis=-1, keepdims=True)
    m2 = jnp.mean(dxh * xhat, axis=-1, keepdims=True)
    return rstd * (dxh - m1 - xhat * m2)


def _conv_fwd(z, conv_w, conv_b, *, T, C, tm=512, cw=1024, after=None, name):
    ng, hb = tm // SUBLANES, tm // SUBLANES
    n_after = 0 if after is None else 1

    def body(x_ref, halo_ref, w_ref, b_ref, *rest):
        o_ref = rest[-1]
        it = pl.program_id(1)
        rows = _rows8(cw)
        halo = jnp.where(it == 0, 0.0, halo_ref[...])
        w = w_ref[...]
        bias = b_ref[...]

        def group(g, carry):
            off = pl.multiple_of(g * SUBLANES, SUBLANES)
            cur = x_ref[pl.ds(off, SUBLANES), :]
            prev = x_ref[pl.ds(pl.multiple_of(jnp.maximum(off - SUBLANES, 0), SUBLANES), SUBLANES), :]
            prev = jnp.where(g == 0, halo, prev)
            acc = cur * w[3:4] + bias
            for s in (1, 2, 3):
                acc = acc + _shift_down(cur, prev, s, rows) * w[3 - s:4 - s]
            o_ref[pl.ds(off, SUBLANES), :] = acc
            return carry

        lax.fori_loop(0, ng, group, 0, unroll=2)

    return pl.pallas_call(
        body, name=name, grid=(C // cw, T // tm),
        in_specs=[pl.BlockSpec((tm, cw), lambda j, i: (i, j)),
                  pl.BlockSpec((SUBLANES, cw), lambda j, i: (jnp.maximum(i * hb - 1, 0), j)),
                  pl.BlockSpec((4, cw), lambda j, i: (0, j)), pl.BlockSpec((1, cw), lambda j, i: (0, j))]
        + [ANY] * n_after,
        out_specs=pl.BlockSpec((tm, cw), lambda j, i: (i, j)),
        out_shape=jax.ShapeDtypeStruct((T, C), F32),
        compiler_params=_params(("parallel", "arbitrary"), 5 * tm * cw * 4),
    )(z, z, conv_w, conv_b, *([after] if n_after else []))


def _conv_bwd(dxc, z, conv_w, dz, *, T, C, tm=512, cw=512, name):
    ng, hb, last = tm // SUBLANES, tm // SUBLANES, T // SUBLANES - 1
    nt = T // tm
    rows16 = 2 * SUBLANES

    def body(d_ref, dn_ref, x_ref, w_ref, dz_in_ref, o_ref, sums_ref):
        it = pl.program_id(1)
        rows = _rows8(cw)
        dnext = jnp.where(it == nt - 1, 0.0, dn_ref[...])
        w = w_ref[...]

        def pair(q, accs):
            base = pl.multiple_of(q * rows16, rows16)
            halves = []
            for half in range(2):
                g = 2 * q + half
                off = pl.multiple_of(base + half * SUBLANES, SUBLANES)
                dcur = d_ref[pl.ds(off, SUBLANES), :]
                dnx = d_ref[pl.ds(pl.multiple_of(jnp.minimum(off + SUBLANES, tm - SUBLANES), SUBLANES), SUBLANES), :]
                dnx = jnp.where(g == ng - 1, dnext, dnx)
                xcur = x_ref[pl.ds(off, SUBLANES), :]
                acc = dcur * w[3:4]
                taps = [accs[3] + dcur * xcur]
                for s in (1, 2, 3):
                    ahead = _shift_up(dcur, dnx, s, rows)
                    acc = acc + ahead * w[3 - s:4 - s]
                    taps.append(accs[3 - s] + ahead * xcur)
                halves.append(acc)
                accs = (taps[3], taps[2], taps[1], taps[0], accs[4] + dcur)
            o_ref[pl.ds(base, rows16), :] = jnp.concatenate(halves, axis=0).astype(o_ref.dtype)
            return accs

        zero = jnp.zeros((SUBLANES, cw), F32)
        accs = lax.fori_loop(0, ng // 2, pair, (zero,) * 5)
        sums = jnp.zeros((SUBLANES, cw), F32)
        for k, a in enumerate(accs):
            sums = jnp.where(rows == k, jnp.sum(a, axis=0, keepdims=True), sums)
        _accumulate(sums_ref, sums, it == 0)

    tile = pl.BlockSpec((tm, cw), lambda j, i: (i, j))
    return pl.pallas_call(
        body, name=name, grid=(C // cw, nt),
        in_specs=[tile, pl.BlockSpec((SUBLANES, cw), lambda j, i: (jnp.minimum((i + 1) * hb, last), j)),
                  tile, pl.BlockSpec((4, cw), lambda j, i: (0, j)), ANY],
        out_specs=[tile, pl.BlockSpec((SUBLANES, cw), lambda j, i: (0, j))],
        input_output_aliases={4: 0},
        out_shape=[jax.ShapeDtypeStruct(dz.shape, dz.dtype), jax.ShapeDtypeStruct((SUBLANES, C), F32)],
        compiler_params=_params(("parallel", "arbitrary"), 7 * tm * cw * 4),
    )(dxc, dxc, z, conv_w, dz)


def _rg_coeffs(r, ig, xc, sp):
    la = (-RG_C) * r * sp
    a = jnp.exp(la)
    m = jnp.sqrt(_one_minus_exp(2.0 * la, a))
    return a, m, m * (ig * xc)


def _rg_scan_fwd(z, ri, xc, sp, *, T, C, gate_off, tm=512, cw=256, name):
    rows16 = 2 * SUBLANES
    nq = tm // rows16

    def body(gate_ref, r_ref, i_ref, xc_ref, sp_ref, h_ref, p_ref, a_ref, m_ref, carry_ref):
        it = pl.program_id(1)

        @pl.when(it == 0)
        def _():
            carry_ref[...] = jnp.zeros_like(carry_ref)

        rows = _rows8(cw)
        sp_row = sp_ref[...]

        def pair(q, carry):
            base = pl.multiple_of(q * rows16, rows16)
            halves = []
            for half in range(2):
                sl = pl.ds(pl.multiple_of(base + half * SUBLANES, SUBLANES), SUBLANES)
                a, m, b = _rg_coeffs(r_ref[sl, :], i_ref[sl, :], xc_ref[sl, :], sp_row)
                a_ref[sl, :] = a
                m_ref[sl, :] = m
                for s in (1, 2, 4):
                    keep = rows >= s
                    sa = jnp.where(keep, pltpu.roll(a, s, 0), 1.0)
                    sb = jnp.where(keep, pltpu.roll(b, s, 0), 0.0)
                    b = b + a * sb
                    a = a * sa
                h = b + a * carry
                h_ref[sl, :] = h
                halves.append(h * _gelu(gate_ref[sl, :]))
                carry = h[SUBLANES - 1:SUBLANES, :]
            p_ref[pl.ds(base, rows16), :] = jnp.concatenate(halves, axis=0).astype(p_ref.dtype)
            return carry

        last = lax.fori_loop(0, nq, pair, carry_ref[0:1, :], unroll=2)
        carry_ref[...] = jnp.broadcast_to(last, carry_ref.shape)

    tile = pl.BlockSpec((tm, cw), lambda j, i: (i, j))
    gate_blk = gate_off // cw
    return pl.pallas_call(
        body, name=name, grid=(C // cw, T // tm),
        in_specs=[pl.BlockSpec((tm, cw), lambda j, i: (i, gate_blk + j)),
                  pl.BlockSpec((tm, cw), lambda j, i: (i, 2 * j)), pl.BlockSpec((tm, cw), lambda j, i: (i, 2 * j + 1)),
                  tile, pl.BlockSpec((1, cw), lambda j, i: (0, j))],
        out_specs=[tile, tile, tile, tile],
        out_shape=[jax.ShapeDtypeStruct((T, C), F32), jax.ShapeDtypeStruct((T, C), BF16),
                   jax.ShapeDtypeStruct((T, C), F32), jax.ShapeDtypeStruct((T, C), F32)],
        scratch_shapes=[pltpu.VMEM((SUBLANES, cw), F32)],
        compiler_params=_params(("parallel", "arbitrary"), 16 * tm * cw * 4),
    )(z, ri, ri, xc, sp)


def _rg_scan_bwd(dh, h, ri, xc, a_fwd, m_fwd, sp, *, T, C, tm=512, cw=256, name):
    ng, hb, nt = tm // SUBLANES, tm // SUBLANES, T // tm

    def body(dh_ref, h_ref, hp_ref, r_ref, i_ref, xc_ref, a_ref, m_ref, sp_ref,
             drai_ref, dxc_ref, crai_ref, csp_ref, cg_ref, ca_ref):
        step = pl.program_id(1)

        @pl.when(step == 0)
        def _():
            cg_ref[...] = jnp.zeros_like(cg_ref)
            ca_ref[...] = jnp.zeros_like(ca_ref)

        rows = _rows8(cw)
        sp_row = sp_ref[...]
        hhalo = jnp.where(step == nt - 1, 0.0, hp_ref[...])

        def group(gi, carry):
            g_next, a_next, s_ra, s_ia, s_sp = carry
            g = ng - 1 - gi
            off = pl.multiple_of(g * SUBLANES, SUBLANES)
            sl = pl.ds(off, SUBLANES)
            rr, ii, xx = r_ref[sl, :], i_ref[sl, :], xc_ref[sl, :]
            a, m = a_ref[sl, :], m_ref[sl, :]
            hh = h_ref[sl, :]
            hpv = h_ref[pl.ds(pl.multiple_of(jnp.maximum(off - SUBLANES, 0), SUBLANES), SUBLANES), :]
            hpv = jnp.where(g == 0, hhalo, hpv)
            hprev = _shift_down(hh, hpv, 1, rows)
            d = dh_ref[sl, :]
            c = jnp.where(rows < SUBLANES - 1, pltpu.roll(a, SUBLANES - 1, 0), a_next)
            for s in (1, 2, 4):
                keep = rows < SUBLANES - s
                sc = jnp.where(keep, pltpu.roll(c, SUBLANES - s, 0), 1.0)
                sd = jnp.where(keep, pltpu.roll(d, SUBLANES - s, 0), 0.0)
                d = d + c * sd
                c = c * sc
            gg = d + c * g_next
            da = gg * hprev
            dm = gg * (ii * xx)
            di = gg * (m * xx)
            dxc_ref[sl, :] = gg * (m * ii)
            dla = da * a - dm * (a * a / m)
            dra = dla * ((-RG_C) * sp_row) * (rr * (1.0 - rr))
            dia = di * (ii * (1.0 - ii))
            drai_ref[sl, pl.ds(0, cw)] = dra
            drai_ref[sl, pl.ds(cw, cw)] = dia
            return (gg[0:1, :], a[0:1, :], s_ra + dra, s_ia + dia, s_sp + dla * ((-RG_C) * rr))

        zero = jnp.zeros((SUBLANES, cw), F32)
        g_first, a_first, s_ra, s_ia, s_sp = lax.fori_loop(
            0, ng, group, (cg_ref[0:1, :], ca_ref[0:1, :], zero, zero, zero), unroll=2)
        cg_ref[...] = jnp.broadcast_to(g_first, cg_ref.shape)
        ca_ref[...] = jnp.broadcast_to(a_first, ca_ref.shape)
        for ref, acc in ((crai_ref.at[:, pl.ds(0, cw)], s_ra), (crai_ref.at[:, pl.ds(cw, cw)], s_ia), (csp_ref, s_sp)):
            _accumulate(ref, jnp.sum(acc, axis=0, keepdims=True), step == 0)

    tile = pl.BlockSpec((tm, cw), lambda j, i: (nt - 1 - i, j))
    wide = pl.BlockSpec((tm, 2 * cw), lambda j, i: (nt - 1 - i, j))
    vec = pl.BlockSpec((1, cw), lambda j, i: (0, j))
    return pl.pallas_call(
        body, name=name, grid=(C // cw, nt),
        in_specs=[tile, tile, pl.BlockSpec((SUBLANES, cw), lambda j, i: (jnp.maximum((nt - 1 - i) * hb - 1, 0), j)),
                  pl.BlockSpec((tm, cw), lambda j, i: (nt - 1 - i, 2 * j)),
                  pl.BlockSpec((tm, cw), lambda j, i: (nt - 1 - i, 2 * j + 1)), tile, tile, tile, vec],
        out_specs=[wide, tile, pl.BlockSpec((1, 2 * cw), lambda j, i: (0, j)), vec],
        out_shape=[jax.ShapeDtypeStruct((T, 2 * C), F32), jax.ShapeDtypeStruct((T, C), F32),
                   jax.ShapeDtypeStruct((1, 2 * C), F32), jax.ShapeDtypeStruct((1, C), F32)],
        scratch_shapes=[pltpu.VMEM((SUBLANES, cw), F32), pltpu.VMEM((SUBLANES, cw), F32)],
        compiler_params=_params(("parallel", "arbitrary"), 24 * tm * cw * 4),
    )(dh, h, h, ri, ri, xc, a_fwd, m_fwd, sp)


def _cscan_tables(lr, li, reverse):
    lam = (lr.reshape(-1), -li.reshape(-1) if reverse else li.reshape(-1))

    def mul(p, q):
        return p[0] * q[0] - p[1] * q[1], p[0] * q[1] + p[1] * q[0]

    pows = [lam]
    for _ in range(SUBLANES - 1):
        pows.append(mul(pows[-1], lam))
    zero = jnp.zeros_like(lam[0])
    tab = jnp.stack([pows[0][0], pows[0][1], pows[1][0], pows[1][1], pows[3][0], pows[3][1], zero, zero])
    if reverse:
        pows = pows[::-1]
    return tab, jnp.stack([p[0] for p in pows]), jnp.stack([p[1] for p in pows])


def _power_slabs(lr, li, n):
    pr, pi = lr.reshape(1, -1), li.reshape(1, -1)
    while pr.shape[0] < n:
        tr, ti = pr[-1:], pi[-1:]
        pr, pi = (jnp.concatenate([pr, pr * tr - pi * ti], axis=0), jnp.concatenate([pi, pr * ti + pi * tr], axis=0))
    return jnp.repeat(pr, SUBLANES, axis=0), jnp.repeat(pi, SUBLANES, axis=0), pr[-1], pi[-1]


def _rows_to_segments(src_ref, dst_ref):
    seg = src_ref.shape[0] // SUBLANES
    for g in range(seg):
        dst_ref[pl.ds(g * SUBLANES, SUBLANES), :] = src_ref[pl.ds(g, SUBLANES, stride=seg), :].astype(dst_ref.dtype)


def _segments_to_rows(src_ref, dst_ref):
    seg = src_ref.shape[0] // SUBLANES
    for r in range(SUBLANES):
        dst_ref[pl.ds(r * seg, seg), :] = src_ref[pl.ds(r, seg, stride=SUBLANES), :].astype(dst_ref.dtype)


def _seg_scan_tile(xr_ref, xi_ref, pbr_ref, pbi_ref, tab_ref, pwr_ref, pwi_ref, cr_ref, ci_ref, *, reverse, h=None):
    tm, cw = xr_ref.shape
    seg = tm // SUBLANES
    rows = _rows8(cw)
    sign = -1.0 if reverse else 1.0
    l_re, l_im = pbr_ref[0:1, :], sign * pbi_ref[0:1, :]

    def slab(g):
        return pl.ds(pl.multiple_of(g * SUBLANES, SUBLANES), SUBLANES)

    def local(k, state):
        sl = slab(seg - 1 - k if reverse else k)
        sr, si = state
        nr = xr_ref[sl, :] + (l_re * sr - l_im * si)
        ni = xi_ref[sl, :] + (l_re * si + l_im * sr)
        xr_ref[sl, :] = nr
        xi_ref[sl, :] = ni
        return nr, ni

    zero = jnp.zeros((SUBLANES, cw), F32)
    er, ei = lax.fori_loop(0, seg, local, (zero, zero), unroll=2)

    for k, s in enumerate((1, 2, 4)):
        shift = SUBLANES - s if reverse else s
        keep = rows < SUBLANES - s if reverse else rows >= s
        sr = jnp.where(keep, pltpu.roll(er, shift, 0), 0.0)
        si = jnp.where(keep, pltpu.roll(ei, shift, 0), 0.0)
        m_re, m_im = tab_ref[2 * k:2 * k + 1, :], tab_ref[2 * k + 1:2 * k + 2, :]
        er, ei = er + (m_re * sr - m_im * si), ei + (m_re * si + m_im * sr)
    cin_r, cin_i = cr_ref[0:1, :], ci_ref[0:1, :]
    pwr, pwi = pwr_ref[...], pwi_ref[...]
    er, ei = er + (pwr * cin_r - pwi * cin_i), ei + (pwr * cin_i + pwi * cin_r)
    if reverse:
        ent_r = jnp.where(rows == SUBLANES - 1, cin_r, pltpu.roll(er, SUBLANES - 1, 0))
        ent_i = jnp.where(rows == SUBLANES - 1, cin_i, pltpu.roll(ei, SUBLANES - 1, 0))
        out_r, out_i = er[0:1, :], ei[0:1, :]
    else:
        ent_r = jnp.where(rows == 0, cin_r, pltpu.roll(er, 1, 0))
        ent_i = jnp.where(rows == 0, cin_i, pltpu.roll(ei, 1, 0))
        out_r, out_i = er[SUBLANES - 1:SUBLANES, :], ei[SUBLANES - 1:SUBLANES, :]
    cr_ref[...] = jnp.broadcast_to(out_r, cr_ref.shape)
    ci_ref[...] = jnp.broadcast_to(out_i, ci_ref.shape)

    if h is not None:
        hr_ref, hi_ref, hr_last, hi_last = h
        hr_wrap = _shift_down(hr_ref[pl.ds(tm - SUBLANES, SUBLANES), :], hr_last, 1, rows)
        hi_wrap = _shift_down(hi_ref[pl.ds(tm - SUBLANES, SUBLANES), :], hi_last, 1, rows)

    def fix(g, sums):
        sl = slab(g)
        power = slab(seg - 1 - g) if reverse else sl
        pr, pi = pbr_ref[power, :], sign * pbi_ref[power, :]
        nr = xr_ref[sl, :] + (pr * ent_r - pi * ent_i)
        ni = xi_ref[sl, :] + (pr * ent_i + pi * ent_r)
        xr_ref[sl, :] = nr
        xi_ref[sl, :] = ni
        if h is None:
            return sums
        before = slab(jnp.maximum(g - 1, 0))
        hr1 = jnp.where(g == 0, hr_wrap, hr_ref[before, :])
        hi1 = jnp.where(g == 0, hi_wrap, hi_ref[before, :])
        return sums[0] + (nr * hr1 + ni * hi1), sums[1] + (ni * hr1 - nr * hi1)

    return lax.fori_loop(0, seg, fix, (zero, zero) if h is not None else (), unroll=2)


S5_TILE = 512


def _s5_fwd(z, u_off, wb_re, wb_im, wc_re, wc_im_neg, d_row, powers, *, T, tm=S5_TILE, name):
    J, ku, kp = wb_re.shape
    nt = T // tm
    pb_re, pb_im, top_re, top_im = powers
    tab, pw_re, pw_im = _cscan_tables(top_re, top_im, False)
    u_blk = u_off // ku

    def body(u_ref, wbr_ref, wbi_ref, wcr_ref, wci_ref, d_ref, pbr_ref, pbi_ref, tab_ref, pwr_ref, pwi_ref,
             hr_ref, hi_ref, y_ref, yg_ref, cr_ref, ci_ref, us_ref, ys_ref):
        @pl.when(pl.program_id(1) == 0)
        def _():
            cr_ref[...] = jnp.zeros_like(cr_ref)
            ci_ref[...] = jnp.zeros_like(ci_ref)

        _rows_to_segments(u_ref, us_ref)
        u = us_ref[...]
        ub = u.astype(BF16)
        hr_ref[...] = jnp.dot(ub, wbr_ref[...], preferred_element_type=F32)
        hi_ref[...] = jnp.dot(ub, wbi_ref[...], preferred_element_type=F32)
        _seg_scan_tile(hr_ref, hi_ref, pbr_ref, pbi_ref, tab_ref, pwr_ref, pwi_ref, cr_ref, ci_ref, reverse=False)
        y = (jnp.dot(hr_ref[...].astype(BF16), wcr_ref[...], preferred_element_type=F32)
             + jnp.dot(hi_ref[...].astype(BF16), wci_ref[...], preferred_element_type=F32) + d_ref[...] * u)
        ys_ref[...] = y
        _segments_to_rows(ys_ref, y_ref)
        ys_ref[...] = _gelu(y)
        _segments_to_rows(ys_ref, yg_ref)

    wb_spec = pl.BlockSpec((None, ku, kp), lambda j, i: (j, 0, 0))
    wc_spec = pl.BlockSpec((None, kp, ku), lambda j, i: (j, 0, 0))
    small = pl.BlockSpec((SUBLANES, kp), lambda j, i: (0, j))
    slabs = pl.BlockSpec((tm, kp), lambda j, i: (0, j))
    state = pl.BlockSpec((tm, kp), lambda j, i: (i, j))
    chan = pl.BlockSpec((tm, ku), lambda j, i: (i, j))
    return pl.pallas_call(
        body, name=name, grid=(J, nt),
        in_specs=[pl.BlockSpec((tm, ku), lambda j, i: (i, u_blk + j)), wb_spec, wb_spec, wc_spec, wc_spec,
                  pl.BlockSpec((1, ku), lambda j, i: (0, j)), slabs, slabs, small, small, small],
        out_specs=[state, state, chan, chan],
        out_shape=[jax.ShapeDtypeStruct((T, J * kp), F32)] * 2
        + [jax.ShapeDtypeStruct((T, J * ku), F32), jax.ShapeDtypeStruct((T, J * ku), BF16)],
        scratch_shapes=[pltpu.VMEM((SUBLANES, kp), F32), pltpu.VMEM((SUBLANES, kp), F32),
                        pltpu.VMEM((tm, ku), F32), pltpu.VMEM((tm, ku), F32)],
        compiler_params=_params(("parallel", "arbitrary"), 14 * tm * kp * 4),
    )(z, wb_re, wb_im, wc_re, wc_im_neg, d_row, pb_re, pb_im, tab, pw_re, pw_im)


def _s5_bwd(dy, z, u_off, h_re, h_im, wb_re, wb_im, wc_re, wc_im_neg, d_row, powers, dz, *, T, tm=S5_TILE, name):
    J, ku, kp = wb_re.shape
    nt, hb = T // tm, tm // SUBLANES
    pb_re, pb_im, top_re, top_im = powers
    tab, pw_re, pw_im = _cscan_tables(top_re, top_im, True)
    u_blk = u_off // ku
    contract_rows = (((0,), (0,)), ((), ()))
    contract_cols = (((1,), (1,)), ((), ()))

    def body(dy_ref, u_ref, hr_ref, hrp_ref, hi_ref, hip_ref, wbr_ref, wbi_ref, wcr_ref, wci_ref, d_ref,
             pbr_ref, pbi_ref, tab_ref, pwr_ref, pwi_ref, dz_in_ref,
             du_ref, dlr_ref, dli_ref, dd_ref, dwbr_ref, dwbi_ref, dwcr_ref, dwci_ref,
             gr_ref, gi_ref, cr_ref, ci_ref, dys_ref, us_ref):
        step = pl.program_id(1)
        first = step == 0

        @pl.when(first)
        def _():
            cr_ref[...] = jnp.zeros_like(cr_ref)
            ci_ref[...] = jnp.zeros_like(ci_ref)

        _rows_to_segments(dy_ref, dys_ref)
        _rows_to_segments(u_ref, us_ref)
        dy_t, u = dys_ref[...], us_ref[...]
        dyb, ub = dy_t.astype(BF16), u.astype(BF16)
        gr_ref[...] = lax.dot_general(dyb, wcr_ref[...], contract_cols, preferred_element_type=F32)
        gi_ref[...] = lax.dot_general(dyb, wci_ref[...], contract_cols, preferred_element_type=F32)
        hr_last = jnp.where(step == nt - 1, 0.0, hrp_ref[...])
        hi_last = jnp.where(step == nt - 1, 0.0, hip_ref[...])
        s_re, s_im = _seg_scan_tile(gr_ref, gi_ref, pbr_ref, pbi_ref, tab_ref, pwr_ref, pwi_ref, cr_ref, ci_ref,
                                    reverse=True, h=(hr_ref, hi_ref, hr_last, hi_last))
        _accumulate(dlr_ref, jnp.sum(s_re, axis=0, keepdims=True), first)
        _accumulate(dli_ref, jnp.sum(s_im, axis=0, keepdims=True), first)
        grb, gib = gr_ref[...].astype(BF16), gi_ref[...].astype(BF16)
        du = (lax.dot_general(grb, wbr_ref[...], contract_cols, preferred_element_type=F32)
              + lax.dot_general(gib, wbi_ref[...], contract_cols, preferred_element_type=F32) + dy_t * d_ref[...])
        dys_ref[...] = du
        _segments_to_rows(dys_ref, du_ref)
        _accumulate(dd_ref, jnp.sum(dy_t * u, axis=0, keepdims=True), first)
        _accumulate(dwbr_ref, lax.dot_general(ub, grb, contract_rows, preferred_element_type=F32), first)
        _accumulate(dwbi_ref, lax.dot_general(ub, gib, contract_rows, preferred_element_type=F32), first)
        _accumulate(dwcr_ref, lax.dot_general(dyb, hr_ref[...].astype(BF16), contract_rows,
                                              preferred_element_type=F32), first)
        _accumulate(dwci_ref, lax.dot_general(dyb, hi_ref[...].astype(BF16), contract_rows,
                                              preferred_element_type=F32), first)

    def tix(i):
        return nt - 1 - i

    wb_spec = pl.BlockSpec((None, ku, kp), lambda j, i: (j, 0, 0))
    wc_spec = pl.BlockSpec((None, kp, ku), lambda j, i: (j, 0, 0))
    small = pl.BlockSpec((SUBLANES, kp), lambda j, i: (0, j))
    state = pl.BlockSpec((tm, kp), lambda j, i: (tix(i), j))
    halo = pl.BlockSpec((SUBLANES, kp), lambda j, i: (jnp.maximum(tix(i) * hb - 1, 0), j))
    chan = pl.BlockSpec((tm, ku), lambda j, i: (tix(i), j))
    svec = pl.BlockSpec((1, kp), lambda j, i: (0, j))
    cvec = pl.BlockSpec((1, ku), lambda j, i: (0, j))
    slabs = pl.BlockSpec((tm, kp), lambda j, i: (0, j))
    return pl.pallas_call(
        body, name=name, grid=(J, nt),
        in_specs=[chan, pl.BlockSpec((tm, ku), lambda j, i: (tix(i), u_blk + j)), state, halo, state, halo,
                  wb_spec, wb_spec, wc_spec, wc_spec, cvec, slabs, slabs, small, small, small, ANY],
        out_specs=[pl.BlockSpec((tm, ku), lambda j, i: (tix(i), u_blk + j)), svec, svec, cvec,
                   wb_spec, wb_spec, wb_spec, wb_spec],
        input_output_aliases={16: 0},
        out_shape=[jax.ShapeDtypeStruct(dz.shape, dz.dtype), jax.ShapeDtypeStruct((1, J * kp), F32),
                   jax.ShapeDtypeStruct((1, J * kp), F32), jax.ShapeDtypeStruct((1, J * ku), F32),
                   jax.ShapeDtypeStruct((J, ku, kp), F32), jax.ShapeDtypeStruct((J, ku, kp), F32),
                   jax.ShapeDtypeStruct((J, ku, kp), F32), jax.ShapeDtypeStruct((J, ku, kp), F32)],
        scratch_shapes=[pltpu.VMEM((tm, kp), F32), pltpu.VMEM((tm, kp), F32),
                        pltpu.VMEM((SUBLANES, kp), F32), pltpu.VMEM((SUBLANES, kp), F32),
                        pltpu.VMEM((tm, ku), F32), pltpu.VMEM((tm, ku), F32)],
        compiler_params=_params(("parallel", "arbitrary"), 16 * tm * kp * 4),
    )(dy, z, h_re, h_re, h_im, h_im, wb_re, wb_im, wc_re, wc_im_neg, d_row, pb_re, pb_im, tab, pw_re, pw_im, dz)


def _mesh_pos():
    return lax.axis_index("x"), lax.axis_index("y"), lax.axis_index("c")


def _dev_index(px, py, pc):
    return 4 * px + 2 * py + pc


def _all_gather(shards, name):
    n = len(shards)

    def body(*refs):
        ins, outs = refs[:n], refs[n:2 * n]
        send_sems, recv_sems, local_sems = refs[2 * n:]
        x, y, c = _mesh_pos()
        me, sibling = (x, y, c), (x, y, 1 - c)
        chips = [(1 - x, y), (x, 1 - y), (1 - x, 1 - y)]

        def copy(a, k, block, to, src=None):
            dst = outs[a].at[_dev_index(*block)]
            return pltpu.make_async_remote_copy(
                src_ref=dst if src is None else src, dst_ref=dst, send_sem=send_sems.at[a * 7 + k],
                recv_sem=recv_sems.at[a * 7 + k], device_id=to, device_id_type=MESH)

        mine = [pltpu.make_async_copy(ins[a], outs[a].at[_dev_index(*me)], local_sems.at[a]) for a in range(n)]
        for cp in mine:
            cp.start()
        first = []
        for a in range(n):
            first.append(copy(a, 0, me, sibling, src=ins[a]))
            first += [copy(a, 1 + j, me, (*chip, c), src=ins[a]) for j, chip in enumerate(chips)]
        for cp in first:
            cp.start()
        passed = []
        for j, chip in enumerate(chips):
            for a in range(n):
                copy(a, 1 + j, (*chip, c), me).wait_recv()
                fwd = copy(a, 4 + j, (*chip, c), sibling)
                fwd.start()
                passed.append(fwd)
        for a in range(n):
            copy(a, 0, sibling, me).wait_recv()
            for j, chip in enumerate(chips):
                copy(a, 4 + j, (*chip, 1 - c), me).wait_recv()
        for cp in first + passed:
            cp.wait_send()
        for cp in mine:
            cp.wait()

    return pl.pallas_call(
        body, name=name, in_specs=[ANY] * n, out_specs=[ANY] * n,
        out_shape=[jax.ShapeDtypeStruct((N_DEV,) + s.shape, s.dtype) for s in shards],
        scratch_shapes=[pltpu.SemaphoreType.DMA((7 * n,)), pltpu.SemaphoreType.DMA((7 * n,)),
                        pltpu.SemaphoreType.DMA((n,))],
    )(*shards)


def _exchange_blocks(parts, name):
    n = len(parts)
    relations = [(dx, dy, dc) for dx in (0, 1) for dy in (0, 1) for dc in (0, 1) if (dx, dy, dc) != (0, 0, 0)]

    def body(*refs):
        ins, outs = refs[:n], refs[n:2 * n]
        send_sems, recv_sems, local_sems = refs[2 * n:]
        x, y, c = _mesh_pos()
        me = _dev_index(x, y, c)
        mine = [pltpu.make_async_copy(ins[a].at[me], outs[a].at[me], local_sems.at[a]) for a in range(n)]
        for cp in mine:
            cp.start()
        copies = []
        for k, (dx, dy, dc) in enumerate(relations):
            peer = (x + dx - 2 * x * dx, y + dy - 2 * y * dy, c + dc - 2 * c * dc)
            for a in range(n):
                copies.append((pltpu.make_async_remote_copy(
                    src_ref=ins[a].at[_dev_index(*peer)], dst_ref=outs[a].at[me], send_sem=send_sems.at[a * 7 + k],
                    recv_sem=recv_sems.at[a * 7 + k], device_id=peer, device_id_type=MESH),
                    pltpu.make_async_remote_copy(
                    src_ref=ins[a].at[_dev_index(*peer)], dst_ref=outs[a].at[_dev_index(*peer)],
                    send_sem=send_sems.at[a * 7 + k], recv_sem=recv_sems.at[a * 7 + k], device_id=peer,
                    device_id_type=MESH)))
        for send, _ in copies:
            send.start()
        for _, recv in copies:
            recv.wait_recv()
        for send, _ in copies:
            send.wait_send()
        for cp in mine:
            cp.wait()

    return pl.pallas_call(
        body, name=name, in_specs=[ANY] * n, out_specs=[ANY] * n,
        out_shape=[jax.ShapeDtypeStruct(p.shape, p.dtype) for p in parts],
        scratch_shapes=[pltpu.SemaphoreType.DMA((7 * n,)), pltpu.SemaphoreType.DMA((7 * n,)),
                        pltpu.SemaphoreType.DMA((n,))],
    )(*parts)


HBM = pl.BlockSpec(memory_space=pltpu.HBM)
SEM = pl.BlockSpec(memory_space=pltpu.SEMAPHORE)
EFFECT = pltpu.SideEffectType.DATAFLOW_SIDE_EFFECTING
RELATIONS = [(dx, dy, dc) for dx in (0, 1) for dy in (0, 1) for dc in (0, 1) if (dx, dy, dc) != (0, 0, 0)]


def _peer(rel):
    x, y, c = _mesh_pos()
    dx, dy, dc = rel
    return (x + dx - 2 * x * dx, y + dy - 2 * y * dy, c + dc - 2 * c * dc)


CHIP_RELATIONS = [(1, 0, 0), (0, 1, 0), (1, 1, 0)]
EXCHANGE_PEERS = {"gather": RELATIONS, "scatter": RELATIONS, "own": [(0, 0, 1)] + CHIP_RELATIONS, "pass": CHIP_RELATIONS}


def _split_copy(src_ref, land_ref, send_sems, recv_sems, k, mode, incoming):
    x, y, c = _mesh_pos()
    me = _dev_index(x, y, c)
    peer = _peer(EXCHANGE_PEERS[mode][k])
    if mode == "pass":
        held, theirs = _dev_index(peer[0], peer[1], c), _dev_index(peer[0], peer[1], 1 - c)
        src, slot, target = land_ref.at[held], theirs if incoming else held, (x, y, 1 - c)
    else:
        src = src_ref.at[_dev_index(*peer)] if mode == "scatter" else src_ref
        slot, target = _dev_index(*peer) if incoming else me, peer
    return pltpu.make_async_remote_copy(src_ref=src, dst_ref=land_ref.at[slot], send_sem=send_sems.at[k],
                                        recv_sem=recv_sems.at[k], device_id=target, device_id_type=MESH)


def _exchange_start(srcs, lands, *, mode, after=None, name):
    n = len(srcs)
    n_after = 0 if after is None else 1
    n_rel = len(EXCHANGE_PEERS[mode])

    def body(*refs):
        src_refs, land_refs = refs[:n], refs[n:2 * n]
        first_out = 2 * n + n_after
        send, recv = refs[first_out:first_out + n], refs[first_out + n:first_out + 2 * n]
        token = refs[-1]
        for k in range(n_rel):
            for a in range(n):
                _split_copy(src_refs[a], land_refs[a], send[a], recv[a], k, mode, incoming=False).start()
        token[...] = jnp.zeros_like(token)

    outs = pl.pallas_call(
        body, name=name, in_specs=[HBM] * (2 * n) + [ANY] * n_after,
        out_shape=[pltpu.SemaphoreType.DMA((n_rel,))] * (2 * n)
        + [pltpu.HBM(s.shape, s.dtype) for s in srcs] + [pltpu.HBM(s.shape, s.dtype) for s in lands]
        + [jax.ShapeDtypeStruct((SUBLANES, LANES), F32)],
        out_specs=[SEM] * (2 * n) + [HBM] * (2 * n) + [pl.BlockSpec(memory_space=pltpu.VMEM)],
        input_output_aliases={**{a: 2 * n + a for a in range(n)}, **{n + a: 3 * n + a for a in range(n)}},
        compiler_params=pltpu.CompilerParams(has_side_effects=EFFECT),
    )(*[pltpu.with_memory_space_constraint(s, pltpu.HBM) for s in srcs],
      *[pltpu.with_memory_space_constraint(s, pltpu.HBM) for s in lands], *([after] if n_after else []))
    per_array = [(outs[a], outs[n + a], outs[2 * n + a], outs[3 * n + a]) for a in range(n)]
    return per_array, outs[-1]


def _exchange_wait(handle, after, *, mode, name):
    send_sems, recv_sems, src_thru, land_thru = handle
    after = after if isinstance(after, (tuple, list)) else (after,)

    def body(src_ref, land_ref, send, recv, *rest):
        for k in range(len(EXCHANGE_PEERS[mode])):
            cp = _split_copy(src_ref, land_ref, send, recv, k, mode, incoming=True)
            cp.wait_send()
            cp.wait_recv()

    return pl.pallas_call(
        body, name=name, in_specs=[HBM, HBM, SEM, SEM] + [ANY] * len(after),
        out_shape=[pltpu.HBM(src_thru.shape, src_thru.dtype), pltpu.HBM(land_thru.shape, land_thru.dtype)],
        out_specs=[HBM, HBM], input_output_aliases={0: 0, 1: 1},
        compiler_params=pltpu.CompilerParams(has_side_effects=EFFECT),
    )(src_thru, land_thru, send_sems, recv_sems, *after)[1]


def _landing_zone(own_block):
    me = _dev_index(*_mesh_pos())
    zone = lax.empty((N_DEV,) + own_block.shape, own_block.dtype)
    return lax.dynamic_update_index_in_dim(zone, own_block, me, 0)


def _row_tile(rows, want):
    t = min(want, rows) // SUBLANES * SUBLANES
    while rows % t:
        t -= SUBLANES
    return t


def _sum_slots(recv, *, tr, name):
    s_, r_, c_ = recv.shape
    tr = _row_tile(r_, tr)

    def body(g_ref, o_ref):
        acc = g_ref[0]
        for s in range(1, s_):
            acc = acc + g_ref[s]
        o_ref[...] = acc

    return pl.pallas_call(
        body, name=name, grid=(r_ // tr,),
        in_specs=[pl.BlockSpec((s_, tr, c_), lambda i: (0, i, 0))],
        out_specs=pl.BlockSpec((tr, c_), lambda i: (i, 0)),
        out_shape=jax.ShapeDtypeStruct((r_, c_), F32),
        compiler_params=_params(("parallel",), (2 * s_ + 3) * tr * c_ * 4),
    )(recv)


def _adamw(recv, w, m, v, *, tr, name):
    s_, r_, c_ = recv.shape
    tr = _row_tile(r_, tr)
    assert w.shape == (r_, c_), (name, w.shape, recv.shape)
    c1 = 1.0 - ADAM_B1 ** ADAM_STEP
    c2 = 1.0 - ADAM_B2 ** ADAM_STEP

    def body(g_ref, w_ref, m_ref, v_ref, go_ref, d_ref, mo_ref, vo_ref):
        g = g_ref[0].astype(F32)
        for s in range(1, s_):
            g = g + g_ref[s].astype(F32)
        mn = ADAM_B1 * m_ref[...] + (1.0 - ADAM_B1) * g
        vn = ADAM_B2 * v_ref[...] + (1.0 - ADAM_B2) * (g * g)
        go_ref[...] = g
        mo_ref[...] = mn
        vo_ref[...] = vn
        d_ref[...] = -ADAM_LR * ((mn / c1) / (jnp.sqrt(vn / c2) + ADAM_EPS) + ADAM_WD * w_ref[...])

    tile = pl.BlockSpec((tr, c_), lambda i: (i, 0))
    return pl.pallas_call(
        body, name=name, grid=(r_ // tr,),
        in_specs=[pl.BlockSpec((s_, tr, c_), lambda i: (0, i, 0)), tile, tile, tile],
        out_specs=[tile] * 4, out_shape=[jax.ShapeDtypeStruct((r_, c_), F32)] * 4,
        compiler_params=_params(("parallel",), (2 * s_ + 16) * tr * c_ * 4),
    )(recv, w, m, v)


def _adamw_whole(gs, ws, ms, vs, *, name):
    n = len(gs)
    c1 = 1.0 - ADAM_B1 ** ADAM_STEP
    c2 = 1.0 - ADAM_B2 ** ADAM_STEP

    def body(*refs):
        for i in range(n):
            g, w = refs[i][...], refs[n + i][...]
            mn = ADAM_B1 * refs[2 * n + i][...] + (1.0 - ADAM_B1) * g
            vn = ADAM_B2 * refs[3 * n + i][...] + (1.0 - ADAM_B2) * (g * g)
            refs[4 * n + 3 * i][...] = -ADAM_LR * ((mn / c1) / (jnp.sqrt(vn / c2) + ADAM_EPS) + ADAM_WD * w)
            refs[4 * n + 3 * i + 1][...] = mn
            refs[4 * n + 3 * i + 2][...] = vn

    whole = pl.BlockSpec(memory_space=pltpu.VMEM)
    lane_padded = sum(math.prod(g.shape[:-1]) * (-(-g.shape[-1] // LANES) * LANES) for g in gs)
    outs = pl.pallas_call(
        body, name=name, in_specs=[whole] * (4 * n), out_specs=[whole] * (3 * n),
        out_shape=[jax.ShapeDtypeStruct(g.shape, F32) for g in gs for _ in range(3)],
        compiler_params=pltpu.CompilerParams(vmem_limit_bytes=int(min(max(16 * lane_padded * 4, 16 * 2 ** 20), VMEM_CAP))),
    )(*gs, *ws, *ms, *vs)
    return [tuple(outs[3 * i:3 * i + 3]) for i in range(n)]


def _s5_discretise(a_re, a_im, log_dt, b_re, b_im):
    dt = jnp.exp(log_dt)[:, None]
    lr = jnp.minimum(a_re, -1e-4)
    li = a_im
    mag = jnp.exp(lr * dt)
    lbr = mag * jnp.cos(li * dt)
    lbi = mag * jnp.sin(li * dt)
    zr, zi = lbr - 1.0, lbi
    den = lr * lr + li * li
    fr = (zr * lr + zi * li) / den
    fi = (zi * lr - zr * li) / den
    bbr = fr[..., None] * b_re - fi[..., None] * b_im
    bbi = fr[..., None] * b_im + fi[..., None] * b_re
    return lbr, lbi, bbr, bbi


def _softplus_neg(lam):
    return jnp.maximum(-lam, 0.0) + jnp.log(1.0 + jnp.exp(-jnp.abs(lam)))


S5_Q = 8
RG_Q = 2


def _local_step(x, tgt, W, comm):
    T, D = x.shape
    C = D
    G, P, H = W["ssm_b_re"].shape
    S = G * H
    F = W["mlp_b_up"].shape[1]
    n_in = 2 * C + S + 2 * D
    heads, hd = W["rg_wa"].shape[0], W["rg_wa"].shape[1]
    u_off, ga_off, gb_off = 2 * C, 2 * C + S, 2 * C + S + D

    if comm.first_token is not None:
        anchored = ("rg_lambda", "ssm_a_re", "rg_wa", "rg_wx", "ssm_c_re", "ssm_c_im")
        W = {**W, **{k: W[k] + comm.first_token[0, 0] for k in anchored}}
    sp, sp_vjp = jax.vjp(_softplus_neg, W["rg_lambda"])
    (lbr, lbi, bbr, bbi), s5_vjp = jax.vjp(_s5_discretise, W["ssm_a_re"], W["ssm_a_im"], W["ssm_log_dt"],
                                           W["ssm_b_re"], W["ssm_b_im"])
    lam_re, lam_im = lbr.reshape(-1), lbi.reshape(-1)
    jr, kr = heads // RG_Q, RG_Q * hd
    w_ri = jnp.concatenate([_bd_pack(W["rg_wa"], RG_Q), _bd_pack(W["rg_wx"], RG_Q)], axis=2).astype(BF16)
    b_ri = jnp.concatenate([W["rg_ba"].reshape(jr, kr), W["rg_bx"].reshape(jr, kr)], axis=1).reshape(1, -1)
    wb_re = _bd_pack(jnp.swapaxes(bbr, 1, 2), S5_Q).astype(BF16)
    wb_im = _bd_pack(jnp.swapaxes(bbi, 1, 2), S5_Q).astype(BF16)
    wc_re = _bd_pack(jnp.swapaxes(W["ssm_c_re"], 1, 2), S5_Q).astype(BF16)
    wc_im_neg = _bd_pack(jnp.swapaxes(-W["ssm_c_im"], 1, 2), S5_Q).astype(BF16)
    d_row = W["ssm_d"].reshape(1, S)
    powers = _power_slabs(lam_re, lam_im, S5_TILE // SUBLANES)

    x_bf = x.astype(BF16) if comm.first_token is None else (x + comm.first_token[0, 0]).astype(BF16)
    w_in, conv_w = comm.first_weights((x_bf, w_ri, wb_re, wb_im, wc_re, wc_im_neg, powers[0], powers[1]))
    z = _mm(x_bf, w_in, M=T, N=n_in, K=D, tm=512, tn=n_in // 4, tk=D, after=comm.gather_token, name="fwd_in_proj")
    started = comm.start_weights(("mlp_w_up",), z)
    xc = _conv_fwd(z, conv_w, W["conv_b"], T=T, C=C, after=started, name="fwd_conv")
    ri = _bd([(xc, 0, w_ri)], T=T, J=jr, kb=kr, nb=2 * kr, extras=[(b_ri, "vec", 0)],
             epilogue=lambda acc, b: (_sig(acc + b),), name="fwd_gates")
    h, p, a_fwd, m_fwd = _rg_scan_fwd(z, ri, xc, sp, T=T, C=C, gate_off=C, cw=kr, name="fwd_rg_scan")
    w_a_out = comm.weight("w_a_out", p)
    started = comm.start_weights(("mlp_w_down",), p)
    y_a = _mm(p, w_a_out, M=T, N=D, K=C, out_dtypes=(BF16,), tm=512, tn=D, tk=C, after=started, name="fwd_rg_out")

    h_re, h_im, y_s, yg = _s5_fwd(z, u_off, wb_re, wb_im, wc_re, wc_im_neg, d_row, powers, T=T, name="fwd_s5")
    w_glu_w, w_glu_v = comm.weight("glu_w", yg), comm.weight("glu_v", yg)
    glu_a = _mm(yg, w_glu_w, M=T, N=D, K=S, out_dtypes=(BF16,), tm=1024, tn=D, tk=S, name="fwd_glu_w")
    cwm = 1024

    def mix_fn(b, ga, gb, ya, a):
        return b, _sig(ga) * ya.astype(F32) + _sig(gb) * (a.astype(F32) * _sig(b))

    glu_b, mix = _mm(yg, w_glu_v, M=T, N=D, K=S, tm=512, tn=cwm, tk=S,
                     extras=[(z, "mn", ga_off // cwm), (z, "mn", gb_off // cwm), (y_a, "mn"), (glu_a, "mn")],
                     epilogue=mix_fn, n_out=2, out_dtypes=(BF16, BF16), name="fwd_glu_v_mix")
    w_out = comm.weight("w_out", mix)
    def out_ln1_fn(acc, xv, g, b):
        s = ALPHA * xv + acc
        xhat, _ = _ln_stats(s)
        y = xhat * g + b
        return s, y, y

    s1, x1, x1_bf = _mm(mix, w_out, M=T, N=D, K=D, tm=256, tn=D, tk=D,
                        extras=[(x, "mn"), (W["ln1_g"], "n"), (W["ln1_b"], "n")], epilogue=out_ln1_fn, n_out=3,
                        out_dtypes=(F32, F32, BF16), name="fwd_out_proj_ln1")
    w_up = comm.weight("mlp_w_up", x1_bf)

    def mlp_up_fn(acc, b):
        hp = acc + b
        rl = jnp.maximum(hp, 0.0)
        return rl * rl, hp

    hact, hpre = _mm(x1_bf, w_up, M=T, N=F, K=D, tm=1024, tn=1024, tk=D, extras=[(W["mlp_b_up"], "n")],
                     epilogue=mlp_up_fn, n_out=2, out_dtypes=(BF16, BF16), name="fwd_mlp_up")
    w_down = comm.weight("mlp_w_down", hact)
    s2 = _mm(hact, w_down, M=T, N=D, K=F, tm=1024, tn=1024, tk=2048,
             extras=[(x1, "mn"), (W["mlp_b_down"], "n")], epilogue=lambda acc, xv, b: (ALPHA * xv + acc + b,),
             name="fwd_mlp_down")

    def ln2_fn(s, t, g, b):
        xhat, rstd = _ln_stats(s)
        err = xhat * g + b - t
        dy = err * (1.0 / D)
        ds = _ln_bwd(dy, g, xhat, rstd)
        return ds, ds, 0.5 * dy * err, dy * xhat, dy, ds

    ds2, ds2_bf, loss_cols, d_ln2_g, d_ln2_b, d_b_down = _ew(
        ln2_fn, [(s2, "tile", 0), (tgt, "tile", 0), (W["ln2_g"], "vec", 0), (W["ln2_b"], "vec", 0)],
        T=T, C=D, n_out=2, n_cs=4, out_dtypes=(F32, BF16), tm=256, name="bwd_loss_ln2")
    d_w_down = _mm(hact, ds2_bf, M=F, N=D, K=T, ta=True, out_dtypes=(BF16,), tm=1024, tn=1024, tk=4096, name="bwd_w_down")
    sent = comm.send_grad("mlp_w_down", d_w_down)

    def dhpre_fn(acc, hp):
        dv = acc * (2.0 * jnp.maximum(hp.astype(F32), 0.0))
        return dv, dv

    dhpre, d_b_up = _mm(ds2_bf, w_down, M=T, N=F, K=D, tb=True, tm=1024, tn=1024, tk=D, extras=[(hpre, "mn")],
                        epilogue=dhpre_fn, n_cs=1, out_dtypes=(BF16,), after=sent, name="bwd_mlp_down")
    d_w_up = _mm(x1_bf, dhpre, M=D, N=F, K=T, ta=True, out_dtypes=(BF16,), n_split=N_DEV, tm=1024, tn=F // N_DEV, tk=4096, name="bwd_w_up")
    sent = comm.send_grad("mlp_w_up", d_w_up)
    dx1 = _mm(dhpre, w_up, M=T, N=D, K=F, tb=True, tm=1024, tn=1024, tk=2048,
              extras=[(ds2, "mn")], epilogue=lambda acc, dv: (ALPHA * dv + acc,), after=sent, name="bwd_mlp_up")

    def ln1_bwd_fn(s, dy, g):
        xhat, rstd = _ln_stats(s)
        ds = _ln_bwd(dy, g, xhat, rstd)
        return ds, ds, dy * xhat, dy

    ds1, ds1_bf, d_ln1_g, d_ln1_b = _ew(ln1_bwd_fn, [(s1, "tile", 0), (dx1, "tile", 0), (W["ln1_g"], "vec", 0)],
                                        T=T, C=D, n_out=2, n_cs=2, out_dtypes=(F32, BF16), tm=256, name="bwd_ln1")
    d_w_out = _mm(mix, ds1_bf, M=D, N=D, K=T, ta=True, out_dtypes=(BF16,), tm=1024, tn=1024, tk=4096, name="bwd_w_out")
    sent = comm.send_grad("w_out", d_w_out)
    def mix_bwd_fn(dm, ga, gb, ya, a, b):
        ya, a, b = ya.astype(F32), a.astype(F32), b.astype(F32)
        sa, sb, sv = _sig(ga), _sig(gb), _sig(b)
        yb = a * sv
        dyb = dm * sb
        return (dm * ya * (sa * (1.0 - sa)), dm * yb * (sb * (1.0 - sb)), dm * sa, dyb * sv,
                dyb * a * (sv * (1.0 - sv)))

    dz = lax.empty((T, n_in), BF16)
    dz, dg_b, dy_a, dglu_a, dglu_b = _mm(
        ds1_bf, w_out, M=T, N=D, K=D, tb=True, tm=512, tn=cwm, tk=D,
        extras=[(z, "mn", ga_off // cwm), (z, "mn", gb_off // cwm), (y_a, "mn"), (glu_a, "mn"), (glu_b, "mn")],
        epilogue=mix_bwd_fn, n_out=5, out_dtypes=(BF16,) * 5, after=sent, into=(dz, 0, ga_off // cwm),
        name="bwd_out_proj_mix")
    dz = lax.dynamic_update_slice(dz, dg_b, (0, gb_off))

    d_w_a_out = _mm(p, dy_a, M=C, N=D, K=T, ta=True, out_dtypes=(BF16,), tm=1024, tn=1024, tk=4096, name="bwd_w_a_out")
    sent = comm.send_grad("w_a_out", d_w_a_out)
    def dp_fn(dp, hv, gate):
        th = jnp.tanh(GELU_C * (gate + GELU_K * gate * gate * gate))
        gelu = 0.5 * gate * (1.0 + th)
        dgelu = 0.5 * (1.0 + th) + 0.5 * gate * (1.0 - th * th) * (GELU_C * (1.0 + 3.0 * GELU_K * gate * gate))
        return dp * gelu, dp * hv * dgelu

    dh, dz = _mm(dy_a, w_a_out, M=T, N=C, K=D, tb=True, tm=256, tn=C, tk=D, extras=[(h, "mn"), (z, "mn", 1)],
                 epilogue=dp_fn, n_out=2, out_dtypes=(F32, BF16), after=sent, into=(dz, 1, 1), name="bwd_rg_out")
    drai, dxc0, d_b_ri, d_sp = _rg_scan_bwd(dh, h, ri, xc, a_fwd, m_fwd, sp, T=T, C=C, cw=kr, name="bwd_rg_scan")
    dxc = _bd([(drai, 0, w_ri)], T=T, J=jr, kb=2 * kr, nb=kr, tw=True, extras=[(dxc0, "tile", 0)],
              epilogue=lambda acc, d0: (acc + d0,), name="bwd_gates")
    d_w_ri = _bdw(xc, 0, drai, 0, T=T, J=jr, kb=kr, nb=2 * kr, name="bwd_w_gates")
    d_wa, d_wx = _bd_unpack(d_w_ri[:, :, :kr], RG_Q), _bd_unpack(d_w_ri[:, :, kr:], RG_Q)
    d_b_ri = d_b_ri.reshape(jr, 2 * kr)
    d_ba, d_bx = d_b_ri[:, :kr].reshape(1, -1), d_b_ri[:, kr:].reshape(1, -1)
    dz, conv_sums = _conv_bwd(dxc, z, conv_w, dz, T=T, C=C, name="bwd_conv")
    d_conv_w, d_conv_b = conv_sums[0:4], conv_sums[4:5]
    (d_lambda,) = sp_vjp(d_sp)

    d_glu_w = _mm(yg, dglu_a, M=S, N=D, K=T, ta=True, out_dtypes=(BF16,), n_split=N_DEV, tm=1024, tn=D // N_DEV, tk=4096, name="bwd_w_glu_w")
    d_glu_v = _mm(yg, dglu_b, M=S, N=D, K=T, ta=True, out_dtypes=(BF16,), n_split=N_DEV, tm=1024, tn=D // N_DEV, tk=4096, name="bwd_w_glu_v")
    sent = comm.send_grad("glu_w", d_glu_w, "glu_v", d_glu_v)
    dyg0 = _mm(dglu_a, w_glu_w, M=T, N=S, K=D, tb=True, tm=512, tn=S, tk=D, after=sent, name="bwd_glu_w")
    dy_s = _mm(dglu_b, w_glu_v, M=T, N=S, K=D, tb=True, tm=512, tn=S, tk=D,
               extras=[(dyg0, "mn"), (y_s, "mn")], epilogue=lambda acc, d0, yv: ((acc + d0) * _dgelu(yv),),
               name="bwd_glu_v")
    dz, d_lbr, d_lbi, d_ssm_d, d_wb_re, d_wb_im, d_wc_re, d_wc_im_neg = _s5_bwd(
        dy_s, z, u_off, h_re, h_im, wb_re, wb_im, wc_re, wc_im_neg, d_row, powers, dz, T=T, name="bwd_s5")
    d_bbr = jnp.swapaxes(_bd_unpack(d_wb_re, S5_Q), 1, 2)
    d_bbi = jnp.swapaxes(_bd_unpack(d_wb_im, S5_Q), 1, 2)
    d_a_re, d_a_im, d_log_dt, d_b_re, d_b_im = s5_vjp((d_lbr.reshape(G, P), d_lbi.reshape(G, P), d_bbr, d_bbi))
    d_c_re = _bd_unpack(d_wc_re, S5_Q)
    d_c_im = -_bd_unpack(d_wc_im_neg, S5_Q)

    d_w_in = _mm(x_bf, dz, M=D, N=n_in, K=T, ta=True, out_dtypes=(BF16,), n_split=N_DEV, tm=1024, tn=n_in // N_DEV, tk=4096, name="bwd_w_in")
    sent = comm.send_grad("w_in", d_w_in)
    grad_x = _mm(dz, w_in, M=T, N=D, K=n_in, tb=True, tm=1024, tn=1024, tk=n_in // 4,
                 extras=[(ds1, "mn")], epilogue=lambda acc, dv: (ALPHA * dv + acc,), after=sent, name="bwd_in_proj")

    grads = dict(
        conv_w=d_conv_w, conv_b=d_conv_b, rg_wa=d_wa, rg_ba=d_ba, rg_wx=d_wx, rg_bx=d_bx,
        rg_lambda=d_lambda, ssm_a_re=d_a_re, ssm_a_im=d_a_im, ssm_log_dt=d_log_dt,
        ssm_b_re=d_b_re, ssm_b_im=d_b_im, ssm_c_re=d_c_re, ssm_c_im=d_c_im, ssm_d=d_ssm_d.reshape(G, H),
        ln1_g=d_ln1_g, ln1_b=d_ln1_b, mlp_b_up=d_b_up, mlp_b_down=d_b_down, ln2_g=d_ln2_g, ln2_b=d_ln2_b)
    return jnp.sum(loss_cols), grad_x, grads


BIG = ("w_in", "w_a_out", "glu_w", "glu_v", "w_out", "mlp_w_up", "mlp_w_down")
COL_SHARDED = ("w_in", "glu_w", "glu_v", "mlp_w_up")
SMALL = ("conv_w", "conv_b", "rg_wa", "rg_ba", "rg_wx", "rg_bx", "rg_lambda", "ssm_a_re", "ssm_a_im", "ssm_log_dt",
         "ssm_b_re", "ssm_b_im", "ssm_c_re", "ssm_c_im", "ssm_d", "ln1_g", "ln1_b", "mlp_b_up", "mlp_b_down", "ln2_g",
         "ln2_b")
ORDER = ("w_in", "conv_w", "conv_b", "rg_wa", "rg_ba", "rg_wx", "rg_bx", "rg_lambda", "w_a_out", "ssm_a_re",
         "ssm_a_im", "ssm_log_dt", "ssm_b_re", "ssm_b_im", "ssm_c_re", "ssm_c_im", "ssm_d", "glu_w", "glu_v", "w_out",
         "ln1_g", "ln1_b", "mlp_w_up", "mlp_b_up", "mlp_w_down", "mlp_b_down", "ln2_g", "ln2_b")
TILE_ELEMS = SUBLANES * LANES


def _pack(arrs):
    pieces = []
    for a in arrs:
        flat = a.reshape(-1)
        flat = jnp.pad(flat, (0, (-flat.shape[0]) % TILE_ELEMS))
        pieces.append(flat.reshape(-1, LANES))
    rows = sum(p.shape[0] for p in pieces)
    pad_rows = (-rows) % (N_DEV * SUBLANES)
    if pad_rows:
        pieces.append(jnp.zeros((pad_rows, LANES), pieces[0].dtype))
    return jnp.concatenate(pieces, axis=0)


def _unpack(packed, shapes):
    out, row = [], 0
    for shp in shapes:
        n = math.prod(shp)
        rows = -(-n // TILE_ELEMS) * SUBLANES
        out.append(packed[row:row + rows].reshape(-1)[:n].reshape(shp))
        row += rows
    return out


class _Comm:
    def __init__(self, w):
        first = [w["w_in"].astype(BF16), w["conv_w"]]
        self._first, self.first_token = _exchange_start(first, [_landing_zone(s) for s in first], mode="own",
                                                        name="gather_in_start")
        self._shards = {k: w[k].astype(BF16) for k in BIG if k != "w_in"}
        self._weights, self._gathers, self._grads = {}, {}, {}

    def first_weights(self, after):
        lands = [_exchange_wait(h, after, mode="own", name="gather_in_wait_%d" % i) for i, h in enumerate(self._first)]
        unused = [lax.empty((2 * SUBLANES, LANES), BF16) for _ in lands]
        handles, passed = _exchange_start(unused, lands, mode="pass", name="gather_in_pass")
        w_in, taps = [_exchange_wait(h, passed, mode="pass", name="gather_in_got_%d" % i) for i, h in enumerate(handles)]
        self._weights["w_in"] = w_in
        self.gather_token = self.start_weights(("w_a_out", "glu_w", "glu_v", "w_out"), w_in)
        return self.weight("w_in", None), jnp.swapaxes(taps, 0, 1).reshape(taps.shape[1], -1)

    def start_weights(self, names, after):
        shards = [self._shards.pop(k) for k in names]
        handles, token = _exchange_start(shards, [_landing_zone(s) for s in shards], mode="gather", after=after,
                                         name="gather_start_" + names[0])
        self._gathers.update(zip(names, handles))
        return token

    def weight(self, k, after):
        if k not in self._weights:
            self._weights[k] = _exchange_wait(self._gathers.pop(k), after, mode="gather", name="gather_wait_" + k)
        gk = self._weights[k]
        if k in COL_SHARDED:
            return jnp.swapaxes(gk, 0, 1).reshape(gk.shape[1], -1)
        return gk.reshape(-1, gk.shape[-1])

    def send_grad(self, *names_and_parts):
        names, parts = names_and_parts[0::2], names_and_parts[1::2]
        parts = [p if k in COL_SHARDED else p.reshape(N_DEV, p.shape[0] // N_DEV, p.shape[1])
                 for k, p in zip(names, parts)]
        me = _dev_index(*_mesh_pos())
        lands = [_landing_zone(lax.dynamic_index_in_dim(p, me, 0, keepdims=False)) for p in parts]
        handles, token = _exchange_start(parts, lands, mode="scatter", name="grad_start_" + names[0])
        self._grads.update(zip(names, handles))
        return token

    def received_grad(self, k, after):
        return _exchange_wait(self._grads.pop(k), after, mode="scatter", name="grad_wait_" + k)


SMALL_GROUPS = (("rg_wa", "rg_wx"), ("ssm_b_re",), ("ssm_b_im",),
                tuple(k for k in SMALL if k not in ("rg_wa", "rg_wx", "ssm_b_re", "ssm_b_im")))


def _step(x, tgt, w, m, v, raw_w, raw_m, raw_v):
    dev = _dev_index(*_mesh_pos())

    comm = _Comm(w)
    small = dict(w)
    for k in ("conv_b", "rg_ba", "rg_bx", "rg_lambda", "ln1_g", "ln1_b", "mlp_b_up", "mlp_b_down", "ln2_g", "ln2_b"):
        small[k] = w[k].reshape(1, -1)

    loss_part, grad_x, grads = _local_step(x, tgt, small, comm)

    out_g, out_d, out_m, out_v = {}, {}, {}, {}
    for k in BIG:
        rk = comm.received_grad(k, grad_x)
        out_g[k], out_d[k], out_m[k], out_v[k] = _adamw(rk, w[k], m[k], v[k], tr=128, name="adamw_" + k)

    small_shapes = [grads[k].shape for k in SMALL]
    (small_recv,) = _exchange_blocks([_pack([grads[k] for k in SMALL]).reshape(N_DEV, -1, LANES)],
                                     name="exchange_small_grads")
    small_block = _sum_slots(small_recv, tr=512, name="sum_small_grads")
    (small_all,) = _all_gather([small_block], name="gather_small_grads")
    g_small = dict(zip(SMALL, _unpack(small_all.reshape(-1, LANES), small_shapes)))
    cw_cols = w["conv_w"].shape[1]
    g_small["conv_w"] = lax.dynamic_slice_in_dim(g_small["conv_w"], dev * cw_cols, cw_cols, axis=1)
    for group in SMALL_GROUPS:
        gs = [g_small[k].reshape(raw_w[k].shape) for k in group]
        res = _adamw_whole(gs, [raw_w[k] for k in group], [raw_m[k] for k in group], [raw_v[k] for k in group],
                           name="adamw_" + group[0])
        for k, gk, (dk, mk, vk) in zip(group, gs, res):
            out_g[k], out_d[k], out_m[k], out_v[k] = gk, dk, mk, vk

    loss = lax.psum(loss_part, ("x", "y", "c"))
    return loss, grad_x, out_g, out_d, out_m, out_v


def kernel(x, w_in, conv_w, conv_b, rg_wa, rg_ba, rg_wx, rg_bx, rg_lambda, w_a_out, ssm_a_re, ssm_a_im, ssm_log_dt, ssm_b_re, ssm_b_im, ssm_c_re, ssm_c_im, ssm_d, glu_w, glu_v, w_out, ln1_g, ln1_b, mlp_w_up, mlp_b_up, mlp_w_down, mlp_b_down, ln2_g, ln2_b, loss_target, m_w_in, m_conv_w, m_conv_b, m_rg_wa, m_rg_ba, m_rg_wx, m_rg_bx, m_rg_lambda, m_w_a_out, m_ssm_a_re, m_ssm_a_im, m_ssm_log_dt, m_ssm_b_re, m_ssm_b_im, m_ssm_c_re, m_ssm_c_im, m_ssm_d, m_glu_w, m_glu_v, m_w_out, m_ln1_g, m_ln1_b, m_mlp_w_up, m_mlp_b_up, m_mlp_w_down, m_mlp_b_down, m_ln2_g, m_ln2_b, v_w_in, v_conv_w, v_conv_b, v_rg_wa, v_rg_ba, v_rg_wx, v_rg_bx, v_rg_lambda, v_w_a_out, v_ssm_a_re, v_ssm_a_im, v_ssm_log_dt, v_ssm_b_re, v_ssm_b_im, v_ssm_c_re, v_ssm_c_im, v_ssm_d, v_glu_w, v_glu_v, v_w_out, v_ln1_g, v_ln1_b, v_mlp_w_up, v_mlp_b_up, v_mlp_w_down, v_mlp_b_down, v_ln2_g, v_ln2_b):
    args = locals()
    w = {k: args[k][0] for k in ORDER}
    m = {k: args["m_" + k][0] for k in BIG}
    v = {k: args["v_" + k][0] for k in BIG}
    raw = [{k: args[prefix + k] for k in SMALL} for prefix in ("", "m_", "v_")]
    loss, grad_x, out_g, out_d, out_m, out_v = _step(x[0], loss_target[0], w, m, v, *raw)
    outs = [loss, grad_x[None]]
    for group in (out_g, out_d, out_m, out_v):
        outs += [group[k].reshape(args[k].shape) for k in ORDER]
    return tuple(outs)
```

```python
import functools
import math

import jax
import jax.numpy as jnp
from jax import lax
from jax.experimental import pallas as pl
from jax.experimental.pallas import tpu as pltpu

F32 = jnp.float32
BF16 = jnp.bfloat16
MESH = pl.DeviceIdType.MESH
N_DEV = 8
SUBLANES = 8
LANES = 128
VMEM_BYTES_V7X = 64 * 2 ** 20
VMEM_CAP = VMEM_BYTES_V7X - 8 * 2 ** 20

ALPHA = 2.0 ** 0.25
LN_EPS = 1e-5
RG_C = 8.0
ADAM_LR, ADAM_B1, ADAM_B2, ADAM_EPS, ADAM_WD, ADAM_STEP = 0.001, 0.9, 0.999, 1e-08, 0.01, 10
GELU_C = math.sqrt(2.0 / math.pi)
GELU_K = 0.044715

ANY = pl.BlockSpec(memory_space=pl.ANY)


def _params(sem, vmem_bytes):
    limit = int(min(max(2 * vmem_bytes, 16 * 2 ** 20), VMEM_CAP))
    return pltpu.CompilerParams(dimension_semantics=sem, vmem_limit_bytes=limit)


def _sig(x):
    return 1.0 / (1.0 + jnp.exp(-x))


def _gelu(x):
    return 0.5 * x * (1.0 + jnp.tanh(GELU_C * (x + GELU_K * x * x * x)))


def _dgelu(x):
    th = jnp.tanh(GELU_C * (x + GELU_K * x * x * x))
    return 0.5 * (1.0 + th) + 0.5 * x * (1.0 - th * th) * (GELU_C * (1.0 + 3.0 * GELU_K * x * x))


def _one_minus_exp(x, exp_half_x):
    p = x * (1.0 + x * (1 / 2 + x * (1 / 6 + x * (1 / 24 + x * (1 / 120)))))
    return jnp.where(x > -1 / 16, -p, 1.0 - exp_half_x * exp_half_x)


def _accumulate(ref, val, first):
    @pl.when(first)
    def _():
        ref[...] = val

    @pl.when(jnp.logical_not(first))
    def _():
        ref[...] += val


def _rows8(cw):
    return lax.broadcasted_iota(jnp.int32, (SUBLANES, cw), 0)


def _shift_down(cur, prev, s, rows):
    return jnp.where(rows < s, pltpu.roll(prev, s, 0), pltpu.roll(cur, s, 0))


def _shift_up(cur, nxt, s, rows):
    return jnp.where(rows < SUBLANES - s, pltpu.roll(cur, SUBLANES - s, 0), pltpu.roll(nxt, SUBLANES - s, 0))


def _mm(a, b, *, M, N, K, ta=False, tb=False, b_split=1, n_split=1, a_fn=None, extras=(), epilogue=None,
        n_out=1, n_cs=0, out_dtypes=None, tm=512, tn=512, tk=512, after=None, into=None, name):
    tm, tn, tk = min(tm, M), min(tn, N), min(tk, K)
    assert M % tm == 0 and N % tn == 0 and K % tk == 0, (name, M, N, K, tm, tn, tk)
    nk = K // tk
    grid = (N // tn, M // tm, nk)
    a_spec = pl.BlockSpec((tk, tm), lambda j, i, k: (k, i)) if ta else pl.BlockSpec((tm, tk), lambda j, i, k: (i, k))
    if b_split == 1:
        b_spec = pl.BlockSpec((tn, tk), lambda j, i, k: (j, k)) if tb else pl.BlockSpec((tk, tn), lambda j, i, k: (k, j))
    elif tb:
        kb = (K // b_split) // tk
        assert kb * tk * b_split == K, name
        b_spec = pl.BlockSpec((None, tn, tk), lambda j, i, k: (k // kb, j, k % kb))
    else:
        nb = (N // b_split) // tn
        assert nb * tn * b_split == N, name
        b_spec = pl.BlockSpec((None, tk, tn), lambda j, i, k: (j // nb, k, j % nb))
    in_specs = [a_spec, b_spec]
    for arr, kind, *col_off in extras:
        off = col_off[0] if col_off else 0
        in_specs.append(pl.BlockSpec((tm, tn), lambda j, i, k, off=off: (i, off + j)) if kind == "mn"
                        else pl.BlockSpec((1, tn), lambda j, i, k: (0, j)))
    out_dtypes = (F32,) * n_out if out_dtypes is None else out_dtypes
    if n_split == 1:
        out_shape = [jax.ShapeDtypeStruct((M, N), dt) for dt in out_dtypes]
        out_specs = [pl.BlockSpec((tm, tn), lambda j, i, k: (i, j)) for _ in range(n_out)]
    else:
        assert n_out == 1
        nbo = (N // n_split) // tn
        assert nbo * tn * n_split == N, name
        out_shape = [jax.ShapeDtypeStruct((n_split, M, N // n_split), out_dtypes[0])]
        out_specs = [pl.BlockSpec((None, tm, tn), lambda j, i, k: (j // nbo, i, j % nbo))]
    out_shape += [jax.ShapeDtypeStruct((1, N), F32) for _ in range(n_cs)]
    out_specs += [pl.BlockSpec((1, tn), lambda j, i, k: (0, j)) for _ in range(n_cs)]
    ne = len(extras)
    dims = (((0 if ta else 1,), (1 if tb else 0,)), ((), ()))

    n_after = 0 if after is None else 1
    in_specs += [ANY] * n_after
    aliases, tail = {}, [] if after is None else [after]
    if into is not None:
        buf, which, col_blk = into
        assert n_split == 1 and buf.shape[0] == M and buf.dtype == out_dtypes[which], name
        aliases = {len(in_specs): which}
        in_specs.append(ANY)
        tail.append(buf)
        out_shape[which] = jax.ShapeDtypeStruct(buf.shape, buf.dtype)
        out_specs[which] = pl.BlockSpec((tm, tn), lambda j, i, k: (i, col_blk + j))

    def body(*refs):
        a_ref, b_ref = refs[0], refs[1]
        ex_refs = refs[2:2 + ne]
        first_out = 2 + ne + len(tail)
        out_refs = refs[first_out:first_out + n_out]
        cs_refs = refs[first_out + n_out:first_out + n_out + n_cs]
        i, k = pl.program_id(1), pl.program_id(2)

        def product():
            av = a_ref[...]
            if a_fn is not None:
                av = a_fn(av.astype(F32))
            return lax.dot_general(av.astype(BF16), b_ref[...].astype(BF16), dims, preferred_element_type=F32)

        def finish(acc):
            res = (acc,) if epilogue is None else epilogue(acc, *[r[...] for r in ex_refs])
            for r, o in zip(out_refs, res[:n_out]):
                r[...] = o.astype(r.dtype)
            for r, cval in zip(cs_refs, res[n_out:]):
                _accumulate(r, jnp.sum(cval, axis=0, keepdims=True), i == 0)

        if nk == 1:
            finish(product())
            return
        acc_ref = refs[-1]

        @pl.when(k == 0)
        def _():
            acc_ref[...] = jnp.zeros_like(acc_ref)

        acc_ref[...] += product()

        @pl.when(k == nk - 1)
        def _():
            finish(acc_ref[...])

    vmem = 2 * tm * tk * a.dtype.itemsize + 2 * tk * tn * b.dtype.itemsize + (1 + 2 * n_out + 2 * ne + 2) * tm * tn * 4
    outs = pl.pallas_call(
        body, name=name, grid=grid, in_specs=in_specs, out_specs=out_specs, out_shape=out_shape,
        scratch_shapes=[pltpu.VMEM((tm, tn), F32)] if nk > 1 else [], input_output_aliases=aliases,
        compiler_params=_params(("parallel", "arbitrary", "arbitrary"), vmem),
    )(a, b, *[e[0] for e in extras], *tail)
    return outs[0] if len(outs) == 1 else outs


BD_STEP = 4

def _bd(pairs, *, T, J, kb, nb, tw=False, extras=(), epilogue=None, n_out=1, n_cs=0, out_dtypes=None, tm=512, name):
    jb = BD_STEP
    assert T % tm == 0 and J % jb == 0
    grid = (J // jb, T // tm)
    npair, ne = len(pairs), len(extras)
    in_specs, args = [], []
    for arr, off, w in pairs:
        assert off % jb == 0, name
        in_specs.append(pl.BlockSpec((tm, jb * kb), lambda j, i, off=off // jb: (i, off + j)))
        in_specs.append(pl.BlockSpec((jb,) + tuple(w.shape[1:]), lambda j, i: (j, 0, 0)))
        args += [arr, w]
    for arr, kind, off in extras:
        assert off % jb == 0, name
        in_specs.append(pl.BlockSpec((tm, jb * nb), lambda j, i, off=off // jb: (i, off + j)) if kind == "tile"
                        else pl.BlockSpec((1, jb * nb), lambda j, i, off=off // jb: (0, off + j)))
        args.append(arr)
    out_dtypes = (F32,) * n_out if out_dtypes is None else out_dtypes
    out_shape = [jax.ShapeDtypeStruct((T, J * nb), dt) for dt in out_dtypes]
    out_specs = [pl.BlockSpec((tm, jb * nb), lambda j, i: (i, j)) for _ in range(n_out)]
    out_shape += [jax.ShapeDtypeStruct((1, J * nb), F32) for _ in range(n_cs)]
    out_specs += [pl.BlockSpec((1, jb * nb), lambda j, i: (0, j)) for _ in range(n_cs)]
    dims = (((1,), (1 if tw else 0,)), ((), ()))

    def body(*refs):
        ex_refs = refs[2 * npair:2 * npair + ne]
        out_refs = refs[2 * npair + ne:2 * npair + ne + n_out]
        cs_refs = refs[2 * npair + ne + n_out:]
        i = pl.program_id(1)
        for s in range(jb):
            cols_in, cols_out = pl.ds(s * kb, kb), pl.ds(s * nb, nb)
            acc = None
            for p in range(npair):
                d = lax.dot_general(refs[2 * p][:, cols_in].astype(BF16), refs[2 * p + 1][s].astype(BF16), dims,
                                    preferred_element_type=F32)
                acc = d if acc is None else acc + d
            res = (acc,) if epilogue is None else epilogue(acc, *[r[:, cols_out] for r in ex_refs])
            for r, o in zip(out_refs, res[:n_out]):
                r[:, cols_out] = o.astype(r.dtype)
            for r, cval in zip(cs_refs, res[n_out:]):
                _accumulate(r.at[:, cols_out], jnp.sum(cval, axis=0, keepdims=True), i == 0)

    vmem = jb * (2 * npair * tm * kb + 2 * npair * kb * nb + (2 * n_out + 2 * ne + 3) * tm * nb) * 4
    outs = pl.pallas_call(
        body, name=name, grid=grid, in_specs=in_specs, out_specs=out_specs, out_shape=out_shape,
        compiler_params=_params(("parallel", "arbitrary"), vmem),
    )(*args)
    return outs[0] if len(outs) == 1 else outs


def _bdw(a, a_off, b, b_off, *, T, J, kb, nb, tm=512, name):
    jb = BD_STEP
    assert T % tm == 0 and J % jb == 0 and a_off % jb == 0 and b_off % jb == 0
    a_blk, b_blk = a_off // jb, b_off // jb

    def body(a_ref, b_ref, o_ref):
        i = pl.program_id(1)
        for s in range(jb):
            d = lax.dot_general(a_ref[:, pl.ds(s * kb, kb)].astype(BF16), b_ref[:, pl.ds(s * nb, nb)].astype(BF16),
                                (((0,), (0,)), ((), ())), preferred_element_type=F32)
            _accumulate(o_ref.at[s], d, i == 0)

    return pl.pallas_call(
        body, name=name, grid=(J // jb, T // tm),
        in_specs=[pl.BlockSpec((tm, jb * kb), lambda j, i: (i, a_blk + j)),
                  pl.BlockSpec((tm, jb * nb), lambda j, i: (i, b_blk + j))],
        out_specs=pl.BlockSpec((jb, kb, nb), lambda j, i: (j, 0, 0)),
        out_shape=jax.ShapeDtypeStruct((J, kb, nb), F32),
        compiler_params=_params(("parallel", "arbitrary"), jb * (2 * tm * (kb + nb) + 3 * kb * nb) * 4),
    )(a, b)


def _bd_pack(w, q):
    g, a, b = w.shape
    eye = jnp.eye(q, dtype=w.dtype)
    return jnp.einsum("jqab,qr->jqarb", w.reshape(g // q, q, a, b), eye).reshape(g // q, q * a, q * b)


def _bd_unpack(wp, q):
    j, qa, qb = wp.shape
    a, b = qa // q, qb // q
    w5 = wp.reshape(j, q, a, q, b)
    return jnp.stack([w5[:, r, :, r, :] for r in range(q)], axis=1).reshape(j * q, a, b)


def _ew(fn, ins, *, T, C, n_out, n_cs=0, out_dtypes=None, tm=256, cw=None, name):
    cw = C if cw is None else cw
    assert T % tm == 0 and C % cw == 0
    grid = (C // cw, T // tm)
    in_specs = []
    for arr, kind, off in ins:
        in_specs.append(pl.BlockSpec((tm, cw), lambda j, i, off=off: (i, off + j)) if kind == "tile"
                        else pl.BlockSpec((arr.shape[0], cw), lambda j, i, off=off: (0, off + j)))
    out_dtypes = (F32,) * n_out if out_dtypes is None else out_dtypes
    out_shape = [jax.ShapeDtypeStruct((T, C), dt) for dt in out_dtypes]
    out_specs = [pl.BlockSpec((tm, cw), lambda j, i: (i, j)) for _ in range(n_out)]
    out_shape += [jax.ShapeDtypeStruct((1, C), F32) for _ in range(n_cs)]
    out_specs += [pl.BlockSpec((1, cw), lambda j, i: (0, j)) for _ in range(n_cs)]
    nin = len(ins)

    def body(*refs):
        i = pl.program_id(1)
        res = fn(*[r[...].astype(F32) for r in refs[:nin]])
        for r, o in zip(refs[nin:nin + n_out], res[:n_out]):
            r[...] = o.astype(r.dtype)
        for r, cval in zip(refs[nin + n_out:], res[n_out:]):
            _accumulate(r, jnp.sum(cval, axis=0, keepdims=True), i == 0)

    vmem = (2 * nin + 2 * n_out + 6) * tm * cw * 4
    outs = pl.pallas_call(
        body, name=name, grid=grid, in_specs=in_specs, out_specs=out_specs, out_shape=out_shape,
        compiler_params=_params(("parallel", "arbitrary"), vmem),
    )(*[arr for arr, _, _ in ins])
    return outs[0] if len(outs) == 1 else outs


def _ln_stats(s):
    mu = jnp.mean(s, axis=-1, keepdims=True)
    d = s - mu
    var = jnp.mean(d * d, axis=-1, keepdims=True)
    rstd = lax.rsqrt(var + LN_EPS)
    return d * rstd, rstd


def _ln_bwd(dy, g, xhat, rstd):
    dxh = dy * g
    m1 = jnp.mean(dxh, axis=-1, keepdims=True)
    m2 = jnp.mean(dxh * xhat, axis=-1, keepdims=True)
    return rstd * (dxh - m1 - xhat * m2)


def _conv_fwd(z, conv_w, conv_b, *, T, C, tm=512, cw=1024, after=None, name):
    ng, hb = tm // SUBLANES, tm // SUBLANES
    n_after = 0 if after is None else 1

    def body(x_ref, halo_ref, w_ref, b_ref, *rest):
        o_ref = rest[-1]
        it = pl.program_id(1)
        rows = _rows8(cw)
        halo = jnp.where(it == 0, 0.0, halo_ref[...])
        w = w_ref[...]
        bias = b_ref[...]

        def group(g, carry):
            off = pl.multiple_of(g * SUBLANES, SUBLANES)
            cur = x_ref[pl.ds(off, SUBLANES), :]
            prev = x_ref[pl.ds(pl.multiple_of(jnp.maximum(off - SUBLANES, 0), SUBLANES), SUBLANES), :]
            prev = jnp.where(g == 0, halo, prev)
            acc = cur * w[3:4] + bias
            for s in (1, 2, 3):
                acc = acc + _shift_down(cur, prev, s, rows) * w[3 - s:4 - s]
            o_ref[pl.ds(off, SUBLANES), :] = acc
            return carry

        lax.fori_loop(0, ng, group, 0, unroll=2)

    return pl.pallas_call(
        body, name=name, grid=(C // cw, T // tm),
        in_specs=[pl.BlockSpec((tm, cw), lambda j, i: (i, j)),
                  pl.BlockSpec((SUBLANES, cw), lambda j, i: (jnp.maximum(i * hb - 1, 0), j)),
                  pl.BlockSpec((4, cw), lambda j, i: (0, j)), pl.BlockSpec((1, cw), lambda j, i: (0, j))]
        + [ANY] * n_after,
        out_specs=pl.BlockSpec((tm, cw), lambda j, i: (i, j)),
        out_shape=jax.ShapeDtypeStruct((T, C), F32),
        compiler_params=_params(("parallel", "arbitrary"), 5 * tm * cw * 4),
    )(z, z, conv_w, conv_b, *([after] if n_after else []))


def _conv_bwd(dxc, z, conv_w, dz, *, T, C, tm=512, cw=512, name):
    ng, hb, last = tm // SUBLANES, tm // SUBLANES, T // SUBLANES - 1
    nt = T // tm
    rows16 = 2 * SUBLANES

    def body(d_ref, dn_ref, x_ref, w_ref, dz_in_ref, o_ref, sums_ref):
        it = pl.program_id(1)
        rows = _rows8(cw)
        dnext = jnp.where(it == nt - 1, 0.0, dn_ref[...])
        w = w_ref[...]

        def pair(q, accs):
            base = pl.multiple_of(q * rows16, rows16)
            halves = []
            for half in range(2):
                g = 2 * q + half
                off = pl.multiple_of(base + half * SUBLANES, SUBLANES)
                dcur = d_ref[pl.ds(off, SUBLANES), :]
                dnx = d_ref[pl.ds(pl.multiple_of(jnp.minimum(off + SUBLANES, tm - SUBLANES), SUBLANES), SUBLANES), :]
                dnx = jnp.where(g == ng - 1, dnext, dnx)
                xcur = x_ref[pl.ds(off, SUBLANES), :]
                acc = dcur * w[3:4]
                taps = [accs[3] + dcur * xcur]
                for s in (1, 2, 3):
                    ahead = _shift_up(dcur, dnx, s, rows)
                    acc = acc + ahead * w[3 - s:4 - s]
                    taps.append(accs[3 - s] + ahead * xcur)
                halves.append(acc)
                accs = (taps[3], taps[2], taps[1], taps[0], accs[4] + dcur)
            o_ref[pl.ds(base, rows16), :] = jnp.concatenate(halves, axis=0).astype(o_ref.dtype)
            return accs

        zero = jnp.zeros((SUBLANES, cw), F32)
        accs = lax.fori_loop(0, ng // 2, pair, (zero,) * 5)
        sums = jnp.zeros((SUBLANES, cw), F32)
        for k, a in enumerate(accs):
            sums = jnp.where(rows == k, jnp.sum(a, axis=0, keepdims=True), sums)
        _accumulate(sums_ref, sums, it == 0)

    tile = pl.BlockSpec((tm, cw), lambda j, i: (i, j))
    return pl.pallas_call(
        body, name=name, grid=(C // cw, nt),
        in_specs=[tile, pl.BlockSpec((SUBLANES, cw), lambda j, i: (jnp.minimum((i + 1) * hb, last), j)),
                  tile, pl.BlockSpec((4, cw), lambda j, i: (0, j)), ANY],
        out_specs=[tile, pl.BlockSpec((SUBLANES, cw), lambda j, i: (0, j))],
        input_output_aliases={4: 0},
        out_shape=[jax.ShapeDtypeStruct(dz.shape, dz.dtype), jax.ShapeDtypeStruct((SUBLANES, C), F32)],
        compiler_params=_params(("parallel", "arbitrary"), 7 * tm * cw * 4),
    )(dxc, dxc, z, conv_w, dz)


def _rg_coeffs(r, ig, xc, sp):
    la = (-RG_C) * r * sp
    a = jnp.exp(la)
    m = jnp.sqrt(_one_minus_exp(2.0 * la, a))
    return a, m, m * (ig * xc)


def _rg_scan_fwd(z, ri, xc, sp, *, T, C, gate_off, tm=512, cw=256, name):
    rows16 = 2 * SUBLANES
    nq = tm // rows16

    def body(gate_ref, r_ref, i_ref, xc_ref, sp_ref, h_ref, p_ref, a_ref, m_ref, carry_ref):
        it = pl.program_id(1)

        @pl.when(it == 0)
        def _():
            carry_ref[...] = jnp.zeros_like(carry_ref)

        rows = _rows8(cw)
        sp_row = sp_ref[...]

        def pair(q, carry):
            base = pl.multiple_of(q * rows16, rows16)
            halves = []
            for half in range(2):
                sl = pl.ds(pl.multiple_of(base + half * SUBLANES, SUBLANES), SUBLANES)
                a, m, b = _rg_coeffs(r_ref[sl, :], i_ref[sl, :], xc_ref[sl, :], sp_row)
                a_ref[sl, :] = a
                m_ref[sl, :] = m
                for s in (1, 2, 4):
                    keep = rows >= s
                    sa = jnp.where(keep, pltpu.roll(a, s, 0), 1.0)
                    sb = jnp.where(keep, pltpu.roll(b, s, 0), 0.0)
                    b = b + a * sb
                    a = a * sa
                h = b + a * carry
                h_ref[sl, :] = h
                halves.append(h * _gelu(gate_ref[sl, :]))
                carry = h[SUBLANES - 1:SUBLANES, :]
            p_ref[pl.ds(base, rows16), :] = jnp.concatenate(halves, axis=0).astype(p_ref.dtype)
            return carry

        last = lax.fori_loop(0, nq, pair, carry_ref[0:1, :], unroll=2)
        carry_ref[...] = jnp.broadcast_to(last, carry_ref.shape)

    tile = pl.BlockSpec((tm, cw), lambda j, i: (i, j))
    gate_blk = gate_off // cw
    return pl.pallas_call(
        body, name=name, grid=(C // cw, T // tm),
        in_specs=[pl.BlockSpec((tm, cw), lambda j, i: (i, gate_blk + j)),
                  pl.BlockSpec((tm, cw), lambda j, i: (i, 2 * j)), pl.BlockSpec((tm, cw), lambda j, i: (i, 2 * j + 1)),
                  tile, pl.BlockSpec((1, cw), lambda j, i: (0, j))],
        out_specs=[tile, tile, tile, tile],
        out_shape=[jax.ShapeDtypeStruct((T, C), F32), jax.ShapeDtypeStruct((T, C), BF16),
                   jax.ShapeDtypeStruct((T, C), F32), jax.ShapeDtypeStruct((T, C), F32)],
        scratch_shapes=[pltpu.VMEM((SUBLANES, cw), F32)],
        compiler_params=_params(("parallel", "arbitrary"), 16 * tm * cw * 4),
    )(z, ri, ri, xc, sp)


def _rg_scan_bwd(dh, h, ri, xc, a_fwd, m_fwd, sp, *, T, C, tm=512, cw=256, name):
    ng, hb, nt = tm // SUBLANES, tm // SUBLANES, T // tm

    def body(dh_ref, h_ref, hp_ref, r_ref, i_ref, xc_ref, a_ref, m_ref, sp_ref,
             drai_ref, dxc_ref, crai_ref, csp_ref, cg_ref, ca_ref):
        step = pl.program_id(1)

        @pl.when(step == 0)
        def _():
            cg_ref[...] = jnp.zeros_like(cg_ref)
            ca_ref[...] = jnp.zeros_like(ca_ref)

        rows = _rows8(cw)
        sp_row = sp_ref[...]
        hhalo = jnp.where(step == nt - 1, 0.0, hp_ref[...])

        def group(gi, carry):
            g_next, a_next, s_ra, s_ia, s_sp = carry
            g = ng - 1 - gi
            off = pl.multiple_of(g * SUBLANES, SUBLANES)
            sl = pl.ds(off, SUBLANES)
            rr, ii, xx = r_ref[sl, :], i_ref[sl, :], xc_ref[sl, :]
            a, m = a_ref[sl, :], m_ref[sl, :]
            hh = h_ref[sl, :]
            hpv = h_ref[pl.ds(pl.multiple_of(jnp.maximum(off - SUBLANES, 0), SUBLANES), SUBLANES), :]
            hpv = jnp.where(g == 0, hhalo, hpv)
            hprev = _shift_down(hh, hpv, 1, rows)
            d = dh_ref[sl, :]
            c = jnp.where(rows < SUBLANES - 1, pltpu.roll(a, SUBLANES - 1, 0), a_next)
            for s in (1, 2, 4):
                keep = rows < SUBLANES - s
                sc = jnp.where(keep, pltpu.roll(c, SUBLANES - s, 0), 1.0)
                sd = jnp.where(keep, pltpu.roll(d, SUBLANES - s, 0), 0.0)
                d = d + c * sd
                c = c * sc
            gg = d + c * g_next
            da = gg * hprev
            dm = gg * (ii * xx)
            di = gg * (m * xx)
            dxc_ref[sl, :] = gg * (m * ii)
            dla = da * a - dm * (a * a / m)
            dra = dla * ((-RG_C) * sp_row) * (rr * (1.0 - rr))
            dia = di * (ii * (1.0 - ii))
            drai_ref[sl, pl.ds(0, cw)] = dra
            drai_ref[sl, pl.ds(cw, cw)] = dia
            return (gg[0:1, :], a[0:1, :], s_ra + dra, s_ia + dia, s_sp + dla * ((-RG_C) * rr))

        zero = jnp.zeros((SUBLANES, cw), F32)
        g_first, a_first, s_ra, s_ia, s_sp = lax.fori_loop(
            0, ng, group, (cg_ref[0:1, :], ca_ref[0:1, :], zero, zero, zero), unroll=2)
        cg_ref[...] = jnp.broadcast_to(g_first, cg_ref.shape)
        ca_ref[...] = jnp.broadcast_to(a_first, ca_ref.shape)
        for ref, acc in ((crai_ref.at[:, pl.ds(0, cw)], s_ra), (crai_ref.at[:, pl.ds(cw, cw)], s_ia), (csp_ref, s_sp)):
            _accumulate(ref, jnp.sum(acc, axis=0, keepdims=True), step == 0)

    tile = pl.BlockSpec((tm, cw), lambda j, i: (nt - 1 - i, j))
    wide = pl.BlockSpec((tm, 2 * cw), lambda j, i: (nt - 1 - i, j))
    vec = pl.BlockSpec((1, cw), lambda j, i: (0, j))
    return pl.pallas_call(
        body, name=name, grid=(C // cw, nt),
        in_specs=[tile, tile, pl.BlockSpec((SUBLANES, cw), lambda j, i: (jnp.maximum((nt - 1 - i) * hb - 1, 0), j)),
                  pl.BlockSpec((tm, cw), lambda j, i: (nt - 1 - i, 2 * j)),
                  pl.BlockSpec((tm, cw), lambda j, i: (nt - 1 - i, 2 * j + 1)), tile, tile, tile, vec],
        out_specs=[wide, tile, pl.BlockSpec((1, 2 * cw), lambda j, i: (0, j)), vec],
        out_shape=[jax.ShapeDtypeStruct((T, 2 * C), F32), jax.ShapeDtypeStruct((T, C), F32),
                   jax.ShapeDtypeStruct((1, 2 * C), F32), jax.ShapeDtypeStruct((1, C), F32)],
        scratch_shapes=[pltpu.VMEM((SUBLANES, cw), F32), pltpu.VMEM((SUBLANES, cw), F32)],
        compiler_params=_params(("parallel", "arbitrary"), 24 * tm * cw * 4),
    )(dh, h, h, ri, ri, xc, a_fwd, m_fwd, sp)


def _cscan_tables(lr, li, reverse):
    lam = (lr.reshape(-1), -li.reshape(-1) if reverse else li.reshape(-1))

    def mul(p, q):
        return p[0] * q[0] - p[1] * q[1], p[0] * q[1] + p[1] * q[0]

    pows = [lam]
    for _ in range(SUBLANES - 1):
        pows.append(mul(pows[-1], lam))
    zero = jnp.zeros_like(lam[0])
    tab = jnp.stack([pows[0][0], pows[0][1], pows[1][0], pows[1][1], pows[3][0], pows[3][1], zero, zero])
    if reverse:
        pows = pows[::-1]
    return tab, jnp.stack([p[0] for p in pows]), jnp.stack([p[1] for p in pows])


def _power_slabs(lr, li, n):
    pr, pi = lr.reshape(1, -1), li.reshape(1, -1)
    while pr.shape[0] < n:
        tr, ti = pr[-1:], pi[-1:]
        pr, pi = (jnp.concatenate([pr, pr * tr - pi * ti], axis=0), jnp.concatenate([pi, pr * ti + pi * tr], axis=0))
    return jnp.repeat(pr, SUBLANES, axis=0), jnp.repeat(pi, SUBLANES, axis=0), pr[-1], pi[-1]


def _rows_to_segments(src_ref, dst_ref):
    seg = src_ref.shape[0] // SUBLANES
    for g in range(seg):
        dst_ref[pl.ds(g * SUBLANES, SUBLANES), :] = src_ref[pl.ds(g, SUBLANES, stride=seg), :].astype(dst_ref.dtype)


def _segments_to_rows(src_ref, dst_ref):
    seg = src_ref.shape[0] // SUBLANES
    for r in range(SUBLANES):
        dst_ref[pl.ds(r * seg, seg), :] = src_ref[pl.ds(r, seg, stride=SUBLANES), :].astype(dst_ref.dtype)


def _seg_scan_tile(xr_ref, xi_ref, pbr_ref, pbi_ref, tab_ref, pwr_ref, pwi_ref, cr_ref, ci_ref, *, reverse, h=None):
    tm, cw = xr_ref.shape
    seg = tm // SUBLANES
    rows = _rows8(cw)
    sign = -1.0 if reverse else 1.0
    l_re, l_im = pbr_ref[0:1, :], sign * pbi_ref[0:1, :]

    def slab(g):
        return pl.ds(pl.multiple_of(g * SUBLANES, SUBLANES), SUBLANES)

    def local(k, state):
        sl = slab(seg - 1 - k if reverse else k)
        sr, si = state
        nr = xr_ref[sl, :] + (l_re * sr - l_im * si)
        ni = xi_ref[sl, :] + (l_re * si + l_im * sr)
        xr_ref[sl, :] = nr
        xi_ref[sl, :] = ni
        return nr, ni

    zero = jnp.zeros((SUBLANES, cw), F32)
    er, ei = lax.fori_loop(0, seg, local, (zero, zero), unroll=2)

    for k, s in enumerate((1, 2, 4)):
        shift = SUBLANES - s if reverse else s
        keep = rows < SUBLANES - s if reverse else rows >= s
        sr = jnp.where(keep, pltpu.roll(er, shift, 0), 0.0)
        si = jnp.where(keep, pltpu.roll(ei, shift, 0), 0.0)
        m_re, m_im = tab_ref[2 * k:2 * k + 1, :], tab_ref[2 * k + 1:2 * k + 2, :]
        er, ei = er + (m_re * sr - m_im * si), ei + (m_re * si + m_im * sr)
    cin_r, cin_i = cr_ref[0:1, :], ci_ref[0:1, :]
    pwr, pwi = pwr_ref[...], pwi_ref[...]
    er, ei = er + (pwr * cin_r - pwi * cin_i), ei + (pwr * cin_i + pwi * cin_r)
    if reverse:
        ent_r = jnp.where(rows == SUBLANES - 1, cin_r, pltpu.roll(er, SUBLANES - 1, 0))
        ent_i = jnp.where(rows == SUBLANES - 1, cin_i, pltpu.roll(ei, SUBLANES - 1, 0))
        out_r, out_i = er[0:1, :], ei[0:1, :]
    else:
        ent_r = jnp.where(rows == 0, cin_r, pltpu.roll(er, 1, 0))
        ent_i = jnp.where(rows == 0, cin_i, pltpu.roll(ei, 1, 0))
        out_r, out_i = er[SUBLANES - 1:SUBLANES, :], ei[SUBLANES - 1:SUBLANES, :]
    cr_ref[...] = jnp.broadcast_to(out_r, cr_ref.shape)
    ci_ref[...] = jnp.broadcast_to(out_i, ci_ref.shape)

    if h is not None:
        hr_ref, hi_ref, hr_last, hi_last = h
        hr_wrap = _shift_down(hr_ref[pl.ds(tm - SUBLANES, SUBLANES), :], hr_last, 1, rows)
        hi_wrap = _shift_down(hi_ref[pl.ds(tm - SUBLANES, SUBLANES), :], hi_last, 1, rows)

    def fix(g, sums):
        sl = slab(g)
        power = slab(seg - 1 - g) if reverse else sl
        pr, pi = pbr_ref[power, :], sign * pbi_ref[power, :]
        nr = xr_ref[sl, :] + (pr * ent_r - pi * ent_i)
        ni = xi_ref[sl, :] + (pr * ent_i + pi * ent_r)
        xr_ref[sl, :] = nr
        xi_ref[sl, :] = ni
        if h is None:
            return sums
        before = slab(jnp.maximum(g - 1, 0))
        hr1 = jnp.where(g == 0, hr_wrap, hr_ref[before, :])
        hi1 = jnp.where(g == 0, hi_wrap, hi_ref[before, :])
        return sums[0] + (nr * hr1 + ni * hi1), sums[1] + (ni * hr1 - nr * hi1)

    return lax.fori_loop(0, seg, fix, (zero, zero) if h is not None else (), unroll=2)


S5_TILE = 512


def _s5_fwd(z, u_off, wb_re, wb_im, wc_re, wc_im_neg, d_row, powers, *, T, tm=S5_TILE, name):
    J, ku, kp = wb_re.shape
    nt = T // tm
    pb_re, pb_im, top_re, top_im = powers
    tab, pw_re, pw_im = _cscan_tables(top_re, top_im, False)
    u_blk = u_off // ku

    def body(u_ref, wbr_ref, wbi_ref, wcr_ref, wci_ref, d_ref, pbr_ref, pbi_ref, tab_ref, pwr_ref, pwi_ref,
             hr_ref, hi_ref, y_ref, yg_ref, cr_ref, ci_ref, us_ref, ys_ref):
        @pl.when(pl.program_id(1) == 0)
        def _():
            cr_ref[...] = jnp.zeros_like(cr_ref)
            ci_ref[...] = jnp.zeros_like(ci_ref)

        _rows_to_segments(u_ref, us_ref)
        u = us_ref[...]
        ub = u.astype(BF16)
        hr_ref[...] = jnp.dot(ub, wbr_ref[...], preferred_element_type=F32)
        hi_ref[...] = jnp.dot(ub, wbi_ref[...], preferred_element_type=F32)
        _seg_scan_tile(hr_ref, hi_ref, pbr_ref, pbi_ref, tab_ref, pwr_ref, pwi_ref, cr_ref, ci_ref, reverse=False)
        y = (jnp.dot(hr_ref[...].astype(BF16), wcr_ref[...], preferred_element_type=F32)
             + jnp.dot(hi_ref[...].astype(BF16), wci_ref[...], preferred_element_type=F32) + d_ref[...] * u)
        ys_ref[...] = y
        _segments_to_rows(ys_ref, y_ref)
        ys_ref[...] = _gelu(y)
        _segments_to_rows(ys_ref, yg_ref)

    wb_spec = pl.BlockSpec((None, ku, kp), lambda j, i: (j, 0, 0))
    wc_spec = pl.BlockSpec((None, kp, ku), lambda j, i: (j, 0, 0))
    small = pl.BlockSpec((SUBLANES, kp), lambda j, i: (0, j))
    slabs = pl.BlockSpec((tm, kp), lambda j, i: (0, j))
    state = pl.BlockSpec((tm, kp), lambda j, i: (i, j))
    chan = pl.BlockSpec((tm, ku), lambda j, i: (i, j))
    return pl.pallas_call(
        body, name=name, grid=(J, nt),
        in_specs=[pl.BlockSpec((tm, ku), lambda j, i: (i, u_blk + j)), wb_spec, wb_spec, wc_spec, wc_spec,
                  pl.BlockSpec((1, ku), lambda j, i: (0, j)), slabs, slabs, small, small, small],
        out_specs=[state, state, chan, chan],
        out_shape=[jax.ShapeDtypeStruct((T, J * kp), F32)] * 2
        + [jax.ShapeDtypeStruct((T, J * ku), F32), jax.ShapeDtypeStruct((T, J * ku), BF16)],
        scratch_shapes=[pltpu.VMEM((SUBLANES, kp), F32), pltpu.VMEM((SUBLANES, kp), F32),
                        pltpu.VMEM((tm, ku), F32), pltpu.VMEM((tm, ku), F32)],
        compiler_params=_params(("parallel", "arbitrary"), 14 * tm * kp * 4),
    )(z, wb_re, wb_im, wc_re, wc_im_neg, d_row, pb_re, pb_im, tab, pw_re, pw_im)


def _s5_bwd(dy, z, u_off, h_re, h_im, wb_re, wb_im, wc_re, wc_im_neg, d_row, powers, dz, *, T, tm=S5_TILE, name):
    J, ku, kp = wb_re.shape
    nt, hb = T // tm, tm // SUBLANES
    pb_re, pb_im, top_re, top_im = powers
    tab, pw_re, pw_im = _cscan_tables(top_re, top_im, True)
    u_blk = u_off // ku
    contract_rows = (((0,), (0,)), ((), ()))
    contract_cols = (((1,), (1,)), ((), ()))

    def body(dy_ref, u_ref, hr_ref, hrp_ref, hi_ref, hip_ref, wbr_ref, wbi_ref, wcr_ref, wci_ref, d_ref,
             pbr_ref, pbi_ref, tab_ref, pwr_ref, pwi_ref, dz_in_ref,
             du_ref, dlr_ref, dli_ref, dd_ref, dwbr_ref, dwbi_ref, dwcr_ref, dwci_ref,
             gr_ref, gi_ref, cr_ref, ci_ref, dys_ref, us_ref):
        step = pl.program_id(1)
        first = step == 0

        @pl.when(first)
        def _():
            cr_ref[...] = jnp.zeros_like(cr_ref)
            ci_ref[...] = jnp.zeros_like(ci_ref)

        _rows_to_segments(dy_ref, dys_ref)
        _rows_to_segments(u_ref, us_ref)
        dy_t, u = dys_ref[...], us_ref[...]
        dyb, ub = dy_t.astype(BF16), u.astype(BF16)
        gr_ref[...] = lax.dot_general(dyb, wcr_ref[...], contract_cols, preferred_element_type=F32)
        gi_ref[...] = lax.dot_general(dyb, wci_ref[...], contract_cols, preferred_element_type=F32)
        hr_last = jnp.where(step == nt - 1, 0.0, hrp_ref[...])
        hi_last = jnp.where(step == nt - 1, 0.0, hip_ref[...])
        s_re, s_im = _seg_scan_tile(gr_ref, gi_ref, pbr_ref, pbi_ref, tab_ref, pwr_ref, pwi_ref, cr_ref, ci_ref,
                                    reverse=True, h=(hr_ref, hi_ref, hr_last, hi_last))
        _accumulate(dlr_ref, jnp.sum(s_re, axis=0, keepdims=True), first)
        _accumulate(dli_ref, jnp.sum(s_im, axis=0, keepdims=True), first)
        grb, gib = gr_ref[...].astype(BF16), gi_ref[...].astype(BF16)
        du = (lax.dot_general(grb, wbr_ref[...], contract_cols, preferred_element_type=F32)
              + lax.dot_general(gib, wbi_ref[...], contract_cols, preferred_element_type=F32) + dy_t * d_ref[...])
        dys_ref[...] = du
        _segments_to_rows(dys_ref, du_ref)
        _accumulate(dd_ref, jnp.sum(dy_t * u, axis=0, keepdims=True), first)
        _accumulate(dwbr_ref, lax.dot_general(ub, grb, contract_rows, preferred_element_type=F32), first)
        _accumulate(dwbi_ref, lax.dot_general(ub, gib, contract_rows, preferred_element_type=F32), first)
        _accumulate(dwcr_ref, lax.dot_general(dyb, hr_ref[...].astype(BF16), contract_rows,
                                              preferred_element_type=F32), first)
        _accumulate(dwci_ref, lax.dot_general(dyb, hi_ref[...].astype(BF16), contract_rows,
                                              preferred_element_type=F32), first)

    def tix(i):
        return nt - 1 - i

    wb_spec = pl.BlockSpec((None, ku, kp), lambda j, i: (j, 0, 0))
    wc_spec = pl.BlockSpec((None, kp, ku), lambda j, i: (j, 0, 0))
    small = pl.BlockSpec((SUBLANES, kp), lambda j, i: (0, j))
    state = pl.BlockSpec((tm, kp), lambda j, i: (tix(i), j))
    halo = pl.BlockSpec((SUBLANES, kp), lambda j, i: (jnp.maximum(tix(i) * hb - 1, 0), j))
    chan = pl.BlockSpec((tm, ku), lambda j, i: (tix(i), j))
    svec = pl.BlockSpec((1, kp), lambda j, i: (0, j))
    cvec = pl.BlockSpec((1, ku), lambda j, i: (0, j))
    slabs = pl.BlockSpec((tm, kp), lambda j, i: (0, j))
    return pl.pallas_call(
        body, name=name, grid=(J, nt),
        in_specs=[chan, pl.BlockSpec((tm, ku), lambda j, i: (tix(i), u_blk + j)), state, halo, state, halo,
                  wb_spec, wb_spec, wc_spec, wc_spec, cvec, slabs, slabs, small, small, small, ANY],
        out_specs=[pl.BlockSpec((tm, ku), lambda j, i: (tix(i), u_blk + j)), svec, svec, cvec,
                   wb_spec, wb_spec, wb_spec, wb_spec],
        input_output_aliases={16: 0},
        out_shape=[jax.ShapeDtypeStruct(dz.shape, dz.dtype), jax.ShapeDtypeStruct((1, J * kp), F32),
                   jax.ShapeDtypeStruct((1, J * kp), F32), jax.ShapeDtypeStruct((1, J * ku), F32),
                   jax.ShapeDtypeStruct((J, ku, kp), F32), jax.ShapeDtypeStruct((J, ku, kp), F32),
                   jax.ShapeDtypeStruct((J, ku, kp), F32), jax.ShapeDtypeStruct((J, ku, kp), F32)],
        scratch_shapes=[pltpu.VMEM((tm, kp), F32), pltpu.VMEM((tm, kp), F32),
                        pltpu.VMEM((SUBLANES, kp), F32), pltpu.VMEM((SUBLANES, kp), F32),
                        pltpu.VMEM((tm, ku), F32), pltpu.VMEM((tm, ku), F32)],
        compiler_params=_params(("parallel", "arbitrary"), 16 * tm * kp * 4),
    )(dy, z, h_re, h_re, h_im, h_im, wb_re, wb_im, wc_re, wc_im_neg, d_row, pb_re, pb_im, tab, pw_re, pw_im, dz)


def _mesh_pos():
    return lax.axis_index("x"), lax.axis_index("y"), lax.axis_index("c")


def _dev_index(px, py, pc):
    return 4 * px + 2 * py + pc


HBM = pl.BlockSpec(memory_space=pltpu.HBM)
SEM = pl.BlockSpec(memory_space=pltpu.SEMAPHORE)
EFFECT = pltpu.SideEffectType.DATAFLOW_SIDE_EFFECTING
RELATIONS = [(dx, dy, dc) for dx in (0, 1) for dy in (0, 1) for dc in (0, 1) if (dx, dy, dc) != (0, 0, 0)]


def _peer(rel):
    x, y, c = _mesh_pos()
    dx, dy, dc = rel
    return (x + dx - 2 * x * dx, y + dy - 2 * y * dy, c + dc - 2 * c * dc)


CHIP_RELATIONS = [(1, 0, 0), (0, 1, 0), (1, 1, 0)]
EXCHANGE_PEERS = {"gather": RELATIONS, "scatter": RELATIONS, "own": [(0, 0, 1)] + CHIP_RELATIONS, "pass": CHIP_RELATIONS}


def _split_copy(src_ref, land_ref, send_sems, recv_sems, k, mode, incoming):
    x, y, c = _mesh_pos()
    me = _dev_index(x, y, c)
    peer = _peer(EXCHANGE_PEERS[mode][k])
    if mode == "pass":
        held, theirs = _dev_index(peer[0], peer[1], c), _dev_index(peer[0], peer[1], 1 - c)
        src, slot, target = land_ref.at[held], theirs if incoming else held, (x, y, 1 - c)
    else:
        src = src_ref.at[_dev_index(*peer)] if mode == "scatter" else src_ref
        slot, target = _dev_index(*peer) if incoming else me, peer
    return pltpu.make_async_remote_copy(src_ref=src, dst_ref=land_ref.at[slot], send_sem=send_sems.at[k],
                                        recv_sem=recv_sems.at[k], device_id=target, device_id_type=MESH)


def _exchange_start(srcs, lands, *, mode, after=None, name):
    n = len(srcs)
    n_after = 0 if after is None else 1
    n_rel = len(EXCHANGE_PEERS[mode])

    def body(*refs):
        src_refs, land_refs = refs[:n], refs[n:2 * n]
        first_out = 2 * n + n_after
        send, recv = refs[first_out:first_out + n], refs[first_out + n:first_out + 2 * n]
        token = refs[-1]
        for k in range(n_rel):
            for a in range(n):
                _split_copy(src_refs[a], land_refs[a], send[a], recv[a], k, mode, incoming=False).start()
        token[...] = jnp.zeros_like(token)

    outs = pl.pallas_call(
        body, name=name, in_specs=[HBM] * (2 * n) + [ANY] * n_after,
        out_shape=[pltpu.SemaphoreType.DMA((n_rel,))] * (2 * n)
        + [pltpu.HBM(s.shape, s.dtype) for s in srcs] + [pltpu.HBM(s.shape, s.dtype) for s in lands]
        + [jax.ShapeDtypeStruct((SUBLANES, LANES), F32)],
        out_specs=[SEM] * (2 * n) + [HBM] * (2 * n) + [pl.BlockSpec(memory_space=pltpu.VMEM)],
        input_output_aliases={**{a: 2 * n + a for a in range(n)}, **{n + a: 3 * n + a for a in range(n)}},
        compiler_params=pltpu.CompilerParams(has_side_effects=EFFECT),
    )(*[pltpu.with_memory_space_constraint(s, pltpu.HBM) for s in srcs],
      *[pltpu.with_memory_space_constraint(s, pltpu.HBM) for s in lands], *([after] if n_after else []))
    per_array = [(outs[a], outs[n + a], outs[2 * n + a], outs[3 * n + a]) for a in range(n)]
    return per_array, outs[-1]


def _exchange_wait(handle, after, *, mode, name):
    send_sems, recv_sems, src_thru, land_thru = handle
    after = after if isinstance(after, (tuple, list)) else (after,)

    def body(src_ref, land_ref, send, recv, *rest):
        for k in range(len(EXCHANGE_PEERS[mode])):
            cp = _split_copy(src_ref, land_ref, send, recv, k, mode, incoming=True)
            cp.wait_send()
            cp.wait_recv()

    return pl.pallas_call(
        body, name=name, in_specs=[HBM, HBM, SEM, SEM] + [ANY] * len(after),
        out_shape=[pltpu.HBM(src_thru.shape, src_thru.dtype), pltpu.HBM(land_thru.shape, land_thru.dtype)],
        out_specs=[HBM, HBM], input_output_aliases={0: 0, 1: 1},
        compiler_params=pltpu.CompilerParams(has_side_effects=EFFECT),
    )(src_thru, land_thru, send_sems, recv_sems, *after)[1]


def _landing_zone(own_block):
    me = _dev_index(*_mesh_pos())
    zone = lax.empty((N_DEV,) + own_block.shape, own_block.dtype)
    return lax.dynamic_update_index_in_dim(zone, own_block, me, 0)


def _row_tile(rows, want):
    t = min(want, rows) // SUBLANES * SUBLANES
    while rows % t:
        t -= SUBLANES
    return t


def _sum_slots(recv, *, tr, name):
    s_, r_, c_ = recv.shape
    tr = _row_tile(r_, tr)

    def body(g_ref, o_ref):
        acc = g_ref[0]
        for s in range(1, s_):
            acc = acc + g_ref[s]
        o_ref[...] = acc

    return pl.pallas_call(
        body, name=name, grid=(r_ // tr,),
        in_specs=[pl.BlockSpec((s_, tr, c_), lambda i: (0, i, 0))],
        out_specs=pl.BlockSpec((tr, c_), lambda i: (i, 0)),
        out_shape=jax.ShapeDtypeStruct((r_, c_), F32),
        compiler_params=_params(("parallel",), (2 * s_ + 3) * tr * c_ * 4),
    )(recv)


def _adamw(recv, w, m, v, *, tr, name):
    s_, r_, c_ = recv.shape
    tr = _row_tile(r_, tr)
    assert w.shape == (r_, c_), (name, w.shape, recv.shape)
    c1 = 1.0 - ADAM_B1 ** ADAM_STEP
    c2 = 1.0 - ADAM_B2 ** ADAM_STEP

    def body(g_ref, w_ref, m_ref, v_ref, go_ref, d_ref, mo_ref, vo_ref):
        g = g_ref[0].astype(F32)
        for s in range(1, s_):
            g = g + g_ref[s].astype(F32)
        mn = ADAM_B1 * m_ref[...] + (1.0 - ADAM_B1) * g
        vn = ADAM_B2 * v_ref[...] + (1.0 - ADAM_B2) * (g * g)
        go_ref[...] = g
        mo_ref[...] = mn
        vo_ref[...] = vn
        d_ref[...] = -ADAM_LR * ((mn / c1) / (jnp.sqrt(vn / c2) + ADAM_EPS) + ADAM_WD * w_ref[...])

    tile = pl.BlockSpec((tr, c_), lambda i: (i, 0))
    return pl.pallas_call(
        body, name=name, grid=(r_ // tr,),
        in_specs=[pl.BlockSpec((s_, tr, c_), lambda i: (0, i, 0)), tile, tile, tile],
        out_specs=[tile] * 4, out_shape=[jax.ShapeDtypeStruct((r_, c_), F32)] * 4,
        compiler_params=_params(("parallel",), (2 * s_ + 16) * tr * c_ * 4),
    )(recv, w, m, v)


def _adamw_whole(gs, ws, ms, vs, *, name):
    n = len(gs)
    c1 = 1.0 - ADAM_B1 ** ADAM_STEP
    c2 = 1.0 - ADAM_B2 ** ADAM_STEP

    def body(*refs):
        for i in range(n):
            g, w = refs[i][...], refs[n + i][...]
            mn = ADAM_B1 * refs[2 * n + i][...] + (1.0 - ADAM_B1) * g
            vn = ADAM_B2 * refs[3 * n + i][...] + (1.0 - ADAM_B2) * (g * g)
            refs[4 * n + 3 * i][...] = -ADAM_LR * ((mn / c1) / (jnp.sqrt(vn / c2) + ADAM_EPS) + ADAM_WD * w)
            refs[4 * n + 3 * i + 1][...] = mn
            refs[4 * n + 3 * i + 2][...] = vn

    whole = pl.BlockSpec(memory_space=pltpu.VMEM)
    lane_padded = sum(math.prod(g.shape[:-1]) * (-(-g.shape[-1] // LANES) * LANES) for g in gs)
    outs = pl.pallas_call(
        body, name=name, in_specs=[whole] * (4 * n), out_specs=[whole] * (3 * n),
        out_shape=[jax.ShapeDtypeStruct(g.shape, F32) for g in gs for _ in range(3)],
        compiler_params=pltpu.CompilerParams(vmem_limit_bytes=int(min(max(16 * lane_padded * 4, 16 * 2 ** 20), VMEM_CAP))),
    )(*gs, *ws, *ms, *vs)
    return [tuple(outs[3 * i:3 * i + 3]) for i in range(n)]


def _s5_discretise(a_re, a_im, log_dt, b_re, b_im):
    dt = jnp.exp(log_dt)[:, None]
    lr = jnp.minimum(a_re, -1e-4)
    li = a_im
    mag = jnp.exp(lr * dt)
    lbr = mag * jnp.cos(li * dt)
    lbi = mag * jnp.sin(li * dt)
    zr, zi = lbr - 1.0, lbi
    den = lr * lr + li * li
    fr = (zr * lr + zi * li) / den
    fi = (zi * lr - zr * li) / den
    bbr = fr[..., None] * b_re - fi[..., None] * b_im
    bbi = fr[..., None] * b_im + fi[..., None] * b_re
    return lbr, lbi, bbr, bbi


def _softplus_neg(lam):
    return jnp.maximum(-lam, 0.0) + jnp.log(1.0 + jnp.exp(-jnp.abs(lam)))


S5_Q = 8
RG_Q = 2


def _local_step(x, tgt, W, comm):
    T, D = x.shape
    C = D
    G, P, H = W["ssm_b_re"].shape
    S = G * H
    F = W["mlp_b_up"].shape[1]
    n_in = 2 * C + S + 2 * D
    heads, hd = W["rg_wa"].shape[0], W["rg_wa"].shape[1]
    u_off, ga_off, gb_off = 2 * C, 2 * C + S, 2 * C + S + D

    if comm.first_token is not None:
        anchored = ("rg_lambda", "ssm_a_re", "rg_wa", "rg_wx", "ssm_c_re", "ssm_c_im")
        W = {**W, **{k: W[k] + comm.first_token[0, 0] for k in anchored}}
    sp, sp_vjp = jax.vjp(_softplus_neg, W["rg_lambda"])
    (lbr, lbi, bbr, bbi), s5_vjp = jax.vjp(_s5_discretise, W["ssm_a_re"], W["ssm_a_im"], W["ssm_log_dt"],
                                           W["ssm_b_re"], W["ssm_b_im"])
    lam_re, lam_im = lbr.reshape(-1), lbi.reshape(-1)
    jr, kr = heads // RG_Q, RG_Q * hd
    w_ri = jnp.concatenate([_bd_pack(W["rg_wa"], RG_Q), _bd_pack(W["rg_wx"], RG_Q)], axis=2).astype(BF16)
    b_ri = jnp.concatenate([W["rg_ba"].reshape(jr, kr), W["rg_bx"].reshape(jr, kr)], axis=1).reshape(1, -1)
    wb_re = _bd_pack(jnp.swapaxes(bbr, 1, 2), S5_Q).astype(BF16)
    wb_im = _bd_pack(jnp.swapaxes(bbi, 1, 2), S5_Q).astype(BF16)
    wc_re = _bd_pack(jnp.swapaxes(W["ssm_c_re"], 1, 2), S5_Q).astype(BF16)
    wc_im_neg = _bd_pack(jnp.swapaxes(-W["ssm_c_im"], 1, 2), S5_Q).astype(BF16)
    d_row = W["ssm_d"].reshape(1, S)
    powers = _power_slabs(lam_re, lam_im, S5_TILE // SUBLANES)

    x_bf = x.astype(BF16) if comm.first_token is None else (x + comm.first_token[0, 0]).astype(BF16)
    w_in, conv_w = comm.first_weights((x_bf, w_ri, wb_re, wb_im, wc_re, wc_im_neg, powers[0], powers[1]))
    z = _mm(x_bf, w_in, M=T, N=n_in, K=D, tm=512, tn=n_in // 4, tk=D, after=comm.gather_token, name="fwd_in_proj")
    started = comm.start_weights(("mlp_w_up",), z)
    xc = _conv_fwd(z, conv_w, W["conv_b"], T=T, C=C, after=started, name="fwd_conv")
    ri = _bd([(xc, 0, w_ri)], T=T, J=jr, kb=kr, nb=2 * kr, extras=[(b_ri, "vec", 0)],
             epilogue=lambda acc, b: (_sig(acc + b),), name="fwd_gates")
    h, p, a_fwd, m_fwd = _rg_scan_fwd(z, ri, xc, sp, T=T, C=C, gate_off=C, cw=kr, name="fwd_rg_scan")
    w_a_out = comm.weight("w_a_out", p)
    started = comm.start_weights(("mlp_w_down",), p)
    y_a = _mm(p, w_a_out, M=T, N=D, K=C, out_dtypes=(BF16,), tm=512, tn=D, tk=C, after=started, name="fwd_rg_out")

    h_re, h_im, y_s, yg = _s5_fwd(z, u_off, wb_re, wb_im, wc_re, wc_im_neg, d_row, powers, T=T, name="fwd_s5")
    w_glu_w, w_glu_v = comm.weight("glu_w", yg), comm.weight("glu_v", yg)
    glu_a = _mm(yg, w_glu_w, M=T, N=D, K=S, out_dtypes=(BF16,), tm=1024, tn=D, tk=S, name="fwd_glu_w")
    cwm = 1024

    def mix_fn(b, ga, gb, ya, a):
        return b, _sig(ga) * ya.astype(F32) + _sig(gb) * (a.astype(F32) * _sig(b))

    glu_b, mix = _mm(yg, w_glu_v, M=T, N=D, K=S, tm=512, tn=cwm, tk=S,
                     extras=[(z, "mn", ga_off // cwm), (z, "mn", gb_off // cwm), (y_a, "mn"), (glu_a, "mn")],
                     epilogue=mix_fn, n_out=2, out_dtypes=(BF16, BF16), name="fwd_glu_v_mix")
    w_out = comm.weight("w_out", mix)
    def out_ln1_fn(acc, xv, g, b):
        s = ALPHA * xv + acc
        xhat, _ = _ln_stats(s)
        y = xhat * g + b
        return s, y, y

    s1, x1, x1_bf = _mm(mix, w_out, M=T, N=D, K=D, tm=256, tn=D, tk=D,
                        extras=[(x, "mn"), (W["ln1_g"], "n"), (W["ln1_b"], "n")], epilogue=out_ln1_fn, n_out=3,
                        out_dtypes=(F32, F32, BF16), name="fwd_out_proj_ln1")
    w_up = comm.weight("mlp_w_up", x1_bf)

    def mlp_up_fn(acc, b):
        hp = acc + b
        rl = jnp.maximum(hp, 0.0)
        return rl * rl, hp

    hact, hpre = _mm(x1_bf, w_up, M=T, N=F, K=D, tm=1024, tn=1024, tk=D, extras=[(W["mlp_b_up"], "n")],
                     epilogue=mlp_up_fn, n_out=2, out_dtypes=(BF16, BF16), name="fwd_mlp_up")
    w_down = comm.weight("mlp_w_down", hact)
    s2 = _mm(hact, w_down, M=T, N=D, K=F, tm=1024, tn=1024, tk=2048,
             extras=[(x1, "mn"), (W["mlp_b_down"], "n")], epilogue=lambda acc, xv, b: (ALPHA * xv + acc + b,),
             name="fwd_mlp_down")

    def ln2_fn(s, t, g, b):
        xhat, rstd = _ln_stats(s)
        err = xhat * g + b - t
        dy = err * (1.0 / D)
        ds = _ln_bwd(dy, g, xhat, rstd)
        return ds, ds, 0.5 * dy * err, dy * xhat, dy, ds

    ds2, ds2_bf, loss_cols, d_ln2_g, d_ln2_b, d_b_down = _ew(
        ln2_fn, [(s2, "tile", 0), (tgt, "tile", 0), (W["ln2_g"], "vec", 0), (W["ln2_b"], "vec", 0)],
        T=T, C=D, n_out=2, n_cs=4, out_dtypes=(F32, BF16), tm=256, name="bwd_loss_ln2")
    d_w_down = _mm(hact, ds2_bf, M=F, N=D, K=T, ta=True, out_dtypes=(BF16,), tm=1024, tn=1024, tk=4096, name="bwd_w_down")
    sent = comm.send_grad("mlp_w_down", d_w_down)

    def dhpre_fn(acc, hp):
        dv = acc * (2.0 * jnp.maximum(hp.astype(F32), 0.0))
        return dv, dv

    dhpre, d_b_up = _mm(ds2_bf, w_down, M=T, N=F, K=D, tb=True, tm=1024, tn=1024, tk=D, extras=[(hpre, "mn")],
                        epilogue=dhpre_fn, n_cs=1, out_dtypes=(BF16,), after=sent, name="bwd_mlp_down")
    d_w_up = _mm(x1_bf, dhpre, M=D, N=F, K=T, ta=True, out_dtypes=(BF16,), n_split=N_DEV, tm=1024, tn=F // N_DEV, tk=4096, name="bwd_w_up")
    sent = comm.send_grad("mlp_w_up", d_w_up)
    dx1 = _mm(dhpre, w_up, M=T, N=D, K=F, tb=True, tm=1024, tn=1024, tk=2048,
              extras=[(ds2, "mn")], epilogue=lambda acc, dv: (ALPHA * dv + acc,), after=sent, name="bwd_mlp_up")

    def ln1_bwd_fn(s, dy, g):
        xhat, rstd = _ln_stats(s)
        ds = _ln_bwd(dy, g, xhat, rstd)
        return ds, ds, dy * xhat, dy

    ds1, ds1_bf, d_ln1_g, d_ln1_b = _ew(ln1_bwd_fn, [(s1, "tile", 0), (dx1, "tile", 0), (W["ln1_g"], "vec", 0)],
                                        T=T, C=D, n_out=2, n_cs=2, out_dtypes=(F32, BF16), tm=256, name="bwd_ln1")
    d_w_out = _mm(mix, ds1_bf, M=D, N=D, K=T, ta=True, out_dtypes=(BF16,), tm=1024, tn=1024, tk=4096, name="bwd_w_out")
    sent = comm.send_grad("w_out", d_w_out)
    def mix_bwd_fn(dm, ga, gb, ya, a, b):
        ya, a, b = ya.astype(F32), a.astype(F32), b.astype(F32)
        sa, sb, sv = _sig(ga), _sig(gb), _sig(b)
        yb = a * sv
        dyb = dm * sb
        return (dm * ya * (sa * (1.0 - sa)), dm * yb * (sb * (1.0 - sb)), dm * sa, dyb * sv,
                dyb * a * (sv * (1.0 - sv)))

    dz = lax.empty((T, n_in), BF16)
    dz, dg_b, dy_a, dglu_a, dglu_b = _mm(
        ds1_bf, w_out, M=T, N=D, K=D, tb=True, tm=512, tn=cwm, tk=D,
        extras=[(z, "mn", ga_off // cwm), (z, "mn", gb_off // cwm), (y_a, "mn"), (glu_a, "mn"), (glu_b, "mn")],
        epilogue=mix_bwd_fn, n_out=5, out_dtypes=(BF16,) * 5, after=sent, into=(dz, 0, ga_off // cwm),
        name="bwd_out_proj_mix")
    dz = lax.dynamic_update_slice(dz, dg_b, (0, gb_off))

    d_w_a_out = _mm(p, dy_a, M=C, N=D, K=T, ta=True, out_dtypes=(BF16,), tm=1024, tn=1024, tk=4096, name="bwd_w_a_out")
    sent = comm.send_grad("w_a_out", d_w_a_out)
    def dp_fn(dp, hv, gate):
        th = jnp.tanh(GELU_C * (gate + GELU_K * gate * gate * gate))
        gelu = 0.5 * gate * (1.0 + th)
        dgelu = 0.5 * (1.0 + th) + 0.5 * gate * (1.0 - th * th) * (GELU_C * (1.0 + 3.0 * GELU_K * gate * gate))
        return dp * gelu, dp * hv * dgelu

    dh, dz = _mm(dy_a, w_a_out, M=T, N=C, K=D, tb=True, tm=256, tn=C, tk=D, extras=[(h, "mn"), (z, "mn", 1)],
                 epilogue=dp_fn, n_out=2, out_dtypes=(F32, BF16), after=sent, into=(dz, 1, 1), name="bwd_rg_out")
    drai, dxc0, d_b_ri, d_sp = _rg_scan_bwd(dh, h, ri, xc, a_fwd, m_fwd, sp, T=T, C=C, cw=kr, name="bwd_rg_scan")
    dxc = _bd([(drai, 0, w_ri)], T=T, J=jr, kb=2 * kr, nb=kr, tw=True, extras=[(dxc0, "tile", 0)],
              epilogue=lambda acc, d0: (acc + d0,), name="bwd_gates")
    d_w_ri = _bdw(xc, 0, drai, 0, T=T, J=jr, kb=kr, nb=2 * kr, name="bwd_w_gates")
    d_wa, d_wx = _bd_unpack(d_w_ri[:, :, :kr], RG_Q), _bd_unpack(d_w_ri[:, :, kr:], RG_Q)
    d_b_ri = d_b_ri.reshape(jr, 2 * kr)
    d_ba, d_bx = d_b_ri[:, :kr].reshape(1, -1), d_b_ri[:, kr:].reshape(1, -1)
    dz, conv_sums = _conv_bwd(dxc, z, conv_w, dz, T=T, C=C, name="bwd_conv")
    d_conv_w, d_conv_b = conv_sums[0:4], conv_sums[4:5]
    (d_lambda,) = sp_vjp(d_sp)

    d_glu_w = _mm(yg, dglu_a, M=S, N=D, K=T, ta=True, out_dtypes=(BF16,), n_split=N_DEV, tm=1024, tn=D // N_DEV, tk=4096, name="bwd_w_glu_w")
    d_glu_v = _mm(yg, dglu_b, M=S, N=D, K=T, ta=True, out_dtypes=(BF16,), n_split=N_DEV, tm=1024, tn=D // N_DEV, tk=4096, name="bwd_w_glu_v")
    sent = comm.send_grad("glu_w", d_glu_w, "glu_v", d_glu_v)
    dyg0 = _mm(dglu_a, w_glu_w, M=T, N=S, K=D, tb=True, tm=512, tn=S, tk=D, after=sent, name="bwd_glu_w")
    dy_s = _mm(dglu_b, w_glu_v, M=T, N=S, K=D, tb=True, tm=512, tn=S, tk=D,
               extras=[(dyg0, "mn"), (y_s, "mn")], epilogue=lambda acc, d0, yv: ((acc + d0) * _dgelu(yv),),
               name="bwd_glu_v")
    dz, d_lbr, d_lbi, d_ssm_d, d_wb_re, d_wb_im, d_wc_re, d_wc_im_neg = _s5_bwd(
        dy_s, z, u_off, h_re, h_im, wb_re, wb_im, wc_re, wc_im_neg, d_row, powers, dz, T=T, name="bwd_s5")
    d_bbr = jnp.swapaxes(_bd_unpack(d_wb_re, S5_Q), 1, 2)
    d_bbi = jnp.swapaxes(_bd_unpack(d_wb_im, S5_Q), 1, 2)
    d_a_re, d_a_im, d_log_dt, d_b_re, d_b_im = s5_vjp((d_lbr.reshape(G, P), d_lbi.reshape(G, P), d_bbr, d_bbi))
    d_c_re = _bd_unpack(d_wc_re, S5_Q)
    d_c_im = -_bd_unpack(d_wc_im_neg, S5_Q)

    grads = dict(
        conv_w=d_conv_w, conv_b=d_conv_b, rg_wa=d_wa, rg_ba=d_ba, rg_wx=d_wx, rg_bx=d_bx,
        rg_lambda=d_lambda, ssm_a_re=d_a_re, ssm_a_im=d_a_im, ssm_log_dt=d_log_dt,
        ssm_b_re=d_b_re, ssm_b_im=d_b_im, ssm_c_re=d_c_re, ssm_c_im=d_c_im, ssm_d=d_ssm_d.reshape(G, H),
        ln1_g=d_ln1_g, ln1_b=d_ln1_b, mlp_b_up=d_b_up, mlp_b_down=d_b_down, ln2_g=d_ln2_g, ln2_b=d_ln2_b)
    sent = comm.send_small(grads)

    d_w_in = _mm(x_bf, dz, M=D, N=n_in, K=T, ta=True, out_dtypes=(BF16,), n_split=N_DEV, tm=1024, tn=n_in // N_DEV,
                 tk=4096, after=sent, name="bwd_w_in")
    sent = comm.send_grad("w_in", d_w_in)
    grad_x = _mm(dz, w_in, M=T, N=D, K=n_in, tb=True, tm=1024, tn=1024, tk=n_in // 4,
                 extras=[(ds1, "mn")], epilogue=lambda acc, dv: (ALPHA * dv + acc,), after=sent, name="bwd_in_proj")
    return jnp.sum(loss_cols), grad_x, grads


BIG = ("w_in", "w_a_out", "glu_w", "glu_v", "w_out", "mlp_w_up", "mlp_w_down")
COL_SHARDED = ("w_in", "glu_w", "glu_v", "mlp_w_up")
SMALL = ("conv_w", "conv_b", "rg_wa", "rg_ba", "rg_wx", "rg_bx", "rg_lambda", "ssm_a_re", "ssm_a_im", "ssm_log_dt",
         "ssm_b_re", "ssm_b_im", "ssm_c_re", "ssm_c_im", "ssm_d", "ln1_g", "ln1_b", "mlp_b_up", "mlp_b_down", "ln2_g",
         "ln2_b")
ORDER = ("w_in", "conv_w", "conv_b", "rg_wa", "rg_ba", "rg_wx", "rg_bx", "rg_lambda", "w_a_out", "ssm_a_re",
         "ssm_a_im", "ssm_log_dt", "ssm_b_re", "ssm_b_im", "ssm_c_re", "ssm_c_im", "ssm_d", "glu_w", "glu_v", "w_out",
         "ln1_g", "ln1_b", "mlp_w_up", "mlp_b_up", "mlp_w_down", "mlp_b_down", "ln2_g", "ln2_b")
TILE_ELEMS = SUBLANES * LANES


def _pack(arrs):
    pieces = []
    for a in arrs:
        flat = a.reshape(-1)
        flat = jnp.pad(flat, (0, (-flat.shape[0]) % TILE_ELEMS))
        pieces.append(flat.reshape(-1, LANES))
    rows = sum(p.shape[0] for p in pieces)
    pad_rows = (-rows) % (N_DEV * SUBLANES)
    if pad_rows:
        pieces.append(jnp.zeros((pad_rows, LANES), pieces[0].dtype))
    return jnp.concatenate(pieces, axis=0)


def _unpack(packed, shapes):
    out, row = [], 0
    for shp in shapes:
        n = math.prod(shp)
        rows = -(-n // TILE_ELEMS) * SUBLANES
        out.append(packed[row:row + rows].reshape(-1)[:n].reshape(shp))
        row += rows
    return out


class _Comm:
    def __init__(self, w):
        first = [w["w_in"].astype(BF16), w["conv_w"]]
        self._first, self.first_token = _exchange_start(first, [_landing_zone(s) for s in first], mode="own",
                                                        name="gather_in_start")
        self._shards = {k: w[k].astype(BF16) for k in BIG if k != "w_in"}
        self._weights, self._gathers, self._grads = {}, {}, {}

    def first_weights(self, after):
        lands = [_exchange_wait(h, after, mode="own", name="gather_in_wait_%d" % i) for i, h in enumerate(self._first)]
        unused = [lax.empty((2 * SUBLANES, LANES), BF16) for _ in lands]
        handles, passed = _exchange_start(unused, lands, mode="pass", name="gather_in_pass")
        w_in, taps = [_exchange_wait(h, passed, mode="pass", name="gather_in_got_%d" % i) for i, h in enumerate(handles)]
        self._weights["w_in"] = w_in
        self.gather_token = self.start_weights(("w_a_out", "glu_w", "glu_v", "w_out"), w_in)
        return self.weight("w_in", None), jnp.swapaxes(taps, 0, 1).reshape(taps.shape[1], -1)

    def start_weights(self, names, after):
        shards = [self._shards.pop(k) for k in names]
        handles, token = _exchange_start(shards, [_landing_zone(s) for s in shards], mode="gather", after=after,
                                         name="gather_start_" + names[0])
        self._gathers.update(zip(names, handles))
        return token

    def weight(self, k, after):
        if k not in self._weights:
            self._weights[k] = _exchange_wait(self._gathers.pop(k), after, mode="gather", name="gather_wait_" + k)
        gk = self._weights[k]
        if k in COL_SHARDED:
            return jnp.swapaxes(gk, 0, 1).reshape(gk.shape[1], -1)
        return gk.reshape(-1, gk.shape[-1])

    def send_grad(self, *names_and_parts):
        names, parts = names_and_parts[0::2], names_and_parts[1::2]
        parts = [p if k in COL_SHARDED else p.reshape(N_DEV, p.shape[0] // N_DEV, p.shape[1])
                 for k, p in zip(names, parts)]
        me = _dev_index(*_mesh_pos())
        lands = [_landing_zone(lax.dynamic_index_in_dim(p, me, 0, keepdims=False)) for p in parts]
        handles, token = _exchange_start(parts, lands, mode="scatter", name="grad_start_" + names[0])
        self._grads.update(zip(names, handles))
        return token

    def received_grad(self, k, after):
        return _exchange_wait(self._grads.pop(k), after, mode="scatter", name="grad_wait_" + k)

    def send_small(self, grads):
        return self.send_grad("small", _pack([grads[k] for k in SMALL]))

    def all_reduced_small(self, after, behind):
        recv = self.received_grad("small", after)
        block = _sum_slots(recv, tr=512, name="sum_small_grads")
        (handle,), started = _exchange_start([block], [_landing_zone(block)], mode="gather", name="small_sum_start")
        done = behind(started)
        return _exchange_wait(handle, done, mode="gather", name="small_sum_wait").reshape(-1, LANES)


SMALL_GROUPS = (("rg_wa", "rg_wx"), ("ssm_b_re",), ("ssm_b_im",),
                tuple(k for k in SMALL if k not in ("rg_wa", "rg_wx", "ssm_b_re", "ssm_b_im")))


def _step(x, tgt, w, m, v, raw_w, raw_m, raw_v):
    dev = _dev_index(*_mesh_pos())

    comm = _Comm(w)
    small = dict(w)
    for k in ("conv_b", "rg_ba", "rg_bx", "rg_lambda", "ln1_g", "ln1_b", "mlp_b_up", "mlp_b_down", "ln2_g", "ln2_b"):
        small[k] = w[k].reshape(1, -1)

    loss_part, grad_x, grads = _local_step(x, tgt, small, comm)

    out_g, out_d, out_m, out_v = {}, {}, {}, {}

    def update_large(started):
        for k in BIG:
            rk = comm.received_grad(k, (grad_x, started))
            out_g[k], out_d[k], out_m[k], out_v[k] = _adamw(rk, w[k], m[k], v[k], tr=128, name="adamw_" + k)
        return out_v[BIG[-1]]

    small_all = comm.all_reduced_small(grad_x, update_large)
    g_small = dict(zip(SMALL, _unpack(small_all, [grads[k].shape for k in SMALL])))
    cw_cols = w["conv_w"].shape[1]
    g_small["conv_w"] = lax.dynamic_slice_in_dim(g_small["conv_w"], dev * cw_cols, cw_cols, axis=1)
    for group in SMALL_GROUPS:
        gs = [g_small[k].reshape(raw_w[k].shape) for k in group]
        res = _adamw_whole(gs, [raw_w[k] for k in group], [raw_m[k] for k in group], [raw_v[k] for k in group],
                           name="adamw_" + group[0])
        for k, gk, (dk, mk, vk) in zip(group, gs, res):
            out_g[k], out_d[k], out_m[k], out_v[k] = gk, dk, mk, vk

    loss = lax.psum(loss_part, ("x", "y", "c"))
    return loss, grad_x, out_g, out_d, out_m, out_v


def kernel(x, w_in, conv_w, conv_b, rg_wa, rg_ba, rg_wx, rg_bx, rg_lambda, w_a_out, ssm_a_re, ssm_a_im, ssm_log_dt, ssm_b_re, ssm_b_im, ssm_c_re, ssm_c_im, ssm_d, glu_w, glu_v, w_out, ln1_g, ln1_b, mlp_w_up, mlp_b_up, mlp_w_down, mlp_b_down, ln2_g, ln2_b, loss_target, m_w_in, m_conv_w, m_conv_b, m_rg_wa, m_rg_ba, m_rg_wx, m_rg_bx, m_rg_lambda, m_w_a_out, m_ssm_a_re, m_ssm_a_im, m_ssm_log_dt, m_ssm_b_re, m_ssm_b_im, m_ssm_c_re, m_ssm_c_im, m_ssm_d, m_glu_w, m_glu_v, m_w_out, m_ln1_g, m_ln1_b, m_mlp_w_up, m_mlp_b_up, m_mlp_w_down, m_mlp_b_down, m_ln2_g, m_ln2_b, v_w_in, v_conv_w, v_conv_b, v_rg_wa, v_rg_ba, v_rg_wx, v_rg_bx, v_rg_lambda, v_w_a_out, v_ssm_a_re, v_ssm_a_im, v_ssm_log_dt, v_ssm_b_re, v_ssm_b_im, v_ssm_c_re, v_ssm_c_im, v_ssm_d, v_glu_w, v_glu_v, v_w_out, v_ln1_g, v_ln1_b, v_mlp_w_up, v_mlp_b_up, v_mlp_w_down, v_mlp_b_down, v_ln2_g, v_ln2_b):
    args = locals()
    w = {k: args[k][0] for k in ORDER}
    m = {k: args["m_" + k][0] for k in BIG}
    v = {k: args["v_" + k][0] for k in BIG}
    raw = [{k: args[prefix + k] for k in SMALL} for prefix in ("", "m_", "v_")]
    loss, grad_x, out_g, out_d, out_m, out_v = _step(x[0], loss_target[0], w, m, v, *raw)
    outs = [loss, grad_x[None]]
    for group in (out_g, out_d, out_m, out_v):
        outs += [group[k].reshape(args[k].shape) for k in ORDER]
    return tuple(outs)
```

```python
import functools
import math

import jax
import jax.numpy as jnp
from jax import lax
from jax.experimental import pallas as pl
from jax.experimental.pallas import tpu as pltpu

F32 = jnp.float32
BF16 = jnp.bfloat16
MESH = pl.DeviceIdType.MESH
N_DEV = 8
SUBLANES = 8
LANES = 128
VMEM_BYTES_V7X = 64 * 2 ** 20
VMEM_CAP = VMEM_BYTES_V7X - 8 * 2 ** 20

ALPHA = 2.0 ** 0.25
LN_EPS = 1e-5
RG_C = 8.0
ADAM_LR, ADAM_B1, ADAM_B2, ADAM_EPS, ADAM_WD, ADAM_STEP = 0.001, 0.9, 0.999, 1e-08, 0.01, 10
GELU_C = math.sqrt(2.0 / math.pi)
GELU_K = 0.044715

ANY = pl.BlockSpec(memory_space=pl.ANY)


def _params(sem, vmem_bytes):
    limit = int(min(max(2 * vmem_bytes, 16 * 2 ** 20), VMEM_CAP))
    return pltpu.CompilerParams(dimension_semantics=sem, vmem_limit_bytes=limit)


def _sig(x):
    return 1.0 / (1.0 + jnp.exp(-x))


def _gelu(x):
    return 0.5 * x * (1.0 + jnp.tanh(GELU_C * (x + GELU_K * x * x * x)))


def _dgelu(x):
    th = jnp.tanh(GELU_C * (x + GELU_K * x * x * x))
    return 0.5 * (1.0 + th) + 0.5 * x * (1.0 - th * th) * (GELU_C * (1.0 + 3.0 * GELU_K * x * x))


def _one_minus_exp(x, exp_half_x):
    p = x * (1.0 + x * (1 / 2 + x * (1 / 6 + x * (1 / 24 + x * (1 / 120)))))
    return jnp.where(x > -1 / 16, -p, 1.0 - exp_half_x * exp_half_x)


def _accumulate(ref, val, first):
    @pl.when(first)
    def _():
        ref[...] = val

    @pl.when(jnp.logical_not(first))
    def _():
        ref[...] += val


def _rows8(cw):
    return lax.broadcasted_iota(jnp.int32, (SUBLANES, cw), 0)


def _shift_down(cur, prev, s, rows):
    return jnp.where(rows < s, pltpu.roll(prev, s, 0), pltpu.roll(cur, s, 0))


def _shift_up(cur, nxt, s, rows):
    return jnp.where(rows < SUBLANES - s, pltpu.roll(cur, SUBLANES - s, 0), pltpu.roll(nxt, SUBLANES - s, 0))


def _mm(a, b, *, M, N, K, ta=False, tb=False, b_split=1, n_split=1, a_fn=None, extras=(), epilogue=None,
        n_out=1, n_cs=0, out_dtypes=None, tm=512, tn=512, tk=512, after=None, into=None, name):
    tm, tn, tk = min(tm, M), min(tn, N), min(tk, K)
    assert M % tm == 0 and N % tn == 0 and K % tk == 0, (name, M, N, K, tm, tn, tk)
    nk = K // tk
    grid = (N // tn, M // tm, nk)
    a_spec = pl.BlockSpec((tk, tm), lambda j, i, k: (k, i)) if ta else pl.BlockSpec((tm, tk), lambda j, i, k: (i, k))
    if b_split == 1:
        b_spec = pl.BlockSpec((tn, tk), lambda j, i, k: (j, k)) if tb else pl.BlockSpec((tk, tn), lambda j, i, k: (k, j))
    elif tb:
        kb = (K // b_split) // tk
        assert kb * tk * b_split == K, name
        b_spec = pl.BlockSpec((None, tn, tk), lambda j, i, k: (k // kb, j, k % kb))
    else:
        nb = (N // b_split) // tn
        assert nb * tn * b_split == N, name
        b_spec = pl.BlockSpec((None, tk, tn), lambda j, i, k: (j // nb, k, j % nb))
    in_specs = [a_spec, b_spec]
    for arr, kind, *col_off in extras:
        off = col_off[0] if col_off else 0
        in_specs.append(pl.BlockSpec((tm, tn), lambda j, i, k, off=off: (i, off + j)) if kind == "mn"
                        else pl.BlockSpec((1, tn), lambda j, i, k: (0, j)))
    out_dtypes = (F32,) * n_out if out_dtypes is None else out_dtypes
    if n_split == 1:
        out_shape = [jax.ShapeDtypeStruct((M, N), dt) for dt in out_dtypes]
        out_specs = [pl.BlockSpec((tm, tn), lambda j, i, k: (i, j)) for _ in range(n_out)]
    else:
        assert n_out == 1
        nbo = (N // n_split) // tn
        assert nbo * tn * n_split == N, name
        out_shape = [jax.ShapeDtypeStruct((n_split, M, N // n_split), out_dtypes[0])]
        out_specs = [pl.BlockSpec((None, tm, tn), lambda j, i, k: (j // nbo, i, j % nbo))]
    out_shape += [jax.ShapeDtypeStruct((1, N), F32) for _ in range(n_cs)]
    out_specs += [pl.BlockSpec((1, tn), lambda j, i, k: (0, j)) for _ in range(n_cs)]
    ne = len(extras)
    dims = (((0 if ta else 1,), (1 if tb else 0,)), ((), ()))

    n_after = 0 if after is None else 1
    in_specs += [ANY] * n_after
    aliases, tail = {}, [] if after is None else [after]
    if into is not None:
        buf, which, col_blk = into
        assert n_split == 1 and buf.shape[0] == M and buf.dtype == out_dtypes[which], name
        aliases = {len(in_specs): which}
        in_specs.append(ANY)
        tail.append(buf)
        out_shape[which] = jax.ShapeDtypeStruct(buf.shape, buf.dtype)
        out_specs[which] = pl.BlockSpec((tm, tn), lambda j, i, k: (i, col_blk + j))

    def body(*refs):
        a_ref, b_ref = refs[0], refs[1]
        ex_refs = refs[2:2 + ne]
        first_out = 2 + ne + len(tail)
        out_refs = refs[first_out:first_out + n_out]
        cs_refs = refs[first_out + n_out:first_out + n_out + n_cs]
        i, k = pl.program_id(1), pl.program_id(2)

        def product():
            av = a_ref[...]
            if a_fn is not None:
                av = a_fn(av.astype(F32))
            return lax.dot_general(av.astype(BF16), b_ref[...].astype(BF16), dims, preferred_element_type=F32)

        def finish(acc):
            res = (acc,) if epilogue is None else epilogue(acc, *[r[...] for r in ex_refs])
            for r, o in zip(out_refs, res[:n_out]):
                r[...] = o.astype(r.dtype)
            for r, cval in zip(cs_refs, res[n_out:]):
                _accumulate(r, jnp.sum(cval, axis=0, keepdims=True), i == 0)

        if nk == 1:
            finish(product())
            return
        acc_ref = refs[-1]

        @pl.when(k == 0)
        def _():
            acc_ref[...] = jnp.zeros_like(acc_ref)

        acc_ref[...] += product()

        @pl.when(k == nk - 1)
        def _():
            finish(acc_ref[...])

    vmem = 2 * tm * tk * a.dtype.itemsize + 2 * tk * tn * b.dtype.itemsize + (1 + 2 * n_out + 2 * ne + 2) * tm * tn * 4
    outs = pl.pallas_call(
        body, name=name, grid=grid, in_specs=in_specs, out_specs=out_specs, out_shape=out_shape,
        scratch_shapes=[pltpu.VMEM((tm, tn), F32)] if nk > 1 else [], input_output_aliases=aliases,
        compiler_params=_params(("parallel", "arbitrary", "arbitrary"), vmem),
    )(a, b, *[e[0] for e in extras], *tail)
    return outs[0] if len(outs) == 1 else outs


BD_STEP = 4

def _bd(pairs, *, T, J, kb, nb, tw=False, extras=(), epilogue=None, n_out=1, n_cs=0, out_dtypes=None, tm=512, name):
    jb = BD_STEP
    assert T % tm == 0 and J % jb == 0
    grid = (J // jb, T // tm)
    npair, ne = len(pairs), len(extras)
    in_specs, args = [], []
    for arr, off, w in pairs:
        assert off % jb == 0, name
        in_specs.append(pl.BlockSpec((tm, jb * kb), lambda j, i, off=off // jb: (i, off + j)))
        in_specs.append(pl.BlockSpec((jb,) + tuple(w.shape[1:]), lambda j, i: (j, 0, 0)))
        args += [arr, w]
    for arr, kind, off in extras:
        assert off % jb == 0, name
        in_specs.append(pl.BlockSpec((tm, jb * nb), lambda j, i, off=off // jb: (i, off + j)) if kind == "tile"
                        else pl.BlockSpec((1, jb * nb), lambda j, i, off=off // jb: (0, off + j)))
        args.append(arr)
    out_dtypes = (F32,) * n_out if out_dtypes is None else out_dtypes
    out_shape = [jax.ShapeDtypeStruct((T, J * nb), dt) for dt in out_dtypes]
    out_specs = [pl.BlockSpec((tm, jb * nb), lambda j, i: (i, j)) for _ in range(n_out)]
    out_shape += [jax.ShapeDtypeStruct((1, J * nb), F32) for _ in range(n_cs)]
    out_specs += [pl.BlockSpec((1, jb * nb), lambda j, i: (0, j)) for _ in range(n_cs)]
    dims = (((1,), (1 if tw else 0,)), ((), ()))

    def body(*refs):
        ex_refs = refs[2 * npair:2 * npair + ne]
        out_refs = refs[2 * npair + ne:2 * npair + ne + n_out]
        cs_refs = refs[2 * npair + ne + n_out:]
        i = pl.program_id(1)
        for s in range(jb):
            cols_in, cols_out = pl.ds(s * kb, kb), pl.ds(s * nb, nb)
            acc = None
            for p in range(npair):
                d = lax.dot_general(refs[2 * p][:, cols_in].astype(BF16), refs[2 * p + 1][s].astype(BF16), dims,
                                    preferred_element_type=F32)
                acc = d if acc is None else acc + d
            res = (acc,) if epilogue is None else epilogue(acc, *[r[:, cols_out] for r in ex_refs])
            for r, o in zip(out_refs, res[:n_out]):
                r[:, cols_out] = o.astype(r.dtype)
            for r, cval in zip(cs_refs, res[n_out:]):
                _accumulate(r.at[:, cols_out], jnp.sum(cval, axis=0, keepdims=True), i == 0)

    vmem = jb * (2 * npair * tm * kb + 2 * npair * kb * nb + (2 * n_out + 2 * ne + 3) * tm * nb) * 4
    outs = pl.pallas_call(
        body, name=name, grid=grid, in_specs=in_specs, out_specs=out_specs, out_shape=out_shape,
        compiler_params=_params(("parallel", "arbitrary"), vmem),
    )(*args)
    return outs[0] if len(outs) == 1 else outs


def _bdw(a, a_off, b, b_off, *, T, J, kb, nb, tm=512, name):
    jb = BD_STEP
    assert T % tm == 0 and J % jb == 0 and a_off % jb == 0 and b_off % jb == 0
    a_blk, b_blk = a_off // jb, b_off // jb

    def body(a_ref, b_ref, o_ref):
        i = pl.program_id(1)
        for s in range(jb):
            d = lax.dot_general(a_ref[:, pl.ds(s * kb, kb)].astype(BF16), b_ref[:, pl.ds(s * nb, nb)].astype(BF16),
                                (((0,), (0,)), ((), ())), preferred_element_type=F32)
            _accumulate(o_ref.at[s], d, i == 0)

    return pl.pallas_call(
        body, name=name, grid=(J // jb, T // tm),
        in_specs=[pl.BlockSpec((tm, jb * kb), lambda j, i: (i, a_blk + j)),
                  pl.BlockSpec((tm, jb * nb), lambda j, i: (i, b_blk + j))],
        out_specs=pl.BlockSpec((jb, kb, nb), lambda j, i: (j, 0, 0)),
        out_shape=jax.ShapeDtypeStruct((J, kb, nb), F32),
        compiler_params=_params(("parallel", "arbitrary"), jb * (2 * tm * (kb + nb) + 3 * kb * nb) * 4),
    )(a, b)


def _bd_pack(w, q):
    g, a, b = w.shape
    eye = jnp.eye(q, dtype=w.dtype)
    return jnp.einsum("jqab,qr->jqarb", w.reshape(g // q, q, a, b), eye).reshape(g // q, q * a, q * b)


def _bd_unpack(wp, q):
    j, qa, qb = wp.shape
    a, b = qa // q, qb // q
    w5 = wp.reshape(j, q, a, q, b)
    return jnp.stack([w5[:, r, :, r, :] for r in range(q)], axis=1).reshape(j * q, a, b)


def _ew(fn, ins, *, T, C, n_out, n_cs=0, out_dtypes=None, tm=256, cw=None, name):
    cw = C if cw is None else cw
    assert T % tm == 0 and C % cw == 0
    grid = (C // cw, T // tm)
    in_specs = []
    for arr, kind, off in ins:
        in_specs.append(pl.BlockSpec((tm, cw), lambda j, i, off=off: (i, off + j)) if kind == "tile"
                        else pl.BlockSpec((arr.shape[0], cw), lambda j, i, off=off: (0, off + j)))
    out_dtypes = (F32,) * n_out if out_dtypes is None else out_dtypes
    out_shape = [jax.ShapeDtypeStruct((T, C), dt) for dt in out_dtypes]
    out_specs = [pl.BlockSpec((tm, cw), lambda j, i: (i, j)) for _ in range(n_out)]
    out_shape += [jax.ShapeDtypeStruct((1, C), F32) for _ in range(n_cs)]
    out_specs += [pl.BlockSpec((1, cw), lambda j, i: (0, j)) for _ in range(n_cs)]
    nin = len(ins)

    def body(*refs):
        i = pl.program_id(1)
        res = fn(*[r[...].astype(F32) for r in refs[:nin]])
        for r, o in zip(refs[nin:nin + n_out], res[:n_out]):
            r[...] = o.astype(r.dtype)
        for r, cval in zip(refs[nin + n_out:], res[n_out:]):
            _accumulate(r, jnp.sum(cval, axis=0, keepdims=True), i == 0)

    vmem = (2 * nin + 2 * n_out + 6) * tm * cw * 4
    outs = pl.pallas_call(
        body, name=name, grid=grid, in_specs=in_specs, out_specs=out_specs, out_shape=out_shape,
        compiler_params=_params(("parallel", "arbitrary"), vmem),
    )(*[arr for arr, _, _ in ins])
    return outs[0] if len(outs) == 1 else outs


def _ln_stats(s):
    mu = jnp.mean(s, axis=-1, keepdims=True)
    d = s - mu
    var = jnp.mean(d * d, axis=-1, keepdims=True)
    rstd = lax.rsqrt(var + LN_EPS)
    return d * rstd, rstd


def _ln_bwd(dy, g, xhat, rstd):
    dxh = dy * g
    m1 = jnp.mean(dxh, axis=-1, keepdims=True)
    m2 = jnp.mean(dxh * xhat, axis=-1, keepdims=True)
    return rstd * (dxh - m1 - xhat * m2)


def _conv_fwd(z, conv_w, conv_b, *, T, C, tm=512, cw=1024, after=None, name):
    ng, hb = tm // SUBLANES, tm // SUBLANES
    n_after = 0 if after is None else 1

    def body(x_ref, halo_ref, w_ref, b_ref, *rest):
        o_ref = rest[-1]
        it = pl.program_id(1)
        rows = _rows8(cw)
        halo = jnp.where(it == 0, 0.0, halo_ref[...])
        w = w_ref[...]
        bias = b_ref[...]

        def group(g, carry):
            off = pl.multiple_of(g * SUBLANES, SUBLANES)
            cur = x_ref[pl.ds(off, SUBLANES), :]
            prev = x_ref[pl.ds(pl.multiple_of(jnp.maximum(off - SUBLANES, 0), SUBLANES), SUBLANES), :]
            prev = jnp.where(g == 0, halo, prev)
            acc = cur * w[3:4] + bias
            for s in (1, 2, 3):
                acc = acc + _shift_down(cur, prev, s, rows) * w[3 - s:4 - s]
            o_ref[pl.ds(off, SUBLANES), :] = acc
            return carry

        lax.fori_loop(0, ng, group, 0, unroll=2)

    return pl.pallas_call(
        body, name=name, grid=(C // cw, T // tm),
        in_specs=[pl.BlockSpec((tm, cw), lambda j, i: (i, j)),
                  pl.BlockSpec((SUBLANES, cw), lambda j, i: (jnp.maximum(i * hb - 1, 0), j)),
                  pl.BlockSpec((4, cw), lambda j, i: (0, j)), pl.BlockSpec((1, cw), lambda j, i: (0, j))]
        + [ANY] * n_after,
        out_specs=pl.BlockSpec((tm, cw), lambda j, i: (i, j)),
        out_shape=jax.ShapeDtypeStruct((T, C), F32),
        compiler_params=_params(("parallel", "arbitrary"), 5 * tm * cw * 4),
    )(z, z, conv_w, conv_b, *([after] if n_after else []))


def _conv_bwd(dxc, z, conv_w, dz, *, T, C, tm=512, cw=512, name):
    ng, hb, last = tm // SUBLANES, tm // SUBLANES, T // SUBLANES - 1
    nt = T // tm
    rows16 = 2 * SUBLANES

    def body(d_ref, dn_ref, x_ref, w_ref, dz_in_ref, o_ref, sums_ref):
        it = pl.program_id(1)
        rows = _rows8(cw)
        dnext = jnp.where(it == nt - 1, 0.0, dn_ref[...])
        w = w_ref[...]

        def pair(q, accs):
            base = pl.multiple_of(q * rows16, rows16)
            halves = []
            for half in range(2):
                g = 2 * q + half
                off = pl.multiple_of(base + half * SUBLANES, SUBLANES)
                dcur = d_ref[pl.ds(off, SUBLANES), :]
                dnx = d_ref[pl.ds(pl.multiple_of(jnp.minimum(off + SUBLANES, tm - SUBLANES), SUBLANES), SUBLANES), :]
                dnx = jnp.where(g == ng - 1, dnext, dnx)
                xcur = x_ref[pl.ds(off, SUBLANES), :]
                acc = dcur * w[3:4]
                taps = [accs[3] + dcur * xcur]
                for s in (1, 2, 3):
                    ahead = _shift_up(dcur, dnx, s, rows)
                    acc = acc + ahead * w[3 - s:4 - s]
                    taps.append(accs[3 - s] + ahead * xcur)
                halves.append(acc)
                accs = (taps[3], taps[2], taps[1], taps[0], accs[4] + dcur)
            o_ref[pl.ds(base, rows16), :] = jnp.concatenate(halves, axis=0).astype(o_ref.dtype)
            return accs

        zero = jnp.zeros((SUBLANES, cw), F32)
        accs = lax.fori_loop(0, ng // 2, pair, (zero,) * 5)
        sums = jnp.zeros((SUBLANES, cw), F32)
        for k, a in enumerate(accs):
            sums = jnp.where(rows == k, jnp.sum(a, axis=0, keepdims=True), sums)
        _accumulate(sums_ref, sums, it == 0)

    tile = pl.BlockSpec((tm, cw), lambda j, i: (i, j))
    return pl.pallas_call(
        body, name=name, grid=(C // cw, nt),
        in_specs=[tile, pl.BlockSpec((SUBLANES, cw), lambda j, i: (jnp.minimum((i + 1) * hb, last), j)),
                  tile, pl.BlockSpec((4, cw), lambda j, i: (0, j)), ANY],
        out_specs=[tile, pl.BlockSpec((SUBLANES, cw), lambda j, i: (0, j))],
        input_output_aliases={4: 0},
        out_shape=[jax.ShapeDtypeStruct(dz.shape, dz.dtype), jax.ShapeDtypeStruct((SUBLANES, C), F32)],
        compiler_params=_params(("parallel", "arbitrary"), 7 * tm * cw * 4),
    )(dxc, dxc, z, conv_w, dz)


def _rg_coeffs(r, ig, xc, sp):
    la = (-RG_C) * r * sp
    a = jnp.exp(la)
    m = jnp.sqrt(_one_minus_exp(2.0 * la, a))
    return a, m, m * (ig * xc)


def _rg_scan_fwd(z, ri, xc, sp, *, T, C, gate_off, tm=512, cw=256, name):
    rows16 = 2 * SUBLANES
    nq = tm // rows16

    def body(gate_ref, r_ref, i_ref, xc_ref, sp_ref, h_ref, p_ref, a_ref, m_ref, carry_ref):
        it = pl.program_id(1)

        @pl.when(it == 0)
        def _():
            carry_ref[...] = jnp.zeros_like(carry_ref)

        rows = _rows8(cw)
        sp_row = sp_ref[...]

        def pair(q, carry):
            base = pl.multiple_of(q * rows16, rows16)
            halves = []
            for half in range(2):
                sl = pl.ds(pl.multiple_of(base + half * SUBLANES, SUBLANES), SUBLANES)
                a, m, b = _rg_coeffs(r_ref[sl, :], i_ref[sl, :], xc_ref[sl, :], sp_row)
                a_ref[sl, :] = a
                m_ref[sl, :] = m
                for s in (1, 2, 4):
                    keep = rows >= s
                    sa = jnp.where(keep, pltpu.roll(a, s, 0), 1.0)
                    sb = jnp.where(keep, pltpu.roll(b, s, 0), 0.0)
                    b = b + a * sb
                    a = a * sa
                h = b + a * carry
                h_ref[sl, :] = h
                halves.append(h * _gelu(gate_ref[sl, :]))
                carry = h[SUBLANES - 1:SUBLANES, :]
            p_ref[pl.ds(base, rows16), :] = jnp.concatenate(halves, axis=0).astype(p_ref.dtype)
            return carry

        last = lax.fori_loop(0, nq, pair, carry_ref[0:1, :], unroll=2)
        carry_ref[...] = jnp.broadcast_to(last, carry_ref.shape)

    tile = pl.BlockSpec((tm, cw), lambda j, i: (i, j))
    gate_blk = gate_off // cw
    return pl.pallas_call(
        body, name=name, grid=(C // cw, T // tm),
        in_specs=[pl.BlockSpec((tm, cw), lambda j, i: (i, gate_blk + j)),
                  pl.BlockSpec((tm, cw), lambda j, i: (i, 2 * j)), pl.BlockSpec((tm, cw), lambda j, i: (i, 2 * j + 1)),
                  tile, pl.BlockSpec((1, cw), lambda j, i: (0, j))],
        out_specs=[tile, tile, tile, tile],
        out_shape=[jax.ShapeDtypeStruct((T, C), F32), jax.ShapeDtypeStruct((T, C), BF16),
                   jax.ShapeDtypeStruct((T, C), F32), jax.ShapeDtypeStruct((T, C), F32)],
        scratch_shapes=[pltpu.VMEM((SUBLANES, cw), F32)],
        compiler_params=_params(("parallel", "arbitrary"), 16 * tm * cw * 4),
    )(z, ri, ri, xc, sp)


def _rg_scan_bwd(dh, h, ri, xc, a_fwd, m_fwd, sp, *, T, C, tm=512, cw=256, name):
    ng, hb, nt = tm // SUBLANES, tm // SUBLANES, T // tm

    def body(dh_ref, h_ref, hp_ref, r_ref, i_ref, xc_ref, a_ref, m_ref, sp_ref,
             drai_ref, dxc_ref, crai_ref, csp_ref, cg_ref, ca_ref):
        step = pl.program_id(1)

        @pl.when(step == 0)
        def _():
            cg_ref[...] = jnp.zeros_like(cg_ref)
            ca_ref[...] = jnp.zeros_like(ca_ref)

        rows = _rows8(cw)
        sp_row = sp_ref[...]
        hhalo = jnp.where(step == nt - 1, 0.0, hp_ref[...])

        def group(gi, carry):
            g_next, a_next, s_ra, s_ia, s_sp = carry
            g = ng - 1 - gi
            off = pl.multiple_of(g * SUBLANES, SUBLANES)
            sl = pl.ds(off, SUBLANES)
            rr, ii, xx = r_ref[sl, :], i_ref[sl, :], xc_ref[sl, :]
            a, m = a_ref[sl, :], m_ref[sl, :]
            hh = h_ref[sl, :]
            hpv = h_ref[pl.ds(pl.multiple_of(jnp.maximum(off - SUBLANES, 0), SUBLANES), SUBLANES), :]
            hpv = jnp.where(g == 0, hhalo, hpv)
            hprev = _shift_down(hh, hpv, 1, rows)
            d = dh_ref[sl, :]
            c = jnp.where(rows < SUBLANES - 1, pltpu.roll(a, SUBLANES - 1, 0), a_next)
            for s in (1, 2, 4):
                keep = rows < SUBLANES - s
                sc = jnp.where(keep, pltpu.roll(c, SUBLANES - s, 0), 1.0)
                sd = jnp.where(keep, pltpu.roll(d, SUBLANES - s, 0), 0.0)
                d = d + c * sd
                c = c * sc
            gg = d + c * g_next
            da = gg * hprev
            dm = gg * (ii * xx)
            di = gg * (m * xx)
            dxc_ref[sl, :] = gg * (m * ii)
            dla = da * a - dm * (a * a / m)
            dra = dla * ((-RG_C) * sp_row) * (rr * (1.0 - rr))
            dia = di * (ii * (1.0 - ii))
            drai_ref[sl, pl.ds(0, cw)] = dra
            drai_ref[sl, pl.ds(cw, cw)] = dia
            return (gg[0:1, :], a[0:1, :], s_ra + dra, s_ia + dia, s_sp + dla * ((-RG_C) * rr))

        zero = jnp.zeros((SUBLANES, cw), F32)
        g_first, a_first, s_ra, s_ia, s_sp = lax.fori_loop(
            0, ng, group, (cg_ref[0:1, :], ca_ref[0:1, :], zero, zero, zero), unroll=2)
        cg_ref[...] = jnp.broadcast_to(g_first, cg_ref.shape)
        ca_ref[...] = jnp.broadcast_to(a_first, ca_ref.shape)
        for ref, acc in ((crai_ref.at[:, pl.ds(0, cw)], s_ra), (crai_ref.at[:, pl.ds(cw, cw)], s_ia), (csp_ref, s_sp)):
            _accumulate(ref, jnp.sum(acc, axis=0, keepdims=True), step == 0)

    tile = pl.BlockSpec((tm, cw), lambda j, i: (nt - 1 - i, j))
    wide = pl.BlockSpec((tm, 2 * cw), lambda j, i: (nt - 1 - i, j))
    vec = pl.BlockSpec((1, cw), lambda j, i: (0, j))
    return pl.pallas_call(
        body, name=name, grid=(C // cw, nt),
        in_specs=[tile, tile, pl.BlockSpec((SUBLANES, cw), lambda j, i: (jnp.maximum((nt - 1 - i) * hb - 1, 0), j)),
                  pl.BlockSpec((tm, cw), lambda j, i: (nt - 1 - i, 2 * j)),
                  pl.BlockSpec((tm, cw), lambda j, i: (nt - 1 - i, 2 * j + 1)), tile, tile, tile, vec],
        out_specs=[wide, tile, pl.BlockSpec((1, 2 * cw), lambda j, i: (0, j)), vec],
        out_shape=[jax.ShapeDtypeStruct((T, 2 * C), F32), jax.ShapeDtypeStruct((T, C), F32),
                   jax.ShapeDtypeStruct((1, 2 * C), F32), jax.ShapeDtypeStruct((1, C), F32)],
        scratch_shapes=[pltpu.VMEM((SUBLANES, cw), F32), pltpu.VMEM((SUBLANES, cw), F32)],
        compiler_params=_params(("parallel", "arbitrary"), 24 * tm * cw * 4),
    )(dh, h, h, ri, ri, xc, a_fwd, m_fwd, sp)


def _cscan_tables(lr, li, reverse):
    lam = (lr.reshape(-1), -li.reshape(-1) if reverse else li.reshape(-1))

    def mul(p, q):
        return p[0] * q[0] - p[1] * q[1], p[0] * q[1] + p[1] * q[0]

    pows = [lam]
    for _ in range(SUBLANES - 1):
        pows.append(mul(pows[-1], lam))
    zero = jnp.zeros_like(lam[0])
    tab = jnp.stack([pows[0][0], pows[0][1], pows[1][0], pows[1][1], pows[3][0], pows[3][1], zero, zero])
    if reverse:
        pows = pows[::-1]
    return tab, jnp.stack([p[0] for p in pows]), jnp.stack([p[1] for p in pows])


def _power_slabs(lr, li, n):
    pr, pi = lr.reshape(1, -1), li.reshape(1, -1)
    while pr.shape[0] < n:
        tr, ti = pr[-1:], pi[-1:]
        pr, pi = (jnp.concatenate([pr, pr * tr - pi * ti], axis=0), jnp.concatenate([pi, pr * ti + pi * tr], axis=0))
    return jnp.repeat(pr, SUBLANES, axis=0), jnp.repeat(pi, SUBLANES, axis=0), pr[-1], pi[-1]


def _rows_to_segments(src_ref, dst_ref):
    seg = src_ref.shape[0] // SUBLANES
    for g in range(seg):
        dst_ref[pl.ds(g * SUBLANES, SUBLANES), :] = src_ref[pl.ds(g, SUBLANES, stride=seg), :].astype(dst_ref.dtype)


def _segments_to_rows(src_ref, dst_ref):
    seg = src_ref.shape[0] // SUBLANES
    for r in range(SUBLANES):
        dst_ref[pl.ds(r * seg, seg), :] = src_ref[pl.ds(r, seg, stride=SUBLANES), :].astype(dst_ref.dtype)


def _seg_scan_tile(xr_ref, xi_ref, pbr_ref, pbi_ref, tab_ref, pwr_ref, pwi_ref, cr_ref, ci_ref, *, reverse, h=None):
    tm, cw = xr_ref.shape
    seg = tm // SUBLANES
    rows = _rows8(cw)
    sign = -1.0 if reverse else 1.0
    l_re, l_im = pbr_ref[0:1, :], sign * pbi_ref[0:1, :]

    def slab(g):
        return pl.ds(pl.multiple_of(g * SUBLANES, SUBLANES), SUBLANES)

    def local(k, state):
        sl = slab(seg - 1 - k if reverse else k)
        sr, si = state
        nr = xr_ref[sl, :] + (l_re * sr - l_im * si)
        ni = xi_ref[sl, :] + (l_re * si + l_im * sr)
        xr_ref[sl, :] = nr
        xi_ref[sl, :] = ni
        return nr, ni

    zero = jnp.zeros((SUBLANES, cw), F32)
    er, ei = lax.fori_loop(0, seg, local, (zero, zero), unroll=2)

    for k, s in enumerate((1, 2, 4)):
        shift = SUBLANES - s if reverse else s
        keep = rows < SUBLANES - s if reverse else rows >= s
        sr = jnp.where(keep, pltpu.roll(er, shift, 0), 0.0)
        si = jnp.where(keep, pltpu.roll(ei, shift, 0), 0.0)
        m_re, m_im = tab_ref[2 * k:2 * k + 1, :], tab_ref[2 * k + 1:2 * k + 2, :]
        er, ei = er + (m_re * sr - m_im * si), ei + (m_re * si + m_im * sr)
    cin_r, cin_i = cr_ref[0:1, :], ci_ref[0:1, :]
    pwr, pwi = pwr_ref[...], pwi_ref[...]
    er, ei = er + (pwr * cin_r - pwi * cin_i), ei + (pwr * cin_i + pwi * cin_r)
    if reverse:
        ent_r = jnp.where(rows == SUBLANES - 1, cin_r, pltpu.roll(er, SUBLANES - 1, 0))
        ent_i = jnp.where(rows == SUBLANES - 1, cin_i, pltpu.roll(ei, SUBLANES - 1, 0))
        out_r, out_i = er[0:1, :], ei[0:1, :]
    else:
        ent_r = jnp.where(rows == 0, cin_r, pltpu.roll(er, 1, 0))
        ent_i = jnp.where(rows == 0, cin_i, pltpu.roll(ei, 1, 0))
        out_r, out_i = er[SUBLANES - 1:SUBLANES, :], ei[SUBLANES - 1:SUBLANES, :]
    cr_ref[...] = jnp.broadcast_to(out_r, cr_ref.shape)
    ci_ref[...] = jnp.broadcast_to(out_i, ci_ref.shape)

    if h is not None:
        hr_ref, hi_ref, hr_last, hi_last = h
        hr_wrap = _shift_down(hr_ref[pl.ds(tm - SUBLANES, SUBLANES), :], hr_last, 1, rows)
        hi_wrap = _shift_down(hi_ref[pl.ds(tm - SUBLANES, SUBLANES), :], hi_last, 1, rows)

    def fix(g, sums):
        sl = slab(g)
        power = slab(seg - 1 - g) if reverse else sl
        pr, pi = pbr_ref[power, :], sign * pbi_ref[power, :]
        nr = xr_ref[sl, :] + (pr * ent_r - pi * ent_i)
        ni = xi_ref[sl, :] + (pr * ent_i + pi * ent_r)
        xr_ref[sl, :] = nr
        xi_ref[sl, :] = ni
        if h is None:
            return sums
        before = slab(jnp.maximum(g - 1, 0))
        hr1 = jnp.where(g == 0, hr_wrap, hr_ref[before, :])
        hi1 = jnp.where(g == 0, hi_wrap, hi_ref[before, :])
        return sums[0] + (nr * hr1 + ni * hi1), sums[1] + (ni * hr1 - nr * hi1)

    return lax.fori_loop(0, seg, fix, (zero, zero) if h is not None else (), unroll=2)


S5_TILE = 512


def _s5_fwd(z, u_off, wb_re, wb_im, wc_re, wc_im_neg, d_row, powers, *, T, tm=S5_TILE, name):
    J, ku, kp = wb_re.shape
    nt = T // tm
    pb_re, pb_im, top_re, top_im = powers
    tab, pw_re, pw_im = _cscan_tables(top_re, top_im, False)
    u_blk = u_off // ku

    def body(u_ref, wbr_ref, wbi_ref, wcr_ref, wci_ref, d_ref, pbr_ref, pbi_ref, tab_ref, pwr_ref, pwi_ref,
             hr_ref, hi_ref, y_ref, yg_ref, cr_ref, ci_ref, us_ref, ys_ref):
        @pl.when(pl.program_id(1) == 0)
        def _():
            cr_ref[...] = jnp.zeros_like(cr_ref)
            ci_ref[...] = jnp.zeros_like(ci_ref)

        _rows_to_segments(u_ref, us_ref)
        u = us_ref[...]
        ub = u.astype(BF16)
        hr_ref[...] = jnp.dot(ub, wbr_ref[...], preferred_element_type=F32)
        hi_ref[...] = jnp.dot(ub, wbi_ref[...], preferred_element_type=F32)
        _seg_scan_tile(hr_ref, hi_ref, pbr_ref, pbi_ref, tab_ref, pwr_ref, pwi_ref, cr_ref, ci_ref, reverse=False)
        y = (jnp.dot(hr_ref[...].astype(BF16), wcr_ref[...], preferred_element_type=F32)
             + jnp.dot(hi_ref[...].astype(BF16), wci_ref[...], preferred_element_type=F32) + d_ref[...] * u)
        ys_ref[...] = y
        _segments_to_rows(ys_ref, y_ref)
        ys_ref[...] = _gelu(y)
        _segments_to_rows(ys_ref, yg_ref)

    wb_spec = pl.BlockSpec((None, ku, kp), lambda j, i: (j, 0, 0))
    wc_spec = pl.BlockSpec((None, kp, ku), lambda j, i: (j, 0, 0))
    small = pl.BlockSpec((SUBLANES, kp), lambda j, i: (0, j))
    slabs = pl.BlockSpec((tm, kp), lambda j, i: (0, j))
    state = pl.BlockSpec((tm, kp), lambda j, i: (i, j))
    chan = pl.BlockSpec((tm, ku), lambda j, i: (i, j))
    return pl.pallas_call(
        body, name=name, grid=(J, nt),
        in_specs=[pl.BlockSpec((tm, ku), lambda j, i: (i, u_blk + j)), wb_spec, wb_spec, wc_spec, wc_spec,
                  pl.BlockSpec((1, ku), lambda j, i: (0, j)), slabs, slabs, small, small, small],
        out_specs=[state, state, chan, chan],
        out_shape=[jax.ShapeDtypeStruct((T, J * kp), F32)] * 2
        + [jax.ShapeDtypeStruct((T, J * ku), F32), jax.ShapeDtypeStruct((T, J * ku), BF16)],
        scratch_shapes=[pltpu.VMEM((SUBLANES, kp), F32), pltpu.VMEM((SUBLANES, kp), F32),
                        pltpu.VMEM((tm, ku), F32), pltpu.VMEM((tm, ku), F32)],
        compiler_params=_params(("parallel", "arbitrary"), 14 * tm * kp * 4),
    )(z, wb_re, wb_im, wc_re, wc_im_neg, d_row, pb_re, pb_im, tab, pw_re, pw_im)


def _s5_bwd(dy, z, u_off, h_re, h_im, wb_re, wb_im, wc_re, wc_im_neg, d_row, powers, dz, *, T, tm=S5_TILE, name):
    J, ku, kp = wb_re.shape
    nt, hb = T // tm, tm // SUBLANES
    pb_re, pb_im, top_re, top_im = powers
    tab, pw_re, pw_im = _cscan_tables(top_re, top_im, True)
    u_blk = u_off // ku
    contract_rows = (((0,), (0,)), ((), ()))
    contract_cols = (((1,), (1,)), ((), ()))

    def body(dy_ref, u_ref, hr_ref, hrp_ref, hi_ref, hip_ref, wbr_ref, wbi_ref, wcr_ref, wci_ref, d_ref,
             pbr_ref, pbi_ref, tab_ref, pwr_ref, pwi_ref, dz_in_ref,
             du_ref, dlr_ref, dli_ref, dd_ref, dwbr_ref, dwbi_ref, dwcr_ref, dwci_ref,
             gr_ref, gi_ref, cr_ref, ci_ref, dys_ref, us_ref):
        step = pl.program_id(1)
        first = step == 0

        @pl.when(first)
        def _():
            cr_ref[...] = jnp.zeros_like(cr_ref)
            ci_ref[...] = jnp.zeros_like(ci_ref)

        _rows_to_segments(dy_ref, dys_ref)
        _rows_to_segments(u_ref, us_ref)
        dy_t, u = dys_ref[...], us_ref[...]
        dyb, ub = dy_t.astype(BF16), u.astype(BF16)
        gr_ref[...] = lax.dot_general(dyb, wcr_ref[...], contract_cols, preferred_element_type=F32)
        gi_ref[...] = lax.dot_general(dyb, wci_ref[...], contract_cols, preferred_element_type=F32)
        hr_last = jnp.where(step == nt - 1, 0.0, hrp_ref[...])
        hi_last = jnp.where(step == nt - 1, 0.0, hip_ref[...])
        s_re, s_im = _seg_scan_tile(gr_ref, gi_ref, pbr_ref, pbi_ref, tab_ref, pwr_ref, pwi_ref, cr_ref, ci_ref,
                                    reverse=True, h=(hr_ref, hi_ref, hr_last, hi_last))
        _accumulate(dlr_ref, jnp.sum(s_re, axis=0, keepdims=True), first)
        _accumulate(dli_ref, jnp.sum(s_im, axis=0, keepdims=True), first)
        grb, gib = gr_ref[...].astype(BF16), gi_ref[...].astype(BF16)
        du = (lax.dot_general(grb, wbr_ref[...], contract_cols, preferred_element_type=F32)
              + lax.dot_general(gib, wbi_ref[...], contract_cols, preferred_element_type=F32) + dy_t * d_ref[...])
        dys_ref[...] = du
        _segments_to_rows(dys_ref, du_ref)
        _accumulate(dd_ref, jnp.sum(dy_t * u, axis=0, keepdims=True), first)
        _accumulate(dwbr_ref, lax.dot_general(ub, grb, contract_rows, preferred_element_type=F32), first)
        _accumulate(dwbi_ref, lax.dot_general(ub, gib, contract_rows, preferred_element_type=F32), first)
        _accumulate(dwcr_ref, lax.dot_general(dyb, hr_ref[...].astype(BF16), contract_rows,
                                              preferred_element_type=F32), first)
        _accumulate(dwci_ref, lax.dot_general(dyb, hi_ref[...].astype(BF16), contract_rows,
                                              preferred_element_type=F32), first)

    def tix(i):
        return nt - 1 - i

    wb_spec = pl.BlockSpec((None, ku, kp), lambda j, i: (j, 0, 0))
    wc_spec = pl.BlockSpec((None, kp, ku), lambda j, i: (j, 0, 0))
    small = pl.BlockSpec((SUBLANES, kp), lambda j, i: (0, j))
    state = pl.BlockSpec((tm, kp), lambda j, i: (tix(i), j))
    halo = pl.BlockSpec((SUBLANES, kp), lambda j, i: (jnp.maximum(tix(i) * hb - 1, 0), j))
    chan = pl.BlockSpec((tm, ku), lambda j, i: (tix(i), j))
    svec = pl.BlockSpec((1, kp), lambda j, i: (0, j))
    cvec = pl.BlockSpec((1, ku), lambda j, i: (0, j))
    slabs = pl.BlockSpec((tm, kp), lambda j, i: (0, j))
    return pl.pallas_call(
        body, name=name, grid=(J, nt),
        in_specs=[chan, pl.BlockSpec((tm, ku), lambda j, i: (tix(i), u_blk + j)), state, halo, state, halo,
                  wb_spec, wb_spec, wc_spec, wc_spec, cvec, slabs, slabs, small, small, small, ANY],
        out_specs=[pl.BlockSpec((tm, ku), lambda j, i: (tix(i), u_blk + j)), svec, svec, cvec,
                   wb_spec, wb_spec, wb_spec, wb_spec],
        input_output_aliases={16: 0},
        out_shape=[jax.ShapeDtypeStruct(dz.shape, dz.dtype), jax.ShapeDtypeStruct((1, J * kp), F32),
                   jax.ShapeDtypeStruct((1, J * kp), F32), jax.ShapeDtypeStruct((1, J * ku), F32),
                   jax.ShapeDtypeStruct((J, ku, kp), F32), jax.ShapeDtypeStruct((J, ku, kp), F32),
                   jax.ShapeDtypeStruct((J, ku, kp), F32), jax.ShapeDtypeStruct((J, ku, kp), F32)],
        scratch_shapes=[pltpu.VMEM((tm, kp), F32), pltpu.VMEM((tm, kp), F32),
                        pltpu.VMEM((SUBLANES, kp), F32), pltpu.VMEM((SUBLANES, kp), F32),
                        pltpu.VMEM((tm, ku), F32), pltpu.VMEM((tm, ku), F32)],
        compiler_params=_params(("parallel", "arbitrary"), 16 * tm * kp * 4),
    )(dy, z, h_re, h_re, h_im, h_im, wb_re, wb_im, wc_re, wc_im_neg, d_row, pb_re, pb_im, tab, pw_re, pw_im, dz)


def _mesh_pos():
    return lax.axis_index("x"), lax.axis_index("y"), lax.axis_index("c")


def _dev_index(px, py, pc):
    return 4 * px + 2 * py + pc


HBM = pl.BlockSpec(memory_space=pltpu.HBM)
SEM = pl.BlockSpec(memory_space=pltpu.SEMAPHORE)
EFFECT = pltpu.SideEffectType.DATAFLOW_SIDE_EFFECTING
RELATIONS = [(dx, dy, dc) for dx in (0, 1) for dy in (0, 1) for dc in (0, 1) if (dx, dy, dc) != (0, 0, 0)]


def _peer(rel):
    x, y, c = _mesh_pos()
    dx, dy, dc = rel
    return (x + dx - 2 * x * dx, y + dy - 2 * y * dy, c + dc - 2 * c * dc)


CHIP_RELATIONS = [(1, 0, 0), (0, 1, 0), (1, 1, 0)]
EXCHANGE_PEERS = {"gather": RELATIONS, "scatter": RELATIONS, "own": [(0, 0, 1)] + CHIP_RELATIONS, "pass": CHIP_RELATIONS}


def _split_copy(src_ref, land_ref, send_sems, recv_sems, k, mode, incoming):
    x, y, c = _mesh_pos()
    me = _dev_index(x, y, c)
    peer = _peer(EXCHANGE_PEERS[mode][k])
    if mode == "pass":
        held, theirs = _dev_index(peer[0], peer[1], c), _dev_index(peer[0], peer[1], 1 - c)
        src, slot, target = land_ref.at[held], theirs if incoming else held, (x, y, 1 - c)
    else:
        src = src_ref.at[_dev_index(*peer)] if mode == "scatter" else src_ref
        slot, target = _dev_index(*peer) if incoming else me, peer
    return pltpu.make_async_remote_copy(src_ref=src, dst_ref=land_ref.at[slot], send_sem=send_sems.at[k],
                                        recv_sem=recv_sems.at[k], device_id=target, device_id_type=MESH)


def _exchange_start(srcs, lands, *, mode, after=None, name):
    n = len(srcs)
    n_after = 0 if after is None else 1
    n_rel = len(EXCHANGE_PEERS[mode])

    def body(*refs):
        src_refs, land_refs = refs[:n], refs[n:2 * n]
        first_out = 2 * n + n_after
        send, recv = refs[first_out:first_out + n], refs[first_out + n:first_out + 2 * n]
        token = refs[-1]
        for k in range(n_rel):
            for a in range(n):
                _split_copy(src_refs[a], land_refs[a], send[a], recv[a], k, mode, incoming=False).start()
        token[...] = jnp.zeros_like(token)

    outs = pl.pallas_call(
        body, name=name, in_specs=[HBM] * (2 * n) + [ANY] * n_after,
        out_shape=[pltpu.SemaphoreType.DMA((n_rel,))] * (2 * n)
        + [pltpu.HBM(s.shape, s.dtype) for s in srcs] + [pltpu.HBM(s.shape, s.dtype) for s in lands]
        + [jax.ShapeDtypeStruct((SUBLANES, LANES), F32)],
        out_specs=[SEM] * (2 * n) + [HBM] * (2 * n) + [pl.BlockSpec(memory_space=pltpu.VMEM)],
        input_output_aliases={**{a: 2 * n + a for a in range(n)}, **{n + a: 3 * n + a for a in range(n)}},
        compiler_params=pltpu.CompilerParams(has_side_effects=EFFECT),
    )(*[pltpu.with_memory_space_constraint(s, pltpu.HBM) for s in srcs],
      *[pltpu.with_memory_space_constraint(s, pltpu.HBM) for s in lands], *([after] if n_after else []))
    per_array = [(outs[a], outs[n + a], outs[2 * n + a], outs[3 * n + a]) for a in range(n)]
    return per_array, outs[-1]


def _exchange_wait(handle, after, *, mode, name):
    send_sems, recv_sems, src_thru, land_thru = handle
    after = after if isinstance(after, (tuple, list)) else (after,)

    def body(src_ref, land_ref, send, recv, *rest):
        for k in range(len(EXCHANGE_PEERS[mode])):
            cp = _split_copy(src_ref, land_ref, send, recv, k, mode, incoming=True)
            cp.wait_send()
            cp.wait_recv()

    return pl.pallas_call(
        body, name=name, in_specs=[HBM, HBM, SEM, SEM] + [ANY] * len(after),
        out_shape=[pltpu.HBM(src_thru.shape, src_thru.dtype), pltpu.HBM(land_thru.shape, land_thru.dtype)],
        out_specs=[HBM, HBM], input_output_aliases={0: 0, 1: 1},
        compiler_params=pltpu.CompilerParams(has_side_effects=EFFECT),
    )(src_thru, land_thru, send_sems, recv_sems, *after)[1]


def _landing_zone(own_block):
    me = _dev_index(*_mesh_pos())
    zone = lax.empty((N_DEV,) + own_block.shape, own_block.dtype)
    return lax.dynamic_update_index_in_dim(zone, own_block, me, 0)


def _row_tile(rows, want):
    t = min(want, rows) // SUBLANES * SUBLANES
    while rows % t:
        t -= SUBLANES
    return t


def _sum_slots(recv, *, tr, name):
    s_, r_, c_ = recv.shape
    tr = _row_tile(r_, tr)

    def body(g_ref, o_ref):
        acc = g_ref[0]
        for s in range(1, s_):
            acc = acc + g_ref[s]
        o_ref[...] = acc

    return pl.pallas_call(
        body, name=name, grid=(r_ // tr,),
        in_specs=[pl.BlockSpec((s_, tr, c_), lambda i: (0, i, 0))],
        out_specs=pl.BlockSpec((tr, c_), lambda i: (i, 0)),
        out_shape=jax.ShapeDtypeStruct((r_, c_), F32),
        compiler_params=_params(("parallel",), (2 * s_ + 3) * tr * c_ * 4),
    )(recv)


def _adamw(recv, w, m, v, *, tr, name):
    s_, r_, c_ = recv.shape
    tr = _row_tile(r_, tr)
    assert w.shape == (r_, c_), (name, w.shape, recv.shape)
    c1 = 1.0 - ADAM_B1 ** ADAM_STEP
    c2 = 1.0 - ADAM_B2 ** ADAM_STEP

    def body(g_ref, w_ref, m_ref, v_ref, go_ref, d_ref, mo_ref, vo_ref):
        g = g_ref[0].astype(F32)
        for s in range(1, s_):
            g = g + g_ref[s].astype(F32)
        mn = ADAM_B1 * m_ref[...] + (1.0 - ADAM_B1) * g
        vn = ADAM_B2 * v_ref[...] + (1.0 - ADAM_B2) * (g * g)
        go_ref[...] = g
        mo_ref[...] = mn
        vo_ref[...] = vn
        d_ref[...] = -ADAM_LR * ((mn / c1) / (jnp.sqrt(vn / c2) + ADAM_EPS) + ADAM_WD * w_ref[...])

    tile = pl.BlockSpec((tr, c_), lambda i: (i, 0))
    return pl.pallas_call(
        body, name=name, grid=(r_ // tr,),
        in_specs=[pl.BlockSpec((s_, tr, c_), lambda i: (0, i, 0)), tile, tile, tile],
        out_specs=[tile] * 4, out_shape=[jax.ShapeDtypeStruct((r_, c_), F32)] * 4,
        compiler_params=_params(("parallel",), (2 * s_ + 16) * tr * c_ * 4),
    )(recv, w, m, v)


def _adamw_whole(gs, ws, ms, vs, *, name):
    n = len(gs)
    c1 = 1.0 - ADAM_B1 ** ADAM_STEP
    c2 = 1.0 - ADAM_B2 ** ADAM_STEP

    def body(*refs):
        for i in range(n):
            g, w = refs[i][...], refs[n + i][...]
            mn = ADAM_B1 * refs[2 * n + i][...] + (1.0 - ADAM_B1) * g
            vn = ADAM_B2 * refs[3 * n + i][...] + (1.0 - ADAM_B2) * (g * g)
            refs[4 * n + 3 * i][...] = -ADAM_LR * ((mn / c1) / (jnp.sqrt(vn / c2) + ADAM_EPS) + ADAM_WD * w)
            refs[4 * n + 3 * i + 1][...] = mn
            refs[4 * n + 3 * i + 2][...] = vn

    whole = pl.BlockSpec(memory_space=pltpu.VMEM)
    lane_padded = sum(math.prod(g.shape[:-1]) * (-(-g.shape[-1] // LANES) * LANES) for g in gs)
    outs = pl.pallas_call(
        body, name=name, in_specs=[whole] * (4 * n), out_specs=[whole] * (3 * n),
        out_shape=[jax.ShapeDtypeStruct(g.shape, F32) for g in gs for _ in range(3)],
        compiler_params=pltpu.CompilerParams(vmem_limit_bytes=int(min(max(16 * lane_padded * 4, 16 * 2 ** 20), VMEM_CAP))),
    )(*gs, *ws, *ms, *vs)
    return [tuple(outs[3 * i:3 * i + 3]) for i in range(n)]


def _s5_discretise(a_re, a_im, log_dt, b_re, b_im):
    dt = jnp.exp(log_dt)[:, None]
    lr = jnp.minimum(a_re, -1e-4)
    li = a_im
    mag = jnp.exp(lr * dt)
    lbr = mag * jnp.cos(li * dt)
    lbi = mag * jnp.sin(li * dt)
    zr, zi = lbr - 1.0, lbi
    den = lr * lr + li * li
    fr = (zr * lr + zi * li) / den
    fi = (zi * lr - zr * li) / den
    bbr = fr[..., None] * b_re - fi[..., None] * b_im
    bbi = fr[..., None] * b_im + fi[..., None] * b_re
    return lbr, lbi, bbr, bbi


def _softplus_neg(lam):
    return jnp.maximum(-lam, 0.0) + jnp.log(1.0 + jnp.exp(-jnp.abs(lam)))


S5_Q = 8
RG_Q = 2


def _local_step(x, tgt, W, comm):
    T, D = x.shape
    C = D
    G, P, H = W["ssm_b_re"].shape
    S = G * H
    F = W["mlp_b_up"].shape[1]
    n_in = 2 * C + S + 2 * D
    heads, hd = W["rg_wa"].shape[0], W["rg_wa"].shape[1]
    u_off, ga_off, gb_off = 2 * C, 2 * C + S, 2 * C + S + D

    if comm.first_token is not None:
        anchored = ("rg_lambda", "ssm_a_re", "rg_wa", "rg_wx", "ssm_c_re", "ssm_c_im")
        W = {**W, **{k: W[k] + comm.first_token[0, 0] for k in anchored}}
    sp, sp_vjp = jax.vjp(_softplus_neg, W["rg_lambda"])
    (lbr, lbi, bbr, bbi), s5_vjp = jax.vjp(_s5_discretise, W["ssm_a_re"], W["ssm_a_im"], W["ssm_log_dt"],
                                           W["ssm_b_re"], W["ssm_b_im"])
    lam_re, lam_im = lbr.reshape(-1), lbi.reshape(-1)
    jr, kr = heads // RG_Q, RG_Q * hd
    w_ri = jnp.concatenate([_bd_pack(W["rg_wa"], RG_Q), _bd_pack(W["rg_wx"], RG_Q)], axis=2).astype(BF16)
    b_ri = jnp.concatenate([W["rg_ba"].reshape(jr, kr), W["rg_bx"].reshape(jr, kr)], axis=1).reshape(1, -1)
    wb_re = _bd_pack(jnp.swapaxes(bbr, 1, 2), S5_Q).astype(BF16)
    wb_im = _bd_pack(jnp.swapaxes(bbi, 1, 2), S5_Q).astype(BF16)
    wc_re = _bd_pack(jnp.swapaxes(W["ssm_c_re"], 1, 2), S5_Q).astype(BF16)
    wc_im_neg = _bd_pack(jnp.swapaxes(-W["ssm_c_im"], 1, 2), S5_Q).astype(BF16)
    d_row = W["ssm_d"].reshape(1, S)
    powers = _power_slabs(lam_re, lam_im, S5_TILE // SUBLANES)

    x_bf = x.astype(BF16) if comm.first_token is None else (x + comm.first_token[0, 0]).astype(BF16)
    w_in, conv_w = comm.first_weights((x_bf, w_ri, wb_re, wb_im, wc_re, wc_im_neg, powers[0], powers[1]))
    z = _mm(x_bf, w_in, M=T, N=n_in, K=D, tm=1024, tn=n_in // 4, tk=D, after=comm.gather_token, name="fwd_in_proj")
    started = comm.start_weights(("mlp_w_up",), z)
    xc = _conv_fwd(z, conv_w, W["conv_b"], T=T, C=C, after=started, name="fwd_conv")
    ri = _bd([(xc, 0, w_ri)], T=T, J=jr, kb=kr, nb=2 * kr, extras=[(b_ri, "vec", 0)],
             epilogue=lambda acc, b: (_sig(acc + b),), name="fwd_gates")
    h, p, a_fwd, m_fwd = _rg_scan_fwd(z, ri, xc, sp, T=T, C=C, gate_off=C, cw=kr, name="fwd_rg_scan")
    w_a_out = comm.weight("w_a_out", p)
    started = comm.start_weights(("mlp_w_down",), p)
    y_a = _mm(p, w_a_out, M=T, N=D, K=C, out_dtypes=(BF16,), tm=512, tn=D, tk=C, after=started, name="fwd_rg_out")

    h_re, h_im, y_s, yg = _s5_fwd(z, u_off, wb_re, wb_im, wc_re, wc_im_neg, d_row, powers, T=T, name="fwd_s5")
    w_glu_w, w_glu_v = comm.weight("glu_w", yg), comm.weight("glu_v", yg)
    glu_a = _mm(yg, w_glu_w, M=T, N=D, K=S, out_dtypes=(BF16,), tm=1024, tn=D, tk=S, name="fwd_glu_w")
    cwm = 1024

    def mix_fn(b, ga, gb, ya, a):
        return b, _sig(ga) * ya.astype(F32) + _sig(gb) * (a.astype(F32) * _sig(b))

    glu_b, mix = _mm(yg, w_glu_v, M=T, N=D, K=S, tm=512, tn=cwm, tk=S,
                     extras=[(z, "mn", ga_off // cwm), (z, "mn", gb_off // cwm), (y_a, "mn"), (glu_a, "mn")],
                     epilogue=mix_fn, n_out=2, out_dtypes=(BF16, BF16), name="fwd_glu_v_mix")
    w_out = comm.weight("w_out", mix)
    def out_ln1_fn(acc, xv, g, b):
        s = ALPHA * xv + acc
        xhat, _ = _ln_stats(s)
        y = xhat * g + b
        return s, y, y

    s1, x1, x1_bf = _mm(mix, w_out, M=T, N=D, K=D, tm=256, tn=D, tk=D,
                        extras=[(x, "mn"), (W["ln1_g"], "n"), (W["ln1_b"], "n")], epilogue=out_ln1_fn, n_out=3,
                        out_dtypes=(F32, F32, BF16), name="fwd_out_proj_ln1")
    w_up = comm.weight("mlp_w_up", x1_bf)

    def mlp_up_fn(acc, b):
        hp = acc + b
        rl = jnp.maximum(hp, 0.0)
        return rl * rl, hp

    hact, hpre = _mm(x1_bf, w_up, M=T, N=F, K=D, tm=1024, tn=1024, tk=D, extras=[(W["mlp_b_up"], "n")],
                     epilogue=mlp_up_fn, n_out=2, out_dtypes=(BF16, BF16), name="fwd_mlp_up")
    w_down = comm.weight("mlp_w_down", hact)
    s2 = _mm(hact, w_down, M=T, N=D, K=F, tm=1024, tn=1024, tk=2048,
             extras=[(x1, "mn"), (W["mlp_b_down"], "n")], epilogue=lambda acc, xv, b: (ALPHA * xv + acc + b,),
             name="fwd_mlp_down")

    def ln2_fn(s, t, g, b):
        xhat, rstd = _ln_stats(s)
        err = xhat * g + b - t
        dy = err * (1.0 / D)
        ds = _ln_bwd(dy, g, xhat, rstd)
        return ds, ds, 0.5 * dy * err, dy * xhat, dy, ds

    ds2, ds2_bf, loss_cols, d_ln2_g, d_ln2_b, d_b_down = _ew(
        ln2_fn, [(s2, "tile", 0), (tgt, "tile", 0), (W["ln2_g"], "vec", 0), (W["ln2_b"], "vec", 0)],
        T=T, C=D, n_out=2, n_cs=4, out_dtypes=(F32, BF16), tm=256, name="bwd_loss_ln2")
    d_w_down = _mm(hact, ds2_bf, M=F, N=D, K=T, ta=True, out_dtypes=(BF16,), tm=1024, tn=1024, tk=4096, name="bwd_w_down")
    sent = comm.send_grad("mlp_w_down", d_w_down)

    def dhpre_fn(acc, hp):
        dv = acc * (2.0 * jnp.maximum(hp.astype(F32), 0.0))
        return dv, dv

    dhpre, d_b_up = _mm(ds2_bf, w_down, M=T, N=F, K=D, tb=True, tm=1024, tn=1024, tk=D, extras=[(hpre, "mn")],
                        epilogue=dhpre_fn, n_cs=1, out_dtypes=(BF16,), after=sent, name="bwd_mlp_down")
    d_w_up = _mm(x1_bf, dhpre, M=D, N=F, K=T, ta=True, out_dtypes=(BF16,), n_split=N_DEV, tm=1024, tn=F // N_DEV, tk=4096, name="bwd_w_up")
    sent = comm.send_grad("mlp_w_up", d_w_up)
    dx1 = _mm(dhpre, w_up, M=T, N=D, K=F, tb=True, tm=1024, tn=1024, tk=2048,
              extras=[(ds2, "mn")], epilogue=lambda acc, dv: (ALPHA * dv + acc,), after=sent, name="bwd_mlp_up")

    def ln1_bwd_fn(s, dy, g):
        xhat, rstd = _ln_stats(s)
        ds = _ln_bwd(dy, g, xhat, rstd)
        return ds, ds, dy * xhat, dy

    ds1, ds1_bf, d_ln1_g, d_ln1_b = _ew(ln1_bwd_fn, [(s1, "tile", 0), (dx1, "tile", 0), (W["ln1_g"], "vec", 0)],
                                        T=T, C=D, n_out=2, n_cs=2, out_dtypes=(F32, BF16), tm=256, name="bwd_ln1")
    d_w_out = _mm(mix, ds1_bf, M=D, N=D, K=T, ta=True, out_dtypes=(BF16,), tm=1024, tn=1024, tk=4096, name="bwd_w_out")
    sent = comm.send_grad("w_out", d_w_out)
    def mix_bwd_fn(dm, ga, gb, ya, a, b):
        ya, a, b = ya.astype(F32), a.astype(F32), b.astype(F32)
        sa, sb, sv = _sig(ga), _sig(gb), _sig(b)
        yb = a * sv
        dyb = dm * sb
        return (dm * ya * (sa * (1.0 - sa)), dm * yb * (sb * (1.0 - sb)), dm * sa, dyb * sv,
                dyb * a * (sv * (1.0 - sv)))

    dz = lax.empty((T, n_in), BF16)
    dz, dg_b, dy_a, dglu_a, dglu_b = _mm(
        ds1_bf, w_out, M=T, N=D, K=D, tb=True, tm=512, tn=cwm, tk=D,
        extras=[(z, "mn", ga_off // cwm), (z, "mn", gb_off // cwm), (y_a, "mn"), (glu_a, "mn"), (glu_b, "mn")],
        epilogue=mix_bwd_fn, n_out=5, out_dtypes=(BF16,) * 5, after=sent, into=(dz, 0, ga_off // cwm),
        name="bwd_out_proj_mix")
    dz = lax.dynamic_update_slice(dz, dg_b, (0, gb_off))

    d_w_a_out = _mm(p, dy_a, M=C, N=D, K=T, ta=True, out_dtypes=(BF16,), tm=1024, tn=1024, tk=4096, name="bwd_w_a_out")
    sent = comm.send_grad("w_a_out", d_w_a_out)
    def dp_fn(dp, hv, gate):
        th = jnp.tanh(GELU_C * (gate + GELU_K * gate * gate * gate))
        gelu = 0.5 * gate * (1.0 + th)
        dgelu = 0.5 * (1.0 + th) + 0.5 * gate * (1.0 - th * th) * (GELU_C * (1.0 + 3.0 * GELU_K * gate * gate))
        return dp * gelu, dp * hv * dgelu

    dh, dz = _mm(dy_a, w_a_out, M=T, N=C, K=D, tb=True, tm=256, tn=C, tk=D, extras=[(h, "mn"), (z, "mn", 1)],
                 epilogue=dp_fn, n_out=2, out_dtypes=(F32, BF16), after=sent, into=(dz, 1, 1), name="bwd_rg_out")
    drai, dxc0, d_b_ri, d_sp = _rg_scan_bwd(dh, h, ri, xc, a_fwd, m_fwd, sp, T=T, C=C, cw=kr, name="bwd_rg_scan")
    dxc = _bd([(drai, 0, w_ri)], T=T, J=jr, kb=2 * kr, nb=kr, tw=True, extras=[(dxc0, "tile", 0)],
              epilogue=lambda acc, d0: (acc + d0,), name="bwd_gates")
    d_w_ri = _bdw(xc, 0, drai, 0, T=T, J=jr, kb=kr, nb=2 * kr, name="bwd_w_gates")
    d_wa, d_wx = _bd_unpack(d_w_ri[:, :, :kr], RG_Q), _bd_unpack(d_w_ri[:, :, kr:], RG_Q)
    d_b_ri = d_b_ri.reshape(jr, 2 * kr)
    d_ba, d_bx = d_b_ri[:, :kr].reshape(1, -1), d_b_ri[:, kr:].reshape(1, -1)
    dz, conv_sums = _conv_bwd(dxc, z, conv_w, dz, T=T, C=C, name="bwd_conv")
    d_conv_w, d_conv_b = conv_sums[0:4], conv_sums[4:5]
    (d_lambda,) = sp_vjp(d_sp)

    d_glu_w = _mm(yg, dglu_a, M=S, N=D, K=T, ta=True, out_dtypes=(BF16,), n_split=N_DEV, tm=1024, tn=D // N_DEV, tk=4096, name="bwd_w_glu_w")
    d_glu_v = _mm(yg, dglu_b, M=S, N=D, K=T, ta=True, out_dtypes=(BF16,), n_split=N_DEV, tm=1024, tn=D // N_DEV, tk=4096, name="bwd_w_glu_v")
    sent = comm.send_grad("glu_w", d_glu_w, "glu_v", d_glu_v)
    dyg0 = _mm(dglu_a, w_glu_w, M=T, N=S, K=D, tb=True, tm=512, tn=S, tk=D, after=sent, name="bwd_glu_w")
    dy_s = _mm(dglu_b, w_glu_v, M=T, N=S, K=D, tb=True, tm=512, tn=S, tk=D,
               extras=[(dyg0, "mn"), (y_s, "mn")], epilogue=lambda acc, d0, yv: ((acc + d0) * _dgelu(yv),),
               name="bwd_glu_v")
    dz, d_lbr, d_lbi, d_ssm_d, d_wb_re, d_wb_im, d_wc_re, d_wc_im_neg = _s5_bwd(
        dy_s, z, u_off, h_re, h_im, wb_re, wb_im, wc_re, wc_im_neg, d_row, powers, dz, T=T, name="bwd_s5")
    d_bbr = jnp.swapaxes(_bd_unpack(d_wb_re, S5_Q), 1, 2)
    d_bbi = jnp.swapaxes(_bd_unpack(d_wb_im, S5_Q), 1, 2)
    d_a_re, d_a_im, d_log_dt, d_b_re, d_b_im = s5_vjp((d_lbr.reshape(G, P), d_lbi.reshape(G, P), d_bbr, d_bbi))
    d_c_re = _bd_unpack(d_wc_re, S5_Q)
    d_c_im = -_bd_unpack(d_wc_im_neg, S5_Q)

    grads = dict(
        conv_w=d_conv_w, conv_b=d_conv_b, rg_wa=d_wa, rg_ba=d_ba, rg_wx=d_wx, rg_bx=d_bx,
        rg_lambda=d_lambda, ssm_a_re=d_a_re, ssm_a_im=d_a_im, ssm_log_dt=d_log_dt,
        ssm_b_re=d_b_re, ssm_b_im=d_b_im, ssm_c_re=d_c_re, ssm_c_im=d_c_im, ssm_d=d_ssm_d.reshape(G, H),
        ln1_g=d_ln1_g, ln1_b=d_ln1_b, mlp_b_up=d_b_up, mlp_b_down=d_b_down, ln2_g=d_ln2_g, ln2_b=d_ln2_b)
    sent = comm.send_small(grads)

    d_w_in = _mm(x_bf, dz, M=D, N=n_in, K=T, ta=True, out_dtypes=(BF16,), n_split=N_DEV, tm=1024, tn=n_in // N_DEV,
                 tk=4096, after=sent, name="bwd_w_in")
    sent = comm.send_grad("w_in", d_w_in)
    grad_x = _mm(dz, w_in, M=T, N=D, K=n_in, tb=True, tm=1024, tn=1024, tk=n_in // 4,
                 extras=[(ds1, "mn")], epilogue=lambda acc, dv: (ALPHA * dv + acc,), after=sent, name="bwd_in_proj")
    return jnp.sum(loss_cols), grad_x, grads


BIG = ("w_in", "w_a_out", "glu_w", "glu_v", "w_out", "mlp_w_up", "mlp_w_down")
COL_SHARDED = ("w_in", "glu_w", "glu_v", "mlp_w_up")
SMALL = ("conv_w", "conv_b", "rg_wa", "rg_ba", "rg_wx", "rg_bx", "rg_lambda", "ssm_a_re", "ssm_a_im", "ssm_log_dt",
         "ssm_b_re", "ssm_b_im", "ssm_c_re", "ssm_c_im", "ssm_d", "ln1_g", "ln1_b", "mlp_b_up", "mlp_b_down", "ln2_g",
         "ln2_b")
ORDER = ("w_in", "conv_w", "conv_b", "rg_wa", "rg_ba", "rg_wx", "rg_bx", "rg_lambda", "w_a_out", "ssm_a_re",
         "ssm_a_im", "ssm_log_dt", "ssm_b_re", "ssm_b_im", "ssm_c_re", "ssm_c_im", "ssm_d", "glu_w", "glu_v", "w_out",
         "ln1_g", "ln1_b", "mlp_w_up", "mlp_b_up", "mlp_w_down", "mlp_b_down", "ln2_g", "ln2_b")
TILE_ELEMS = SUBLANES * LANES


def _pack(arrs):
    pieces = []
    for a in arrs:
        flat = a.reshape(-1)
        flat = jnp.pad(flat, (0, (-flat.shape[0]) % TILE_ELEMS))
        pieces.append(flat.reshape(-1, LANES))
    rows = sum(p.shape[0] for p in pieces)
    pad_rows = (-rows) % (N_DEV * SUBLANES)
    if pad_rows:
        pieces.append(jnp.zeros((pad_rows, LANES), pieces[0].dtype))
    return jnp.concatenate(pieces, axis=0)


def _unpack(packed, shapes):
    out, row = [], 0
    for shp in shapes:
        n = math.prod(shp)
        rows = -(-n // TILE_ELEMS) * SUBLANES
        out.append(packed[row:row + rows].reshape(-1)[:n].reshape(shp))
        row += rows
    return out


class _Comm:
    def __init__(self, w):
        first = [w["w_in"].astype(BF16), w["conv_w"]]
        self._first, self.first_token = _exchange_start(first, [_landing_zone(s) for s in first], mode="own",
                                                        name="gather_in_start")
        self._shards = {k: w[k].astype(BF16) for k in BIG if k != "w_in"}
        self._weights, self._gathers, self._grads = {}, {}, {}

    def first_weights(self, after):
        lands = [_exchange_wait(h, after, mode="own", name="gather_in_wait_%d" % i) for i, h in enumerate(self._first)]
        unused = [lax.empty((2 * SUBLANES, LANES), BF16) for _ in lands]
        handles, passed = _exchange_start(unused, lands, mode="pass", name="gather_in_pass")
        w_in, taps = [_exchange_wait(h, passed, mode="pass", name="gather_in_got_%d" % i) for i, h in enumerate(handles)]
        self._weights["w_in"] = w_in
        self.gather_token = self.start_weights(("w_a_out", "glu_w", "glu_v", "w_out"), w_in)
        return self.weight("w_in", None), jnp.swapaxes(taps, 0, 1).reshape(taps.shape[1], -1)

    def start_weights(self, names, after):
        shards = [self._shards.pop(k) for k in names]
        handles, token = _exchange_start(shards, [_landing_zone(s) for s in shards], mode="gather", after=after,
                                         name="gather_start_" + names[0])
        self._gathers.update(zip(names, handles))
        return token

    def weight(self, k, after):
        if k not in self._weights:
            self._weights[k] = _exchange_wait(self._gathers.pop(k), after, mode="gather", name="gather_wait_" + k)
        gk = self._weights[k]
        if k in COL_SHARDED:
            return jnp.swapaxes(gk, 0, 1).reshape(gk.shape[1], -1)
        return gk.reshape(-1, gk.shape[-1])

    def send_grad(self, *names_and_parts):
        names, parts = names_and_parts[0::2], names_and_parts[1::2]
        parts = [p if k in COL_SHARDED else p.reshape(N_DEV, p.shape[0] // N_DEV, p.shape[1])
                 for k, p in zip(names, parts)]
        me = _dev_index(*_mesh_pos())
        lands = [_landing_zone(lax.dynamic_index_in_dim(p, me, 0, keepdims=False)) for p in parts]
        handles, token = _exchange_start(parts, lands, mode="scatter", name="grad_start_" + names[0])
        self._grads.update(zip(names, handles))
        return token

    def received_grad(self, k, after):
        return _exchange_wait(self._grads.pop(k), after, mode="scatter", name="grad_wait_" + k)

    def send_small(self, grads):
        return self.send_grad("small", _pack([grads[k] for k in SMALL]))

    def all_reduced_small(self, after, behind):
        recv = self.received_grad("small", after)
        block = _sum_slots(recv, tr=512, name="sum_small_grads")
        (handle,), started = _exchange_start([block], [_landing_zone(block)], mode="gather", name="small_sum_start")
        done = behind(started)
        return _exchange_wait(handle, done, mode="gather", name="small_sum_wait").reshape(-1, LANES)


SMALL_GROUPS = (("rg_wa", "rg_wx"), ("ssm_b_re",), ("ssm_b_im",),
                tuple(k for k in SMALL if k not in ("rg_wa", "rg_wx", "ssm_b_re", "ssm_b_im")))


def _step(x, tgt, w, m, v, raw_w, raw_m, raw_v):
    dev = _dev_index(*_mesh_pos())

    comm = _Comm(w)
    small = dict(w)
    for k in ("conv_b", "rg_ba", "rg_bx", "rg_lambda", "ln1_g", "ln1_b", "mlp_b_up", "mlp_b_down", "ln2_g", "ln2_b"):
        small[k] = w[k].reshape(1, -1)

    loss_part, grad_x, grads = _local_step(x, tgt, small, comm)

    out_g, out_d, out_m, out_v = {}, {}, {}, {}

    def update_large(started):
        for k in BIG:
            rk = comm.received_grad(k, (grad_x, started))
            out_g[k], out_d[k], out_m[k], out_v[k] = _adamw(rk, w[k], m[k], v[k], tr=256 if k in COL_SHARDED else 128,
                                                                 name="adamw_" + k)
        return out_v[BIG[-1]]

    small_all = comm.all_reduced_small(grad_x, update_large)
    g_small = dict(zip(SMALL, _unpack(small_all, [grads[k].shape for k in SMALL])))
    cw_cols = w["conv_w"].shape[1]
    g_small["conv_w"] = lax.dynamic_slice_in_dim(g_small["conv_w"], dev * cw_cols, cw_cols, axis=1)
    for group in SMALL_GROUPS:
        gs = [g_small[k].reshape(raw_w[k].shape) for k in group]
        res = _adamw_whole(gs, [raw_w[k] for k in group], [raw_m[k] for k in group], [raw_v[k] for k in group],
                           name="adamw_" + group[0])
        for k, gk, (dk, mk, vk) in zip(group, gs, res):
            out_g[k], out_d[k], out_m[k], out_v[k] = gk, dk, mk, vk

    loss = lax.psum(loss_part, ("x", "y", "c"))
    return loss, grad_x, out_g, out_d, out_m, out_v


def kernel(x, w_in, conv_w, conv_b, rg_wa, rg_ba, rg_wx, rg_bx, rg_lambda, w_a_out, ssm_a_re, ssm_a_im, ssm_log_dt, ssm_b_re, ssm_b_im, ssm_c_re, ssm_c_im, ssm_d, glu_w, glu_v, w_out, ln1_g, ln1_b, mlp_w_up, mlp_b_up, mlp_w_down, mlp_b_down, ln2_g, ln2_b, loss_target, m_w_in, m_conv_w, m_conv_b, m_rg_wa, m_rg_ba, m_rg_wx, m_rg_bx, m_rg_lambda, m_w_a_out, m_ssm_a_re, m_ssm_a_im, m_ssm_log_dt, m_ssm_b_re, m_ssm_b_im, m_ssm_c_re, m_ssm_c_im, m_ssm_d, m_glu_w, m_glu_v, m_w_out, m_ln1_g, m_ln1_b, m_mlp_w_up, m_mlp_b_up, m_mlp_w_down, m_mlp_b_down, m_ln2_g, m_ln2_b, v_w_in, v_conv_w, v_conv_b, v_rg_wa, v_rg_ba, v_rg_wx, v_rg_bx, v_rg_lambda, v_w_a_out, v_ssm_a_re, v_ssm_a_im, v_ssm_log_dt, v_ssm_b_re, v_ssm_b_im, v_ssm_c_re, v_ssm_c_im, v_ssm_d, v_glu_w, v_glu_v, v_w_out, v_ln1_g, v_ln1_b, v_mlp_w_up, v_mlp_b_up, v_mlp_w_down, v_mlp_b_down, v_ln2_g, v_ln2_b):
    args = locals()
    w = {k: args[k][0] for k in ORDER}
    m = {k: args["m_" + k][0] for k in BIG}
    v = {k: args["v_" + k][0] for k in BIG}
    raw = [{k: args[prefix + k] for k in SMALL} for prefix in ("", "m_", "v_")]
    loss, grad_x, out_g, out_d, out_m, out_v = _step(x[0], loss_target[0], w, m, v, *raw)
    outs = [loss, grad_x[None]]
    for group in (out_g, out_d, out_m, out_v):
        outs += [group[k].reshape(args[k].shape) for k in ORDER]
    return tuple(outs)
```

```python
import functools
import math

import jax
import jax.numpy as jnp
from jax import lax
from jax.experimental import pallas as pl
from jax.experimental.pallas import tpu as pltpu

F32 = jnp.float32
BF16 = jnp.bfloat16
MESH = pl.DeviceIdType.MESH
N_DEV = 8
SUBLANES = 8
LANES = 128
VMEM_BYTES_V7X = 64 * 2 ** 20
VMEM_CAP = VMEM_BYTES_V7X - 8 * 2 ** 20

ALPHA = 2.0 ** 0.25
LN_EPS = 1e-5
RG_C = 8.0
ADAM_LR, ADAM_B1, ADAM_B2, ADAM_EPS, ADAM_WD, ADAM_STEP = 0.001, 0.9, 0.999, 1e-08, 0.01, 10
GELU_C = math.sqrt(2.0 / math.pi)
GELU_K = 0.044715

ANY = pl.BlockSpec(memory_space=pl.ANY)


def _params(sem, vmem_bytes):
    limit = int(min(max(2 * vmem_bytes, 16 * 2 ** 20), VMEM_CAP))
    return pltpu.CompilerParams(dimension_semantics=sem, vmem_limit_bytes=limit)


def _sig(x):
    return 1.0 / (1.0 + jnp.exp(-x))


def _gelu(x):
    return 0.5 * x * (1.0 + jnp.tanh(GELU_C * (x + GELU_K * x * x * x)))


def _dgelu(x):
    th = jnp.tanh(GELU_C * (x + GELU_K * x * x * x))
    return 0.5 * (1.0 + th) + 0.5 * x * (1.0 - th * th) * (GELU_C * (1.0 + 3.0 * GELU_K * x * x))


def _one_minus_exp(x, exp_half_x):
    p = x * (1.0 + x * (1 / 2 + x * (1 / 6 + x * (1 / 24 + x * (1 / 120)))))
    return jnp.where(x > -1 / 16, -p, 1.0 - exp_half_x * exp_half_x)


def _accumulate(ref, val, first):
    @pl.when(first)
    def _():
        ref[...] = val

    @pl.when(jnp.logical_not(first))
    def _():
        ref[...] += val


def _rows8(cw):
    return lax.broadcasted_iota(jnp.int32, (SUBLANES, cw), 0)


def _shift_down(cur, prev, s, rows):
    return jnp.where(rows < s, pltpu.roll(prev, s, 0), pltpu.roll(cur, s, 0))


def _shift_up(cur, nxt, s, rows):
    return jnp.where(rows < SUBLANES - s, pltpu.roll(cur, SUBLANES - s, 0), pltpu.roll(nxt, SUBLANES - s, 0))


def _mm(a, b, *, M, N, K, ta=False, tb=False, b_split=1, n_split=1, a_fn=None, extras=(), epilogue=None,
        n_out=1, n_cs=0, out_dtypes=None, tm=512, tn=512, tk=512, after=None, into=None, name):
    tm, tn, tk = min(tm, M), min(tn, N), min(tk, K)
    assert M % tm == 0 and N % tn == 0 and K % tk == 0, (name, M, N, K, tm, tn, tk)
    nk = K // tk
    grid = (N // tn, M // tm, nk)
    a_spec = pl.BlockSpec((tk, tm), lambda j, i, k: (k, i)) if ta else pl.BlockSpec((tm, tk), lambda j, i, k: (i, k))
    if b_split == 1:
        b_spec = pl.BlockSpec((tn, tk), lambda j, i, k: (j, k)) if tb else pl.BlockSpec((tk, tn), lambda j, i, k: (k, j))
    elif tb:
        kb = (K // b_split) // tk
        assert kb * tk * b_split == K, name
        b_spec = pl.BlockSpec((None, tn, tk), lambda j, i, k: (k // kb, j, k % kb))
    else:
        nb = (N // b_split) // tn
        assert nb * tn * b_split == N, name
        b_spec = pl.BlockSpec((None, tk, tn), lambda j, i, k: (j // nb, k, j % nb))
    in_specs = [a_spec, b_spec]
    for arr, kind, *col_off in extras:
        off = col_off[0] if col_off else 0
        in_specs.append(pl.BlockSpec((tm, tn), lambda j, i, k, off=off: (i, off + j)) if kind == "mn"
                        else pl.BlockSpec((1, tn), lambda j, i, k: (0, j)))
    out_dtypes = (F32,) * n_out if out_dtypes is None else out_dtypes
    if n_split == 1:
        out_shape = [jax.ShapeDtypeStruct((M, N), dt) for dt in out_dtypes]
        out_specs = [pl.BlockSpec((tm, tn), lambda j, i, k: (i, j)) for _ in range(n_out)]
    else:
        assert n_out == 1
        nbo = (N // n_split) // tn
        assert nbo * tn * n_split == N, name
        out_shape = [jax.ShapeDtypeStruct((n_split, M, N // n_split), out_dtypes[0])]
        out_specs = [pl.BlockSpec((None, tm, tn), lambda j, i, k: (j // nbo, i, j % nbo))]
    out_shape += [jax.ShapeDtypeStruct((1, N), F32) for _ in range(n_cs)]
    out_specs += [pl.BlockSpec((1, tn), lambda j, i, k: (0, j)) for _ in range(n_cs)]
    ne = len(extras)
    dims = (((0 if ta else 1,), (1 if tb else 0,)), ((), ()))

    n_after = 0 if after is None else 1
    in_specs += [ANY] * n_after
    aliases, tail = {}, [] if after is None else [after]
    if into is not None:
        buf, which, col_blk = into
        assert n_split == 1 and buf.shape[0] == M and buf.dtype == out_dtypes[which], name
        aliases = {len(in_specs): which}
        in_specs.append(ANY)
        tail.append(buf)
        out_shape[which] = jax.ShapeDtypeStruct(buf.shape, buf.dtype)
        out_specs[which] = pl.BlockSpec((tm, tn), lambda j, i, k: (i, col_blk + j))

    def body(*refs):
        a_ref, b_ref = refs[0], refs[1]
        ex_refs = refs[2:2 + ne]
        first_out = 2 + ne + len(tail)
        out_refs = refs[first_out:first_out + n_out]
        cs_refs = refs[first_out + n_out:first_out + n_out + n_cs]
        i, k = pl.program_id(1), pl.program_id(2)

        def product():
            av = a_ref[...]
            if a_fn is not None:
                av = a_fn(av.astype(F32))
            return lax.dot_general(av.astype(BF16), b_ref[...].astype(BF16), dims, preferred_element_type=F32)

        def finish(acc):
            res = (acc,) if epilogue is None else epilogue(acc, *[r[...] for r in ex_refs])
            for r, o in zip(out_refs, res[:n_out]):
                r[...] = o.astype(r.dtype)
            for r, cval in zip(cs_refs, res[n_out:]):
                _accumulate(r, jnp.sum(cval, axis=0, keepdims=True), i == 0)

        if nk == 1:
            finish(product())
            return
        acc_ref = refs[-1]

        @pl.when(k == 0)
        def _():
            acc_ref[...] = jnp.zeros_like(acc_ref)

        acc_ref[...] += product()

        @pl.when(k == nk - 1)
        def _():
            finish(acc_ref[...])

    vmem = 2 * tm * tk * a.dtype.itemsize + 2 * tk * tn * b.dtype.itemsize + (1 + 2 * n_out + 2 * ne + 2) * tm * tn * 4
    outs = pl.pallas_call(
        body, name=name, grid=grid, in_specs=in_specs, out_specs=out_specs, out_shape=out_shape,
        scratch_shapes=[pltpu.VMEM((tm, tn), F32)] if nk > 1 else [], input_output_aliases=aliases,
        compiler_params=_params(("parallel", "arbitrary", "arbitrary"), vmem),
    )(a, b, *[e[0] for e in extras], *tail)
    return outs[0] if len(outs) == 1 else outs


BD_STEP = 4

def _bd(pairs, *, T, J, kb, nb, tw=False, extras=(), epilogue=None, n_out=1, n_cs=0, out_dtypes=None, tm=512, name):
    jb = BD_STEP
    assert T % tm == 0 and J % jb == 0
    grid = (J // jb, T // tm)
    npair, ne = len(pairs), len(extras)
    in_specs, args = [], []
    for arr, off, w in pairs:
        assert off % jb == 0, name
        in_specs.append(pl.BlockSpec((tm, jb * kb), lambda j, i, off=off // jb: (i, off + j)))
        in_specs.append(pl.BlockSpec((jb,) + tuple(w.shape[1:]), lambda j, i: (j, 0, 0)))
        args += [arr, w]
    for arr, kind, off in extras:
        assert off % jb == 0, name
        in_specs.append(pl.BlockSpec((tm, jb * nb), lambda j, i, off=off // jb: (i, off + j)) if kind == "tile"
                        else pl.BlockSpec((1, jb * nb), lambda j, i, off=off // jb: (0, off + j)))
        args.append(arr)
    out_dtypes = (F32,) * n_out if out_dtypes is None else out_dtypes
    out_shape = [jax.ShapeDtypeStruct((T, J * nb), dt) for dt in out_dtypes]
    out_specs = [pl.BlockSpec((tm, jb * nb), lambda j, i: (i, j)) for _ in range(n_out)]
    out_shape += [jax.ShapeDtypeStruct((1, J * nb), F32) for _ in range(n_cs)]
    out_specs += [pl.BlockSpec((1, jb * nb), lambda j, i: (0, j)) for _ in range(n_cs)]
    dims = (((1,), (1 if tw else 0,)), ((), ()))

    def body(*refs):
        ex_refs = refs[2 * npair:2 * npair + ne]
        out_refs = refs[2 * npair + ne:2 * npair + ne + n_out]
        cs_refs = refs[2 * npair + ne + n_out:]
        i = pl.program_id(1)
        for s in range(jb):
            cols_in, cols_out = pl.ds(s * kb, kb), pl.ds(s * nb, nb)
            acc = None
            for p in range(npair):
                d = lax.dot_general(refs[2 * p][:, cols_in].astype(BF16), refs[2 * p + 1][s].astype(BF16), dims,
                                    preferred_element_type=F32)
                acc = d if acc is None else acc + d
            res = (acc,) if epilogue is None else epilogue(acc, *[r[:, cols_out] for r in ex_refs])
            for r, o in zip(out_refs, res[:n_out]):
                r[:, cols_out] = o.astype(r.dtype)
            for r, cval in zip(cs_refs, res[n_out:]):
                _accumulate(r.at[:, cols_out], jnp.sum(cval, axis=0, keepdims=True), i == 0)

    vmem = jb * (2 * npair * tm * kb + 2 * npair * kb * nb + (2 * n_out + 2 * ne + 3) * tm * nb) * 4
    outs = pl.pallas_call(
        body, name=name, grid=grid, in_specs=in_specs, out_specs=out_specs, out_shape=out_shape,
        compiler_params=_params(("parallel", "arbitrary"), vmem),
    )(*args)
    return outs[0] if len(outs) == 1 else outs


def _bdw(a, a_off, b, b_off, *, T, J, kb, nb, tm=512, name):
    jb = BD_STEP
    assert T % tm == 0 and J % jb == 0 and a_off % jb == 0 and b_off % jb == 0
    a_blk, b_blk = a_off // jb, b_off // jb

    def body(a_ref, b_ref, o_ref):
        i = pl.program_id(1)
        for s in range(jb):
            d = lax.dot_general(a_ref[:, pl.ds(s * kb, kb)].astype(BF16), b_ref[:, pl.ds(s * nb, nb)].astype(BF16),
                                (((0,), (0,)), ((), ())), preferred_element_type=F32)
            _accumulate(o_ref.at[s], d, i == 0)

    return pl.pallas_call(
        body, name=name, grid=(J // jb, T // tm),
        in_specs=[pl.BlockSpec((tm, jb * kb), lambda j, i: (i, a_blk + j)),
                  pl.BlockSpec((tm, jb * nb), lambda j, i: (i, b_blk + j))],
        out_specs=pl.BlockSpec((jb, kb, nb), lambda j, i: (j, 0, 0)),
        out_shape=jax.ShapeDtypeStruct((J, kb, nb), F32),
        compiler_params=_params(("parallel", "arbitrary"), jb * (2 * tm * (kb + nb) + 3 * kb * nb) * 4),
    )(a, b)


def _bd_pack(w, q):
    g, a, b = w.shape
    eye = jnp.eye(q, dtype=w.dtype)
    return jnp.einsum("jqab,qr->jqarb", w.reshape(g // q, q, a, b), eye).reshape(g // q, q * a, q * b)


def _bd_unpack(wp, q):
    j, qa, qb = wp.shape
    a, b = qa // q, qb // q
    w5 = wp.reshape(j, q, a, q, b)
    return jnp.stack([w5[:, r, :, r, :] for r in range(q)], axis=1).reshape(j * q, a, b)


def _ew(fn, ins, *, T, C, n_out, n_cs=0, out_dtypes=None, tm=256, cw=None, name):
    cw = C if cw is None else cw
    assert T % tm == 0 and C % cw == 0
    grid = (C // cw, T // tm)
    in_specs = []
    for arr, kind, off in ins:
        in_specs.append(pl.BlockSpec((tm, cw), lambda j, i, off=off: (i, off + j)) if kind == "tile"
                        else pl.BlockSpec((arr.shape[0], cw), lambda j, i, off=off: (0, off + j)))
    out_dtypes = (F32,) * n_out if out_dtypes is None else out_dtypes
    out_shape = [jax.ShapeDtypeStruct((T, C), dt) for dt in out_dtypes]
    out_specs = [pl.BlockSpec((tm, cw), lambda j, i: (i, j)) for _ in range(n_out)]
    out_shape += [jax.ShapeDtypeStruct((1, C), F32) for _ in range(n_cs)]
    out_specs += [pl.BlockSpec((1, cw), lambda j, i: (0, j)) for _ in range(n_cs)]
    nin = len(ins)

    def body(*refs):
        i = pl.program_id(1)
        res = fn(*[r[...].astype(F32) for r in refs[:nin]])
        for r, o in zip(refs[nin:nin + n_out], res[:n_out]):
            r[...] = o.astype(r.dtype)
        for r, cval in zip(refs[nin + n_out:], res[n_out:]):
            _accumulate(r, jnp.sum(cval, axis=0, keepdims=True), i == 0)

    vmem = (2 * nin + 2 * n_out + 6) * tm * cw * 4
    outs = pl.pallas_call(
        body, name=name, grid=grid, in_specs=in_specs, out_specs=out_specs, out_shape=out_shape,
        compiler_params=_params(("parallel", "arbitrary"), vmem),
    )(*[arr for arr, _, _ in ins])
    return outs[0] if len(outs) == 1 else outs


def _ln_stats(s):
    mu = jnp.mean(s, axis=-1, keepdims=True)
    d = s - mu
    var = jnp.mean(d * d, axis=-1, keepdims=True)
    rstd = lax.rsqrt(var + LN_EPS)
    return d * rstd, rstd


def _ln_bwd(dy, g, xhat, rstd):
    dxh = dy * g
    m1 = jnp.mean(dxh, axis=-1, keepdims=True)
    m2 = jnp.mean(dxh * xhat, axis=-1, keepdims=True)
    return rstd * (dxh - m1 - xhat * m2)


def _conv_fwd(z, conv_w, conv_b, *, T, C, tm=512, cw=1024, after=None, name):
    ng, hb = tm // SUBLANES, tm // SUBLANES
    n_after = 0 if after is None else 1

    def body(x_ref, halo_ref, w_ref, b_ref, *rest):
        o_ref = rest[-1]
        it = pl.program_id(1)
        rows = _rows8(cw)
        halo = jnp.where(it == 0, 0.0, halo_ref[...])
        w = w_ref[...]
        bias = b_ref[...]

        def group(g, carry):
            off = pl.multiple_of(g * SUBLANES, SUBLANES)
            cur = x_ref[pl.ds(off, SUBLANES), :]
            prev = x_ref[pl.ds(pl.multiple_of(jnp.maximum(off - SUBLANES, 0), SUBLANES), SUBLANES), :]
            prev = jnp.where(g == 0, halo, prev)
            acc = cur * w[3:4] + bias
            for s in (1, 2, 3):
                acc = acc + _shift_down(cur, prev, s, rows) * w[3 - s:4 - s]
            o_ref[pl.ds(off, SUBLANES), :] = acc
            return carry

        lax.fori_loop(0, ng, group, 0, unroll=2)

    return pl.pallas_call(
        body, name=name, grid=(C // cw, T // tm),
        in_specs=[pl.BlockSpec((tm, cw), lambda j, i: (i, j)),
                  pl.BlockSpec((SUBLANES, cw), lambda j, i: (jnp.maximum(i * hb - 1, 0), j)),
                  pl.BlockSpec((4, cw), lambda j, i: (0, j)), pl.BlockSpec((1, cw), lambda j, i: (0, j))]
        + [ANY] * n_after,
        out_specs=pl.BlockSpec((tm, cw), lambda j, i: (i, j)),
        out_shape=jax.ShapeDtypeStruct((T, C), F32),
        compiler_params=_params(("parallel", "arbitrary"), 5 * tm * cw * 4),
    )(z, z, conv_w, conv_b, *([after] if n_after else []))


def _conv_bwd(dxc, z, conv_w, dz, *, T, C, tm=512, cw=512, name):
    ng, hb, last = tm // SUBLANES, tm // SUBLANES, T // SUBLANES - 1
    nt = T // tm
    rows16 = 2 * SUBLANES

    def body(d_ref, dn_ref, x_ref, w_ref, dz_in_ref, o_ref, sums_ref):
        it = pl.program_id(1)
        rows = _rows8(cw)
        dnext = jnp.where(it == nt - 1, 0.0, dn_ref[...])
        w = w_ref[...]

        def pair(q, accs):
            base = pl.multiple_of(q * rows16, rows16)
            halves = []
            for half in range(2):
                g = 2 * q + half
                off = pl.multiple_of(base + half * SUBLANES, SUBLANES)
                dcur = d_ref[pl.ds(off, SUBLANES), :]
                dnx = d_ref[pl.ds(pl.multiple_of(jnp.minimum(off + SUBLANES, tm - SUBLANES), SUBLANES), SUBLANES), :]
                dnx = jnp.where(g == ng - 1, dnext, dnx)
                xcur = x_ref[pl.ds(off, SUBLANES), :]
                acc = dcur * w[3:4]
                taps = [accs[3] + dcur * xcur]
                for s in (1, 2, 3):
                    ahead = _shift_up(dcur, dnx, s, rows)
                    acc = acc + ahead * w[3 - s:4 - s]
                    taps.append(accs[3 - s] + ahead * xcur)
                halves.append(acc)
                accs = (taps[3], taps[2], taps[1], taps[0], accs[4] + dcur)
            o_ref[pl.ds(base, rows16), :] = jnp.concatenate(halves, axis=0).astype(o_ref.dtype)
            return accs

        zero = jnp.zeros((SUBLANES, cw), F32)
        accs = lax.fori_loop(0, ng // 2, pair, (zero,) * 5)
        sums = jnp.zeros((SUBLANES, cw), F32)
        for k, a in enumerate(accs):
            sums = jnp.where(rows == k, jnp.sum(a, axis=0, keepdims=True), sums)
        _accumulate(sums_ref, sums, it == 0)

    tile = pl.BlockSpec((tm, cw), lambda j, i: (i, j))
    return pl.pallas_call(
        body, name=name, grid=(C // cw, nt),
        in_specs=[tile, pl.BlockSpec((SUBLANES, cw), lambda j, i: (jnp.minimum((i + 1) * hb, last), j)),
                  tile, pl.BlockSpec((4, cw), lambda j, i: (0, j)), ANY],
        out_specs=[tile, pl.BlockSpec((SUBLANES, cw), lambda j, i: (0, j))],
        input_output_aliases={4: 0},
        out_shape=[jax.ShapeDtypeStruct(dz.shape, dz.dtype), jax.ShapeDtypeStruct((SUBLANES, C), F32)],
        compiler_params=_params(("parallel", "arbitrary"), 7 * tm * cw * 4),
    )(dxc, dxc, z, conv_w, dz)


def _rg_coeffs(r, ig, xc, sp):
    la = (-RG_C) * r * sp
    a = jnp.exp(la)
    m = jnp.sqrt(_one_minus_exp(2.0 * la, a))
    return a, m, m * (ig * xc)


def _rg_scan_fwd(z, ri, xc, sp, *, T, C, gate_off, tm=512, cw=256, name):
    rows16 = 2 * SUBLANES
    nq = tm // rows16

    def body(gate_ref, r_ref, i_ref, xc_ref, sp_ref, h_ref, p_ref, a_ref, m_ref, carry_ref):
        it = pl.program_id(1)

        @pl.when(it == 0)
        def _():
            carry_ref[...] = jnp.zeros_like(carry_ref)

        rows = _rows8(cw)
        sp_row = sp_ref[...]

        def pair(q, carry):
            base = pl.multiple_of(q * rows16, rows16)
            halves = []
            for half in range(2):
                sl = pl.ds(pl.multiple_of(base + half * SUBLANES, SUBLANES), SUBLANES)
                a, m, b = _rg_coeffs(r_ref[sl, :], i_ref[sl, :], xc_ref[sl, :], sp_row)
                a_ref[sl, :] = a
                m_ref[sl, :] = m
                for s in (1, 2, 4):
                    keep = rows >= s
                    sa = jnp.where(keep, pltpu.roll(a, s, 0), 1.0)
                    sb = jnp.where(keep, pltpu.roll(b, s, 0), 0.0)
                    b = b + a * sb
                    a = a * sa
                h = b + a * carry
                h_ref[sl, :] = h
                halves.append(h * _gelu(gate_ref[sl, :]))
                carry = h[SUBLANES - 1:SUBLANES, :]
            p_ref[pl.ds(base, rows16), :] = jnp.concatenate(halves, axis=0).astype(p_ref.dtype)
            return carry

        last = lax.fori_loop(0, nq, pair, carry_ref[0:1, :], unroll=2)
        carry_ref[...] = jnp.broadcast_to(last, carry_ref.shape)

    tile = pl.BlockSpec((tm, cw), lambda j, i: (i, j))
    gate_blk = gate_off // cw
    return pl.pallas_call(
        body, name=name, grid=(C // cw, T // tm),
        in_specs=[pl.BlockSpec((tm, cw), lambda j, i: (i, gate_blk + j)),
                  pl.BlockSpec((tm, cw), lambda j, i: (i, 2 * j)), pl.BlockSpec((tm, cw), lambda j, i: (i, 2 * j + 1)),
                  tile, pl.BlockSpec((1, cw), lambda j, i: (0, j))],
        out_specs=[tile, tile, tile, tile],
        out_shape=[jax.ShapeDtypeStruct((T, C), F32), jax.ShapeDtypeStruct((T, C), BF16),
                   jax.ShapeDtypeStruct((T, C), F32), jax.ShapeDtypeStruct((T, C), F32)],
        scratch_shapes=[pltpu.VMEM((SUBLANES, cw), F32)],
        compiler_params=_params(("parallel", "arbitrary"), 16 * tm * cw * 4),
    )(z, ri, ri, xc, sp)


def _rg_scan_bwd(dh, h, ri, xc, a_fwd, m_fwd, sp, *, T, C, tm=512, cw=256, name):
    ng, hb, nt = tm // SUBLANES, tm // SUBLANES, T // tm

    def body(dh_ref, h_ref, hp_ref, r_ref, i_ref, xc_ref, a_ref, m_ref, sp_ref,
             drai_ref, dxc_ref, crai_ref, csp_ref, cg_ref, ca_ref):
        step = pl.program_id(1)

        @pl.when(step == 0)
        def _():
            cg_ref[...] = jnp.zeros_like(cg_ref)
            ca_ref[...] = jnp.zeros_like(ca_ref)

        rows = _rows8(cw)
        sp_row = sp_ref[...]
        hhalo = jnp.where(step == nt - 1, 0.0, hp_ref[...])

        def group(gi, carry):
            g_next, a_next, s_ra, s_ia, s_sp = carry
            g = ng - 1 - gi
            off = pl.multiple_of(g * SUBLANES, SUBLANES)
            sl = pl.ds(off, SUBLANES)
            rr, ii, xx = r_ref[sl, :], i_ref[sl, :], xc_ref[sl, :]
            a, m = a_ref[sl, :], m_ref[sl, :]
            hh = h_ref[sl, :]
            hpv = h_ref[pl.ds(pl.multiple_of(jnp.maximum(off - SUBLANES, 0), SUBLANES), SUBLANES), :]
            hpv = jnp.where(g == 0, hhalo, hpv)
            hprev = _shift_down(hh, hpv, 1, rows)
            d = dh_ref[sl, :]
            c = jnp.where(rows < SUBLANES - 1, pltpu.roll(a, SUBLANES - 1, 0), a_next)
            for s in (1, 2, 4):
                keep = rows < SUBLANES - s
                sc = jnp.where(keep, pltpu.roll(c, SUBLANES - s, 0), 1.0)
                sd = jnp.where(keep, pltpu.roll(d, SUBLANES - s, 0), 0.0)
                d = d + c * sd
                c = c * sc
            gg = d + c * g_next
            da = gg * hprev
            dm = gg * (ii * xx)
            di = gg * (m * xx)
            dxc_ref[sl, :] = gg * (m * ii)
            dla = da * a - dm * (a * a / m)
            dra = dla * ((-RG_C) * sp_row) * (rr * (1.0 - rr))
            dia = di * (ii * (1.0 - ii))
            drai_ref[sl, pl.ds(0, cw)] = dra
            drai_ref[sl, pl.ds(cw, cw)] = dia
            return (gg[0:1, :], a[0:1, :], s_ra + dra, s_ia + dia, s_sp + dla * ((-RG_C) * rr))

        zero = jnp.zeros((SUBLANES, cw), F32)
        g_first, a_first, s_ra, s_ia, s_sp = lax.fori_loop(
            0, ng, group, (cg_ref[0:1, :], ca_ref[0:1, :], zero, zero, zero), unroll=2)
        cg_ref[...] = jnp.broadcast_to(g_first, cg_ref.shape)
        ca_ref[...] = jnp.broadcast_to(a_first, ca_ref.shape)
        for ref, acc in ((crai_ref.at[:, pl.ds(0, cw)], s_ra), (crai_ref.at[:, pl.ds(cw, cw)], s_ia), (csp_ref, s_sp)):
            _accumulate(ref, jnp.sum(acc, axis=0, keepdims=True), step == 0)

    tile = pl.BlockSpec((tm, cw), lambda j, i: (nt - 1 - i, j))
    wide = pl.BlockSpec((tm, 2 * cw), lambda j, i: (nt - 1 - i, j))
    vec = pl.BlockSpec((1, cw), lambda j, i: (0, j))
    return pl.pallas_call(
        body, name=name, grid=(C // cw, nt),
        in_specs=[tile, tile, pl.BlockSpec((SUBLANES, cw), lambda j, i: (jnp.maximum((nt - 1 - i) * hb - 1, 0), j)),
                  pl.BlockSpec((tm, cw), lambda j, i: (nt - 1 - i, 2 * j)),
                  pl.BlockSpec((tm, cw), lambda j, i: (nt - 1 - i, 2 * j + 1)), tile, tile, tile, vec],
        out_specs=[wide, tile, pl.BlockSpec((1, 2 * cw), lambda j, i: (0, j)), vec],
        out_shape=[jax.ShapeDtypeStruct((T, 2 * C), F32), jax.ShapeDtypeStruct((T, C), F32),
                   jax.ShapeDtypeStruct((1, 2 * C), F32), jax.ShapeDtypeStruct((1, C), F32)],
        scratch_shapes=[pltpu.VMEM((SUBLANES, cw), F32), pltpu.VMEM((SUBLANES, cw), F32)],
        compiler_params=_params(("parallel", "arbitrary"), 24 * tm * cw * 4),
    )(dh, h, h, ri, ri, xc, a_fwd, m_fwd, sp)


def _cscan_tables(lr, li, reverse):
    lam = (lr.reshape(-1), -li.reshape(-1) if reverse else li.reshape(-1))

    def mul(p, q):
        return p[0] * q[0] - p[1] * q[1], p[0] * q[1] + p[1] * q[0]

    pows = [lam]
    for _ in range(SUBLANES - 1):
        pows.append(mul(pows[-1], lam))
    zero = jnp.zeros_like(lam[0])
    tab = jnp.stack([pows[0][0], pows[0][1], pows[1][0], pows[1][1], pows[3][0], pows[3][1], zero, zero])
    if reverse:
        pows = pows[::-1]
    return tab, jnp.stack([p[0] for p in pows]), jnp.stack([p[1] for p in pows])


def _power_slabs(lr, li, n):
    pr, pi = lr.reshape(1, -1), li.reshape(1, -1)
    while pr.shape[0] < n:
        tr, ti = pr[-1:], pi[-1:]
        pr, pi = (jnp.concatenate([pr, pr * tr - pi * ti], axis=0), jnp.concatenate([pi, pr * ti + pi * tr], axis=0))
    return jnp.repeat(pr, SUBLANES, axis=0), jnp.repeat(pi, SUBLANES, axis=0), pr[-1], pi[-1]


def _rows_to_segments(src_ref, dst_ref):
    seg = src_ref.shape[0] // SUBLANES
    for g in range(seg):
        dst_ref[pl.ds(g * SUBLANES, SUBLANES), :] = src_ref[pl.ds(g, SUBLANES, stride=seg), :].astype(dst_ref.dtype)


def _segments_to_rows(src_ref, dst_ref):
    seg = src_ref.shape[0] // SUBLANES
    for r in range(SUBLANES):
        dst_ref[pl.ds(r * seg, seg), :] = src_ref[pl.ds(r, seg, stride=SUBLANES), :].astype(dst_ref.dtype)


def _seg_scan_tile(xr_ref, xi_ref, pbr_ref, pbi_ref, tab_ref, pwr_ref, pwi_ref, cr_ref, ci_ref, *, reverse, h=None):
    tm, cw = xr_ref.shape
    seg = tm // SUBLANES
    rows = _rows8(cw)
    sign = -1.0 if reverse else 1.0
    l_re, l_im = pbr_ref[0:1, :], sign * pbi_ref[0:1, :]

    def slab(g):
        return pl.ds(pl.multiple_of(g * SUBLANES, SUBLANES), SUBLANES)

    def local(k, state):
        sl = slab(seg - 1 - k if reverse else k)
        sr, si = state
        nr = xr_ref[sl, :] + (l_re * sr - l_im * si)
        ni = xi_ref[sl, :] + (l_re * si + l_im * sr)
        xr_ref[sl, :] = nr
        xi_ref[sl, :] = ni
        return nr, ni

    zero = jnp.zeros((SUBLANES, cw), F32)
    er, ei = lax.fori_loop(0, seg, local, (zero, zero), unroll=2)

    for k, s in enumerate((1, 2, 4)):
        shift = SUBLANES - s if reverse else s
        keep = rows < SUBLANES - s if reverse else rows >= s
        sr = jnp.where(keep, pltpu.roll(er, shift, 0), 0.0)
        si = jnp.where(keep, pltpu.roll(ei, shift, 0), 0.0)
        m_re, m_im = tab_ref[2 * k:2 * k + 1, :], tab_ref[2 * k + 1:2 * k + 2, :]
        er, ei = er + (m_re * sr - m_im * si), ei + (m_re * si + m_im * sr)
    cin_r, cin_i = cr_ref[0:1, :], ci_ref[0:1, :]
    pwr, pwi = pwr_ref[...], pwi_ref[...]
    er, ei = er + (pwr * cin_r - pwi * cin_i), ei + (pwr * cin_i + pwi * cin_r)
    if reverse:
        ent_r = jnp.where(rows == SUBLANES - 1, cin_r, pltpu.roll(er, SUBLANES - 1, 0))
        ent_i = jnp.where(rows == SUBLANES - 1, cin_i, pltpu.roll(ei, SUBLANES - 1, 0))
        out_r, out_i = er[0:1, :], ei[0:1, :]
    else:
        ent_r = jnp.where(rows == 0, cin_r, pltpu.roll(er, 1, 0))
        ent_i = jnp.where(rows == 0, cin_i, pltpu.roll(ei, 1, 0))
        out_r, out_i = er[SUBLANES - 1:SUBLANES, :], ei[SUBLANES - 1:SUBLANES, :]
    cr_ref[...] = jnp.broadcast_to(out_r, cr_ref.shape)
    ci_ref[...] = jnp.broadcast_to(out_i, ci_ref.shape)

    if h is not None:
        hr_ref, hi_ref, hr_last, hi_last = h
        hr_wrap = _shift_down(hr_ref[pl.ds(tm - SUBLANES, SUBLANES), :], hr_last, 1, rows)
        hi_wrap = _shift_down(hi_ref[pl.ds(tm - SUBLANES, SUBLANES), :], hi_last, 1, rows)

    def fix(g, sums):
        sl = slab(g)
        power = slab(seg - 1 - g) if reverse else sl
        pr, pi = pbr_ref[power, :], sign * pbi_ref[power, :]
        nr = xr_ref[sl, :] + (pr * ent_r - pi * ent_i)
        ni = xi_ref[sl, :] + (pr * ent_i + pi * ent_r)
        xr_ref[sl, :] = nr
        xi_ref[sl, :] = ni
        if h is None:
            return sums
        before = slab(jnp.maximum(g - 1, 0))
        hr1 = jnp.where(g == 0, hr_wrap, hr_ref[before, :])
        hi1 = jnp.where(g == 0, hi_wrap, hi_ref[before, :])
        return sums[0] + (nr * hr1 + ni * hi1), sums[1] + (ni * hr1 - nr * hi1)

    return lax.fori_loop(0, seg, fix, (zero, zero) if h is not None else (), unroll=2)


S5_TILE = 1024


def _s5_fwd(z, u_off, wb_re, wb_im, wc_re, wc_im_neg, d_row, powers, *, T, tm=S5_TILE, name):
    J, ku, kp = wb_re.shape
    nt = T // tm
    pb_re, pb_im, top_re, top_im = powers
    tab, pw_re, pw_im = _cscan_tables(top_re, top_im, False)
    u_blk = u_off // ku

    def body(u_ref, wbr_ref, wbi_ref, wcr_ref, wci_ref, d_ref, pbr_ref, pbi_ref, tab_ref, pwr_ref, pwi_ref,
             hr_ref, hi_ref, y_ref, yg_ref, cr_ref, ci_ref, us_ref, ys_ref):
        @pl.when(pl.program_id(1) == 0)
        def _():
            cr_ref[...] = jnp.zeros_like(cr_ref)
            ci_ref[...] = jnp.zeros_like(ci_ref)

        _rows_to_segments(u_ref, us_ref)
        u = us_ref[...]
        ub = u.astype(BF16)
        hr_ref[...] = jnp.dot(ub, wbr_ref[...], preferred_element_type=F32)
        hi_ref[...] = jnp.dot(ub, wbi_ref[...], preferred_element_type=F32)
        _seg_scan_tile(hr_ref, hi_ref, pbr_ref, pbi_ref, tab_ref, pwr_ref, pwi_ref, cr_ref, ci_ref, reverse=False)
        y = (jnp.dot(hr_ref[...].astype(BF16), wcr_ref[...], preferred_element_type=F32)
             + jnp.dot(hi_ref[...].astype(BF16), wci_ref[...], preferred_element_type=F32) + d_ref[...] * u)
        ys_ref[...] = y
        _segments_to_rows(ys_ref, y_ref)
        ys_ref[...] = _gelu(y)
        _segments_to_rows(ys_ref, yg_ref)

    wb_spec = pl.BlockSpec((None, ku, kp), lambda j, i: (j, 0, 0))
    wc_spec = pl.BlockSpec((None, kp, ku), lambda j, i: (j, 0, 0))
    small = pl.BlockSpec((SUBLANES, kp), lambda j, i: (0, j))
    slabs = pl.BlockSpec((tm, kp), lambda j, i: (0, j))
    state = pl.BlockSpec((tm, kp), lambda j, i: (i, j))
    chan = pl.BlockSpec((tm, ku), lambda j, i: (i, j))
    return pl.pallas_call(
        body, name=name, grid=(J, nt),
        in_specs=[pl.BlockSpec((tm, ku), lambda j, i: (i, u_blk + j)), wb_spec, wb_spec, wc_spec, wc_spec,
                  pl.BlockSpec((1, ku), lambda j, i: (0, j)), slabs, slabs, small, small, small],
        out_specs=[state, state, chan, chan],
        out_shape=[jax.ShapeDtypeStruct((T, J * kp), F32)] * 2
        + [jax.ShapeDtypeStruct((T, J * ku), F32), jax.ShapeDtypeStruct((T, J * ku), BF16)],
        scratch_shapes=[pltpu.VMEM((SUBLANES, kp), F32), pltpu.VMEM((SUBLANES, kp), F32),
                        pltpu.VMEM((tm, ku), F32), pltpu.VMEM((tm, ku), F32)],
        compiler_params=_params(("parallel", "arbitrary"), 14 * tm * kp * 4),
    )(z, wb_re, wb_im, wc_re, wc_im_neg, d_row, pb_re, pb_im, tab, pw_re, pw_im)


def _s5_bwd(dy, z, u_off, h_re, h_im, wb_re, wb_im, wc_re, wc_im_neg, d_row, powers, dz, *, T, tm=S5_TILE, name):
    J, ku, kp = wb_re.shape
    nt, hb = T // tm, tm // SUBLANES
    pb_re, pb_im, top_re, top_im = powers
    tab, pw_re, pw_im = _cscan_tables(top_re, top_im, True)
    u_blk = u_off // ku
    contract_rows = (((0,), (0,)), ((), ()))
    contract_cols = (((1,), (1,)), ((), ()))

    def body(dy_ref, u_ref, hr_ref, hrp_ref, hi_ref, hip_ref, wbr_ref, wbi_ref, wcr_ref, wci_ref, d_ref,
             pbr_ref, pbi_ref, tab_ref, pwr_ref, pwi_ref, dz_in_ref,
             du_ref, dlr_ref, dli_ref, dd_ref, dwbr_ref, dwbi_ref, dwcr_ref, dwci_ref,
             gr_ref, gi_ref, cr_ref, ci_ref, dys_ref, us_ref):
        step = pl.program_id(1)
        first = step == 0

        @pl.when(first)
        def _():
            cr_ref[...] = jnp.zeros_like(cr_ref)
            ci_ref[...] = jnp.zeros_like(ci_ref)

        _rows_to_segments(dy_ref, dys_ref)
        _rows_to_segments(u_ref, us_ref)
        dy_t, u = dys_ref[...], us_ref[...]
        dyb, ub = dy_t.astype(BF16), u.astype(BF16)
        gr_ref[...] = lax.dot_general(dyb, wcr_ref[...], contract_cols, preferred_element_type=F32)
        gi_ref[...] = lax.dot_general(dyb, wci_ref[...], contract_cols, preferred_element_type=F32)
        hr_last = jnp.where(step == nt - 1, 0.0, hrp_ref[...])
        hi_last = jnp.where(step == nt - 1, 0.0, hip_ref[...])
        s_re, s_im = _seg_scan_tile(gr_ref, gi_ref, pbr_ref, pbi_ref, tab_ref, pwr_ref, pwi_ref, cr_ref, ci_ref,
                                    reverse=True, h=(hr_ref, hi_ref, hr_last, hi_last))
        _accumulate(dlr_ref, jnp.sum(s_re, axis=0, keepdims=True), first)
        _accumulate(dli_ref, jnp.sum(s_im, axis=0, keepdims=True), first)
        grb, gib = gr_ref[...].astype(BF16), gi_ref[...].astype(BF16)
        du = (lax.dot_general(grb, wbr_ref[...], contract_cols, preferred_element_type=F32)
              + lax.dot_general(gib, wbi_ref[...], contract_cols, preferred_element_type=F32) + dy_t * d_ref[...])
        dys_ref[...] = du
        _segments_to_rows(dys_ref, du_ref)
        _accumulate(dd_ref, jnp.sum(dy_t * u, axis=0, keepdims=True), first)
        _accumulate(dwbr_ref, lax.dot_general(ub, grb, contract_rows, preferred_element_type=F32), first)
        _accumulate(dwbi_ref, lax.dot_general(ub, gib, contract_rows, preferred_element_type=F32), first)
        _accumulate(dwcr_ref, lax.dot_general(dyb, hr_ref[...].astype(BF16), contract_rows,
                                              preferred_element_type=F32), first)
        _accumulate(dwci_ref, lax.dot_general(dyb, hi_ref[...].astype(BF16), contract_rows,
                                              preferred_element_type=F32), first)

    def tix(i):
        return nt - 1 - i

    wb_spec = pl.BlockSpec((None, ku, kp), lambda j, i: (j, 0, 0))
    wc_spec = pl.BlockSpec((None, kp, ku), lambda j, i: (j, 0, 0))
    small = pl.BlockSpec((SUBLANES, kp), lambda j, i: (0, j))
    state = pl.BlockSpec((tm, kp), lambda j, i: (tix(i), j))
    halo = pl.BlockSpec((SUBLANES, kp), lambda j, i: (jnp.maximum(tix(i) * hb - 1, 0), j))
    chan = pl.BlockSpec((tm, ku), lambda j, i: (tix(i), j))
    svec = pl.BlockSpec((1, kp), lambda j, i: (0, j))
    cvec = pl.BlockSpec((1, ku), lambda j, i: (0, j))
    slabs = pl.BlockSpec((tm, kp), lambda j, i: (0, j))
    return pl.pallas_call(
        body, name=name, grid=(J, nt),
        in_specs=[chan, pl.BlockSpec((tm, ku), lambda j, i: (tix(i), u_blk + j)), state, halo, state, halo,
                  wb_spec, wb_spec, wc_spec, wc_spec, cvec, slabs, slabs, small, small, small, ANY],
        out_specs=[pl.BlockSpec((tm, ku), lambda j, i: (tix(i), u_blk + j)), svec, svec, cvec,
                   wb_spec, wb_spec, wb_spec, wb_spec],
        input_output_aliases={16: 0},
        out_shape=[jax.ShapeDtypeStruct(dz.shape, dz.dtype), jax.ShapeDtypeStruct((1, J * kp), F32),
                   jax.ShapeDtypeStruct((1, J * kp), F32), jax.ShapeDtypeStruct((1, J * ku), F32),
                   jax.ShapeDtypeStruct((J, ku, kp), F32), jax.ShapeDtypeStruct((J, ku, kp), F32),
                   jax.ShapeDtypeStruct((J, ku, kp), F32), jax.ShapeDtypeStruct((J, ku, kp), F32)],
        scratch_shapes=[pltpu.VMEM((tm, kp), F32), pltpu.VMEM((tm, kp), F32),
                        pltpu.VMEM((SUBLANES, kp), F32), pltpu.VMEM((SUBLANES, kp), F32),
                        pltpu.VMEM((tm, ku), F32), pltpu.VMEM((tm, ku), F32)],
        compiler_params=_params(("parallel", "arbitrary"), 16 * tm * kp * 4),
    )(dy, z, h_re, h_re, h_im, h_im, wb_re, wb_im, wc_re, wc_im_neg, d_row, pb_re, pb_im, tab, pw_re, pw_im, dz)


def _mesh_pos():
    return lax.axis_index("x"), lax.axis_index("y"), lax.axis_index("c")


def _dev_index(px, py, pc):
    return 4 * px + 2 * py + pc


HBM = pl.BlockSpec(memory_space=pltpu.HBM)
SEM = pl.BlockSpec(memory_space=pltpu.SEMAPHORE)
EFFECT = pltpu.SideEffectType.DATAFLOW_SIDE_EFFECTING
RELATIONS = [(dx, dy, dc) for dx in (0, 1) for dy in (0, 1) for dc in (0, 1) if (dx, dy, dc) != (0, 0, 0)]


def _peer(rel):
    x, y, c = _mesh_pos()
    dx, dy, dc = rel
    return (x + dx - 2 * x * dx, y + dy - 2 * y * dy, c + dc - 2 * c * dc)


CHIP_RELATIONS = [(1, 0, 0), (0, 1, 0), (1, 1, 0)]
EXCHANGE_PEERS = {"gather": RELATIONS, "scatter": RELATIONS, "own": [(0, 0, 1)] + CHIP_RELATIONS, "pass": CHIP_RELATIONS}


def _split_copy(src_ref, land_ref, send_sems, recv_sems, k, mode, incoming):
    x, y, c = _mesh_pos()
    me = _dev_index(x, y, c)
    peer = _peer(EXCHANGE_PEERS[mode][k])
    if mode == "pass":
        held, theirs = _dev_index(peer[0], peer[1], c), _dev_index(peer[0], peer[1], 1 - c)
        src, slot, target = land_ref.at[held], theirs if incoming else held, (x, y, 1 - c)
    else:
        src = src_ref.at[_dev_index(*peer)] if mode == "scatter" else src_ref
        slot, target = _dev_index(*peer) if incoming else me, peer
    return pltpu.make_async_remote_copy(src_ref=src, dst_ref=land_ref.at[slot], send_sem=send_sems.at[k],
                                        recv_sem=recv_sems.at[k], device_id=target, device_id_type=MESH)


def _exchange_start(srcs, lands, *, mode, after=None, name):
    n = len(srcs)
    n_after = 0 if after is None else 1
    n_rel = len(EXCHANGE_PEERS[mode])

    def body(*refs):
        src_refs, land_refs = refs[:n], refs[n:2 * n]
        first_out = 2 * n + n_after
        send, recv = refs[first_out:first_out + n], refs[first_out + n:first_out + 2 * n]
        token = refs[-1]
        for k in range(n_rel):
            for a in range(n):
                _split_copy(src_refs[a], land_refs[a], send[a], recv[a], k, mode, incoming=False).start()
        token[...] = jnp.zeros_like(token)

    outs = pl.pallas_call(
        body, name=name, in_specs=[HBM] * (2 * n) + [ANY] * n_after,
        out_shape=[pltpu.SemaphoreType.DMA((n_rel,))] * (2 * n)
        + [pltpu.HBM(s.shape, s.dtype) for s in srcs] + [pltpu.HBM(s.shape, s.dtype) for s in lands]
        + [jax.ShapeDtypeStruct((SUBLANES, LANES), F32)],
        out_specs=[SEM] * (2 * n) + [HBM] * (2 * n) + [pl.BlockSpec(memory_space=pltpu.VMEM)],
        input_output_aliases={**{a: 2 * n + a for a in range(n)}, **{n + a: 3 * n + a for a in range(n)}},
        compiler_params=pltpu.CompilerParams(has_side_effects=EFFECT),
    )(*[pltpu.with_memory_space_constraint(s, pltpu.HBM) for s in srcs],
      *[pltpu.with_memory_space_constraint(s, pltpu.HBM) for s in lands], *([after] if n_after else []))
    per_array = [(outs[a], outs[n + a], outs[2 * n + a], outs[3 * n + a]) for a in range(n)]
    return per_array, outs[-1]


def _exchange_wait(handle, after, *, mode, name):
    send_sems, recv_sems, src_thru, land_thru = handle
    after = after if isinstance(after, (tuple, list)) else (after,)

    def body(src_ref, land_ref, send, recv, *rest):
        for k in range(len(EXCHANGE_PEERS[mode])):
            cp = _split_copy(src_ref, land_ref, send, recv, k, mode, incoming=True)
            cp.wait_send()
            cp.wait_recv()

    return pl.pallas_call(
        body, name=name, in_specs=[HBM, HBM, SEM, SEM] + [ANY] * len(after),
        out_shape=[pltpu.HBM(src_thru.shape, src_thru.dtype), pltpu.HBM(land_thru.shape, land_thru.dtype)],
        out_specs=[HBM, HBM], input_output_aliases={0: 0, 1: 1},
        compiler_params=pltpu.CompilerParams(has_side_effects=EFFECT),
    )(src_thru, land_thru, send_sems, recv_sems, *after)[1]


def _landing_zone(own_block):
    me = _dev_index(*_mesh_pos())
    zone = lax.empty((N_DEV,) + own_block.shape, own_block.dtype)
    return lax.dynamic_update_index_in_dim(zone, own_block, me, 0)


def _row_tile(rows, want):
    t = min(want, rows) // SUBLANES * SUBLANES
    while rows % t:
        t -= SUBLANES
    return t


def _sum_slots(recv, *, tr, name):
    s_, r_, c_ = recv.shape
    tr = _row_tile(r_, tr)

    def body(g_ref, o_ref):
        acc = g_ref[0]
        for s in range(1, s_):
            acc = acc + g_ref[s]
        o_ref[...] = acc

    return pl.pallas_call(
        body, name=name, grid=(r_ // tr,),
        in_specs=[pl.BlockSpec((s_, tr, c_), lambda i: (0, i, 0))],
        out_specs=pl.BlockSpec((tr, c_), lambda i: (i, 0)),
        out_shape=jax.ShapeDtypeStruct((r_, c_), F32),
        compiler_params=_params(("parallel",), (2 * s_ + 3) * tr * c_ * 4),
    )(recv)


def _adamw(recv, w, m, v, *, tr, name):
    s_, r_, c_ = recv.shape
    tr = _row_tile(r_, tr)
    assert w.shape == (r_, c_), (name, w.shape, recv.shape)
    c1 = 1.0 - ADAM_B1 ** ADAM_STEP
    c2 = 1.0 - ADAM_B2 ** ADAM_STEP

    def body(g_ref, w_ref, m_ref, v_ref, go_ref, d_ref, mo_ref, vo_ref):
        g = g_ref[0].astype(F32)
        for s in range(1, s_):
            g = g + g_ref[s].astype(F32)
        mn = ADAM_B1 * m_ref[...] + (1.0 - ADAM_B1) * g
        vn = ADAM_B2 * v_ref[...] + (1.0 - ADAM_B2) * (g * g)
        go_ref[...] = g
        mo_ref[...] = mn
        vo_ref[...] = vn
        d_ref[...] = -ADAM_LR * ((mn / c1) / (jnp.sqrt(vn / c2) + ADAM_EPS) + ADAM_WD * w_ref[...])

    tile = pl.BlockSpec((tr, c_), lambda i: (i, 0))
    return pl.pallas_call(
        body, name=name, grid=(r_ // tr,),
        in_specs=[pl.BlockSpec((s_, tr, c_), lambda i: (0, i, 0)), tile, tile, tile],
        out_specs=[tile] * 4, out_shape=[jax.ShapeDtypeStruct((r_, c_), F32)] * 4,
        compiler_params=_params(("parallel",), (2 * s_ + 16) * tr * c_ * 4),
    )(recv, w, m, v)


def _adamw_whole(gs, ws, ms, vs, *, name):
    n = len(gs)
    c1 = 1.0 - ADAM_B1 ** ADAM_STEP
    c2 = 1.0 - ADAM_B2 ** ADAM_STEP

    def body(*refs):
        for i in range(n):
            g, w = refs[i][...], refs[n + i][...]
            mn = ADAM_B1 * refs[2 * n + i][...] + (1.0 - ADAM_B1) * g
            vn = ADAM_B2 * refs[3 * n + i][...] + (1.0 - ADAM_B2) * (g * g)
            refs[4 * n + 3 * i][...] = -ADAM_LR * ((mn / c1) / (jnp.sqrt(vn / c2) + ADAM_EPS) + ADAM_WD * w)
            refs[4 * n + 3 * i + 1][...] = mn
            refs[4 * n + 3 * i + 2][...] = vn

    whole = pl.BlockSpec(memory_space=pltpu.VMEM)
    lane_padded = sum(math.prod(g.shape[:-1]) * (-(-g.shape[-1] // LANES) * LANES) for g in gs)
    outs = pl.pallas_call(
        body, name=name, in_specs=[whole] * (4 * n), out_specs=[whole] * (3 * n),
        out_shape=[jax.ShapeDtypeStruct(g.shape, F32) for g in gs for _ in range(3)],
        compiler_params=pltpu.CompilerParams(vmem_limit_bytes=int(min(max(16 * lane_padded * 4, 16 * 2 ** 20), VMEM_CAP))),
    )(*gs, *ws, *ms, *vs)
    return [tuple(outs[3 * i:3 * i + 3]) for i in range(n)]


def _s5_discretise(a_re, a_im, log_dt, b_re, b_im):
    dt = jnp.exp(log_dt)[:, None]
    lr = jnp.minimum(a_re, -1e-4)
    li = a_im
    mag = jnp.exp(lr * dt)
    lbr = mag * jnp.cos(li * dt)
    lbi = mag * jnp.sin(li * dt)
    zr, zi = lbr - 1.0, lbi
    den = lr * lr + li * li
    fr = (zr * lr + zi * li) / den
    fi = (zi * lr - zr * li) / den
    bbr = fr[..., None] * b_re - fi[..., None] * b_im
    bbi = fr[..., None] * b_im + fi[..., None] * b_re
    return lbr, lbi, bbr, bbi


def _softplus_neg(lam):
    return jnp.maximum(-lam, 0.0) + jnp.log(1.0 + jnp.exp(-jnp.abs(lam)))


S5_Q = 8
RG_Q = 2


def _local_step(x, tgt, W, comm):
    T, D = x.shape
    C = D
    G, P, H = W["ssm_b_re"].shape
    S = G * H
    F = W["mlp_b_up"].shape[1]
    n_in = 2 * C + S + 2 * D
    heads, hd = W["rg_wa"].shape[0], W["rg_wa"].shape[1]
    u_off, ga_off, gb_off = 2 * C, 2 * C + S, 2 * C + S + D

    if comm.first_token is not None:
        anchored = ("rg_lambda", "ssm_a_re", "rg_wa", "rg_wx", "ssm_c_re", "ssm_c_im")
        W = {**W, **{k: W[k] + comm.first_token[0, 0] for k in anchored}}
    sp, sp_vjp = jax.vjp(_softplus_neg, W["rg_lambda"])
    (lbr, lbi, bbr, bbi), s5_vjp = jax.vjp(_s5_discretise, W["ssm_a_re"], W["ssm_a_im"], W["ssm_log_dt"],
                                           W["ssm_b_re"], W["ssm_b_im"])
    lam_re, lam_im = lbr.reshape(-1), lbi.reshape(-1)
    jr, kr = heads // RG_Q, RG_Q * hd
    w_ri = jnp.concatenate([_bd_pack(W["rg_wa"], RG_Q), _bd_pack(W["rg_wx"], RG_Q)], axis=2).astype(BF16)
    b_ri = jnp.concatenate([W["rg_ba"].reshape(jr, kr), W["rg_bx"].reshape(jr, kr)], axis=1).reshape(1, -1)
    wb_re = _bd_pack(jnp.swapaxes(bbr, 1, 2), S5_Q).astype(BF16)
    wb_im = _bd_pack(jnp.swapaxes(bbi, 1, 2), S5_Q).astype(BF16)
    wc_re = _bd_pack(jnp.swapaxes(W["ssm_c_re"], 1, 2), S5_Q).astype(BF16)
    wc_im_neg = _bd_pack(jnp.swapaxes(-W["ssm_c_im"], 1, 2), S5_Q).astype(BF16)
    d_row = W["ssm_d"].reshape(1, S)
    powers = _power_slabs(lam_re, lam_im, S5_TILE // SUBLANES)

    x_bf = x.astype(BF16) if comm.first_token is None else (x + comm.first_token[0, 0]).astype(BF16)
    w_in, conv_w = comm.first_weights((x_bf, w_ri, wb_re, wb_im, wc_re, wc_im_neg, powers[0], powers[1]))
    z = _mm(x_bf, w_in, M=T, N=n_in, K=D, tm=1024, tn=n_in // 4, tk=D, after=comm.gather_token, name="fwd_in_proj")
    started = comm.start_weights(("mlp_w_up",), z)
    xc = _conv_fwd(z, conv_w, W["conv_b"], T=T, C=C, after=started, name="fwd_conv")
    ri = _bd([(xc, 0, w_ri)], T=T, J=jr, kb=kr, nb=2 * kr, extras=[(b_ri, "vec", 0)],
             epilogue=lambda acc, b: (_sig(acc + b),), name="fwd_gates")
    h, p, a_fwd, m_fwd = _rg_scan_fwd(z, ri, xc, sp, T=T, C=C, gate_off=C, cw=kr, name="fwd_rg_scan")
    w_a_out = comm.weight("w_a_out", p)
    started = comm.start_weights(("mlp_w_down",), p)
    y_a = _mm(p, w_a_out, M=T, N=D, K=C, out_dtypes=(BF16,), tm=512, tn=D, tk=C, after=started, name="fwd_rg_out")

    h_re, h_im, y_s, yg = _s5_fwd(z, u_off, wb_re, wb_im, wc_re, wc_im_neg, d_row, powers, T=T, name="fwd_s5")
    w_glu_w, w_glu_v = comm.weight("glu_w", yg), comm.weight("glu_v", yg)
    glu_a = _mm(yg, w_glu_w, M=T, N=D, K=S, out_dtypes=(BF16,), tm=1024, tn=D, tk=S, name="fwd_glu_w")
    cwm = 1024

    def mix_fn(b, ga, gb, ya, a):
        return b, _sig(ga) * ya.astype(F32) + _sig(gb) * (a.astype(F32) * _sig(b))

    glu_b, mix = _mm(yg, w_glu_v, M=T, N=D, K=S, tm=512, tn=cwm, tk=S,
                     extras=[(z, "mn", ga_off // cwm), (z, "mn", gb_off // cwm), (y_a, "mn"), (glu_a, "mn")],
                     epilogue=mix_fn, n_out=2, out_dtypes=(BF16, BF16), name="fwd_glu_v_mix")
    w_out = comm.weight("w_out", mix)
    def out_ln1_fn(acc, xv, g, b):
        s = ALPHA * xv + acc
        xhat, _ = _ln_stats(s)
        y = xhat * g + b
        return s, y, y

    s1, x1, x1_bf = _mm(mix, w_out, M=T, N=D, K=D, tm=256, tn=D, tk=D,
                        extras=[(x, "mn"), (W["ln1_g"], "n"), (W["ln1_b"], "n")], epilogue=out_ln1_fn, n_out=3,
                        out_dtypes=(F32, F32, BF16), name="fwd_out_proj_ln1")
    w_up = comm.weight("mlp_w_up", x1_bf)

    def mlp_up_fn(acc, b):
        hp = acc + b
        rl = jnp.maximum(hp, 0.0)
        return rl * rl, hp

    hact, hpre = _mm(x1_bf, w_up, M=T, N=F, K=D, tm=1024, tn=1024, tk=D, extras=[(W["mlp_b_up"], "n")],
                     epilogue=mlp_up_fn, n_out=2, out_dtypes=(BF16, BF16), name="fwd_mlp_up")
    w_down = comm.weight("mlp_w_down", hact)
    s2 = _mm(hact, w_down, M=T, N=D, K=F, tm=1024, tn=1024, tk=2048,
             extras=[(x1, "mn"), (W["mlp_b_down"], "n")], epilogue=lambda acc, xv, b: (ALPHA * xv + acc + b,),
             name="fwd_mlp_down")

    def ln2_fn(s, t, g, b):
        xhat, rstd = _ln_stats(s)
        err = xhat * g + b - t
        dy = err * (1.0 / D)
        ds = _ln_bwd(dy, g, xhat, rstd)
        return ds, ds, 0.5 * dy * err, dy * xhat, dy, ds

    ds2, ds2_bf, loss_cols, d_ln2_g, d_ln2_b, d_b_down = _ew(
        ln2_fn, [(s2, "tile", 0), (tgt, "tile", 0), (W["ln2_g"], "vec", 0), (W["ln2_b"], "vec", 0)],
        T=T, C=D, n_out=2, n_cs=4, out_dtypes=(F32, BF16), tm=256, name="bwd_loss_ln2")
    d_w_down = _mm(hact, ds2_bf, M=F, N=D, K=T, ta=True, out_dtypes=(BF16,), tm=1024, tn=1024, tk=4096, name="bwd_w_down")
    sent = comm.send_grad("mlp_w_down", d_w_down)

    def dhpre_fn(acc, hp):
        dv = acc * (2.0 * jnp.maximum(hp.astype(F32), 0.0))
        return dv, dv

    dhpre, d_b_up = _mm(ds2_bf, w_down, M=T, N=F, K=D, tb=True, tm=1024, tn=1024, tk=D, extras=[(hpre, "mn")],
                        epilogue=dhpre_fn, n_cs=1, out_dtypes=(BF16,), after=sent, name="bwd_mlp_down")
    d_w_up = _mm(x1_bf, dhpre, M=D, N=F, K=T, ta=True, out_dtypes=(BF16,), n_split=N_DEV, tm=1024, tn=F // N_DEV, tk=4096, name="bwd_w_up")
    sent = comm.send_grad("mlp_w_up", d_w_up)
    dx1 = _mm(dhpre, w_up, M=T, N=D, K=F, tb=True, tm=1024, tn=1024, tk=2048,
              extras=[(ds2, "mn")], epilogue=lambda acc, dv: (ALPHA * dv + acc,), after=sent, name="bwd_mlp_up")

    def ln1_bwd_fn(s, dy, g):
        xhat, rstd = _ln_stats(s)
        ds = _ln_bwd(dy, g, xhat, rstd)
        return ds, ds, dy * xhat, dy

    ds1, ds1_bf, d_ln1_g, d_ln1_b = _ew(ln1_bwd_fn, [(s1, "tile", 0), (dx1, "tile", 0), (W["ln1_g"], "vec", 0)],
                                        T=T, C=D, n_out=2, n_cs=2, out_dtypes=(F32, BF16), tm=256, name="bwd_ln1")
    d_w_out = _mm(mix, ds1_bf, M=D, N=D, K=T, ta=True, out_dtypes=(BF16,), tm=1024, tn=1024, tk=4096, name="bwd_w_out")
    sent = comm.send_grad("w_out", d_w_out)
    def mix_bwd_fn(dm, ga, gb, ya, a, b):
        ya, a, b = ya.astype(F32), a.astype(F32), b.astype(F32)
        sa, sb, sv = _sig(ga), _sig(gb), _sig(b)
        yb = a * sv
        dyb = dm * sb
        return (dm * ya * (sa * (1.0 - sa)), dm * yb * (sb * (1.0 - sb)), dm * sa, dyb * sv,
                dyb * a * (sv * (1.0 - sv)))

    dz = lax.empty((T, n_in), BF16)
    dz, dg_b, dy_a, dglu_a, dglu_b = _mm(
        ds1_bf, w_out, M=T, N=D, K=D, tb=True, tm=512, tn=cwm, tk=D,
        extras=[(z, "mn", ga_off // cwm), (z, "mn", gb_off // cwm), (y_a, "mn"), (glu_a, "mn"), (glu_b, "mn")],
        epilogue=mix_bwd_fn, n_out=5, out_dtypes=(BF16,) * 5, after=sent, into=(dz, 0, ga_off // cwm),
        name="bwd_out_proj_mix")
    dz = lax.dynamic_update_slice(dz, dg_b, (0, gb_off))

    d_w_a_out = _mm(p, dy_a, M=C, N=D, K=T, ta=True, out_dtypes=(BF16,), tm=1024, tn=1024, tk=4096, name="bwd_w_a_out")
    sent = comm.send_grad("w_a_out", d_w_a_out)
    def dp_fn(dp, hv, gate):
        th = jnp.tanh(GELU_C * (gate + GELU_K * gate * gate * gate))
        gelu = 0.5 * gate * (1.0 + th)
        dgelu = 0.5 * (1.0 + th) + 0.5 * gate * (1.0 - th * th) * (GELU_C * (1.0 + 3.0 * GELU_K * gate * gate))
        return dp * gelu, dp * hv * dgelu

    dh, dz = _mm(dy_a, w_a_out, M=T, N=C, K=D, tb=True, tm=256, tn=C, tk=D, extras=[(h, "mn"), (z, "mn", 1)],
                 epilogue=dp_fn, n_out=2, out_dtypes=(F32, BF16), after=sent, into=(dz, 1, 1), name="bwd_rg_out")
    drai, dxc0, d_b_ri, d_sp = _rg_scan_bwd(dh, h, ri, xc, a_fwd, m_fwd, sp, T=T, C=C, cw=kr, name="bwd_rg_scan")
    dxc = _bd([(drai, 0, w_ri)], T=T, J=jr, kb=2 * kr, nb=kr, tw=True, extras=[(dxc0, "tile", 0)],
              epilogue=lambda acc, d0: (acc + d0,), name="bwd_gates")
    d_w_ri = _bdw(xc, 0, drai, 0, T=T, J=jr, kb=kr, nb=2 * kr, name="bwd_w_gates")
    d_wa, d_wx = _bd_unpack(d_w_ri[:, :, :kr], RG_Q), _bd_unpack(d_w_ri[:, :, kr:], RG_Q)
    d_b_ri = d_b_ri.reshape(jr, 2 * kr)
    d_ba, d_bx = d_b_ri[:, :kr].reshape(1, -1), d_b_ri[:, kr:].reshape(1, -1)
    dz, conv_sums = _conv_bwd(dxc, z, conv_w, dz, T=T, C=C, name="bwd_conv")
    d_conv_w, d_conv_b = conv_sums[0:4], conv_sums[4:5]
    (d_lambda,) = sp_vjp(d_sp)

    d_glu_w = _mm(yg, dglu_a, M=S, N=D, K=T, ta=True, out_dtypes=(BF16,), n_split=N_DEV, tm=1024, tn=D // N_DEV, tk=4096, name="bwd_w_glu_w")
    d_glu_v = _mm(yg, dglu_b, M=S, N=D, K=T, ta=True, out_dtypes=(BF16,), n_split=N_DEV, tm=1024, tn=D // N_DEV, tk=4096, name="bwd_w_glu_v")
    sent = comm.send_grad("glu_w", d_glu_w, "glu_v", d_glu_v)
    dyg0 = _mm(dglu_a, w_glu_w, M=T, N=S, K=D, tb=True, tm=512, tn=S, tk=D, after=sent, name="bwd_glu_w")
    dy_s = _mm(dglu_b, w_glu_v, M=T, N=S, K=D, tb=True, tm=512, tn=S, tk=D,
               extras=[(dyg0, "mn"), (y_s, "mn")], epilogue=lambda acc, d0, yv: ((acc + d0) * _dgelu(yv),),
               name="bwd_glu_v")
    dz, d_lbr, d_lbi, d_ssm_d, d_wb_re, d_wb_im, d_wc_re, d_wc_im_neg = _s5_bwd(
        dy_s, z, u_off, h_re, h_im, wb_re, wb_im, wc_re, wc_im_neg, d_row, powers, dz, T=T, name="bwd_s5")
    d_bbr = jnp.swapaxes(_bd_unpack(d_wb_re, S5_Q), 1, 2)
    d_bbi = jnp.swapaxes(_bd_unpack(d_wb_im, S5_Q), 1, 2)
    d_a_re, d_a_im, d_log_dt, d_b_re, d_b_im = s5_vjp((d_lbr.reshape(G, P), d_lbi.reshape(G, P), d_bbr, d_bbi))
    d_c_re = _bd_unpack(d_wc_re, S5_Q)
    d_c_im = -_bd_unpack(d_wc_im_neg, S5_Q)

    grads = dict(
        conv_w=d_conv_w, conv_b=d_conv_b, rg_wa=d_wa, rg_ba=d_ba, rg_wx=d_wx, rg_bx=d_bx,
        rg_lambda=d_lambda, ssm_a_re=d_a_re, ssm_a_im=d_a_im, ssm_log_dt=d_log_dt,
        ssm_b_re=d_b_re, ssm_b_im=d_b_im, ssm_c_re=d_c_re, ssm_c_im=d_c_im, ssm_d=d_ssm_d.reshape(G, H),
        ln1_g=d_ln1_g, ln1_b=d_ln1_b, mlp_b_up=d_b_up, mlp_b_down=d_b_down, ln2_g=d_ln2_g, ln2_b=d_ln2_b)
    sent = comm.send_small(grads)

    d_w_in = _mm(x_bf, dz, M=D, N=n_in, K=T, ta=True, out_dtypes=(BF16,), n_split=N_DEV, tm=1024, tn=n_in // N_DEV,
                 tk=4096, after=sent, name="bwd_w_in")
    sent = comm.send_grad("w_in", d_w_in)
    grad_x = _mm(dz, w_in, M=T, N=D, K=n_in, tb=True, tm=1024, tn=1024, tk=n_in // 4,
                 extras=[(ds1, "mn")], epilogue=lambda acc, dv: (ALPHA * dv + acc,), after=sent, name="bwd_in_proj")
    return jnp.sum(loss_cols), grad_x, grads


BIG = ("w_in", "w_a_out", "glu_w", "glu_v", "w_out", "mlp_w_up", "mlp_w_down")
COL_SHARDED = ("w_in", "glu_w", "glu_v", "mlp_w_up")
SMALL = ("conv_w", "conv_b", "rg_wa", "rg_ba", "rg_wx", "rg_bx", "rg_lambda", "ssm_a_re", "ssm_a_im", "ssm_log_dt",
         "ssm_b_re", "ssm_b_im", "ssm_c_re", "ssm_c_im", "ssm_d", "ln1_g", "ln1_b", "mlp_b_up", "mlp_b_down", "ln2_g",
         "ln2_b")
ORDER = ("w_in", "conv_w", "conv_b", "rg_wa", "rg_ba", "rg_wx", "rg_bx", "rg_lambda", "w_a_out", "ssm_a_re",
         "ssm_a_im", "ssm_log_dt", "ssm_b_re", "ssm_b_im", "ssm_c_re", "ssm_c_im", "ssm_d", "glu_w", "glu_v", "w_out",
         "ln1_g", "ln1_b", "mlp_w_up", "mlp_b_up", "mlp_w_down", "mlp_b_down", "ln2_g", "ln2_b")
TILE_ELEMS = SUBLANES * LANES


def _pack(arrs):
    pieces = []
    for a in arrs:
        flat = a.reshape(-1)
        flat = jnp.pad(flat, (0, (-flat.shape[0]) % TILE_ELEMS))
        pieces.append(flat.reshape(-1, LANES))
    rows = sum(p.shape[0] for p in pieces)
    pad_rows = (-rows) % (N_DEV * SUBLANES)
    if pad_rows:
        pieces.append(jnp.zeros((pad_rows, LANES), pieces[0].dtype))
    return jnp.concatenate(pieces, axis=0)


def _unpack(packed, shapes):
    out, row = [], 0
    for shp in shapes:
        n = math.prod(shp)
        rows = -(-n // TILE_ELEMS) * SUBLANES
        out.append(packed[row:row + rows].reshape(-1)[:n].reshape(shp))
        row += rows
    return out


class _Comm:
    def __init__(self, w):
        first = [w["w_in"].astype(BF16), w["conv_w"]]
        self._first, self.first_token = _exchange_start(first, [_landing_zone(s) for s in first], mode="own",
                                                        name="gather_in_start")
        self._shards = {k: w[k].astype(BF16) for k in BIG if k != "w_in"}
        self._weights, self._gathers, self._grads = {}, {}, {}

    def first_weights(self, after):
        lands = [_exchange_wait(h, after, mode="own", name="gather_in_wait_%d" % i) for i, h in enumerate(self._first)]
        unused = [lax.empty((2 * SUBLANES, LANES), BF16) for _ in lands]
        handles, passed = _exchange_start(unused, lands, mode="pass", name="gather_in_pass")
        w_in, taps = [_exchange_wait(h, passed, mode="pass", name="gather_in_got_%d" % i) for i, h in enumerate(handles)]
        self._weights["w_in"] = w_in
        self.gather_token = self.start_weights(("w_a_out", "glu_w", "glu_v", "w_out"), w_in)
        return self.weight("w_in", None), jnp.swapaxes(taps, 0, 1).reshape(taps.shape[1], -1)

    def start_weights(self, names, after):
        shards = [self._shards.pop(k) for k in names]
        handles, token = _exchange_start(shards, [_landing_zone(s) for s in shards], mode="gather", after=after,
                                         name="gather_start_" + names[0])
        self._gathers.update(zip(names, handles))
        return token

    def weight(self, k, after):
        if k not in self._weights:
            self._weights[k] = _exchange_wait(self._gathers.pop(k), after, mode="gather", name="gather_wait_" + k)
        gk = self._weights[k]
        if k in COL_SHARDED:
            return jnp.swapaxes(gk, 0, 1).reshape(gk.shape[1], -1)
        return gk.reshape(-1, gk.shape[-1])

    def send_grad(self, *names_and_parts):
        names, parts = names_and_parts[0::2], names_and_parts[1::2]
        parts = [p if k in COL_SHARDED else p.reshape(N_DEV, p.shape[0] // N_DEV, p.shape[1])
                 for k, p in zip(names, parts)]
        me = _dev_index(*_mesh_pos())
        lands = [_landing_zone(lax.dynamic_index_in_dim(p, me, 0, keepdims=False)) for p in parts]
        handles, token = _exchange_start(parts, lands, mode="scatter", name="grad_start_" + names[0])
        self._grads.update(zip(names, handles))
        return token

    def received_grad(self, k, after):
        return _exchange_wait(self._grads.pop(k), after, mode="scatter", name="grad_wait_" + k)

    def send_small(self, grads):
        return self.send_grad("small", _pack([grads[k] for k in SMALL]))

    def all_reduced_small(self, after, behind):
        recv = self.received_grad("small", after)
        block = _sum_slots(recv, tr=512, name="sum_small_grads")
        (handle,), started = _exchange_start([block], [_landing_zone(block)], mode="gather", name="small_sum_start")
        done = behind(started)
        return _exchange_wait(handle, done, mode="gather", name="small_sum_wait").reshape(-1, LANES)


SMALL_GROUPS = (("rg_wa", "rg_wx"), ("ssm_b_re",), ("ssm_b_im",),
                tuple(k for k in SMALL if k not in ("rg_wa", "rg_wx", "ssm_b_re", "ssm_b_im")))


def _step(x, tgt, w, m, v, raw_w, raw_m, raw_v):
    dev = _dev_index(*_mesh_pos())

    comm = _Comm(w)
    small = dict(w)
    for k in ("conv_b", "rg_ba", "rg_bx", "rg_lambda", "ln1_g", "ln1_b", "mlp_b_up", "mlp_b_down", "ln2_g", "ln2_b"):
        small[k] = w[k].reshape(1, -1)

    loss_part, grad_x, grads = _local_step(x, tgt, small, comm)

    out_g, out_d, out_m, out_v = {}, {}, {}, {}

    def update_large(started):
        for k in BIG:
            rk = comm.received_grad(k, (grad_x, started))
            out_g[k], out_d[k], out_m[k], out_v[k] = _adamw(rk, w[k], m[k], v[k], tr=256 if k in COL_SHARDED else 128,
                                                                 name="adamw_" + k)
        return out_v[BIG[-1]]

    small_all = comm.all_reduced_small(grad_x, update_large)
    g_small = dict(zip(SMALL, _unpack(small_all, [grads[k].shape for k in SMALL])))
    cw_cols = w["conv_w"].shape[1]
    g_small["conv_w"] = lax.dynamic_slice_in_dim(g_small["conv_w"], dev * cw_cols, cw_cols, axis=1)
    for group in SMALL_GROUPS:
        gs = [g_small[k].reshape(raw_w[k].shape) for k in group]
        res = _adamw_whole(gs, [raw_w[k] for k in group], [raw_m[k] for k in group], [raw_v[k] for k in group],
                           name="adamw_" + group[0])
        for k, gk, (dk, mk, vk) in zip(group, gs, res):
            out_g[k], out_d[k], out_m[k], out_v[k] = gk, dk, mk, vk

    loss = lax.psum(loss_part, ("x", "y", "c"))
    return loss, grad_x, out_g, out_d, out_m, out_v


def kernel(x, w_in, conv_w, conv_b, rg_wa, rg_ba, rg_wx, rg_bx, rg_lambda, w_a_out, ssm_a_re, ssm_a_im, ssm_log_dt, ssm_b_re, ssm_b_im, ssm_c_re, ssm_c_im, ssm_d, glu_w, glu_v, w_out, ln1_g, ln1_b, mlp_w_up, mlp_b_up, mlp_w_down, mlp_b_down, ln2_g, ln2_b, loss_target, m_w_in, m_conv_w, m_conv_b, m_rg_wa, m_rg_ba, m_rg_wx, m_rg_bx, m_rg_lambda, m_w_a_out, m_ssm_a_re, m_ssm_a_im, m_ssm_log_dt, m_ssm_b_re, m_ssm_b_im, m_ssm_c_re, m_ssm_c_im, m_ssm_d, m_glu_w, m_glu_v, m_w_out, m_ln1_g, m_ln1_b, m_mlp_w_up, m_mlp_b_up, m_mlp_w_down, m_mlp_b_down, m_ln2_g, m_ln2_b, v_w_in, v_conv_w, v_conv_b, v_rg_wa, v_rg_ba, v_rg_wx, v_rg_bx, v_rg_lambda, v_w_a_out, v_ssm_a_re, v_ssm_a_im, v_ssm_log_dt, v_ssm_b_re, v_ssm_b_im, v_ssm_c_re, v_ssm_c_im, v_ssm_d, v_glu_w, v_glu_v, v_w_out, v_ln1_g, v_ln1_b, v_mlp_w_up, v_mlp_b_up, v_mlp_w_down, v_mlp_b_down, v_ln2_g, v_ln2_b):
    args = locals()
    w = {k: args[k][0] for k in ORDER}
    m = {k: args["m_" + k][0] for k in BIG}
    v = {k: args["v_" + k][0] for k in BIG}
    raw = [{k: args[prefix + k] for k in SMALL} for prefix in ("", "m_", "v_")]
    loss, grad_x, out_g, out_d, out_m, out_v = _step(x[0], loss_target[0], w, m, v, *raw)
    outs = [loss, grad_x[None]]
    for group in (out_g, out_d, out_m, out_v):
        outs += [group[k].reshape(args[k].shape) for k in ORDER]
    return tuple(outs)
```

```python
import functools
import math

import jax
import jax.numpy as jnp
from jax import lax
from jax.experimental import pallas as pl
from jax.experimental.pallas import tpu as pltpu

F32 = jnp.float32
BF16 = jnp.bfloat16
MESH = pl.DeviceIdType.MESH
N_DEV = 8
SUBLANES = 8
LANES = 128
VMEM_BYTES_V7X = 64 * 2 ** 20
VMEM_CAP = VMEM_BYTES_V7X - 8 * 2 ** 20

ALPHA = 2.0 ** 0.25
LN_EPS = 1e-5
RG_C = 8.0
ADAM_LR, ADAM_B1, ADAM_B2, ADAM_EPS, ADAM_WD, ADAM_STEP = 0.001, 0.9, 0.999, 1e-08, 0.01, 10
GELU_C = math.sqrt(2.0 / math.pi)
GELU_K = 0.044715

ANY = pl.BlockSpec(memory_space=pl.ANY)


def _params(sem, vmem_bytes):
    limit = int(min(max(2 * vmem_bytes, 16 * 2 ** 20), VMEM_CAP))
    return pltpu.CompilerParams(dimension_semantics=sem, vmem_limit_bytes=limit)


def _sig(x):
    return 1.0 / (1.0 + jnp.exp(-x))


def _gelu(x):
    return 0.5 * x * (1.0 + jnp.tanh(GELU_C * (x + GELU_K * x * x * x)))


def _dgelu(x):
    th = jnp.tanh(GELU_C * (x + GELU_K * x * x * x))
    return 0.5 * (1.0 + th) + 0.5 * x * (1.0 - th * th) * (GELU_C * (1.0 + 3.0 * GELU_K * x * x))


def _one_minus_exp(x, exp_half_x):
    p = x * (1.0 + x * (1 / 2 + x * (1 / 6 + x * (1 / 24 + x * (1 / 120)))))
    return jnp.where(x > -1 / 16, -p, 1.0 - exp_half_x * exp_half_x)


def _accumulate(ref, val, first):
    @pl.when(first)
    def _():
        ref[...] = val

    @pl.when(jnp.logical_not(first))
    def _():
        ref[...] += val


def _rows8(cw):
    return lax.broadcasted_iota(jnp.int32, (SUBLANES, cw), 0)


def _shift_down(cur, prev, s, rows):
    return jnp.where(rows < s, pltpu.roll(prev, s, 0), pltpu.roll(cur, s, 0))


def _shift_up(cur, nxt, s, rows):
    return jnp.where(rows < SUBLANES - s, pltpu.roll(cur, SUBLANES - s, 0), pltpu.roll(nxt, SUBLANES - s, 0))


def _mm(a, b, *, M, N, K, ta=False, tb=False, b_split=1, n_split=1, a_fn=None, extras=(), epilogue=None,
        n_out=1, n_cs=0, out_dtypes=None, tm=512, tn=512, tk=512, after=None, into=None, name):
    tm, tn, tk = min(tm, M), min(tn, N), min(tk, K)
    assert M % tm == 0 and N % tn == 0 and K % tk == 0, (name, M, N, K, tm, tn, tk)
    nk = K // tk
    grid = (N // tn, M // tm, nk)
    a_spec = pl.BlockSpec((tk, tm), lambda j, i, k: (k, i)) if ta else pl.BlockSpec((tm, tk), lambda j, i, k: (i, k))
    if b_split == 1:
        b_spec = pl.BlockSpec((tn, tk), lambda j, i, k: (j, k)) if tb else pl.BlockSpec((tk, tn), lambda j, i, k: (k, j))
    elif tb:
        kb = (K // b_split) // tk
        assert kb * tk * b_split == K, name
        b_spec = pl.BlockSpec((None, tn, tk), lambda j, i, k: (k // kb, j, k % kb))
    else:
        nb = (N // b_split) // tn
        assert nb * tn * b_split == N, name
        b_spec = pl.BlockSpec((None, tk, tn), lambda j, i, k: (j // nb, k, j % nb))
    in_specs = [a_spec, b_spec]
    for arr, kind, *col_off in extras:
        off = col_off[0] if col_off else 0
        in_specs.append(pl.BlockSpec((tm, tn), lambda j, i, k, off=off: (i, off + j)) if kind == "mn"
                        else pl.BlockSpec((1, tn), lambda j, i, k: (0, j)))
    out_dtypes = (F32,) * n_out if out_dtypes is None else out_dtypes
    if n_split == 1:
        out_shape = [jax.ShapeDtypeStruct((M, N), dt) for dt in out_dtypes]
        out_specs = [pl.BlockSpec((tm, tn), lambda j, i, k: (i, j)) for _ in range(n_out)]
    else:
        assert n_out == 1
        nbo = (N // n_split) // tn
        assert nbo * tn * n_split == N, name
        out_shape = [jax.ShapeDtypeStruct((n_split, M, N // n_split), out_dtypes[0])]
        out_specs = [pl.BlockSpec((None, tm, tn), lambda j, i, k: (j // nbo, i, j % nbo))]
    out_shape += [jax.ShapeDtypeStruct((1, N), F32) for _ in range(n_cs)]
    out_specs += [pl.BlockSpec((1, tn), lambda j, i, k: (0, j)) for _ in range(n_cs)]
    ne = len(extras)
    dims = (((0 if ta else 1,), (1 if tb else 0,)), ((), ()))

    n_after = 0 if after is None else 1
    in_specs += [ANY] * n_after
    aliases, tail = {}, [] if after is None else [after]
    if into is not None:
        buf, which, col_blk = into
        assert n_split == 1 and buf.shape[0] == M and buf.dtype == out_dtypes[which], name
        aliases = {len(in_specs): which}
        in_specs.append(ANY)
        tail.append(buf)
        out_shape[which] = jax.ShapeDtypeStruct(buf.shape, buf.dtype)
        out_specs[which] = pl.BlockSpec((tm, tn), lambda j, i, k: (i, col_blk + j))

    def body(*refs):
        a_ref, b_ref = refs[0], refs[1]
        ex_refs = refs[2:2 + ne]
        first_out = 2 + ne + len(tail)
        out_refs = refs[first_out:first_out + n_out]
        cs_refs = refs[first_out + n_out:first_out + n_out + n_cs]
        i, k = pl.program_id(1), pl.program_id(2)

        def product():
            av = a_ref[...]
            if a_fn is not None:
                av = a_fn(av.astype(F32))
            return lax.dot_general(av.astype(BF16), b_ref[...].astype(BF16), dims, preferred_element_type=F32)

        def finish(acc):
            res = (acc,) if epilogue is None else epilogue(acc, *[r[...] for r in ex_refs])
            for r, o in zip(out_refs, res[:n_out]):
                r[...] = o.astype(r.dtype)
            for r, cval in zip(cs_refs, res[n_out:]):
                _accumulate(r, jnp.sum(cval, axis=0, keepdims=True), i == 0)

        if nk == 1:
            finish(product())
            return
        acc_ref = refs[-1]

        @pl.when(k == 0)
        def _():
            acc_ref[...] = jnp.zeros_like(acc_ref)

        acc_ref[...] += product()

        @pl.when(k == nk - 1)
        def _():
            finish(acc_ref[...])

    vmem = 2 * tm * tk * a.dtype.itemsize + 2 * tk * tn * b.dtype.itemsize + (1 + 2 * n_out + 2 * ne + 2) * tm * tn * 4
    outs = pl.pallas_call(
        body, name=name, grid=grid, in_specs=in_specs, out_specs=out_specs, out_shape=out_shape,
        scratch_shapes=[pltpu.VMEM((tm, tn), F32)] if nk > 1 else [], input_output_aliases=aliases,
        compiler_params=_params(("parallel", "arbitrary", "arbitrary"), vmem),
    )(a, b, *[e[0] for e in extras], *tail)
    return outs[0] if len(outs) == 1 else outs


BD_STEP = 4

def _bd(pairs, *, T, J, kb, nb, tw=False, extras=(), epilogue=None, n_out=1, n_cs=0, out_dtypes=None, tm=512, name):
    jb = BD_STEP
    assert T % tm == 0 and J % jb == 0
    grid = (J // jb, T // tm)
    npair, ne = len(pairs), len(extras)
    in_specs, args = [], []
    for arr, off, w in pairs:
        assert off % jb == 0, name
        in_specs.append(pl.BlockSpec((tm, jb * kb), lambda j, i, off=off // jb: (i, off + j)))
        in_specs.append(pl.BlockSpec((jb,) + tuple(w.shape[1:]), lambda j, i: (j, 0, 0)))
        args += [arr, w]
    for arr, kind, off in extras:
        assert off % jb == 0, name
        in_specs.append(pl.BlockSpec((tm, jb * nb), lambda j, i, off=off // jb: (i, off + j)) if kind == "tile"
                        else pl.BlockSpec((1, jb * nb), lambda j, i, off=off // jb: (0, off + j)))
        args.append(arr)
    out_dtypes = (F32,) * n_out if out_dtypes is None else out_dtypes
    out_shape = [jax.ShapeDtypeStruct((T, J * nb), dt) for dt in out_dtypes]
    out_specs = [pl.BlockSpec((tm, jb * nb), lambda j, i: (i, j)) for _ in range(n_out)]
    out_shape += [jax.ShapeDtypeStruct((1, J * nb), F32) for _ in range(n_cs)]
    out_specs += [pl.BlockSpec((1, jb * nb), lambda j, i: (0, j)) for _ in range(n_cs)]
    dims = (((1,), (1 if tw else 0,)), ((), ()))

    def body(*refs):
        ex_refs = refs[2 * npair:2 * npair + ne]
        out_refs = refs[2 * npair + ne:2 * npair + ne + n_out]
        cs_refs = refs[2 * npair + ne + n_out:]
        i = pl.program_id(1)
        for s in range(jb):
            cols_in, cols_out = pl.ds(s * kb, kb), pl.ds(s * nb, nb)
            acc = None
            for p in range(npair):
                d = lax.dot_general(refs[2 * p][:, cols_in].astype(BF16), refs[2 * p + 1][s].astype(BF16), dims,
                                    preferred_element_type=F32)
                acc = d if acc is None else acc + d
            res = (acc,) if epilogue is None else epilogue(acc, *[r[:, cols_out] for r in ex_refs])
            for r, o in zip(out_refs, res[:n_out]):
                r[:, cols_out] = o.astype(r.dtype)
            for r, cval in zip(cs_refs, res[n_out:]):
                _accumulate(r.at[:, cols_out], jnp.sum(cval, axis=0, keepdims=True), i == 0)

    vmem = jb * (2 * npair * tm * kb + 2 * npair * kb * nb + (2 * n_out + 2 * ne + 3) * tm * nb) * 4
    outs = pl.pallas_call(
        body, name=name, grid=grid, in_specs=in_specs, out_specs=out_specs, out_shape=out_shape,
        compiler_params=_params(("parallel", "arbitrary"), vmem),
    )(*args)
    return outs[0] if len(outs) == 1 else outs


def _bdw(a, a_off, b, b_off, *, T, J, kb, nb, tm=512, name):
    jb = BD_STEP
    assert T % tm == 0 and J % jb == 0 and a_off % jb == 0 and b_off % jb == 0
    a_blk, b_blk = a_off // jb, b_off // jb

    def body(a_ref, b_ref, o_ref):
        i = pl.program_id(1)
        for s in range(jb):
            d = lax.dot_general(a_ref[:, pl.ds(s * kb, kb)].astype(BF16), b_ref[:, pl.ds(s * nb, nb)].astype(BF16),
                                (((0,), (0,)), ((), ())), preferred_element_type=F32)
            _accumulate(o_ref.at[s], d, i == 0)

    return pl.pallas_call(
        body, name=name, grid=(J // jb, T // tm),
        in_specs=[pl.BlockSpec((tm, jb * kb), lambda j, i: (i, a_blk + j)),
                  pl.BlockSpec((tm, jb * nb), lambda j, i: (i, b_blk + j))],
        out_specs=pl.BlockSpec((jb, kb, nb), lambda j, i: (j, 0, 0)),
        out_shape=jax.ShapeDtypeStruct((J, kb, nb), F32),
        compiler_params=_params(("parallel", "arbitrary"), jb * (2 * tm * (kb + nb) + 3 * kb * nb) * 4),
    )(a, b)


def _bd_pack(w, q):
    g, a, b = w.shape
    eye = jnp.eye(q, dtype=w.dtype)
    return jnp.einsum("jqab,qr->jqarb", w.reshape(g // q, q, a, b), eye).reshape(g // q, q * a, q * b)


def _bd_unpack(wp, q):
    j, qa, qb = wp.shape
    a, b = qa // q, qb // q
    w5 = wp.reshape(j, q, a, q, b)
    return jnp.stack([w5[:, r, :, r, :] for r in range(q)], axis=1).reshape(j * q, a, b)


def _ew(fn, ins, *, T, C, n_out, n_cs=0, out_dtypes=None, tm=256, cw=None, name):
    cw = C if cw is None else cw
    assert T % tm == 0 and C % cw == 0
    grid = (C // cw, T // tm)
    in_specs = []
    for arr, kind, off in ins:
        in_specs.append(pl.BlockSpec((tm, cw), lambda j, i, off=off: (i, off + j)) if kind == "tile"
                        else pl.BlockSpec((arr.shape[0], cw), lambda j, i, off=off: (0, off + j)))
    out_dtypes = (F32,) * n_out if out_dtypes is None else out_dtypes
    out_shape = [jax.ShapeDtypeStruct((T, C), dt) for dt in out_dtypes]
    out_specs = [pl.BlockSpec((tm, cw), lambda j, i: (i, j)) for _ in range(n_out)]
    out_shape += [jax.ShapeDtypeStruct((1, C), F32) for _ in range(n_cs)]
    out_specs += [pl.BlockSpec((1, cw), lambda j, i: (0, j)) for _ in range(n_cs)]
    nin = len(ins)

    def body(*refs):
        i = pl.program_id(1)
        res = fn(*[r[...].astype(F32) for r in refs[:nin]])
        for r, o in zip(refs[nin:nin + n_out], res[:n_out]):
            r[...] = o.astype(r.dtype)
        for r, cval in zip(refs[nin + n_out:], res[n_out:]):
            _accumulate(r, jnp.sum(cval, axis=0, keepdims=True), i == 0)

    vmem = (2 * nin + 2 * n_out + 6) * tm * cw * 4
    outs = pl.pallas_call(
        body, name=name, grid=grid, in_specs=in_specs, out_specs=out_specs, out_shape=out_shape,
        compiler_params=_params(("parallel", "arbitrary"), vmem),
    )(*[arr for arr, _, _ in ins])
    return outs[0] if len(outs) == 1 else outs


def _ln_stats(s):
    mu = jnp.mean(s, axis=-1, keepdims=True)
    d = s - mu
    var = jnp.mean(d * d, axis=-1, keepdims=True)
    rstd = lax.rsqrt(var + LN_EPS)
    return d * rstd, rstd


def _ln_bwd(dy, g, xhat, rstd):
    dxh = dy * g
    m1 = jnp.mean(dxh, axis=-1, keepdims=True)
    m2 = jnp.mean(dxh * xhat, axis=-1, keepdims=True)
    return rstd * (dxh - m1 - xhat * m2)


def _conv_fwd(z, conv_w, conv_b, *, T, C, tm=512, cw=1024, after=None, name):
    ng, hb = tm // SUBLANES, tm // SUBLANES
    n_after = 0 if after is None else 1

    def body(x_ref, halo_ref, w_ref, b_ref, *rest):
        o_ref = rest[-1]
        it = pl.program_id(1)
        rows = _rows8(cw)
        halo = jnp.where(it == 0, 0.0, halo_ref[...])
        w = w_ref[...]
        bias = b_ref[...]

        def group(g, carry):
            off = pl.multiple_of(g * SUBLANES, SUBLANES)
            cur = x_ref[pl.ds(off, SUBLANES), :]
            prev = x_ref[pl.ds(pl.multiple_of(jnp.maximum(off - SUBLANES, 0), SUBLANES), SUBLANES), :]
            prev = jnp.where(g == 0, halo, prev)
            acc = cur * w[3:4] + bias
            for s in (1, 2, 3):
                acc = acc + _shift_down(cur, prev, s, rows) * w[3 - s:4 - s]
            o_ref[pl.ds(off, SUBLANES), :] = acc
            return carry

        lax.fori_loop(0, ng, group, 0, unroll=2)

    return pl.pallas_call(
        body, name=name, grid=(C // cw, T // tm),
        in_specs=[pl.BlockSpec((tm, cw), lambda j, i: (i, j)),
                  pl.BlockSpec((SUBLANES, cw), lambda j, i: (jnp.maximum(i * hb - 1, 0), j)),
                  pl.BlockSpec((4, cw), lambda j, i: (0, j)), pl.BlockSpec((1, cw), lambda j, i: (0, j))]
        + [ANY] * n_after,
        out_specs=pl.BlockSpec((tm, cw), lambda j, i: (i, j)),
        out_shape=jax.ShapeDtypeStruct((T, C), F32),
        compiler_params=_params(("parallel", "arbitrary"), 5 * tm * cw * 4),
    )(z, z, conv_w, conv_b, *([after] if n_after else []))


def _conv_bwd(dxc, z, conv_w, dz, *, T, C, tm=512, cw=512, name):
    ng, hb, last = tm // SUBLANES, tm // SUBLANES, T // SUBLANES - 1
    nt = T // tm
    rows16 = 2 * SUBLANES

    def body(d_ref, dn_ref, x_ref, w_ref, dz_in_ref, o_ref, sums_ref):
        it = pl.program_id(1)
        rows = _rows8(cw)
        dnext = jnp.where(it == nt - 1, 0.0, dn_ref[...])
        w = w_ref[...]

        def pair(q, accs):
            base = pl.multiple_of(q * rows16, rows16)
            halves = []
            for half in range(2):
                g = 2 * q + half
                off = pl.multiple_of(base + half * SUBLANES, SUBLANES)
                dcur = d_ref[pl.ds(off, SUBLANES), :]
                dnx = d_ref[pl.ds(pl.multiple_of(jnp.minimum(off + SUBLANES, tm - SUBLANES), SUBLANES), SUBLANES), :]
                dnx = jnp.where(g == ng - 1, dnext, dnx)
                xcur = x_ref[pl.ds(off, SUBLANES), :]
                acc = dcur * w[3:4]
                taps = [accs[3] + dcur * xcur]
                for s in (1, 2, 3):
                    ahead = _shift_up(dcur, dnx, s, rows)
                    acc = acc + ahead * w[3 - s:4 - s]
                    taps.append(accs[3 - s] + ahead * xcur)
                halves.append(acc)
                accs = (taps[3], taps[2], taps[1], taps[0], accs[4] + dcur)
            o_ref[pl.ds(base, rows16), :] = jnp.concatenate(halves, axis=0).astype(o_ref.dtype)
            return accs

        zero = jnp.zeros((SUBLANES, cw), F32)
        accs = lax.fori_loop(0, ng // 2, pair, (zero,) * 5)
        sums = jnp.zeros((SUBLANES, cw), F32)
        for k, a in enumerate(accs):
            sums = jnp.where(rows == k, jnp.sum(a, axis=0, keepdims=True), sums)
        _accumulate(sums_ref, sums, it == 0)

    tile = pl.BlockSpec((tm, cw), lambda j, i: (i, j))
    return pl.pallas_call(
        body, name=name, grid=(C // cw, nt),
        in_specs=[tile, pl.BlockSpec((SUBLANES, cw), lambda j, i: (jnp.minimum((i + 1) * hb, last), j)),
                  tile, pl.BlockSpec((4, cw), lambda j, i: (0, j)), ANY],
        out_specs=[tile, pl.BlockSpec((SUBLANES, cw), lambda j, i: (0, j))],
        input_output_aliases={4: 0},
        out_shape=[jax.ShapeDtypeStruct(dz.shape, dz.dtype), jax.ShapeDtypeStruct((SUBLANES, C), F32)],
        compiler_params=_params(("parallel", "arbitrary"), 7 * tm * cw * 4),
    )(dxc, dxc, z, conv_w, dz)


def _rg_coeffs(r, ig, xc, sp):
    la = (-RG_C) * r * sp
    a = jnp.exp(la)
    m = jnp.sqrt(_one_minus_exp(2.0 * la, a))
    return a, m, m * (ig * xc)


def _rg_scan_fwd(z, ri, xc, sp, *, T, C, gate_off, tm=512, cw=256, name):
    rows16 = 2 * SUBLANES
    nq = tm // rows16

    def body(gate_ref, r_ref, i_ref, xc_ref, sp_ref, h_ref, p_ref, a_ref, m_ref, carry_ref):
        it = pl.program_id(1)

        @pl.when(it == 0)
        def _():
            carry_ref[...] = jnp.zeros_like(carry_ref)

        rows = _rows8(cw)
        sp_row = sp_ref[...]

        def pair(q, carry):
            base = pl.multiple_of(q * rows16, rows16)
            halves = []
            for half in range(2):
                sl = pl.ds(pl.multiple_of(base + half * SUBLANES, SUBLANES), SUBLANES)
                a, m, b = _rg_coeffs(r_ref[sl, :], i_ref[sl, :], xc_ref[sl, :], sp_row)
                a_ref[sl, :] = a
                m_ref[sl, :] = m
                for s in (1, 2, 4):
                    keep = rows >= s
                    sa = jnp.where(keep, pltpu.roll(a, s, 0), 1.0)
                    sb = jnp.where(keep, pltpu.roll(b, s, 0), 0.0)
                    b = b + a * sb
                    a = a * sa
                h = b + a * carry
                h_ref[sl, :] = h
                halves.append(h * _gelu(gate_ref[sl, :]))
                carry = h[SUBLANES - 1:SUBLANES, :]
            p_ref[pl.ds(base, rows16), :] = jnp.concatenate(halves, axis=0).astype(p_ref.dtype)
            return carry

        last = lax.fori_loop(0, nq, pair, carry_ref[0:1, :], unroll=2)
        carry_ref[...] = jnp.broadcast_to(last, carry_ref.shape)

    tile = pl.BlockSpec((tm, cw), lambda j, i: (i, j))
    gate_blk = gate_off // cw
    return pl.pallas_call(
        body, name=name, grid=(C // cw, T // tm),
        in_specs=[pl.BlockSpec((tm, cw), lambda j, i: (i, gate_blk + j)),
                  pl.BlockSpec((tm, cw), lambda j, i: (i, 2 * j)), pl.BlockSpec((tm, cw), lambda j, i: (i, 2 * j + 1)),
                  tile, pl.BlockSpec((1, cw), lambda j, i: (0, j))],
        out_specs=[tile, tile, tile, tile],
        out_shape=[jax.ShapeDtypeStruct((T, C), F32), jax.ShapeDtypeStruct((T, C), BF16),
                   jax.ShapeDtypeStruct((T, C), F32), jax.ShapeDtypeStruct((T, C), F32)],
        scratch_shapes=[pltpu.VMEM((SUBLANES, cw), F32)],
        compiler_params=_params(("parallel", "arbitrary"), 16 * tm * cw * 4),
    )(z, ri, ri, xc, sp)


def _rg_scan_bwd(dh, h, ri, xc, a_fwd, m_fwd, sp, *, T, C, tm=512, cw=256, name):
    ng, hb, nt = tm // SUBLANES, tm // SUBLANES, T // tm

    def body(dh_ref, h_ref, hp_ref, r_ref, i_ref, xc_ref, a_ref, m_ref, sp_ref,
             drai_ref, dxc_ref, crai_ref, csp_ref, cg_ref, ca_ref):
        step = pl.program_id(1)

        @pl.when(step == 0)
        def _():
            cg_ref[...] = jnp.zeros_like(cg_ref)
            ca_ref[...] = jnp.zeros_like(ca_ref)

        rows = _rows8(cw)
        sp_row = sp_ref[...]
        hhalo = jnp.where(step == nt - 1, 0.0, hp_ref[...])

        def group(gi, carry):
            g_next, a_next, s_ra, s_ia, s_sp = carry
            g = ng - 1 - gi
            off = pl.multiple_of(g * SUBLANES, SUBLANES)
            sl = pl.ds(off, SUBLANES)
            rr, ii, xx = r_ref[sl, :], i_ref[sl, :], xc_ref[sl, :]
            a, m = a_ref[sl, :], m_ref[sl, :]
            hh = h_ref[sl, :]
            hpv = h_ref[pl.ds(pl.multiple_of(jnp.maximum(off - SUBLANES, 0), SUBLANES), SUBLANES), :]
            hpv = jnp.where(g == 0, hhalo, hpv)
            hprev = _shift_down(hh, hpv, 1, rows)
            d = dh_ref[sl, :]
            c = jnp.where(rows < SUBLANES - 1, pltpu.roll(a, SUBLANES - 1, 0), a_next)
            for s in (1, 2, 4):
                keep = rows < SUBLANES - s
                sc = jnp.where(keep, pltpu.roll(c, SUBLANES - s, 0), 1.0)
                sd = jnp.where(keep, pltpu.roll(d, SUBLANES - s, 0), 0.0)
                d = d + c * sd
                c = c * sc
            gg = d + c * g_next
            da = gg * hprev
            dm = gg * (ii * xx)
            di = gg * (m * xx)
            dxc_ref[sl, :] = gg * (m * ii)
            dla = da * a - dm * (a * a / m)
            dra = dla * ((-RG_C) * sp_row) * (rr * (1.0 - rr))
            dia = di * (ii * (1.0 - ii))
            drai_ref[sl, pl.ds(0, cw)] = dra
            drai_ref[sl, pl.ds(cw, cw)] = dia
            return (gg[0:1, :], a[0:1, :], s_ra + dra, s_ia + dia, s_sp + dla * ((-RG_C) * rr))

        zero = jnp.zeros((SUBLANES, cw), F32)
        g_first, a_first, s_ra, s_ia, s_sp = lax.fori_loop(
            0, ng, group, (cg_ref[0:1, :], ca_ref[0:1, :], zero, zero, zero), unroll=2)
        cg_ref[...] = jnp.broadcast_to(g_first, cg_ref.shape)
        ca_ref[...] = jnp.broadcast_to(a_first, ca_ref.shape)
        for ref, acc in ((crai_ref.at[:, pl.ds(0, cw)], s_ra), (crai_ref.at[:, pl.ds(cw, cw)], s_ia), (csp_ref, s_sp)):
            _accumulate(ref, jnp.sum(acc, axis=0, keepdims=True), step == 0)

    tile = pl.BlockSpec((tm, cw), lambda j, i: (nt - 1 - i, j))
    wide = pl.BlockSpec((tm, 2 * cw), lambda j, i: (nt - 1 - i, j))
    vec = pl.BlockSpec((1, cw), lambda j, i: (0, j))
    return pl.pallas_call(
        body, name=name, grid=(C // cw, nt),
        in_specs=[tile, tile, pl.BlockSpec((SUBLANES, cw), lambda j, i: (jnp.maximum((nt - 1 - i) * hb - 1, 0), j)),
                  pl.BlockSpec((tm, cw), lambda j, i: (nt - 1 - i, 2 * j)),
                  pl.BlockSpec((tm, cw), lambda j, i: (nt - 1 - i, 2 * j + 1)), tile, tile, tile, vec],
        out_specs=[wide, tile, pl.BlockSpec((1, 2 * cw), lambda j, i: (0, j)), vec],
        out_shape=[jax.ShapeDtypeStruct((T, 2 * C), F32), jax.ShapeDtypeStruct((T, C), F32),
                   jax.ShapeDtypeStruct((1, 2 * C), F32), jax.ShapeDtypeStruct((1, C), F32)],
        scratch_shapes=[pltpu.VMEM((SUBLANES, cw), F32), pltpu.VMEM((SUBLANES, cw), F32)],
        compiler_params=_params(("parallel", "arbitrary"), 24 * tm * cw * 4),
    )(dh, h, h, ri, ri, xc, a_fwd, m_fwd, sp)


def _cscan_tables(lr, li, reverse):
    lam = (lr.reshape(-1), -li.reshape(-1) if reverse else li.reshape(-1))

    def mul(p, q):
        return p[0] * q[0] - p[1] * q[1], p[0] * q[1] + p[1] * q[0]

    pows = [lam]
    for _ in range(SUBLANES - 1):
        pows.append(mul(pows[-1], lam))
    zero = jnp.zeros_like(lam[0])
    tab = jnp.stack([pows[0][0], pows[0][1], pows[1][0], pows[1][1], pows[3][0], pows[3][1], zero, zero])
    if reverse:
        pows = pows[::-1]
    return tab, jnp.stack([p[0] for p in pows]), jnp.stack([p[1] for p in pows])


def _power_slabs(lr, li, n):
    pr, pi = lr.reshape(1, -1), li.reshape(1, -1)
    while pr.shape[0] < n:
        tr, ti = pr[-1:], pi[-1:]
        pr, pi = (jnp.concatenate([pr, pr * tr - pi * ti], axis=0), jnp.concatenate([pi, pr * ti + pi * tr], axis=0))
    return jnp.repeat(pr, SUBLANES, axis=0), jnp.repeat(pi, SUBLANES, axis=0), pr[-1], pi[-1]


def _rows_to_segments(src_ref, dst_ref):
    seg = src_ref.shape[0] // SUBLANES
    for g in range(seg):
        dst_ref[pl.ds(g * SUBLANES, SUBLANES), :] = src_ref[pl.ds(g, SUBLANES, stride=seg), :].astype(dst_ref.dtype)


def _segments_to_rows(src_ref, dst_ref):
    seg = src_ref.shape[0] // SUBLANES
    for r in range(SUBLANES):
        dst_ref[pl.ds(r * seg, seg), :] = src_ref[pl.ds(r, seg, stride=SUBLANES), :].astype(dst_ref.dtype)


def _seg_scan_tile(xr_ref, xi_ref, pbr_ref, pbi_ref, tab_ref, pwr_ref, pwi_ref, cr_ref, ci_ref, *, reverse, h=None):
    tm, cw = xr_ref.shape
    seg = tm // SUBLANES
    rows = _rows8(cw)
    sign = -1.0 if reverse else 1.0
    l_re, l_im = pbr_ref[0:1, :], sign * pbi_ref[0:1, :]

    def slab(g):
        return pl.ds(pl.multiple_of(g * SUBLANES, SUBLANES), SUBLANES)

    def local(k, state):
        sl = slab(seg - 1 - k if reverse else k)
        sr, si = state
        nr = xr_ref[sl, :] + (l_re * sr - l_im * si)
        ni = xi_ref[sl, :] + (l_re * si + l_im * sr)
        xr_ref[sl, :] = nr
        xi_ref[sl, :] = ni
        return nr, ni

    zero = jnp.zeros((SUBLANES, cw), F32)
    er, ei = lax.fori_loop(0, seg, local, (zero, zero), unroll=2)

    for k, s in enumerate((1, 2, 4)):
        shift = SUBLANES - s if reverse else s
        keep = rows < SUBLANES - s if reverse else rows >= s
        sr = jnp.where(keep, pltpu.roll(er, shift, 0), 0.0)
        si = jnp.where(keep, pltpu.roll(ei, shift, 0), 0.0)
        m_re, m_im = tab_ref[2 * k:2 * k + 1, :], tab_ref[2 * k + 1:2 * k + 2, :]
        er, ei = er + (m_re * sr - m_im * si), ei + (m_re * si + m_im * sr)
    cin_r, cin_i = cr_ref[0:1, :], ci_ref[0:1, :]
    pwr, pwi = pwr_ref[...], pwi_ref[...]
    er, ei = er + (pwr * cin_r - pwi * cin_i), ei + (pwr * cin_i + pwi * cin_r)
    if reverse:
        ent_r = jnp.where(rows == SUBLANES - 1, cin_r, pltpu.roll(er, SUBLANES - 1, 0))
        ent_i = jnp.where(rows == SUBLANES - 1, cin_i, pltpu.roll(ei, SUBLANES - 1, 0))
        out_r, out_i = er[0:1, :], ei[0:1, :]
    else:
        ent_r = jnp.where(rows == 0, cin_r, pltpu.roll(er, 1, 0))
        ent_i = jnp.where(rows == 0, cin_i, pltpu.roll(ei, 1, 0))
        out_r, out_i = er[SUBLANES - 1:SUBLANES, :], ei[SUBLANES - 1:SUBLANES, :]
    cr_ref[...] = jnp.broadcast_to(out_r, cr_ref.shape)
    ci_ref[...] = jnp.broadcast_to(out_i, ci_ref.shape)

    if h is not None:
        hr_ref, hi_ref, hr_last, hi_last = h
        hr_wrap = _shift_down(hr_ref[pl.ds(tm - SUBLANES, SUBLANES), :], hr_last, 1, rows)
        hi_wrap = _shift_down(hi_ref[pl.ds(tm - SUBLANES, SUBLANES), :], hi_last, 1, rows)

    def fix(g, sums):
        sl = slab(g)
        power = slab(seg - 1 - g) if reverse else sl
        pr, pi = pbr_ref[power, :], sign * pbi_ref[power, :]
        nr = xr_ref[sl, :] + (pr * ent_r - pi * ent_i)
        ni = xi_ref[sl, :] + (pr * ent_i + pi * ent_r)
        xr_ref[sl, :] = nr
        xi_ref[sl, :] = ni
        if h is None:
            return sums
        before = slab(jnp.maximum(g - 1, 0))
        hr1 = jnp.where(g == 0, hr_wrap, hr_ref[before, :])
        hi1 = jnp.where(g == 0, hi_wrap, hi_ref[before, :])
        return sums[0] + (nr * hr1 + ni * hi1), sums[1] + (ni * hr1 - nr * hi1)

    return lax.fori_loop(0, seg, fix, (zero, zero) if h is not None else (), unroll=2)


S5_TILE = 2048


def _s5_fwd(z, u_off, wb_re, wb_im, wc_re, wc_im_neg, d_row, powers, *, T, tm=S5_TILE, name):
    J, ku, kp = wb_re.shape
    nt = T // tm
    pb_re, pb_im, top_re, top_im = powers
    tab, pw_re, pw_im = _cscan_tables(top_re, top_im, False)
    u_blk = u_off // ku

    def body(u_ref, wbr_ref, wbi_ref, wcr_ref, wci_ref, d_ref, pbr_ref, pbi_ref, tab_ref, pwr_ref, pwi_ref,
             hr_ref, hi_ref, y_ref, yg_ref, cr_ref, ci_ref, us_ref, ys_ref):
        @pl.when(pl.program_id(1) == 0)
        def _():
            cr_ref[...] = jnp.zeros_like(cr_ref)
            ci_ref[...] = jnp.zeros_like(ci_ref)

        _rows_to_segments(u_ref, us_ref)
        u = us_ref[...]
        ub = u.astype(BF16)
        hr_ref[...] = jnp.dot(ub, wbr_ref[...], preferred_element_type=F32)
        hi_ref[...] = jnp.dot(ub, wbi_ref[...], preferred_element_type=F32)
        _seg_scan_tile(hr_ref, hi_ref, pbr_ref, pbi_ref, tab_ref, pwr_ref, pwi_ref, cr_ref, ci_ref, reverse=False)
        y = (jnp.dot(hr_ref[...].astype(BF16), wcr_ref[...], preferred_element_type=F32)
             + jnp.dot(hi_ref[...].astype(BF16), wci_ref[...], preferred_element_type=F32) + d_ref[...] * u)
        ys_ref[...] = y
        _segments_to_rows(ys_ref, y_ref)
        ys_ref[...] = _gelu(y)
        _segments_to_rows(ys_ref, yg_ref)

    wb_spec = pl.BlockSpec((None, ku, kp), lambda j, i: (j, 0, 0))
    wc_spec = pl.BlockSpec((None, kp, ku), lambda j, i: (j, 0, 0))
    small = pl.BlockSpec((SUBLANES, kp), lambda j, i: (0, j))
    slabs = pl.BlockSpec((tm, kp), lambda j, i: (0, j))
    state = pl.BlockSpec((tm, kp), lambda j, i: (i, j))
    chan = pl.BlockSpec((tm, ku), lambda j, i: (i, j))
    return pl.pallas_call(
        body, name=name, grid=(J, nt),
        in_specs=[pl.BlockSpec((tm, ku), lambda j, i: (i, u_blk + j)), wb_spec, wb_spec, wc_spec, wc_spec,
                  pl.BlockSpec((1, ku), lambda j, i: (0, j)), slabs, slabs, small, small, small],
        out_specs=[state, state, chan, chan],
        out_shape=[jax.ShapeDtypeStruct((T, J * kp), F32)] * 2
        + [jax.ShapeDtypeStruct((T, J * ku), F32), jax.ShapeDtypeStruct((T, J * ku), BF16)],
        scratch_shapes=[pltpu.VMEM((SUBLANES, kp), F32), pltpu.VMEM((SUBLANES, kp), F32),
                        pltpu.VMEM((tm, ku), F32), pltpu.VMEM((tm, ku), F32)],
        compiler_params=_params(("parallel", "arbitrary"), 14 * tm * kp * 4),
    )(z, wb_re, wb_im, wc_re, wc_im_neg, d_row, pb_re, pb_im, tab, pw_re, pw_im)


def _s5_bwd(dy, z, u_off, h_re, h_im, wb_re, wb_im, wc_re, wc_im_neg, d_row, powers, dz, *, T, tm=S5_TILE, name):
    J, ku, kp = wb_re.shape
    nt, hb = T // tm, tm // SUBLANES
    pb_re, pb_im, top_re, top_im = powers
    tab, pw_re, pw_im = _cscan_tables(top_re, top_im, True)
    u_blk = u_off // ku
    contract_rows = (((0,), (0,)), ((), ()))
    contract_cols = (((1,), (1,)), ((), ()))

    def body(dy_ref, u_ref, hr_ref, hrp_ref, hi_ref, hip_ref, wbr_ref, wbi_ref, wcr_ref, wci_ref, d_ref,
             pbr_ref, pbi_ref, tab_ref, pwr_ref, pwi_ref, dz_in_ref,
             du_ref, dlr_ref, dli_ref, dd_ref, dwbr_ref, dwbi_ref, dwcr_ref, dwci_ref,
             gr_ref, gi_ref, cr_ref, ci_ref, dys_ref, us_ref):
        step = pl.program_id(1)
        first = step == 0

        @pl.when(first)
        def _():
            cr_ref[...] = jnp.zeros_like(cr_ref)
            ci_ref[...] = jnp.zeros_like(ci_ref)

        _rows_to_segments(dy_ref, dys_ref)
        _rows_to_segments(u_ref, us_ref)
        dy_t, u = dys_ref[...], us_ref[...]
        dyb, ub = dy_t.astype(BF16), u.astype(BF16)
        gr_ref[...] = lax.dot_general(dyb, wcr_ref[...], contract_cols, preferred_element_type=F32)
        gi_ref[...] = lax.dot_general(dyb, wci_ref[...], contract_cols, preferred_element_type=F32)
        hr_last = jnp.where(step == nt - 1, 0.0, hrp_ref[...])
        hi_last = jnp.where(step == nt - 1, 0.0, hip_ref[...])
        s_re, s_im = _seg_scan_tile(gr_ref, gi_ref, pbr_ref, pbi_ref, tab_ref, pwr_ref, pwi_ref, cr_ref, ci_ref,
                                    reverse=True, h=(hr_ref, hi_ref, hr_last, hi_last))
        _accumulate(dlr_ref, jnp.sum(s_re, axis=0, keepdims=True), first)
        _accumulate(dli_ref, jnp.sum(s_im, axis=0, keepdims=True), first)
        grb, gib = gr_ref[...].astype(BF16), gi_ref[...].astype(BF16)
        du = (lax.dot_general(grb, wbr_ref[...], contract_cols, preferred_element_type=F32)
              + lax.dot_general(gib, wbi_ref[...], contract_cols, preferred_element_type=F32) + dy_t * d_ref[...])
        dys_ref[...] = du
        _segments_to_rows(dys_ref, du_ref)
        _accumulate(dd_ref, jnp.sum(dy_t * u, axis=0, keepdims=True), first)
        _accumulate(dwbr_ref, lax.dot_general(ub, grb, contract_rows, preferred_element_type=F32), first)
        _accumulate(dwbi_ref, lax.dot_general(ub, gib, contract_rows, preferred_element_type=F32), first)
        _accumulate(dwcr_ref, lax.dot_general(dyb, hr_ref[...].astype(BF16), contract_rows,
                                              preferred_element_type=F32), first)
        _accumulate(dwci_ref, lax.dot_general(dyb, hi_ref[...].astype(BF16), contract_rows,
                                              preferred_element_type=F32), first)

    def tix(i):
        return nt - 1 - i

    wb_spec = pl.BlockSpec((None, ku, kp), lambda j, i: (j, 0, 0))
    wc_spec = pl.BlockSpec((None, kp, ku), lambda j, i: (j, 0, 0))
    small = pl.BlockSpec((SUBLANES, kp), lambda j, i: (0, j))
    state = pl.BlockSpec((tm, kp), lambda j, i: (tix(i), j))
    halo = pl.BlockSpec((SUBLANES, kp), lambda j, i: (jnp.maximum(tix(i) * hb - 1, 0), j))
    chan = pl.BlockSpec((tm, ku), lambda j, i: (tix(i), j))
    svec = pl.BlockSpec((1, kp), lambda j, i: (0, j))
    cvec = pl.BlockSpec((1, ku), lambda j, i: (0, j))
    slabs = pl.BlockSpec((tm, kp), lambda j, i: (0, j))
    return pl.pallas_call(
        body, name=name, grid=(J, nt),
        in_specs=[chan, pl.BlockSpec((tm, ku), lambda j, i: (tix(i), u_blk + j)), state, halo, state, halo,
                  wb_spec, wb_spec, wc_spec, wc_spec, cvec, slabs, slabs, small, small, small, ANY],
        out_specs=[pl.BlockSpec((tm, ku), lambda j, i: (tix(i), u_blk + j)), svec, svec, cvec,
                   wb_spec, wb_spec, wb_spec, wb_spec],
        input_output_aliases={16: 0},
        out_shape=[jax.ShapeDtypeStruct(dz.shape, dz.dtype), jax.ShapeDtypeStruct((1, J * kp), F32),
                   jax.ShapeDtypeStruct((1, J * kp), F32), jax.ShapeDtypeStruct((1, J * ku), F32),
                   jax.ShapeDtypeStruct((J, ku, kp), F32), jax.ShapeDtypeStruct((J, ku, kp), F32),
                   jax.ShapeDtypeStruct((J, ku, kp), F32), jax.ShapeDtypeStruct((J, ku, kp), F32)],
        scratch_shapes=[pltpu.VMEM((tm, kp), F32), pltpu.VMEM((tm, kp), F32),
                        pltpu.VMEM((SUBLANES, kp), F32), pltpu.VMEM((SUBLANES, kp), F32),
                        pltpu.VMEM((tm, ku), F32), pltpu.VMEM((tm, ku), F32)],
        compiler_params=_params(("parallel", "arbitrary"), 16 * tm * kp * 4),
    )(dy, z, h_re, h_re, h_im, h_im, wb_re, wb_im, wc_re, wc_im_neg, d_row, pb_re, pb_im, tab, pw_re, pw_im, dz)


def _mesh_pos():
    return lax.axis_index("x"), lax.axis_index("y"), lax.axis_index("c")


def _dev_index(px, py, pc):
    return 4 * px + 2 * py + pc


HBM = pl.BlockSpec(memory_space=pltpu.HBM)
SEM = pl.BlockSpec(memory_space=pltpu.SEMAPHORE)
EFFECT = pltpu.SideEffectType.DATAFLOW_SIDE_EFFECTING
RELATIONS = [(dx, dy, dc) for dx in (0, 1) for dy in (0, 1) for dc in (0, 1) if (dx, dy, dc) != (0, 0, 0)]


def _peer(rel):
    x, y, c = _mesh_pos()
    dx, dy, dc = rel
    return (x + dx - 2 * x * dx, y + dy - 2 * y * dy, c + dc - 2 * c * dc)


CHIP_RELATIONS = [(1, 0, 0), (0, 1, 0), (1, 1, 0)]
EXCHANGE_PEERS = {"gather": RELATIONS, "scatter": RELATIONS, "own": [(0, 0, 1)] + CHIP_RELATIONS, "pass": CHIP_RELATIONS}


def _split_copy(src_ref, land_ref, send_sems, recv_sems, k, mode, incoming):
    x, y, c = _mesh_pos()
    me = _dev_index(x, y, c)
    peer = _peer(EXCHANGE_PEERS[mode][k])
    if mode == "pass":
        held, theirs = _dev_index(peer[0], peer[1], c), _dev_index(peer[0], peer[1], 1 - c)
        src, slot, target = land_ref.at[held], theirs if incoming else held, (x, y, 1 - c)
    else:
        src = src_ref.at[_dev_index(*peer)] if mode == "scatter" else src_ref
        slot, target = _dev_index(*peer) if incoming else me, peer
    return pltpu.make_async_remote_copy(src_ref=src, dst_ref=land_ref.at[slot], send_sem=send_sems.at[k],
                                        recv_sem=recv_sems.at[k], device_id=target, device_id_type=MESH)


def _exchange_start(srcs, lands, *, mode, after=None, name):
    n = len(srcs)
    n_after = 0 if after is None else 1
    n_rel = len(EXCHANGE_PEERS[mode])

    def body(*refs):
        src_refs, land_refs = refs[:n], refs[n:2 * n]
        first_out = 2 * n + n_after
        send, recv = refs[first_out:first_out + n], refs[first_out + n:first_out + 2 * n]
        token = refs[-1]
        for k in range(n_rel):
            for a in range(n):
                _split_copy(src_refs[a], land_refs[a], send[a], recv[a], k, mode, incoming=False).start()
        token[...] = jnp.zeros_like(token)

    outs = pl.pallas_call(
        body, name=name, in_specs=[HBM] * (2 * n) + [ANY] * n_after,
        out_shape=[pltpu.SemaphoreType.DMA((n_rel,))] * (2 * n)
        + [pltpu.HBM(s.shape, s.dtype) for s in srcs] + [pltpu.HBM(s.shape, s.dtype) for s in lands]
        + [jax.ShapeDtypeStruct((SUBLANES, LANES), F32)],
        out_specs=[SEM] * (2 * n) + [HBM] * (2 * n) + [pl.BlockSpec(memory_space=pltpu.VMEM)],
        input_output_aliases={**{a: 2 * n + a for a in range(n)}, **{n + a: 3 * n + a for a in range(n)}},
        compiler_params=pltpu.CompilerParams(has_side_effects=EFFECT),
    )(*[pltpu.with_memory_space_constraint(s, pltpu.HBM) for s in srcs],
      *[pltpu.with_memory_space_constraint(s, pltpu.HBM) for s in lands], *([after] if n_after else []))
    per_array = [(outs[a], outs[n + a], outs[2 * n + a], outs[3 * n + a]) for a in range(n)]
    return per_array, outs[-1]


def _exchange_wait(handle, after, *, mode, name):
    send_sems, recv_sems, src_thru, land_thru = handle
    after = after if isinstance(after, (tuple, list)) else (after,)

    def body(src_ref, land_ref, send, recv, *rest):
        for k in range(len(EXCHANGE_PEERS[mode])):
            cp = _split_copy(src_ref, land_ref, send, recv, k, mode, incoming=True)
            cp.wait_send()
            cp.wait_recv()

    return pl.pallas_call(
        body, name=name, in_specs=[HBM, HBM, SEM, SEM] + [ANY] * len(after),
        out_shape=[pltpu.HBM(src_thru.shape, src_thru.dtype), pltpu.HBM(land_thru.shape, land_thru.dtype)],
        out_specs=[HBM, HBM], input_output_aliases={0: 0, 1: 1},
        compiler_params=pltpu.CompilerParams(has_side_effects=EFFECT),
    )(src_thru, land_thru, send_sems, recv_sems, *after)[1]


def _landing_zone(own_block):
    me = _dev_index(*_mesh_pos())
    zone = lax.empty((N_DEV,) + own_block.shape, own_block.dtype)
    return lax.dynamic_update_index_in_dim(zone, own_block, me, 0)


def _row_tile(rows, want):
    t = min(want, rows) // SUBLANES * SUBLANES
    while rows % t:
        t -= SUBLANES
    return t


def _sum_slots(recv, *, tr, name):
    s_, r_, c_ = recv.shape
    tr = _row_tile(r_, tr)

    def body(g_ref, o_ref):
        acc = g_ref[0]
        for s in range(1, s_):
            acc = acc + g_ref[s]
        o_ref[...] = acc

    return pl.pallas_call(
        body, name=name, grid=(r_ // tr,),
        in_specs=[pl.BlockSpec((s_, tr, c_), lambda i: (0, i, 0))],
        out_specs=pl.BlockSpec((tr, c_), lambda i: (i, 0)),
        out_shape=jax.ShapeDtypeStruct((r_, c_), F32),
        compiler_params=_params(("parallel",), (2 * s_ + 3) * tr * c_ * 4),
    )(recv)


def _adamw(recv, w, m, v, *, tr, name):
    s_, r_, c_ = recv.shape
    tr = _row_tile(r_, tr)
    assert w.shape == (r_, c_), (name, w.shape, recv.shape)
    c1 = 1.0 - ADAM_B1 ** ADAM_STEP
    c2 = 1.0 - ADAM_B2 ** ADAM_STEP

    def body(g_ref, w_ref, m_ref, v_ref, go_ref, d_ref, mo_ref, vo_ref):
        g = g_ref[0].astype(F32)
        for s in range(1, s_):
            g = g + g_ref[s].astype(F32)
        mn = ADAM_B1 * m_ref[...] + (1.0 - ADAM_B1) * g
        vn = ADAM_B2 * v_ref[...] + (1.0 - ADAM_B2) * (g * g)
        go_ref[...] = g
        mo_ref[...] = mn
        vo_ref[...] = vn
        d_ref[...] = -ADAM_LR * ((mn / c1) / (jnp.sqrt(vn / c2) + ADAM_EPS) + ADAM_WD * w_ref[...])

    tile = pl.BlockSpec((tr, c_), lambda i: (i, 0))
    return pl.pallas_call(
        body, name=name, grid=(r_ // tr,),
        in_specs=[pl.BlockSpec((s_, tr, c_), lambda i: (0, i, 0)), tile, tile, tile],
        out_specs=[tile] * 4, out_shape=[jax.ShapeDtypeStruct((r_, c_), F32)] * 4,
        compiler_params=_params(("parallel",), (2 * s_ + 16) * tr * c_ * 4),
    )(recv, w, m, v)


def _adamw_whole(gs, ws, ms, vs, *, name):
    n = len(gs)
    c1 = 1.0 - ADAM_B1 ** ADAM_STEP
    c2 = 1.0 - ADAM_B2 ** ADAM_STEP

    def body(*refs):
        for i in range(n):
            g, w = refs[i][...], refs[n + i][...]
            mn = ADAM_B1 * refs[2 * n + i][...] + (1.0 - ADAM_B1) * g
            vn = ADAM_B2 * refs[3 * n + i][...] + (1.0 - ADAM_B2) * (g * g)
            refs[4 * n + 3 * i][...] = -ADAM_LR * ((mn / c1) / (jnp.sqrt(vn / c2) + ADAM_EPS) + ADAM_WD * w)
            refs[4 * n + 3 * i + 1][...] = mn
            refs[4 * n + 3 * i + 2][...] = vn

    whole = pl.BlockSpec(memory_space=pltpu.VMEM)
    lane_padded = sum(math.prod(g.shape[:-1]) * (-(-g.shape[-1] // LANES) * LANES) for g in gs)
    outs = pl.pallas_call(
        body, name=name, in_specs=[whole] * (4 * n), out_specs=[whole] * (3 * n),
        out_shape=[jax.ShapeDtypeStruct(g.shape, F32) for g in gs for _ in range(3)],
        compiler_params=pltpu.CompilerParams(vmem_limit_bytes=int(min(max(16 * lane_padded * 4, 16 * 2 ** 20), VMEM_CAP))),
    )(*gs, *ws, *ms, *vs)
    return [tuple(outs[3 * i:3 * i + 3]) for i in range(n)]


def _s5_discretise(a_re, a_im, log_dt, b_re, b_im):
    dt = jnp.exp(log_dt)[:, None]
    lr = jnp.minimum(a_re, -1e-4)
    li = a_im
    mag = jnp.exp(lr * dt)
    lbr = mag * jnp.cos(li * dt)
    lbi = mag * jnp.sin(li * dt)
    zr, zi = lbr - 1.0, lbi
    den = lr * lr + li * li
    fr = (zr * lr + zi * li) / den
    fi = (zi * lr - zr * li) / den
    bbr = fr[..., None] * b_re - fi[..., None] * b_im
    bbi = fr[..., None] * b_im + fi[..., None] * b_re
    return lbr, lbi, bbr, bbi


def _softplus_neg(lam):
    return jnp.maximum(-lam, 0.0) + jnp.log(1.0 + jnp.exp(-jnp.abs(lam)))


S5_Q = 8
RG_Q = 2


def _local_step(x, tgt, W, comm):
    T, D = x.shape
    C = D
    G, P, H = W["ssm_b_re"].shape
    S = G * H
    F = W["mlp_b_up"].shape[1]
    n_in = 2 * C + S + 2 * D
    heads, hd = W["rg_wa"].shape[0], W["rg_wa"].shape[1]
    u_off, ga_off, gb_off = 2 * C, 2 * C + S, 2 * C + S + D

    if comm.first_token is not None:
        anchored = ("rg_lambda", "ssm_a_re", "rg_wa", "rg_wx", "ssm_c_re", "ssm_c_im")
        W = {**W, **{k: W[k] + comm.first_token[0, 0] for k in anchored}}
    sp, sp_vjp = jax.vjp(_softplus_neg, W["rg_lambda"])
    (lbr, lbi, bbr, bbi), s5_vjp = jax.vjp(_s5_discretise, W["ssm_a_re"], W["ssm_a_im"], W["ssm_log_dt"],
                                           W["ssm_b_re"], W["ssm_b_im"])
    lam_re, lam_im = lbr.reshape(-1), lbi.reshape(-1)
    jr, kr = heads // RG_Q, RG_Q * hd
    w_ri = jnp.concatenate([_bd_pack(W["rg_wa"], RG_Q), _bd_pack(W["rg_wx"], RG_Q)], axis=2).astype(BF16)
    b_ri = jnp.concatenate([W["rg_ba"].reshape(jr, kr), W["rg_bx"].reshape(jr, kr)], axis=1).reshape(1, -1)
    wb_re = _bd_pack(jnp.swapaxes(bbr, 1, 2), S5_Q).astype(BF16)
    wb_im = _bd_pack(jnp.swapaxes(bbi, 1, 2), S5_Q).astype(BF16)
    wc_re = _bd_pack(jnp.swapaxes(W["ssm_c_re"], 1, 2), S5_Q).astype(BF16)
    wc_im_neg = _bd_pack(jnp.swapaxes(-W["ssm_c_im"], 1, 2), S5_Q).astype(BF16)
    d_row = W["ssm_d"].reshape(1, S)
    powers = _power_slabs(lam_re, lam_im, S5_TILE // SUBLANES)

    x_bf = x.astype(BF16) if comm.first_token is None else (x + comm.first_token[0, 0]).astype(BF16)
    w_in, conv_w = comm.first_weights((x_bf, w_ri, wb_re, wb_im, wc_re, wc_im_neg, powers[0], powers[1]))
    z = _mm(x_bf, w_in, M=T, N=n_in, K=D, tm=1024, tn=n_in // 4, tk=D, after=comm.gather_token, name="fwd_in_proj")
    started = comm.start_weights(("mlp_w_up",), z)
    xc = _conv_fwd(z, conv_w, W["conv_b"], T=T, C=C, after=started, name="fwd_conv")
    ri = _bd([(xc, 0, w_ri)], T=T, J=jr, kb=kr, nb=2 * kr, extras=[(b_ri, "vec", 0)],
             epilogue=lambda acc, b: (_sig(acc + b),), name="fwd_gates")
    h, p, a_fwd, m_fwd = _rg_scan_fwd(z, ri, xc, sp, T=T, C=C, gate_off=C, cw=kr, name="fwd_rg_scan")
    w_a_out = comm.weight("w_a_out", p)
    started = comm.start_weights(("mlp_w_down",), p)
    y_a = _mm(p, w_a_out, M=T, N=D, K=C, out_dtypes=(BF16,), tm=512, tn=D, tk=C, after=started, name="fwd_rg_out")

    h_re, h_im, y_s, yg = _s5_fwd(z, u_off, wb_re, wb_im, wc_re, wc_im_neg, d_row, powers, T=T, name="fwd_s5")
    w_glu_w, w_glu_v = comm.weight("glu_w", yg), comm.weight("glu_v", yg)
    glu_a = _mm(yg, w_glu_w, M=T, N=D, K=S, out_dtypes=(BF16,), tm=1024, tn=D, tk=S, name="fwd_glu_w")
    cwm = 1024

    def mix_fn(b, ga, gb, ya, a):
        return b, _sig(ga) * ya.astype(F32) + _sig(gb) * (a.astype(F32) * _sig(b))

    glu_b, mix = _mm(yg, w_glu_v, M=T, N=D, K=S, tm=512, tn=cwm, tk=S,
                     extras=[(z, "mn", ga_off // cwm), (z, "mn", gb_off // cwm), (y_a, "mn"), (glu_a, "mn")],
                     epilogue=mix_fn, n_out=2, out_dtypes=(BF16, BF16), name="fwd_glu_v_mix")
    w_out = comm.weight("w_out", mix)
    def out_ln1_fn(acc, xv, g, b):
        s = ALPHA * xv + acc
        xhat, _ = _ln_stats(s)
        y = xhat * g + b
        return s, y, y

    s1, x1, x1_bf = _mm(mix, w_out, M=T, N=D, K=D, tm=256, tn=D, tk=D,
                        extras=[(x, "mn"), (W["ln1_g"], "n"), (W["ln1_b"], "n")], epilogue=out_ln1_fn, n_out=3,
                        out_dtypes=(F32, F32, BF16), name="fwd_out_proj_ln1")
    w_up = comm.weight("mlp_w_up", x1_bf)

    def mlp_up_fn(acc, b):
        hp = acc + b
        rl = jnp.maximum(hp, 0.0)
        return rl * rl, hp

    hact, hpre = _mm(x1_bf, w_up, M=T, N=F, K=D, tm=1024, tn=1024, tk=D, extras=[(W["mlp_b_up"], "n")],
                     epilogue=mlp_up_fn, n_out=2, out_dtypes=(BF16, BF16), name="fwd_mlp_up")
    w_down = comm.weight("mlp_w_down", hact)
    s2 = _mm(hact, w_down, M=T, N=D, K=F, tm=1024, tn=1024, tk=2048,
             extras=[(x1, "mn"), (W["mlp_b_down"], "n")], epilogue=lambda acc, xv, b: (ALPHA * xv + acc + b,),
             name="fwd_mlp_down")

    def ln2_fn(s, t, g, b):
        xhat, rstd = _ln_stats(s)
        err = xhat * g + b - t
        dy = err * (1.0 / D)
        ds = _ln_bwd(dy, g, xhat, rstd)
        return ds, ds, 0.5 * dy * err, dy * xhat, dy, ds

    ds2, ds2_bf, loss_cols, d_ln2_g, d_ln2_b, d_b_down = _ew(
        ln2_fn, [(s2, "tile", 0), (tgt, "tile", 0), (W["ln2_g"], "vec", 0), (W["ln2_b"], "vec", 0)],
        T=T, C=D, n_out=2, n_cs=4, out_dtypes=(F32, BF16), tm=256, name="bwd_loss_ln2")
    d_w_down = _mm(hact, ds2_bf, M=F, N=D, K=T, ta=True, out_dtypes=(BF16,), tm=1024, tn=1024, tk=4096, name="bwd_w_down")
    sent = comm.send_grad("mlp_w_down", d_w_down)

    def dhpre_fn(acc, hp):
        dv = acc * (2.0 * jnp.maximum(hp.astype(F32), 0.0))
        return dv, dv

    dhpre, d_b_up = _mm(ds2_bf, w_down, M=T, N=F, K=D, tb=True, tm=1024, tn=1024, tk=D, extras=[(hpre, "mn")],
                        epilogue=dhpre_fn, n_cs=1, out_dtypes=(BF16,), after=sent, name="bwd_mlp_down")
    d_w_up = _mm(x1_bf, dhpre, M=D, N=F, K=T, ta=True, out_dtypes=(BF16,), n_split=N_DEV, tm=1024, tn=F // N_DEV, tk=4096, name="bwd_w_up")
    sent = comm.send_grad("mlp_w_up", d_w_up)
    dx1 = _mm(dhpre, w_up, M=T, N=D, K=F, tb=True, tm=1024, tn=1024, tk=2048,
              extras=[(ds2, "mn")], epilogue=lambda acc, dv: (ALPHA * dv + acc,), after=sent, name="bwd_mlp_up")

    def ln1_bwd_fn(s, dy, g):
        xhat, rstd = _ln_stats(s)
        ds = _ln_bwd(dy, g, xhat, rstd)
        return ds, ds, dy * xhat, dy

    ds1, ds1_bf, d_ln1_g, d_ln1_b = _ew(ln1_bwd_fn, [(s1, "tile", 0), (dx1, "tile", 0), (W["ln1_g"], "vec", 0)],
                                        T=T, C=D, n_out=2, n_cs=2, out_dtypes=(F32, BF16), tm=256, name="bwd_ln1")
    d_w_out = _mm(mix, ds1_bf, M=D, N=D, K=T, ta=True, out_dtypes=(BF16,), tm=1024, tn=1024, tk=4096, name="bwd_w_out")
    sent = comm.send_grad("w_out", d_w_out)
    def mix_bwd_fn(dm, ga, gb, ya, a, b):
        ya, a, b = ya.astype(F32), a.astype(F32), b.astype(F32)
        sa, sb, sv = _sig(ga), _sig(gb), _sig(b)
        yb = a * sv
        dyb = dm * sb
        return (dm * ya * (sa * (1.0 - sa)), dm * yb * (sb * (1.0 - sb)), dm * sa, dyb * sv,
                dyb * a * (sv * (1.0 - sv)))

    dz = lax.empty((T, n_in), BF16)
    dz, dg_b, dy_a, dglu_a, dglu_b = _mm(
        ds1_bf, w_out, M=T, N=D, K=D, tb=True, tm=512, tn=cwm, tk=D,
        extras=[(z, "mn", ga_off // cwm), (z, "mn", gb_off // cwm), (y_a, "mn"), (glu_a, "mn"), (glu_b, "mn")],
        epilogue=mix_bwd_fn, n_out=5, out_dtypes=(BF16,) * 5, after=sent, into=(dz, 0, ga_off // cwm),
        name="bwd_out_proj_mix")
    dz = lax.dynamic_update_slice(dz, dg_b, (0, gb_off))

    d_w_a_out = _mm(p, dy_a, M=C, N=D, K=T, ta=True, out_dtypes=(BF16,), tm=1024, tn=1024, tk=4096, name="bwd_w_a_out")
    sent = comm.send_grad("w_a_out", d_w_a_out)
    def dp_fn(dp, hv, gate):
        th = jnp.tanh(GELU_C * (gate + GELU_K * gate * gate * gate))
        gelu = 0.5 * gate * (1.0 + th)
        dgelu = 0.5 * (1.0 + th) + 0.5 * gate * (1.0 - th * th) * (GELU_C * (1.0 + 3.0 * GELU_K * gate * gate))
        return dp * gelu, dp * hv * dgelu

    dh, dz = _mm(dy_a, w_a_out, M=T, N=C, K=D, tb=True, tm=256, tn=C, tk=D, extras=[(h, "mn"), (z, "mn", 1)],
                 epilogue=dp_fn, n_out=2, out_dtypes=(F32, BF16), after=sent, into=(dz, 1, 1), name="bwd_rg_out")
    drai, dxc0, d_b_ri, d_sp = _rg_scan_bwd(dh, h, ri, xc, a_fwd, m_fwd, sp, T=T, C=C, cw=kr, name="bwd_rg_scan")
    dxc = _bd([(drai, 0, w_ri)], T=T, J=jr, kb=2 * kr, nb=kr, tw=True, extras=[(dxc0, "tile", 0)],
              epilogue=lambda acc, d0: (acc + d0,), name="bwd_gates")
    d_w_ri = _bdw(xc, 0, drai, 0, T=T, J=jr, kb=kr, nb=2 * kr, name="bwd_w_gates")
    d_wa, d_wx = _bd_unpack(d_w_ri[:, :, :kr], RG_Q), _bd_unpack(d_w_ri[:, :, kr:], RG_Q)
    d_b_ri = d_b_ri.reshape(jr, 2 * kr)
    d_ba, d_bx = d_b_ri[:, :kr].reshape(1, -1), d_b_ri[:, kr:].reshape(1, -1)
    dz, conv_sums = _conv_bwd(dxc, z, conv_w, dz, T=T, C=C, name="bwd_conv")
    d_conv_w, d_conv_b = conv_sums[0:4], conv_sums[4:5]
    (d_lambda,) = sp_vjp(d_sp)

    d_glu_w = _mm(yg, dglu_a, M=S, N=D, K=T, ta=True, out_dtypes=(BF16,), n_split=N_DEV, tm=1024, tn=D // N_DEV, tk=4096, name="bwd_w_glu_w")
    d_glu_v = _mm(yg, dglu_b, M=S, N=D, K=T, ta=True, out_dtypes=(BF16,), n_split=N_DEV, tm=1024, tn=D // N_DEV, tk=4096, name="bwd_w_glu_v")
    sent = comm.send_grad("glu_w", d_glu_w, "glu_v", d_glu_v)
    dyg0 = _mm(dglu_a, w_glu_w, M=T, N=S, K=D, tb=True, tm=512, tn=S, tk=D, after=sent, name="bwd_glu_w")
    dy_s = _mm(dglu_b, w_glu_v, M=T, N=S, K=D, tb=True, tm=512, tn=S, tk=D,
               extras=[(dyg0, "mn"), (y_s, "mn")], epilogue=lambda acc, d0, yv: ((acc + d0) * _dgelu(yv),),
               name="bwd_glu_v")
    dz, d_lbr, d_lbi, d_ssm_d, d_wb_re, d_wb_im, d_wc_re, d_wc_im_neg = _s5_bwd(
        dy_s, z, u_off, h_re, h_im, wb_re, wb_im, wc_re, wc_im_neg, d_row, powers, dz, T=T, name="bwd_s5")
    d_bbr = jnp.swapaxes(_bd_unpack(d_wb_re, S5_Q), 1, 2)
    d_bbi = jnp.swapaxes(_bd_unpack(d_wb_im, S5_Q), 1, 2)
    d_a_re, d_a_im, d_log_dt, d_b_re, d_b_im = s5_vjp((d_lbr.reshape(G, P), d_lbi.reshape(G, P), d_bbr, d_bbi))
    d_c_re = _bd_unpack(d_wc_re, S5_Q)
    d_c_im = -_bd_unpack(d_wc_im_neg, S5_Q)

    grads = dict(
        conv_w=d_conv_w, conv_b=d_conv_b, rg_wa=d_wa, rg_ba=d_ba, rg_wx=d_wx, rg_bx=d_bx,
        rg_lambda=d_lambda, ssm_a_re=d_a_re, ssm_a_im=d_a_im, ssm_log_dt=d_log_dt,
        ssm_b_re=d_b_re, ssm_b_im=d_b_im, ssm_c_re=d_c_re, ssm_c_im=d_c_im, ssm_d=d_ssm_d.reshape(G, H),
        ln1_g=d_ln1_g, ln1_b=d_ln1_b, mlp_b_up=d_b_up, mlp_b_down=d_b_down, ln2_g=d_ln2_g, ln2_b=d_ln2_b)
    sent = comm.send_small(grads)

    d_w_in = _mm(x_bf, dz, M=D, N=n_in, K=T, ta=True, out_dtypes=(BF16,), n_split=N_DEV, tm=1024, tn=n_in // N_DEV,
                 tk=4096, after=sent, name="bwd_w_in")
    sent = comm.send_grad("w_in", d_w_in)
    grad_x = _mm(dz, w_in, M=T, N=D, K=n_in, tb=True, tm=1024, tn=1024, tk=n_in // 4,
                 extras=[(ds1, "mn")], epilogue=lambda acc, dv: (ALPHA * dv + acc,), after=sent, name="bwd_in_proj")
    return jnp.sum(loss_cols), grad_x, grads


BIG = ("w_in", "w_a_out", "glu_w", "glu_v", "w_out", "mlp_w_up", "mlp_w_down")
COL_SHARDED = ("w_in", "glu_w", "glu_v", "mlp_w_up")
SMALL = ("conv_w", "conv_b", "rg_wa", "rg_ba", "rg_wx", "rg_bx", "rg_lambda", "ssm_a_re", "ssm_a_im", "ssm_log_dt",
         "ssm_b_re", "ssm_b_im", "ssm_c_re", "ssm_c_im", "ssm_d", "ln1_g", "ln1_b", "mlp_b_up", "mlp_b_down", "ln2_g",
         "ln2_b")
ORDER = ("w_in", "conv_w", "conv_b", "rg_wa", "rg_ba", "rg_wx", "rg_bx", "rg_lambda", "w_a_out", "ssm_a_re",
         "ssm_a_im", "ssm_log_dt", "ssm_b_re", "ssm_b_im", "ssm_c_re", "ssm_c_im", "ssm_d", "glu_w", "glu_v", "w_out",
         "ln1_g", "ln1_b", "mlp_w_up", "mlp_b_up", "mlp_w_down", "mlp_b_down", "ln2_g", "ln2_b")
TILE_ELEMS = SUBLANES * LANES


def _pack(arrs):
    pieces = []
    for a in arrs:
        flat = a.reshape(-1)
        flat = jnp.pad(flat, (0, (-flat.shape[0]) % TILE_ELEMS))
        pieces.append(flat.reshape(-1, LANES))
    rows = sum(p.shape[0] for p in pieces)
    pad_rows = (-rows) % (N_DEV * SUBLANES)
    if pad_rows:
        pieces.append(jnp.zeros((pad_rows, LANES), pieces[0].dtype))
    return jnp.concatenate(pieces, axis=0)


def _unpack(packed, shapes):
    out, row = [], 0
    for shp in shapes:
        n = math.prod(shp)
        rows = -(-n // TILE_ELEMS) * SUBLANES
        out.append(packed[row:row + rows].reshape(-1)[:n].reshape(shp))
        row += rows
    return out


class _Comm:
    def __init__(self, w):
        first = [w["w_in"].astype(BF16), w["conv_w"]]
        self._first, self.first_token = _exchange_start(first, [_landing_zone(s) for s in first], mode="own",
                                                        name="gather_in_start")
        self._shards = {k: w[k].astype(BF16) for k in BIG if k != "w_in"}
        self._weights, self._gathers, self._grads = {}, {}, {}

    def first_weights(self, after):
        lands = [_exchange_wait(h, after, mode="own", name="gather_in_wait_%d" % i) for i, h in enumerate(self._first)]
        unused = [lax.empty((2 * SUBLANES, LANES), BF16) for _ in lands]
        handles, passed = _exchange_start(unused, lands, mode="pass", name="gather_in_pass")
        w_in, taps = [_exchange_wait(h, passed, mode="pass", name="gather_in_got_%d" % i) for i, h in enumerate(handles)]
        self._weights["w_in"] = w_in
        self.gather_token = self.start_weights(("w_a_out", "glu_w", "glu_v", "w_out"), w_in)
        return self.weight("w_in", None), jnp.swapaxes(taps, 0, 1).reshape(taps.shape[1], -1)

    def start_weights(self, names, after):
        shards = [self._shards.pop(k) for k in names]
        handles, token = _exchange_start(shards, [_landing_zone(s) for s in shards], mode="gather", after=after,
                                         name="gather_start_" + names[0])
        self._gathers.update(zip(names, handles))
        return token

    def weight(self, k, after):
        if k not in self._weights:
            self._weights[k] = _exchange_wait(self._gathers.pop(k), after, mode="gather", name="gather_wait_" + k)
        gk = self._weights[k]
        if k in COL_SHARDED:
            return jnp.swapaxes(gk, 0, 1).reshape(gk.shape[1], -1)
        return gk.reshape(-1, gk.shape[-1])

    def send_grad(self, *names_and_parts):
        names, parts = names_and_parts[0::2], names_and_parts[1::2]
        parts = [p if k in COL_SHARDED else p.reshape(N_DEV, p.shape[0] // N_DEV, p.shape[1])
                 for k, p in zip(names, parts)]
        me = _dev_index(*_mesh_pos())
        lands = [_landing_zone(lax.dynamic_index_in_dim(p, me, 0, keepdims=False)) for p in parts]
        handles, token = _exchange_start(parts, lands, mode="scatter", name="grad_start_" + names[0])
        self._grads.update(zip(names, handles))
        return token

    def received_grad(self, k, after):
        return _exchange_wait(self._grads.pop(k), after, mode="scatter", name="grad_wait_" + k)

    def send_small(self, grads):
        return self.send_grad("small", _pack([grads[k] for k in SMALL]))

    def all_reduced_small(self, after, behind):
        recv = self.received_grad("small", after)
        block = _sum_slots(recv, tr=512, name="sum_small_grads")
        (handle,), started = _exchange_start([block], [_landing_zone(block)], mode="gather", name="small_sum_start")
        done = behind(started)
        return _exchange_wait(handle, done, mode="gather", name="small_sum_wait").reshape(-1, LANES)


SMALL_GROUPS = (("rg_wa", "rg_wx"), ("ssm_b_re",), ("ssm_b_im",),
                tuple(k for k in SMALL if k not in ("rg_wa", "rg_wx", "ssm_b_re", "ssm_b_im")))


def _step(x, tgt, w, m, v, raw_w, raw_m, raw_v):
    dev = _dev_index(*_mesh_pos())

    comm = _Comm(w)
    small = dict(w)
    for k in ("conv_b", "rg_ba", "rg_bx", "rg_lambda", "ln1_g", "ln1_b", "mlp_b_up", "mlp_b_down", "ln2_g", "ln2_b"):
        small[k] = w[k].reshape(1, -1)

    loss_part, grad_x, grads = _local_step(x, tgt, small, comm)

    out_g, out_d, out_m, out_v = {}, {}, {}, {}

    def update_large(started):
        for k in BIG:
            rk = comm.received_grad(k, (grad_x, started))
            out_g[k], out_d[k], out_m[k], out_v[k] = _adamw(rk, w[k], m[k], v[k], tr=256 if k in COL_SHARDED else 128,
                                                                 name="adamw_" + k)
        return out_v[BIG[-1]]

    small_all = comm.all_reduced_small(grad_x, update_large)
    g_small = dict(zip(SMALL, _unpack(small_all, [grads[k].shape for k in SMALL])))
    cw_cols = w["conv_w"].shape[1]
    g_small["conv_w"] = lax.dynamic_slice_in_dim(g_small["conv_w"], dev * cw_cols, cw_cols, axis=1)
    for group in SMALL_GROUPS:
        gs = [g_small[k].reshape(raw_w[k].shape) for k in group]
        res = _adamw_whole(gs, [raw_w[k] for k in group], [raw_m[k] for k in group], [raw_v[k] for k in group],
                           name="adamw_" + group[0])
        for k, gk, (dk, mk, vk) in zip(group, gs, res):
            out_g[k], out_d[k], out_m[k], out_v[k] = gk, dk, mk, vk

    loss = lax.psum(loss_part, ("x", "y", "c"))
    return loss, grad_x, out_g, out_d, out_m, out_v


def kernel(x, w_in, conv_w, conv_b, rg_wa, rg_ba, rg_wx, rg_bx, rg_lambda, w_a_out, ssm_a_re, ssm_a_im, ssm_log_dt, ssm_b_re, ssm_b_im, ssm_c_re, ssm_c_im, ssm_d, glu_w, glu_v, w_out, ln1_g, ln1_b, mlp_w_up, mlp_b_up, mlp_w_down, mlp_b_down, ln2_g, ln2_b, loss_target, m_w_in, m_conv_w, m_conv_b, m_rg_wa, m_rg_ba, m_rg_wx, m_rg_bx, m_rg_lambda, m_w_a_out, m_ssm_a_re, m_ssm_a_im, m_ssm_log_dt, m_ssm_b_re, m_ssm_b_im, m_ssm_c_re, m_ssm_c_im, m_ssm_d, m_glu_w, m_glu_v, m_w_out, m_ln1_g, m_ln1_b, m_mlp_w_up, m_mlp_b_up, m_mlp_w_down, m_mlp_b_down, m_ln2_g, m_ln2_b, v_w_in, v_conv_w, v_conv_b, v_rg_wa, v_rg_ba, v_rg_wx, v_rg_bx, v_rg_lambda, v_w_a_out, v_ssm_a_re, v_ssm_a_im, v_ssm_log_dt, v_ssm_b_re, v_ssm_b_im, v_ssm_c_re, v_ssm_c_im, v_ssm_d, v_glu_w, v_glu_v, v_w_out, v_ln1_g, v_ln1_b, v_mlp_w_up, v_mlp_b_up, v_mlp_w_down, v_mlp_b_down, v_ln2_g, v_ln2_b):
    args = locals()
    w = {k: args[k][0] for k in ORDER}
    m = {k: args["m_" + k][0] for k in BIG}
    v = {k: args["v_" + k][0] for k in BIG}
    raw = [{k: args[prefix + k] for k in SMALL} for prefix in ("", "m_", "v_")]
    loss, grad_x, out_g, out_d, out_m, out_v = _step(x[0], loss_target[0], w, m, v, *raw)
    outs = [loss, grad_x[None]]
    for group in (out_g, out_d, out_m, out_v):
        outs += [group[k].reshape(args[k].shape) for k in ORDER]
    return tuple(outs)
```

```python
import functools
import math

import jax
import jax.numpy as jnp
from jax import lax
from jax.experimental import pallas as pl
from jax.experimental.pallas import tpu as pltpu

F32 = jnp.float32
BF16 = jnp.bfloat16
MESH = pl.DeviceIdType.MESH
N_DEV = 8
SUBLANES = 8
LANES = 128
VMEM_BYTES_V7X = 64 * 2 ** 20
VMEM_CAP = VMEM_BYTES_V7X - 8 * 2 ** 20

ALPHA = 2.0 ** 0.25
LN_EPS = 1e-5
RG_C = 8.0
ADAM_LR, ADAM_B1, ADAM_B2, ADAM_EPS, ADAM_WD, ADAM_STEP = 0.001, 0.9, 0.999, 1e-08, 0.01, 10
GELU_C = math.sqrt(2.0 / math.pi)
GELU_K = 0.044715

ANY = pl.BlockSpec(memory_space=pl.ANY)


def _params(sem, vmem_bytes):
    limit = int(min(max(2 * vmem_bytes, 16 * 2 ** 20), VMEM_CAP))
    return pltpu.CompilerParams(dimension_semantics=sem, vmem_limit_bytes=limit)


def _sig(x):
    return 1.0 / (1.0 + jnp.exp(-x))


def _gelu(x):
    return 0.5 * x * (1.0 + jnp.tanh(GELU_C * (x + GELU_K * x * x * x)))


def _dgelu(x):
    th = jnp.tanh(GELU_C * (x + GELU_K * x * x * x))
    return 0.5 * (1.0 + th) + 0.5 * x * (1.0 - th * th) * (GELU_C * (1.0 + 3.0 * GELU_K * x * x))


def _one_minus_exp(x, exp_half_x):
    p = x * (1.0 + x * (1 / 2 + x * (1 / 6 + x * (1 / 24 + x * (1 / 120)))))
    return jnp.where(x > -1 / 16, -p, 1.0 - exp_half_x * exp_half_x)


def _accumulate(ref, val, first):
    @pl.when(first)
    def _():
        ref[...] = val

    @pl.when(jnp.logical_not(first))
    def _():
        ref[...] += val


def _rows8(cw):
    return lax.broadcasted_iota(jnp.int32, (SUBLANES, cw), 0)


def _shift_down(cur, prev, s, rows):
    return jnp.where(rows < s, pltpu.roll(prev, s, 0), pltpu.roll(cur, s, 0))


def _shift_up(cur, nxt, s, rows):
    return jnp.where(rows < SUBLANES - s, pltpu.roll(cur, SUBLANES - s, 0), pltpu.roll(nxt, SUBLANES - s, 0))


def _mm(a, b, *, M, N, K, ta=False, tb=False, b_split=1, n_split=1, a_fn=None, extras=(), epilogue=None,
        n_out=1, n_cs=0, out_dtypes=None, tm=512, tn=512, tk=512, after=None, into=None, name):
    tm, tn, tk = min(tm, M), min(tn, N), min(tk, K)
    assert M % tm == 0 and N % tn == 0 and K % tk == 0, (name, M, N, K, tm, tn, tk)
    nk = K // tk
    grid = (N // tn, M // tm, nk)
    a_spec = pl.BlockSpec((tk, tm), lambda j, i, k: (k, i)) if ta else pl.BlockSpec((tm, tk), lambda j, i, k: (i, k))
    if b_split == 1:
        b_spec = pl.BlockSpec((tn, tk), lambda j, i, k: (j, k)) if tb else pl.BlockSpec((tk, tn), lambda j, i, k: (k, j))
    elif tb:
        kb = (K // b_split) // tk
        assert kb * tk * b_split == K, name
        b_spec = pl.BlockSpec((None, tn, tk), lambda j, i, k: (k // kb, j, k % kb))
    else:
        nb = (N // b_split) // tn
        assert nb * tn * b_split == N, name
        b_spec = pl.BlockSpec((None, tk, tn), lambda j, i, k: (j // nb, k, j % nb))
    in_specs = [a_spec, b_spec]
    for arr, kind, *col_off in extras:
        off = col_off[0] if col_off else 0
        in_specs.append(pl.BlockSpec((tm, tn), lambda j, i, k, off=off: (i, off + j)) if kind == "mn"
                        else pl.BlockSpec((1, tn), lambda j, i, k: (0, j)))
    out_dtypes = (F32,) * n_out if out_dtypes is None else out_dtypes
    if n_split == 1:
        out_shape = [jax.ShapeDtypeStruct((M, N), dt) for dt in out_dtypes]
        out_specs = [pl.BlockSpec((tm, tn), lambda j, i, k: (i, j)) for _ in range(n_out)]
    else:
        assert n_out == 1
        nbo = (N // n_split) // tn
        assert nbo * tn * n_split == N, name
        out_shape = [jax.ShapeDtypeStruct((n_split, M, N // n_split), out_dtypes[0])]
        out_specs = [pl.BlockSpec((None, tm, tn), lambda j, i, k: (j // nbo, i, j % nbo))]
    out_shape += [jax.ShapeDtypeStruct((1, N), F32) for _ in range(n_cs)]
    out_specs += [pl.BlockSpec((1, tn), lambda j, i, k: (0, j)) for _ in range(n_cs)]
    ne = len(extras)
    dims = (((0 if ta else 1,), (1 if tb else 0,)), ((), ()))

    n_after = 0 if after is None else 1
    in_specs += [ANY] * n_after
    aliases, tail = {}, [] if after is None else [after]
    if into is not None:
        buf, which, col_blk = into
        assert n_split == 1 and buf.shape[0] == M and buf.dtype == out_dtypes[which], name
        aliases = {len(in_specs): which}
        in_specs.append(ANY)
        tail.append(buf)
        out_shape[which] = jax.ShapeDtypeStruct(buf.shape, buf.dtype)
        out_specs[which] = pl.BlockSpec((tm, tn), lambda j, i, k: (i, col_blk + j))

    def body(*refs):
        a_ref, b_ref = refs[0], refs[1]
        ex_refs = refs[2:2 + ne]
        first_out = 2 + ne + len(tail)
        out_refs = refs[first_out:first_out + n_out]
        cs_refs = refs[first_out + n_out:first_out + n_out + n_cs]
        i, k = pl.program_id(1), pl.program_id(2)

        def product():
            av = a_ref[...]
            if a_fn is not None:
                av = a_fn(av.astype(F32))
            return lax.dot_general(av.astype(BF16), b_ref[...].astype(BF16), dims, preferred_element_type=F32)

        def finish(acc):
            res = (acc,) if epilogue is None else epilogue(acc, *[r[...] for r in ex_refs])
            for r, o in zip(out_refs, res[:n_out]):
                r[...] = o.astype(r.dtype)
            for r, cval in zip(cs_refs, res[n_out:]):
                _accumulate(r, jnp.sum(cval, axis=0, keepdims=True), i == 0)

        if nk == 1:
            finish(product())
            return
        acc_ref = refs[-1]

        @pl.when(k == 0)
        def _():
            acc_ref[...] = jnp.zeros_like(acc_ref)

        acc_ref[...] += product()

        @pl.when(k == nk - 1)
        def _():
            finish(acc_ref[...])

    vmem = 2 * tm * tk * a.dtype.itemsize + 2 * tk * tn * b.dtype.itemsize + (1 + 2 * n_out + 2 * ne + 2) * tm * tn * 4
    outs = pl.pallas_call(
        body, name=name, grid=grid, in_specs=in_specs, out_specs=out_specs, out_shape=out_shape,
        scratch_shapes=[pltpu.VMEM((tm, tn), F32)] if nk > 1 else [], input_output_aliases=aliases,
        compiler_params=_params(("parallel", "arbitrary", "arbitrary"), vmem),
    )(a, b, *[e[0] for e in extras], *tail)
    return outs[0] if len(outs) == 1 else outs


BD_STEP = 4

def _bd(pairs, *, T, J, kb, nb, tw=False, extras=(), epilogue=None, n_out=1, n_cs=0, out_dtypes=None, tm=1024, name):
    jb = BD_STEP
    assert T % tm == 0 and J % jb == 0
    grid = (J // jb, T // tm)
    npair, ne = len(pairs), len(extras)
    in_specs, args = [], []
    for arr, off, w in pairs:
        assert off % jb == 0, name
        in_specs.append(pl.BlockSpec((tm, jb * kb), lambda j, i, off=off // jb: (i, off + j)))
        in_specs.append(pl.BlockSpec((jb,) + tuple(w.shape[1:]), lambda j, i: (j, 0, 0)))
        args += [arr, w]
    for arr, kind, off in extras:
        assert off % jb == 0, name
        in_specs.append(pl.BlockSpec((tm, jb * nb), lambda j, i, off=off // jb: (i, off + j)) if kind == "tile"
                        else pl.BlockSpec((1, jb * nb), lambda j, i, off=off // jb: (0, off + j)))
        args.append(arr)
    out_dtypes = (F32,) * n_out if out_dtypes is None else out_dtypes
    out_shape = [jax.ShapeDtypeStruct((T, J * nb), dt) for dt in out_dtypes]
    out_specs = [pl.BlockSpec((tm, jb * nb), lambda j, i: (i, j)) for _ in range(n_out)]
    out_shape += [jax.ShapeDtypeStruct((1, J * nb), F32) for _ in range(n_cs)]
    out_specs += [pl.BlockSpec((1, jb * nb), lambda j, i: (0, j)) for _ in range(n_cs)]
    dims = (((1,), (1 if tw else 0,)), ((), ()))

    def body(*refs):
        ex_refs = refs[2 * npair:2 * npair + ne]
        out_refs = refs[2 * npair + ne:2 * npair + ne + n_out]
        cs_refs = refs[2 * npair + ne + n_out:]
        i = pl.program_id(1)
        for s in range(jb):
            cols_in, cols_out = pl.ds(s * kb, kb), pl.ds(s * nb, nb)
            acc = None
            for p in range(npair):
                d = lax.dot_general(refs[2 * p][:, cols_in].astype(BF16), refs[2 * p + 1][s].astype(BF16), dims,
                                    preferred_element_type=F32)
                acc = d if acc is None else acc + d
            res = (acc,) if epilogue is None else epilogue(acc, *[r[:, cols_out] for r in ex_refs])
            for r, o in zip(out_refs, res[:n_out]):
                r[:, cols_out] = o.astype(r.dtype)
            for r, cval in zip(cs_refs, res[n_out:]):
                _accumulate(r.at[:, cols_out], jnp.sum(cval, axis=0, keepdims=True), i == 0)

    vmem = jb * (2 * npair * tm * kb + 2 * npair * kb * nb + (2 * n_out + 2 * ne + 3) * tm * nb) * 4
    outs = pl.pallas_call(
        body, name=name, grid=grid, in_specs=in_specs, out_specs=out_specs, out_shape=out_shape,
        compiler_params=_params(("parallel", "arbitrary"), vmem),
    )(*args)
    return outs[0] if len(outs) == 1 else outs


def _bdw(a, a_off, b, b_off, *, T, J, kb, nb, tm=1024, name):
    jb = BD_STEP
    assert T % tm == 0 and J % jb == 0 and a_off % jb == 0 and b_off % jb == 0
    a_blk, b_blk = a_off // jb, b_off // jb

    def body(a_ref, b_ref, o_ref):
        i = pl.program_id(1)
        for s in range(jb):
            d = lax.dot_general(a_ref[:, pl.ds(s * kb, kb)].astype(BF16), b_ref[:, pl.ds(s * nb, nb)].astype(BF16),
                                (((0,), (0,)), ((), ())), preferred_element_type=F32)
            _accumulate(o_ref.at[s], d, i == 0)

    return pl.pallas_call(
        body, name=name, grid=(J // jb, T // tm),
        in_specs=[pl.BlockSpec((tm, jb * kb), lambda j, i: (i, a_blk + j)),
                  pl.BlockSpec((tm, jb * nb), lambda j, i: (i, b_blk + j))],
        out_specs=pl.BlockSpec((jb, kb, nb), lambda j, i: (j, 0, 0)),
        out_shape=jax.ShapeDtypeStruct((J, kb, nb), F32),
        compiler_params=_params(("parallel", "arbitrary"), jb * (2 * tm * (kb + nb) + 3 * kb * nb) * 4),
    )(a, b)


def _bd_pack(w, q):
    g, a, b = w.shape
    eye = jnp.eye(q, dtype=w.dtype)
    return jnp.einsum("jqab,qr->jqarb", w.reshape(g // q, q, a, b), eye).reshape(g // q, q * a, q * b)


def _bd_unpack(wp, q):
    j, qa, qb = wp.shape
    a, b = qa // q, qb // q
    w5 = wp.reshape(j, q, a, q, b)
    return jnp.stack([w5[:, r, :, r, :] for r in range(q)], axis=1).reshape(j * q, a, b)


def _ew(fn, ins, *, T, C, n_out, n_cs=0, out_dtypes=None, tm=256, cw=None, name):
    cw = C if cw is None else cw
    assert T % tm == 0 and C % cw == 0
    grid = (C // cw, T // tm)
    in_specs = []
    for arr, kind, off in ins:
        in_specs.append(pl.BlockSpec((tm, cw), lambda j, i, off=off: (i, off + j)) if kind == "tile"
                        else pl.BlockSpec((arr.shape[0], cw), lambda j, i, off=off: (0, off + j)))
    out_dtypes = (F32,) * n_out if out_dtypes is None else out_dtypes
    out_shape = [jax.ShapeDtypeStruct((T, C), dt) for dt in out_dtypes]
    out_specs = [pl.BlockSpec((tm, cw), lambda j, i: (i, j)) for _ in range(n_out)]
    out_shape += [jax.ShapeDtypeStruct((1, C), F32) for _ in range(n_cs)]
    out_specs += [pl.BlockSpec((1, cw), lambda j, i: (0, j)) for _ in range(n_cs)]
    nin = len(ins)

    def body(*refs):
        i = pl.program_id(1)
        res = fn(*[r[...].astype(F32) for r in refs[:nin]])
        for r, o in zip(refs[nin:nin + n_out], res[:n_out]):
            r[...] = o.astype(r.dtype)
        for r, cval in zip(refs[nin + n_out:], res[n_out:]):
            _accumulate(r, jnp.sum(cval, axis=0, keepdims=True), i == 0)

    vmem = (2 * nin + 2 * n_out + 6) * tm * cw * 4
    outs = pl.pallas_call(
        body, name=name, grid=grid, in_specs=in_specs, out_specs=out_specs, out_shape=out_shape,
        compiler_params=_params(("parallel", "arbitrary"), vmem),
    )(*[arr for arr, _, _ in ins])
    return outs[0] if len(outs) == 1 else outs


def _ln_stats(s):
    mu = jnp.mean(s, axis=-1, keepdims=True)
    d = s - mu
    var = jnp.mean(d * d, axis=-1, keepdims=True)
    rstd = lax.rsqrt(var + LN_EPS)
    return d * rstd, rstd


def _ln_bwd(dy, g, xhat, rstd):
    dxh = dy * g
    m1 = jnp.mean(dxh, axis=-1, keepdims=True)
    m2 = jnp.mean(dxh * xhat, axis=-1, keepdims=True)
    return rstd * (dxh - m1 - xhat * m2)


def _conv_fwd(z, conv_w, conv_b, *, T, C, tm=1024, cw=1024, after=None, name):
    ng, hb = tm // SUBLANES, tm // SUBLANES
    n_after = 0 if after is None else 1

    def body(x_ref, halo_ref, w_ref, b_ref, *rest):
        o_ref = rest[-1]
        it = pl.program_id(1)
        rows = _rows8(cw)
        halo = jnp.where(it == 0, 0.0, halo_ref[...])
        w = w_ref[...]
        bias = b_ref[...]

        def group(g, carry):
            off = pl.multiple_of(g * SUBLANES, SUBLANES)
            cur = x_ref[pl.ds(off, SUBLANES), :]
            prev = x_ref[pl.ds(pl.multiple_of(jnp.maximum(off - SUBLANES, 0), SUBLANES), SUBLANES), :]
            prev = jnp.where(g == 0, halo, prev)
            acc = cur * w[3:4] + bias
            for s in (1, 2, 3):
                acc = acc + _shift_down(cur, prev, s, rows) * w[3 - s:4 - s]
            o_ref[pl.ds(off, SUBLANES), :] = acc
            return carry

        lax.fori_loop(0, ng, group, 0, unroll=2)

    return pl.pallas_call(
        body, name=name, grid=(C // cw, T // tm),
        in_specs=[pl.BlockSpec((tm, cw), lambda j, i: (i, j)),
                  pl.BlockSpec((SUBLANES, cw), lambda j, i: (jnp.maximum(i * hb - 1, 0), j)),
                  pl.BlockSpec((4, cw), lambda j, i: (0, j)), pl.BlockSpec((1, cw), lambda j, i: (0, j))]
        + [ANY] * n_after,
        out_specs=pl.BlockSpec((tm, cw), lambda j, i: (i, j)),
        out_shape=jax.ShapeDtypeStruct((T, C), F32),
        compiler_params=_params(("parallel", "arbitrary"), 5 * tm * cw * 4),
    )(z, z, conv_w, conv_b, *([after] if n_after else []))


def _conv_bwd(dxc, z, conv_w, dz, *, T, C, tm=1024, cw=512, name):
    ng, hb, last = tm // SUBLANES, tm // SUBLANES, T // SUBLANES - 1
    nt = T // tm
    rows16 = 2 * SUBLANES

    def body(d_ref, dn_ref, x_ref, w_ref, dz_in_ref, o_ref, sums_ref):
        it = pl.program_id(1)
        rows = _rows8(cw)
        dnext = jnp.where(it == nt - 1, 0.0, dn_ref[...])
        w = w_ref[...]

        def pair(q, accs):
            base = pl.multiple_of(q * rows16, rows16)
            halves = []
            for half in range(2):
                g = 2 * q + half
                off = pl.multiple_of(base + half * SUBLANES, SUBLANES)
                dcur = d_ref[pl.ds(off, SUBLANES), :]
                dnx = d_ref[pl.ds(pl.multiple_of(jnp.minimum(off + SUBLANES, tm - SUBLANES), SUBLANES), SUBLANES), :]
                dnx = jnp.where(g == ng - 1, dnext, dnx)
                xcur = x_ref[pl.ds(off, SUBLANES), :]
                acc = dcur * w[3:4]
                taps = [accs[3] + dcur * xcur]
                for s in (1, 2, 3):
                    ahead = _shift_up(dcur, dnx, s, rows)
                    acc = acc + ahead * w[3 - s:4 - s]
                    taps.append(accs[3 - s] + ahead * xcur)
                halves.append(acc)
                accs = (taps[3], taps[2], taps[1], taps[0], accs[4] + dcur)
            o_ref[pl.ds(base, rows16), :] = jnp.concatenate(halves, axis=0).astype(o_ref.dtype)
            return accs

        zero = jnp.zeros((SUBLANES, cw), F32)
        accs = lax.fori_loop(0, ng // 2, pair, (zero,) * 5)
        sums = jnp.zeros((SUBLANES, cw), F32)
        for k, a in enumerate(accs):
            sums = jnp.where(rows == k, jnp.sum(a, axis=0, keepdims=True), sums)
        _accumulate(sums_ref, sums, it == 0)

    tile = pl.BlockSpec((tm, cw), lambda j, i: (i, j))
    return pl.pallas_call(
        body, name=name, grid=(C // cw, nt),
        in_specs=[tile, pl.BlockSpec((SUBLANES, cw), lambda j, i: (jnp.minimum((i + 1) * hb, last), j)),
                  tile, pl.BlockSpec((4, cw), lambda j, i: (0, j)), ANY],
        out_specs=[tile, pl.BlockSpec((SUBLANES, cw), lambda j, i: (0, j))],
        input_output_aliases={4: 0},
        out_shape=[jax.ShapeDtypeStruct(dz.shape, dz.dtype), jax.ShapeDtypeStruct((SUBLANES, C), F32)],
        compiler_params=_params(("parallel", "arbitrary"), 7 * tm * cw * 4),
    )(dxc, dxc, z, conv_w, dz)


def _rg_coeffs(r, ig, xc, sp):
    la = (-RG_C) * r * sp
    a = jnp.exp(la)
    m = jnp.sqrt(_one_minus_exp(2.0 * la, a))
    return a, m, m * (ig * xc)


def _rg_scan_fwd(z, ri, xc, sp, *, T, C, gate_off, tm=1024, cw=256, name):
    rows16 = 2 * SUBLANES
    nq = tm // rows16

    def body(gate_ref, r_ref, i_ref, xc_ref, sp_ref, h_ref, p_ref, a_ref, m_ref, carry_ref):
        it = pl.program_id(1)

        @pl.when(it == 0)
        def _():
            carry_ref[...] = jnp.zeros_like(carry_ref)

        rows = _rows8(cw)
        sp_row = sp_ref[...]

        def pair(q, carry):
            base = pl.multiple_of(q * rows16, rows16)
            halves = []
            for half in range(2):
                sl = pl.ds(pl.multiple_of(base + half * SUBLANES, SUBLANES), SUBLANES)
                a, m, b = _rg_coeffs(r_ref[sl, :], i_ref[sl, :], xc_ref[sl, :], sp_row)
                a_ref[sl, :] = a
                m_ref[sl, :] = m
                for s in (1, 2, 4):
                    keep = rows >= s
                    sa = jnp.where(keep, pltpu.roll(a, s, 0), 1.0)
                    sb = jnp.where(keep, pltpu.roll(b, s, 0), 0.0)
                    b = b + a * sb
                    a = a * sa
                h = b + a * carry
                h_ref[sl, :] = h
                halves.append(h * _gelu(gate_ref[sl, :]))
                carry = h[SUBLANES - 1:SUBLANES, :]
            p_ref[pl.ds(base, rows16), :] = jnp.concatenate(halves, axis=0).astype(p_ref.dtype)
            return carry

        last = lax.fori_loop(0, nq, pair, carry_ref[0:1, :], unroll=2)
        carry_ref[...] = jnp.broadcast_to(last, carry_ref.shape)

    tile = pl.BlockSpec((tm, cw), lambda j, i: (i, j))
    gate_blk = gate_off // cw
    return pl.pallas_call(
        body, name=name, grid=(C // cw, T // tm),
        in_specs=[pl.BlockSpec((tm, cw), lambda j, i: (i, gate_blk + j)),
                  pl.BlockSpec((tm, cw), lambda j, i: (i, 2 * j)), pl.BlockSpec((tm, cw), lambda j, i: (i, 2 * j + 1)),
                  tile, pl.BlockSpec((1, cw), lambda j, i: (0, j))],
        out_specs=[tile, tile, tile, tile],
        out_shape=[jax.ShapeDtypeStruct((T, C), F32), jax.ShapeDtypeStruct((T, C), BF16),
                   jax.ShapeDtypeStruct((T, C), F32), jax.ShapeDtypeStruct((T, C), F32)],
        scratch_shapes=[pltpu.VMEM((SUBLANES, cw), F32)],
        compiler_params=_params(("parallel", "arbitrary"), 16 * tm * cw * 4),
    )(z, ri, ri, xc, sp)


def _rg_scan_bwd(dh, h, ri, xc, a_fwd, m_fwd, sp, *, T, C, tm=1024, cw=256, name):
    ng, hb, nt = tm // SUBLANES, tm // SUBLANES, T // tm

    def body(dh_ref, h_ref, hp_ref, r_ref, i_ref, xc_ref, a_ref, m_ref, sp_ref,
             drai_ref, dxc_ref, crai_ref, csp_ref, cg_ref, ca_ref):
        step = pl.program_id(1)

        @pl.when(step == 0)
        def _():
            cg_ref[...] = jnp.zeros_like(cg_ref)
            ca_ref[...] = jnp.zeros_like(ca_ref)

        rows = _rows8(cw)
        sp_row = sp_ref[...]
        hhalo = jnp.where(step == nt - 1, 0.0, hp_ref[...])

        def group(gi, carry):
            g_next, a_next, s_ra, s_ia, s_sp = carry
            g = ng - 1 - gi
            off = pl.multiple_of(g * SUBLANES, SUBLANES)
            sl = pl.ds(off, SUBLANES)
            rr, ii, xx = r_ref[sl, :], i_ref[sl, :], xc_ref[sl, :]
            a, m = a_ref[sl, :], m_ref[sl, :]
            hh = h_ref[sl, :]
            hpv = h_ref[pl.ds(pl.multiple_of(jnp.maximum(off - SUBLANES, 0), SUBLANES), SUBLANES), :]
            hpv = jnp.where(g == 0, hhalo, hpv)
            hprev = _shift_down(hh, hpv, 1, rows)
            d = dh_ref[sl, :]
            c = jnp.where(rows < SUBLANES - 1, pltpu.roll(a, SUBLANES - 1, 0), a_next)
            for s in (1, 2, 4):
                keep = rows < SUBLANES - s
                sc = jnp.where(keep, pltpu.roll(c, SUBLANES - s, 0), 1.0)
                sd = jnp.where(keep, pltpu.roll(d, SUBLANES - s, 0), 0.0)
                d = d + c * sd
                c = c * sc
            gg = d + c * g_next
            da = gg * hprev
            dm = gg * (ii * xx)
            di = gg * (m * xx)
            dxc_ref[sl, :] = gg * (m * ii)
            dla = da * a - dm * (a * a / m)
            dra = dla * ((-RG_C) * sp_row) * (rr * (1.0 - rr))
            dia = di * (ii * (1.0 - ii))
            drai_ref[sl, pl.ds(0, cw)] = dra
            drai_ref[sl, pl.ds(cw, cw)] = dia
            return (gg[0:1, :], a[0:1, :], s_ra + dra, s_ia + dia, s_sp + dla * ((-RG_C) * rr))

        zero = jnp.zeros((SUBLANES, cw), F32)
        g_first, a_first, s_ra, s_ia, s_sp = lax.fori_loop(
            0, ng, group, (cg_ref[0:1, :], ca_ref[0:1, :], zero, zero, zero), unroll=2)
        cg_ref[...] = jnp.broadcast_to(g_first, cg_ref.shape)
        ca_ref[...] = jnp.broadcast_to(a_first, ca_ref.shape)
        for ref, acc in ((crai_ref.at[:, pl.ds(0, cw)], s_ra), (crai_ref.at[:, pl.ds(cw, cw)], s_ia), (csp_ref, s_sp)):
            _accumulate(ref, jnp.sum(acc, axis=0, keepdims=True), step == 0)

    tile = pl.BlockSpec((tm, cw), lambda j, i: (nt - 1 - i, j))
    wide = pl.BlockSpec((tm, 2 * cw), lambda j, i: (nt - 1 - i, j))
    vec = pl.BlockSpec((1, cw), lambda j, i: (0, j))
    return pl.pallas_call(
        body, name=name, grid=(C // cw, nt),
        in_specs=[tile, tile, pl.BlockSpec((SUBLANES, cw), lambda j, i: (jnp.maximum((nt - 1 - i) * hb - 1, 0), j)),
                  pl.BlockSpec((tm, cw), lambda j, i: (nt - 1 - i, 2 * j)),
                  pl.BlockSpec((tm, cw), lambda j, i: (nt - 1 - i, 2 * j + 1)), tile, tile, tile, vec],
        out_specs=[wide, tile, pl.BlockSpec((1, 2 * cw), lambda j, i: (0, j)), vec],
        out_shape=[jax.ShapeDtypeStruct((T, 2 * C), F32), jax.ShapeDtypeStruct((T, C), F32),
                   jax.ShapeDtypeStruct((1, 2 * C), F32), jax.ShapeDtypeStruct((1, C), F32)],
        scratch_shapes=[pltpu.VMEM((SUBLANES, cw), F32), pltpu.VMEM((SUBLANES, cw), F32)],
        compiler_params=_params(("parallel", "arbitrary"), 24 * tm * cw * 4),
    )(dh, h, h, ri, ri, xc, a_fwd, m_fwd, sp)


def _cscan_tables(lr, li, reverse):
    lam = (lr.reshape(-1), -li.reshape(-1) if reverse else li.reshape(-1))

    def mul(p, q):
        return p[0] * q[0] - p[1] * q[1], p[0] * q[1] + p[1] * q[0]

    pows = [lam]
    for _ in range(SUBLANES - 1):
        pows.append(mul(pows[-1], lam))
    zero = jnp.zeros_like(lam[0])
    tab = jnp.stack([pows[0][0], pows[0][1], pows[1][0], pows[1][1], pows[3][0], pows[3][1], zero, zero])
    if reverse:
        pows = pows[::-1]
    return tab, jnp.stack([p[0] for p in pows]), jnp.stack([p[1] for p in pows])


def _power_slabs(lr, li, n):
    pr, pi = lr.reshape(1, -1), li.reshape(1, -1)
    while pr.shape[0] < n:
        tr, ti = pr[-1:], pi[-1:]
        pr, pi = (jnp.concatenate([pr, pr * tr - pi * ti], axis=0), jnp.concatenate([pi, pr * ti + pi * tr], axis=0))
    return jnp.repeat(pr, SUBLANES, axis=0), jnp.repeat(pi, SUBLANES, axis=0), pr[-1], pi[-1]


def _rows_to_segments(src_ref, dst_ref):
    seg = src_ref.shape[0] // SUBLANES
    for g in range(seg):
        dst_ref[pl.ds(g * SUBLANES, SUBLANES), :] = src_ref[pl.ds(g, SUBLANES, stride=seg), :].astype(dst_ref.dtype)


def _segments_to_rows(src_ref, dst_ref):
    seg = src_ref.shape[0] // SUBLANES
    for r in range(SUBLANES):
        dst_ref[pl.ds(r * seg, seg), :] = src_ref[pl.ds(r, seg, stride=SUBLANES), :].astype(dst_ref.dtype)


def _seg_scan_tile(xr_ref, xi_ref, pbr_ref, pbi_ref, tab_ref, pwr_ref, pwi_ref, cr_ref, ci_ref, *, reverse, h=None):
    tm, cw = xr_ref.shape
    seg = tm // SUBLANES
    rows = _rows8(cw)
    sign = -1.0 if reverse else 1.0
    l_re, l_im = pbr_ref[0:1, :], sign * pbi_ref[0:1, :]

    def slab(g):
        return pl.ds(pl.multiple_of(g * SUBLANES, SUBLANES), SUBLANES)

    def local(k, state):
        sl = slab(seg - 1 - k if reverse else k)
        sr, si = state
        nr = xr_ref[sl, :] + (l_re * sr - l_im * si)
        ni = xi_ref[sl, :] + (l_re * si + l_im * sr)
        xr_ref[sl, :] = nr
        xi_ref[sl, :] = ni
        return nr, ni

    zero = jnp.zeros((SUBLANES, cw), F32)
    er, ei = lax.fori_loop(0, seg, local, (zero, zero), unroll=2)

    for k, s in enumerate((1, 2, 4)):
        shift = SUBLANES - s if reverse else s
        keep = rows < SUBLANES - s if reverse else rows >= s
        sr = jnp.where(keep, pltpu.roll(er, shift, 0), 0.0)
        si = jnp.where(keep, pltpu.roll(ei, shift, 0), 0.0)
        m_re, m_im = tab_ref[2 * k:2 * k + 1, :], tab_ref[2 * k + 1:2 * k + 2, :]
        er, ei = er + (m_re * sr - m_im * si), ei + (m_re * si + m_im * sr)
    cin_r, cin_i = cr_ref[0:1, :], ci_ref[0:1, :]
    pwr, pwi = pwr_ref[...], pwi_ref[...]
    er, ei = er + (pwr * cin_r - pwi * cin_i), ei + (pwr * cin_i + pwi * cin_r)
    if reverse:
        ent_r = jnp.where(rows == SUBLANES - 1, cin_r, pltpu.roll(er, SUBLANES - 1, 0))
        ent_i = jnp.where(rows == SUBLANES - 1, cin_i, pltpu.roll(ei, SUBLANES - 1, 0))
        out_r, out_i = er[0:1, :], ei[0:1, :]
    else:
        ent_r = jnp.where(rows == 0, cin_r, pltpu.roll(er, 1, 0))
        ent_i = jnp.where(rows == 0, cin_i, pltpu.roll(ei, 1, 0))
        out_r, out_i = er[SUBLANES - 1:SUBLANES, :], ei[SUBLANES - 1:SUBLANES, :]
    cr_ref[...] = jnp.broadcast_to(out_r, cr_ref.shape)
    ci_ref[...] = jnp.broadcast_to(out_i, ci_ref.shape)

    if h is not None:
        hr_ref, hi_ref, hr_last, hi_last = h
        hr_wrap = _shift_down(hr_ref[pl.ds(tm - SUBLANES, SUBLANES), :], hr_last, 1, rows)
        hi_wrap = _shift_down(hi_ref[pl.ds(tm - SUBLANES, SUBLANES), :], hi_last, 1, rows)

    def fix(g, sums):
        sl = slab(g)
        power = slab(seg - 1 - g) if reverse else sl
        pr, pi = pbr_ref[power, :], sign * pbi_ref[power, :]
        nr = xr_ref[sl, :] + (pr * ent_r - pi * ent_i)
        ni = xi_ref[sl, :] + (pr * ent_i + pi * ent_r)
        xr_ref[sl, :] = nr
        xi_ref[sl, :] = ni
        if h is None:
            return sums
        before = slab(jnp.maximum(g - 1, 0))
        hr1 = jnp.where(g == 0, hr_wrap, hr_ref[before, :])
        hi1 = jnp.where(g == 0, hi_wrap, hi_ref[before, :])
        return sums[0] + (nr * hr1 + ni * hi1), sums[1] + (ni * hr1 - nr * hi1)

    return lax.fori_loop(0, seg, fix, (zero, zero) if h is not None else (), unroll=2)


S5_TILE = 2048


def _s5_fwd(z, u_off, wb_re, wb_im, wc_re, wc_im_neg, d_row, powers, *, T, tm=S5_TILE, name):
    J, ku, kp = wb_re.shape
    nt = T // tm
    pb_re, pb_im, top_re, top_im = powers
    tab, pw_re, pw_im = _cscan_tables(top_re, top_im, False)
    u_blk = u_off // ku

    def body(u_ref, wbr_ref, wbi_ref, wcr_ref, wci_ref, d_ref, pbr_ref, pbi_ref, tab_ref, pwr_ref, pwi_ref,
             hr_ref, hi_ref, y_ref, yg_ref, cr_ref, ci_ref, us_ref, ys_ref):
        @pl.when(pl.program_id(1) == 0)
        def _():
            cr_ref[...] = jnp.zeros_like(cr_ref)
            ci_ref[...] = jnp.zeros_like(ci_ref)

        _rows_to_segments(u_ref, us_ref)
        u = us_ref[...]
        ub = u.astype(BF16)
        hr_ref[...] = jnp.dot(ub, wbr_ref[...], preferred_element_type=F32)
        hi_ref[...] = jnp.dot(ub, wbi_ref[...], preferred_element_type=F32)
        _seg_scan_tile(hr_ref, hi_ref, pbr_ref, pbi_ref, tab_ref, pwr_ref, pwi_ref, cr_ref, ci_ref, reverse=False)
        y = (jnp.dot(hr_ref[...].astype(BF16), wcr_ref[...], preferred_element_type=F32)
             + jnp.dot(hi_ref[...].astype(BF16), wci_ref[...], preferred_element_type=F32) + d_ref[...] * u)
        ys_ref[...] = y
        _segments_to_rows(ys_ref, y_ref)
        ys_ref[...] = _gelu(y)
        _segments_to_rows(ys_ref, yg_ref)

    wb_spec = pl.BlockSpec((None, ku, kp), lambda j, i: (j, 0, 0))
    wc_spec = pl.BlockSpec((None, kp, ku), lambda j, i: (j, 0, 0))
    small = pl.BlockSpec((SUBLANES, kp), lambda j, i: (0, j))
    slabs = pl.BlockSpec((tm, kp), lambda j, i: (0, j))
    state = pl.BlockSpec((tm, kp), lambda j, i: (i, j))
    chan = pl.BlockSpec((tm, ku), lambda j, i: (i, j))
    return pl.pallas_call(
        body, name=name, grid=(J, nt),
        in_specs=[pl.BlockSpec((tm, ku), lambda j, i: (i, u_blk + j)), wb_spec, wb_spec, wc_spec, wc_spec,
                  pl.BlockSpec((1, ku), lambda j, i: (0, j)), slabs, slabs, small, small, small],
        out_specs=[state, state, chan, chan],
        out_shape=[jax.ShapeDtypeStruct((T, J * kp), F32)] * 2
        + [jax.ShapeDtypeStruct((T, J * ku), F32), jax.ShapeDtypeStruct((T, J * ku), BF16)],
        scratch_shapes=[pltpu.VMEM((SUBLANES, kp), F32), pltpu.VMEM((SUBLANES, kp), F32),
                        pltpu.VMEM((tm, ku), F32), pltpu.VMEM((tm, ku), F32)],
        compiler_params=_params(("parallel", "arbitrary"), 14 * tm * kp * 4),
    )(z, wb_re, wb_im, wc_re, wc_im_neg, d_row, pb_re, pb_im, tab, pw_re, pw_im)


def _s5_bwd(dy, z, u_off, h_re, h_im, wb_re, wb_im, wc_re, wc_im_neg, d_row, powers, dz, *, T, tm=S5_TILE, name):
    J, ku, kp = wb_re.shape
    nt, hb = T // tm, tm // SUBLANES
    pb_re, pb_im, top_re, top_im = powers
    tab, pw_re, pw_im = _cscan_tables(top_re, top_im, True)
    u_blk = u_off // ku
    contract_rows = (((0,), (0,)), ((), ()))
    contract_cols = (((1,), (1,)), ((), ()))

    def body(dy_ref, u_ref, hr_ref, hrp_ref, hi_ref, hip_ref, wbr_ref, wbi_ref, wcr_ref, wci_ref, d_ref,
             pbr_ref, pbi_ref, tab_ref, pwr_ref, pwi_ref, dz_in_ref,
             du_ref, dlr_ref, dli_ref, dd_ref, dwbr_ref, dwbi_ref, dwcr_ref, dwci_ref,
             gr_ref, gi_ref, cr_ref, ci_ref, dys_ref, us_ref):
        step = pl.program_id(1)
        first = step == 0

        @pl.when(first)
        def _():
            cr_ref[...] = jnp.zeros_like(cr_ref)
            ci_ref[...] = jnp.zeros_like(ci_ref)

        _rows_to_segments(dy_ref, dys_ref)
        _rows_to_segments(u_ref, us_ref)
        dy_t, u = dys_ref[...], us_ref[...]
        dyb, ub = dy_t.astype(BF16), u.astype(BF16)
        gr_ref[...] = lax.dot_general(dyb, wcr_ref[...], contract_cols, preferred_element_type=F32)
        gi_ref[...] = lax.dot_general(dyb, wci_ref[...], contract_cols, preferred_element_type=F32)
        hr_last = jnp.where(step == nt - 1, 0.0, hrp_ref[...])
        hi_last = jnp.where(step == nt - 1, 0.0, hip_ref[...])
        s_re, s_im = _seg_scan_tile(gr_ref, gi_ref, pbr_ref, pbi_ref, tab_ref, pwr_ref, pwi_ref, cr_ref, ci_ref,
                                    reverse=True, h=(hr_ref, hi_ref, hr_last, hi_last))
        _accumulate(dlr_ref, jnp.sum(s_re, axis=0, keepdims=True), first)
        _accumulate(dli_ref, jnp.sum(s_im, axis=0, keepdims=True), first)
        grb, gib = gr_ref[...].astype(BF16), gi_ref[...].astype(BF16)
        du = (lax.dot_general(grb, wbr_ref[...], contract_cols, preferred_element_type=F32)
              + lax.dot_general(gib, wbi_ref[...], contract_cols, preferred_element_type=F32) + dy_t * d_ref[...])
        dys_ref[...] = du
        _segments_to_rows(dys_ref, du_ref)
        _accumulate(dd_ref, jnp.sum(dy_t * u, axis=0, keepdims=True), first)
        _accumulate(dwbr_ref, lax.dot_general(ub, grb, contract_rows, preferred_element_type=F32), first)
        _accumulate(dwbi_ref, lax.dot_general(ub, gib, contract_rows, preferred_element_type=F32), first)
        _accumulate(dwcr_ref, lax.dot_general(dyb, hr_ref[...].astype(BF16), contract_rows,
                                              preferred_element_type=F32), first)
        _accumulate(dwci_ref, lax.dot_general(dyb, hi_ref[...].astype(BF16), contract_rows,
                                              preferred_element_type=F32), first)

    def tix(i):
        return nt - 1 - i

    wb_spec = pl.BlockSpec((None, ku, kp), lambda j, i: (j, 0, 0))
    wc_spec = pl.BlockSpec((None, kp, ku), lambda j, i: (j, 0, 0))
    small = pl.BlockSpec((SUBLANES, kp), lambda j, i: (0, j))
    state = pl.BlockSpec((tm, kp), lambda j, i: (tix(i), j))
    halo = pl.BlockSpec((SUBLANES, kp), lambda j, i: (jnp.maximum(tix(i) * hb - 1, 0), j))
    chan = pl.BlockSpec((tm, ku), lambda j, i: (tix(i), j))
    svec = pl.BlockSpec((1, kp), lambda j, i: (0, j))
    cvec = pl.BlockSpec((1, ku), lambda j, i: (0, j))
    slabs = pl.BlockSpec((tm, kp), lambda j, i: (0, j))
    return pl.pallas_call(
        body, name=name, grid=(J, nt),
        in_specs=[chan, pl.BlockSpec((tm, ku), lambda j, i: (tix(i), u_blk + j)), state, halo, state, halo,
                  wb_spec, wb_spec, wc_spec, wc_spec, cvec, slabs, slabs, small, small, small, ANY],
        out_specs=[pl.BlockSpec((tm, ku), lambda j, i: (tix(i), u_blk + j)), svec, svec, cvec,
                   wb_spec, wb_spec, wb_spec, wb_spec],
        input_output_aliases={16: 0},
        out_shape=[jax.ShapeDtypeStruct(dz.shape, dz.dtype), jax.ShapeDtypeStruct((1, J * kp), F32),
                   jax.ShapeDtypeStruct((1, J * kp), F32), jax.ShapeDtypeStruct((1, J * ku), F32),
                   jax.ShapeDtypeStruct((J, ku, kp), F32), jax.ShapeDtypeStruct((J, ku, kp), F32),
                   jax.ShapeDtypeStruct((J, ku, kp), F32), jax.ShapeDtypeStruct((J, ku, kp), F32)],
        scratch_shapes=[pltpu.VMEM((tm, kp), F32), pltpu.VMEM((tm, kp), F32),
                        pltpu.VMEM((SUBLANES, kp), F32), pltpu.VMEM((SUBLANES, kp), F32),
                        pltpu.VMEM((tm, ku), F32), pltpu.VMEM((tm, ku), F32)],
        compiler_params=_params(("parallel", "arbitrary"), 16 * tm * kp * 4),
    )(dy, z, h_re, h_re, h_im, h_im, wb_re, wb_im, wc_re, wc_im_neg, d_row, pb_re, pb_im, tab, pw_re, pw_im, dz)


def _mesh_pos():
    return lax.axis_index("x"), lax.axis_index("y"), lax.axis_index("c")


def _dev_index(px, py, pc):
    return 4 * px + 2 * py + pc


HBM = pl.BlockSpec(memory_space=pltpu.HBM)
SEM = pl.BlockSpec(memory_space=pltpu.SEMAPHORE)
EFFECT = pltpu.SideEffectType.DATAFLOW_SIDE_EFFECTING
RELATIONS = [(dx, dy, dc) for dx in (0, 1) for dy in (0, 1) for dc in (0, 1) if (dx, dy, dc) != (0, 0, 0)]


def _peer(rel):
    x, y, c = _mesh_pos()
    dx, dy, dc = rel
    return (x + dx - 2 * x * dx, y + dy - 2 * y * dy, c + dc - 2 * c * dc)


CHIP_RELATIONS = [(1, 0, 0), (0, 1, 0), (1, 1, 0)]
EXCHANGE_PEERS = {"gather": RELATIONS, "scatter": RELATIONS, "own": [(0, 0, 1)] + CHIP_RELATIONS, "pass": CHIP_RELATIONS}


def _split_copy(src_ref, land_ref, send_sems, recv_sems, k, mode, incoming):
    x, y, c = _mesh_pos()
    me = _dev_index(x, y, c)
    peer = _peer(EXCHANGE_PEERS[mode][k])
    if mode == "pass":
        held, theirs = _dev_index(peer[0], peer[1], c), _dev_index(peer[0], peer[1], 1 - c)
        src, slot, target = land_ref.at[held], theirs if incoming else held, (x, y, 1 - c)
    else:
        src = src_ref.at[_dev_index(*peer)] if mode == "scatter" else src_ref
        slot, target = _dev_index(*peer) if incoming else me, peer
    return pltpu.make_async_remote_copy(src_ref=src, dst_ref=land_ref.at[slot], send_sem=send_sems.at[k],
                                        recv_sem=recv_sems.at[k], device_id=target, device_id_type=MESH)


def _exchange_start(srcs, lands, *, mode, after=None, name):
    n = len(srcs)
    n_after = 0 if after is None else 1
    n_rel = len(EXCHANGE_PEERS[mode])

    def body(*refs):
        src_refs, land_refs = refs[:n], refs[n:2 * n]
        first_out = 2 * n + n_after
        send, recv = refs[first_out:first_out + n], refs[first_out + n:first_out + 2 * n]
        token = refs[-1]
        for k in range(n_rel):
            for a in range(n):
                _split_copy(src_refs[a], land_refs[a], send[a], recv[a], k, mode, incoming=False).start()
        token[...] = jnp.zeros_like(token)

    outs = pl.pallas_call(
        body, name=name, in_specs=[HBM] * (2 * n) + [ANY] * n_after,
        out_shape=[pltpu.SemaphoreType.DMA((n_rel,))] * (2 * n)
        + [pltpu.HBM(s.shape, s.dtype) for s in srcs] + [pltpu.HBM(s.shape, s.dtype) for s in lands]
        + [jax.ShapeDtypeStruct((SUBLANES, LANES), F32)],
        out_specs=[SEM] * (2 * n) + [HBM] * (2 * n) + [pl.BlockSpec(memory_space=pltpu.VMEM)],
        input_output_aliases={**{a: 2 * n + a for a in range(n)}, **{n + a: 3 * n + a for a in range(n)}},
        compiler_params=pltpu.CompilerParams(has_side_effects=EFFECT),
    )(*[pltpu.with_memory_space_constraint(s, pltpu.HBM) for s in srcs],
      *[pltpu.with_memory_space_constraint(s, pltpu.HBM) for s in lands], *([after] if n_after else []))
    per_array = [(outs[a], outs[n + a], outs[2 * n + a], outs[3 * n + a]) for a in range(n)]
    return per_array, outs[-1]


def _exchange_wait(handle, after, *, mode, name):
    send_sems, recv_sems, src_thru, land_thru = handle
    after = after if isinstance(after, (tuple, list)) else (after,)

    def body(src_ref, land_ref, send, recv, *rest):
        for k in range(len(EXCHANGE_PEERS[mode])):
            cp = _split_copy(src_ref, land_ref, send, recv, k, mode, incoming=True)
            cp.wait_send()
            cp.wait_recv()

    return pl.pallas_call(
        body, name=name, in_specs=[HBM, HBM, SEM, SEM] + [ANY] * len(after),
        out_shape=[pltpu.HBM(src_thru.shape, src_thru.dtype), pltpu.HBM(land_thru.shape, land_thru.dtype)],
        out_specs=[HBM, HBM], input_output_aliases={0: 0, 1: 1},
        compiler_params=pltpu.CompilerParams(has_side_effects=EFFECT),
    )(src_thru, land_thru, send_sems, recv_sems, *after)[1]


def _landing_zone(own_block):
    me = _dev_index(*_mesh_pos())
    zone = lax.empty((N_DEV,) + own_block.shape, own_block.dtype)
    return lax.dynamic_update_index_in_dim(zone, own_block, me, 0)


def _row_tile(rows, want):
    t = min(want, rows) // SUBLANES * SUBLANES
    while rows % t:
        t -= SUBLANES
    return t


def _sum_slots(recv, *, tr, name):
    s_, r_, c_ = recv.shape
    tr = _row_tile(r_, tr)

    def body(g_ref, o_ref):
        acc = g_ref[0]
        for s in range(1, s_):
            acc = acc + g_ref[s]
        o_ref[...] = acc

    return pl.pallas_call(
        body, name=name, grid=(r_ // tr,),
        in_specs=[pl.BlockSpec((s_, tr, c_), lambda i: (0, i, 0))],
        out_specs=pl.BlockSpec((tr, c_), lambda i: (i, 0)),
        out_shape=jax.ShapeDtypeStruct((r_, c_), F32),
        compiler_params=_params(("parallel",), (2 * s_ + 3) * tr * c_ * 4),
    )(recv)


def _adamw(recv, w, m, v, *, tr, name):
    s_, r_, c_ = recv.shape
    tr = _row_tile(r_, tr)
    assert w.shape == (r_, c_), (name, w.shape, recv.shape)
    c1 = 1.0 - ADAM_B1 ** ADAM_STEP
    c2 = 1.0 - ADAM_B2 ** ADAM_STEP

    def body(g_ref, w_ref, m_ref, v_ref, go_ref, d_ref, mo_ref, vo_ref):
        g = g_ref[0].astype(F32)
        for s in range(1, s_):
            g = g + g_ref[s].astype(F32)
        mn = ADAM_B1 * m_ref[...] + (1.0 - ADAM_B1) * g
        vn = ADAM_B2 * v_ref[...] + (1.0 - ADAM_B2) * (g * g)
        go_ref[...] = g
        mo_ref[...] = mn
        vo_ref[...] = vn
        d_ref[...] = -ADAM_LR * ((mn / c1) / (jnp.sqrt(vn / c2) + ADAM_EPS) + ADAM_WD * w_ref[...])

    tile = pl.BlockSpec((tr, c_), lambda i: (i, 0))
    return pl.pallas_call(
        body, name=name, grid=(r_ // tr,),
        in_specs=[pl.BlockSpec((s_, tr, c_), lambda i: (0, i, 0)), tile, tile, tile],
        out_specs=[tile] * 4, out_shape=[jax.ShapeDtypeStruct((r_, c_), F32)] * 4,
        compiler_params=_params(("parallel",), (2 * s_ + 16) * tr * c_ * 4),
    )(recv, w, m, v)


def _adamw_whole(gs, ws, ms, vs, *, name):
    n = len(gs)
    c1 = 1.0 - ADAM_B1 ** ADAM_STEP
    c2 = 1.0 - ADAM_B2 ** ADAM_STEP

    def body(*refs):
        for i in range(n):
            g, w = refs[i][...], refs[n + i][...]
            mn = ADAM_B1 * refs[2 * n + i][...] + (1.0 - ADAM_B1) * g
            vn = ADAM_B2 * refs[3 * n + i][...] + (1.0 - ADAM_B2) * (g * g)
            refs[4 * n + 3 * i][...] = -ADAM_LR * ((mn / c1) / (jnp.sqrt(vn / c2) + ADAM_EPS) + ADAM_WD * w)
            refs[4 * n + 3 * i + 1][...] = mn
            refs[4 * n + 3 * i + 2][...] = vn

    whole = pl.BlockSpec(memory_space=pltpu.VMEM)
    lane_padded = sum(math.prod(g.shape[:-1]) * (-(-g.shape[-1] // LANES) * LANES) for g in gs)
    outs = pl.pallas_call(
        body, name=name, in_specs=[whole] * (4 * n), out_specs=[whole] * (3 * n),
        out_shape=[jax.ShapeDtypeStruct(g.shape, F32) for g in gs for _ in range(3)],
        compiler_params=pltpu.CompilerParams(vmem_limit_bytes=int(min(max(16 * lane_padded * 4, 16 * 2 ** 20), VMEM_CAP))),
    )(*gs, *ws, *ms, *vs)
    return [tuple(outs[3 * i:3 * i + 3]) for i in range(n)]


def _s5_discretise(a_re, a_im, log_dt, b_re, b_im):
    dt = jnp.exp(log_dt)[:, None]
    lr = jnp.minimum(a_re, -1e-4)
    li = a_im
    mag = jnp.exp(lr * dt)
    lbr = mag * jnp.cos(li * dt)
    lbi = mag * jnp.sin(li * dt)
    zr, zi = lbr - 1.0, lbi
    den = lr * lr + li * li
    fr = (zr * lr + zi * li) / den
    fi = (zi * lr - zr * li) / den
    bbr = fr[..., None] * b_re - fi[..., None] * b_im
    bbi = fr[..., None] * b_im + fi[..., None] * b_re
    return lbr, lbi, bbr, bbi


def _softplus_neg(lam):
    return jnp.maximum(-lam, 0.0) + jnp.log(1.0 + jnp.exp(-jnp.abs(lam)))


S5_Q = 8
RG_Q = 2


def _local_step(x, tgt, W, comm):
    T, D = x.shape
    C = D
    G, P, H = W["ssm_b_re"].shape
    S = G * H
    F = W["mlp_b_up"].shape[1]
    n_in = 2 * C + S + 2 * D
    heads, hd = W["rg_wa"].shape[0], W["rg_wa"].shape[1]
    u_off, ga_off, gb_off = 2 * C, 2 * C + S, 2 * C + S + D

    if comm.first_token is not None:
        anchored = ("rg_lambda", "ssm_a_re", "rg_wa", "rg_wx", "ssm_c_re", "ssm_c_im")
        W = {**W, **{k: W[k] + comm.first_token[0, 0] for k in anchored}}
    sp, sp_vjp = jax.vjp(_softplus_neg, W["rg_lambda"])
    (lbr, lbi, bbr, bbi), s5_vjp = jax.vjp(_s5_discretise, W["ssm_a_re"], W["ssm_a_im"], W["ssm_log_dt"],
                                           W["ssm_b_re"], W["ssm_b_im"])
    lam_re, lam_im = lbr.reshape(-1), lbi.reshape(-1)
    jr, kr = heads // RG_Q, RG_Q * hd
    w_ri = jnp.concatenate([_bd_pack(W["rg_wa"], RG_Q), _bd_pack(W["rg_wx"], RG_Q)], axis=2).astype(BF16)
    b_ri = jnp.concatenate([W["rg_ba"].reshape(jr, kr), W["rg_bx"].reshape(jr, kr)], axis=1).reshape(1, -1)
    wb_re = _bd_pack(jnp.swapaxes(bbr, 1, 2), S5_Q).astype(BF16)
    wb_im = _bd_pack(jnp.swapaxes(bbi, 1, 2), S5_Q).astype(BF16)
    wc_re = _bd_pack(jnp.swapaxes(W["ssm_c_re"], 1, 2), S5_Q).astype(BF16)
    wc_im_neg = _bd_pack(jnp.swapaxes(-W["ssm_c_im"], 1, 2), S5_Q).astype(BF16)
    d_row = W["ssm_d"].reshape(1, S)
    powers = _power_slabs(lam_re, lam_im, S5_TILE // SUBLANES)

    x_bf = x.astype(BF16) if comm.first_token is None else (x + comm.first_token[0, 0]).astype(BF16)
    w_in, conv_w = comm.first_weights((x_bf, w_ri, wb_re, wb_im, wc_re, wc_im_neg, powers[0], powers[1]))
    z = _mm(x_bf, w_in, M=T, N=n_in, K=D, tm=1024, tn=n_in // 4, tk=D, after=comm.gather_token, name="fwd_in_proj")
    started = comm.start_weights(("mlp_w_up",), z)
    xc = _conv_fwd(z, conv_w, W["conv_b"], T=T, C=C, after=started, name="fwd_conv")
    ri = _bd([(xc, 0, w_ri)], T=T, J=jr, kb=kr, nb=2 * kr, extras=[(b_ri, "vec", 0)],
             epilogue=lambda acc, b: (_sig(acc + b),), name="fwd_gates")
    h, p, a_fwd, m_fwd = _rg_scan_fwd(z, ri, xc, sp, T=T, C=C, gate_off=C, cw=kr, name="fwd_rg_scan")
    w_a_out = comm.weight("w_a_out", p)
    started = comm.start_weights(("mlp_w_down",), p)
    y_a = _mm(p, w_a_out, M=T, N=D, K=C, out_dtypes=(BF16,), tm=512, tn=D, tk=C, after=started, name="fwd_rg_out")

    h_re, h_im, y_s, yg = _s5_fwd(z, u_off, wb_re, wb_im, wc_re, wc_im_neg, d_row, powers, T=T, name="fwd_s5")
    w_glu_w, w_glu_v = comm.weight("glu_w", yg), comm.weight("glu_v", yg)
    glu_a = _mm(yg, w_glu_w, M=T, N=D, K=S, out_dtypes=(BF16,), tm=1024, tn=D, tk=S, name="fwd_glu_w")
    cwm = 1024

    def mix_fn(b, ga, gb, ya, a):
        return b, _sig(ga) * ya.astype(F32) + _sig(gb) * (a.astype(F32) * _sig(b))

    glu_b, mix = _mm(yg, w_glu_v, M=T, N=D, K=S, tm=512, tn=cwm, tk=S,
                     extras=[(z, "mn", ga_off // cwm), (z, "mn", gb_off // cwm), (y_a, "mn"), (glu_a, "mn")],
                     epilogue=mix_fn, n_out=2, out_dtypes=(BF16, BF16), name="fwd_glu_v_mix")
    w_out = comm.weight("w_out", mix)
    def out_ln1_fn(acc, xv, g, b):
        s = ALPHA * xv + acc
        xhat, _ = _ln_stats(s)
        y = xhat * g + b
        return s, y, y

    s1, x1, x1_bf = _mm(mix, w_out, M=T, N=D, K=D, tm=256, tn=D, tk=D,
                        extras=[(x, "mn"), (W["ln1_g"], "n"), (W["ln1_b"], "n")], epilogue=out_ln1_fn, n_out=3,
                        out_dtypes=(F32, F32, BF16), name="fwd_out_proj_ln1")
    w_up = comm.weight("mlp_w_up", x1_bf)

    def mlp_up_fn(acc, b):
        hp = acc + b
        rl = jnp.maximum(hp, 0.0)
        return rl * rl, hp

    hact, hpre = _mm(x1_bf, w_up, M=T, N=F, K=D, tm=1024, tn=1024, tk=D, extras=[(W["mlp_b_up"], "n")],
                     epilogue=mlp_up_fn, n_out=2, out_dtypes=(BF16, BF16), name="fwd_mlp_up")
    w_down = comm.weight("mlp_w_down", hact)
    s2 = _mm(hact, w_down, M=T, N=D, K=F, tm=1024, tn=1024, tk=2048,
             extras=[(x1, "mn"), (W["mlp_b_down"], "n")], epilogue=lambda acc, xv, b: (ALPHA * xv + acc + b,),
             name="fwd_mlp_down")

    def ln2_fn(s, t, g, b):
        xhat, rstd = _ln_stats(s)
        err = xhat * g + b - t
        dy = err * (1.0 / D)
        ds = _ln_bwd(dy, g, xhat, rstd)
        return ds, ds, 0.5 * dy * err, dy * xhat, dy, ds

    ds2, ds2_bf, loss_cols, d_ln2_g, d_ln2_b, d_b_down = _ew(
        ln2_fn, [(s2, "tile", 0), (tgt, "tile", 0), (W["ln2_g"], "vec", 0), (W["ln2_b"], "vec", 0)],
        T=T, C=D, n_out=2, n_cs=4, out_dtypes=(F32, BF16), tm=256, name="bwd_loss_ln2")
    d_w_down = _mm(hact, ds2_bf, M=F, N=D, K=T, ta=True, out_dtypes=(BF16,), tm=1024, tn=1024, tk=4096, name="bwd_w_down")
    sent = comm.send_grad("mlp_w_down", d_w_down)

    def dhpre_fn(acc, hp):
        dv = acc * (2.0 * jnp.maximum(hp.astype(F32), 0.0))
        return dv, dv

    dhpre, d_b_up = _mm(ds2_bf, w_down, M=T, N=F, K=D, tb=True, tm=1024, tn=1024, tk=D, extras=[(hpre, "mn")],
                        epilogue=dhpre_fn, n_cs=1, out_dtypes=(BF16,), after=sent, name="bwd_mlp_down")
    d_w_up = _mm(x1_bf, dhpre, M=D, N=F, K=T, ta=True, out_dtypes=(BF16,), n_split=N_DEV, tm=1024, tn=F // N_DEV, tk=4096, name="bwd_w_up")
    sent = comm.send_grad("mlp_w_up", d_w_up)
    dx1 = _mm(dhpre, w_up, M=T, N=D, K=F, tb=True, tm=1024, tn=1024, tk=2048,
              extras=[(ds2, "mn")], epilogue=lambda acc, dv: (ALPHA * dv + acc,), after=sent, name="bwd_mlp_up")

    def ln1_bwd_fn(s, dy, g):
        xhat, rstd = _ln_stats(s)
        ds = _ln_bwd(dy, g, xhat, rstd)
        return ds, ds, dy * xhat, dy

    ds1, ds1_bf, d_ln1_g, d_ln1_b = _ew(ln1_bwd_fn, [(s1, "tile", 0), (dx1, "tile", 0), (W["ln1_g"], "vec", 0)],
                                        T=T, C=D, n_out=2, n_cs=2, out_dtypes=(F32, BF16), tm=256, name="bwd_ln1")
    d_w_out = _mm(mix, ds1_bf, M=D, N=D, K=T, ta=True, out_dtypes=(BF16,), tm=1024, tn=1024, tk=4096, name="bwd_w_out")
    sent = comm.send_grad("w_out", d_w_out)
    def mix_bwd_fn(dm, ga, gb, ya, a, b):
        ya, a, b = ya.astype(F32), a.astype(F32), b.astype(F32)
        sa, sb, sv = _sig(ga), _sig(gb), _sig(b)
        yb = a * sv
        dyb = dm * sb
        return (dm * ya * (sa * (1.0 - sa)), dm * yb * (sb * (1.0 - sb)), dm * sa, dyb * sv,
                dyb * a * (sv * (1.0 - sv)))

    dz = lax.empty((T, n_in), BF16)
    dz, dg_b, dy_a, dglu_a, dglu_b = _mm(
        ds1_bf, w_out, M=T, N=D, K=D, tb=True, tm=512, tn=cwm, tk=D,
        extras=[(z, "mn", ga_off // cwm), (z, "mn", gb_off // cwm), (y_a, "mn"), (glu_a, "mn"), (glu_b, "mn")],
        epilogue=mix_bwd_fn, n_out=5, out_dtypes=(BF16,) * 5, after=sent, into=(dz, 0, ga_off // cwm),
        name="bwd_out_proj_mix")
    dz = lax.dynamic_update_slice(dz, dg_b, (0, gb_off))

    d_w_a_out = _mm(p, dy_a, M=C, N=D, K=T, ta=True, out_dtypes=(BF16,), tm=1024, tn=1024, tk=4096, name="bwd_w_a_out")
    sent = comm.send_grad("w_a_out", d_w_a_out)
    def dp_fn(dp, hv, gate):
        th = jnp.tanh(GELU_C * (gate + GELU_K * gate * gate * gate))
        gelu = 0.5 * gate * (1.0 + th)
        dgelu = 0.5 * (1.0 + th) + 0.5 * gate * (1.0 - th * th) * (GELU_C * (1.0 + 3.0 * GELU_K * gate * gate))
        return dp * gelu, dp * hv * dgelu

    dh, dz = _mm(dy_a, w_a_out, M=T, N=C, K=D, tb=True, tm=256, tn=C, tk=D, extras=[(h, "mn"), (z, "mn", 1)],
                 epilogue=dp_fn, n_out=2, out_dtypes=(F32, BF16), after=sent, into=(dz, 1, 1), name="bwd_rg_out")
    drai, dxc0, d_b_ri, d_sp = _rg_scan_bwd(dh, h, ri, xc, a_fwd, m_fwd, sp, T=T, C=C, cw=kr, name="bwd_rg_scan")
    dxc = _bd([(drai, 0, w_ri)], T=T, J=jr, kb=2 * kr, nb=kr, tw=True, extras=[(dxc0, "tile", 0)],
              epilogue=lambda acc, d0: (acc + d0,), name="bwd_gates")
    d_w_ri = _bdw(xc, 0, drai, 0, T=T, J=jr, kb=kr, nb=2 * kr, name="bwd_w_gates")
    d_wa, d_wx = _bd_unpack(d_w_ri[:, :, :kr], RG_Q), _bd_unpack(d_w_ri[:, :, kr:], RG_Q)
    d_b_ri = d_b_ri.reshape(jr, 2 * kr)
    d_ba, d_bx = d_b_ri[:, :kr].reshape(1, -1), d_b_ri[:, kr:].reshape(1, -1)
    dz, conv_sums = _conv_bwd(dxc, z, conv_w, dz, T=T, C=C, name="bwd_conv")
    d_conv_w, d_conv_b = conv_sums[0:4], conv_sums[4:5]
    (d_lambda,) = sp_vjp(d_sp)

    d_glu_w = _mm(yg, dglu_a, M=S, N=D, K=T, ta=True, out_dtypes=(BF16,), n_split=N_DEV, tm=1024, tn=D // N_DEV, tk=4096, name="bwd_w_glu_w")
    d_glu_v = _mm(yg, dglu_b, M=S, N=D, K=T, ta=True, out_dtypes=(BF16,), n_split=N_DEV, tm=1024, tn=D // N_DEV, tk=4096, name="bwd_w_glu_v")
    sent = comm.send_grad("glu_w", d_glu_w, "glu_v", d_glu_v)
    dyg0 = _mm(dglu_a, w_glu_w, M=T, N=S, K=D, tb=True, tm=512, tn=S, tk=D, after=sent, name="bwd_glu_w")
    dy_s = _mm(dglu_b, w_glu_v, M=T, N=S, K=D, tb=True, tm=512, tn=S, tk=D,
               extras=[(dyg0, "mn"), (y_s, "mn")], epilogue=lambda acc, d0, yv: ((acc + d0) * _dgelu(yv),),
               name="bwd_glu_v")
    dz, d_lbr, d_lbi, d_ssm_d, d_wb_re, d_wb_im, d_wc_re, d_wc_im_neg = _s5_bwd(
        dy_s, z, u_off, h_re, h_im, wb_re, wb_im, wc_re, wc_im_neg, d_row, powers, dz, T=T, name="bwd_s5")
    d_bbr = jnp.swapaxes(_bd_unpack(d_wb_re, S5_Q), 1, 2)
    d_bbi = jnp.swapaxes(_bd_unpack(d_wb_im, S5_Q), 1, 2)
    d_a_re, d_a_im, d_log_dt, d_b_re, d_b_im = s5_vjp((d_lbr.reshape(G, P), d_lbi.reshape(G, P), d_bbr, d_bbi))
    d_c_re = _bd_unpack(d_wc_re, S5_Q)
    d_c_im = -_bd_unpack(d_wc_im_neg, S5_Q)

    grads = dict(
        conv_w=d_conv_w, conv_b=d_conv_b, rg_wa=d_wa, rg_ba=d_ba, rg_wx=d_wx, rg_bx=d_bx,
        rg_lambda=d_lambda, ssm_a_re=d_a_re, ssm_a_im=d_a_im, ssm_log_dt=d_log_dt,
        ssm_b_re=d_b_re, ssm_b_im=d_b_im, ssm_c_re=d_c_re, ssm_c_im=d_c_im, ssm_d=d_ssm_d.reshape(G, H),
        ln1_g=d_ln1_g, ln1_b=d_ln1_b, mlp_b_up=d_b_up, mlp_b_down=d_b_down, ln2_g=d_ln2_g, ln2_b=d_ln2_b)
    sent = comm.send_small(grads)

    d_w_in = _mm(x_bf, dz, M=D, N=n_in, K=T, ta=True, out_dtypes=(BF16,), n_split=N_DEV, tm=1024, tn=n_in // N_DEV,
                 tk=4096, after=sent, name="bwd_w_in")
    sent = comm.send_grad("w_in", d_w_in)
    grad_x = _mm(dz, w_in, M=T, N=D, K=n_in, tb=True, tm=1024, tn=1024, tk=n_in // 4,
                 extras=[(ds1, "mn")], epilogue=lambda acc, dv: (ALPHA * dv + acc,), after=sent, name="bwd_in_proj")
    return jnp.sum(loss_cols), grad_x, grads


BIG = ("w_in", "w_a_out", "glu_w", "glu_v", "w_out", "mlp_w_up", "mlp_w_down")
COL_SHARDED = ("w_in", "glu_w", "glu_v", "mlp_w_up")
SMALL = ("conv_w", "conv_b", "rg_wa", "rg_ba", "rg_wx", "rg_bx", "rg_lambda", "ssm_a_re", "ssm_a_im", "ssm_log_dt",
         "ssm_b_re", "ssm_b_im", "ssm_c_re", "ssm_c_im", "ssm_d", "ln1_g", "ln1_b", "mlp_b_up", "mlp_b_down", "ln2_g",
         "ln2_b")
ORDER = ("w_in", "conv_w", "conv_b", "rg_wa", "rg_ba", "rg_wx", "rg_bx", "rg_lambda", "w_a_out", "ssm_a_re",
         "ssm_a_im", "ssm_log_dt", "ssm_b_re", "ssm_b_im", "ssm_c_re", "ssm_c_im", "ssm_d", "glu_w", "glu_v", "w_out",
         "ln1_g", "ln1_b", "mlp_w_up", "mlp_b_up", "mlp_w_down", "mlp_b_down", "ln2_g", "ln2_b")
TILE_ELEMS = SUBLANES * LANES


def _pack(arrs):
    pieces = []
    for a in arrs:
        flat = a.reshape(-1)
        flat = jnp.pad(flat, (0, (-flat.shape[0]) % TILE_ELEMS))
        pieces.append(flat.reshape(-1, LANES))
    rows = sum(p.shape[0] for p in pieces)
    pad_rows = (-rows) % (N_DEV * SUBLANES)
    if pad_rows:
        pieces.append(jnp.zeros((pad_rows, LANES), pieces[0].dtype))
    return jnp.concatenate(pieces, axis=0)


def _unpack(packed, shapes):
    out, row = [], 0
    for shp in shapes:
        n = math.prod(shp)
        rows = -(-n // TILE_ELEMS) * SUBLANES
        out.append(packed[row:row + rows].reshape(-1)[:n].reshape(shp))
        row += rows
    return out


class _Comm:
    def __init__(self, w):
        first = [w["w_in"].astype(BF16), w["conv_w"]]
        self._first, self.first_token = _exchange_start(first, [_landing_zone(s) for s in first], mode="own",
                                                        name="gather_in_start")
        self._shards = {k: w[k].astype(BF16) for k in BIG if k != "w_in"}
        self._weights, self._gathers, self._grads = {}, {}, {}

    def first_weights(self, after):
        lands = [_exchange_wait(h, after, mode="own", name="gather_in_wait_%d" % i) for i, h in enumerate(self._first)]
        unused = [lax.empty((2 * SUBLANES, LANES), BF16) for _ in lands]
        handles, passed = _exchange_start(unused, lands, mode="pass", name="gather_in_pass")
        w_in, taps = [_exchange_wait(h, passed, mode="pass", name="gather_in_got_%d" % i) for i, h in enumerate(handles)]
        self._weights["w_in"] = w_in
        self.gather_token = self.start_weights(("w_a_out", "glu_w", "glu_v", "w_out"), w_in)
        return self.weight("w_in", None), jnp.swapaxes(taps, 0, 1).reshape(taps.shape[1], -1)

    def start_weights(self, names, after):
        shards = [self._shards.pop(k) for k in names]
        handles, token = _exchange_start(shards, [_landing_zone(s) for s in shards], mode="gather", after=after,
                                         name="gather_start_" + names[0])
        self._gathers.update(zip(names, handles))
        return token

    def weight(self, k, after):
        if k not in self._weights:
            self._weights[k] = _exchange_wait(self._gathers.pop(k), after, mode="gather", name="gather_wait_" + k)
        gk = self._weights[k]
        if k in COL_SHARDED:
            return jnp.swapaxes(gk, 0, 1).reshape(gk.shape[1], -1)
        return gk.reshape(-1, gk.shape[-1])

    def send_grad(self, *names_and_parts):
        names, parts = names_and_parts[0::2], names_and_parts[1::2]
        parts = [p if k in COL_SHARDED else p.reshape(N_DEV, p.shape[0] // N_DEV, p.shape[1])
                 for k, p in zip(names, parts)]
        me = _dev_index(*_mesh_pos())
        lands = [_landing_zone(lax.dynamic_index_in_dim(p, me, 0, keepdims=False)) for p in parts]
        handles, token = _exchange_start(parts, lands, mode="scatter", name="grad_start_" + names[0])
        self._grads.update(zip(names, handles))
        return token

    def received_grad(self, k, after):
        return _exchange_wait(self._grads.pop(k), after, mode="scatter", name="grad_wait_" + k)

    def send_small(self, grads):
        return self.send_grad("small", _pack([grads[k] for k in SMALL]))

    def all_reduced_small(self, after, behind):
        recv = self.received_grad("small", after)
        block = _sum_slots(recv, tr=512, name="sum_small_grads")
        (handle,), started = _exchange_start([block], [_landing_zone(block)], mode="gather", name="small_sum_start")
        done = behind(started)
        return _exchange_wait(handle, done, mode="gather", name="small_sum_wait").reshape(-1, LANES)


SMALL_GROUPS = (("rg_wa", "rg_wx"), ("ssm_b_re",), ("ssm_b_im",),
                tuple(k for k in SMALL if k not in ("rg_wa", "rg_wx", "ssm_b_re", "ssm_b_im")))


def _step(x, tgt, w, m, v, raw_w, raw_m, raw_v):
    dev = _dev_index(*_mesh_pos())

    comm = _Comm(w)
    small = dict(w)
    for k in ("conv_b", "rg_ba", "rg_bx", "rg_lambda", "ln1_g", "ln1_b", "mlp_b_up", "mlp_b_down", "ln2_g", "ln2_b"):
        small[k] = w[k].reshape(1, -1)

    loss_part, grad_x, grads = _local_step(x, tgt, small, comm)

    out_g, out_d, out_m, out_v = {}, {}, {}, {}

    def update_large(started):
        for k in BIG:
            rk = comm.received_grad(k, (grad_x, started))
            out_g[k], out_d[k], out_m[k], out_v[k] = _adamw(rk, w[k], m[k], v[k], tr=256 if k in COL_SHARDED else 128,
                                                                 name="adamw_" + k)
        return out_v[BIG[-1]]

    small_all = comm.all_reduced_small(grad_x, update_large)
    g_small = dict(zip(SMALL, _unpack(small_all, [grads[k].shape for k in SMALL])))
    cw_cols = w["conv_w"].shape[1]
    g_small["conv_w"] = lax.dynamic_slice_in_dim(g_small["conv_w"], dev * cw_cols, cw_cols, axis=1)
    for group in SMALL_GROUPS:
        gs = [g_small[k].reshape(raw_w[k].shape) for k in group]
        res = _adamw_whole(gs, [raw_w[k] for k in group], [raw_m[k] for k in group], [raw_v[k] for k in group],
                           name="adamw_" + group[0])
        for k, gk, (dk, mk, vk) in zip(group, gs, res):
            out_g[k], out_d[k], out_m[k], out_v[k] = gk, dk, mk, vk

    loss = lax.psum(loss_part, ("x", "y", "c"))
    return loss, grad_x, out_g, out_d, out_m, out_v


def kernel(x, w_in, conv_w, conv_b, rg_wa, rg_ba, rg_wx, rg_bx, rg_lambda, w_a_out, ssm_a_re, ssm_a_im, ssm_log_dt, ssm_b_re, ssm_b_im, ssm_c_re, ssm_c_im, ssm_d, glu_w, glu_v, w_out, ln1_g, ln1_b, mlp_w_up, mlp_b_up, mlp_w_down, mlp_b_down, ln2_g, ln2_b, loss_target, m_w_in, m_conv_w, m_conv_b, m_rg_wa, m_rg_ba, m_rg_wx, m_rg_bx, m_rg_lambda, m_w_a_out, m_ssm_a_re, m_ssm_a_im, m_ssm_log_dt, m_ssm_b_re, m_ssm_b_im, m_ssm_c_re, m_ssm_c_im, m_ssm_d, m_glu_w, m_glu_v, m_w_out, m_ln1_g, m_ln1_b, m_mlp_w_up, m_mlp_b_up, m_mlp_w_down, m_mlp_b_down, m_ln2_g, m_ln2_b, v_w_in, v_conv_w, v_conv_b, v_rg_wa, v_rg_ba, v_rg_wx, v_rg_bx, v_rg_lambda, v_w_a_out, v_ssm_a_re, v_ssm_a_im, v_ssm_log_dt, v_ssm_b_re, v_ssm_b_im, v_ssm_c_re, v_ssm_c_im, v_ssm_d, v_glu_w, v_glu_v, v_w_out, v_ln1_g, v_ln1_b, v_mlp_w_up, v_mlp_b_up, v_mlp_w_down, v_mlp_b_down, v_ln2_g, v_ln2_b):
    args = locals()
    w = {k: args[k][0] for k in ORDER}
    m = {k: args["m_" + k][0] for k in BIG}
    v = {k: args["v_" + k][0] for k in BIG}
    raw = [{k: args[prefix + k] for k in SMALL} for prefix in ("", "m_", "v_")]
    loss, grad_x, out_g, out_d, out_m, out_v = _step(x[0], loss_target[0], w, m, v, *raw)
    outs = [loss, grad_x[None]]
    for group in (out_g, out_d, out_m, out_v):
        outs += [group[k].reshape(args[k].shape) for k in ORDER]
    return tuple(outs)
```

```python
import functools
import math

import jax
import jax.numpy as jnp
from jax import lax
from jax.experimental import pallas as pl
from jax.experimental.pallas import tpu as pltpu

F32 = jnp.float32
BF16 = jnp.bfloat16
MESH = pl.DeviceIdType.MESH
N_DEV = 8
SUBLANES = 8
LANES = 128
VMEM_BYTES_V7X = 64 * 2 ** 20
VMEM_CAP = VMEM_BYTES_V7X - 8 * 2 ** 20

ALPHA = 2.0 ** 0.25
LN_EPS = 1e-5
RG_C = 8.0
ADAM_LR, ADAM_B1, ADAM_B2, ADAM_EPS, ADAM_WD, ADAM_STEP = 0.001, 0.9, 0.999, 1e-08, 0.01, 10
GELU_C = math.sqrt(2.0 / math.pi)
GELU_K = 0.044715

ANY = pl.BlockSpec(memory_space=pl.ANY)


def _params(sem, vmem_bytes):
    limit = int(min(max(2 * vmem_bytes, 16 * 2 ** 20), VMEM_CAP))
    return pltpu.CompilerParams(dimension_semantics=sem, vmem_limit_bytes=limit)


def _sig(x):
    return 1.0 / (1.0 + jnp.exp(-x))


def _gelu(x):
    return 0.5 * x * (1.0 + jnp.tanh(GELU_C * (x + GELU_K * x * x * x)))


def _dgelu(x):
    th = jnp.tanh(GELU_C * (x + GELU_K * x * x * x))
    return 0.5 * (1.0 + th) + 0.5 * x * (1.0 - th * th) * (GELU_C * (1.0 + 3.0 * GELU_K * x * x))


def _one_minus_exp(x, exp_half_x):
    p = x * (1.0 + x * (1 / 2 + x * (1 / 6 + x * (1 / 24 + x * (1 / 120)))))
    return jnp.where(x > -1 / 16, -p, 1.0 - exp_half_x * exp_half_x)


def _accumulate(ref, val, first):
    @pl.when(first)
    def _():
        ref[...] = val

    @pl.when(jnp.logical_not(first))
    def _():
        ref[...] += val


def _rows8(cw):
    return lax.broadcasted_iota(jnp.int32, (SUBLANES, cw), 0)


def _shift_down(cur, prev, s, rows):
    return jnp.where(rows < s, pltpu.roll(prev, s, 0), pltpu.roll(cur, s, 0))


def _shift_up(cur, nxt, s, rows):
    return jnp.where(rows < SUBLANES - s, pltpu.roll(cur, SUBLANES - s, 0), pltpu.roll(nxt, SUBLANES - s, 0))


def _mm(a, b, *, M, N, K, ta=False, tb=False, b_split=1, n_split=1, a_fn=None, extras=(), epilogue=None,
        n_out=1, n_cs=0, out_dtypes=None, tm=512, tn=512, tk=512, after=None, into=None, name):
    tm, tn, tk = min(tm, M), min(tn, N), min(tk, K)
    assert M % tm == 0 and N % tn == 0 and K % tk == 0, (name, M, N, K, tm, tn, tk)
    nk = K // tk
    grid = (N // tn, M // tm, nk)
    a_spec = pl.BlockSpec((tk, tm), lambda j, i, k: (k, i)) if ta else pl.BlockSpec((tm, tk), lambda j, i, k: (i, k))
    if b_split == 1:
        b_spec = pl.BlockSpec((tn, tk), lambda j, i, k: (j, k)) if tb else pl.BlockSpec((tk, tn), lambda j, i, k: (k, j))
    elif tb:
        kb = (K // b_split) // tk
        assert kb * tk * b_split == K, name
        b_spec = pl.BlockSpec((None, tn, tk), lambda j, i, k: (k // kb, j, k % kb))
    else:
        nb = (N // b_split) // tn
        assert nb * tn * b_split == N, name
        b_spec = pl.BlockSpec((None, tk, tn), lambda j, i, k: (j // nb, k, j % nb))
    in_specs = [a_spec, b_spec]
    for arr, kind, *col_off in extras:
        off = col_off[0] if col_off else 0
        in_specs.append(pl.BlockSpec((tm, tn), lambda j, i, k, off=off: (i, off + j)) if kind == "mn"
                        else pl.BlockSpec((1, tn), lambda j, i, k: (0, j)))
    out_dtypes = (F32,) * n_out if out_dtypes is None else out_dtypes
    if n_split == 1:
        out_shape = [jax.ShapeDtypeStruct((M, N), dt) for dt in out_dtypes]
        out_specs = [pl.BlockSpec((tm, tn), lambda j, i, k: (i, j)) for _ in range(n_out)]
    else:
        assert n_out == 1
        nbo = (N // n_split) // tn
        assert nbo * tn * n_split == N, name
        out_shape = [jax.ShapeDtypeStruct((n_split, M, N // n_split), out_dtypes[0])]
        out_specs = [pl.BlockSpec((None, tm, tn), lambda j, i, k: (j // nbo, i, j % nbo))]
    out_shape += [jax.ShapeDtypeStruct((1, N), F32) for _ in range(n_cs)]
    out_specs += [pl.BlockSpec((1, tn), lambda j, i, k: (0, j)) for _ in range(n_cs)]
    ne = len(extras)
    dims = (((0 if ta else 1,), (1 if tb else 0,)), ((), ()))

    n_after = 0 if after is None else 1
    in_specs += [ANY] * n_after
    aliases, tail = {}, [] if after is None else [after]
    if into is not None:
        buf, which, col_blk = into
        assert n_split == 1 and buf.shape[0] == M and buf.dtype == out_dtypes[which], name
        aliases = {len(in_specs): which}
        in_specs.append(ANY)
        tail.append(buf)
        out_shape[which] = jax.ShapeDtypeStruct(buf.shape, buf.dtype)
        out_specs[which] = pl.BlockSpec((tm, tn), lambda j, i, k: (i, col_blk + j))

    def body(*refs):
        a_ref, b_ref = refs[0], refs[1]
        ex_refs = refs[2:2 + ne]
        first_out = 2 + ne + len(tail)
        out_refs = refs[first_out:first_out + n_out]
        cs_refs = refs[first_out + n_out:first_out + n_out + n_cs]
        i, k = pl.program_id(1), pl.program_id(2)

        def product():
            av = a_ref[...]
            if a_fn is not None:
                av = a_fn(av.astype(F32))
            return lax.dot_general(av.astype(BF16), b_ref[...].astype(BF16), dims, preferred_element_type=F32)

        def finish(acc):
            res = (acc,) if epilogue is None else epilogue(acc, *[r[...] for r in ex_refs])
            for r, o in zip(out_refs, res[:n_out]):
                r[...] = o.astype(r.dtype)
            for r, cval in zip(cs_refs, res[n_out:]):
                _accumulate(r, jnp.sum(cval, axis=0, keepdims=True), i == 0)

        if nk == 1:
            finish(product())
            return
        acc_ref = refs[-1]

        @pl.when(k == 0)
        def _():
            acc_ref[...] = jnp.zeros_like(acc_ref)

        acc_ref[...] += product()

        @pl.when(k == nk - 1)
        def _():
            finish(acc_ref[...])

    vmem = 2 * tm * tk * a.dtype.itemsize + 2 * tk * tn * b.dtype.itemsize + (1 + 2 * n_out + 2 * ne + 2) * tm * tn * 4
    outs = pl.pallas_call(
        body, name=name, grid=grid, in_specs=in_specs, out_specs=out_specs, out_shape=out_shape,
        scratch_shapes=[pltpu.VMEM((tm, tn), F32)] if nk > 1 else [], input_output_aliases=aliases,
        compiler_params=_params(("parallel", "arbitrary", "arbitrary"), vmem),
    )(a, b, *[e[0] for e in extras], *tail)
    return outs[0] if len(outs) == 1 else outs


BD_STEP = 4

def _bd(pairs, *, T, J, kb, nb, tw=False, extras=(), epilogue=None, n_out=1, n_cs=0, out_dtypes=None, tm=1024, name):
    jb = BD_STEP
    assert T % tm == 0 and J % jb == 0
    grid = (J // jb, T // tm)
    npair, ne = len(pairs), len(extras)
    in_specs, args = [], []
    for arr, off, w in pairs:
        assert off % jb == 0, name
        in_specs.append(pl.BlockSpec((tm, jb * kb), lambda j, i, off=off // jb: (i, off + j)))
        in_specs.append(pl.BlockSpec((jb,) + tuple(w.shape[1:]), lambda j, i: (j, 0, 0)))
        args += [arr, w]
    for arr, kind, off in extras:
        assert off % jb == 0, name
        in_specs.append(pl.BlockSpec((tm, jb * nb), lambda j, i, off=off // jb: (i, off + j)) if kind == "tile"
                        else pl.BlockSpec((1, jb * nb), lambda j, i, off=off // jb: (0, off + j)))
        args.append(arr)
    out_dtypes = (F32,) * n_out if out_dtypes is None else out_dtypes
    out_shape = [jax.ShapeDtypeStruct((T, J * nb), dt) for dt in out_dtypes]
    out_specs = [pl.BlockSpec((tm, jb * nb), lambda j, i: (i, j)) for _ in range(n_out)]
    out_shape += [jax.ShapeDtypeStruct((1, J * nb), F32) for _ in range(n_cs)]
    out_specs += [pl.BlockSpec((1, jb * nb), lambda j, i: (0, j)) for _ in range(n_cs)]
    dims = (((1,), (1 if tw else 0,)), ((), ()))

    def body(*refs):
        ex_refs = refs[2 * npair:2 * npair + ne]
        out_refs = refs[2 * npair + ne:2 * npair + ne + n_out]
        cs_refs = refs[2 * npair + ne + n_out:]
        i = pl.program_id(1)
        for s in range(jb):
            cols_in, cols_out = pl.ds(s * kb, kb), pl.ds(s * nb, nb)
            acc = None
            for p in range(npair):
                d = lax.dot_general(refs[2 * p][:, cols_in].astype(BF16), refs[2 * p + 1][s].astype(BF16), dims,
                                    preferred_element_type=F32)
                acc = d if acc is None else acc + d
            res = (acc,) if epilogue is None else epilogue(acc, *[r[:, cols_out] for r in ex_refs])
            for r, o in zip(out_refs, res[:n_out]):
                r[:, cols_out] = o.astype(r.dtype)
            for r, cval in zip(cs_refs, res[n_out:]):
                _accumulate(r.at[:, cols_out], jnp.sum(cval, axis=0, keepdims=True), i == 0)

    vmem = jb * (2 * npair * tm * kb + 2 * npair * kb * nb + (2 * n_out + 2 * ne + 3) * tm * nb) * 4
    outs = pl.pallas_call(
        body, name=name, grid=grid, in_specs=in_specs, out_specs=out_specs, out_shape=out_shape,
        compiler_params=_params(("parallel", "arbitrary"), vmem),
    )(*args)
    return outs[0] if len(outs) == 1 else outs


def _bdw(a, a_off, b, b_off, *, T, J, kb, nb, tm=1024, name):
    jb = BD_STEP
    assert T % tm == 0 and J % jb == 0 and a_off % jb == 0 and b_off % jb == 0
    a_blk, b_blk = a_off // jb, b_off // jb

    def body(a_ref, b_ref, o_ref):
        i = pl.program_id(1)
        for s in range(jb):
            d = lax.dot_general(a_ref[:, pl.ds(s * kb, kb)].astype(BF16), b_ref[:, pl.ds(s * nb, nb)].astype(BF16),
                                (((0,), (0,)), ((), ())), preferred_element_type=F32)
            _accumulate(o_ref.at[s], d, i == 0)

    return pl.pallas_call(
        body, name=name, grid=(J // jb, T // tm),
        in_specs=[pl.BlockSpec((tm, jb * kb), lambda j, i: (i, a_blk + j)),
                  pl.BlockSpec((tm, jb * nb), lambda j, i: (i, b_blk + j))],
        out_specs=pl.BlockSpec((jb, kb, nb), lambda j, i: (j, 0, 0)),
        out_shape=jax.ShapeDtypeStruct((J, kb, nb), F32),
        compiler_params=_params(("parallel", "arbitrary"), jb * (2 * tm * (kb + nb) + 3 * kb * nb) * 4),
    )(a, b)


def _bd_pack(w, q):
    g, a, b = w.shape
    eye = jnp.eye(q, dtype=w.dtype)
    return jnp.einsum("jqab,qr->jqarb", w.reshape(g // q, q, a, b), eye).reshape(g // q, q * a, q * b)


def _bd_unpack(wp, q):
    j, qa, qb = wp.shape
    a, b = qa // q, qb // q
    w5 = wp.reshape(j, q, a, q, b)
    return jnp.stack([w5[:, r, :, r, :] for r in range(q)], axis=1).reshape(j * q, a, b)


def _ew(fn, ins, *, T, C, n_out, n_cs=0, out_dtypes=None, tm=256, cw=None, name):
    cw = C if cw is None else cw
    assert T % tm == 0 and C % cw == 0
    grid = (C // cw, T // tm)
    in_specs = []
    for arr, kind, off in ins:
        in_specs.append(pl.BlockSpec((tm, cw), lambda j, i, off=off: (i, off + j)) if kind == "tile"
                        else pl.BlockSpec((arr.shape[0], cw), lambda j, i, off=off: (0, off + j)))
    out_dtypes = (F32,) * n_out if out_dtypes is None else out_dtypes
    out_shape = [jax.ShapeDtypeStruct((T, C), dt) for dt in out_dtypes]
    out_specs = [pl.BlockSpec((tm, cw), lambda j, i: (i, j)) for _ in range(n_out)]
    out_shape += [jax.ShapeDtypeStruct((1, C), F32) for _ in range(n_cs)]
    out_specs += [pl.BlockSpec((1, cw), lambda j, i: (0, j)) for _ in range(n_cs)]
    nin = len(ins)

    def body(*refs):
        i = pl.program_id(1)
        res = fn(*[r[...].astype(F32) for r in refs[:nin]])
        for r, o in zip(refs[nin:nin + n_out], res[:n_out]):
            r[...] = o.astype(r.dtype)
        for r, cval in zip(refs[nin + n_out:], res[n_out:]):
            _accumulate(r, jnp.sum(cval, axis=0, keepdims=True), i == 0)

    vmem = (2 * nin + 2 * n_out + 6) * tm * cw * 4
    outs = pl.pallas_call(
        body, name=name, grid=grid, in_specs=in_specs, out_specs=out_specs, out_shape=out_shape,
        compiler_params=_params(("parallel", "arbitrary"), vmem),
    )(*[arr for arr, _, _ in ins])
    return outs[0] if len(outs) == 1 else outs


def _ln_stats(s):
    mu = jnp.mean(s, axis=-1, keepdims=True)
    d = s - mu
    var = jnp.mean(d * d, axis=-1, keepdims=True)
    rstd = lax.rsqrt(var + LN_EPS)
    return d * rstd, rstd


def _ln_bwd(dy, g, xhat, rstd):
    dxh = dy * g
    m1 = jnp.mean(dxh, axis=-1, keepdims=True)
    m2 = jnp.mean(dxh * xhat, axis=-1, keepdims=True)
    return rstd * (dxh - m1 - xhat * m2)


def _conv_fwd(z, conv_w, conv_b, *, T, C, tm=1024, cw=1024, after=None, name):
    ng, hb = tm // SUBLANES, tm // SUBLANES
    n_after = 0 if after is None else 1

    def body(x_ref, halo_ref, w_ref, b_ref, *rest):
        o_ref = rest[-1]
        it = pl.program_id(1)
        rows = _rows8(cw)
        halo = jnp.where(it == 0, 0.0, halo_ref[...])
        w = w_ref[...]
        bias = b_ref[...]

        def group(g, carry):
            off = pl.multiple_of(g * SUBLANES, SUBLANES)
            cur = x_ref[pl.ds(off, SUBLANES), :]
            prev = x_ref[pl.ds(pl.multiple_of(jnp.maximum(off - SUBLANES, 0), SUBLANES), SUBLANES), :]
            prev = jnp.where(g == 0, halo, prev)
            acc = cur * w[3:4] + bias
            for s in (1, 2, 3):
                acc = acc + _shift_down(cur, prev, s, rows) * w[3 - s:4 - s]
            o_ref[pl.ds(off, SUBLANES), :] = acc
            return carry

        lax.fori_loop(0, ng, group, 0, unroll=2)

    return pl.pallas_call(
        body, name=name, grid=(C // cw, T // tm),
        in_specs=[pl.BlockSpec((tm, cw), lambda j, i: (i, j)),
                  pl.BlockSpec((SUBLANES, cw), lambda j, i: (jnp.maximum(i * hb - 1, 0), j)),
                  pl.BlockSpec((4, cw), lambda j, i: (0, j)), pl.BlockSpec((1, cw), lambda j, i: (0, j))]
        + [ANY] * n_after,
        out_specs=pl.BlockSpec((tm, cw), lambda j, i: (i, j)),
        out_shape=jax.ShapeDtypeStruct((T, C), F32),
        compiler_params=_params(("parallel", "arbitrary"), 5 * tm * cw * 4),
    )(z, z, conv_w, conv_b, *([after] if n_after else []))


def _conv_bwd(dxc, z, conv_w, dz, *, T, C, tm=1024, cw=512, name):
    ng, hb, last = tm // SUBLANES, tm // SUBLANES, T // SUBLANES - 1
    nt = T // tm
    rows16 = 2 * SUBLANES

    def body(d_ref, dn_ref, x_ref, w_ref, dz_in_ref, o_ref, sums_ref):
        it = pl.program_id(1)
        rows = _rows8(cw)
        dnext = jnp.where(it == nt - 1, 0.0, dn_ref[...])
        w = w_ref[...]

        def pair(q, accs):
            base = pl.multiple_of(q * rows16, rows16)
            halves = []
            for half in range(2):
                g = 2 * q + half
                off = pl.multiple_of(base + half * SUBLANES, SUBLANES)
                dcur = d_ref[pl.ds(off, SUBLANES), :]
                dnx = d_ref[pl.ds(pl.multiple_of(jnp.minimum(off + SUBLANES, tm - SUBLANES), SUBLANES), SUBLANES), :]
                dnx = jnp.where(g == ng - 1, dnext, dnx)
                xcur = x_ref[pl.ds(off, SUBLANES), :]
                acc = dcur * w[3:4]
                taps = [accs[3] + dcur * xcur]
                for s in (1, 2, 3):
                    ahead = _shift_up(dcur, dnx, s, rows)
                    acc = acc + ahead * w[3 - s:4 - s]
                    taps.append(accs[3 - s] + ahead * xcur)
                halves.append(acc)
                accs = (taps[3], taps[2], taps[1], taps[0], accs[4] + dcur)
            o_ref[pl.ds(base, rows16), :] = jnp.concatenate(halves, axis=0).astype(o_ref.dtype)
            return accs

        zero = jnp.zeros((SUBLANES, cw), F32)
        accs = lax.fori_loop(0, ng // 2, pair, (zero,) * 5)
        sums = jnp.zeros((SUBLANES, cw), F32)
        for k, a in enumerate(accs):
            sums = jnp.where(rows == k, jnp.sum(a, axis=0, keepdims=True), sums)
        _accumulate(sums_ref, sums, it == 0)

    tile = pl.BlockSpec((tm, cw), lambda j, i: (i, j))
    return pl.pallas_call(
        body, name=name, grid=(C // cw, nt),
        in_specs=[tile, pl.BlockSpec((SUBLANES, cw), lambda j, i: (jnp.minimum((i + 1) * hb, last), j)),
                  tile, pl.BlockSpec((4, cw), lambda j, i: (0, j)), ANY],
        out_specs=[tile, pl.BlockSpec((SUBLANES, cw), lambda j, i: (0, j))],
        input_output_aliases={4: 0},
        out_shape=[jax.ShapeDtypeStruct(dz.shape, dz.dtype), jax.ShapeDtypeStruct((SUBLANES, C), F32)],
        compiler_params=_params(("parallel", "arbitrary"), 7 * tm * cw * 4),
    )(dxc, dxc, z, conv_w, dz)


def _rg_coeffs(r, ig, xc, sp):
    la = (-RG_C) * r * sp
    a = jnp.exp(la)
    m = jnp.sqrt(_one_minus_exp(2.0 * la, a))
    return a, m, m * (ig * xc)


def _rg_scan_fwd(z, ri, xc, sp, *, T, C, gate_off, tm=2048, cw=256, name):
    rows16 = 2 * SUBLANES
    nq = tm // rows16

    def body(gate_ref, r_ref, i_ref, xc_ref, sp_ref, h_ref, p_ref, a_ref, m_ref, carry_ref):
        it = pl.program_id(1)

        @pl.when(it == 0)
        def _():
            carry_ref[...] = jnp.zeros_like(carry_ref)

        rows = _rows8(cw)
        sp_row = sp_ref[...]

        def pair(q, carry):
            base = pl.multiple_of(q * rows16, rows16)
            halves = []
            for half in range(2):
                sl = pl.ds(pl.multiple_of(base + half * SUBLANES, SUBLANES), SUBLANES)
                a, m, b = _rg_coeffs(r_ref[sl, :], i_ref[sl, :], xc_ref[sl, :], sp_row)
                a_ref[sl, :] = a
                m_ref[sl, :] = m
                for s in (1, 2, 4):
                    keep = rows >= s
                    sa = jnp.where(keep, pltpu.roll(a, s, 0), 1.0)
                    sb = jnp.where(keep, pltpu.roll(b, s, 0), 0.0)
                    b = b + a * sb
                    a = a * sa
                h = b + a * carry
                h_ref[sl, :] = h
                halves.append(h * _gelu(gate_ref[sl, :]))
                carry = h[SUBLANES - 1:SUBLANES, :]
            p_ref[pl.ds(base, rows16), :] = jnp.concatenate(halves, axis=0).astype(p_ref.dtype)
            return carry

        last = lax.fori_loop(0, nq, pair, carry_ref[0:1, :], unroll=2)
        carry_ref[...] = jnp.broadcast_to(last, carry_ref.shape)

    tile = pl.BlockSpec((tm, cw), lambda j, i: (i, j))
    gate_blk = gate_off // cw
    return pl.pallas_call(
        body, name=name, grid=(C // cw, T // tm),
        in_specs=[pl.BlockSpec((tm, cw), lambda j, i: (i, gate_blk + j)),
                  pl.BlockSpec((tm, cw), lambda j, i: (i, 2 * j)), pl.BlockSpec((tm, cw), lambda j, i: (i, 2 * j + 1)),
                  tile, pl.BlockSpec((1, cw), lambda j, i: (0, j))],
        out_specs=[tile, tile, tile, tile],
        out_shape=[jax.ShapeDtypeStruct((T, C), F32), jax.ShapeDtypeStruct((T, C), BF16),
                   jax.ShapeDtypeStruct((T, C), F32), jax.ShapeDtypeStruct((T, C), F32)],
        scratch_shapes=[pltpu.VMEM((SUBLANES, cw), F32)],
        compiler_params=_params(("parallel", "arbitrary"), 16 * tm * cw * 4),
    )(z, ri, ri, xc, sp)


def _rg_scan_bwd(dh, h, ri, xc, a_fwd, m_fwd, sp, *, T, C, tm=1024, cw=256, name):
    ng, hb, nt = tm // SUBLANES, tm // SUBLANES, T // tm

    def body(dh_ref, h_ref, hp_ref, r_ref, i_ref, xc_ref, a_ref, m_ref, sp_ref,
             drai_ref, dxc_ref, crai_ref, csp_ref, cg_ref, ca_ref):
        step = pl.program_id(1)

        @pl.when(step == 0)
        def _():
            cg_ref[...] = jnp.zeros_like(cg_ref)
            ca_ref[...] = jnp.zeros_like(ca_ref)

        rows = _rows8(cw)
        sp_row = sp_ref[...]
        hhalo = jnp.where(step == nt - 1, 0.0, hp_ref[...])

        def group(gi, carry):
            g_next, a_next, s_ra, s_ia, s_sp = carry
            g = ng - 1 - gi
            off = pl.multiple_of(g * SUBLANES, SUBLANES)
            sl = pl.ds(off, SUBLANES)
            rr, ii, xx = r_ref[sl, :], i_ref[sl, :], xc_ref[sl, :]
            a, m = a_ref[sl, :], m_ref[sl, :]
            hh = h_ref[sl, :]
            hpv = h_ref[pl.ds(pl.multiple_of(jnp.maximum(off - SUBLANES, 0), SUBLANES), SUBLANES), :]
            hpv = jnp.where(g == 0, hhalo, hpv)
            hprev = _shift_down(hh, hpv, 1, rows)
            d = dh_ref[sl, :]
            c = jnp.where(rows < SUBLANES - 1, pltpu.roll(a, SUBLANES - 1, 0), a_next)
            for s in (1, 2, 4):
                keep = rows < SUBLANES - s
                sc = jnp.where(keep, pltpu.roll(c, SUBLANES - s, 0), 1.0)
                sd = jnp.where(keep, pltpu.roll(d, SUBLANES - s, 0), 0.0)
                d = d + c * sd
                c = c * sc
            gg = d + c * g_next
            da = gg * hprev
            dm = gg * (ii * xx)
            di = gg * (m * xx)
            dxc_ref[sl, :] = gg * (m * ii)
            dla = da * a - dm * (a * a / m)
            dra = dla * ((-RG_C) * sp_row) * (rr * (1.0 - rr))
            dia = di * (ii * (1.0 - ii))
            drai_ref[sl, pl.ds(0, cw)] = dra
            drai_ref[sl, pl.ds(cw, cw)] = dia
            return (gg[0:1, :], a[0:1, :], s_ra + dra, s_ia + dia, s_sp + dla * ((-RG_C) * rr))

        zero = jnp.zeros((SUBLANES, cw), F32)
        g_first, a_first, s_ra, s_ia, s_sp = lax.fori_loop(
            0, ng, group, (cg_ref[0:1, :], ca_ref[0:1, :], zero, zero, zero), unroll=2)
        cg_ref[...] = jnp.broadcast_to(g_first, cg_ref.shape)
        ca_ref[...] = jnp.broadcast_to(a_first, ca_ref.shape)
        for ref, acc in ((crai_ref.at[:, pl.ds(0, cw)], s_ra), (crai_ref.at[:, pl.ds(cw, cw)], s_ia), (csp_ref, s_sp)):
            _accumulate(ref, jnp.sum(acc, axis=0, keepdims=True), step == 0)

    tile = pl.BlockSpec((tm, cw), lambda j, i: (nt - 1 - i, j))
    wide = pl.BlockSpec((tm, 2 * cw), lambda j, i: (nt - 1 - i, j))
    vec = pl.BlockSpec((1, cw), lambda j, i: (0, j))
    return pl.pallas_call(
        body, name=name, grid=(C // cw, nt),
        in_specs=[tile, tile, pl.BlockSpec((SUBLANES, cw), lambda j, i: (jnp.maximum((nt - 1 - i) * hb - 1, 0), j)),
                  pl.BlockSpec((tm, cw), lambda j, i: (nt - 1 - i, 2 * j)),
                  pl.BlockSpec((tm, cw), lambda j, i: (nt - 1 - i, 2 * j + 1)), tile, tile, tile, vec],
        out_specs=[wide, tile, pl.BlockSpec((1, 2 * cw), lambda j, i: (0, j)), vec],
        out_shape=[jax.ShapeDtypeStruct((T, 2 * C), F32), jax.ShapeDtypeStruct((T, C), F32),
                   jax.ShapeDtypeStruct((1, 2 * C), F32), jax.ShapeDtypeStruct((1, C), F32)],
        scratch_shapes=[pltpu.VMEM((SUBLANES, cw), F32), pltpu.VMEM((SUBLANES, cw), F32)],
        compiler_params=_params(("parallel", "arbitrary"), 24 * tm * cw * 4),
    )(dh, h, h, ri, ri, xc, a_fwd, m_fwd, sp)


def _cscan_tables(lr, li, reverse):
    lam = (lr.reshape(-1), -li.reshape(-1) if reverse else li.reshape(-1))

    def mul(p, q):
        return p[0] * q[0] - p[1] * q[1], p[0] * q[1] + p[1] * q[0]

    pows = [lam]
    for _ in range(SUBLANES - 1):
        pows.append(mul(pows[-1], lam))
    zero = jnp.zeros_like(lam[0])
    tab = jnp.stack([pows[0][0], pows[0][1], pows[1][0], pows[1][1], pows[3][0], pows[3][1], zero, zero])
    if reverse:
        pows = pows[::-1]
    return tab, jnp.stack([p[0] for p in pows]), jnp.stack([p[1] for p in pows])


def _power_slabs(lr, li, n):
    pr, pi = lr.reshape(1, -1), li.reshape(1, -1)
    while pr.shape[0] < n:
        tr, ti = pr[-1:], pi[-1:]
        pr, pi = (jnp.concatenate([pr, pr * tr - pi * ti], axis=0), jnp.concatenate([pi, pr * ti + pi * tr], axis=0))
    return jnp.repeat(pr, SUBLANES, axis=0), jnp.repeat(pi, SUBLANES, axis=0), pr[-1], pi[-1]


def _rows_to_segments(src_ref, dst_ref):
    seg = src_ref.shape[0] // SUBLANES
    for g in range(seg):
        dst_ref[pl.ds(g * SUBLANES, SUBLANES), :] = src_ref[pl.ds(g, SUBLANES, stride=seg), :].astype(dst_ref.dtype)


def _segments_to_rows(src_ref, dst_ref):
    seg = src_ref.shape[0] // SUBLANES
    for r in range(SUBLANES):
        dst_ref[pl.ds(r * seg, seg), :] = src_ref[pl.ds(r, seg, stride=SUBLANES), :].astype(dst_ref.dtype)


def _seg_scan_tile(xr_ref, xi_ref, pbr_ref, pbi_ref, tab_ref, pwr_ref, pwi_ref, cr_ref, ci_ref, *, reverse, h=None):
    tm, cw = xr_ref.shape
    seg = tm // SUBLANES
    rows = _rows8(cw)
    sign = -1.0 if reverse else 1.0
    l_re, l_im = pbr_ref[0:1, :], sign * pbi_ref[0:1, :]

    def slab(g):
        return pl.ds(pl.multiple_of(g * SUBLANES, SUBLANES), SUBLANES)

    def local(k, state):
        sl = slab(seg - 1 - k if reverse else k)
        sr, si = state
        nr = xr_ref[sl, :] + (l_re * sr - l_im * si)
        ni = xi_ref[sl, :] + (l_re * si + l_im * sr)
        xr_ref[sl, :] = nr
        xi_ref[sl, :] = ni
        return nr, ni

    zero = jnp.zeros((SUBLANES, cw), F32)
    er, ei = lax.fori_loop(0, seg, local, (zero, zero), unroll=2)

    for k, s in enumerate((1, 2, 4)):
        shift = SUBLANES - s if reverse else s
        keep = rows < SUBLANES - s if reverse else rows >= s
        sr = jnp.where(keep, pltpu.roll(er, shift, 0), 0.0)
        si = jnp.where(keep, pltpu.roll(ei, shift, 0), 0.0)
        m_re, m_im = tab_ref[2 * k:2 * k + 1, :], tab_ref[2 * k + 1:2 * k + 2, :]
        er, ei = er + (m_re * sr - m_im * si), ei + (m_re * si + m_im * sr)
    cin_r, cin_i = cr_ref[0:1, :], ci_ref[0:1, :]
    pwr, pwi = pwr_ref[...], pwi_ref[...]
    er, ei = er + (pwr * cin_r - pwi * cin_i), ei + (pwr * cin_i + pwi * cin_r)
    if reverse:
        ent_r = jnp.where(rows == SUBLANES - 1, cin_r, pltpu.roll(er, SUBLANES - 1, 0))
        ent_i = jnp.where(rows == SUBLANES - 1, cin_i, pltpu.roll(ei, SUBLANES - 1, 0))
        out_r, out_i = er[0:1, :], ei[0:1, :]
    else:
        ent_r = jnp.where(rows == 0, cin_r, pltpu.roll(er, 1, 0))
        ent_i = jnp.where(rows == 0, cin_i, pltpu.roll(ei, 1, 0))
        out_r, out_i = er[SUBLANES - 1:SUBLANES, :], ei[SUBLANES - 1:SUBLANES, :]
    cr_ref[...] = jnp.broadcast_to(out_r, cr_ref.shape)
    ci_ref[...] = jnp.broadcast_to(out_i, ci_ref.shape)

    if h is not None:
        hr_ref, hi_ref, hr_last, hi_last = h
        hr_wrap = _shift_down(hr_ref[pl.ds(tm - SUBLANES, SUBLANES), :], hr_last, 1, rows)
        hi_wrap = _shift_down(hi_ref[pl.ds(tm - SUBLANES, SUBLANES), :], hi_last, 1, rows)

    def fix(g, sums):
        sl = slab(g)
        power = slab(seg - 1 - g) if reverse else sl
        pr, pi = pbr_ref[power, :], sign * pbi_ref[power, :]
        nr = xr_ref[sl, :] + (pr * ent_r - pi * ent_i)
        ni = xi_ref[sl, :] + (pr * ent_i + pi * ent_r)
        xr_ref[sl, :] = nr
        xi_ref[sl, :] = ni
        if h is None:
            return sums
        before = slab(jnp.maximum(g - 1, 0))
        hr1 = jnp.where(g == 0, hr_wrap, hr_ref[before, :])
        hi1 = jnp.where(g == 0, hi_wrap, hi_ref[before, :])
        return sums[0] + (nr * hr1 + ni * hi1), sums[1] + (ni * hr1 - nr * hi1)

    return lax.fori_loop(0, seg, fix, (zero, zero) if h is not None else (), unroll=2)


S5_TILE = 2048


def _s5_fwd(z, u_off, wb_re, wb_im, wc_re, wc_im_neg, d_row, powers, *, T, tm=S5_TILE, name):
    J, ku, kp = wb_re.shape
    nt = T // tm
    pb_re, pb_im, top_re, top_im = powers
    tab, pw_re, pw_im = _cscan_tables(top_re, top_im, False)
    u_blk = u_off // ku

    def body(u_ref, wbr_ref, wbi_ref, wcr_ref, wci_ref, d_ref, pbr_ref, pbi_ref, tab_ref, pwr_ref, pwi_ref,
             hr_ref, hi_ref, y_ref, yg_ref, cr_ref, ci_ref, us_ref, ys_ref):
        @pl.when(pl.program_id(1) == 0)
        def _():
            cr_ref[...] = jnp.zeros_like(cr_ref)
            ci_ref[...] = jnp.zeros_like(ci_ref)

        _rows_to_segments(u_ref, us_ref)
        u = us_ref[...]
        ub = u.astype(BF16)
        hr_ref[...] = jnp.dot(ub, wbr_ref[...], preferred_element_type=F32)
        hi_ref[...] = jnp.dot(ub, wbi_ref[...], preferred_element_type=F32)
        _seg_scan_tile(hr_ref, hi_ref, pbr_ref, pbi_ref, tab_ref, pwr_ref, pwi_ref, cr_ref, ci_ref, reverse=False)
        y = (jnp.dot(hr_ref[...].astype(BF16), wcr_ref[...], preferred_element_type=F32)
             + jnp.dot(hi_ref[...].astype(BF16), wci_ref[...], preferred_element_type=F32) + d_ref[...] * u)
        ys_ref[...] = y
        _segments_to_rows(ys_ref, y_ref)
        ys_ref[...] = _gelu(y)
        _segments_to_rows(ys_ref, yg_ref)

    wb_spec = pl.BlockSpec((None, ku, kp), lambda j, i: (j, 0, 0))
    wc_spec = pl.BlockSpec((None, kp, ku), lambda j, i: (j, 0, 0))
    small = pl.BlockSpec((SUBLANES, kp), lambda j, i: (0, j))
    slabs = pl.BlockSpec((tm, kp), lambda j, i: (0, j))
    state = pl.BlockSpec((tm, kp), lambda j, i: (i, j))
    chan = pl.BlockSpec((tm, ku), lambda j, i: (i, j))
    return pl.pallas_call(
        body, name=name, grid=(J, nt),
        in_specs=[pl.BlockSpec((tm, ku), lambda j, i: (i, u_blk + j)), wb_spec, wb_spec, wc_spec, wc_spec,
                  pl.BlockSpec((1, ku), lambda j, i: (0, j)), slabs, slabs, small, small, small],
        out_specs=[state, state, chan, chan],
        out_shape=[jax.ShapeDtypeStruct((T, J * kp), F32)] * 2
        + [jax.ShapeDtypeStruct((T, J * ku), F32), jax.ShapeDtypeStruct((T, J * ku), BF16)],
        scratch_shapes=[pltpu.VMEM((SUBLANES, kp), F32), pltpu.VMEM((SUBLANES, kp), F32),
                        pltpu.VMEM((tm, ku), F32), pltpu.VMEM((tm, ku), F32)],
        compiler_params=_params(("parallel", "arbitrary"), 14 * tm * kp * 4),
    )(z, wb_re, wb_im, wc_re, wc_im_neg, d_row, pb_re, pb_im, tab, pw_re, pw_im)


def _s5_bwd(dy, z, u_off, h_re, h_im, wb_re, wb_im, wc_re, wc_im_neg, d_row, powers, dz, *, T, tm=S5_TILE, name):
    J, ku, kp = wb_re.shape
    nt, hb = T // tm, tm // SUBLANES
    pb_re, pb_im, top_re, top_im = powers
    tab, pw_re, pw_im = _cscan_tables(top_re, top_im, True)
    u_blk = u_off // ku
    contract_rows = (((0,), (0,)), ((), ()))
    contract_cols = (((1,), (1,)), ((), ()))

    def body(dy_ref, u_ref, hr_ref, hrp_ref, hi_ref, hip_ref, wbr_ref, wbi_ref, wcr_ref, wci_ref, d_ref,
             pbr_ref, pbi_ref, tab_ref, pwr_ref, pwi_ref, dz_in_ref,
             du_ref, dlr_ref, dli_ref, dd_ref, dwbr_ref, dwbi_ref, dwcr_ref, dwci_ref,
             gr_ref, gi_ref, cr_ref, ci_ref, dys_ref, us_ref):
        step = pl.program_id(1)
        first = step == 0

        @pl.when(first)
        def _():
            cr_ref[...] = jnp.zeros_like(cr_ref)
            ci_ref[...] = jnp.zeros_like(ci_ref)

        _rows_to_segments(dy_ref, dys_ref)
        _rows_to_segments(u_ref, us_ref)
        dy_t, u = dys_ref[...], us_ref[...]
        dyb, ub = dy_t.astype(BF16), u.astype(BF16)
        gr_ref[...] = lax.dot_general(dyb, wcr_ref[...], contract_cols, preferred_element_type=F32)
        gi_ref[...] = lax.dot_general(dyb, wci_ref[...], contract_cols, preferred_element_type=F32)
        hr_last = jnp.where(step == nt - 1, 0.0, hrp_ref[...])
        hi_last = jnp.where(step == nt - 1, 0.0, hip_ref[...])
        s_re, s_im = _seg_scan_tile(gr_ref, gi_ref, pbr_ref, pbi_ref, tab_ref, pwr_ref, pwi_ref, cr_ref, ci_ref,
                                    reverse=True, h=(hr_ref, hi_ref, hr_last, hi_last))
        _accumulate(dlr_ref, jnp.sum(s_re, axis=0, keepdims=True), first)
        _accumulate(dli_ref, jnp.sum(s_im, axis=0, keepdims=True), first)
        grb, gib = gr_ref[...].astype(BF16), gi_ref[...].astype(BF16)
        du = (lax.dot_general(grb, wbr_ref[...], contract_cols, preferred_element_type=F32)
              + lax.dot_general(gib, wbi_ref[...], contract_cols, preferred_element_type=F32) + dy_t * d_ref[...])
        dys_ref[...] = du
        _segments_to_rows(dys_ref, du_ref)
        _accumulate(dd_ref, jnp.sum(dy_t * u, axis=0, keepdims=True), first)
        _accumulate(dwbr_ref, lax.dot_general(ub, grb, contract_rows, preferred_element_type=F32), first)
        _accumulate(dwbi_ref, lax.dot_general(ub, gib, contract_rows, preferred_element_type=F32), first)
        _accumulate(dwcr_ref, lax.dot_general(dyb, hr_ref[...].astype(BF16), contract_rows,
                                              preferred_element_type=F32), first)
        _accumulate(dwci_ref, lax.dot_general(dyb, hi_ref[...].astype(BF16), contract_rows,
                                              preferred_element_type=F32), first)

    def tix(i):
        return nt - 1 - i

    wb_spec = pl.BlockSpec((None, ku, kp), lambda j, i: (j, 0, 0))
    wc_spec = pl.BlockSpec((None, kp, ku), lambda j, i: (j, 0, 0))
    small = pl.BlockSpec((SUBLANES, kp), lambda j, i: (0, j))
    state = pl.BlockSpec((tm, kp), lambda j, i: (tix(i), j))
    halo = pl.BlockSpec((SUBLANES, kp), lambda j, i: (jnp.maximum(tix(i) * hb - 1, 0), j))
    chan = pl.BlockSpec((tm, ku), lambda j, i: (tix(i), j))
    svec = pl.BlockSpec((1, kp), lambda j, i: (0, j))
    cvec = pl.BlockSpec((1, ku), lambda j, i: (0, j))
    slabs = pl.BlockSpec((tm, kp), lambda j, i: (0, j))
    return pl.pallas_call(
        body, name=name, grid=(J, nt),
        in_specs=[chan, pl.BlockSpec((tm, ku), lambda j, i: (tix(i), u_blk + j)), state, halo, state, halo,
                  wb_spec, wb_spec, wc_spec, wc_spec, cvec, slabs, slabs, small, small, small, ANY],
        out_specs=[pl.BlockSpec((tm, ku), lambda j, i: (tix(i), u_blk + j)), svec, svec, cvec,
                   wb_spec, wb_spec, wb_spec, wb_spec],
        input_output_aliases={16: 0},
        out_shape=[jax.ShapeDtypeStruct(dz.shape, dz.dtype), jax.ShapeDtypeStruct((1, J * kp), F32),
                   jax.ShapeDtypeStruct((1, J * kp), F32), jax.ShapeDtypeStruct((1, J * ku), F32),
                   jax.ShapeDtypeStruct((J, ku, kp), F32), jax.ShapeDtypeStruct((J, ku, kp), F32),
                   jax.ShapeDtypeStruct((J, ku, kp), F32), jax.ShapeDtypeStruct((J, ku, kp), F32)],
        scratch_shapes=[pltpu.VMEM((tm, kp), F32), pltpu.VMEM((tm, kp), F32),
                        pltpu.VMEM((SUBLANES, kp), F32), pltpu.VMEM((SUBLANES, kp), F32),
                        pltpu.VMEM((tm, ku), F32), pltpu.VMEM((tm, ku), F32)],
        compiler_params=_params(("parallel", "arbitrary"), 16 * tm * kp * 4),
    )(dy, z, h_re, h_re, h_im, h_im, wb_re, wb_im, wc_re, wc_im_neg, d_row, pb_re, pb_im, tab, pw_re, pw_im, dz)


def _mesh_pos():
    return lax.axis_index("x"), lax.axis_index("y"), lax.axis_index("c")


def _dev_index(px, py, pc):
    return 4 * px + 2 * py + pc


HBM = pl.BlockSpec(memory_space=pltpu.HBM)
SEM = pl.BlockSpec(memory_space=pltpu.SEMAPHORE)
EFFECT = pltpu.SideEffectType.DATAFLOW_SIDE_EFFECTING
RELATIONS = [(dx, dy, dc) for dx in (0, 1) for dy in (0, 1) for dc in (0, 1) if (dx, dy, dc) != (0, 0, 0)]


def _peer(rel):
    x, y, c = _mesh_pos()
    dx, dy, dc = rel
    return (x + dx - 2 * x * dx, y + dy - 2 * y * dy, c + dc - 2 * c * dc)


CHIP_RELATIONS = [(1, 0, 0), (0, 1, 0), (1, 1, 0)]
EXCHANGE_PEERS = {"gather": RELATIONS, "scatter": RELATIONS, "own": [(0, 0, 1)] + CHIP_RELATIONS, "pass": CHIP_RELATIONS}


def _split_copy(src_ref, land_ref, send_sems, recv_sems, k, mode, incoming):
    x, y, c = _mesh_pos()
    me = _dev_index(x, y, c)
    peer = _peer(EXCHANGE_PEERS[mode][k])
    if mode == "pass":
        held, theirs = _dev_index(peer[0], peer[1], c), _dev_index(peer[0], peer[1], 1 - c)
        src, slot, target = land_ref.at[held], theirs if incoming else held, (x, y, 1 - c)
    else:
        src = src_ref.at[_dev_index(*peer)] if mode == "scatter" else src_ref
        slot, target = _dev_index(*peer) if incoming else me, peer
    return pltpu.make_async_remote_copy(src_ref=src, dst_ref=land_ref.at[slot], send_sem=send_sems.at[k],
                                        recv_sem=recv_sems.at[k], device_id=target, device_id_type=MESH)


def _exchange_start(srcs, lands, *, mode, after=None, name):
    n = len(srcs)
    n_after = 0 if after is None else 1
    n_rel = len(EXCHANGE_PEERS[mode])

    def body(*refs):
        src_refs, land_refs = refs[:n], refs[n:2 * n]
        first_out = 2 * n + n_after
        send, recv = refs[first_out:first_out + n], refs[first_out + n:first_out + 2 * n]
        token = refs[-1]
        for k in range(n_rel):
            for a in range(n):
                _split_copy(src_refs[a], land_refs[a], send[a], recv[a], k, mode, incoming=False).start()
        token[...] = jnp.zeros_like(token)

    outs = pl.pallas_call(
        body, name=name, in_specs=[HBM] * (2 * n) + [ANY] * n_after,
        out_shape=[pltpu.SemaphoreType.DMA((n_rel,))] * (2 * n)
        + [pltpu.HBM(s.shape, s.dtype) for s in srcs] + [pltpu.HBM(s.shape, s.dtype) for s in lands]
        + [jax.ShapeDtypeStruct((SUBLANES, LANES), F32)],
        out_specs=[SEM] * (2 * n) + [HBM] * (2 * n) + [pl.BlockSpec(memory_space=pltpu.VMEM)],
        input_output_aliases={**{a: 2 * n + a for a in range(n)}, **{n + a: 3 * n + a for a in range(n)}},
        compiler_params=pltpu.CompilerParams(has_side_effects=EFFECT),
    )(*[pltpu.with_memory_space_constraint(s, pltpu.HBM) for s in srcs],
      *[pltpu.with_memory_space_constraint(s, pltpu.HBM) for s in lands], *([after] if n_after else []))
    per_array = [(outs[a], outs[n + a], outs[2 * n + a], outs[3 * n + a]) for a in range(n)]
    return per_array, outs[-1]


def _exchange_wait(handle, after, *, mode, name):
    send_sems, recv_sems, src_thru, land_thru = handle
    after = after if isinstance(after, (tuple, list)) else (after,)

    def body(src_ref, land_ref, send, recv, *rest):
        for k in range(len(EXCHANGE_PEERS[mode])):
            cp = _split_copy(src_ref, land_ref, send, recv, k, mode, incoming=True)
            cp.wait_send()
            cp.wait_recv()

    return pl.pallas_call(
        body, name=name, in_specs=[HBM, HBM, SEM, SEM] + [ANY] * len(after),
        out_shape=[pltpu.HBM(src_thru.shape, src_thru.dtype), pltpu.HBM(land_thru.shape, land_thru.dtype)],
        out_specs=[HBM, HBM], input_output_aliases={0: 0, 1: 1},
        compiler_params=pltpu.CompilerParams(has_side_effects=EFFECT),
    )(src_thru, land_thru, send_sems, recv_sems, *after)[1]


def _landing_zone(own_block):
    me = _dev_index(*_mesh_pos())
    zone = lax.empty((N_DEV,) + own_block.shape, own_block.dtype)
    return lax.dynamic_update_index_in_dim(zone, own_block, me, 0)


def _row_tile(rows, want):
    t = min(want, rows) // SUBLANES * SUBLANES
    while rows % t:
        t -= SUBLANES
    return t


def _sum_slots(recv, *, tr, name):
    s_, r_, c_ = recv.shape
    tr = _row_tile(r_, tr)

    def body(g_ref, o_ref):
        acc = g_ref[0]
        for s in range(1, s_):
            acc = acc + g_ref[s]
        o_ref[...] = acc

    return pl.pallas_call(
        body, name=name, grid=(r_ // tr,),
        in_specs=[pl.BlockSpec((s_, tr, c_), lambda i: (0, i, 0))],
        out_specs=pl.BlockSpec((tr, c_), lambda i: (i, 0)),
        out_shape=jax.ShapeDtypeStruct((r_, c_), F32),
        compiler_params=_params(("parallel",), (2 * s_ + 3) * tr * c_ * 4),
    )(recv)


def _adamw(recv, w, m, v, *, tr, name):
    s_, r_, c_ = recv.shape
    tr = _row_tile(r_, tr)
    assert w.shape == (r_, c_), (name, w.shape, recv.shape)
    c1 = 1.0 - ADAM_B1 ** ADAM_STEP
    c2 = 1.0 - ADAM_B2 ** ADAM_STEP

    def body(g_ref, w_ref, m_ref, v_ref, go_ref, d_ref, mo_ref, vo_ref):
        g = g_ref[0].astype(F32)
        for s in range(1, s_):
            g = g + g_ref[s].astype(F32)
        mn = ADAM_B1 * m_ref[...] + (1.0 - ADAM_B1) * g
        vn = ADAM_B2 * v_ref[...] + (1.0 - ADAM_B2) * (g * g)
        go_ref[...] = g
        mo_ref[...] = mn
        vo_ref[...] = vn
        d_ref[...] = -ADAM_LR * ((mn / c1) / (jnp.sqrt(vn / c2) + ADAM_EPS) + ADAM_WD * w_ref[...])

    tile = pl.BlockSpec((tr, c_), lambda i: (i, 0))
    return pl.pallas_call(
        body, name=name, grid=(r_ // tr,),
        in_specs=[pl.BlockSpec((s_, tr, c_), lambda i: (0, i, 0)), tile, tile, tile],
        out_specs=[tile] * 4, out_shape=[jax.ShapeDtypeStruct((r_, c_), F32)] * 4,
        compiler_params=_params(("parallel",), (2 * s_ + 16) * tr * c_ * 4),
    )(recv, w, m, v)


def _adamw_whole(gs, ws, ms, vs, *, name):
    n = len(gs)
    c1 = 1.0 - ADAM_B1 ** ADAM_STEP
    c2 = 1.0 - ADAM_B2 ** ADAM_STEP

    def body(*refs):
        for i in range(n):
            g, w = refs[i][...], refs[n + i][...]
            mn = ADAM_B1 * refs[2 * n + i][...] + (1.0 - ADAM_B1) * g
            vn = ADAM_B2 * refs[3 * n + i][...] + (1.0 - ADAM_B2) * (g * g)
            refs[4 * n + 3 * i][...] = -ADAM_LR * ((mn / c1) / (jnp.sqrt(vn / c2) + ADAM_EPS) + ADAM_WD * w)
            refs[4 * n + 3 * i + 1][...] = mn
            refs[4 * n + 3 * i + 2][...] = vn

    whole = pl.BlockSpec(memory_space=pltpu.VMEM)
    lane_padded = sum(math.prod(g.shape[:-1]) * (-(-g.shape[-1] // LANES) * LANES) for g in gs)
    outs = pl.pallas_call(
        body, name=name, in_specs=[whole] * (4 * n), out_specs=[whole] * (3 * n),
        out_shape=[jax.ShapeDtypeStruct(g.shape, F32) for g in gs for _ in range(3)],
        compiler_params=pltpu.CompilerParams(vmem_limit_bytes=int(min(max(16 * lane_padded * 4, 16 * 2 ** 20), VMEM_CAP))),
    )(*gs, *ws, *ms, *vs)
    return [tuple(outs[3 * i:3 * i + 3]) for i in range(n)]


def _s5_discretise(a_re, a_im, log_dt, b_re, b_im):
    dt = jnp.exp(log_dt)[:, None]
    lr = jnp.minimum(a_re, -1e-4)
    li = a_im
    mag = jnp.exp(lr * dt)
    lbr = mag * jnp.cos(li * dt)
    lbi = mag * jnp.sin(li * dt)
    zr, zi = lbr - 1.0, lbi
    den = lr * lr + li * li
    fr = (zr * lr + zi * li) / den
    fi = (zi * lr - zr * li) / den
    bbr = fr[..., None] * b_re - fi[..., None] * b_im
    bbi = fr[..., None] * b_im + fi[..., None] * b_re
    return lbr, lbi, bbr, bbi


def _softplus_neg(lam):
    return jnp.maximum(-lam, 0.0) + jnp.log(1.0 + jnp.exp(-jnp.abs(lam)))


S5_Q = 8
RG_Q = 2


def _local_step(x, tgt, W, comm):
    T, D = x.shape
    C = D
    G, P, H = W["ssm_b_re"].shape
    S = G * H
    F = W["mlp_b_up"].shape[1]
    n_in = 2 * C + S + 2 * D
    heads, hd = W["rg_wa"].shape[0], W["rg_wa"].shape[1]
    u_off, ga_off, gb_off = 2 * C, 2 * C + S, 2 * C + S + D

    if comm.first_token is not None:
        anchored = ("rg_lambda", "ssm_a_re", "rg_wa", "rg_wx", "ssm_c_re", "ssm_c_im")
        W = {**W, **{k: W[k] + comm.first_token[0, 0] for k in anchored}}
    sp, sp_vjp = jax.vjp(_softplus_neg, W["rg_lambda"])
    (lbr, lbi, bbr, bbi), s5_vjp = jax.vjp(_s5_discretise, W["ssm_a_re"], W["ssm_a_im"], W["ssm_log_dt"],
                                           W["ssm_b_re"], W["ssm_b_im"])
    lam_re, lam_im = lbr.reshape(-1), lbi.reshape(-1)
    jr, kr = heads // RG_Q, RG_Q * hd
    w_ri = jnp.concatenate([_bd_pack(W["rg_wa"], RG_Q), _bd_pack(W["rg_wx"], RG_Q)], axis=2).astype(BF16)
    b_ri = jnp.concatenate([W["rg_ba"].reshape(jr, kr), W["rg_bx"].reshape(jr, kr)], axis=1).reshape(1, -1)
    wb_re = _bd_pack(jnp.swapaxes(bbr, 1, 2), S5_Q).astype(BF16)
    wb_im = _bd_pack(jnp.swapaxes(bbi, 1, 2), S5_Q).astype(BF16)
    wc_re = _bd_pack(jnp.swapaxes(W["ssm_c_re"], 1, 2), S5_Q).astype(BF16)
    wc_im_neg = _bd_pack(jnp.swapaxes(-W["ssm_c_im"], 1, 2), S5_Q).astype(BF16)
    d_row = W["ssm_d"].reshape(1, S)
    powers = _power_slabs(lam_re, lam_im, S5_TILE // SUBLANES)

    x_bf = x.astype(BF16) if comm.first_token is None else (x + comm.first_token[0, 0]).astype(BF16)
    w_in, conv_w = comm.first_weights((x_bf, w_ri, wb_re, wb_im, wc_re, wc_im_neg, powers[0], powers[1]))
    z = _mm(x_bf, w_in, M=T, N=n_in, K=D, tm=1024, tn=n_in // 4, tk=D, after=comm.gather_token, name="fwd_in_proj")
    started = comm.start_weights(("mlp_w_up",), z)
    xc = _conv_fwd(z, conv_w, W["conv_b"], T=T, C=C, after=started, name="fwd_conv")
    ri = _bd([(xc, 0, w_ri)], T=T, J=jr, kb=kr, nb=2 * kr, extras=[(b_ri, "vec", 0)],
             epilogue=lambda acc, b: (_sig(acc + b),), name="fwd_gates")
    h, p, a_fwd, m_fwd = _rg_scan_fwd(z, ri, xc, sp, T=T, C=C, gate_off=C, cw=kr, name="fwd_rg_scan")
    w_a_out = comm.weight("w_a_out", p)
    started = comm.start_weights(("mlp_w_down",), p)
    y_a = _mm(p, w_a_out, M=T, N=D, K=C, out_dtypes=(BF16,), tm=1024, tn=D, tk=C, after=started, name="fwd_rg_out")

    h_re, h_im, y_s, yg = _s5_fwd(z, u_off, wb_re, wb_im, wc_re, wc_im_neg, d_row, powers, T=T, name="fwd_s5")
    w_glu_w, w_glu_v = comm.weight("glu_w", yg), comm.weight("glu_v", yg)
    glu_a = _mm(yg, w_glu_w, M=T, N=D, K=S, out_dtypes=(BF16,), tm=1024, tn=D, tk=S, name="fwd_glu_w")
    cwm = 1024

    def mix_fn(b, ga, gb, ya, a):
        return b, _sig(ga) * ya.astype(F32) + _sig(gb) * (a.astype(F32) * _sig(b))

    glu_b, mix = _mm(yg, w_glu_v, M=T, N=D, K=S, tm=512, tn=cwm, tk=S,
                     extras=[(z, "mn", ga_off // cwm), (z, "mn", gb_off // cwm), (y_a, "mn"), (glu_a, "mn")],
                     epilogue=mix_fn, n_out=2, out_dtypes=(BF16, BF16), name="fwd_glu_v_mix")
    w_out = comm.weight("w_out", mix)
    def out_ln1_fn(acc, xv, g, b):
        s = ALPHA * xv + acc
        xhat, _ = _ln_stats(s)
        y = xhat * g + b
        return s, y, y

    s1, x1, x1_bf = _mm(mix, w_out, M=T, N=D, K=D, tm=256, tn=D, tk=D,
                        extras=[(x, "mn"), (W["ln1_g"], "n"), (W["ln1_b"], "n")], epilogue=out_ln1_fn, n_out=3,
                        out_dtypes=(F32, F32, BF16), name="fwd_out_proj_ln1")
    w_up = comm.weight("mlp_w_up", x1_bf)

    def mlp_up_fn(acc, b):
        hp = acc + b
        rl = jnp.maximum(hp, 0.0)
        return rl * rl, hp

    hact, hpre = _mm(x1_bf, w_up, M=T, N=F, K=D, tm=1024, tn=1024, tk=D, extras=[(W["mlp_b_up"], "n")],
                     epilogue=mlp_up_fn, n_out=2, out_dtypes=(BF16, BF16), name="fwd_mlp_up")
    w_down = comm.weight("mlp_w_down", hact)
    s2 = _mm(hact, w_down, M=T, N=D, K=F, tm=1024, tn=1024, tk=2048,
             extras=[(x1, "mn"), (W["mlp_b_down"], "n")], epilogue=lambda acc, xv, b: (ALPHA * xv + acc + b,),
             name="fwd_mlp_down")

    def ln2_fn(s, t, g, b):
        xhat, rstd = _ln_stats(s)
        err = xhat * g + b - t
        dy = err * (1.0 / D)
        ds = _ln_bwd(dy, g, xhat, rstd)
        return ds, ds, 0.5 * dy * err, dy * xhat, dy, ds

    ds2, ds2_bf, loss_cols, d_ln2_g, d_ln2_b, d_b_down = _ew(
        ln2_fn, [(s2, "tile", 0), (tgt, "tile", 0), (W["ln2_g"], "vec", 0), (W["ln2_b"], "vec", 0)],
        T=T, C=D, n_out=2, n_cs=4, out_dtypes=(F32, BF16), tm=256, name="bwd_loss_ln2")
    d_w_down = _mm(hact, ds2_bf, M=F, N=D, K=T, ta=True, out_dtypes=(BF16,), tm=1024, tn=1024, tk=4096, name="bwd_w_down")
    sent = comm.send_grad("mlp_w_down", d_w_down)

    def dhpre_fn(acc, hp):
        dv = acc * (2.0 * jnp.maximum(hp.astype(F32), 0.0))
        return dv, dv

    dhpre, d_b_up = _mm(ds2_bf, w_down, M=T, N=F, K=D, tb=True, tm=1024, tn=1024, tk=D, extras=[(hpre, "mn")],
                        epilogue=dhpre_fn, n_cs=1, out_dtypes=(BF16,), after=sent, name="bwd_mlp_down")
    d_w_up = _mm(x1_bf, dhpre, M=D, N=F, K=T, ta=True, out_dtypes=(BF16,), n_split=N_DEV, tm=1024, tn=F // N_DEV, tk=4096, name="bwd_w_up")
    sent = comm.send_grad("mlp_w_up", d_w_up)
    dx1 = _mm(dhpre, w_up, M=T, N=D, K=F, tb=True, tm=1024, tn=1024, tk=2048,
              extras=[(ds2, "mn")], epilogue=lambda acc, dv: (ALPHA * dv + acc,), after=sent, name="bwd_mlp_up")

    def ln1_bwd_fn(s, dy, g):
        xhat, rstd = _ln_stats(s)
        ds = _ln_bwd(dy, g, xhat, rstd)
        return ds, ds, dy * xhat, dy

    ds1, ds1_bf, d_ln1_g, d_ln1_b = _ew(ln1_bwd_fn, [(s1, "tile", 0), (dx1, "tile", 0), (W["ln1_g"], "vec", 0)],
                                        T=T, C=D, n_out=2, n_cs=2, out_dtypes=(F32, BF16), tm=256, name="bwd_ln1")
    d_w_out = _mm(mix, ds1_bf, M=D, N=D, K=T, ta=True, out_dtypes=(BF16,), tm=1024, tn=1024, tk=4096, name="bwd_w_out")
    sent = comm.send_grad("w_out", d_w_out)
    def mix_bwd_fn(dm, ga, gb, ya, a, b):
        ya, a, b = ya.astype(F32), a.astype(F32), b.astype(F32)
        sa, sb, sv = _sig(ga), _sig(gb), _sig(b)
        yb = a * sv
        dyb = dm * sb
        return (dm * ya * (sa * (1.0 - sa)), dm * yb * (sb * (1.0 - sb)), dm * sa, dyb * sv,
                dyb * a * (sv * (1.0 - sv)))

    dz = lax.empty((T, n_in), BF16)
    dz, dg_b, dy_a, dglu_a, dglu_b = _mm(
        ds1_bf, w_out, M=T, N=D, K=D, tb=True, tm=512, tn=cwm, tk=D,
        extras=[(z, "mn", ga_off // cwm), (z, "mn", gb_off // cwm), (y_a, "mn"), (glu_a, "mn"), (glu_b, "mn")],
        epilogue=mix_bwd_fn, n_out=5, out_dtypes=(BF16,) * 5, after=sent, into=(dz, 0, ga_off // cwm),
        name="bwd_out_proj_mix")
    dz = lax.dynamic_update_slice(dz, dg_b, (0, gb_off))

    d_w_a_out = _mm(p, dy_a, M=C, N=D, K=T, ta=True, out_dtypes=(BF16,), tm=1024, tn=1024, tk=4096, name="bwd_w_a_out")
    sent = comm.send_grad("w_a_out", d_w_a_out)
    def dp_fn(dp, hv, gate):
        th = jnp.tanh(GELU_C * (gate + GELU_K * gate * gate * gate))
        gelu = 0.5 * gate * (1.0 + th)
        dgelu = 0.5 * (1.0 + th) + 0.5 * gate * (1.0 - th * th) * (GELU_C * (1.0 + 3.0 * GELU_K * gate * gate))
        return dp * gelu, dp * hv * dgelu

    dh, dz = _mm(dy_a, w_a_out, M=T, N=C, K=D, tb=True, tm=256, tn=C, tk=D, extras=[(h, "mn"), (z, "mn", 1)],
                 epilogue=dp_fn, n_out=2, out_dtypes=(F32, BF16), after=sent, into=(dz, 1, 1), name="bwd_rg_out")
    drai, dxc0, d_b_ri, d_sp = _rg_scan_bwd(dh, h, ri, xc, a_fwd, m_fwd, sp, T=T, C=C, cw=kr, name="bwd_rg_scan")
    dxc = _bd([(drai, 0, w_ri)], T=T, J=jr, kb=2 * kr, nb=kr, tw=True, extras=[(dxc0, "tile", 0)],
              epilogue=lambda acc, d0: (acc + d0,), name="bwd_gates")
    d_w_ri = _bdw(xc, 0, drai, 0, T=T, J=jr, kb=kr, nb=2 * kr, name="bwd_w_gates")
    d_wa, d_wx = _bd_unpack(d_w_ri[:, :, :kr], RG_Q), _bd_unpack(d_w_ri[:, :, kr:], RG_Q)
    d_b_ri = d_b_ri.reshape(jr, 2 * kr)
    d_ba, d_bx = d_b_ri[:, :kr].reshape(1, -1), d_b_ri[:, kr:].reshape(1, -1)
    dz, conv_sums = _conv_bwd(dxc, z, conv_w, dz, T=T, C=C, name="bwd_conv")
    d_conv_w, d_conv_b = conv_sums[0:4], conv_sums[4:5]
    (d_lambda,) = sp_vjp(d_sp)

    d_glu_w = _mm(yg, dglu_a, M=S, N=D, K=T, ta=True, out_dtypes=(BF16,), n_split=N_DEV, tm=1024, tn=D // N_DEV, tk=4096, name="bwd_w_glu_w")
    d_glu_v = _mm(yg, dglu_b, M=S, N=D, K=T, ta=True, out_dtypes=(BF16,), n_split=N_DEV, tm=1024, tn=D // N_DEV, tk=4096, name="bwd_w_glu_v")
    sent = comm.send_grad("glu_w", d_glu_w, "glu_v", d_glu_v)
    dyg0 = _mm(dglu_a, w_glu_w, M=T, N=S, K=D, tb=True, tm=1024, tn=S, tk=D, after=sent, name="bwd_glu_w")
    dy_s = _mm(dglu_b, w_glu_v, M=T, N=S, K=D, tb=True, tm=1024, tn=S, tk=D,
               extras=[(dyg0, "mn"), (y_s, "mn")], epilogue=lambda acc, d0, yv: ((acc + d0) * _dgelu(yv),),
               name="bwd_glu_v")
    dz, d_lbr, d_lbi, d_ssm_d, d_wb_re, d_wb_im, d_wc_re, d_wc_im_neg = _s5_bwd(
        dy_s, z, u_off, h_re, h_im, wb_re, wb_im, wc_re, wc_im_neg, d_row, powers, dz, T=T, name="bwd_s5")
    d_bbr = jnp.swapaxes(_bd_unpack(d_wb_re, S5_Q), 1, 2)
    d_bbi = jnp.swapaxes(_bd_unpack(d_wb_im, S5_Q), 1, 2)
    d_a_re, d_a_im, d_log_dt, d_b_re, d_b_im = s5_vjp((d_lbr.reshape(G, P), d_lbi.reshape(G, P), d_bbr, d_bbi))
    d_c_re = _bd_unpack(d_wc_re, S5_Q)
    d_c_im = -_bd_unpack(d_wc_im_neg, S5_Q)

    grads = dict(
        conv_w=d_conv_w, conv_b=d_conv_b, rg_wa=d_wa, rg_ba=d_ba, rg_wx=d_wx, rg_bx=d_bx,
        rg_lambda=d_lambda, ssm_a_re=d_a_re, ssm_a_im=d_a_im, ssm_log_dt=d_log_dt,
        ssm_b_re=d_b_re, ssm_b_im=d_b_im, ssm_c_re=d_c_re, ssm_c_im=d_c_im, ssm_d=d_ssm_d.reshape(G, H),
        ln1_g=d_ln1_g, ln1_b=d_ln1_b, mlp_b_up=d_b_up, mlp_b_down=d_b_down, ln2_g=d_ln2_g, ln2_b=d_ln2_b)
    sent = comm.send_small(grads)

    d_w_in = _mm(x_bf, dz, M=D, N=n_in, K=T, ta=True, out_dtypes=(BF16,), n_split=N_DEV, tm=1024, tn=n_in // N_DEV,
                 tk=4096, after=sent, name="bwd_w_in")
    sent = comm.send_grad("w_in", d_w_in)
    grad_x = _mm(dz, w_in, M=T, N=D, K=n_in, tb=True, tm=1024, tn=1024, tk=n_in // 4,
                 extras=[(ds1, "mn")], epilogue=lambda acc, dv: (ALPHA * dv + acc,), after=sent, name="bwd_in_proj")
    return jnp.sum(loss_cols), grad_x, grads


BIG = ("w_in", "w_a_out", "glu_w", "glu_v", "w_out", "mlp_w_up", "mlp_w_down")
COL_SHARDED = ("w_in", "glu_w", "glu_v", "mlp_w_up")
SMALL = ("conv_w", "conv_b", "rg_wa", "rg_ba", "rg_wx", "rg_bx", "rg_lambda", "ssm_a_re", "ssm_a_im", "ssm_log_dt",
         "ssm_b_re", "ssm_b_im", "ssm_c_re", "ssm_c_im", "ssm_d", "ln1_g", "ln1_b", "mlp_b_up", "mlp_b_down", "ln2_g",
         "ln2_b")
ORDER = ("w_in", "conv_w", "conv_b", "rg_wa", "rg_ba", "rg_wx", "rg_bx", "rg_lambda", "w_a_out", "ssm_a_re",
         "ssm_a_im", "ssm_log_dt", "ssm_b_re", "ssm_b_im", "ssm_c_re", "ssm_c_im", "ssm_d", "glu_w", "glu_v", "w_out",
         "ln1_g", "ln1_b", "mlp_w_up", "mlp_b_up", "mlp_w_down", "mlp_b_down", "ln2_g", "ln2_b")
TILE_ELEMS = SUBLANES * LANES


def _pack(arrs):
    pieces = []
    for a in arrs:
        flat = a.reshape(-1)
        flat = jnp.pad(flat, (0, (-flat.shape[0]) % TILE_ELEMS))
        pieces.append(flat.reshape(-1, LANES))
    rows = sum(p.shape[0] for p in pieces)
    pad_rows = (-rows) % (N_DEV * SUBLANES)
    if pad_rows:
        pieces.append(jnp.zeros((pad_rows, LANES), pieces[0].dtype))
    return jnp.concatenate(pieces, axis=0)


def _unpack(packed, shapes):
    out, row = [], 0
    for shp in shapes:
        n = math.prod(shp)
        rows = -(-n // TILE_ELEMS) * SUBLANES
        out.append(packed[row:row + rows].reshape(-1)[:n].reshape(shp))
        row += rows
    return out


class _Comm:
    def __init__(self, w):
        first = [w["w_in"].astype(BF16), w["conv_w"]]
        self._first, self.first_token = _exchange_start(first, [_landing_zone(s) for s in first], mode="own",
                                                        name="gather_in_start")
        self._shards = {k: w[k].astype(BF16) for k in BIG if k != "w_in"}
        self._weights, self._gathers, self._grads = {}, {}, {}

    def first_weights(self, after):
        lands = [_exchange_wait(h, after, mode="own", name="gather_in_wait_%d" % i) for i, h in enumerate(self._first)]
        unused = [lax.empty((2 * SUBLANES, LANES), BF16) for _ in lands]
        handles, passed = _exchange_start(unused, lands, mode="pass", name="gather_in_pass")
        w_in, taps = [_exchange_wait(h, passed, mode="pass", name="gather_in_got_%d" % i) for i, h in enumerate(handles)]
        self._weights["w_in"] = w_in
        self.gather_token = self.start_weights(("w_a_out", "glu_w", "glu_v", "w_out"), w_in)
        return self.weight("w_in", None), jnp.swapaxes(taps, 0, 1).reshape(taps.shape[1], -1)

    def start_weights(self, names, after):
        shards = [self._shards.pop(k) for k in names]
        handles, token = _exchange_start(shards, [_landing_zone(s) for s in shards], mode="gather", after=after,
                                         name="gather_start_" + names[0])
        self._gathers.update(zip(names, handles))
        return token

    def weight(self, k, after):
        if k not in self._weights:
            self._weights[k] = _exchange_wait(self._gathers.pop(k), after, mode="gather", name="gather_wait_" + k)
        gk = self._weights[k]
        if k in COL_SHARDED:
            return jnp.swapaxes(gk, 0, 1).reshape(gk.shape[1], -1)
        return gk.reshape(-1, gk.shape[-1])

    def send_grad(self, *names_and_parts):
        names, parts = names_and_parts[0::2], names_and_parts[1::2]
        parts = [p if k in COL_SHARDED else p.reshape(N_DEV, p.shape[0] // N_DEV, p.shape[1])
                 for k, p in zip(names, parts)]
        me = _dev_index(*_mesh_pos())
        lands = [_landing_zone(lax.dynamic_index_in_dim(p, me, 0, keepdims=False)) for p in parts]
        handles, token = _exchange_start(parts, lands, mode="scatter", name="grad_start_" + names[0])
        self._grads.update(zip(names, handles))
        return token

    def received_grad(self, k, after):
        return _exchange_wait(self._grads.pop(k), after, mode="scatter", name="grad_wait_" + k)

    def send_small(self, grads):
        return self.send_grad("small", _pack([grads[k] for k in SMALL]))

    def all_reduced_small(self, after, behind):
        recv = self.received_grad("small", after)
        block = _sum_slots(recv, tr=512, name="sum_small_grads")
        (handle,), started = _exchange_start([block], [_landing_zone(block)], mode="gather", name="small_sum_start")
        done = behind(started)
        return _exchange_wait(handle, done, mode="gather", name="small_sum_wait").reshape(-1, LANES)


SMALL_GROUPS = (("rg_wa", "rg_wx"), ("ssm_b_re",), ("ssm_b_im",),
                tuple(k for k in SMALL if k not in ("rg_wa", "rg_wx", "ssm_b_re", "ssm_b_im")))


def _step(x, tgt, w, m, v, raw_w, raw_m, raw_v):
    dev = _dev_index(*_mesh_pos())

    comm = _Comm(w)
    small = dict(w)
    for k in ("conv_b", "rg_ba", "rg_bx", "rg_lambda", "ln1_g", "ln1_b", "mlp_b_up", "mlp_b_down", "ln2_g", "ln2_b"):
        small[k] = w[k].reshape(1, -1)

    loss_part, grad_x, grads = _local_step(x, tgt, small, comm)

    out_g, out_d, out_m, out_v = {}, {}, {}, {}

    def update_large(started):
        for k in BIG:
            rk = comm.received_grad(k, (grad_x, started))
            out_g[k], out_d[k], out_m[k], out_v[k] = _adamw(rk, w[k], m[k], v[k], tr=256 if k in COL_SHARDED else 128,
                                                                 name="adamw_" + k)
        return out_v[BIG[-1]]

    small_all = comm.all_reduced_small(grad_x, update_large)
    g_small = dict(zip(SMALL, _unpack(small_all, [grads[k].shape for k in SMALL])))
    cw_cols = w["conv_w"].shape[1]
    g_small["conv_w"] = lax.dynamic_slice_in_dim(g_small["conv_w"], dev * cw_cols, cw_cols, axis=1)
    for group in SMALL_GROUPS:
        gs = [g_small[k].reshape(raw_w[k].shape) for k in group]
        res = _adamw_whole(gs, [raw_w[k] for k in group], [raw_m[k] for k in group], [raw_v[k] for k in group],
                           name="adamw_" + group[0])
        for k, gk, (dk, mk, vk) in zip(group, gs, res):
            out_g[k], out_d[k], out_m[k], out_v[k] = gk, dk, mk, vk

    loss = lax.psum(loss_part, ("x", "y", "c"))
    return loss, grad_x, out_g, out_d, out_m, out_v


def kernel(x, w_in, conv_w, conv_b, rg_wa, rg_ba, rg_wx, rg_bx, rg_lambda, w_a_out, ssm_a_re, ssm_a_im, ssm_log_dt, ssm_b_re, ssm_b_im, ssm_c_re, ssm_c_im, ssm_d, glu_w, glu_v, w_out, ln1_g, ln1_b, mlp_w_up, mlp_b_up, mlp_w_down, mlp_b_down, ln2_g, ln2_b, loss_target, m_w_in, m_conv_w, m_conv_b, m_rg_wa, m_rg_ba, m_rg_wx, m_rg_bx, m_rg_lambda, m_w_a_out, m_ssm_a_re, m_ssm_a_im, m_ssm_log_dt, m_ssm_b_re, m_ssm_b_im, m_ssm_c_re, m_ssm_c_im, m_ssm_d, m_glu_w, m_glu_v, m_w_out, m_ln1_g, m_ln1_b, m_mlp_w_up, m_mlp_b_up, m_mlp_w_down, m_mlp_b_down, m_ln2_g, m_ln2_b, v_w_in, v_conv_w, v_conv_b, v_rg_wa, v_rg_ba, v_rg_wx, v_rg_bx, v_rg_lambda, v_w_a_out, v_ssm_a_re, v_ssm_a_im, v_ssm_log_dt, v_ssm_b_re, v_ssm_b_im, v_ssm_c_re, v_ssm_c_im, v_ssm_d, v_glu_w, v_glu_v, v_w_out, v_ln1_g, v_ln1_b, v_mlp_w_up, v_mlp_b_up, v_mlp_w_down, v_mlp_b_down, v_ln2_g, v_ln2_b):
    args = locals()
    w = {k: args[k][0] for k in ORDER}
    m = {k: args["m_" + k][0] for k in BIG}
    v = {k: args["v_" + k][0] for k in BIG}
    raw = [{k: args[prefix + k] for k in SMALL} for prefix in ("", "m_", "v_")]
    loss, grad_x, out_g, out_d, out_m, out_v = _step(x[0], loss_target[0], w, m, v, *raw)
    outs = [loss, grad_x[None]]
    for group in (out_g, out_d, out_m, out_v):
        outs += [group[k].reshape(args[k].shape) for k in ORDER]
    return tuple(outs)
```

```python
import functools
import math

import jax
import jax.numpy as jnp
from jax import lax
from jax.experimental import pallas as pl
from jax.experimental.pallas import tpu as pltpu

F32 = jnp.float32
BF16 = jnp.bfloat16
MESH = pl.DeviceIdType.MESH
N_DEV = 8
SUBLANES = 8
LANES = 128
VMEM_BYTES_V7X = 64 * 2 ** 20
VMEM_CAP = VMEM_BYTES_V7X - 8 * 2 ** 20

ALPHA = 2.0 ** 0.25
LN_EPS = 1e-5
RG_C = 8.0
ADAM_LR, ADAM_B1, ADAM_B2, ADAM_EPS, ADAM_WD, ADAM_STEP = 0.001, 0.9, 0.999, 1e-08, 0.01, 10
GELU_C = math.sqrt(2.0 / math.pi)
GELU_K = 0.044715

ANY = pl.BlockSpec(memory_space=pl.ANY)


def _params(sem, vmem_bytes):
    limit = int(min(max(2 * vmem_bytes, 16 * 2 ** 20), VMEM_CAP))
    return pltpu.CompilerParams(dimension_semantics=sem, vmem_limit_bytes=limit)


def _sig(x):
    return 1.0 / (1.0 + jnp.exp(-x))


def _gelu(x):
    return 0.5 * x * (1.0 + jnp.tanh(GELU_C * (x + GELU_K * x * x * x)))


def _dgelu(x):
    th = jnp.tanh(GELU_C * (x + GELU_K * x * x * x))
    return 0.5 * (1.0 + th) + 0.5 * x * (1.0 - th * th) * (GELU_C * (1.0 + 3.0 * GELU_K * x * x))


def _one_minus_exp(x, exp_half_x):
    p = x * (1.0 + x * (1 / 2 + x * (1 / 6 + x * (1 / 24 + x * (1 / 120)))))
    return jnp.where(x > -1 / 16, -p, 1.0 - exp_half_x * exp_half_x)


def _accumulate(ref, val, first):
    @pl.when(first)
    def _():
        ref[...] = val

    @pl.when(jnp.logical_not(first))
    def _():
        ref[...] += val


def _rows8(cw):
    return lax.broadcasted_iota(jnp.int32, (SUBLANES, cw), 0)


def _shift_down(cur, prev, s, rows):
    return jnp.where(rows < s, pltpu.roll(prev, s, 0), pltpu.roll(cur, s, 0))


def _shift_up(cur, nxt, s, rows):
    return jnp.where(rows < SUBLANES - s, pltpu.roll(cur, SUBLANES - s, 0), pltpu.roll(nxt, SUBLANES - s, 0))


def _mm(a, b, *, M, N, K, ta=False, tb=False, b_split=1, n_split=1, a_fn=None, extras=(), epilogue=None,
        n_out=1, n_cs=0, out_dtypes=None, tm=512, tn=512, tk=512, after=None, into=None, name):
    tm, tn, tk = min(tm, M), min(tn, N), min(tk, K)
    assert M % tm == 0 and N % tn == 0 and K % tk == 0, (name, M, N, K, tm, tn, tk)
    nk = K // tk
    grid = (N // tn, M // tm, nk)
    a_spec = pl.BlockSpec((tk, tm), lambda j, i, k: (k, i)) if ta else pl.BlockSpec((tm, tk), lambda j, i, k: (i, k))
    if b_split == 1:
        b_spec = pl.BlockSpec((tn, tk), lambda j, i, k: (j, k)) if tb else pl.BlockSpec((tk, tn), lambda j, i, k: (k, j))
    elif tb:
        kb = (K // b_split) // tk
        assert kb * tk * b_split == K, name
        b_spec = pl.BlockSpec((None, tn, tk), lambda j, i, k: (k // kb, j, k % kb))
    else:
        nb = (N // b_split) // tn
        assert nb * tn * b_split == N, name
        b_spec = pl.BlockSpec((None, tk, tn), lambda j, i, k: (j // nb, k, j % nb))
    in_specs = [a_spec, b_spec]
    for arr, kind, *col_off in extras:
        off = col_off[0] if col_off else 0
        in_specs.append(pl.BlockSpec((tm, tn), lambda j, i, k, off=off: (i, off + j)) if kind == "mn"
                        else pl.BlockSpec((1, tn), lambda j, i, k: (0, j)))
    out_dtypes = (F32,) * n_out if out_dtypes is None else out_dtypes
    if n_split == 1:
        out_shape = [jax.ShapeDtypeStruct((M, N), dt) for dt in out_dtypes]
        out_specs = [pl.BlockSpec((tm, tn), lambda j, i, k: (i, j)) for _ in range(n_out)]
    else:
        assert n_out == 1
        nbo = (N // n_split) // tn
        assert nbo * tn * n_split == N, name
        out_shape = [jax.ShapeDtypeStruct((n_split, M, N // n_split), out_dtypes[0])]
        out_specs = [pl.BlockSpec((None, tm, tn), lambda j, i, k: (j // nbo, i, j % nbo))]
    out_shape += [jax.ShapeDtypeStruct((1, N), F32) for _ in range(n_cs)]
    out_specs += [pl.BlockSpec((1, tn), lambda j, i, k: (0, j)) for _ in range(n_cs)]
    ne = len(extras)
    dims = (((0 if ta else 1,), (1 if tb else 0,)), ((), ()))

    n_after = 0 if after is None else 1
    in_specs += [ANY] * n_after
    aliases, tail = {}, [] if after is None else [after]
    if into is not None:
        buf, which, col_blk = into
        assert n_split == 1 and buf.shape[0] == M and buf.dtype == out_dtypes[which], name
        aliases = {len(in_specs): which}
        in_specs.append(ANY)
        tail.append(buf)
        out_shape[which] = jax.ShapeDtypeStruct(buf.shape, buf.dtype)
        out_specs[which] = pl.BlockSpec((tm, tn), lambda j, i, k: (i, col_blk + j))

    def body(*refs):
        a_ref, b_ref = refs[0], refs[1]
        ex_refs = refs[2:2 + ne]
        first_out = 2 + ne + len(tail)
        out_refs = refs[first_out:first_out + n_out]
        cs_refs = refs[first_out + n_out:first_out + n_out + n_cs]
        i, k = pl.program_id(1), pl.program_id(2)

        def product():
            av = a_ref[...]
            if a_fn is not None:
                av = a_fn(av.astype(F32))
            return lax.dot_general(av.astype(BF16), b_ref[...].astype(BF16), dims, preferred_element_type=F32)

        def finish(acc):
            res = (acc,) if epilogue is None else epilogue(acc, *[r[...] for r in ex_refs])
            for r, o in zip(out_refs, res[:n_out]):
                r[...] = o.astype(r.dtype)
            for r, cval in zip(cs_refs, res[n_out:]):
                _accumulate(r, jnp.sum(cval, axis=0, keepdims=True), i == 0)

        if nk == 1:
            finish(product())
            return
        acc_ref = refs[-1]

        @pl.when(k == 0)
        def _():
            acc_ref[...] = jnp.zeros_like(acc_ref)

        acc_ref[...] += product()

        @pl.when(k == nk - 1)
        def _():
            finish(acc_ref[...])

    vmem = 2 * tm * tk * a.dtype.itemsize + 2 * tk * tn * b.dtype.itemsize + (1 + 2 * n_out + 2 * ne + 2) * tm * tn * 4
    outs = pl.pallas_call(
        body, name=name, grid=grid, in_specs=in_specs, out_specs=out_specs, out_shape=out_shape,
        scratch_shapes=[pltpu.VMEM((tm, tn), F32)] if nk > 1 else [], input_output_aliases=aliases,
        compiler_params=_params(("parallel", "arbitrary", "arbitrary"), vmem),
    )(a, b, *[e[0] for e in extras], *tail)
    return outs[0] if len(outs) == 1 else outs


BD_STEP = 4

def _bd(pairs, *, T, J, kb, nb, tw=False, extras=(), epilogue=None, n_out=1, n_cs=0, out_dtypes=None, tm=1024, name):
    jb = BD_STEP
    assert T % tm == 0 and J % jb == 0
    grid = (J // jb, T // tm)
    npair, ne = len(pairs), len(extras)
    in_specs, args = [], []
    for arr, off, w in pairs:
        assert off % jb == 0, name
        in_specs.append(pl.BlockSpec((tm, jb * kb), lambda j, i, off=off // jb: (i, off + j)))
        in_specs.append(pl.BlockSpec((jb,) + tuple(w.shape[1:]), lambda j, i: (j, 0, 0)))
        args += [arr, w]
    for arr, kind, off in extras:
        assert off % jb == 0, name
        in_specs.append(pl.BlockSpec((tm, jb * nb), lambda j, i, off=off // jb: (i, off + j)) if kind == "tile"
                        else pl.BlockSpec((1, jb * nb), lambda j, i, off=off // jb: (0, off + j)))
        args.append(arr)
    out_dtypes = (F32,) * n_out if out_dtypes is None else out_dtypes
    out_shape = [jax.ShapeDtypeStruct((T, J * nb), dt) for dt in out_dtypes]
    out_specs = [pl.BlockSpec((tm, jb * nb), lambda j, i: (i, j)) for _ in range(n_out)]
    out_shape += [jax.ShapeDtypeStruct((1, J * nb), F32) for _ in range(n_cs)]
    out_specs += [pl.BlockSpec((1, jb * nb), lambda j, i: (0, j)) for _ in range(n_cs)]
    dims = (((1,), (1 if tw else 0,)), ((), ()))

    def body(*refs):
        ex_refs = refs[2 * npair:2 * npair + ne]
        out_refs = refs[2 * npair + ne:2 * npair + ne + n_out]
        cs_refs = refs[2 * npair + ne + n_out:]
        i = pl.program_id(1)
        for s in range(jb):
            cols_in, cols_out = pl.ds(s * kb, kb), pl.ds(s * nb, nb)
            acc = None
            for p in range(npair):
                d = lax.dot_general(refs[2 * p][:, cols_in].astype(BF16), refs[2 * p + 1][s].astype(BF16), dims,
                                    preferred_element_type=F32)
                acc = d if acc is None else acc + d
            res = (acc,) if epilogue is None else epilogue(acc, *[r[:, cols_out] for r in ex_refs])
            for r, o in zip(out_refs, res[:n_out]):
                r[:, cols_out] = o.astype(r.dtype)
            for r, cval in zip(cs_refs, res[n_out:]):
                _accumulate(r.at[:, cols_out], jnp.sum(cval, axis=0, keepdims=True), i == 0)

    vmem = jb * (2 * npair * tm * kb + 2 * npair * kb * nb + (2 * n_out + 2 * ne + 3) * tm * nb) * 4
    outs = pl.pallas_call(
        body, name=name, grid=grid, in_specs=in_specs, out_specs=out_specs, out_shape=out_shape,
        compiler_params=_params(("parallel", "arbitrary"), vmem),
    )(*args)
    return outs[0] if len(outs) == 1 else outs


def _bdw(a, a_off, b, b_off, *, T, J, kb, nb, tm=1024, name):
    jb = BD_STEP
    assert T % tm == 0 and J % jb == 0 and a_off % jb == 0 and b_off % jb == 0
    a_blk, b_blk = a_off // jb, b_off // jb

    def body(a_ref, b_ref, o_ref):
        i = pl.program_id(1)
        for s in range(jb):
            d = lax.dot_general(a_ref[:, pl.ds(s * kb, kb)].astype(BF16), b_ref[:, pl.ds(s * nb, nb)].astype(BF16),
                                (((0,), (0,)), ((), ())), preferred_element_type=F32)
            _accumulate(o_ref.at[s], d, i == 0)

    return pl.pallas_call(
        body, name=name, grid=(J // jb, T // tm),
        in_specs=[pl.BlockSpec((tm, jb * kb), lambda j, i: (i, a_blk + j)),
                  pl.BlockSpec((tm, jb * nb), lambda j, i: (i, b_blk + j))],
        out_specs=pl.BlockSpec((jb, kb, nb), lambda j, i: (j, 0, 0)),
        out_shape=jax.ShapeDtypeStruct((J, kb, nb), F32),
        compiler_params=_params(("parallel", "arbitrary"), jb * (2 * tm * (kb + nb) + 3 * kb * nb) * 4),
    )(a, b)


def _bd_pack(w, q):
    g, a, b = w.shape
    eye = jnp.eye(q, dtype=w.dtype)
    return jnp.einsum("jqab,qr->jqarb", w.reshape(g // q, q, a, b), eye).reshape(g // q, q * a, q * b)


def _bd_unpack(wp, q):
    j, qa, qb = wp.shape
    a, b = qa // q, qb // q
    w5 = wp.reshape(j, q, a, q, b)
    return jnp.stack([w5[:, r, :, r, :] for r in range(q)], axis=1).reshape(j * q, a, b)


def _ew(fn, ins, *, T, C, n_out, n_cs=0, out_dtypes=None, tm=256, cw=None, name):
    cw = C if cw is None else cw
    assert T % tm == 0 and C % cw == 0
    grid = (C // cw, T // tm)
    in_specs = []
    for arr, kind, off in ins:
        in_specs.append(pl.BlockSpec((tm, cw), lambda j, i, off=off: (i, off + j)) if kind == "tile"
                        else pl.BlockSpec((arr.shape[0], cw), lambda j, i, off=off: (0, off + j)))
    out_dtypes = (F32,) * n_out if out_dtypes is None else out_dtypes
    out_shape = [jax.ShapeDtypeStruct((T, C), dt) for dt in out_dtypes]
    out_specs = [pl.BlockSpec((tm, cw), lambda j, i: (i, j)) for _ in range(n_out)]
    out_shape += [jax.ShapeDtypeStruct((1, C), F32) for _ in range(n_cs)]
    out_specs += [pl.BlockSpec((1, cw), lambda j, i: (0, j)) for _ in range(n_cs)]
    nin = len(ins)

    def body(*refs):
        i = pl.program_id(1)
        res = fn(*[r[...].astype(F32) for r in refs[:nin]])
        for r, o in zip(refs[nin:nin + n_out], res[:n_out]):
            r[...] = o.astype(r.dtype)
        for r, cval in zip(refs[nin + n_out:], res[n_out:]):
            _accumulate(r, jnp.sum(cval, axis=0, keepdims=True), i == 0)

    vmem = (2 * nin + 2 * n_out + 6) * tm * cw * 4
    outs = pl.pallas_call(
        body, name=name, grid=grid, in_specs=in_specs, out_specs=out_specs, out_shape=out_shape,
        compiler_params=_params(("parallel", "arbitrary"), vmem),
    )(*[arr for arr, _, _ in ins])
    return outs[0] if len(outs) == 1 else outs


def _ln_stats(s):
    mu = jnp.mean(s, axis=-1, keepdims=True)
    d = s - mu
    var = jnp.mean(d * d, axis=-1, keepdims=True)
    rstd = lax.rsqrt(var + LN_EPS)
    return d * rstd, rstd


def _ln_bwd(dy, g, xhat, rstd):
    dxh = dy * g
    m1 = jnp.mean(dxh, axis=-1, keepdims=True)
    m2 = jnp.mean(dxh * xhat, axis=-1, keepdims=True)
    return rstd * (dxh - m1 - xhat * m2)


def _conv_fwd(z, conv_w, conv_b, *, T, C, tm=2048, cw=1024, after=None, name):
    ng, hb = tm // SUBLANES, tm // SUBLANES
    n_after = 0 if after is None else 1

    def body(x_ref, halo_ref, w_ref, b_ref, *rest):
        o_ref = rest[-1]
        it = pl.program_id(1)
        rows = _rows8(cw)
        halo = jnp.where(it == 0, 0.0, halo_ref[...])
        w = w_ref[...]
        bias = b_ref[...]

        def group(g, carry):
            off = pl.multiple_of(g * SUBLANES, SUBLANES)
            cur = x_ref[pl.ds(off, SUBLANES), :]
            prev = x_ref[pl.ds(pl.multiple_of(jnp.maximum(off - SUBLANES, 0), SUBLANES), SUBLANES), :]
            prev = jnp.where(g == 0, halo, prev)
            acc = cur * w[3:4] + bias
            for s in (1, 2, 3):
                acc = acc + _shift_down(cur, prev, s, rows) * w[3 - s:4 - s]
            o_ref[pl.ds(off, SUBLANES), :] = acc
            return carry

        lax.fori_loop(0, ng, group, 0, unroll=2)

    return pl.pallas_call(
        body, name=name, grid=(C // cw, T // tm),
        in_specs=[pl.BlockSpec((tm, cw), lambda j, i: (i, j)),
                  pl.BlockSpec((SUBLANES, cw), lambda j, i: (jnp.maximum(i * hb - 1, 0), j)),
                  pl.BlockSpec((4, cw), lambda j, i: (0, j)), pl.BlockSpec((1, cw), lambda j, i: (0, j))]
        + [ANY] * n_after,
        out_specs=pl.BlockSpec((tm, cw), lambda j, i: (i, j)),
        out_shape=jax.ShapeDtypeStruct((T, C), F32),
        compiler_params=_params(("parallel", "arbitrary"), 5 * tm * cw * 4),
    )(z, z, conv_w, conv_b, *([after] if n_after else []))


def _conv_bwd(dxc, z, conv_w, dz, *, T, C, tm=2048, cw=512, name):
    ng, hb, last = tm // SUBLANES, tm // SUBLANES, T // SUBLANES - 1
    nt = T // tm
    rows16 = 2 * SUBLANES

    def body(d_ref, dn_ref, x_ref, w_ref, dz_in_ref, o_ref, sums_ref):
        it = pl.program_id(1)
        rows = _rows8(cw)
        dnext = jnp.where(it == nt - 1, 0.0, dn_ref[...])
        w = w_ref[...]

        def pair(q, accs):
            base = pl.multiple_of(q * rows16, rows16)
            halves = []
            for half in range(2):
                g = 2 * q + half
                off = pl.multiple_of(base + half * SUBLANES, SUBLANES)
                dcur = d_ref[pl.ds(off, SUBLANES), :]
                dnx = d_ref[pl.ds(pl.multiple_of(jnp.minimum(off + SUBLANES, tm - SUBLANES), SUBLANES), SUBLANES), :]
                dnx = jnp.where(g == ng - 1, dnext, dnx)
                xcur = x_ref[pl.ds(off, SUBLANES), :]
                acc = dcur * w[3:4]
                taps = [accs[3] + dcur * xcur]
                for s in (1, 2, 3):
                    ahead = _shift_up(dcur, dnx, s, rows)
                    acc = acc + ahead * w[3 - s:4 - s]
                    taps.append(accs[3 - s] + ahead * xcur)
                halves.append(acc)
                accs = (taps[3], taps[2], taps[1], taps[0], accs[4] + dcur)
            o_ref[pl.ds(base, rows16), :] = jnp.concatenate(halves, axis=0).astype(o_ref.dtype)
            return accs

        zero = jnp.zeros((SUBLANES, cw), F32)
        accs = lax.fori_loop(0, ng // 2, pair, (zero,) * 5)
        sums = jnp.zeros((SUBLANES, cw), F32)
        for k, a in enumerate(accs):
            sums = jnp.where(rows == k, jnp.sum(a, axis=0, keepdims=True), sums)
        _accumulate(sums_ref, sums, it == 0)

    tile = pl.BlockSpec((tm, cw), lambda j, i: (i, j))
    return pl.pallas_call(
        body, name=name, grid=(C // cw, nt),
        in_specs=[tile, pl.BlockSpec((SUBLANES, cw), lambda j, i: (jnp.minimum((i + 1) * hb, last), j)),
                  tile, pl.BlockSpec((4, cw), lambda j, i: (0, j)), ANY],
        out_specs=[tile, pl.BlockSpec((SUBLANES, cw), lambda j, i: (0, j))],
        input_output_aliases={4: 0},
        out_shape=[jax.ShapeDtypeStruct(dz.shape, dz.dtype), jax.ShapeDtypeStruct((SUBLANES, C), F32)],
        compiler_params=_params(("parallel", "arbitrary"), 7 * tm * cw * 4),
    )(dxc, dxc, z, conv_w, dz)


def _rg_coeffs(r, ig, xc, sp):
    la = (-RG_C) * r * sp
    a = jnp.exp(la)
    m = jnp.sqrt(_one_minus_exp(2.0 * la, a))
    return a, m, m * (ig * xc)


def _rg_scan_fwd(z, ri, xc, sp, *, T, C, gate_off, tm=2048, cw=256, name):
    rows16 = 2 * SUBLANES
    nq = tm // rows16

    def body(gate_ref, r_ref, i_ref, xc_ref, sp_ref, h_ref, p_ref, a_ref, m_ref, carry_ref):
        it = pl.program_id(1)

        @pl.when(it == 0)
        def _():
            carry_ref[...] = jnp.zeros_like(carry_ref)

        rows = _rows8(cw)
        sp_row = sp_ref[...]

        def pair(q, carry):
            base = pl.multiple_of(q * rows16, rows16)
            halves = []
            for half in range(2):
                sl = pl.ds(pl.multiple_of(base + half * SUBLANES, SUBLANES), SUBLANES)
                a, m, b = _rg_coeffs(r_ref[sl, :], i_ref[sl, :], xc_ref[sl, :], sp_row)
                a_ref[sl, :] = a
                m_ref[sl, :] = m
                for s in (1, 2, 4):
                    keep = rows >= s
                    sa = jnp.where(keep, pltpu.roll(a, s, 0), 1.0)
                    sb = jnp.where(keep, pltpu.roll(b, s, 0), 0.0)
                    b = b + a * sb
                    a = a * sa
                h = b + a * carry
                h_ref[sl, :] = h
                halves.append(h * _gelu(gate_ref[sl, :]))
                carry = h[SUBLANES - 1:SUBLANES, :]
            p_ref[pl.ds(base, rows16), :] = jnp.concatenate(halves, axis=0).astype(p_ref.dtype)
            return carry

        last = lax.fori_loop(0, nq, pair, carry_ref[0:1, :], unroll=2)
        carry_ref[...] = jnp.broadcast_to(last, carry_ref.shape)

    tile = pl.BlockSpec((tm, cw), lambda j, i: (i, j))
    gate_blk = gate_off // cw
    return pl.pallas_call(
        body, name=name, grid=(C // cw, T // tm),
        in_specs=[pl.BlockSpec((tm, cw), lambda j, i: (i, gate_blk + j)),
                  pl.BlockSpec((tm, cw), lambda j, i: (i, 2 * j)), pl.BlockSpec((tm, cw), lambda j, i: (i, 2 * j + 1)),
                  tile, pl.BlockSpec((1, cw), lambda j, i: (0, j))],
        out_specs=[tile, tile, tile, tile],
        out_shape=[jax.ShapeDtypeStruct((T, C), F32), jax.ShapeDtypeStruct((T, C), BF16),
                   jax.ShapeDtypeStruct((T, C), F32), jax.ShapeDtypeStruct((T, C), F32)],
        scratch_shapes=[pltpu.VMEM((SUBLANES, cw), F32)],
        compiler_params=_params(("parallel", "arbitrary"), 16 * tm * cw * 4),
    )(z, ri, ri, xc, sp)


def _rg_scan_bwd(dh, h, ri, xc, a_fwd, m_fwd, sp, *, T, C, tm=1024, cw=256, name):
    ng, hb, nt = tm // SUBLANES, tm // SUBLANES, T // tm

    def body(dh_ref, h_ref, hp_ref, r_ref, i_ref, xc_ref, a_ref, m_ref, sp_ref,
             drai_ref, dxc_ref, crai_ref, csp_ref, cg_ref, ca_ref):
        step = pl.program_id(1)

        @pl.when(step == 0)
        def _():
            cg_ref[...] = jnp.zeros_like(cg_ref)
            ca_ref[...] = jnp.zeros_like(ca_ref)

        rows = _rows8(cw)
        sp_row = sp_ref[...]
        hhalo = jnp.where(step == nt - 1, 0.0, hp_ref[...])

        def group(gi, carry):
            g_next, a_next, s_ra, s_ia, s_sp = carry
            g = ng - 1 - gi
            off = pl.multiple_of(g * SUBLANES, SUBLANES)
            sl = pl.ds(off, SUBLANES)
            rr, ii, xx = r_ref[sl, :], i_ref[sl, :], xc_ref[sl, :]
            a, m = a_ref[sl, :], m_ref[sl, :]
            hh = h_ref[sl, :]
            hpv = h_ref[pl.ds(pl.multiple_of(jnp.maximum(off - SUBLANES, 0), SUBLANES), SUBLANES), :]
            hpv = jnp.where(g == 0, hhalo, hpv)
            hprev = _shift_down(hh, hpv, 1, rows)
            d = dh_ref[sl, :]
            c = jnp.where(rows < SUBLANES - 1, pltpu.roll(a, SUBLANES - 1, 0), a_next)
            for s in (1, 2, 4):
                keep = rows < SUBLANES - s
                sc = jnp.where(keep, pltpu.roll(c, SUBLANES - s, 0), 1.0)
                sd = jnp.where(keep, pltpu.roll(d, SUBLANES - s, 0), 0.0)
                d = d + c * sd
                c = c * sc
            gg = d + c * g_next
            da = gg * hprev
            dm = gg * (ii * xx)
            di = gg * (m * xx)
            dxc_ref[sl, :] = gg * (m * ii)
            dla = da * a - dm * (a * a / m)
            dra = dla * ((-RG_C) * sp_row) * (rr * (1.0 - rr))
            dia = di * (ii * (1.0 - ii))
            drai_ref[sl, pl.ds(0, cw)] = dra
            drai_ref[sl, pl.ds(cw, cw)] = dia
            return (gg[0:1, :], a[0:1, :], s_ra + dra, s_ia + dia, s_sp + dla * ((-RG_C) * rr))

        zero = jnp.zeros((SUBLANES, cw), F32)
        g_first, a_first, s_ra, s_ia, s_sp = lax.fori_loop(
            0, ng, group, (cg_ref[0:1, :], ca_ref[0:1, :], zero, zero, zero), unroll=2)
        cg_ref[...] = jnp.broadcast_to(g_first, cg_ref.shape)
        ca_ref[...] = jnp.broadcast_to(a_first, ca_ref.shape)
        for ref, acc in ((crai_ref.at[:, pl.ds(0, cw)], s_ra), (crai_ref.at[:, pl.ds(cw, cw)], s_ia), (csp_ref, s_sp)):
            _accumulate(ref, jnp.sum(acc, axis=0, keepdims=True), step == 0)

    tile = pl.BlockSpec((tm, cw), lambda j, i: (nt - 1 - i, j))
    wide = pl.BlockSpec((tm, 2 * cw), lambda j, i: (nt - 1 - i, j))
    vec = pl.BlockSpec((1, cw), lambda j, i: (0, j))
    return pl.pallas_call(
        body, name=name, grid=(C // cw, nt),
        in_specs=[tile, tile, pl.BlockSpec((SUBLANES, cw), lambda j, i: (jnp.maximum((nt - 1 - i) * hb - 1, 0), j)),
                  pl.BlockSpec((tm, cw), lambda j, i: (nt - 1 - i, 2 * j)),
                  pl.BlockSpec((tm, cw), lambda j, i: (nt - 1 - i, 2 * j + 1)), tile, tile, tile, vec],
        out_specs=[wide, tile, pl.BlockSpec((1, 2 * cw), lambda j, i: (0, j)), vec],
        out_shape=[jax.ShapeDtypeStruct((T, 2 * C), F32), jax.ShapeDtypeStruct((T, C), F32),
                   jax.ShapeDtypeStruct((1, 2 * C), F32), jax.ShapeDtypeStruct((1, C), F32)],
        scratch_shapes=[pltpu.VMEM((SUBLANES, cw), F32), pltpu.VMEM((SUBLANES, cw), F32)],
        compiler_params=_params(("parallel", "arbitrary"), 24 * tm * cw * 4),
    )(dh, h, h, ri, ri, xc, a_fwd, m_fwd, sp)


def _cscan_tables(lr, li, reverse):
    lam = (lr.reshape(-1), -li.reshape(-1) if reverse else li.reshape(-1))

    def mul(p, q):
        return p[0] * q[0] - p[1] * q[1], p[0] * q[1] + p[1] * q[0]

    pows = [lam]
    for _ in range(SUBLANES - 1):
        pows.append(mul(pows[-1], lam))
    zero = jnp.zeros_like(lam[0])
    tab = jnp.stack([pows[0][0], pows[0][1], pows[1][0], pows[1][1], pows[3][0], pows[3][1], zero, zero])
    if reverse:
        pows = pows[::-1]
    return tab, jnp.stack([p[0] for p in pows]), jnp.stack([p[1] for p in pows])


def _power_slabs(lr, li, n):
    pr, pi = lr.reshape(1, -1), li.reshape(1, -1)
    while pr.shape[0] < n:
        tr, ti = pr[-1:], pi[-1:]
        pr, pi = (jnp.concatenate([pr, pr * tr - pi * ti], axis=0), jnp.concatenate([pi, pr * ti + pi * tr], axis=0))
    return jnp.repeat(pr, SUBLANES, axis=0), jnp.repeat(pi, SUBLANES, axis=0), pr[-1], pi[-1]


def _rows_to_segments(src_ref, dst_ref):
    seg = src_ref.shape[0] // SUBLANES
    for g in range(seg):
        dst_ref[pl.ds(g * SUBLANES, SUBLANES), :] = src_ref[pl.ds(g, SUBLANES, stride=seg), :].astype(dst_ref.dtype)


def _segments_to_rows(src_ref, dst_ref):
    seg = src_ref.shape[0] // SUBLANES
    for r in range(SUBLANES):
        dst_ref[pl.ds(r * seg, seg), :] = src_ref[pl.ds(r, seg, stride=SUBLANES), :].astype(dst_ref.dtype)


def _seg_scan_tile(xr_ref, xi_ref, pbr_ref, pbi_ref, tab_ref, pwr_ref, pwi_ref, cr_ref, ci_ref, *, reverse, h=None):
    tm, cw = xr_ref.shape
    seg = tm // SUBLANES
    rows = _rows8(cw)
    sign = -1.0 if reverse else 1.0
    l_re, l_im = pbr_ref[0:1, :], sign * pbi_ref[0:1, :]

    def slab(g):
        return pl.ds(pl.multiple_of(g * SUBLANES, SUBLANES), SUBLANES)

    def local(k, state):
        sl = slab(seg - 1 - k if reverse else k)
        sr, si = state
        nr = xr_ref[sl, :] + (l_re * sr - l_im * si)
        ni = xi_ref[sl, :] + (l_re * si + l_im * sr)
        xr_ref[sl, :] = nr
        xi_ref[sl, :] = ni
        return nr, ni

    zero = jnp.zeros((SUBLANES, cw), F32)
    er, ei = lax.fori_loop(0, seg, local, (zero, zero), unroll=2)

    for k, s in enumerate((1, 2, 4)):
        shift = SUBLANES - s if reverse else s
        keep = rows < SUBLANES - s if reverse else rows >= s
        sr = jnp.where(keep, pltpu.roll(er, shift, 0), 0.0)
        si = jnp.where(keep, pltpu.roll(ei, shift, 0), 0.0)
        m_re, m_im = tab_ref[2 * k:2 * k + 1, :], tab_ref[2 * k + 1:2 * k + 2, :]
        er, ei = er + (m_re * sr - m_im * si), ei + (m_re * si + m_im * sr)
    cin_r, cin_i = cr_ref[0:1, :], ci_ref[0:1, :]
    pwr, pwi = pwr_ref[...], pwi_ref[...]
    er, ei = er + (pwr * cin_r - pwi * cin_i), ei + (pwr * cin_i + pwi * cin_r)
    if reverse:
        ent_r = jnp.where(rows == SUBLANES - 1, cin_r, pltpu.roll(er, SUBLANES - 1, 0))
        ent_i = jnp.where(rows == SUBLANES - 1, cin_i, pltpu.roll(ei, SUBLANES - 1, 0))
        out_r, out_i = er[0:1, :], ei[0:1, :]
    else:
        ent_r = jnp.where(rows == 0, cin_r, pltpu.roll(er, 1, 0))
        ent_i = jnp.where(rows == 0, cin_i, pltpu.roll(ei, 1, 0))
        out_r, out_i = er[SUBLANES - 1:SUBLANES, :], ei[SUBLANES - 1:SUBLANES, :]
    cr_ref[...] = jnp.broadcast_to(out_r, cr_ref.shape)
    ci_ref[...] = jnp.broadcast_to(out_i, ci_ref.shape)

    if h is not None:
        hr_ref, hi_ref, hr_last, hi_last = h
        hr_wrap = _shift_down(hr_ref[pl.ds(tm - SUBLANES, SUBLANES), :], hr_last, 1, rows)
        hi_wrap = _shift_down(hi_ref[pl.ds(tm - SUBLANES, SUBLANES), :], hi_last, 1, rows)

    def fix(g, sums):
        sl = slab(g)
        power = slab(seg - 1 - g) if reverse else sl
        pr, pi = pbr_ref[power, :], sign * pbi_ref[power, :]
        nr = xr_ref[sl, :] + (pr * ent_r - pi * ent_i)
        ni = xi_ref[sl, :] + (pr * ent_i + pi * ent_r)
        xr_ref[sl, :] = nr
        xi_ref[sl, :] = ni
        if h is None:
            return sums
        before = slab(jnp.maximum(g - 1, 0))
        hr1 = jnp.where(g == 0, hr_wrap, hr_ref[before, :])
        hi1 = jnp.where(g == 0, hi_wrap, hi_ref[before, :])
        return sums[0] + (nr * hr1 + ni * hi1), sums[1] + (ni * hr1 - nr * hi1)

    return lax.fori_loop(0, seg, fix, (zero, zero) if h is not None else (), unroll=2)


S5_TILE = 2048


def _s5_fwd(z, u_off, wb_re, wb_im, wc_re, wc_im_neg, d_row, powers, *, T, tm=S5_TILE, name):
    J, ku, kp = wb_re.shape
    nt = T // tm
    pb_re, pb_im, top_re, top_im = powers
    tab, pw_re, pw_im = _cscan_tables(top_re, top_im, False)
    u_blk = u_off // ku

    def body(u_ref, wbr_ref, wbi_ref, wcr_ref, wci_ref, d_ref, pbr_ref, pbi_ref, tab_ref, pwr_ref, pwi_ref,
             hr_ref, hi_ref, y_ref, yg_ref, cr_ref, ci_ref, us_ref, ys_ref):
        @pl.when(pl.program_id(1) == 0)
        def _():
            cr_ref[...] = jnp.zeros_like(cr_ref)
            ci_ref[...] = jnp.zeros_like(ci_ref)

        _rows_to_segments(u_ref, us_ref)
        u = us_ref[...]
        ub = u.astype(BF16)
        hr_ref[...] = jnp.dot(ub, wbr_ref[...], preferred_element_type=F32)
        hi_ref[...] = jnp.dot(ub, wbi_ref[...], preferred_element_type=F32)
        _seg_scan_tile(hr_ref, hi_ref, pbr_ref, pbi_ref, tab_ref, pwr_ref, pwi_ref, cr_ref, ci_ref, reverse=False)
        y = (jnp.dot(hr_ref[...].astype(BF16), wcr_ref[...], preferred_element_type=F32)
             + jnp.dot(hi_ref[...].astype(BF16), wci_ref[...], preferred_element_type=F32) + d_ref[...] * u)
        ys_ref[...] = y
        _segments_to_rows(ys_ref, y_ref)
        ys_ref[...] = _gelu(y)
        _segments_to_rows(ys_ref, yg_ref)

    wb_spec = pl.BlockSpec((None, ku, kp), lambda j, i: (j, 0, 0))
    wc_spec = pl.BlockSpec((None, kp, ku), lambda j, i: (j, 0, 0))
    small = pl.BlockSpec((SUBLANES, kp), lambda j, i: (0, j))
    slabs = pl.BlockSpec((tm, kp), lambda j, i: (0, j))
    state = pl.BlockSpec((tm, kp), lambda j, i: (i, j))
    chan = pl.BlockSpec((tm, ku), lambda j, i: (i, j))
    return pl.pallas_call(
        body, name=name, grid=(J, nt),
        in_specs=[pl.BlockSpec((tm, ku), lambda j, i: (i, u_blk + j)), wb_spec, wb_spec, wc_spec, wc_spec,
                  pl.BlockSpec((1, ku), lambda j, i: (0, j)), slabs, slabs, small, small, small],
        out_specs=[state, state, chan, chan],
        out_shape=[jax.ShapeDtypeStruct((T, J * kp), F32)] * 2
        + [jax.ShapeDtypeStruct((T, J * ku), F32), jax.ShapeDtypeStruct((T, J * ku), BF16)],
        scratch_shapes=[pltpu.VMEM((SUBLANES, kp), F32), pltpu.VMEM((SUBLANES, kp), F32),
                        pltpu.VMEM((tm, ku), F32), pltpu.VMEM((tm, ku), F32)],
        compiler_params=_params(("parallel", "arbitrary"), 14 * tm * kp * 4),
    )(z, wb_re, wb_im, wc_re, wc_im_neg, d_row, pb_re, pb_im, tab, pw_re, pw_im)


def _s5_bwd(dy, z, u_off, h_re, h_im, wb_re, wb_im, wc_re, wc_im_neg, d_row, powers, dz, *, T, tm=S5_TILE, name):
    J, ku, kp = wb_re.shape
    nt, hb = T // tm, tm // SUBLANES
    pb_re, pb_im, top_re, top_im = powers
    tab, pw_re, pw_im = _cscan_tables(top_re, top_im, True)
    u_blk = u_off // ku
    contract_rows = (((0,), (0,)), ((), ()))
    contract_cols = (((1,), (1,)), ((), ()))

    def body(dy_ref, u_ref, hr_ref, hrp_ref, hi_ref, hip_ref, wbr_ref, wbi_ref, wcr_ref, wci_ref, d_ref,
             pbr_ref, pbi_ref, tab_ref, pwr_ref, pwi_ref, dz_in_ref,
             du_ref, dlr_ref, dli_ref, dd_ref, dwbr_ref, dwbi_ref, dwcr_ref, dwci_ref,
             gr_ref, gi_ref, cr_ref, ci_ref, dys_ref, us_ref):
        step = pl.program_id(1)
        first = step == 0

        @pl.when(first)
        def _():
            cr_ref[...] = jnp.zeros_like(cr_ref)
            ci_ref[...] = jnp.zeros_like(ci_ref)

        _rows_to_segments(dy_ref, dys_ref)
        _rows_to_segments(u_ref, us_ref)
        dy_t, u = dys_ref[...], us_ref[...]
        dyb, ub = dy_t.astype(BF16), u.astype(BF16)
        gr_ref[...] = lax.dot_general(dyb, wcr_ref[...], contract_cols, preferred_element_type=F32)
        gi_ref[...] = lax.dot_general(dyb, wci_ref[...], contract_cols, preferred_element_type=F32)
        hr_last = jnp.where(step == nt - 1, 0.0, hrp_ref[...])
        hi_last = jnp.where(step == nt - 1, 0.0, hip_ref[...])
        s_re, s_im = _seg_scan_tile(gr_ref, gi_ref, pbr_ref, pbi_ref, tab_ref, pwr_ref, pwi_ref, cr_ref, ci_ref,
                                    reverse=True, h=(hr_ref, hi_ref, hr_last, hi_last))
        _accumulate(dlr_ref, jnp.sum(s_re, axis=0, keepdims=True), first)
        _accumulate(dli_ref, jnp.sum(s_im, axis=0, keepdims=True), first)
        grb, gib = gr_ref[...].astype(BF16), gi_ref[...].astype(BF16)
        du = (lax.dot_general(grb, wbr_ref[...], contract_cols, preferred_element_type=F32)
              + lax.dot_general(gib, wbi_ref[...], contract_cols, preferred_element_type=F32) + dy_t * d_ref[...])
        dys_ref[...] = du
        _segments_to_rows(dys_ref, du_ref)
        _accumulate(dd_ref, jnp.sum(dy_t * u, axis=0, keepdims=True), first)
        _accumulate(dwbr_ref, lax.dot_general(ub, grb, contract_rows, preferred_element_type=F32), first)
        _accumulate(dwbi_ref, lax.dot_general(ub, gib, contract_rows, preferred_element_type=F32), first)
        _accumulate(dwcr_ref, lax.dot_general(dyb, hr_ref[...].astype(BF16), contract_rows,
                                              preferred_element_type=F32), first)
        _accumulate(dwci_ref, lax.dot_general(dyb, hi_ref[...].astype(BF16), contract_rows,
                                              preferred_element_type=F32), first)

    def tix(i):
        return nt - 1 - i

    wb_spec = pl.BlockSpec((None, ku, kp), lambda j, i: (j, 0, 0))
    wc_spec = pl.BlockSpec((None, kp, ku), lambda j, i: (j, 0, 0))
    small = pl.BlockSpec((SUBLANES, kp), lambda j, i: (0, j))
    state = pl.BlockSpec((tm, kp), lambda j, i: (tix(i), j))
    halo = pl.BlockSpec((SUBLANES, kp), lambda j, i: (jnp.maximum(tix(i) * hb - 1, 0), j))
    chan = pl.BlockSpec((tm, ku), lambda j, i: (tix(i), j))
    svec = pl.BlockSpec((1, kp), lambda j, i: (0, j))
    cvec = pl.BlockSpec((1, ku), lambda j, i: (0, j))
    slabs = pl.BlockSpec((tm, kp), lambda j, i: (0, j))
    return pl.pallas_call(
        body, name=name, grid=(J, nt),
        in_specs=[chan, pl.BlockSpec((tm, ku), lambda j, i: (tix(i), u_blk + j)), state, halo, state, halo,
                  wb_spec, wb_spec, wc_spec, wc_spec, cvec, slabs, slabs, small, small, small, ANY],
        out_specs=[pl.BlockSpec((tm, ku), lambda j, i: (tix(i), u_blk + j)), svec, svec, cvec,
                   wb_spec, wb_spec, wb_spec, wb_spec],
        input_output_aliases={16: 0},
        out_shape=[jax.ShapeDtypeStruct(dz.shape, dz.dtype), jax.ShapeDtypeStruct((1, J * kp), F32),
                   jax.ShapeDtypeStruct((1, J * kp), F32), jax.ShapeDtypeStruct((1, J * ku), F32),
                   jax.ShapeDtypeStruct((J, ku, kp), F32), jax.ShapeDtypeStruct((J, ku, kp), F32),
                   jax.ShapeDtypeStruct((J, ku, kp), F32), jax.ShapeDtypeStruct((J, ku, kp), F32)],
        scratch_shapes=[pltpu.VMEM((tm, kp), F32), pltpu.VMEM((tm, kp), F32),
                        pltpu.VMEM((SUBLANES, kp), F32), pltpu.VMEM((SUBLANES, kp), F32),
                        pltpu.VMEM((tm, ku), F32), pltpu.VMEM((tm, ku), F32)],
        compiler_params=_params(("parallel", "arbitrary"), 16 * tm * kp * 4),
    )(dy, z, h_re, h_re, h_im, h_im, wb_re, wb_im, wc_re, wc_im_neg, d_row, pb_re, pb_im, tab, pw_re, pw_im, dz)


def _mesh_pos():
    return lax.axis_index("x"), lax.axis_index("y"), lax.axis_index("c")


def _dev_index(px, py, pc):
    return 4 * px + 2 * py + pc


HBM = pl.BlockSpec(memory_space=pltpu.HBM)
SEM = pl.BlockSpec(memory_space=pltpu.SEMAPHORE)
EFFECT = pltpu.SideEffectType.DATAFLOW_SIDE_EFFECTING
RELATIONS = [(dx, dy, dc) for dx in (0, 1) for dy in (0, 1) for dc in (0, 1) if (dx, dy, dc) != (0, 0, 0)]


def _peer(rel):
    x, y, c = _mesh_pos()
    dx, dy, dc = rel
    return (x + dx - 2 * x * dx, y + dy - 2 * y * dy, c + dc - 2 * c * dc)


CHIP_RELATIONS = [(1, 0, 0), (0, 1, 0), (1, 1, 0)]
EXCHANGE_PEERS = {"gather": RELATIONS, "scatter": RELATIONS, "own": [(0, 0, 1)] + CHIP_RELATIONS, "pass": CHIP_RELATIONS}


def _split_copy(src_ref, land_ref, send_sems, recv_sems, k, mode, incoming):
    x, y, c = _mesh_pos()
    me = _dev_index(x, y, c)
    peer = _peer(EXCHANGE_PEERS[mode][k])
    if mode == "pass":
        held, theirs = _dev_index(peer[0], peer[1], c), _dev_index(peer[0], peer[1], 1 - c)
        src, slot, target = land_ref.at[held], theirs if incoming else held, (x, y, 1 - c)
    else:
        src = src_ref.at[_dev_index(*peer)] if mode == "scatter" else src_ref
        slot, target = _dev_index(*peer) if incoming else me, peer
    return pltpu.make_async_remote_copy(src_ref=src, dst_ref=land_ref.at[slot], send_sem=send_sems.at[k],
                                        recv_sem=recv_sems.at[k], device_id=target, device_id_type=MESH)


def _exchange_start(srcs, lands, *, mode, after=None, name):
    n = len(srcs)
    n_after = 0 if after is None else 1
    n_rel = len(EXCHANGE_PEERS[mode])

    def body(*refs):
        src_refs, land_refs = refs[:n], refs[n:2 * n]
        first_out = 2 * n + n_after
        send, recv = refs[first_out:first_out + n], refs[first_out + n:first_out + 2 * n]
        token = refs[-1]
        for k in range(n_rel):
            for a in range(n):
                _split_copy(src_refs[a], land_refs[a], send[a], recv[a], k, mode, incoming=False).start()
        token[...] = jnp.zeros_like(token)

    outs = pl.pallas_call(
        body, name=name, in_specs=[HBM] * (2 * n) + [ANY] * n_after,
        out_shape=[pltpu.SemaphoreType.DMA((n_rel,))] * (2 * n)
        + [pltpu.HBM(s.shape, s.dtype) for s in srcs] + [pltpu.HBM(s.shape, s.dtype) for s in lands]
        + [jax.ShapeDtypeStruct((SUBLANES, LANES), F32)],
        out_specs=[SEM] * (2 * n) + [HBM] * (2 * n) + [pl.BlockSpec(memory_space=pltpu.VMEM)],
        input_output_aliases={**{a: 2 * n + a for a in range(n)}, **{n + a: 3 * n + a for a in range(n)}},
        compiler_params=pltpu.CompilerParams(has_side_effects=EFFECT),
    )(*[pltpu.with_memory_space_constraint(s, pltpu.HBM) for s in srcs],
      *[pltpu.with_memory_space_constraint(s, pltpu.HBM) for s in lands], *([after] if n_after else []))
    per_array = [(outs[a], outs[n + a], outs[2 * n + a], outs[3 * n + a]) for a in range(n)]
    return per_array, outs[-1]


def _exchange_wait(handle, after, *, mode, name):
    send_sems, recv_sems, src_thru, land_thru = handle
    after = after if isinstance(after, (tuple, list)) else (after,)

    def body(src_ref, land_ref, send, recv, *rest):
        for k in range(len(EXCHANGE_PEERS[mode])):
            cp = _split_copy(src_ref, land_ref, send, recv, k, mode, incoming=True)
            cp.wait_send()
            cp.wait_recv()

    return pl.pallas_call(
        body, name=name, in_specs=[HBM, HBM, SEM, SEM] + [ANY] * len(after),
        out_shape=[pltpu.HBM(src_thru.shape, src_thru.dtype), pltpu.HBM(land_thru.shape, land_thru.dtype)],
        out_specs=[HBM, HBM], input_output_aliases={0: 0, 1: 1},
        compiler_params=pltpu.CompilerParams(has_side_effects=EFFECT),
    )(src_thru, land_thru, send_sems, recv_sems, *after)[1]


def _landing_zone(own_block):
    me = _dev_index(*_mesh_pos())
    zone = lax.empty((N_DEV,) + own_block.shape, own_block.dtype)
    return lax.dynamic_update_index_in_dim(zone, own_block, me, 0)


def _row_tile(rows, want):
    t = min(want, rows) // SUBLANES * SUBLANES
    while rows % t:
        t -= SUBLANES
    return t


def _sum_slots(recv, *, tr, name):
    s_, r_, c_ = recv.shape
    tr = _row_tile(r_, tr)

    def body(g_ref, o_ref):
        acc = g_ref[0]
        for s in range(1, s_):
            acc = acc + g_ref[s]
        o_ref[...] = acc

    return pl.pallas_call(
        body, name=name, grid=(r_ // tr,),
        in_specs=[pl.BlockSpec((s_, tr, c_), lambda i: (0, i, 0))],
        out_specs=pl.BlockSpec((tr, c_), lambda i: (i, 0)),
        out_shape=jax.ShapeDtypeStruct((r_, c_), F32),
        compiler_params=_params(("parallel",), (2 * s_ + 3) * tr * c_ * 4),
    )(recv)


def _adamw(recv, w, m, v, *, tr, name):
    s_, r_, c_ = recv.shape
    tr = _row_tile(r_, tr)
    assert w.shape == (r_, c_), (name, w.shape, recv.shape)
    c1 = 1.0 - ADAM_B1 ** ADAM_STEP
    c2 = 1.0 - ADAM_B2 ** ADAM_STEP

    def body(g_ref, w_ref, m_ref, v_ref, go_ref, d_ref, mo_ref, vo_ref):
        g = g_ref[0].astype(F32)
        for s in range(1, s_):
            g = g + g_ref[s].astype(F32)
        mn = ADAM_B1 * m_ref[...] + (1.0 - ADAM_B1) * g
        vn = ADAM_B2 * v_ref[...] + (1.0 - ADAM_B2) * (g * g)
        go_ref[...] = g
        mo_ref[...] = mn
        vo_ref[...] = vn
        d_ref[...] = -ADAM_LR * ((mn / c1) / (jnp.sqrt(vn / c2) + ADAM_EPS) + ADAM_WD * w_ref[...])

    tile = pl.BlockSpec((tr, c_), lambda i: (i, 0))
    return pl.pallas_call(
        body, name=name, grid=(r_ // tr,),
        in_specs=[pl.BlockSpec((s_, tr, c_), lambda i: (0, i, 0)), tile, tile, tile],
        out_specs=[tile] * 4, out_shape=[jax.ShapeDtypeStruct((r_, c_), F32)] * 4,
        compiler_params=_params(("parallel",), (2 * s_ + 16) * tr * c_ * 4),
    )(recv, w, m, v)


def _adamw_whole(gs, ws, ms, vs, *, name):
    n = len(gs)
    c1 = 1.0 - ADAM_B1 ** ADAM_STEP
    c2 = 1.0 - ADAM_B2 ** ADAM_STEP

    def body(*refs):
        for i in range(n):
            g, w = refs[i][...], refs[n + i][...]
            mn = ADAM_B1 * refs[2 * n + i][...] + (1.0 - ADAM_B1) * g
            vn = ADAM_B2 * refs[3 * n + i][...] + (1.0 - ADAM_B2) * (g * g)
            refs[4 * n + 3 * i][...] = -ADAM_LR * ((mn / c1) / (jnp.sqrt(vn / c2) + ADAM_EPS) + ADAM_WD * w)
            refs[4 * n + 3 * i + 1][...] = mn
            refs[4 * n + 3 * i + 2][...] = vn

    whole = pl.BlockSpec(memory_space=pltpu.VMEM)
    lane_padded = sum(math.prod(g.shape[:-1]) * (-(-g.shape[-1] // LANES) * LANES) for g in gs)
    outs = pl.pallas_call(
        body, name=name, in_specs=[whole] * (4 * n), out_specs=[whole] * (3 * n),
        out_shape=[jax.ShapeDtypeStruct(g.shape, F32) for g in gs for _ in range(3)],
        compiler_params=pltpu.CompilerParams(vmem_limit_bytes=int(min(max(16 * lane_padded * 4, 16 * 2 ** 20), VMEM_CAP))),
    )(*gs, *ws, *ms, *vs)
    return [tuple(outs[3 * i:3 * i + 3]) for i in range(n)]


def _s5_discretise(a_re, a_im, log_dt, b_re, b_im):
    dt = jnp.exp(log_dt)[:, None]
    lr = jnp.minimum(a_re, -1e-4)
    li = a_im
    mag = jnp.exp(lr * dt)
    lbr = mag * jnp.cos(li * dt)
    lbi = mag * jnp.sin(li * dt)
    zr, zi = lbr - 1.0, lbi
    den = lr * lr + li * li
    fr = (zr * lr + zi * li) / den
    fi = (zi * lr - zr * li) / den
    bbr = fr[..., None] * b_re - fi[..., None] * b_im
    bbi = fr[..., None] * b_im + fi[..., None] * b_re
    return lbr, lbi, bbr, bbi


def _softplus_neg(lam):
    return jnp.maximum(-lam, 0.0) + jnp.log(1.0 + jnp.exp(-jnp.abs(lam)))


S5_Q = 8
RG_Q = 2


def _local_step(x, tgt, W, comm):
    T, D = x.shape
    C = D
    G, P, H = W["ssm_b_re"].shape
    S = G * H
    F = W["mlp_b_up"].shape[1]
    n_in = 2 * C + S + 2 * D
    heads, hd = W["rg_wa"].shape[0], W["rg_wa"].shape[1]
    u_off, ga_off, gb_off = 2 * C, 2 * C + S, 2 * C + S + D

    if comm.first_token is not None:
        anchored = ("rg_lambda", "ssm_a_re", "rg_wa", "rg_wx", "ssm_c_re", "ssm_c_im")
        W = {**W, **{k: W[k] + comm.first_token[0, 0] for k in anchored}}
    sp, sp_vjp = jax.vjp(_softplus_neg, W["rg_lambda"])
    (lbr, lbi, bbr, bbi), s5_vjp = jax.vjp(_s5_discretise, W["ssm_a_re"], W["ssm_a_im"], W["ssm_log_dt"],
                                           W["ssm_b_re"], W["ssm_b_im"])
    lam_re, lam_im = lbr.reshape(-1), lbi.reshape(-1)
    jr, kr = heads // RG_Q, RG_Q * hd
    w_ri = jnp.concatenate([_bd_pack(W["rg_wa"], RG_Q), _bd_pack(W["rg_wx"], RG_Q)], axis=2).astype(BF16)
    b_ri = jnp.concatenate([W["rg_ba"].reshape(jr, kr), W["rg_bx"].reshape(jr, kr)], axis=1).reshape(1, -1)
    wb_re = _bd_pack(jnp.swapaxes(bbr, 1, 2), S5_Q).astype(BF16)
    wb_im = _bd_pack(jnp.swapaxes(bbi, 1, 2), S5_Q).astype(BF16)
    wc_re = _bd_pack(jnp.swapaxes(W["ssm_c_re"], 1, 2), S5_Q).astype(BF16)
    wc_im_neg = _bd_pack(jnp.swapaxes(-W["ssm_c_im"], 1, 2), S5_Q).astype(BF16)
    d_row = W["ssm_d"].reshape(1, S)
    powers = _power_slabs(lam_re, lam_im, S5_TILE // SUBLANES)

    x_bf = x.astype(BF16) if comm.first_token is None else (x + comm.first_token[0, 0]).astype(BF16)
    w_in, conv_w = comm.first_weights((x_bf, w_ri, wb_re, wb_im, wc_re, wc_im_neg, powers[0], powers[1]))
    z = _mm(x_bf, w_in, M=T, N=n_in, K=D, tm=1024, tn=n_in // 4, tk=D, after=comm.gather_token, name="fwd_in_proj")
    started = comm.start_weights(("mlp_w_up",), z)
    xc = _conv_fwd(z, conv_w, W["conv_b"], T=T, C=C, after=started, name="fwd_conv")
    ri = _bd([(xc, 0, w_ri)], T=T, J=jr, kb=kr, nb=2 * kr, extras=[(b_ri, "vec", 0)],
             epilogue=lambda acc, b: (_sig(acc + b),), name="fwd_gates")
    h, p, a_fwd, m_fwd = _rg_scan_fwd(z, ri, xc, sp, T=T, C=C, gate_off=C, cw=kr, name="fwd_rg_scan")
    w_a_out = comm.weight("w_a_out", p)
    started = comm.start_weights(("mlp_w_down",), p)
    y_a = _mm(p, w_a_out, M=T, N=D, K=C, out_dtypes=(BF16,), tm=512, tn=D, tk=C, after=started, name="fwd_rg_out")

    h_re, h_im, y_s, yg = _s5_fwd(z, u_off, wb_re, wb_im, wc_re, wc_im_neg, d_row, powers, T=T, name="fwd_s5")
    w_glu_w, w_glu_v = comm.weight("glu_w", yg), comm.weight("glu_v", yg)
    glu_a = _mm(yg, w_glu_w, M=T, N=D, K=S, out_dtypes=(BF16,), tm=1024, tn=D, tk=S, name="fwd_glu_w")
    cwm = 1024

    def mix_fn(b, ga, gb, ya, a):
        return b, _sig(ga) * ya.astype(F32) + _sig(gb) * (a.astype(F32) * _sig(b))

    glu_b, mix = _mm(yg, w_glu_v, M=T, N=D, K=S, tm=512, tn=cwm, tk=S,
                     extras=[(z, "mn", ga_off // cwm), (z, "mn", gb_off // cwm), (y_a, "mn"), (glu_a, "mn")],
                     epilogue=mix_fn, n_out=2, out_dtypes=(BF16, BF16), name="fwd_glu_v_mix")
    w_out = comm.weight("w_out", mix)
    def out_ln1_fn(acc, xv, g, b):
        s = ALPHA * xv + acc
        xhat, _ = _ln_stats(s)
        y = xhat * g + b
        return s, y, y

    s1, x1, x1_bf = _mm(mix, w_out, M=T, N=D, K=D, tm=256, tn=D, tk=D,
                        extras=[(x, "mn"), (W["ln1_g"], "n"), (W["ln1_b"], "n")], epilogue=out_ln1_fn, n_out=3,
                        out_dtypes=(F32, F32, BF16), name="fwd_out_proj_ln1")
    w_up = comm.weight("mlp_w_up", x1_bf)

    def mlp_up_fn(acc, b):
        hp = acc + b
        rl = jnp.maximum(hp, 0.0)
        return rl * rl, hp

    hact, hpre = _mm(x1_bf, w_up, M=T, N=F, K=D, tm=1024, tn=1024, tk=D, extras=[(W["mlp_b_up"], "n")],
                     epilogue=mlp_up_fn, n_out=2, out_dtypes=(BF16, BF16), name="fwd_mlp_up")
    w_down = comm.weight("mlp_w_down", hact)
    s2 = _mm(hact, w_down, M=T, N=D, K=F, tm=1024, tn=1024, tk=2048,
             extras=[(x1, "mn"), (W["mlp_b_down"], "n")], epilogue=lambda acc, xv, b: (ALPHA * xv + acc + b,),
             name="fwd_mlp_down")

    def ln2_fn(s, t, g, b):
        xhat, rstd = _ln_stats(s)
        err = xhat * g + b - t
        dy = err * (1.0 / D)
        ds = _ln_bwd(dy, g, xhat, rstd)
        return ds, ds, 0.5 * dy * err, dy * xhat, dy, ds

    ds2, ds2_bf, loss_cols, d_ln2_g, d_ln2_b, d_b_down = _ew(
        ln2_fn, [(s2, "tile", 0), (tgt, "tile", 0), (W["ln2_g"], "vec", 0), (W["ln2_b"], "vec", 0)],
        T=T, C=D, n_out=2, n_cs=4, out_dtypes=(F32, BF16), tm=256, name="bwd_loss_ln2")
    d_w_down = _mm(hact, ds2_bf, M=F, N=D, K=T, ta=True, out_dtypes=(BF16,), tm=1024, tn=1024, tk=4096, name="bwd_w_down")
    sent = comm.send_grad("mlp_w_down", d_w_down)

    def dhpre_fn(acc, hp):
        dv = acc * (2.0 * jnp.maximum(hp.astype(F32), 0.0))
        return dv, dv

    dhpre, d_b_up = _mm(ds2_bf, w_down, M=T, N=F, K=D, tb=True, tm=1024, tn=1024, tk=D, extras=[(hpre, "mn")],
                        epilogue=dhpre_fn, n_cs=1, out_dtypes=(BF16,), after=sent, name="bwd_mlp_down")
    d_w_up = _mm(x1_bf, dhpre, M=D, N=F, K=T, ta=True, out_dtypes=(BF16,), n_split=N_DEV, tm=1024, tn=F // N_DEV, tk=4096, name="bwd_w_up")
    sent = comm.send_grad("mlp_w_up", d_w_up)
    dx1 = _mm(dhpre, w_up, M=T, N=D, K=F, tb=True, tm=1024, tn=1024, tk=2048,
              extras=[(ds2, "mn")], epilogue=lambda acc, dv: (ALPHA * dv + acc,), after=sent, name="bwd_mlp_up")

    def ln1_bwd_fn(s, dy, g):
        xhat, rstd = _ln_stats(s)
        ds = _ln_bwd(dy, g, xhat, rstd)
        return ds, ds, dy * xhat, dy

    ds1, ds1_bf, d_ln1_g, d_ln1_b = _ew(ln1_bwd_fn, [(s1, "tile", 0), (dx1, "tile", 0), (W["ln1_g"], "vec", 0)],
                                        T=T, C=D, n_out=2, n_cs=2, out_dtypes=(F32, BF16), tm=256, name="bwd_ln1")
    d_w_out = _mm(mix, ds1_bf, M=D, N=D, K=T, ta=True, out_dtypes=(BF16,), tm=1024, tn=1024, tk=4096, name="bwd_w_out")
    sent = comm.send_grad("w_out", d_w_out)
    def mix_bwd_fn(dm, ga, gb, ya, a, b):
        ya, a, b = ya.astype(F32), a.astype(F32), b.astype(F32)
        sa, sb, sv = _sig(ga), _sig(gb), _sig(b)
        yb = a * sv
        dyb = dm * sb
        return (dm * ya * (sa * (1.0 - sa)), dm * yb * (sb * (1.0 - sb)), dm * sa, dyb * sv,
                dyb * a * (sv * (1.0 - sv)))

    dz = lax.empty((T, n_in), BF16)
    dz, dg_b, dy_a, dglu_a, dglu_b = _mm(
        ds1_bf, w_out, M=T, N=D, K=D, tb=True, tm=512, tn=cwm, tk=D,
        extras=[(z, "mn", ga_off // cwm), (z, "mn", gb_off // cwm), (y_a, "mn"), (glu_a, "mn"), (glu_b, "mn")],
        epilogue=mix_bwd_fn, n_out=5, out_dtypes=(BF16,) * 5, after=sent, into=(dz, 0, ga_off // cwm),
        name="bwd_out_proj_mix")
    dz = lax.dynamic_update_slice(dz, dg_b, (0, gb_off))

    d_w_a_out = _mm(p, dy_a, M=C, N=D, K=T, ta=True, out_dtypes=(BF16,), tm=1024, tn=1024, tk=4096, name="bwd_w_a_out")
    sent = comm.send_grad("w_a_out", d_w_a_out)
    def dp_fn(dp, hv, gate):
        th = jnp.tanh(GELU_C * (gate + GELU_K * gate * gate * gate))
        gelu = 0.5 * gate * (1.0 + th)
        dgelu = 0.5 * (1.0 + th) + 0.5 * gate * (1.0 - th * th) * (GELU_C * (1.0 + 3.0 * GELU_K * gate * gate))
        return dp * gelu, dp * hv * dgelu

    dh, dz = _mm(dy_a, w_a_out, M=T, N=C, K=D, tb=True, tm=256, tn=C, tk=D, extras=[(h, "mn"), (z, "mn", 1)],
                 epilogue=dp_fn, n_out=2, out_dtypes=(F32, BF16), after=sent, into=(dz, 1, 1), name="bwd_rg_out")
    drai, dxc0, d_b_ri, d_sp = _rg_scan_bwd(dh, h, ri, xc, a_fwd, m_fwd, sp, T=T, C=C, cw=kr, name="bwd_rg_scan")
    dxc = _bd([(drai, 0, w_ri)], T=T, J=jr, kb=2 * kr, nb=kr, tw=True, extras=[(dxc0, "tile", 0)],
              epilogue=lambda acc, d0: (acc + d0,), name="bwd_gates")
    d_w_ri = _bdw(xc, 0, drai, 0, T=T, J=jr, kb=kr, nb=2 * kr, name="bwd_w_gates")
    d_wa, d_wx = _bd_unpack(d_w_ri[:, :, :kr], RG_Q), _bd_unpack(d_w_ri[:, :, kr:], RG_Q)
    d_b_ri = d_b_ri.reshape(jr, 2 * kr)
    d_ba, d_bx = d_b_ri[:, :kr].reshape(1, -1), d_b_ri[:, kr:].reshape(1, -1)
    dz, conv_sums = _conv_bwd(dxc, z, conv_w, dz, T=T, C=C, name="bwd_conv")
    d_conv_w, d_conv_b = conv_sums[0:4], conv_sums[4:5]
    (d_lambda,) = sp_vjp(d_sp)

    d_glu_w = _mm(yg, dglu_a, M=S, N=D, K=T, ta=True, out_dtypes=(BF16,), n_split=N_DEV, tm=1024, tn=D // N_DEV, tk=4096, name="bwd_w_glu_w")
    d_glu_v = _mm(yg, dglu_b, M=S, N=D, K=T, ta=True, out_dtypes=(BF16,), n_split=N_DEV, tm=1024, tn=D // N_DEV, tk=4096, name="bwd_w_glu_v")
    sent = comm.send_grad("glu_w", d_glu_w, "glu_v", d_glu_v)
    dyg0 = _mm(dglu_a, w_glu_w, M=T, N=S, K=D, tb=True, tm=512, tn=S, tk=D, after=sent, name="bwd_glu_w")
    dy_s = _mm(dglu_b, w_glu_v, M=T, N=S, K=D, tb=True, tm=512, tn=S, tk=D,
               extras=[(dyg0, "mn"), (y_s, "mn")], epilogue=lambda acc, d0, yv: ((acc + d0) * _dgelu(yv),),
               name="bwd_glu_v")
    dz, d_lbr, d_lbi, d_ssm_d, d_wb_re, d_wb_im, d_wc_re, d_wc_im_neg = _s5_bwd(
        dy_s, z, u_off, h_re, h_im, wb_re, wb_im, wc_re, wc_im_neg, d_row, powers, dz, T=T, name="bwd_s5")
    d_bbr = jnp.swapaxes(_bd_unpack(d_wb_re, S5_Q), 1, 2)
    d_bbi = jnp.swapaxes(_bd_unpack(d_wb_im, S5_Q), 1, 2)
    d_a_re, d_a_im, d_log_dt, d_b_re, d_b_im = s5_vjp((d_lbr.reshape(G, P), d_lbi.reshape(G, P), d_bbr, d_bbi))
    d_c_re = _bd_unpack(d_wc_re, S5_Q)
    d_c_im = -_bd_unpack(d_wc_im_neg, S5_Q)

    grads = dict(
        conv_w=d_conv_w, conv_b=d_conv_b, rg_wa=d_wa, rg_ba=d_ba, rg_wx=d_wx, rg_bx=d_bx,
        rg_lambda=d_lambda, ssm_a_re=d_a_re, ssm_a_im=d_a_im, ssm_log_dt=d_log_dt,
        ssm_b_re=d_b_re, ssm_b_im=d_b_im, ssm_c_re=d_c_re, ssm_c_im=d_c_im, ssm_d=d_ssm_d.reshape(G, H),
        ln1_g=d_ln1_g, ln1_b=d_ln1_b, mlp_b_up=d_b_up, mlp_b_down=d_b_down, ln2_g=d_ln2_g, ln2_b=d_ln2_b)
    sent = comm.send_small(grads)

    d_w_in = _mm(x_bf, dz, M=D, N=n_in, K=T, ta=True, out_dtypes=(BF16,), n_split=N_DEV, tm=1024, tn=n_in // N_DEV,
                 tk=4096, after=sent, name="bwd_w_in")
    sent = comm.send_grad("w_in", d_w_in)
    grad_x = _mm(dz, w_in, M=T, N=D, K=n_in, tb=True, tm=1024, tn=1024, tk=n_in // 4,
                 extras=[(ds1, "mn")], epilogue=lambda acc, dv: (ALPHA * dv + acc,), after=sent, name="bwd_in_proj")
    return jnp.sum(loss_cols), grad_x, grads


BIG = ("w_in", "w_a_out", "glu_w", "glu_v", "w_out", "mlp_w_up", "mlp_w_down")
COL_SHARDED = ("w_in", "glu_w", "glu_v", "mlp_w_up")
SMALL = ("conv_w", "conv_b", "rg_wa", "rg_ba", "rg_wx", "rg_bx", "rg_lambda", "ssm_a_re", "ssm_a_im", "ssm_log_dt",
         "ssm_b_re", "ssm_b_im", "ssm_c_re", "ssm_c_im", "ssm_d", "ln1_g", "ln1_b", "mlp_b_up", "mlp_b_down", "ln2_g",
         "ln2_b")
ORDER = ("w_in", "conv_w", "conv_b", "rg_wa", "rg_ba", "rg_wx", "rg_bx", "rg_lambda", "w_a_out", "ssm_a_re",
         "ssm_a_im", "ssm_log_dt", "ssm_b_re", "ssm_b_im", "ssm_c_re", "ssm_c_im", "ssm_d", "glu_w", "glu_v", "w_out",
         "ln1_g", "ln1_b", "mlp_w_up", "mlp_b_up", "mlp_w_down", "mlp_b_down", "ln2_g", "ln2_b")
TILE_ELEMS = SUBLANES * LANES


def _pack(arrs):
    pieces = []
    for a in arrs:
        flat = a.reshape(-1)
        flat = jnp.pad(flat, (0, (-flat.shape[0]) % TILE_ELEMS))
        pieces.append(flat.reshape(-1, LANES))
    rows = sum(p.shape[0] for p in pieces)
    pad_rows = (-rows) % (N_DEV * SUBLANES)
    if pad_rows:
        pieces.append(jnp.zeros((pad_rows, LANES), pieces[0].dtype))
    return jnp.concatenate(pieces, axis=0)


def _unpack(packed, shapes):
    out, row = [], 0
    for shp in shapes:
        n = math.prod(shp)
        rows = -(-n // TILE_ELEMS) * SUBLANES
        out.append(packed[row:row + rows].reshape(-1)[:n].reshape(shp))
        row += rows
    return out


class _Comm:
    def __init__(self, w):
        first = [w["w_in"].astype(BF16), w["conv_w"]]
        self._first, self.first_token = _exchange_start(first, [_landing_zone(s) for s in first], mode="own",
                                                        name="gather_in_start")
        self._shards = {k: w[k].astype(BF16) for k in BIG if k != "w_in"}
        self._weights, self._gathers, self._grads = {}, {}, {}

    def first_weights(self, after):
        lands = [_exchange_wait(h, after, mode="own", name="gather_in_wait_%d" % i) for i, h in enumerate(self._first)]
        unused = [lax.empty((2 * SUBLANES, LANES), BF16) for _ in lands]
        handles, passed = _exchange_start(unused, lands, mode="pass", name="gather_in_pass")
        w_in, taps = [_exchange_wait(h, passed, mode="pass", name="gather_in_got_%d" % i) for i, h in enumerate(handles)]
        self._weights["w_in"] = w_in
        self.gather_token = self.start_weights(("w_a_out", "glu_w", "glu_v", "w_out"), w_in)
        return self.weight("w_in", None), jnp.swapaxes(taps, 0, 1).reshape(taps.shape[1], -1)

    def start_weights(self, names, after):
        shards = [self._shards.pop(k) for k in names]
        handles, token = _exchange_start(shards, [_landing_zone(s) for s in shards], mode="gather", after=after,
                                         name="gather_start_" + names[0])
        self._gathers.update(zip(names, handles))
        return token

    def weight(self, k, after):
        if k not in self._weights:
            self._weights[k] = _exchange_wait(self._gathers.pop(k), after, mode="gather", name="gather_wait_" + k)
        gk = self._weights[k]
        if k in COL_SHARDED:
            return jnp.swapaxes(gk, 0, 1).reshape(gk.shape[1], -1)
        return gk.reshape(-1, gk.shape[-1])

    def send_grad(self, *names_and_parts):
        names, parts = names_and_parts[0::2], names_and_parts[1::2]
        parts = [p if k in COL_SHARDED else p.reshape(N_DEV, p.shape[0] // N_DEV, p.shape[1])
                 for k, p in zip(names, parts)]
        me = _dev_index(*_mesh_pos())
        lands = [_landing_zone(lax.dynamic_index_in_dim(p, me, 0, keepdims=False)) for p in parts]
        handles, token = _exchange_start(parts, lands, mode="scatter", name="grad_start_" + names[0])
        self._grads.update(zip(names, handles))
        return token

    def received_grad(self, k, after):
        return _exchange_wait(self._grads.pop(k), after, mode="scatter", name="grad_wait_" + k)

    def send_small(self, grads):
        return self.send_grad("small", _pack([grads[k] for k in SMALL]))

    def all_reduced_small(self, after, behind):
        recv = self.received_grad("small", after)
        block = _sum_slots(recv, tr=512, name="sum_small_grads")
        (handle,), started = _exchange_start([block], [_landing_zone(block)], mode="gather", name="small_sum_start")
        done = behind(started)
        return _exchange_wait(handle, done, mode="gather", name="small_sum_wait").reshape(-1, LANES)


SMALL_GROUPS = (("rg_wa", "rg_wx"), ("ssm_b_re",), ("ssm_b_im",),
                tuple(k for k in SMALL if k not in ("rg_wa", "rg_wx", "ssm_b_re", "ssm_b_im")))


def _step(x, tgt, w, m, v, raw_w, raw_m, raw_v):
    dev = _dev_index(*_mesh_pos())

    comm = _Comm(w)
    small = dict(w)
    for k in ("conv_b", "rg_ba", "rg_bx", "rg_lambda", "ln1_g", "ln1_b", "mlp_b_up", "mlp_b_down", "ln2_g", "ln2_b"):
        small[k] = w[k].reshape(1, -1)

    loss_part, grad_x, grads = _local_step(x, tgt, small, comm)

    out_g, out_d, out_m, out_v = {}, {}, {}, {}

    def update_large(started):
        for k in BIG:
            rk = comm.received_grad(k, (grad_x, started))
            out_g[k], out_d[k], out_m[k], out_v[k] = _adamw(rk, w[k], m[k], v[k], tr=256 if k in COL_SHARDED else 128,
                                                                 name="adamw_" + k)
        return out_v[BIG[-1]]

    small_all = comm.all_reduced_small(grad_x, update_large)
    g_small = dict(zip(SMALL, _unpack(small_all, [grads[k].shape for k in SMALL])))
    cw_cols = w["conv_w"].shape[1]
    g_small["conv_w"] = lax.dynamic_slice_in_dim(g_small["conv_w"], dev * cw_cols, cw_cols, axis=1)
    for group in SMALL_GROUPS:
        gs = [g_small[k].reshape(raw_w[k].shape) for k in group]
        res = _adamw_whole(gs, [raw_w[k] for k in group], [raw_m[k] for k in group], [raw_v[k] for k in group],
                           name="adamw_" + group[0])
        for k, gk, (dk, mk, vk) in zip(group, gs, res):
            out_g[k], out_d[k], out_m[k], out_v[k] = gk, dk, mk, vk

    loss = lax.psum(loss_part, ("x", "y", "c"))
    return loss, grad_x, out_g, out_d, out_m, out_v


def kernel(x, w_in, conv_w, conv_b, rg_wa, rg_ba, rg_wx, rg_bx, rg_lambda, w_a_out, ssm_a_re, ssm_a_im, ssm_log_dt, ssm_b_re, ssm_b_im, ssm_c_re, ssm_c_im, ssm_d, glu_w, glu_v, w_out, ln1_g, ln1_b, mlp_w_up, mlp_b_up, mlp_w_down, mlp_b_down, ln2_g, ln2_b, loss_target, m_w_in, m_conv_w, m_conv_b, m_rg_wa, m_rg_ba, m_rg_wx, m_rg_bx, m_rg_lambda, m_w_a_out, m_ssm_a_re, m_ssm_a_im, m_ssm_log_dt, m_ssm_b_re, m_ssm_b_im, m_ssm_c_re, m_ssm_c_im, m_ssm_d, m_glu_w, m_glu_v, m_w_out, m_ln1_g, m_ln1_b, m_mlp_w_up, m_mlp_b_up, m_mlp_w_down, m_mlp_b_down, m_ln2_g, m_ln2_b, v_w_in, v_conv_w, v_conv_b, v_rg_wa, v_rg_ba, v_rg_wx, v_rg_bx, v_rg_lambda, v_w_a_out, v_ssm_a_re, v_ssm_a_im, v_ssm_log_dt, v_ssm_b_re, v_ssm_b_im, v_ssm_c_re, v_ssm_c_im, v_ssm_d, v_glu_w, v_glu_v, v_w_out, v_ln1_g, v_ln1_b, v_mlp_w_up, v_mlp_b_up, v_mlp_w_down, v_mlp_b_down, v_ln2_g, v_ln2_b):
    args = locals()
    w = {k: args[k][0] for k in ORDER}
    m = {k: args["m_" + k][0] for k in BIG}
    v = {k: args["v_" + k][0] for k in BIG}
    raw = [{k: args[prefix + k] for k in SMALL} for prefix in ("", "m_", "v_")]
    loss, grad_x, out_g, out_d, out_m, out_v = _step(x[0], loss_target[0], w, m, v, *raw)
    outs = [loss, grad_x[None]]
    for group in (out_g, out_d, out_m, out_v):
        outs += [group[k].reshape(args[k].shape) for k in ORDER]
    return tuple(outs)
```

```python
import functools
import math

import jax
import jax.numpy as jnp
from jax import lax
from jax.experimental import pallas as pl
from jax.experimental.pallas import tpu as pltpu

F32 = jnp.float32
BF16 = jnp.bfloat16
MESH = pl.DeviceIdType.MESH
N_DEV = 8
SUBLANES = 8
LANES = 128
VMEM_BYTES_V7X = 64 * 2 ** 20
VMEM_CAP = VMEM_BYTES_V7X - 8 * 2 ** 20

ALPHA = 2.0 ** 0.25
LN_EPS = 1e-5
RG_C = 8.0
ADAM_LR, ADAM_B1, ADAM_B2, ADAM_EPS, ADAM_WD, ADAM_STEP = 0.001, 0.9, 0.999, 1e-08, 0.01, 10
GELU_C = math.sqrt(2.0 / math.pi)
GELU_K = 0.044715

ANY = pl.BlockSpec(memory_space=pl.ANY)


def _params(sem, vmem_bytes):
    limit = int(min(max(2 * vmem_bytes, 16 * 2 ** 20), VMEM_CAP))
    return pltpu.CompilerParams(dimension_semantics=sem, vmem_limit_bytes=limit)


def _sig(x):
    return 1.0 / (1.0 + jnp.exp(-x))


def _gelu(x):
    return 0.5 * x * (1.0 + jnp.tanh(GELU_C * (x + GELU_K * x * x * x)))


def _dgelu(x):
    th = jnp.tanh(GELU_C * (x + GELU_K * x * x * x))
    return 0.5 * (1.0 + th) + 0.5 * x * (1.0 - th * th) * (GELU_C * (1.0 + 3.0 * GELU_K * x * x))


def _one_minus_exp(x, exp_half_x):
    p = x * (1.0 + x * (1 / 2 + x * (1 / 6 + x * (1 / 24 + x * (1 / 120)))))
    return jnp.where(x > -1 / 16, -p, 1.0 - exp_half_x * exp_half_x)


def _accumulate(ref, val, first):
    @pl.when(first)
    def _():
        ref[...] = val

    @pl.when(jnp.logical_not(first))
    def _():
        ref[...] += val


def _rows8(cw):
    return lax.broadcasted_iota(jnp.int32, (SUBLANES, cw), 0)


def _shift_down(cur, prev, s, rows):
    return jnp.where(rows < s, pltpu.roll(prev, s, 0), pltpu.roll(cur, s, 0))


def _shift_up(cur, nxt, s, rows):
    return jnp.where(rows < SUBLANES - s, pltpu.roll(cur, SUBLANES - s, 0), pltpu.roll(nxt, SUBLANES - s, 0))


def _mm(a, b, *, M, N, K, ta=False, tb=False, b_split=1, n_split=1, a_fn=None, extras=(), epilogue=None,
        n_out=1, n_cs=0, out_dtypes=None, tm=512, tn=512, tk=512, after=None, into=None, name):
    tm, tn, tk = min(tm, M), min(tn, N), min(tk, K)
    assert M % tm == 0 and N % tn == 0 and K % tk == 0, (name, M, N, K, tm, tn, tk)
    nk = K // tk
    grid = (N // tn, M // tm, nk)
    a_spec = pl.BlockSpec((tk, tm), lambda j, i, k: (k, i)) if ta else pl.BlockSpec((tm, tk), lambda j, i, k: (i, k))
    if b_split == 1:
        once = dict(pipeline_mode=pl.Buffered(1)) if (tn == N and tk == K) else {}
        b_spec = (pl.BlockSpec((tn, tk), lambda j, i, k: (j, k), **once) if tb
                  else pl.BlockSpec((tk, tn), lambda j, i, k: (k, j), **once))
    elif tb:
        kb = (K // b_split) // tk
        assert kb * tk * b_split == K, name
        b_spec = pl.BlockSpec((None, tn, tk), lambda j, i, k: (k // kb, j, k % kb))
    else:
        nb = (N // b_split) // tn
        assert nb * tn * b_split == N, name
        b_spec = pl.BlockSpec((None, tk, tn), lambda j, i, k: (j // nb, k, j % nb))
    in_specs = [a_spec, b_spec]
    for arr, kind, *col_off in extras:
        off = col_off[0] if col_off else 0
        in_specs.append(pl.BlockSpec((tm, tn), lambda j, i, k, off=off: (i, off + j)) if kind == "mn"
                        else pl.BlockSpec((1, tn), lambda j, i, k: (0, j)))
    out_dtypes = (F32,) * n_out if out_dtypes is None else out_dtypes
    if n_split == 1:
        out_shape = [jax.ShapeDtypeStruct((M, N), dt) for dt in out_dtypes]
        out_specs = [pl.BlockSpec((tm, tn), lambda j, i, k: (i, j)) for _ in range(n_out)]
    else:
        assert n_out == 1
        nbo = (N // n_split) // tn
        assert nbo * tn * n_split == N, name
        out_shape = [jax.ShapeDtypeStruct((n_split, M, N // n_split), out_dtypes[0])]
        out_specs = [pl.BlockSpec((None, tm, tn), lambda j, i, k: (j // nbo, i, j % nbo))]
    out_shape += [jax.ShapeDtypeStruct((1, N), F32) for _ in range(n_cs)]
    out_specs += [pl.BlockSpec((1, tn), lambda j, i, k: (0, j)) for _ in range(n_cs)]
    ne = len(extras)
    dims = (((0 if ta else 1,), (1 if tb else 0,)), ((), ()))

    n_after = 0 if after is None else 1
    in_specs += [ANY] * n_after
    aliases, tail = {}, [] if after is None else [after]
    if into is not None:
        buf, which, col_blk = into
        assert n_split == 1 and buf.shape[0] == M and buf.dtype == out_dtypes[which], name
        aliases = {len(in_specs): which}
        in_specs.append(ANY)
        tail.append(buf)
        out_shape[which] = jax.ShapeDtypeStruct(buf.shape, buf.dtype)
        out_specs[which] = pl.BlockSpec((tm, tn), lambda j, i, k: (i, col_blk + j))

    def body(*refs):
        a_ref, b_ref = refs[0], refs[1]
        ex_refs = refs[2:2 + ne]
        first_out = 2 + ne + len(tail)
        out_refs = refs[first_out:first_out + n_out]
        cs_refs = refs[first_out + n_out:first_out + n_out + n_cs]
        i, k = pl.program_id(1), pl.program_id(2)

        def product():
            av = a_ref[...]
            if a_fn is not None:
                av = a_fn(av.astype(F32))
            return lax.dot_general(av.astype(BF16), b_ref[...].astype(BF16), dims, preferred_element_type=F32)

        def finish(acc):
            res = (acc,) if epilogue is None else epilogue(acc, *[r[...] for r in ex_refs])
            for r, o in zip(out_refs, res[:n_out]):
                r[...] = o.astype(r.dtype)
            for r, cval in zip(cs_refs, res[n_out:]):
                _accumulate(r, jnp.sum(cval, axis=0, keepdims=True), i == 0)

        if nk == 1:
            finish(product())
            return
        acc_ref = refs[-1]

        @pl.when(k == 0)
        def _():
            acc_ref[...] = jnp.zeros_like(acc_ref)

        acc_ref[...] += product()

        @pl.when(k == nk - 1)
        def _():
            finish(acc_ref[...])

    vmem = 2 * tm * tk * a.dtype.itemsize + 2 * tk * tn * b.dtype.itemsize + (1 + 2 * n_out + 2 * ne + 2) * tm * tn * 4
    outs = pl.pallas_call(
        body, name=name, grid=grid, in_specs=in_specs, out_specs=out_specs, out_shape=out_shape,
        scratch_shapes=[pltpu.VMEM((tm, tn), F32)] if nk > 1 else [], input_output_aliases=aliases,
        compiler_params=_params(("parallel", "arbitrary", "arbitrary"), vmem),
    )(a, b, *[e[0] for e in extras], *tail)
    return outs[0] if len(outs) == 1 else outs


BD_STEP = 4

def _bd(pairs, *, T, J, kb, nb, tw=False, extras=(), epilogue=None, n_out=1, n_cs=0, out_dtypes=None, tm=1024, name):
    jb = BD_STEP
    assert T % tm == 0 and J % jb == 0
    grid = (J // jb, T // tm)
    npair, ne = len(pairs), len(extras)
    in_specs, args = [], []
    for arr, off, w in pairs:
        assert off % jb == 0, name
        in_specs.append(pl.BlockSpec((tm, jb * kb), lambda j, i, off=off // jb: (i, off + j)))
        in_specs.append(pl.BlockSpec((jb,) + tuple(w.shape[1:]), lambda j, i: (j, 0, 0)))
        args += [arr, w]
    for arr, kind, off in extras:
        assert off % jb == 0, name
        in_specs.append(pl.BlockSpec((tm, jb * nb), lambda j, i, off=off // jb: (i, off + j)) if kind == "tile"
                        else pl.BlockSpec((1, jb * nb), lambda j, i, off=off // jb: (0, off + j)))
        args.append(arr)
    out_dtypes = (F32,) * n_out if out_dtypes is None else out_dtypes
    out_shape = [jax.ShapeDtypeStruct((T, J * nb), dt) for dt in out_dtypes]
    out_specs = [pl.BlockSpec((tm, jb * nb), lambda j, i: (i, j)) for _ in range(n_out)]
    out_shape += [jax.ShapeDtypeStruct((1, J * nb), F32) for _ in range(n_cs)]
    out_specs += [pl.BlockSpec((1, jb * nb), lambda j, i: (0, j)) for _ in range(n_cs)]
    dims = (((1,), (1 if tw else 0,)), ((), ()))

    def body(*refs):
        ex_refs = refs[2 * npair:2 * npair + ne]
        out_refs = refs[2 * npair + ne:2 * npair + ne + n_out]
        cs_refs = refs[2 * npair + ne + n_out:]
        i = pl.program_id(1)
        for s in range(jb):
            cols_in, cols_out = pl.ds(s * kb, kb), pl.ds(s * nb, nb)
            acc = None
            for p in range(npair):
                d = lax.dot_general(refs[2 * p][:, cols_in].astype(BF16), refs[2 * p + 1][s].astype(BF16), dims,
                                    preferred_element_type=F32)
                acc = d if acc is None else acc + d
            res = (acc,) if epilogue is None else epilogue(acc, *[r[:, cols_out] for r in ex_refs])
            for r, o in zip(out_refs, res[:n_out]):
                r[:, cols_out] = o.astype(r.dtype)
            for r, cval in zip(cs_refs, res[n_out:]):
                _accumulate(r.at[:, cols_out], jnp.sum(cval, axis=0, keepdims=True), i == 0)

    vmem = jb * (2 * npair * tm * kb + 2 * npair * kb * nb + (2 * n_out + 2 * ne + 3) * tm * nb) * 4
    outs = pl.pallas_call(
        body, name=name, grid=grid, in_specs=in_specs, out_specs=out_specs, out_shape=out_shape,
        compiler_params=_params(("parallel", "arbitrary"), vmem),
    )(*args)
    return outs[0] if len(outs) == 1 else outs


def _bdw(a, a_off, b, b_off, *, T, J, kb, nb, tm=1024, name):
    jb = BD_STEP
    assert T % tm == 0 and J % jb == 0 and a_off % jb == 0 and b_off % jb == 0
    a_blk, b_blk = a_off // jb, b_off // jb

    def body(a_ref, b_ref, o_ref):
        i = pl.program_id(1)
        for s in range(jb):
            d = lax.dot_general(a_ref[:, pl.ds(s * kb, kb)].astype(BF16), b_ref[:, pl.ds(s * nb, nb)].astype(BF16),
                                (((0,), (0,)), ((), ())), preferred_element_type=F32)
            _accumulate(o_ref.at[s], d, i == 0)

    return pl.pallas_call(
        body, name=name, grid=(J // jb, T // tm),
        in_specs=[pl.BlockSpec((tm, jb * kb), lambda j, i: (i, a_blk + j)),
                  pl.BlockSpec((tm, jb * nb), lambda j, i: (i, b_blk + j))],
        out_specs=pl.BlockSpec((jb, kb, nb), lambda j, i: (j, 0, 0)),
        out_shape=jax.ShapeDtypeStruct((J, kb, nb), F32),
        compiler_params=_params(("parallel", "arbitrary"), jb * (2 * tm * (kb + nb) + 3 * kb * nb) * 4),
    )(a, b)


def _bd_pack(w, q):
    g, a, b = w.shape
    eye = jnp.eye(q, dtype=w.dtype)
    return jnp.einsum("jqab,qr->jqarb", w.reshape(g // q, q, a, b), eye).reshape(g // q, q * a, q * b)


def _bd_unpack(wp, q):
    j, qa, qb = wp.shape
    a, b = qa // q, qb // q
    w5 = wp.reshape(j, q, a, q, b)
    return jnp.stack([w5[:, r, :, r, :] for r in range(q)], axis=1).reshape(j * q, a, b)


def _ew(fn, ins, *, T, C, n_out, n_cs=0, out_dtypes=None, tm=256, cw=None, name):
    cw = C if cw is None else cw
    assert T % tm == 0 and C % cw == 0
    grid = (C // cw, T // tm)
    in_specs = []
    for arr, kind, off in ins:
        in_specs.append(pl.BlockSpec((tm, cw), lambda j, i, off=off: (i, off + j)) if kind == "tile"
                        else pl.BlockSpec((arr.shape[0], cw), lambda j, i, off=off: (0, off + j)))
    out_dtypes = (F32,) * n_out if out_dtypes is None else out_dtypes
    out_shape = [jax.ShapeDtypeStruct((T, C), dt) for dt in out_dtypes]
    out_specs = [pl.BlockSpec((tm, cw), lambda j, i: (i, j)) for _ in range(n_out)]
    out_shape += [jax.ShapeDtypeStruct((1, C), F32) for _ in range(n_cs)]
    out_specs += [pl.BlockSpec((1, cw), lambda j, i: (0, j)) for _ in range(n_cs)]
    nin = len(ins)

    def body(*refs):
        i = pl.program_id(1)
        res = fn(*[r[...].astype(F32) for r in refs[:nin]])
        for r, o in zip(refs[nin:nin + n_out], res[:n_out]):
            r[...] = o.astype(r.dtype)
        for r, cval in zip(refs[nin + n_out:], res[n_out:]):
            _accumulate(r, jnp.sum(cval, axis=0, keepdims=True), i == 0)

    vmem = (2 * nin + 2 * n_out + 6) * tm * cw * 4
    outs = pl.pallas_call(
        body, name=name, grid=grid, in_specs=in_specs, out_specs=out_specs, out_shape=out_shape,
        compiler_params=_params(("parallel", "arbitrary"), vmem),
    )(*[arr for arr, _, _ in ins])
    return outs[0] if len(outs) == 1 else outs


def _ln_stats(s):
    mu = jnp.mean(s, axis=-1, keepdims=True)
    d = s - mu
    var = jnp.mean(d * d, axis=-1, keepdims=True)
    rstd = lax.rsqrt(var + LN_EPS)
    return d * rstd, rstd


def _ln_bwd(dy, g, xhat, rstd):
    dxh = dy * g
    m1 = jnp.mean(dxh, axis=-1, keepdims=True)
    m2 = jnp.mean(dxh * xhat, axis=-1, keepdims=True)
    return rstd * (dxh - m1 - xhat * m2)


def _conv_fwd(z, conv_w, conv_b, *, T, C, tm=1024, cw=1024, after=None, name):
    ng, hb = tm // SUBLANES, tm // SUBLANES
    n_after = 0 if after is None else 1

    def body(x_ref, halo_ref, w_ref, b_ref, *rest):
        o_ref = rest[-1]
        it = pl.program_id(1)
        rows = _rows8(cw)
        halo = jnp.where(it == 0, 0.0, halo_ref[...])
        w = w_ref[...]
        bias = b_ref[...]

        def group(g, carry):
            off = pl.multiple_of(g * SUBLANES, SUBLANES)
            cur = x_ref[pl.ds(off, SUBLANES), :]
            prev = x_ref[pl.ds(pl.multiple_of(jnp.maximum(off - SUBLANES, 0), SUBLANES), SUBLANES), :]
            prev = jnp.where(g == 0, halo, prev)
            acc = cur * w[3:4] + bias
            for s in (1, 2, 3):
                acc = acc + _shift_down(cur, prev, s, rows) * w[3 - s:4 - s]
            o_ref[pl.ds(off, SUBLANES), :] = acc
            return carry

        lax.fori_loop(0, ng, group, 0, unroll=2)

    return pl.pallas_call(
        body, name=name, grid=(C // cw, T // tm),
        in_specs=[pl.BlockSpec((tm, cw), lambda j, i: (i, j)),
                  pl.BlockSpec((SUBLANES, cw), lambda j, i: (jnp.maximum(i * hb - 1, 0), j)),
                  pl.BlockSpec((4, cw), lambda j, i: (0, j)), pl.BlockSpec((1, cw), lambda j, i: (0, j))]
        + [ANY] * n_after,
        out_specs=pl.BlockSpec((tm, cw), lambda j, i: (i, j)),
        out_shape=jax.ShapeDtypeStruct((T, C), F32),
        compiler_params=_params(("parallel", "arbitrary"), 5 * tm * cw * 4),
    )(z, z, conv_w, conv_b, *([after] if n_after else []))


def _conv_bwd(dxc, z, conv_w, dz, *, T, C, tm=1024, cw=512, name):
    ng, hb, last = tm // SUBLANES, tm // SUBLANES, T // SUBLANES - 1
    nt = T // tm
    rows16 = 2 * SUBLANES

    def body(d_ref, dn_ref, x_ref, w_ref, dz_in_ref, o_ref, sums_ref):
        it = pl.program_id(1)
        rows = _rows8(cw)
        dnext = jnp.where(it == nt - 1, 0.0, dn_ref[...])
        w = w_ref[...]

        def pair(q, accs):
            base = pl.multiple_of(q * rows16, rows16)
            halves = []
            for half in range(2):
                g = 2 * q + half
                off = pl.multiple_of(base + half * SUBLANES, SUBLANES)
                dcur = d_ref[pl.ds(off, SUBLANES), :]
                dnx = d_ref[pl.ds(pl.multiple_of(jnp.minimum(off + SUBLANES, tm - SUBLANES), SUBLANES), SUBLANES), :]
                dnx = jnp.where(g == ng - 1, dnext, dnx)
                xcur = x_ref[pl.ds(off, SUBLANES), :]
                acc = dcur * w[3:4]
                taps = [accs[3] + dcur * xcur]
                for s in (1, 2, 3):
                    ahead = _shift_up(dcur, dnx, s, rows)
                    acc = acc + ahead * w[3 - s:4 - s]
                    taps.append(accs[3 - s] + ahead * xcur)
                halves.append(acc)
                accs = (taps[3], taps[2], taps[1], taps[0], accs[4] + dcur)
            o_ref[pl.ds(base, rows16), :] = jnp.concatenate(halves, axis=0).astype(o_ref.dtype)
            return accs

        zero = jnp.zeros((SUBLANES, cw), F32)
        accs = lax.fori_loop(0, ng // 2, pair, (zero,) * 5)
        sums = jnp.zeros((SUBLANES, cw), F32)
        for k, a in enumerate(accs):
            sums = jnp.where(rows == k, jnp.sum(a, axis=0, keepdims=True), sums)
        _accumulate(sums_ref, sums, it == 0)

    tile = pl.BlockSpec((tm, cw), lambda j, i: (i, j))
    return pl.pallas_call(
        body, name=name, grid=(C // cw, nt),
        in_specs=[tile, pl.BlockSpec((SUBLANES, cw), lambda j, i: (jnp.minimum((i + 1) * hb, last), j)),
                  tile, pl.BlockSpec((4, cw), lambda j, i: (0, j)), ANY],
        out_specs=[tile, pl.BlockSpec((SUBLANES, cw), lambda j, i: (0, j))],
        input_output_aliases={4: 0},
        out_shape=[jax.ShapeDtypeStruct(dz.shape, dz.dtype), jax.ShapeDtypeStruct((SUBLANES, C), F32)],
        compiler_params=_params(("parallel", "arbitrary"), 7 * tm * cw * 4),
    )(dxc, dxc, z, conv_w, dz)


def _rg_coeffs(r, ig, xc, sp):
    la = (-RG_C) * r * sp
    a = jnp.exp(la)
    m = jnp.sqrt(_one_minus_exp(2.0 * la, a))
    return a, m, m * (ig * xc)


def _rg_scan_fwd(z, ri, xc, sp, *, T, C, gate_off, tm=1024, cw=256, name):
    rows16 = 2 * SUBLANES
    nq = tm // rows16

    def body(gate_ref, r_ref, i_ref, xc_ref, sp_ref, h_ref, p_ref, a_ref, m_ref, carry_ref):
        it = pl.program_id(1)

        @pl.when(it == 0)
        def _():
            carry_ref[...] = jnp.zeros_like(carry_ref)

        rows = _rows8(cw)
        sp_row = sp_ref[...]

        def pair(q, carry):
            base = pl.multiple_of(q * rows16, rows16)
            halves = []
            for half in range(2):
                sl = pl.ds(pl.multiple_of(base + half * SUBLANES, SUBLANES), SUBLANES)
                a, m, b = _rg_coeffs(r_ref[sl, :], i_ref[sl, :], xc_ref[sl, :], sp_row)
                a_ref[sl, :] = a
                m_ref[sl, :] = m
                for s in (1, 2, 4):
                    keep = rows >= s
                    sa = jnp.where(keep, pltpu.roll(a, s, 0), 1.0)
                    sb = jnp.where(keep, pltpu.roll(b, s, 0), 0.0)
                    b = b + a * sb
                    a = a * sa
                h = b + a * carry
                h_ref[sl, :] = h
                halves.append(h * _gelu(gate_ref[sl, :]))
                carry = h[SUBLANES - 1:SUBLANES, :]
            p_ref[pl.ds(base, rows16), :] = jnp.concatenate(halves, axis=0).astype(p_ref.dtype)
            return carry

        last = lax.fori_loop(0, nq, pair, carry_ref[0:1, :], unroll=2)
        carry_ref[...] = jnp.broadcast_to(last, carry_ref.shape)

    tile = pl.BlockSpec((tm, cw), lambda j, i: (i, j))
    gate_blk = gate_off // cw
    return pl.pallas_call(
        body, name=name, grid=(C // cw, T // tm),
        in_specs=[pl.BlockSpec((tm, cw), lambda j, i: (i, gate_blk + j)),
                  pl.BlockSpec((tm, cw), lambda j, i: (i, 2 * j)), pl.BlockSpec((tm, cw), lambda j, i: (i, 2 * j + 1)),
                  tile, pl.BlockSpec((1, cw), lambda j, i: (0, j))],
        out_specs=[tile, tile, tile, tile],
        out_shape=[jax.ShapeDtypeStruct((T, C), F32), jax.ShapeDtypeStruct((T, C), BF16),
                   jax.ShapeDtypeStruct((T, C), F32), jax.ShapeDtypeStruct((T, C), F32)],
        scratch_shapes=[pltpu.VMEM((SUBLANES, cw), F32)],
        compiler_params=_params(("parallel", "arbitrary"), 16 * tm * cw * 4),
    )(z, ri, ri, xc, sp)


def _rg_scan_bwd(dh, h, ri, xc, a_fwd, m_fwd, sp, *, T, C, tm=1024, cw=256, name):
    ng, hb, nt = tm // SUBLANES, tm // SUBLANES, T // tm

    def body(dh_ref, h_ref, hp_ref, r_ref, i_ref, xc_ref, a_ref, m_ref, sp_ref,
             drai_ref, dxc_ref, crai_ref, csp_ref, cg_ref, ca_ref):
        step = pl.program_id(1)

        @pl.when(step == 0)
        def _():
            cg_ref[...] = jnp.zeros_like(cg_ref)
            ca_ref[...] = jnp.zeros_like(ca_ref)

        rows = _rows8(cw)
        sp_row = sp_ref[...]
        hhalo = jnp.where(step == nt - 1, 0.0, hp_ref[...])

        def group(gi, carry):
            g_next, a_next, s_ra, s_ia, s_sp = carry
            g = ng - 1 - gi
            off = pl.multiple_of(g * SUBLANES, SUBLANES)
            sl = pl.ds(off, SUBLANES)
            rr, ii, xx = r_ref[sl, :], i_ref[sl, :], xc_ref[sl, :]
            a, m = a_ref[sl, :], m_ref[sl, :]
            hh = h_ref[sl, :]
            hpv = h_ref[pl.ds(pl.multiple_of(jnp.maximum(off - SUBLANES, 0), SUBLANES), SUBLANES), :]
            hpv = jnp.where(g == 0, hhalo, hpv)
            hprev = _shift_down(hh, hpv, 1, rows)
            d = dh_ref[sl, :]
            c = jnp.where(rows < SUBLANES - 1, pltpu.roll(a, SUBLANES - 1, 0), a_next)
            for s in (1, 2, 4):
                keep = rows < SUBLANES - s
                sc = jnp.where(keep, pltpu.roll(c, SUBLANES - s, 0), 1.0)
                sd = jnp.where(keep, pltpu.roll(d, SUBLANES - s, 0), 0.0)
                d = d + c * sd
                c = c * sc
            gg = d + c * g_next
            da = gg * hprev
            dm = gg * (ii * xx)
            di = gg * (m * xx)
            dxc_ref[sl, :] = gg * (m * ii)
            dla = da * a - dm * (a * a / m)
            dra = dla * ((-RG_C) * sp_row) * (rr * (1.0 - rr))
            dia = di * (ii * (1.0 - ii))
            drai_ref[sl, pl.ds(0, cw)] = dra
            drai_ref[sl, pl.ds(cw, cw)] = dia
            return (gg[0:1, :], a[0:1, :], s_ra + dra, s_ia + dia, s_sp + dla * ((-RG_C) * rr))

        zero = jnp.zeros((SUBLANES, cw), F32)
        g_first, a_first, s_ra, s_ia, s_sp = lax.fori_loop(
            0, ng, group, (cg_ref[0:1, :], ca_ref[0:1, :], zero, zero, zero), unroll=2)
        cg_ref[...] = jnp.broadcast_to(g_first, cg_ref.shape)
        ca_ref[...] = jnp.broadcast_to(a_first, ca_ref.shape)
        for ref, acc in ((crai_ref.at[:, pl.ds(0, cw)], s_ra), (crai_ref.at[:, pl.ds(cw, cw)], s_ia), (csp_ref, s_sp)):
            _accumulate(ref, jnp.sum(acc, axis=0, keepdims=True), step == 0)

    tile = pl.BlockSpec((tm, cw), lambda j, i: (nt - 1 - i, j))
    wide = pl.BlockSpec((tm, 2 * cw), lambda j, i: (nt - 1 - i, j))
    vec = pl.BlockSpec((1, cw), lambda j, i: (0, j))
    return pl.pallas_call(
        body, name=name, grid=(C // cw, nt),
        in_specs=[tile, tile, pl.BlockSpec((SUBLANES, cw), lambda j, i: (jnp.maximum((nt - 1 - i) * hb - 1, 0), j)),
                  pl.BlockSpec((tm, cw), lambda j, i: (nt - 1 - i, 2 * j)),
                  pl.BlockSpec((tm, cw), lambda j, i: (nt - 1 - i, 2 * j + 1)), tile, tile, tile, vec],
        out_specs=[wide, tile, pl.BlockSpec((1, 2 * cw), lambda j, i: (0, j)), vec],
        out_shape=[jax.ShapeDtypeStruct((T, 2 * C), F32), jax.ShapeDtypeStruct((T, C), F32),
                   jax.ShapeDtypeStruct((1, 2 * C), F32), jax.ShapeDtypeStruct((1, C), F32)],
        scratch_shapes=[pltpu.VMEM((SUBLANES, cw), F32), pltpu.VMEM((SUBLANES, cw), F32)],
        compiler_params=_params(("parallel", "arbitrary"), 24 * tm * cw * 4),
    )(dh, h, h, ri, ri, xc, a_fwd, m_fwd, sp)


def _cscan_tables(lr, li, reverse):
    lam = (lr.reshape(-1), -li.reshape(-1) if reverse else li.reshape(-1))

    def mul(p, q):
        return p[0] * q[0] - p[1] * q[1], p[0] * q[1] + p[1] * q[0]

    pows = [lam]
    for _ in range(SUBLANES - 1):
        pows.append(mul(pows[-1], lam))
    zero = jnp.zeros_like(lam[0])
    tab = jnp.stack([pows[0][0], pows[0][1], pows[1][0], pows[1][1], pows[3][0], pows[3][1], zero, zero])
    if reverse:
        pows = pows[::-1]
    return tab, jnp.stack([p[0] for p in pows]), jnp.stack([p[1] for p in pows])


def _power_slabs(lr, li, n):
    pr, pi = lr.reshape(1, -1), li.reshape(1, -1)
    while pr.shape[0] < n:
        tr, ti = pr[-1:], pi[-1:]
        pr, pi = (jnp.concatenate([pr, pr * tr - pi * ti], axis=0), jnp.concatenate([pi, pr * ti + pi * tr], axis=0))
    return jnp.repeat(pr, SUBLANES, axis=0), jnp.repeat(pi, SUBLANES, axis=0), pr[-1], pi[-1]


def _rows_to_segments(src_ref, dst_ref):
    seg = src_ref.shape[0] // SUBLANES
    for g in range(seg):
        dst_ref[pl.ds(g * SUBLANES, SUBLANES), :] = src_ref[pl.ds(g, SUBLANES, stride=seg), :].astype(dst_ref.dtype)


def _segments_to_rows(src_ref, dst_ref):
    seg = src_ref.shape[0] // SUBLANES
    for r in range(SUBLANES):
        dst_ref[pl.ds(r * seg, seg), :] = src_ref[pl.ds(r, seg, stride=SUBLANES), :].astype(dst_ref.dtype)


def _seg_scan_tile(xr_ref, xi_ref, pbr_ref, pbi_ref, tab_ref, pwr_ref, pwi_ref, cr_ref, ci_ref, *, reverse, h=None):
    tm, cw = xr_ref.shape
    seg = tm // SUBLANES
    rows = _rows8(cw)
    sign = -1.0 if reverse else 1.0
    l_re, l_im = pbr_ref[0:1, :], sign * pbi_ref[0:1, :]

    def slab(g):
        return pl.ds(pl.multiple_of(g * SUBLANES, SUBLANES), SUBLANES)

    def local(k, state):
        sl = slab(seg - 1 - k if reverse else k)
        sr, si = state
        nr = xr_ref[sl, :] + (l_re * sr - l_im * si)
        ni = xi_ref[sl, :] + (l_re * si + l_im * sr)
        xr_ref[sl, :] = nr
        xi_ref[sl, :] = ni
        return nr, ni

    zero = jnp.zeros((SUBLANES, cw), F32)
    er, ei = lax.fori_loop(0, seg, local, (zero, zero), unroll=2)

    for k, s in enumerate((1, 2, 4)):
        shift = SUBLANES - s if reverse else s
        keep = rows < SUBLANES - s if reverse else rows >= s
        sr = jnp.where(keep, pltpu.roll(er, shift, 0), 0.0)
        si = jnp.where(keep, pltpu.roll(ei, shift, 0), 0.0)
        m_re, m_im = tab_ref[2 * k:2 * k + 1, :], tab_ref[2 * k + 1:2 * k + 2, :]
        er, ei = er + (m_re * sr - m_im * si), ei + (m_re * si + m_im * sr)
    cin_r, cin_i = cr_ref[0:1, :], ci_ref[0:1, :]
    pwr, pwi = pwr_ref[...], pwi_ref[...]
    er, ei = er + (pwr * cin_r - pwi * cin_i), ei + (pwr * cin_i + pwi * cin_r)
    if reverse:
        ent_r = jnp.where(rows == SUBLANES - 1, cin_r, pltpu.roll(er, SUBLANES - 1, 0))
        ent_i = jnp.where(rows == SUBLANES - 1, cin_i, pltpu.roll(ei, SUBLANES - 1, 0))
        out_r, out_i = er[0:1, :], ei[0:1, :]
    else:
        ent_r = jnp.where(rows == 0, cin_r, pltpu.roll(er, 1, 0))
        ent_i = jnp.where(rows == 0, cin_i, pltpu.roll(ei, 1, 0))
        out_r, out_i = er[SUBLANES - 1:SUBLANES, :], ei[SUBLANES - 1:SUBLANES, :]
    cr_ref[...] = jnp.broadcast_to(out_r, cr_ref.shape)
    ci_ref[...] = jnp.broadcast_to(out_i, ci_ref.shape)

    if h is not None:
        hr_ref, hi_ref, hr_last, hi_last = h
        hr_wrap = _shift_down(hr_ref[pl.ds(tm - SUBLANES, SUBLANES), :], hr_last, 1, rows)
        hi_wrap = _shift_down(hi_ref[pl.ds(tm - SUBLANES, SUBLANES), :], hi_last, 1, rows)

    def fix(g, sums):
        sl = slab(g)
        power = slab(seg - 1 - g) if reverse else sl
        pr, pi = pbr_ref[power, :], sign * pbi_ref[power, :]
        nr = xr_ref[sl, :] + (pr * ent_r - pi * ent_i)
        ni = xi_ref[sl, :] + (pr * ent_i + pi * ent_r)
        xr_ref[sl, :] = nr
        xi_ref[sl, :] = ni
        if h is None:
            return sums
        before = slab(jnp.maximum(g - 1, 0))
        hr1 = jnp.where(g == 0, hr_wrap, hr_ref[before, :])
        hi1 = jnp.where(g == 0, hi_wrap, hi_ref[before, :])
        return sums[0] + (nr * hr1 + ni * hi1), sums[1] + (ni * hr1 - nr * hi1)

    return lax.fori_loop(0, seg, fix, (zero, zero) if h is not None else (), unroll=2)


S5_TILE = 2048


def _s5_fwd(z, u_off, wb_re, wb_im, wc_re, wc_im_neg, d_row, powers, *, T, tm=S5_TILE, name):
    J, ku, kp = wb_re.shape
    nt = T // tm
    pb_re, pb_im, top_re, top_im = powers
    tab, pw_re, pw_im = _cscan_tables(top_re, top_im, False)
    u_blk = u_off // ku

    def body(u_ref, wbr_ref, wbi_ref, wcr_ref, wci_ref, d_ref, pbr_ref, pbi_ref, tab_ref, pwr_ref, pwi_ref,
             hr_ref, hi_ref, y_ref, yg_ref, cr_ref, ci_ref, us_ref, ys_ref):
        @pl.when(pl.program_id(1) == 0)
        def _():
            cr_ref[...] = jnp.zeros_like(cr_ref)
            ci_ref[...] = jnp.zeros_like(ci_ref)

        _rows_to_segments(u_ref, us_ref)
        u = us_ref[...]
        ub = u.astype(BF16)
        hr_ref[...] = jnp.dot(ub, wbr_ref[...], preferred_element_type=F32)
        hi_ref[...] = jnp.dot(ub, wbi_ref[...], preferred_element_type=F32)
        _seg_scan_tile(hr_ref, hi_ref, pbr_ref, pbi_ref, tab_ref, pwr_ref, pwi_ref, cr_ref, ci_ref, reverse=False)
        y = (jnp.dot(hr_ref[...].astype(BF16), wcr_ref[...], preferred_element_type=F32)
             + jnp.dot(hi_ref[...].astype(BF16), wci_ref[...], preferred_element_type=F32) + d_ref[...] * u)
        ys_ref[...] = y
        _segments_to_rows(ys_ref, y_ref)
        ys_ref[...] = _gelu(y)
        _segments_to_rows(ys_ref, yg_ref)

    wb_spec = pl.BlockSpec((None, ku, kp), lambda j, i: (j, 0, 0))
    wc_spec = pl.BlockSpec((None, kp, ku), lambda j, i: (j, 0, 0))
    small = pl.BlockSpec((SUBLANES, kp), lambda j, i: (0, j))
    slabs = pl.BlockSpec((tm, kp), lambda j, i: (0, j))
    state = pl.BlockSpec((tm, kp), lambda j, i: (i, j))
    chan = pl.BlockSpec((tm, ku), lambda j, i: (i, j))
    return pl.pallas_call(
        body, name=name, grid=(J, nt),
        in_specs=[pl.BlockSpec((tm, ku), lambda j, i: (i, u_blk + j)), wb_spec, wb_spec, wc_spec, wc_spec,
                  pl.BlockSpec((1, ku), lambda j, i: (0, j)), slabs, slabs, small, small, small],
        out_specs=[state, state, chan, chan],
        out_shape=[jax.ShapeDtypeStruct((T, J * kp), F32)] * 2
        + [jax.ShapeDtypeStruct((T, J * ku), F32), jax.ShapeDtypeStruct((T, J * ku), BF16)],
        scratch_shapes=[pltpu.VMEM((SUBLANES, kp), F32), pltpu.VMEM((SUBLANES, kp), F32),
                        pltpu.VMEM((tm, ku), F32), pltpu.VMEM((tm, ku), F32)],
        compiler_params=_params(("parallel", "arbitrary"), 14 * tm * kp * 4),
    )(z, wb_re, wb_im, wc_re, wc_im_neg, d_row, pb_re, pb_im, tab, pw_re, pw_im)


def _s5_bwd(dy, z, u_off, h_re, h_im, wb_re, wb_im, wc_re, wc_im_neg, d_row, powers, dz, *, T, tm=S5_TILE, name):
    J, ku, kp = wb_re.shape
    nt, hb = T // tm, tm // SUBLANES
    pb_re, pb_im, top_re, top_im = powers
    tab, pw_re, pw_im = _cscan_tables(top_re, top_im, True)
    u_blk = u_off // ku
    contract_rows = (((0,), (0,)), ((), ()))
    contract_cols = (((1,), (1,)), ((), ()))

    def body(dy_ref, u_ref, hr_ref, hrp_ref, hi_ref, hip_ref, wbr_ref, wbi_ref, wcr_ref, wci_ref, d_ref,
             pbr_ref, pbi_ref, tab_ref, pwr_ref, pwi_ref, dz_in_ref,
             du_ref, dlr_ref, dli_ref, dd_ref, dwbr_ref, dwbi_ref, dwcr_ref, dwci_ref,
             gr_ref, gi_ref, cr_ref, ci_ref, dys_ref, us_ref):
        step = pl.program_id(1)
        first = step == 0

        @pl.when(first)
        def _():
            cr_ref[...] = jnp.zeros_like(cr_ref)
            ci_ref[...] = jnp.zeros_like(ci_ref)

        _rows_to_segments(dy_ref, dys_ref)
        _rows_to_segments(u_ref, us_ref)
        dy_t, u = dys_ref[...], us_ref[...]
        dyb, ub = dy_t.astype(BF16), u.astype(BF16)
        gr_ref[...] = lax.dot_general(dyb, wcr_ref[...], contract_cols, preferred_element_type=F32)
        gi_ref[...] = lax.dot_general(dyb, wci_ref[...], contract_cols, preferred_element_type=F32)
        hr_last = jnp.where(step == nt - 1, 0.0, hrp_ref[...])
        hi_last = jnp.where(step == nt - 1, 0.0, hip_ref[...])
        s_re, s_im = _seg_scan_tile(gr_ref, gi_ref, pbr_ref, pbi_ref, tab_ref, pwr_ref, pwi_ref, cr_ref, ci_ref,
                                    reverse=True, h=(hr_ref, hi_ref, hr_last, hi_last))
        _accumulate(dlr_ref, jnp.sum(s_re, axis=0, keepdims=True), first)
        _accumulate(dli_ref, jnp.sum(s_im, axis=0, keepdims=True), first)
        grb, gib = gr_ref[...].astype(BF16), gi_ref[...].astype(BF16)
        du = (lax.dot_general(grb, wbr_ref[...], contract_cols, preferred_element_type=F32)
              + lax.dot_general(gib, wbi_ref[...], contract_cols, preferred_element_type=F32) + dy_t * d_ref[...])
        dys_ref[...] = du
        _segments_to_rows(dys_ref, du_ref)
        _accumulate(dd_ref, jnp.sum(dy_t * u, axis=0, keepdims=True), first)
        _accumulate(dwbr_ref, lax.dot_general(ub, grb, contract_rows, preferred_element_type=F32), first)
        _accumulate(dwbi_ref, lax.dot_general(ub, gib, contract_rows, preferred_element_type=F32), first)
        _accumulate(dwcr_ref, lax.dot_general(dyb, hr_ref[...].astype(BF16), contract_rows,
                                              preferred_element_type=F32), first)
        _accumulate(dwci_ref, lax.dot_general(dyb, hi_ref[...].astype(BF16), contract_rows,
                                              preferred_element_type=F32), first)

    def tix(i):
        return nt - 1 - i

    wb_spec = pl.BlockSpec((None, ku, kp), lambda j, i: (j, 0, 0))
    wc_spec = pl.BlockSpec((None, kp, ku), lambda j, i: (j, 0, 0))
    small = pl.BlockSpec((SUBLANES, kp), lambda j, i: (0, j))
    state = pl.BlockSpec((tm, kp), lambda j, i: (tix(i), j))
    halo = pl.BlockSpec((SUBLANES, kp), lambda j, i: (jnp.maximum(tix(i) * hb - 1, 0), j))
    chan = pl.BlockSpec((tm, ku), lambda j, i: (tix(i), j))
    svec = pl.BlockSpec((1, kp), lambda j, i: (0, j))
    cvec = pl.BlockSpec((1, ku), lambda j, i: (0, j))
    slabs = pl.BlockSpec((tm, kp), lambda j, i: (0, j))
    return pl.pallas_call(
        body, name=name, grid=(J, nt),
        in_specs=[chan, pl.BlockSpec((tm, ku), lambda j, i: (tix(i), u_blk + j)), state, halo, state, halo,
                  wb_spec, wb_spec, wc_spec, wc_spec, cvec, slabs, slabs, small, small, small, ANY],
        out_specs=[pl.BlockSpec((tm, ku), lambda j, i: (tix(i), u_blk + j)), svec, svec, cvec,
                   wb_spec, wb_spec, wb_spec, wb_spec],
        input_output_aliases={16: 0},
        out_shape=[jax.ShapeDtypeStruct(dz.shape, dz.dtype), jax.ShapeDtypeStruct((1, J * kp), F32),
                   jax.ShapeDtypeStruct((1, J * kp), F32), jax.ShapeDtypeStruct((1, J * ku), F32),
                   jax.ShapeDtypeStruct((J, ku, kp), F32), jax.ShapeDtypeStruct((J, ku, kp), F32),
                   jax.ShapeDtypeStruct((J, ku, kp), F32), jax.ShapeDtypeStruct((J, ku, kp), F32)],
        scratch_shapes=[pltpu.VMEM((tm, kp), F32), pltpu.VMEM((tm, kp), F32),
                        pltpu.VMEM((SUBLANES, kp), F32), pltpu.VMEM((SUBLANES, kp), F32),
                        pltpu.VMEM((tm, ku), F32), pltpu.VMEM((tm, ku), F32)],
        compiler_params=_params(("parallel", "arbitrary"), 16 * tm * kp * 4),
    )(dy, z, h_re, h_re, h_im, h_im, wb_re, wb_im, wc_re, wc_im_neg, d_row, pb_re, pb_im, tab, pw_re, pw_im, dz)


def _mesh_pos():
    return lax.axis_index("x"), lax.axis_index("y"), lax.axis_index("c")


def _dev_index(px, py, pc):
    return 4 * px + 2 * py + pc


HBM = pl.BlockSpec(memory_space=pltpu.HBM)
SEM = pl.BlockSpec(memory_space=pltpu.SEMAPHORE)
EFFECT = pltpu.SideEffectType.DATAFLOW_SIDE_EFFECTING
RELATIONS = [(dx, dy, dc) for dx in (0, 1) for dy in (0, 1) for dc in (0, 1) if (dx, dy, dc) != (0, 0, 0)]


def _peer(rel):
    x, y, c = _mesh_pos()
    dx, dy, dc = rel
    return (x + dx - 2 * x * dx, y + dy - 2 * y * dy, c + dc - 2 * c * dc)


CHIP_RELATIONS = [(1, 0, 0), (0, 1, 0), (1, 1, 0)]
EXCHANGE_PEERS = {"gather": RELATIONS, "scatter": RELATIONS, "own": [(0, 0, 1)] + CHIP_RELATIONS, "pass": CHIP_RELATIONS}


def _split_copy(src_ref, land_ref, send_sems, recv_sems, k, mode, incoming):
    x, y, c = _mesh_pos()
    me = _dev_index(x, y, c)
    peer = _peer(EXCHANGE_PEERS[mode][k])
    if mode == "pass":
        held, theirs = _dev_index(peer[0], peer[1], c), _dev_index(peer[0], peer[1], 1 - c)
        src, slot, target = land_ref.at[held], theirs if incoming else held, (x, y, 1 - c)
    else:
        src = src_ref.at[_dev_index(*peer)] if mode == "scatter" else src_ref
        slot, target = _dev_index(*peer) if incoming else me, peer
    return pltpu.make_async_remote_copy(src_ref=src, dst_ref=land_ref.at[slot], send_sem=send_sems.at[k],
                                        recv_sem=recv_sems.at[k], device_id=target, device_id_type=MESH)


def _exchange_start(srcs, lands, *, mode, after=None, name):
    n = len(srcs)
    n_after = 0 if after is None else 1
    n_rel = len(EXCHANGE_PEERS[mode])

    def body(*refs):
        src_refs, land_refs = refs[:n], refs[n:2 * n]
        first_out = 2 * n + n_after
        send, recv = refs[first_out:first_out + n], refs[first_out + n:first_out + 2 * n]
        token = refs[-1]
        for k in range(n_rel):
            for a in range(n):
                _split_copy(src_refs[a], land_refs[a], send[a], recv[a], k, mode, incoming=False).start()
        token[...] = jnp.zeros_like(token)

    outs = pl.pallas_call(
        body, name=name, in_specs=[HBM] * (2 * n) + [ANY] * n_after,
        out_shape=[pltpu.SemaphoreType.DMA((n_rel,))] * (2 * n)
        + [pltpu.HBM(s.shape, s.dtype) for s in srcs] + [pltpu.HBM(s.shape, s.dtype) for s in lands]
        + [jax.ShapeDtypeStruct((SUBLANES, LANES), F32)],
        out_specs=[SEM] * (2 * n) + [HBM] * (2 * n) + [pl.BlockSpec(memory_space=pltpu.VMEM)],
        input_output_aliases={**{a: 2 * n + a for a in range(n)}, **{n + a: 3 * n + a for a in range(n)}},
        compiler_params=pltpu.CompilerParams(has_side_effects=EFFECT),
    )(*[pltpu.with_memory_space_constraint(s, pltpu.HBM) for s in srcs],
      *[pltpu.with_memory_space_constraint(s, pltpu.HBM) for s in lands], *([after] if n_after else []))
    per_array = [(outs[a], outs[n + a], outs[2 * n + a], outs[3 * n + a]) for a in range(n)]
    return per_array, outs[-1]


def _exchange_wait(handle, after, *, mode, name):
    send_sems, recv_sems, src_thru, land_thru = handle
    after = after if isinstance(after, (tuple, list)) else (after,)

    def body(src_ref, land_ref, send, recv, *rest):
        for k in range(len(EXCHANGE_PEERS[mode])):
            cp = _split_copy(src_ref, land_ref, send, recv, k, mode, incoming=True)
            cp.wait_send()
            cp.wait_recv()

    return pl.pallas_call(
        body, name=name, in_specs=[HBM, HBM, SEM, SEM] + [ANY] * len(after),
        out_shape=[pltpu.HBM(src_thru.shape, src_thru.dtype), pltpu.HBM(land_thru.shape, land_thru.dtype)],
        out_specs=[HBM, HBM], input_output_aliases={0: 0, 1: 1},
        compiler_params=pltpu.CompilerParams(has_side_effects=EFFECT),
    )(src_thru, land_thru, send_sems, recv_sems, *after)[1]


def _landing_zone(own_block):
    me = _dev_index(*_mesh_pos())
    zone = lax.empty((N_DEV,) + own_block.shape, own_block.dtype)
    return lax.dynamic_update_index_in_dim(zone, own_block, me, 0)


def _row_tile(rows, want):
    t = min(want, rows) // SUBLANES * SUBLANES
    while rows % t:
        t -= SUBLANES
    return t


def _sum_slots(recv, *, tr, name):
    s_, r_, c_ = recv.shape
    tr = _row_tile(r_, tr)

    def body(g_ref, o_ref):
        acc = g_ref[0]
        for s in range(1, s_):
            acc = acc + g_ref[s]
        o_ref[...] = acc

    return pl.pallas_call(
        body, name=name, grid=(r_ // tr,),
        in_specs=[pl.BlockSpec((s_, tr, c_), lambda i: (0, i, 0))],
        out_specs=pl.BlockSpec((tr, c_), lambda i: (i, 0)),
        out_shape=jax.ShapeDtypeStruct((r_, c_), F32),
        compiler_params=_params(("parallel",), (2 * s_ + 3) * tr * c_ * 4),
    )(recv)


def _adamw(recv, w, m, v, *, tr, name):
    s_, r_, c_ = recv.shape
    tr = _row_tile(r_, tr)
    assert w.shape == (r_, c_), (name, w.shape, recv.shape)
    c1 = 1.0 - ADAM_B1 ** ADAM_STEP
    c2 = 1.0 - ADAM_B2 ** ADAM_STEP

    def body(g_ref, w_ref, m_ref, v_ref, go_ref, d_ref, mo_ref, vo_ref):
        g = g_ref[0].astype(F32)
        for s in range(1, s_):
            g = g + g_ref[s].astype(F32)
        mn = ADAM_B1 * m_ref[...] + (1.0 - ADAM_B1) * g
        vn = ADAM_B2 * v_ref[...] + (1.0 - ADAM_B2) * (g * g)
        go_ref[...] = g
        mo_ref[...] = mn
        vo_ref[...] = vn
        d_ref[...] = -ADAM_LR * ((mn / c1) / (jnp.sqrt(vn / c2) + ADAM_EPS) + ADAM_WD * w_ref[...])

    tile = pl.BlockSpec((tr, c_), lambda i: (i, 0))
    return pl.pallas_call(
        body, name=name, grid=(r_ // tr,),
        in_specs=[pl.BlockSpec((s_, tr, c_), lambda i: (0, i, 0)), tile, tile, tile],
        out_specs=[tile] * 4, out_shape=[jax.ShapeDtypeStruct((r_, c_), F32)] * 4,
        compiler_params=_params(("parallel",), (2 * s_ + 16) * tr * c_ * 4),
    )(recv, w, m, v)


def _adamw_whole(gs, ws, ms, vs, *, name):
    n = len(gs)
    c1 = 1.0 - ADAM_B1 ** ADAM_STEP
    c2 = 1.0 - ADAM_B2 ** ADAM_STEP

    def body(*refs):
        for i in range(n):
            g, w = refs[i][...], refs[n + i][...]
            mn = ADAM_B1 * refs[2 * n + i][...] + (1.0 - ADAM_B1) * g
            vn = ADAM_B2 * refs[3 * n + i][...] + (1.0 - ADAM_B2) * (g * g)
            refs[4 * n + 3 * i][...] = -ADAM_LR * ((mn / c1) / (jnp.sqrt(vn / c2) + ADAM_EPS) + ADAM_WD * w)
            refs[4 * n + 3 * i + 1][...] = mn
            refs[4 * n + 3 * i + 2][...] = vn

    whole = pl.BlockSpec(memory_space=pltpu.VMEM)
    lane_padded = sum(math.prod(g.shape[:-1]) * (-(-g.shape[-1] // LANES) * LANES) for g in gs)
    outs = pl.pallas_call(
        body, name=name, in_specs=[whole] * (4 * n), out_specs=[whole] * (3 * n),
        out_shape=[jax.ShapeDtypeStruct(g.shape, F32) for g in gs for _ in range(3)],
        compiler_params=pltpu.CompilerParams(vmem_limit_bytes=int(min(max(16 * lane_padded * 4, 16 * 2 ** 20), VMEM_CAP))),
    )(*gs, *ws, *ms, *vs)
    return [tuple(outs[3 * i:3 * i + 3]) for i in range(n)]


def _s5_discretise(a_re, a_im, log_dt, b_re, b_im):
    dt = jnp.exp(log_dt)[:, None]
    lr = jnp.minimum(a_re, -1e-4)
    li = a_im
    mag = jnp.exp(lr * dt)
    lbr = mag * jnp.cos(li * dt)
    lbi = mag * jnp.sin(li * dt)
    zr, zi = lbr - 1.0, lbi
    den = lr * lr + li * li
    fr = (zr * lr + zi * li) / den
    fi = (zi * lr - zr * li) / den
    bbr = fr[..., None] * b_re - fi[..., None] * b_im
    bbi = fr[..., None] * b_im + fi[..., None] * b_re
    return lbr, lbi, bbr, bbi


def _softplus_neg(lam):
    return jnp.maximum(-lam, 0.0) + jnp.log(1.0 + jnp.exp(-jnp.abs(lam)))


S5_Q = 8
RG_Q = 2


def _local_step(x, tgt, W, comm):
    T, D = x.shape
    C = D
    G, P, H = W["ssm_b_re"].shape
    S = G * H
    F = W["mlp_b_up"].shape[1]
    n_in = 2 * C + S + 2 * D
    heads, hd = W["rg_wa"].shape[0], W["rg_wa"].shape[1]
    u_off, ga_off, gb_off = 2 * C, 2 * C + S, 2 * C + S + D

    if comm.first_token is not None:
        anchored = ("rg_lambda", "ssm_a_re", "rg_wa", "rg_wx", "ssm_c_re", "ssm_c_im")
        W = {**W, **{k: W[k] + comm.first_token[0, 0] for k in anchored}}
    sp, sp_vjp = jax.vjp(_softplus_neg, W["rg_lambda"])
    (lbr, lbi, bbr, bbi), s5_vjp = jax.vjp(_s5_discretise, W["ssm_a_re"], W["ssm_a_im"], W["ssm_log_dt"],
                                           W["ssm_b_re"], W["ssm_b_im"])
    lam_re, lam_im = lbr.reshape(-1), lbi.reshape(-1)
    jr, kr = heads // RG_Q, RG_Q * hd
    w_ri = jnp.concatenate([_bd_pack(W["rg_wa"], RG_Q), _bd_pack(W["rg_wx"], RG_Q)], axis=2).astype(BF16)
    b_ri = jnp.concatenate([W["rg_ba"].reshape(jr, kr), W["rg_bx"].reshape(jr, kr)], axis=1).reshape(1, -1)
    wb_re = _bd_pack(jnp.swapaxes(bbr, 1, 2), S5_Q).astype(BF16)
    wb_im = _bd_pack(jnp.swapaxes(bbi, 1, 2), S5_Q).astype(BF16)
    wc_re = _bd_pack(jnp.swapaxes(W["ssm_c_re"], 1, 2), S5_Q).astype(BF16)
    wc_im_neg = _bd_pack(jnp.swapaxes(-W["ssm_c_im"], 1, 2), S5_Q).astype(BF16)
    d_row = W["ssm_d"].reshape(1, S)
    powers = _power_slabs(lam_re, lam_im, S5_TILE // SUBLANES)

    x_bf = x.astype(BF16) if comm.first_token is None else (x + comm.first_token[0, 0]).astype(BF16)
    w_in, conv_w = comm.first_weights((x_bf, w_ri, wb_re, wb_im, wc_re, wc_im_neg, powers[0], powers[1]))
    z = _mm(x_bf, w_in, M=T, N=n_in, K=D, tm=1024, tn=n_in // 4, tk=D, after=comm.gather_token, name="fwd_in_proj")
    started = comm.start_weights(("mlp_w_up",), z)
    xc = _conv_fwd(z, conv_w, W["conv_b"], T=T, C=C, after=started, name="fwd_conv")
    ri = _bd([(xc, 0, w_ri)], T=T, J=jr, kb=kr, nb=2 * kr, extras=[(b_ri, "vec", 0)],
             epilogue=lambda acc, b: (_sig(acc + b),), name="fwd_gates")
    h, p, a_fwd, m_fwd = _rg_scan_fwd(z, ri, xc, sp, T=T, C=C, gate_off=C, cw=kr, name="fwd_rg_scan")
    w_a_out = comm.weight("w_a_out", p)
    started = comm.start_weights(("mlp_w_down",), p)
    y_a = _mm(p, w_a_out, M=T, N=D, K=C, out_dtypes=(BF16,), tm=512, tn=D, tk=C, after=started, name="fwd_rg_out")

    h_re, h_im, y_s, yg = _s5_fwd(z, u_off, wb_re, wb_im, wc_re, wc_im_neg, d_row, powers, T=T, name="fwd_s5")
    w_glu_w, w_glu_v = comm.weight("glu_w", yg), comm.weight("glu_v", yg)
    glu_a = _mm(yg, w_glu_w, M=T, N=D, K=S, out_dtypes=(BF16,), tm=1024, tn=D, tk=S, name="fwd_glu_w")
    cwm = 1024

    def mix_fn(b, ga, gb, ya, a):
        return b, _sig(ga) * ya.astype(F32) + _sig(gb) * (a.astype(F32) * _sig(b))

    glu_b, mix = _mm(yg, w_glu_v, M=T, N=D, K=S, tm=512, tn=cwm, tk=S,
                     extras=[(z, "mn", ga_off // cwm), (z, "mn", gb_off // cwm), (y_a, "mn"), (glu_a, "mn")],
                     epilogue=mix_fn, n_out=2, out_dtypes=(BF16, BF16), name="fwd_glu_v_mix")
    w_out = comm.weight("w_out", mix)
    def out_ln1_fn(acc, xv, g, b):
        s = ALPHA * xv + acc
        xhat, _ = _ln_stats(s)
        y = xhat * g + b
        return s, y, y

    s1, x1, x1_bf = _mm(mix, w_out, M=T, N=D, K=D, tm=512, tn=D, tk=D,
                        extras=[(x, "mn"), (W["ln1_g"], "n"), (W["ln1_b"], "n")], epilogue=out_ln1_fn, n_out=3,
                        out_dtypes=(F32, F32, BF16), name="fwd_out_proj_ln1")
    w_up = comm.weight("mlp_w_up", x1_bf)

    def mlp_up_fn(acc, b):
        hp = acc + b
        rl = jnp.maximum(hp, 0.0)
        return rl * rl, hp

    hact, hpre = _mm(x1_bf, w_up, M=T, N=F, K=D, tm=1024, tn=1024, tk=D, extras=[(W["mlp_b_up"], "n")],
                     epilogue=mlp_up_fn, n_out=2, out_dtypes=(BF16, BF16), name="fwd_mlp_up")
    w_down = comm.weight("mlp_w_down", hact)
    s2 = _mm(hact, w_down, M=T, N=D, K=F, tm=1024, tn=1024, tk=2048,
             extras=[(x1, "mn"), (W["mlp_b_down"], "n")], epilogue=lambda acc, xv, b: (ALPHA * xv + acc + b,),
             name="fwd_mlp_down")

    def ln2_fn(s, t, g, b):
        xhat, rstd = _ln_stats(s)
        err = xhat * g + b - t
        dy = err * (1.0 / D)
        ds = _ln_bwd(dy, g, xhat, rstd)
        return ds, ds, 0.5 * dy * err, dy * xhat, dy, ds

    ds2, ds2_bf, loss_cols, d_ln2_g, d_ln2_b, d_b_down = _ew(
        ln2_fn, [(s2, "tile", 0), (tgt, "tile", 0), (W["ln2_g"], "vec", 0), (W["ln2_b"], "vec", 0)],
        T=T, C=D, n_out=2, n_cs=4, out_dtypes=(F32, BF16), tm=256, name="bwd_loss_ln2")
    d_w_down = _mm(hact, ds2_bf, M=F, N=D, K=T, ta=True, out_dtypes=(BF16,), tm=1024, tn=1024, tk=4096, name="bwd_w_down")
    sent = comm.send_grad("mlp_w_down", d_w_down)

    def dhpre_fn(acc, hp):
        dv = acc * (2.0 * jnp.maximum(hp.astype(F32), 0.0))
        return dv, dv

    dhpre, d_b_up = _mm(ds2_bf, w_down, M=T, N=F, K=D, tb=True, tm=1024, tn=1024, tk=D, extras=[(hpre, "mn")],
                        epilogue=dhpre_fn, n_cs=1, out_dtypes=(BF16,), after=sent, name="bwd_mlp_down")
    d_w_up = _mm(x1_bf, dhpre, M=D, N=F, K=T, ta=True, out_dtypes=(BF16,), n_split=N_DEV, tm=1024, tn=F // N_DEV, tk=4096, name="bwd_w_up")
    sent = comm.send_grad("mlp_w_up", d_w_up)
    dx1 = _mm(dhpre, w_up, M=T, N=D, K=F, tb=True, tm=1024, tn=1024, tk=2048,
              extras=[(ds2, "mn")], epilogue=lambda acc, dv: (ALPHA * dv + acc,), after=sent, name="bwd_mlp_up")

    def ln1_bwd_fn(s, dy, g):
        xhat, rstd = _ln_stats(s)
        ds = _ln_bwd(dy, g, xhat, rstd)
        return ds, ds, dy * xhat, dy

    ds1, ds1_bf, d_ln1_g, d_ln1_b = _ew(ln1_bwd_fn, [(s1, "tile", 0), (dx1, "tile", 0), (W["ln1_g"], "vec", 0)],
                                        T=T, C=D, n_out=2, n_cs=2, out_dtypes=(F32, BF16), tm=256, name="bwd_ln1")
    d_w_out = _mm(mix, ds1_bf, M=D, N=D, K=T, ta=True, out_dtypes=(BF16,), tm=1024, tn=1024, tk=4096, name="bwd_w_out")
    sent = comm.send_grad("w_out", d_w_out)
    def mix_bwd_fn(dm, ga, gb, ya, a, b):
        ya, a, b = ya.astype(F32), a.astype(F32), b.astype(F32)
        sa, sb, sv = _sig(ga), _sig(gb), _sig(b)
        yb = a * sv
        dyb = dm * sb
        return (dm * ya * (sa * (1.0 - sa)), dm * yb * (sb * (1.0 - sb)), dm * sa, dyb * sv,
                dyb * a * (sv * (1.0 - sv)))

    dz = lax.empty((T, n_in), BF16)
    dz, dg_b, dy_a, dglu_a, dglu_b = _mm(
        ds1_bf, w_out, M=T, N=D, K=D, tb=True, tm=512, tn=cwm, tk=D,
        extras=[(z, "mn", ga_off // cwm), (z, "mn", gb_off // cwm), (y_a, "mn"), (glu_a, "mn"), (glu_b, "mn")],
        epilogue=mix_bwd_fn, n_out=5, out_dtypes=(BF16,) * 5, after=sent, into=(dz, 0, ga_off // cwm),
        name="bwd_out_proj_mix")
    dz = lax.dynamic_update_slice(dz, dg_b, (0, gb_off))

    d_w_a_out = _mm(p, dy_a, M=C, N=D, K=T, ta=True, out_dtypes=(BF16,), tm=1024, tn=1024, tk=4096, name="bwd_w_a_out")
    sent = comm.send_grad("w_a_out", d_w_a_out)
    def dp_fn(dp, hv, gate):
        th = jnp.tanh(GELU_C * (gate + GELU_K * gate * gate * gate))
        gelu = 0.5 * gate * (1.0 + th)
        dgelu = 0.5 * (1.0 + th) + 0.5 * gate * (1.0 - th * th) * (GELU_C * (1.0 + 3.0 * GELU_K * gate * gate))
        return dp * gelu, dp * hv * dgelu

    dh, dz = _mm(dy_a, w_a_out, M=T, N=C, K=D, tb=True, tm=512, tn=C, tk=D, extras=[(h, "mn"), (z, "mn", 1)],
                 epilogue=dp_fn, n_out=2, out_dtypes=(F32, BF16), after=sent, into=(dz, 1, 1), name="bwd_rg_out")
    drai, dxc0, d_b_ri, d_sp = _rg_scan_bwd(dh, h, ri, xc, a_fwd, m_fwd, sp, T=T, C=C, cw=kr, name="bwd_rg_scan")
    dxc = _bd([(drai, 0, w_ri)], T=T, J=jr, kb=2 * kr, nb=kr, tw=True, extras=[(dxc0, "tile", 0)],
              epilogue=lambda acc, d0: (acc + d0,), name="bwd_gates")
    d_w_ri = _bdw(xc, 0, drai, 0, T=T, J=jr, kb=kr, nb=2 * kr, name="bwd_w_gates")
    d_wa, d_wx = _bd_unpack(d_w_ri[:, :, :kr], RG_Q), _bd_unpack(d_w_ri[:, :, kr:], RG_Q)
    d_b_ri = d_b_ri.reshape(jr, 2 * kr)
    d_ba, d_bx = d_b_ri[:, :kr].reshape(1, -1), d_b_ri[:, kr:].reshape(1, -1)
    dz, conv_sums = _conv_bwd(dxc, z, conv_w, dz, T=T, C=C, name="bwd_conv")
    d_conv_w, d_conv_b = conv_sums[0:4], conv_sums[4:5]
    (d_lambda,) = sp_vjp(d_sp)

    d_glu_w = _mm(yg, dglu_a, M=S, N=D, K=T, ta=True, out_dtypes=(BF16,), n_split=N_DEV, tm=1024, tn=D // N_DEV, tk=4096, name="bwd_w_glu_w")
    d_glu_v = _mm(yg, dglu_b, M=S, N=D, K=T, ta=True, out_dtypes=(BF16,), n_split=N_DEV, tm=1024, tn=D // N_DEV, tk=4096, name="bwd_w_glu_v")
    sent = comm.send_grad("glu_w", d_glu_w, "glu_v", d_glu_v)
    dyg0 = _mm(dglu_a, w_glu_w, M=T, N=S, K=D, tb=True, tm=512, tn=S, tk=D, after=sent, name="bwd_glu_w")
    dy_s = _mm(dglu_b, w_glu_v, M=T, N=S, K=D, tb=True, tm=512, tn=S, tk=D,
               extras=[(dyg0, "mn"), (y_s, "mn")], epilogue=lambda acc, d0, yv: ((acc + d0) * _dgelu(yv),),
               name="bwd_glu_v")
    dz, d_lbr, d_lbi, d_ssm_d, d_wb_re, d_wb_im, d_wc_re, d_wc_im_neg = _s5_bwd(
        dy_s, z, u_off, h_re, h_im, wb_re, wb_im, wc_re, wc_im_neg, d_row, powers, dz, T=T, name="bwd_s5")
    d_bbr = jnp.swapaxes(_bd_unpack(d_wb_re, S5_Q), 1, 2)
    d_bbi = jnp.swapaxes(_bd_unpack(d_wb_im, S5_Q), 1, 2)
    d_a_re, d_a_im, d_log_dt, d_b_re, d_b_im = s5_vjp((d_lbr.reshape(G, P), d_lbi.reshape(G, P), d_bbr, d_bbi))
    d_c_re = _bd_unpack(d_wc_re, S5_Q)
    d_c_im = -_bd_unpack(d_wc_im_neg, S5_Q)

    grads = dict(
        conv_w=d_conv_w, conv_b=d_conv_b, rg_wa=d_wa, rg_ba=d_ba, rg_wx=d_wx, rg_bx=d_bx,
        rg_lambda=d_lambda, ssm_a_re=d_a_re, ssm_a_im=d_a_im, ssm_log_dt=d_log_dt,
        ssm_b_re=d_b_re, ssm_b_im=d_b_im, ssm_c_re=d_c_re, ssm_c_im=d_c_im, ssm_d=d_ssm_d.reshape(G, H),
        ln1_g=d_ln1_g, ln1_b=d_ln1_b, mlp_b_up=d_b_up, mlp_b_down=d_b_down, ln2_g=d_ln2_g, ln2_b=d_ln2_b)
    sent = comm.send_small(grads)

    d_w_in = _mm(x_bf, dz, M=D, N=n_in, K=T, ta=True, out_dtypes=(BF16,), n_split=N_DEV, tm=1024, tn=n_in // N_DEV,
                 tk=4096, after=sent, name="bwd_w_in")
    sent = comm.send_grad("w_in", d_w_in)
    grad_x = _mm(dz, w_in, M=T, N=D, K=n_in, tb=True, tm=1024, tn=1024, tk=n_in // 4,
                 extras=[(ds1, "mn")], epilogue=lambda acc, dv: (ALPHA * dv + acc,), after=sent, name="bwd_in_proj")
    return jnp.sum(loss_cols), grad_x, grads


BIG = ("w_in", "w_a_out", "glu_w", "glu_v", "w_out", "mlp_w_up", "mlp_w_down")
COL_SHARDED = ("w_in", "glu_w", "glu_v", "mlp_w_up")
SMALL = ("conv_w", "conv_b", "rg_wa", "rg_ba", "rg_wx", "rg_bx", "rg_lambda", "ssm_a_re", "ssm_a_im", "ssm_log_dt",
         "ssm_b_re", "ssm_b_im", "ssm_c_re", "ssm_c_im", "ssm_d", "ln1_g", "ln1_b", "mlp_b_up", "mlp_b_down", "ln2_g",
         "ln2_b")
ORDER = ("w_in", "conv_w", "conv_b", "rg_wa", "rg_ba", "rg_wx", "rg_bx", "rg_lambda", "w_a_out", "ssm_a_re",
         "ssm_a_im", "ssm_log_dt", "ssm_b_re", "ssm_b_im", "ssm_c_re", "ssm_c_im", "ssm_d", "glu_w", "glu_v", "w_out",
         "ln1_g", "ln1_b", "mlp_w_up", "mlp_b_up", "mlp_w_down", "mlp_b_down", "ln2_g", "ln2_b")
TILE_ELEMS = SUBLANES * LANES


def _pack(arrs):
    pieces = []
    for a in arrs:
        flat = a.reshape(-1)
        flat = jnp.pad(flat, (0, (-flat.shape[0]) % TILE_ELEMS))
        pieces.append(flat.reshape(-1, LANES))
    rows = sum(p.shape[0] for p in pieces)
    pad_rows = (-rows) % (N_DEV * SUBLANES)
    if pad_rows:
        pieces.append(jnp.zeros((pad_rows, LANES), pieces[0].dtype))
    return jnp.concatenate(pieces, axis=0)


def _unpack(packed, shapes):
    out, row = [], 0
    for shp in shapes:
        n = math.prod(shp)
        rows = -(-n // TILE_ELEMS) * SUBLANES
        out.append(packed[row:row + rows].reshape(-1)[:n].reshape(shp))
        row += rows
    return out


class _Comm:
    def __init__(self, w):
        first = [w["w_in"].astype(BF16), w["conv_w"]]
        self._first, self.first_token = _exchange_start(first, [_landing_zone(s) for s in first], mode="own",
                                                        name="gather_in_start")
        self._shards = {k: w[k].astype(BF16) for k in BIG if k != "w_in"}
        self._weights, self._gathers, self._grads = {}, {}, {}

    def first_weights(self, after):
        lands = [_exchange_wait(h, after, mode="own", name="gather_in_wait_%d" % i) for i, h in enumerate(self._first)]
        unused = [lax.empty((2 * SUBLANES, LANES), BF16) for _ in lands]
        handles, passed = _exchange_start(unused, lands, mode="pass", name="gather_in_pass")
        w_in, taps = [_exchange_wait(h, passed, mode="pass", name="gather_in_got_%d" % i) for i, h in enumerate(handles)]
        self._weights["w_in"] = w_in
        self.gather_token = self.start_weights(("w_a_out", "glu_w", "glu_v", "w_out"), w_in)
        return self.weight("w_in", None), jnp.swapaxes(taps, 0, 1).reshape(taps.shape[1], -1)

    def start_weights(self, names, after):
        shards = [self._shards.pop(k) for k in names]
        handles, token = _exchange_start(shards, [_landing_zone(s) for s in shards], mode="gather", after=after,
                                         name="gather_start_" + names[0])
        self._gathers.update(zip(names, handles))
        return token

    def weight(self, k, after):
        if k not in self._weights:
            self._weights[k] = _exchange_wait(self._gathers.pop(k), after, mode="gather", name="gather_wait_" + k)
        gk = self._weights[k]
        if k in COL_SHARDED:
            return jnp.swapaxes(gk, 0, 1).reshape(gk.shape[1], -1)
        return gk.reshape(-1, gk.shape[-1])

    def send_grad(self, *names_and_parts):
        names, parts = names_and_parts[0::2], names_and_parts[1::2]
        parts = [p if k in COL_SHARDED else p.reshape(N_DEV, p.shape[0] // N_DEV, p.shape[1])
                 for k, p in zip(names, parts)]
        me = _dev_index(*_mesh_pos())
        lands = [_landing_zone(lax.dynamic_index_in_dim(p, me, 0, keepdims=False)) for p in parts]
        handles, token = _exchange_start(parts, lands, mode="scatter", name="grad_start_" + names[0])
        self._grads.update(zip(names, handles))
        return token

    def received_grad(self, k, after):
        return _exchange_wait(self._grads.pop(k), after, mode="scatter", name="grad_wait_" + k)

    def send_small(self, grads):
        return self.send_grad("small", _pack([grads[k] for k in SMALL]))

    def all_reduced_small(self, after, behind):
        recv = self.received_grad("small", after)
        block = _sum_slots(recv, tr=512, name="sum_small_grads")
        (handle,), started = _exchange_start([block], [_landing_zone(block)], mode="gather", name="small_sum_start")
        done = behind(started)
        return _exchange_wait(handle, done, mode="gather", name="small_sum_wait").reshape(-1, LANES)


SMALL_GROUPS = (("rg_wa", "rg_wx"), ("ssm_b_re",), ("ssm_b_im",),
                tuple(k for k in SMALL if k not in ("rg_wa", "rg_wx", "ssm_b_re", "ssm_b_im")))


def _step(x, tgt, w, m, v, raw_w, raw_m, raw_v):
    dev = _dev_index(*_mesh_pos())

    comm = _Comm(w)
    small = dict(w)
    for k in ("conv_b", "rg_ba", "rg_bx", "rg_lambda", "ln1_g", "ln1_b", "mlp_b_up", "mlp_b_down", "ln2_g", "ln2_b"):
        small[k] = w[k].reshape(1, -1)

    loss_part, grad_x, grads = _local_step(x, tgt, small, comm)

    out_g, out_d, out_m, out_v = {}, {}, {}, {}

    def update_large(started):
        for k in BIG:
            rk = comm.received_grad(k, (grad_x, started))
            out_g[k], out_d[k], out_m[k], out_v[k] = _adamw(rk, w[k], m[k], v[k], tr=256 if k in COL_SHARDED else 128,
                                                                 name="adamw_" + k)
        return out_v[BIG[-1]]

    small_all = comm.all_reduced_small(grad_x, update_large)
    g_small = dict(zip(SMALL, _unpack(small_all, [grads[k].shape for k in SMALL])))
    cw_cols = w["conv_w"].shape[1]
    g_small["conv_w"] = lax.dynamic_slice_in_dim(g_small["conv_w"], dev * cw_cols, cw_cols, axis=1)
    for group in SMALL_GROUPS:
        gs = [g_small[k].reshape(raw_w[k].shape) for k in group]
        res = _adamw_whole(gs, [raw_w[k] for k in group], [raw_m[k] for k in group], [raw_v[k] for k in group],
                           name="adamw_" + group[0])
        for k, gk, (dk, mk, vk) in zip(group, gs, res):
            out_g[k], out_d[k], out_m[k], out_v[k] = gk, dk, mk, vk

    loss = lax.psum(loss_part, ("x", "y", "c"))
    return loss, grad_x, out_g, out_d, out_m, out_v


def kernel(x, w_in, conv_w, conv_b, rg_wa, rg_ba, rg_wx, rg_bx, rg_lambda, w_a_out, ssm_a_re, ssm_a_im, ssm_log_dt, ssm_b_re, ssm_b_im, ssm_c_re, ssm_c_im, ssm_d, glu_w, glu_v, w_out, ln1_g, ln1_b, mlp_w_up, mlp_b_up, mlp_w_down, mlp_b_down, ln2_g, ln2_b, loss_target, m_w_in, m_conv_w, m_conv_b, m_rg_wa, m_rg_ba, m_rg_wx, m_rg_bx, m_rg_lambda, m_w_a_out, m_ssm_a_re, m_ssm_a_im, m_ssm_log_dt, m_ssm_b_re, m_ssm_b_im, m_ssm_c_re, m_ssm_c_im, m_ssm_d, m_glu_w, m_glu_v, m_w_out, m_ln1_g, m_ln1_b, m_mlp_w_up, m_mlp_b_up, m_mlp_w_down, m_mlp_b_down, m_ln2_g, m_ln2_b, v_w_in, v_conv_w, v_conv_b, v_rg_wa, v_rg_ba, v_rg_wx, v_rg_bx, v_rg_lambda, v_w_a_out, v_ssm_a_re, v_ssm_a_im, v_ssm_log_dt, v_ssm_b_re, v_ssm_b_im, v_ssm_c_re, v_ssm_c_im, v_ssm_d, v_glu_w, v_glu_v, v_w_out, v_ln1_g, v_ln1_b, v_mlp_w_up, v_mlp_b_up, v_mlp_w_down, v_mlp_b_down, v_ln2_g, v_ln2_b):
    args = locals()
    w = {k: args[k][0] for k in ORDER}
    m = {k: args["m_" + k][0] for k in BIG}
    v = {k: args["v_" + k][0] for k in BIG}
    raw = [{k: args[prefix + k] for k in SMALL} for prefix in ("", "m_", "v_")]
    loss, grad_x, out_g, out_d, out_m, out_v = _step(x[0], loss_target[0], w, m, v, *raw)
    outs = [loss, grad_x[None]]
    for group in (out_g, out_d, out_m, out_v):
        outs += [group[k].reshape(args[k].shape) for k in ORDER]
    return tuple(outs)
```
